```python
import jax, jax.numpy as jnp
from jax import lax
import numpy as np

D_MODEL = 2048
BATCH = 8
SEQ = 2048
DEPTH = 1

CONV_CH = 1024
CONV_K = 31
N_HEADS = 8
QK_NOPE = 128
QK_ROPE = 64
V_HEAD = 128
QK_HEAD = QK_NOPE + QK_ROPE
Q_LORA = 768
KV_LORA = 512
ATTN_CH = N_HEADS * V_HEAD
MIX_WIDTH = CONV_CH + ATTN_CH
IN_COLS = 2 * CONV_CH + Q_LORA + KV_LORA + QK_ROPE
ROPE_THETA = 10000.0
Q_BLOCK = 128
D_FF = ((8 * D_MODEL // 3 + 255) // 256) * 256
EPS = 1e-6

kernel_name = "hymba_conformer_mla_sandwich_layer"


def rmsnorm(x, g):
    xf = x.astype(jnp.float32)
    y = xf * lax.rsqrt(jnp.mean(xf * xf, axis=-1, keepdims=True) + EPS)
    return (y * g.astype(jnp.float32)).astype(x.dtype)


def layernorm(x, g, b):
    xf = x.astype(jnp.float32)
    mu = jnp.mean(xf, axis=-1, keepdims=True)
    var = jnp.mean(jnp.square(xf - mu), axis=-1, keepdims=True)
    y = (xf - mu) * lax.rsqrt(var + EPS)
    return (y * g.astype(jnp.float32) + b.astype(jnp.float32)).astype(x.dtype)


def rope_tables(positions, dtype):
    inv_freq = ROPE_THETA ** (-jnp.arange(0, QK_ROPE, 2, dtype=jnp.float32) / QK_ROPE)
    ang = positions.astype(jnp.float32)[..., None] * inv_freq
    return jnp.cos(ang).astype(dtype), jnp.sin(ang).astype(dtype)


def apply_rope(x, cos, sin):
    x1, x2 = jnp.split(x, 2, axis=-1)
    return jnp.concatenate([x1 * cos - x2 * sin, x2 * cos + x1 * sin], axis=-1)


def causal_depthwise_conv(u, w, b):
    y = lax.conv_general_dilated(
        u, w[:, None, :], window_strides=(1,), padding=[(CONV_K - 1, 0)],
        dimension_numbers=("NWC", "WIO", "NWC"), feature_group_count=u.shape[-1])
    return y + b


def causal_attention(q, k, v):
    B, S, H, Dq = q.shape
    nblk = S // Q_BLOCK
    qb = q.reshape(B, nblk, Q_BLOCK, H, Dq).transpose(1, 0, 2, 3, 4)
    kpos = jnp.arange(S)
    scale = Dq ** -0.5
    neg = jnp.finfo(jnp.float32).min

    def one_block(args):
        i, qi = args
        s = jnp.einsum('bqhd,bkhd->bhqk', qi, k).astype(jnp.float32) * scale
        qpos = i * Q_BLOCK + jnp.arange(Q_BLOCK)
        s = jnp.where(kpos[None, :] <= qpos[:, None], s, neg)
        p = jax.nn.softmax(s, axis=-1).astype(v.dtype)
        return jnp.einsum('bhqk,bkhd->bqhd', p, v)

    out = lax.map(one_block, (jnp.arange(nblk), qb))
    return out.transpose(1, 0, 2, 3, 4).reshape(B, S, H * v.shape[-1])


def _fwd_setup_inputs(seed: int = 0) -> dict:
    key = jax.random.key(seed)
    ks = jax.random.split(key, 24)
    f = jnp.float32
    L = DEPTH

    def w(k, shape, fan_in):
        return jax.random.normal(k, shape, f) * (fan_in ** -0.5)

    def gain(k, n):
        return jnp.ones((L, n), f) + 0.05 * jax.random.normal(k, (L, n), f)

    x = jax.random.normal(ks[0], (BATCH, SEQ, D_MODEL), f)
    offset = jax.random.randint(ks[1], (BATCH, 1), 0, 1024, dtype=jnp.int32)
    positions = offset + jnp.arange(SEQ, dtype=jnp.int32)[None, :]
    return {
        "x": x,
        "positions": positions,
        "pre_mix_norm": gain(ks[2], D_MODEL),
        "w_in": w(ks[3], (L, D_MODEL, IN_COLS), D_MODEL),
        "q_norm": gain(ks[4], Q_LORA),
        "w_uq": w(ks[5], (L, Q_LORA, N_HEADS * QK_HEAD), Q_LORA),
        "kv_norm": gain(ks[6], KV_LORA),
        "w_ukv": w(ks[7], (L, KV_LORA, N_HEADS * (QK_NOPE + V_HEAD)), KV_LORA),
        "conv_w": w(ks[8], (L, CONV_K, CONV_CH), CONV_K),
        "conv_b": 0.02 * jax.random.normal(ks[9], (L, CONV_CH), f),
        "conv_ln_g": gain(ks[10], CONV_CH),
        "conv_ln_b": 0.02 * jax.random.normal(ks[11], (L, CONV_CH), f),
        "conv_out_norm": gain(ks[12], CONV_CH),
        "attn_out_norm": gain(ks[13], ATTN_CH),
        "w_out": w(ks[14], (L, MIX_WIDTH, D_MODEL), MIX_WIDTH),
        "post_mix_norm": gain(ks[15], D_MODEL),
        "pre_ffn_norm": gain(ks[16], D_MODEL),
        "w_gate": w(ks[17], (L, D_MODEL, D_FF), D_MODEL),
        "w_up": w(ks[18], (L, D_MODEL, D_FF), D_MODEL),
        "w_down": w(ks[19], (L, D_FF, D_MODEL), D_FF),
        "post_ffn_norm": gain(ks[20], D_MODEL),
    }


def _fwd_reference(x, positions, pre_mix_norm, w_in, q_norm, w_uq, kv_norm, w_ukv,
              conv_w, conv_b, conv_ln_g, conv_ln_b, conv_out_norm, attn_out_norm,
              w_out, post_mix_norm, pre_ffn_norm, w_gate, w_up, w_down, post_ffn_norm):
    B, S, _ = x.shape
    cos, sin = rope_tables(positions, x.dtype)
    c1 = 2 * CONV_CH
    c2 = c1 + Q_LORA
    c3 = c2 + KV_LORA
    for l in range(DEPTH):
        h = rmsnorm(x, pre_mix_norm[l])
        z = h @ w_in[l]
        conv_in, q_lat, kv_lat, k_rope = z[..., :c1], z[..., c1:c2], z[..., c2:c3], z[..., c3:]

        a, g = jnp.split(conv_in, 2, axis=-1)
        u = a * jax.nn.sigmoid(g)
        u = causal_depthwise_conv(u, conv_w[l], conv_b[l])
        u = jax.nn.silu(layernorm(u, conv_ln_g[l], conv_ln_b[l]))

        q = (rmsnorm(q_lat, q_norm[l]) @ w_uq[l]).reshape(B, S, N_HEADS, QK_HEAD)
        q_nope, q_pe = q[..., :QK_NOPE], q[..., QK_NOPE:]
        q_pe = apply_rope(q_pe, cos[:, :, None, :], sin[:, :, None, :])
        kv = (rmsnorm(kv_lat, kv_norm[l]) @ w_ukv[l]).reshape(B, S, N_HEADS, QK_NOPE + V_HEAD)
        k_nope, v = kv[..., :QK_NOPE], kv[..., QK_NOPE:]
        k_pe = apply_rope(k_rope, cos, sin)
        k_pe = jnp.broadcast_to(k_pe[:, :, None, :], (B, S, N_HEADS, QK_ROPE))
        q_full = jnp.concatenate([q_nope, q_pe], axis=-1)
        k_full = jnp.concatenate([k_nope, k_pe], axis=-1)
        attn = causal_attention(q_full, k_full, v)

        mix = jnp.concatenate([rmsnorm(u, conv_out_norm[l]),
                               rmsnorm(attn, attn_out_norm[l])], axis=-1) @ w_out[l]
        x = x + rmsnorm(mix, post_mix_norm[l])

        hf = rmsnorm(x, pre_ffn_norm[l])
        ff = (jax.nn.silu(hf @ w_gate[l]) * (hf @ w_up[l])) @ w_down[l]
        x = x + rmsnorm(ff, post_ffn_norm[l])
    return x


import jax as _jax
import jax.numpy as _jnp

TWIN_FORMAT = 'train_step'
FWD_PARAMS = ['x', 'positions', 'pre_mix_norm', 'w_in', 'q_norm', 'w_uq', 'kv_norm', 'w_ukv', 'conv_w', 'conv_b', 'conv_ln_g', 'conv_ln_b', 'conv_out_norm', 'attn_out_norm', 'w_out', 'post_mix_norm', 'pre_ffn_norm', 'w_gate', 'w_up', 'w_down', 'post_ffn_norm']
TWIN_WEIGHTS = ['pre_mix_norm', 'w_in', 'q_norm', 'w_uq', 'kv_norm', 'w_ukv', 'conv_w', 'conv_b', 'conv_ln_g', 'conv_ln_b', 'conv_out_norm', 'attn_out_norm', 'w_out', 'post_mix_norm', 'pre_ffn_norm', 'w_gate', 'w_up', 'w_down', 'post_ffn_norm']
TWIN_DIFF_INPUT = 'x'
TWIN_INPUTS = ['x', 'positions', 'pre_mix_norm', 'w_in', 'q_norm', 'w_uq', 'kv_norm', 'w_ukv', 'conv_w', 'conv_b', 'conv_ln_g', 'conv_ln_b', 'conv_out_norm', 'attn_out_norm', 'w_out', 'post_mix_norm', 'pre_ffn_norm', 'w_gate', 'w_up', 'w_down', 'post_ffn_norm', 'loss_target', 'm_pre_mix_norm', 'm_w_in', 'm_q_norm', 'm_w_uq', 'm_kv_norm', 'm_w_ukv', 'm_conv_w', 'm_conv_b', 'm_conv_ln_g', 'm_conv_ln_b', 'm_conv_out_norm', 'm_attn_out_norm', 'm_w_out', 'm_post_mix_norm', 'm_pre_ffn_norm', 'm_w_gate', 'm_w_up', 'm_w_down', 'm_post_ffn_norm', 'v_pre_mix_norm', 'v_w_in', 'v_q_norm', 'v_w_uq', 'v_kv_norm', 'v_w_ukv', 'v_conv_w', 'v_conv_b', 'v_conv_ln_g', 'v_conv_ln_b', 'v_conv_out_norm', 'v_attn_out_norm', 'v_w_out', 'v_post_mix_norm', 'v_pre_ffn_norm', 'v_w_gate', 'v_w_up', 'v_w_down', 'v_post_ffn_norm']
TWIN_OUTPUTS = ['loss', 'grad_x', 'grad_pre_mix_norm', 'grad_w_in', 'grad_q_norm', 'grad_w_uq', 'grad_kv_norm', 'grad_w_ukv', 'grad_conv_w', 'grad_conv_b', 'grad_conv_ln_g', 'grad_conv_ln_b', 'grad_conv_out_norm', 'grad_attn_out_norm', 'grad_w_out', 'grad_post_mix_norm', 'grad_pre_ffn_norm', 'grad_w_gate', 'grad_w_up', 'grad_w_down', 'grad_post_ffn_norm', 'delta_pre_mix_norm', 'delta_w_in', 'delta_q_norm', 'delta_w_uq', 'delta_kv_norm', 'delta_w_ukv', 'delta_conv_w', 'delta_conv_b', 'delta_conv_ln_g', 'delta_conv_ln_b', 'delta_conv_out_norm', 'delta_attn_out_norm', 'delta_w_out', 'delta_post_mix_norm', 'delta_pre_ffn_norm', 'delta_w_gate', 'delta_w_up', 'delta_w_down', 'delta_post_ffn_norm', 'new_m_pre_mix_norm', 'new_m_w_in', 'new_m_q_norm', 'new_m_w_uq', 'new_m_kv_norm', 'new_m_w_ukv', 'new_m_conv_w', 'new_m_conv_b', 'new_m_conv_ln_g', 'new_m_conv_ln_b', 'new_m_conv_out_norm', 'new_m_attn_out_norm', 'new_m_w_out', 'new_m_post_mix_norm', 'new_m_pre_ffn_norm', 'new_m_w_gate', 'new_m_w_up', 'new_m_w_down', 'new_m_post_ffn_norm', 'new_v_pre_mix_norm', 'new_v_w_in', 'new_v_q_norm', 'new_v_w_uq', 'new_v_kv_norm', 'new_v_w_ukv', 'new_v_conv_w', 'new_v_conv_b', 'new_v_conv_ln_g', 'new_v_conv_ln_b', 'new_v_conv_out_norm', 'new_v_attn_out_norm', 'new_v_w_out', 'new_v_post_mix_norm', 'new_v_pre_ffn_norm', 'new_v_w_gate', 'new_v_w_up', 'new_v_w_down', 'new_v_post_ffn_norm']
TWIN_LEAF_KINDS = {'loss': 'loss', 'grad_x': 'grad_x', 'grad_pre_mix_norm': 'grad_w', 'grad_w_in': 'grad_w', 'grad_q_norm': 'grad_w', 'grad_w_uq': 'grad_w', 'grad_kv_norm': 'grad_w', 'grad_w_ukv': 'grad_w', 'grad_conv_w': 'grad_w', 'grad_conv_b': 'grad_w', 'grad_conv_ln_g': 'grad_w', 'grad_conv_ln_b': 'grad_w', 'grad_conv_out_norm': 'grad_w', 'grad_attn_out_norm': 'grad_w', 'grad_w_out': 'grad_w', 'grad_post_mix_norm': 'grad_w', 'grad_pre_ffn_norm': 'grad_w', 'grad_w_gate': 'grad_w', 'grad_w_up': 'grad_w', 'grad_w_down': 'grad_w', 'grad_post_ffn_norm': 'grad_w', 'delta_pre_mix_norm': 'delta_w', 'delta_w_in': 'delta_w', 'delta_q_norm': 'delta_w', 'delta_w_uq': 'delta_w', 'delta_kv_norm': 'delta_w', 'delta_w_ukv': 'delta_w', 'delta_conv_w': 'delta_w', 'delta_conv_b': 'delta_w', 'delta_conv_ln_g': 'delta_w', 'delta_conv_ln_b': 'delta_w', 'delta_conv_out_norm': 'delta_w', 'delta_attn_out_norm': 'delta_w', 'delta_w_out': 'delta_w', 'delta_post_mix_norm': 'delta_w', 'delta_pre_ffn_norm': 'delta_w', 'delta_w_gate': 'delta_w', 'delta_w_up': 'delta_w', 'delta_w_down': 'delta_w', 'delta_post_ffn_norm': 'delta_w', 'new_m_pre_mix_norm': 'new_m', 'new_m_w_in': 'new_m', 'new_m_q_norm': 'new_m', 'new_m_w_uq': 'new_m', 'new_m_kv_norm': 'new_m', 'new_m_w_ukv': 'new_m', 'new_m_conv_w': 'new_m', 'new_m_conv_b': 'new_m', 'new_m_conv_ln_g': 'new_m', 'new_m_conv_ln_b': 'new_m', 'new_m_conv_out_norm': 'new_m', 'new_m_attn_out_norm': 'new_m', 'new_m_w_out': 'new_m', 'new_m_post_mix_norm': 'new_m', 'new_m_pre_ffn_norm': 'new_m', 'new_m_w_gate': 'new_m', 'new_m_w_up': 'new_m', 'new_m_w_down': 'new_m', 'new_m_post_ffn_norm': 'new_m', 'new_v_pre_mix_norm': 'new_v', 'new_v_w_in': 'new_v', 'new_v_q_norm': 'new_v', 'new_v_w_uq': 'new_v', 'new_v_kv_norm': 'new_v', 'new_v_w_ukv': 'new_v', 'new_v_conv_w': 'new_v', 'new_v_conv_b': 'new_v', 'new_v_conv_ln_g': 'new_v', 'new_v_conv_ln_b': 'new_v', 'new_v_conv_out_norm': 'new_v', 'new_v_attn_out_norm': 'new_v', 'new_v_w_out': 'new_v', 'new_v_post_mix_norm': 'new_v', 'new_v_pre_ffn_norm': 'new_v', 'new_v_w_gate': 'new_v', 'new_v_w_up': 'new_v', 'new_v_w_down': 'new_v', 'new_v_post_ffn_norm': 'new_v'}


def _forward(args):
    return _fwd_reference(*[args[k] for k in FWD_PARAMS])


def _output_shape():
    out = _jax.eval_shape(lambda: _forward(_fwd_setup_inputs(0)))
    return out.shape, out.dtype

N_MICROBATCH = 1
ADAM_LR = 0.001
ADAM_B1 = 0.9
ADAM_B2 = 0.999
ADAM_EPS = 1e-08
ADAM_WD = 0.01
ADAM_STEP = 10
PER_EXAMPLE_BATCH_AXIS = {'x': 0, 'positions': 0, 'loss_target': 0}
SHARED_INPUTS = []
_WEIGHT_DTYPES = {'pre_mix_norm': _jnp.float32, 'w_in': _jnp.float32, 'q_norm': _jnp.float32, 'w_uq': _jnp.float32, 'kv_norm': _jnp.float32, 'w_ukv': _jnp.float32, 'conv_w': _jnp.float32, 'conv_b': _jnp.float32, 'conv_ln_g': _jnp.float32, 'conv_ln_b': _jnp.float32, 'conv_out_norm': _jnp.float32, 'attn_out_norm': _jnp.float32, 'w_out': _jnp.float32, 'post_mix_norm': _jnp.float32, 'pre_ffn_norm': _jnp.float32, 'w_gate': _jnp.float32, 'w_up': _jnp.float32, 'w_down': _jnp.float32, 'post_ffn_norm': _jnp.float32}
MOMENT_SCALE = {'pre_mix_norm': 2.814750e-01, 'w_in': 2.114406e-01, 'q_norm': 2.148032e-01, 'w_uq': 1.447088e-01, 'kv_norm': 5.464205e-01, 'w_ukv': 2.263217e-01, 'conv_w': 1.773381e-01, 'conv_b': 1.228283e+00, 'conv_ln_g': 4.750349e-01, 'conv_ln_b': 6.935987e-01, 'conv_out_norm': 2.898948e-01, 'attn_out_norm': 3.055614e-01, 'w_out': 2.973472e-01, 'post_mix_norm': 8.040874e+00, 'pre_ffn_norm': 2.759355e-01, 'w_gate': 8.937290e-02, 'w_up': 1.259652e-01, 'w_down': 2.087112e-01, 'post_ffn_norm': 8.049944e+00}


def _to_microbatches(a, axis):
    t = _jnp.moveaxis(a, axis, 0)
    t = t.reshape((N_MICROBATCH, t.shape[0] // N_MICROBATCH) + t.shape[1:])
    return _jnp.moveaxis(t, 1, axis + 1)


def setup_inputs(seed: int = 0) -> dict:
    inp = _fwd_setup_inputs(seed)
    key = _jax.random.fold_in(_jax.random.key(seed), 7919)
    shape, _ = _output_shape()
    out = dict(inp)
    out["loss_target"] = _jax.random.normal(_jax.random.fold_in(key, 0), shape, _jnp.float32)
    for i, name in enumerate(TWIN_WEIGHTS):
        w = inp[name].astype(_jnp.float32)
        if MOMENT_SCALE is None:
            s = _jnp.sqrt(_jnp.mean(_jnp.square(w)) + 1e-30)
        else:
            s = MOMENT_SCALE[name]
        km, kv = _jax.random.split(_jax.random.fold_in(key, i + 1))
        out[name] = w
        out["m_" + name] = s * _jax.random.normal(km, w.shape, _jnp.float32)
        out["v_" + name] = (s * s) * _jax.random.uniform(kv, w.shape, _jnp.float32, 0.5, 1.5)
    if N_MICROBATCH > 1:
        for name, axis in PER_EXAMPLE_BATCH_AXIS.items():
            out[name] = _to_microbatches(out[name], axis)
    return {'x': out['x'], 'positions': out['positions'], 'pre_mix_norm': out['pre_mix_norm'], 'w_in': out['w_in'], 'q_norm': out['q_norm'], 'w_uq': out['w_uq'], 'kv_norm': out['kv_norm'], 'w_ukv': out['w_ukv'], 'conv_w': out['conv_w'], 'conv_b': out['conv_b'], 'conv_ln_g': out['conv_ln_g'], 'conv_ln_b': out['conv_ln_b'], 'conv_out_norm': out['conv_out_norm'], 'attn_out_norm': out['attn_out_norm'], 'w_out': out['w_out'], 'post_mix_norm': out['post_mix_norm'], 'pre_ffn_norm': out['pre_ffn_norm'], 'w_gate': out['w_gate'], 'w_up': out['w_up'], 'w_down': out['w_down'], 'post_ffn_norm': out['post_ffn_norm'], 'loss_target': out['loss_target'], 'm_pre_mix_norm': out['m_pre_mix_norm'], 'm_w_in': out['m_w_in'], 'm_q_norm': out['m_q_norm'], 'm_w_uq': out['m_w_uq'], 'm_kv_norm': out['m_kv_norm'], 'm_w_ukv': out['m_w_ukv'], 'm_conv_w': out['m_conv_w'], 'm_conv_b': out['m_conv_b'], 'm_conv_ln_g': out['m_conv_ln_g'], 'm_conv_ln_b': out['m_conv_ln_b'], 'm_conv_out_norm': out['m_conv_out_norm'], 'm_attn_out_norm': out['m_attn_out_norm'], 'm_w_out': out['m_w_out'], 'm_post_mix_norm': out['m_post_mix_norm'], 'm_pre_ffn_norm': out['m_pre_ffn_norm'], 'm_w_gate': out['m_w_gate'], 'm_w_up': out['m_w_up'], 'm_w_down': out['m_w_down'], 'm_post_ffn_norm': out['m_post_ffn_norm'], 'v_pre_mix_norm': out['v_pre_mix_norm'], 'v_w_in': out['v_w_in'], 'v_q_norm': out['v_q_norm'], 'v_w_uq': out['v_w_uq'], 'v_kv_norm': out['v_kv_norm'], 'v_w_ukv': out['v_w_ukv'], 'v_conv_w': out['v_conv_w'], 'v_conv_b': out['v_conv_b'], 'v_conv_ln_g': out['v_conv_ln_g'], 'v_conv_ln_b': out['v_conv_ln_b'], 'v_conv_out_norm': out['v_conv_out_norm'], 'v_attn_out_norm': out['v_attn_out_norm'], 'v_w_out': out['v_w_out'], 'v_post_mix_norm': out['v_post_mix_norm'], 'v_pre_ffn_norm': out['v_pre_ffn_norm'], 'v_w_gate': out['v_w_gate'], 'v_w_up': out['v_w_up'], 'v_w_down': out['v_w_down'], 'v_post_ffn_norm': out['v_post_ffn_norm']}


def _loss(weights, diff, rest, loss_target):
    with _jax.named_scope("forward"):
        args = {**rest, TWIN_DIFF_INPUT: diff, **{k: w.astype(_WEIGHT_DTYPES[k]) for k, w in weights.items()}}
        y = _forward(args)
    with _jax.named_scope("loss_head"):
        err = _jnp.square(y.astype(_jnp.float32) - loss_target)
        return 0.5 * _jnp.sum(_jnp.mean(err, axis=-1)) if err.ndim else 0.5 * err


def _adamw(w, g, m, v):
    m = ADAM_B1 * m + (1.0 - ADAM_B1) * g
    v = ADAM_B2 * v + (1.0 - ADAM_B2) * _jnp.square(g)
    m_hat = m / (1.0 - ADAM_B1 ** ADAM_STEP)
    v_hat = v / (1.0 - ADAM_B2 ** ADAM_STEP)
    delta = -ADAM_LR * (m_hat / (_jnp.sqrt(v_hat) + ADAM_EPS) + ADAM_WD * w)
    return delta, m, v


def reference(x, positions, pre_mix_norm, w_in, q_norm, w_uq, kv_norm, w_ukv, conv_w, conv_b, conv_ln_g, conv_ln_b, conv_out_norm, attn_out_norm, w_out, post_mix_norm, pre_ffn_norm, w_gate, w_up, w_down, post_ffn_norm, loss_target, m_pre_mix_norm, m_w_in, m_q_norm, m_w_uq, m_kv_norm, m_w_ukv, m_conv_w, m_conv_b, m_conv_ln_g, m_conv_ln_b, m_conv_out_norm, m_attn_out_norm, m_w_out, m_post_mix_norm, m_pre_ffn_norm, m_w_gate, m_w_up, m_w_down, m_post_ffn_norm, v_pre_mix_norm, v_w_in, v_q_norm, v_w_uq, v_kv_norm, v_w_ukv, v_conv_w, v_conv_b, v_conv_ln_g, v_conv_ln_b, v_conv_out_norm, v_attn_out_norm, v_w_out, v_post_mix_norm, v_pre_ffn_norm, v_w_gate, v_w_up, v_w_down, v_post_ffn_norm):
    given = dict(x=x, positions=positions, pre_mix_norm=pre_mix_norm, w_in=w_in, q_norm=q_norm, w_uq=w_uq, kv_norm=kv_norm, w_ukv=w_ukv, conv_w=conv_w, conv_b=conv_b, conv_ln_g=conv_ln_g, conv_ln_b=conv_ln_b, conv_out_norm=conv_out_norm, attn_out_norm=attn_out_norm, w_out=w_out, post_mix_norm=post_mix_norm, pre_ffn_norm=pre_ffn_norm, w_gate=w_gate, w_up=w_up, w_down=w_down, post_ffn_norm=post_ffn_norm, loss_target=loss_target, m_pre_mix_norm=m_pre_mix_norm, m_w_in=m_w_in, m_q_norm=m_q_norm, m_w_uq=m_w_uq, m_kv_norm=m_kv_norm, m_w_ukv=m_w_ukv, m_conv_w=m_conv_w, m_conv_b=m_conv_b, m_conv_ln_g=m_conv_ln_g, m_conv_ln_b=m_conv_ln_b, m_conv_out_norm=m_conv_out_norm, m_attn_out_norm=m_attn_out_norm, m_w_out=m_w_out, m_post_mix_norm=m_post_mix_norm, m_pre_ffn_norm=m_pre_ffn_norm, m_w_gate=m_w_gate, m_w_up=m_w_up, m_w_down=m_w_down, m_post_ffn_norm=m_post_ffn_norm, v_pre_mix_norm=v_pre_mix_norm, v_w_in=v_w_in, v_q_norm=v_q_norm, v_w_uq=v_w_uq, v_kv_norm=v_kv_norm, v_w_ukv=v_w_ukv, v_conv_w=v_conv_w, v_conv_b=v_conv_b, v_conv_ln_g=v_conv_ln_g, v_conv_ln_b=v_conv_ln_b, v_conv_out_norm=v_conv_out_norm, v_attn_out_norm=v_attn_out_norm, v_w_out=v_w_out, v_post_mix_norm=v_post_mix_norm, v_pre_ffn_norm=v_pre_ffn_norm, v_w_gate=v_w_gate, v_w_up=v_w_up, v_w_down=v_w_down, v_post_ffn_norm=v_post_ffn_norm)
    weights = {n: given[n] for n in TWIN_WEIGHTS}
    shared = {n: given[n] for n in SHARED_INPUTS}
    per_example = {n: given[n] for n in ['x', 'positions']}
    grad_fn = _jax.value_and_grad(_loss, argnums=(0, 1))

    def one_microbatch(ex, loss_target):
        ex = dict(ex)
        diff = ex.pop(TWIN_DIFF_INPUT)
        return grad_fn(weights, diff, {**shared, **ex}, loss_target)

    if N_MICROBATCH == 1:
        loss, (grad_w, grad_x) = one_microbatch(per_example, given["loss_target"])
    else:
        def body(carry, xs):
            loss_sum, grad_sum = carry
            l_k, (gw_k, gx_k) = one_microbatch(xs[0], xs[1])
            with _jax.named_scope("update"):
                return (loss_sum + l_k, _jax.tree.map(_jnp.add, grad_sum, gw_k)), gx_k

        init = (_jnp.zeros((), _jnp.float32), _jax.tree.map(_jnp.zeros_like, weights))
        (loss, grad_w), grad_x = _jax.lax.scan(body, init, (per_example, given["loss_target"]))
    with _jax.named_scope("update"):
        delta_w, new_m, new_v = {}, {}, {}
        for n in TWIN_WEIGHTS:
            delta_w[n], new_m[n], new_v[n] = _adamw(weights[n], grad_w[n], given["m_" + n], given["v_" + n])
    return (loss, grad_x, *[grad_w[n] for n in TWIN_WEIGHTS], *[delta_w[n] for n in TWIN_WEIGHTS],
            *[new_m[n] for n in TWIN_WEIGHTS], *[new_v[n] for n in TWIN_WEIGHTS])
```

```python
import functools

import jax
import jax.numpy as jnp
from jax import lax
from jax.experimental import pallas as pl
from jax.experimental.pallas import tpu as pltpu

N_DEV = 8
N_HEADS = 8
QK_NOPE = 128
QK_ROPE = 64
V_HEAD = 128
QK_HEAD = QK_NOPE + QK_ROPE
HEAD_PAD = 256
LANES = 128
CONV_K = 31
CONV_PAD = 32
EPS = 1e-6
ROPE_THETA = 10000.0
ADAM_LR = 0.001
ADAM_B1 = 0.9
ADAM_B2 = 0.999
ADAM_EPS = 1e-08
ADAM_WD = 0.01
ADAM_STEP = 10
VMEM_LIMIT = 56 * 1024 * 1024
F32 = jnp.float32
BF16 = jnp.bfloat16
MESH = pl.DeviceIdType.MESH
NEG = -1e30


def _pick(n, prefs):
    for p in prefs:
        if p <= n and n % p == 0:
            return p
    return n


def _params(sem):
    return pltpu.CompilerParams(dimension_semantics=sem, vmem_limit_bytes=VMEM_LIMIT)


_DIMS = {"nn": (((1,), (0,)), ((), ())), "nt": (((1,), (1,)), ((), ())), "tn": (((0,), (0,)), ((), ()))}


def _mm(a, b, mode, name, out_dtype=F32, add=None):
    if mode == "nn":
        (m, k), (k2, n) = a.shape, b.shape
    elif mode == "nt":
        (m, k), (n, k2) = a.shape, b.shape
    else:
        (k, m), (k2, n) = a.shape, b.shape
    assert k == k2, (a.shape, b.shape, mode)
    tm = _pick(m, (512, 384, 256, 128))
    tn = _pick(n, (512, 384, 256, 128))
    tk = _pick(k, (2048, 1536, 1152, 1024, 768, 512, 384, 256, 128))
    nk = k // tk
    dims = _DIMS[mode]

    def body(a_ref, b_ref, *rest):
        o_ref, acc_ref = rest[-2:]
        kk = pl.program_id(2)

        @pl.when(kk == 0)
        def _():
            acc_ref[...] = jnp.zeros_like(acc_ref) if add is None else rest[0][...]

        acc_ref[...] += lax.dot_general(a_ref[...].astype(BF16), b_ref[...].astype(BF16), dims,
                                        preferred_element_type=F32)

        @pl.when(kk == nk - 1)
        def _():
            o_ref[...] = acc_ref[...].astype(o_ref.dtype)

    if mode == "tn":
        a_spec = pl.BlockSpec((tk, tm), lambda i, j, kk: (kk, i))
    else:
        a_spec = pl.BlockSpec((tm, tk), lambda i, j, kk: (i, kk))
    if mode == "nt":
        b_spec = pl.BlockSpec((tn, tk), lambda i, j, kk: (j, kk))
    else:
        b_spec = pl.BlockSpec((tk, tn), lambda i, j, kk: (kk, j))
    o_spec = pl.BlockSpec((tm, tn), lambda i, j, kk: (i, j))
    extra = [] if add is None else [add]
    return pl.pallas_call(
        body, name=name,
        grid=(m // tm, n // tn, nk),
        in_specs=[a_spec, b_spec] + [o_spec] * len(extra),
        out_specs=o_spec,
        out_shape=jax.ShapeDtypeStruct((m, n), out_dtype),
        scratch_shapes=[pltpu.VMEM((tm, tn), F32)],
        compiler_params=_params(("parallel", "parallel", "arbitrary")),
    )(a, b, *extra)


def _sigmoid(x):
    return 1.0 / (1.0 + jnp.exp(-x))


def _rms(x, g):
    r = lax.rsqrt(jnp.mean(x * x, axis=-1, keepdims=True) + EPS)
    return (x * r) * g


def _rms_bwd(x, g, dy):
    r = lax.rsqrt(jnp.mean(x * x, axis=-1, keepdims=True) + EPS)
    xh = x * r
    dyg = dy * g
    dx = r * (dyg - xh * jnp.mean(dyg * xh, axis=-1, keepdims=True))
    return dx, jnp.sum(dy * xh, axis=0, keepdims=True)


def _ln(x, g, b):
    mu = jnp.mean(x, axis=-1, keepdims=True)
    xc = x - mu
    rs = lax.rsqrt(jnp.mean(xc * xc, axis=-1, keepdims=True) + EPS)
    return (xc * rs) * g + b


def _ln_bwd(x, g, dy):
    mu = jnp.mean(x, axis=-1, keepdims=True)
    xc = x - mu
    rs = lax.rsqrt(jnp.mean(xc * xc, axis=-1, keepdims=True) + EPS)
    xh = xc * rs
    dyg = dy * g
    dx = rs * (dyg - jnp.mean(dyg, axis=-1, keepdims=True) - xh * jnp.mean(dyg * xh, axis=-1, keepdims=True))
    return dx, jnp.sum(dy * xh, axis=0, keepdims=True), jnp.sum(dy, axis=0, keepdims=True)


def _silu(x):
    return x * _sigmoid(x)


def _silu_grad(x):
    s = _sigmoid(x)
    return s * (1.0 + x * (1.0 - s))


def _rope(x, cos, sa, sb):
    return x * cos + pltpu.roll(x, 96, 1) * sa + pltpu.roll(x, 32, 1) * sb


def _rope_t(d, cos, sa, sb):
    return d * cos - pltpu.roll(d, 96, 1) * sa - pltpu.roll(d, 32, 1) * sb


def _rows(ts, w):
    return pl.BlockSpec((ts, w), lambda i: (i, 0))


def _vec(w):
    return pl.BlockSpec((1, w), lambda i: (0, 0))


def _acc_init(i, *refs):
    @pl.when(i == 0)
    def _():
        for r in refs:
            r[...] = jnp.zeros_like(r)


def _rope_tables(pos, inv_freq):
    s = pos.shape[0]
    ts = _pick(s, (512, 256, 128))

    def body(p_ref, f_ref, c_ref, sa_ref, sb_ref):
        ang = p_ref[...].astype(F32) * f_ref[...]
        lane = lax.broadcasted_iota(jnp.int32, ang.shape, 1)
        c, sn = jnp.cos(ang), jnp.sin(ang)
        c_ref[...] = jnp.where(lane < QK_ROPE, c, 0.0)
        sa_ref[...] = jnp.where(lane < QK_ROPE // 2, -sn, 0.0)
        sb_ref[...] = jnp.where((lane >= QK_ROPE // 2) & (lane < QK_ROPE), sn, 0.0)

    out = jax.ShapeDtypeStruct((s, LANES), F32)
    return pl.pallas_call(
        body, name="rope_tables", grid=(s // ts,),
        in_specs=[_rows(ts, 1), _vec(LANES)],
        out_specs=[_rows(ts, LANES)] * 3, out_shape=[out] * 3,
        compiler_params=_params(("parallel",)),
    )(pos, inv_freq)


def _pre_fwd(x, g):
    s, d = x.shape
    ts = _pick(s, (256, 128))

    def body(x_ref, g_ref, h_ref):
        h_ref[...] = _rms(x_ref[...], g_ref[...]).astype(BF16)

    return pl.pallas_call(
        body, name="pre_fwd", grid=(s // ts,),
        in_specs=[_rows(ts, d), _vec(d)], out_specs=_rows(ts, d),
        out_shape=jax.ShapeDtypeStruct((s, d), BF16),
        compiler_params=_params(("parallel",)),
    )(x, g)


def _split_fwd(z, gq, gkv, tabs, c, ql, kvl):
    s, zw = z.shape
    ts = _pick(s, (256, 128))
    o_q, o_kv, o_kr = 2 * c, 2 * c + ql, 2 * c + ql + kvl

    def body(z_ref, gq_ref, gkv_ref, c_ref, sa_ref, sb_ref, u0_ref, qn_ref, kvn_ref, kpe_ref):
        u0_ref[...] = z_ref[:, 0:c] * _sigmoid(z_ref[:, c:2 * c])
        qn_ref[...] = _rms(z_ref[:, o_q:o_kv], gq_ref[...]).astype(BF16)
        kvn_ref[...] = _rms(z_ref[:, o_kv:o_kr], gkv_ref[...]).astype(BF16)
        kpe_ref[...] = _rope(z_ref[:, o_kr:o_kr + LANES], c_ref[...], sa_ref[...], sb_ref[...]).astype(BF16)

    return pl.pallas_call(
        body, name="split_fwd", grid=(s // ts,),
        in_specs=[_rows(ts, zw), _vec(ql), _vec(kvl)] + [_rows(ts, LANES)] * 3,
        out_specs=[_rows(ts, c), _rows(ts, ql), _rows(ts, kvl), _rows(ts, LANES)],
        out_shape=[jax.ShapeDtypeStruct((s, c), F32), jax.ShapeDtypeStruct((s, ql), BF16),
                   jax.ShapeDtypeStruct((s, kvl), BF16), jax.ShapeDtypeStruct((s, LANES), BF16)],
        compiler_params=_params(("parallel",)),
    )(z, gq, gkv, *tabs)


def _split_bwd(du0, z, dqn, dkvn, dkpe_h, gq, gkv, tabs, c, ql, kvl):
    s, zw = z.shape
    ts = _pick(s, (256, 128))
    o_q, o_kv, o_kr = 2 * c, 2 * c + ql, 2 * c + ql + kvl

    def body(du0_ref, z_ref, dqn_ref, dkvn_ref, dkh_ref, gq_ref, gkv_ref, c_ref, sa_ref, sb_ref,
             dz_ref, dgq_ref, dgkv_ref):
        _acc_init(pl.program_id(0), dgq_ref, dgkv_ref)
        du0 = du0_ref[...]
        a = z_ref[:, 0:c]
        sg = _sigmoid(z_ref[:, c:2 * c])
        dz_ref[:, 0:c] = (du0 * sg).astype(BF16)
        dz_ref[:, c:2 * c] = (du0 * a * sg * (1.0 - sg)).astype(BF16)
        dq, dgq = _rms_bwd(z_ref[:, o_q:o_kv], gq_ref[...], dqn_ref[...])
        dz_ref[:, o_q:o_kv] = dq.astype(BF16)
        dgq_ref[...] += dgq
        dkv, dgkv = _rms_bwd(z_ref[:, o_kv:o_kr], gkv_ref[...], dkvn_ref[...])
        dz_ref[:, o_kv:o_kr] = dkv.astype(BF16)
        dgkv_ref[...] += dgkv
        dk = dkh_ref[:, 0:LANES]
        for h in range(1, N_HEADS):
            dk = dk + dkh_ref[:, h * LANES:(h + 1) * LANES]
        dz_ref[:, o_kr:o_kr + LANES] = _rope_t(dk, c_ref[...], sa_ref[...], sb_ref[...]).astype(BF16)

    return pl.pallas_call(
        body, name="split_bwd", grid=(s // ts,),
        in_specs=[_rows(ts, c), _rows(ts, zw), _rows(ts, ql), _rows(ts, kvl), _rows(ts, N_HEADS * LANES),
                  _vec(ql), _vec(kvl)] + [_rows(ts, LANES)] * 3,
        out_specs=[_rows(ts, zw), _vec(ql), _vec(kvl)],
        out_shape=[jax.ShapeDtypeStruct((s, zw), BF16), jax.ShapeDtypeStruct((1, ql), F32),
                   jax.ShapeDtypeStruct((1, kvl), F32)],
        compiler_params=_params(("arbitrary",)),
    )(du0, z, dqn, dkvn, dkpe_h, gq, gkv, *tabs)


def _q_rope(qpre, tabs, transpose, out_dtype, name):
    s, w = qpre.shape
    ts = _pick(s, (256, 128))
    rot = _rope_t if transpose else _rope

    def body(q_ref, c_ref, sa_ref, sb_ref, o_ref):
        cs, sa, sb = c_ref[...], sa_ref[...], sb_ref[...]
        for h in range(N_HEADS):
            lo = h * HEAD_PAD
            o_ref[:, lo:lo + QK_NOPE] = q_ref[:, lo:lo + QK_NOPE].astype(out_dtype)
            o_ref[:, lo + QK_NOPE:lo + HEAD_PAD] = rot(q_ref[:, lo + QK_NOPE:lo + HEAD_PAD], cs, sa, sb).astype(out_dtype)

    return pl.pallas_call(
        body, name=name, grid=(s // ts,),
        in_specs=[_rows(ts, w)] + [_rows(ts, LANES)] * 3, out_specs=_rows(ts, w),
        out_shape=jax.ShapeDtypeStruct((s, w), out_dtype),
        compiler_params=_params(("parallel",)),
    )(qpre, *tabs)


def _conv_fwd(u0, w, b):
    s, c = u0.shape
    tc = LANES
    rc = _pick(s, (256, 128))

    def body(u_ref, w_ref, b_ref, o_ref, pad_ref):
        pad_ref[0:CONV_PAD, :] = jnp.zeros((CONV_PAD, tc), F32)
        pad_ref[CONV_PAD:CONV_PAD + s, :] = u_ref[...]
        for r in range(s // rc):
            acc = jnp.broadcast_to(b_ref[...], (rc, tc))
            for k in range(CONV_K):
                lo = r * rc + CONV_PAD - (CONV_K - 1) + k
                acc = acc + w_ref[k:k + 1, :] * pad_ref[lo:lo + rc, :]
            o_ref[r * rc:(r + 1) * rc, :] = acc

    col = lambda j: (0, j)
    return pl.pallas_call(
        body, name="conv_fwd", grid=(c // tc,),
        in_specs=[pl.BlockSpec((s, tc), col), pl.BlockSpec((CONV_K, tc), col), pl.BlockSpec((1, tc), col)],
        out_specs=pl.BlockSpec((s, tc), col),
        out_shape=jax.ShapeDtypeStruct((s, c), F32),
        scratch_shapes=[pltpu.VMEM((s + CONV_PAD, tc), F32)],
        compiler_params=_params(("parallel",)),
    )(u0, w, b)


def _conv_bwd(du1, u0, w):
    s, c = u0.shape
    tc = LANES
    rc = _pick(s, (256, 128))

    def body(d_ref, u_ref, w_ref, du_ref, dw_ref, db_ref, upad_ref, dpad_ref):
        upad_ref[0:CONV_PAD, :] = jnp.zeros((CONV_PAD, tc), F32)
        upad_ref[CONV_PAD:CONV_PAD + s, :] = u_ref[...]
        dpad_ref[0:s, :] = d_ref[...]
        dpad_ref[s:s + CONV_PAD, :] = jnp.zeros((CONV_PAD, tc), F32)
        for r in range(s // rc):
            acc = jnp.zeros((rc, tc), F32)
            for k in range(CONV_K):
                lo = r * rc + (CONV_K - 1) - k
                acc = acc + w_ref[k:k + 1, :] * dpad_ref[lo:lo + rc, :]
            du_ref[r * rc:(r + 1) * rc, :] = acc
        for k in range(CONV_K):
            acc8 = jnp.zeros((8, tc), F32)
            for r in range(s // rc):
                lo = r * rc + CONV_PAD - (CONV_K - 1) + k
                prod = d_ref[r * rc:(r + 1) * rc, :] * upad_ref[lo:lo + rc, :]
                acc8 = acc8 + jnp.sum(prod.reshape(rc // 8, 8, tc), axis=0)
            dw_ref[k:k + 1, :] = jnp.sum(acc8, axis=0, keepdims=True)
        db_ref[...] = jnp.sum(d_ref[...], axis=0, keepdims=True)

    col = lambda j: (0, j)
    return pl.pallas_call(
        body, name="conv_bwd", grid=(c // tc,),
        in_specs=[pl.BlockSpec((s, tc), col), pl.BlockSpec((s, tc), col), pl.BlockSpec((CONV_K, tc), col)],
        out_specs=[pl.BlockSpec((s, tc), col), pl.BlockSpec((CONV_K, tc), col), pl.BlockSpec((1, tc), col)],
        out_shape=[jax.ShapeDtypeStruct((s, c), F32), jax.ShapeDtypeStruct((CONV_K, c), F32),
                   jax.ShapeDtypeStruct((1, c), F32)],
        scratch_shapes=[pltpu.VMEM((s + CONV_PAD, tc), F32), pltpu.VMEM((s + CONV_PAD, tc), F32)],
        compiler_params=_params(("parallel",)),
    )(du1, u0, w)


def _causal_mask(sc, qi, kj, tq, tk):
    rows = qi * tq + lax.broadcasted_iota(jnp.int32, sc.shape, 0)
    cols = kj * tk + lax.broadcasted_iota(jnp.int32, sc.shape, 1)
    return jnp.where(cols <= rows, sc, NEG)


def _attn_fwd(q, kv, kpe):
    s = q.shape[0]
    tq = tk = _pick(s, (256, 128))
    scale = QK_HEAD ** -0.5
    nt = (((1,), (1,)), ((), ()))

    def body(q_ref, kn_ref, v_ref, kpe_ref, o_ref, lse_ref, kf_ref, vb_ref):
        i = pl.program_id(1)

        @pl.when(i == 0)
        def _():
            kf_ref[:, 0:QK_NOPE] = kn_ref[...].astype(BF16)
            kf_ref[:, QK_NOPE:HEAD_PAD] = kpe_ref[...]
            vb_ref[...] = v_ref[...].astype(BF16)

        qb = q_ref[...]

        def step(j, carry):
            m, l, acc = carry
            off = pl.multiple_of(j * tk, tk)
            sc = lax.dot_general(qb, kf_ref[pl.ds(off, tk), :], nt, preferred_element_type=F32) * scale
            sc = _causal_mask(sc, i, j, tq, tk)
            m_new = jnp.maximum(m, jnp.max(sc, axis=1, keepdims=True))
            p = jnp.exp(sc - m_new)
            alpha = jnp.exp(m - m_new)
            l = alpha * l + jnp.sum(p, axis=1, keepdims=True)
            acc = alpha * acc + jnp.dot(p.astype(BF16), vb_ref[pl.ds(off, tk), :], preferred_element_type=F32)
            return m_new, l, acc

        init = (jnp.full((tq, 1), NEG, F32), jnp.zeros((tq, 1), F32), jnp.zeros((tq, V_HEAD), F32))
        m, l, acc = lax.fori_loop(0, i + 1, step, init)
        o_ref[...] = acc / l
        lse_ref[...] = jnp.broadcast_to(m + jnp.log(l), (tq, LANES))

    return pl.pallas_call(
        body, name="attn_fwd", grid=(N_HEADS, s // tq),
        in_specs=[pl.BlockSpec((tq, HEAD_PAD), lambda h, i: (i, h)),
                  pl.BlockSpec((s, QK_NOPE), lambda h, i: (0, 2 * h)),
                  pl.BlockSpec((s, V_HEAD), lambda h, i: (0, 2 * h + 1)),
                  pl.BlockSpec((s, LANES), lambda h, i: (0, 0))],
        out_specs=[pl.BlockSpec((tq, V_HEAD), lambda h, i: (i, h)),
                   pl.BlockSpec((tq, LANES), lambda h, i: (i, h))],
        out_shape=[jax.ShapeDtypeStruct((s, N_HEADS * V_HEAD), F32),
                   jax.ShapeDtypeStruct((s, N_HEADS * LANES), F32)],
        scratch_shapes=[pltpu.VMEM((s, HEAD_PAD), BF16), pltpu.VMEM((s, V_HEAD), BF16)],
        compiler_params=_params(("parallel", "arbitrary")),
    )(q, kv, kv, kpe)


def _attn_bwd(q, kv, kpe, o, do, lse):
    s = q.shape[0]
    tq = tk = _pick(s, (256, 128))
    nq = s // tq
    scale = QK_HEAD ** -0.5
    nt = (((1,), (1,)), ((), ()))
    tn = (((0,), (0,)), ((), ()))

    def body(q_ref, kn_ref, v_ref, kpe_ref, o_ref, do_ref, lse_ref, dq_ref, dkv_ref, dkpe_ref):
        j = pl.program_id(1)

        @pl.when(j == 0)
        def _():
            dq_ref[...] = jnp.zeros_like(dq_ref)

        kf = jnp.concatenate([kn_ref[...].astype(BF16), kpe_ref[...]], axis=1)
        vb = v_ref[...].astype(BF16)

        def step(i, carry):
            dk, dv = carry
            off = pl.multiple_of(i * tq, tq)
            qb = q_ref[pl.ds(off, tq), :]
            dob = do_ref[pl.ds(off, tq), :]
            delta = jnp.sum(dob * o_ref[pl.ds(off, tq), :], axis=1, keepdims=True)
            lse_b = lse_ref[pl.ds(off, tq), :][:, 0:1]
            sc = lax.dot_general(qb, kf, nt, preferred_element_type=F32) * scale
            sc = _causal_mask(sc, i, j, tq, tk)
            p = jnp.exp(sc - lse_b)
            dob16 = dob.astype(BF16)
            dv = dv + lax.dot_general(p.astype(BF16), dob16, tn, preferred_element_type=F32)
            dp = lax.dot_general(dob16, vb, nt, preferred_element_type=F32)
            ds = (p * (dp - delta) * scale).astype(BF16)
            dq_ref[pl.ds(off, tq), :] += jnp.dot(ds, kf, preferred_element_type=F32)
            dk = dk + lax.dot_general(ds, qb, tn, preferred_element_type=F32)
            return dk, dv

        dk, dv = lax.fori_loop(j, nq, step, (jnp.zeros((tk, HEAD_PAD), F32), jnp.zeros((tk, V_HEAD), F32)))
        dkv_ref[:, 0:QK_NOPE] = dk[:, 0:QK_NOPE]
        dkv_ref[:, QK_NOPE:HEAD_PAD] = dv
        dkpe_ref[...] = dk[:, QK_NOPE:HEAD_PAD]

    head_rows = lambda w: pl.BlockSpec((s, w), lambda h, j: (0, h))
    return pl.pallas_call(
        body, name="attn_bwd", grid=(N_HEADS, s // tk),
        in_specs=[head_rows(HEAD_PAD),
                  pl.BlockSpec((tk, QK_NOPE), lambda h, j: (j, 2 * h)),
                  pl.BlockSpec((tk, V_HEAD), lambda h, j: (j, 2 * h + 1)),
                  pl.BlockSpec((tk, LANES), lambda h, j: (j, 0)),
                  head_rows(V_HEAD), head_rows(V_HEAD), head_rows(LANES)],
        out_specs=[head_rows(HEAD_PAD),
                   pl.BlockSpec((tk, HEAD_PAD), lambda h, j: (j, h)),
                   pl.BlockSpec((tk, LANES), lambda h, j: (j, h))],
        out_shape=[jax.ShapeDtypeStruct((s, N_HEADS * HEAD_PAD), F32),
                   jax.ShapeDtypeStruct((s, N_HEADS * HEAD_PAD), F32),
                   jax.ShapeDtypeStruct((s, N_HEADS * LANES), F32)],
        compiler_params=_params(("parallel", "arbitrary")),
    )(q, kv, kv, kpe, o, do, lse)


def _mix_fwd(u1, lng, lnb, gcon, attn, gattn):
    s, c = u1.shape
    ac = attn.shape[1]
    ts = _pick(s, (256, 128))

    def body(u_ref, lg_ref, lb_ref, gc_ref, a_ref, ga_ref, o_ref):
        t3 = _silu(_ln(u_ref[...], lg_ref[...], lb_ref[...]))
        o_ref[:, 0:c] = _rms(t3, gc_ref[...]).astype(BF16)
        o_ref[:, c:c + ac] = _rms(a_ref[...], ga_ref[...]).astype(BF16)

    return pl.pallas_call(
        body, name="mix_fwd", grid=(s // ts,),
        in_specs=[_rows(ts, c), _vec(c), _vec(c), _vec(c), _rows(ts, ac), _vec(ac)],
        out_specs=_rows(ts, c + ac),
        out_shape=jax.ShapeDtypeStruct((s, c + ac), BF16),
        compiler_params=_params(("parallel",)),
    )(u1, lng, lnb, gcon, attn, gattn)


def _mix_bwd(dmixin, u1, lng, lnb, gcon, attn, gattn):
    s, c = u1.shape
    ac = attn.shape[1]
    ts = _pick(s, (256, 128))

    def body(d_ref, u_ref, lg_ref, lb_ref, gc_ref, a_ref, ga_ref,
             du_ref, da_ref, dlg_ref, dlb_ref, dgc_ref, dga_ref):
        _acc_init(pl.program_id(0), dlg_ref, dlb_ref, dgc_ref, dga_ref)
        u = u_ref[...]
        t2 = _ln(u, lg_ref[...], lb_ref[...])
        dt3, dgc = _rms_bwd(_silu(t2), gc_ref[...], d_ref[:, 0:c])
        du, dlg, dlb = _ln_bwd(u, lg_ref[...], dt3 * _silu_grad(t2))
        du_ref[...] = du
        dlg_ref[...] += dlg
        dlb_ref[...] += dlb
        dgc_ref[...] += dgc
        da, dga = _rms_bwd(a_ref[...], ga_ref[...], d_ref[:, c:c + ac])
        da_ref[...] = da
        dga_ref[...] += dga

    return pl.pallas_call(
        body, name="mix_bwd", grid=(s // ts,),
        in_specs=[_rows(ts, c + ac), _rows(ts, c), _vec(c), _vec(c), _vec(c), _rows(ts, ac), _vec(ac)],
        out_specs=[_rows(ts, c), _rows(ts, ac), _vec(c), _vec(c), _vec(c), _vec(ac)],
        out_shape=[jax.ShapeDtypeStruct((s, c), F32), jax.ShapeDtypeStruct((s, ac), F32),
                   jax.ShapeDtypeStruct((1, c), F32), jax.ShapeDtypeStruct((1, c), F32),
                   jax.ShapeDtypeStruct((1, c), F32), jax.ShapeDtypeStruct((1, ac), F32)],
        compiler_params=_params(("arbitrary",)),
    )(dmixin, u1, lng, lnb, gcon, attn, gattn)


def _post_mix_fwd(x, mix, gpost, gpre):
    s, d = x.shape
    ts = _pick(s, (256, 128))

    def body(x_ref, m_ref, gp_ref, gf_ref, x1_ref, hf_ref):
        x1 = x_ref[...] + _rms(m_ref[...], gp_ref[...])
        x1_ref[...] = x1
        hf_ref[...] = _rms(x1, gf_ref[...]).astype(BF16)

    return pl.pallas_call(
        body, name="post_mix_fwd", grid=(s // ts,),
        in_specs=[_rows(ts, d), _rows(ts, d), _vec(d), _vec(d)],
        out_specs=[_rows(ts, d), _rows(ts, d)],
        out_shape=[jax.ShapeDtypeStruct((s, d), F32), jax.ShapeDtypeStruct((s, d), BF16)],
        compiler_params=_params(("parallel",)),
    )(x, mix, gpost, gpre)


def _post_mix_bwd(dy, dhf, x1, gpre, mix, gpost):
    s, d = x1.shape
    ts = _pick(s, (256, 128))

    def body(dy_ref, dh_ref, x1_ref, gf_ref, m_ref, gp_ref, dx1_ref, dm_ref, dgf_ref, dgp_ref):
        _acc_init(pl.program_id(0), dgf_ref, dgp_ref)
        dxa, dgf = _rms_bwd(x1_ref[...], gf_ref[...], dh_ref[...])
        dx1 = dy_ref[...] + dxa
        dx1_ref[...] = dx1
        dgf_ref[...] += dgf
        dm, dgp = _rms_bwd(m_ref[...], gp_ref[...], dx1)
        dm_ref[...] = dm.astype(BF16)
        dgp_ref[...] += dgp

    return pl.pallas_call(
        body, name="post_mix_bwd", grid=(s // ts,),
        in_specs=[_rows(ts, d), _rows(ts, d), _rows(ts, d), _vec(d), _rows(ts, d), _vec(d)],
        out_specs=[_rows(ts, d), _rows(ts, d), _vec(d), _vec(d)],
        out_shape=[jax.ShapeDtypeStruct((s, d), F32), jax.ShapeDtypeStruct((s, d), BF16),
                   jax.ShapeDtypeStruct((1, d), F32), jax.ShapeDtypeStruct((1, d), F32)],
        compiler_params=_params(("arbitrary",)),
    )(dy, dhf, x1, gpre, mix, gpost)


def _ffn_act(gate, up):
    s, f = gate.shape
    ts = _pick(s, (256, 128))
    tf = _pick(f, (512, 256, 128))

    def body(g_ref, u_ref, o_ref):
        o_ref[...] = (_silu(g_ref[...]) * u_ref[...]).astype(BF16)

    return pl.pallas_call(
        body, name="ffn_act", grid=(s // ts, f // tf),
        in_specs=[pl.BlockSpec((ts, tf), lambda i, j: (i, j))] * 2,
        out_specs=pl.BlockSpec((ts, tf), lambda i, j: (i, j)),
        out_shape=jax.ShapeDtypeStruct((s, f), BF16),
        compiler_params=_params(("parallel", "parallel")),
    )(gate, up)


def _ffn_act_bwd(dact, gate, up):
    s, f = gate.shape
    ts = _pick(s, (256, 128))
    tf = _pick(f, (512, 256, 128))
    nf = f // tf

    def body(d_ref, g_ref, u_ref, dg_ref, du_ref):
        d, g = d_ref[...], g_ref[...]
        dg_ref[...] = (d * u_ref[...] * _silu_grad(g)).astype(BF16)
        du_ref[...] = (d * _silu(g)).astype(BF16)

    blk = pl.BlockSpec((ts, tf), lambda i, j: (i, j))
    return pl.pallas_call(
        body, name="ffn_act_bwd", grid=(s // ts, nf),
        in_specs=[blk] * 3, out_specs=[blk] * 2,
        out_shape=[jax.ShapeDtypeStruct((s, f), BF16)] * 2,
        compiler_params=_params(("parallel", "parallel")),
    )(dact, gate, up)


def _final(ff, x1, tgt, g):
    s, d = x1.shape
    ts = _pick(s, (256, 128))

    def body(ff_ref, x1_ref, t_ref, g_ref, loss_ref, dy_ref, dff_ref, dg_ref):
        _acc_init(pl.program_id(0), loss_ref, dg_ref)
        ff_v = ff_ref[...]
        err = x1_ref[...] + _rms(ff_v, g_ref[...]) - t_ref[...]
        tok = jnp.mean(err * err, axis=-1, keepdims=True)
        loss_ref[...] += 0.5 * jnp.sum(tok, axis=0, keepdims=True)
        dy = err * (1.0 / d)
        dy_ref[...] = dy
        dff, dg = _rms_bwd(ff_v, g_ref[...], dy)
        dff_ref[...] = dff.astype(BF16)
        dg_ref[...] += dg

    return pl.pallas_call(
        body, name="final", grid=(s // ts,),
        in_specs=[_rows(ts, d), _rows(ts, d), _rows(ts, d), _vec(d)],
        out_specs=[_vec(LANES), _rows(ts, d), _rows(ts, d), _vec(d)],
        out_shape=[jax.ShapeDtypeStruct((1, LANES), F32), jax.ShapeDtypeStruct((s, d), F32),
                   jax.ShapeDtypeStruct((s, d), BF16), jax.ShapeDtypeStruct((1, d), F32)],
        compiler_params=_params(("arbitrary",)),
    )(ff, x1, tgt, g)


def _pre_bwd(dx1, dh, x, g):
    s, d = x.shape
    ts = _pick(s, (256, 128))

    def body(dx1_ref, dh_ref, x_ref, g_ref, dx_ref, dg_ref):
        _acc_init(pl.program_id(0), dg_ref)
        dxa, dg = _rms_bwd(x_ref[...], g_ref[...], dh_ref[...])
        dx_ref[...] = dx1_ref[...] + dxa
        dg_ref[...] += dg

    return pl.pallas_call(
        body, name="pre_bwd", grid=(s // ts,),
        in_specs=[_rows(ts, d), _rows(ts, d), _rows(ts, d), _vec(d)],
        out_specs=[_rows(ts, d), _vec(d)],
        out_shape=[jax.ShapeDtypeStruct((s, d), F32), jax.ShapeDtypeStruct((1, d), F32)],
        compiler_params=_params(("arbitrary",)),
    )(dx1, dh, x, g)


def _local_step(x, pos, tgt, vecs, w_in_p, w_uq_p, w_ukv, conv_w, w_out, w_gate, w_up, w_down):
    c = vecs["conv_b"].shape[1]
    ql = vecs["q_norm"].shape[1]
    kvl = vecs["kv_norm"].shape[1]
    half = jnp.arange(0, QK_ROPE, 2, dtype=F32)
    freq = ROPE_THETA ** (-half / QK_ROPE)
    inv_freq = jnp.concatenate([freq, freq, jnp.zeros((LANES - QK_ROPE,), F32)])[None, :]
    tabs = _rope_tables(pos, inv_freq)

    h = _pre_fwd(x, vecs["pre_mix_norm"])
    z = _mm(h, w_in_p, "nn", "mm_z")
    u0, qn, kvn, kpe = _split_fwd(z, vecs["q_norm"], vecs["kv_norm"], tabs, c, ql, kvl)
    u1 = _conv_fwd(u0, conv_w, vecs["conv_b"])
    q = _q_rope(_mm(qn, w_uq_p, "nn", "mm_q"), tabs, False, BF16, "q_rope")
    kv = _mm(kvn, w_ukv, "nn", "mm_kv")
    attn, lse = _attn_fwd(q, kv, kpe)
    mixin = _mix_fwd(u1, vecs["conv_ln_g"], vecs["conv_ln_b"], vecs["conv_out_norm"], attn, vecs["attn_out_norm"])
    mix = _mm(mixin, w_out, "nn", "mm_mix")
    x1, hf = _post_mix_fwd(x, mix, vecs["post_mix_norm"], vecs["pre_ffn_norm"])
    gate = _mm(hf, w_gate, "nn", "mm_gate")
    up = _mm(hf, w_up, "nn", "mm_up")
    act = _ffn_act(gate, up)
    ff = _mm(act, w_down, "nn", "mm_ff")
    loss, dy, dff, d_post_ffn = _final(ff, x1, tgt, vecs["post_ffn_norm"])

    g = {"post_ffn_norm": d_post_ffn}
    dact = _mm(dff, w_down, "nt", "mm_dact")
    g["w_down"] = _mm(act, dff, "tn", "mm_dw_down", BF16)
    dgate, dup = _ffn_act_bwd(dact, gate, up)
    dhf = _mm(dup, w_up, "nt", "mm_dhf_up", add=_mm(dgate, w_gate, "nt", "mm_dhf_gate"))
    g["w_gate"] = _mm(hf, dgate, "tn", "mm_dw_gate", BF16)
    g["w_up"] = _mm(hf, dup, "tn", "mm_dw_up", BF16)
    dx1, dmix, g["pre_ffn_norm"], g["post_mix_norm"] = _post_mix_bwd(
        dy, dhf, x1, vecs["pre_ffn_norm"], mix, vecs["post_mix_norm"])
    dmixin = _mm(dmix, w_out, "nt", "mm_dmixin")
    g["w_out"] = _mm(mixin, dmix, "tn", "mm_dw_out", BF16)
    du1, dattn, g["conv_ln_g"], g["conv_ln_b"], g["conv_out_norm"], g["attn_out_norm"] = _mix_bwd(
        dmixin, u1, vecs["conv_ln_g"], vecs["conv_ln_b"], vecs["conv_out_norm"], attn, vecs["attn_out_norm"])
    du0, g["conv_w"], g["conv_b"] = _conv_bwd(du1, u0, conv_w)
    dq, dkv, dkpe_h = _attn_bwd(q, kv, kpe, attn, dattn, lse)
    dqpre = _q_rope(dq, tabs, True, BF16, "q_rope_bwd")
    dqn = _mm(dqpre, w_uq_p, "nt", "mm_dqn")
    g["w_uq_p"] = _mm(qn, dqpre, "tn", "mm_dw_uq", BF16)
    dkvn = _mm(dkv, w_ukv, "nt", "mm_dkvn")
    g["w_ukv"] = _mm(kvn, dkv, "tn", "mm_dw_ukv", BF16)
    dz, g["q_norm"], g["kv_norm"] = _split_bwd(du0, z, dqn, dkvn, dkpe_h, vecs["q_norm"], vecs["kv_norm"], tabs, c, ql, kvl)
    dh = _mm(dz, w_in_p, "nt", "mm_dh")
    g["w_in_p"] = _mm(h, dz, "tn", "mm_dw_in", BF16)
    grad_x, g["pre_mix_norm"] = _pre_bwd(dx1, dh, x, vecs["pre_mix_norm"])
    return loss, grad_x, g


def _my_index():
    return 4 * lax.axis_index("x") + 2 * lax.axis_index("y") + lax.axis_index("c")


def _coords(idx):
    return ((idx >> 2) & 1, (idx >> 1) & 1, idx & 1)


def _exchange(arrays, gather, name):
    n = len(arrays)
    out_shapes = [jax.ShapeDtypeStruct(((N_DEV,) + a.shape) if gather else a.shape, a.dtype) for a in arrays]

    def body(*refs):
        ins, outs = refs[:n], refs[n:2 * n]
        send_sems, recv_sems, local_sems = refs[2 * n:]
        me = _my_index()

        def src_of(k, dst_dev):
            return ins[k] if gather else ins[k].at[dst_dev]

        local = [pltpu.make_async_copy(src_of(k, me), outs[k].at[me], local_sems.at[k]) for k in range(n)]
        for cp in local:
            cp.start()
        sends = []
        for p in range(1, N_DEV):
            peer = me ^ p
            for k in range(n):
                sends.append(pltpu.make_async_remote_copy(
                    src_ref=src_of(k, peer), dst_ref=outs[k].at[me],
                    send_sem=send_sems.at[k, p - 1], recv_sem=recv_sems.at[k, p - 1],
                    device_id=_coords(peer), device_id_type=MESH))
        for cp in sends:
            cp.start()
        for p in range(1, N_DEV):
            peer = me ^ p
            for k in range(n):
                pltpu.make_async_remote_copy(
                    src_ref=src_of(k, peer), dst_ref=outs[k].at[peer],
                    send_sem=send_sems.at[k, p - 1], recv_sem=recv_sems.at[k, p - 1],
                    device_id=_coords(peer), device_id_type=MESH).wait_recv()
        for cp in sends:
            cp.wait_send()
        for cp in local:
            cp.wait()

    any_spec = pl.BlockSpec(memory_space=pl.ANY)
    return pl.pallas_call(
        body, name=name,
        in_specs=[any_spec] * n, out_specs=[any_spec] * n, out_shape=out_shapes,
        scratch_shapes=[pltpu.SemaphoreType.DMA((n, N_DEV - 1)), pltpu.SemaphoreType.DMA((n, N_DEV - 1)),
                        pltpu.SemaphoreType.DMA((n,))],
        compiler_params=pltpu.CompilerParams(has_side_effects=True),
    )(*arrays)


def _reduce_adamw(parts, w, m, v, name):
    r, c = w.shape
    tr = _pick(r, (256, 128, 64, 32, 16))
    c1 = 1.0 - ADAM_B1
    c2 = 1.0 - ADAM_B2
    bc1 = 1.0 - ADAM_B1 ** ADAM_STEP
    bc2 = 1.0 - ADAM_B2 ** ADAM_STEP

    def body(p_ref, w_ref, m_ref, v_ref, g_ref, d_ref, nm_ref, nv_ref):
        g = p_ref[0].astype(F32)
        for j in range(1, N_DEV):
            g = g + p_ref[j].astype(F32)
        nm = ADAM_B1 * m_ref[...] + c1 * g
        nv = ADAM_B2 * v_ref[...] + c2 * (g * g)
        g_ref[...] = g
        nm_ref[...] = nm
        nv_ref[...] = nv
        d_ref[...] = -ADAM_LR * ((nm / bc1) / (jnp.sqrt(nv / bc2) + ADAM_EPS) + ADAM_WD * w_ref[...])

    blk = pl.BlockSpec((tr, c), lambda i: (i, 0))
    out = jax.ShapeDtypeStruct((r, c), F32)
    return pl.pallas_call(
        body, name=name, grid=(r // tr,),
        in_specs=[pl.BlockSpec((N_DEV, tr, c), lambda i: (0, i, 0)), blk, blk, blk],
        out_specs=[blk] * 4, out_shape=[out] * 4,
        compiler_params=_params(("parallel",)),
    )(parts, w, m, v)


_BIG = ("w_in", "w_uq", "w_ukv", "conv_w", "w_out", "w_gate", "w_up", "w_down")
_SMALL = ("pre_mix_norm", "q_norm", "kv_norm", "conv_b", "conv_ln_g", "conv_ln_b", "conv_out_norm",
          "attn_out_norm", "post_mix_norm", "pre_ffn_norm", "post_ffn_norm")
_ORDER = ("pre_mix_norm", "w_in", "q_norm", "w_uq", "kv_norm", "w_ukv", "conv_w", "conv_b", "conv_ln_g",
          "conv_ln_b", "conv_out_norm", "attn_out_norm", "w_out", "post_mix_norm", "pre_ffn_norm", "w_gate",
          "w_up", "w_down", "post_ffn_norm")


def _cols_from_shards(g):
    return jnp.transpose(g, (1, 0, 2)).reshape(g.shape[1], N_DEV * g.shape[2])


def _cols_to_shards(w):
    k, n8 = w.shape
    return jnp.transpose(w.reshape(k, N_DEV, n8 // N_DEV), (1, 0, 2))


def _step(x, positions, loss_target, w, m, v):
    s, d = x.shape[1], x.shape[2]
    x2, tgt = x[0], loss_target[0]
    pos = positions.reshape(s, 1)
    vecs = {n: w[n] for n in _SMALL}

    shards = [w[n][0] if n == "conv_w" else w[n][0].astype(BF16) for n in _BIG]
    gath = dict(zip(_BIG, _exchange(shards, True, "gather_weights")))
    w_in_f = _cols_from_shards(gath["w_in"])
    zw = w_in_f.shape[1] + LANES - QK_ROPE
    w_in_p = jnp.pad(w_in_f, ((0, 0), (0, zw - w_in_f.shape[1])))
    w_uq_p = _cols_from_shards(jnp.pad(gath["w_uq"], ((0, 0), (0, 0), (0, HEAD_PAD - QK_HEAD))))
    w_ukv = _cols_from_shards(gath["w_ukv"])
    conv_w_f = _cols_from_shards(gath["conv_w"])
    w_out_f = gath["w_out"].reshape(-1, d)
    w_gate_f = _cols_from_shards(gath["w_gate"])
    w_up_f = _cols_from_shards(gath["w_up"])
    w_down_f = gath["w_down"].reshape(-1, d)

    loss, grad_x, g = _local_step(x2, pos, tgt, vecs, w_in_p, w_uq_p, w_ukv, conv_w_f, w_out_f, w_gate_f, w_up_f, w_down_f)

    blocks = {
        "w_in": _cols_to_shards(g["w_in_p"][:, :w_in_f.shape[1]]),
        "w_uq": _cols_to_shards(g["w_uq_p"])[:, :, :QK_HEAD],
        "w_ukv": _cols_to_shards(g["w_ukv"]),
        "conv_w": _cols_to_shards(g["conv_w"]),
        "w_out": g["w_out"].reshape(N_DEV, -1, d),
        "w_gate": _cols_to_shards(g["w_gate"]),
        "w_up": _cols_to_shards(g["w_up"]),
        "w_down": g["w_down"].reshape(N_DEV, -1, d),
    }
    recv = dict(zip(_BIG, _exchange([blocks[n] for n in _BIG], False, "scatter_grads")))
    small = jnp.concatenate([g[n] for n in _SMALL], axis=1)
    small_all = _exchange([small], True, "gather_small_grads")[0]

    res = {}
    for n in _BIG:
        res[n] = _reduce_adamw(recv[n], w[n][0], m[n][0], v[n][0], "adamw_" + n)
        res[n] = [t[None] for t in res[n]]
    cat = lambda t: jnp.concatenate([t[n] for n in _SMALL], axis=1)
    sg, sd, sm, sv = _reduce_adamw(small_all, cat(w), cat(m), cat(v), "adamw_small")
    off = 0
    for n in _SMALL:
        width = w[n].shape[1]
        res[n] = [t[:, off:off + width] for t in (sg, sd, sm, sv)]
        off += width

    total = lax.psum(loss[0, 0], ("x", "y", "c"))
    outs = [total, grad_x[None]]
    for part in range(4):
        outs.extend(res[n][part] for n in _ORDER)
    return tuple(outs)


def kernel(x, positions, pre_mix_norm, w_in, q_norm, w_uq, kv_norm, w_ukv, conv_w, conv_b, conv_ln_g, conv_ln_b, conv_out_norm, attn_out_norm, w_out, post_mix_norm, pre_ffn_norm, w_gate, w_up, w_down, post_ffn_norm, loss_target, m_pre_mix_norm, m_w_in, m_q_norm, m_w_uq, m_kv_norm, m_w_ukv, m_conv_w, m_conv_b, m_conv_ln_g, m_conv_ln_b, m_conv_out_norm, m_attn_out_norm, m_w_out, m_post_mix_norm, m_pre_ffn_norm, m_w_gate, m_w_up, m_w_down, m_post_ffn_norm, v_pre_mix_norm, v_w_in, v_q_norm, v_w_uq, v_kv_norm, v_w_ukv, v_conv_w, v_conv_b, v_conv_ln_g, v_conv_ln_b, v_conv_out_norm, v_attn_out_norm, v_w_out, v_post_mix_norm, v_pre_ffn_norm, v_w_gate, v_w_up, v_w_down, v_post_ffn_norm):
    w = dict(zip(_ORDER, (pre_mix_norm, w_in, q_norm, w_uq, kv_norm, w_ukv, conv_w, conv_b, conv_ln_g, conv_ln_b,
                          conv_out_norm, attn_out_norm, w_out, post_mix_norm, pre_ffn_norm, w_gate, w_up, w_down,
                          post_ffn_norm)))
    m = dict(zip(_ORDER, (m_pre_mix_norm, m_w_in, m_q_norm, m_w_uq, m_kv_norm, m_w_ukv, m_conv_w, m_conv_b,
                          m_conv_ln_g, m_conv_ln_b, m_conv_out_norm, m_attn_out_norm, m_w_out, m_post_mix_norm,
                          m_pre_ffn_norm, m_w_gate, m_w_up, m_w_down, m_post_ffn_norm)))
    v = dict(zip(_ORDER, (v_pre_mix_norm, v_w_in, v_q_norm, v_w_uq, v_kv_norm, v_w_ukv, v_conv_w, v_conv_b,
                          v_conv_ln_g, v_conv_ln_b, v_conv_out_norm, v_attn_out_norm, v_w_out, v_post_mix_norm,
                          v_pre_ffn_norm, v_w_gate, v_w_up, v_w_down, v_post_ffn_norm)))
    return _step(x, positions, loss_target, w, m, v)
```

```python
import functools

import jax
import jax.numpy as jnp
from jax import lax
from jax.experimental import pallas as pl
from jax.experimental.pallas import tpu as pltpu

N_DEV = 8
N_HEADS = 8
QK_NOPE = 128
QK_ROPE = 64
V_HEAD = 128
QK_HEAD = QK_NOPE + QK_ROPE
HEAD_PAD = 256
LANES = 128
CONV_K = 31
CONV_PAD = 32
EPS = 1e-6
ROPE_THETA = 10000.0
ADAM_LR = 0.001
ADAM_B1 = 0.9
ADAM_B2 = 0.999
ADAM_EPS = 1e-08
ADAM_WD = 0.01
ADAM_STEP = 10
VMEM_LIMIT = 56 * 1024 * 1024
F32 = jnp.float32
BF16 = jnp.bfloat16
MESH = pl.DeviceIdType.MESH
NEG = -1e30


def _pick(n, prefs):
    for p in prefs:
        if p <= n and n % p == 0:
            return p
    return n


def _params(sem):
    return pltpu.CompilerParams(dimension_semantics=sem, vmem_limit_bytes=VMEM_LIMIT)


_DIMS = {"nn": (((1,), (0,)), ((), ())), "nt": (((1,), (1,)), ((), ())), "tn": (((0,), (0,)), ((), ()))}


MM_VMEM_BUDGET = 40 * 1024 * 1024
MM_MAX_MACS = 3 * 1024 ** 3


V7X_HBM_BYTES_PER_S = 3.0e12
V7X_MXU_MACS_PER_S = 0.45e15
GRID_STEP_S = 0.35e-6


def _mm_tiles(m, n, k, size_a, size_b, size_o):
    best = None
    for tm in sorted({m, 1024, 512, 256, 128}, reverse=True):
        if tm > m or m % tm:
            continue
        for tn in sorted({n, 2048, 1024, 512, 384, 256, 128}, reverse=True):
            if tn > n or n % tn:
                continue
            vmem = 2 * (tm * k * size_a + k * tn * size_b + tm * tn * size_o)
            if vmem > MM_VMEM_BUDGET or tm * tn * k > MM_MAX_MACS:
                continue
            b_reads = 1 if tn == n else m // tm
            traffic = m * k * size_a + b_reads * k * n * size_b + m * n * size_o
            exposed = tm * k * size_a + k * tn * size_b + tm * tn * size_o
            steps = (m // tm) * (n // tn)
            key = (max(traffic / V7X_HBM_BYTES_PER_S, m * n * k / V7X_MXU_MACS_PER_S)
                   + exposed / V7X_HBM_BYTES_PER_S + steps * GRID_STEP_S)
            if best is None or key < best[0]:
                best = (key, tm, tn)
    assert best is not None, (m, n, k)
    return best[1], best[2]


def _mm(a, b, mode, name, out_dtype=F32, add=None):
    if mode == "nn":
        (m, k), (k2, n) = a.shape, b.shape
    elif mode == "nt":
        (m, k), (n, k2) = a.shape, b.shape
    else:
        (k, m), (k2, n) = a.shape, b.shape
    assert k == k2, (a.shape, b.shape, mode)
    tm, tn = _mm_tiles(m, n, k, a.dtype.itemsize, b.dtype.itemsize,
                       jnp.dtype(out_dtype).itemsize + (0 if add is None else 4))
    dims = _DIMS[mode]

    def body(a_ref, b_ref, *rest):
        acc = lax.dot_general(a_ref[...].astype(BF16), b_ref[...].astype(BF16), dims, preferred_element_type=F32)
        if add is not None:
            acc = acc + rest[0][...]
        rest[-1][...] = acc.astype(rest[-1].dtype)

    if mode == "tn":
        a_spec = pl.BlockSpec((k, tm), lambda i, j: (0, i))
    else:
        a_spec = pl.BlockSpec((tm, k), lambda i, j: (i, 0))
    if mode == "nt":
        b_spec = pl.BlockSpec((tn, k), lambda i, j: (j, 0))
    else:
        b_spec = pl.BlockSpec((k, tn), lambda i, j: (0, j))
    o_spec = pl.BlockSpec((tm, tn), lambda i, j: (i, j))
    extra = [] if add is None else [add]
    return pl.pallas_call(
        body, name=name,
        grid=(m // tm, n // tn),
        in_specs=[a_spec, b_spec] + [o_spec] * len(extra),
        out_specs=o_spec,
        out_shape=jax.ShapeDtypeStruct((m, n), out_dtype),
        compiler_params=_params(("parallel", "parallel")),
    )(a, b, *extra)


def _sigmoid(x):
    return 1.0 / (1.0 + jnp.exp(-x))


def _rms(x, g):
    r = lax.rsqrt(jnp.mean(x * x, axis=-1, keepdims=True) + EPS)
    return (x * r) * g


def _rms_bwd(x, g, dy):
    r = lax.rsqrt(jnp.mean(x * x, axis=-1, keepdims=True) + EPS)
    xh = x * r
    dyg = dy * g
    dx = r * (dyg - xh * jnp.mean(dyg * xh, axis=-1, keepdims=True))
    return dx, jnp.sum(dy * xh, axis=0, keepdims=True)


def _ln(x, g, b):
    mu = jnp.mean(x, axis=-1, keepdims=True)
    xc = x - mu
    rs = lax.rsqrt(jnp.mean(xc * xc, axis=-1, keepdims=True) + EPS)
    return (xc * rs) * g + b


def _ln_bwd(x, g, dy):
    mu = jnp.mean(x, axis=-1, keepdims=True)
    xc = x - mu
    rs = lax.rsqrt(jnp.mean(xc * xc, axis=-1, keepdims=True) + EPS)
    xh = xc * rs
    dyg = dy * g
    dx = rs * (dyg - jnp.mean(dyg, axis=-1, keepdims=True) - xh * jnp.mean(dyg * xh, axis=-1, keepdims=True))
    return dx, jnp.sum(dy * xh, axis=0, keepdims=True), jnp.sum(dy, axis=0, keepdims=True)


def _silu(x):
    return x * _sigmoid(x)


def _silu_grad(x):
    s = _sigmoid(x)
    return s * (1.0 + x * (1.0 - s))


def _rope(x, cos, sa, sb):
    return x * cos + pltpu.roll(x, 96, 1) * sa + pltpu.roll(x, 32, 1) * sb


def _rope_t(d, cos, sa, sb):
    return d * cos - pltpu.roll(d, 96, 1) * sa - pltpu.roll(d, 32, 1) * sb


def _rows(ts, w):
    return pl.BlockSpec((ts, w), lambda i: (i, 0))


def _vec(w):
    return pl.BlockSpec((1, w), lambda i: (0, 0))


def _acc_init(i, *refs):
    @pl.when(i == 0)
    def _():
        for r in refs:
            r[...] = jnp.zeros_like(r)


def _rope_tables(pos, inv_freq):
    s = pos.shape[0]
    ts = _pick(s, (512, 256, 128))

    def body(p_ref, f_ref, c_ref, sa_ref, sb_ref):
        ang = p_ref[...].astype(F32) * f_ref[...]
        lane = lax.broadcasted_iota(jnp.int32, ang.shape, 1)
        c, sn = jnp.cos(ang), jnp.sin(ang)
        c_ref[...] = jnp.where(lane < QK_ROPE, c, 0.0)
        sa_ref[...] = jnp.where(lane < QK_ROPE // 2, -sn, 0.0)
        sb_ref[...] = jnp.where((lane >= QK_ROPE // 2) & (lane < QK_ROPE), sn, 0.0)

    out = jax.ShapeDtypeStruct((s, LANES), F32)
    return pl.pallas_call(
        body, name="rope_tables", grid=(s // ts,),
        in_specs=[_rows(ts, 1), _vec(LANES)],
        out_specs=[_rows(ts, LANES)] * 3, out_shape=[out] * 3,
        compiler_params=_params(("parallel",)),
    )(pos, inv_freq)


def _pre_fwd(x, g):
    s, d = x.shape
    ts = _pick(s, (256, 128))

    def body(x_ref, g_ref, h_ref):
        h_ref[...] = _rms(x_ref[...], g_ref[...]).astype(BF16)

    return pl.pallas_call(
        body, name="pre_fwd", grid=(s // ts,),
        in_specs=[_rows(ts, d), _vec(d)], out_specs=_rows(ts, d),
        out_shape=jax.ShapeDtypeStruct((s, d), BF16),
        compiler_params=_params(("parallel",)),
    )(x, g)


def _split_fwd(z, gq, gkv, tabs, c, ql, kvl):
    s, zw = z.shape
    ts = _pick(s, (256, 128))
    o_q, o_kv, o_kr = 2 * c, 2 * c + ql, 2 * c + ql + kvl

    def body(z_ref, gq_ref, gkv_ref, c_ref, sa_ref, sb_ref, u0_ref, qn_ref, kvn_ref, kpe_ref):
        u0_ref[...] = z_ref[:, 0:c] * _sigmoid(z_ref[:, c:2 * c])
        qn_ref[...] = _rms(z_ref[:, o_q:o_kv], gq_ref[...]).astype(BF16)
        kvn_ref[...] = _rms(z_ref[:, o_kv:o_kr], gkv_ref[...]).astype(BF16)
        kpe_ref[...] = _rope(z_ref[:, o_kr:o_kr + LANES], c_ref[...], sa_ref[...], sb_ref[...]).astype(BF16)

    return pl.pallas_call(
        body, name="split_fwd", grid=(s // ts,),
        in_specs=[_rows(ts, zw), _vec(ql), _vec(kvl)] + [_rows(ts, LANES)] * 3,
        out_specs=[_rows(ts, c), _rows(ts, ql), _rows(ts, kvl), _rows(ts, LANES)],
        out_shape=[jax.ShapeDtypeStruct((s, c), F32), jax.ShapeDtypeStruct((s, ql), BF16),
                   jax.ShapeDtypeStruct((s, kvl), BF16), jax.ShapeDtypeStruct((s, LANES), BF16)],
        compiler_params=_params(("parallel",)),
    )(z, gq, gkv, *tabs)


def _split_bwd(du0, z, dqn, dkvn, dkpe_h, gq, gkv, tabs, c, ql, kvl):
    s, zw = z.shape
    ts = _pick(s, (256, 128))
    o_q, o_kv, o_kr = 2 * c, 2 * c + ql, 2 * c + ql + kvl

    def body(du0_ref, z_ref, dqn_ref, dkvn_ref, dkh_ref, gq_ref, gkv_ref, c_ref, sa_ref, sb_ref,
             dz_ref, dgq_ref, dgkv_ref):
        _acc_init(pl.program_id(0), dgq_ref, dgkv_ref)
        du0 = du0_ref[...]
        a = z_ref[:, 0:c]
        sg = _sigmoid(z_ref[:, c:2 * c])
        dz_ref[:, 0:c] = (du0 * sg).astype(BF16)
        dz_ref[:, c:2 * c] = (du0 * a * sg * (1.0 - sg)).astype(BF16)
        dq, dgq = _rms_bwd(z_ref[:, o_q:o_kv], gq_ref[...], dqn_ref[...])
        dz_ref[:, o_q:o_kv] = dq.astype(BF16)
        dgq_ref[...] += dgq
        dkv, dgkv = _rms_bwd(z_ref[:, o_kv:o_kr], gkv_ref[...], dkvn_ref[...])
        dz_ref[:, o_kv:o_kr] = dkv.astype(BF16)
        dgkv_ref[...] += dgkv
        dk = dkh_ref[:, 0:LANES]
        for h in range(1, N_HEADS):
            dk = dk + dkh_ref[:, h * LANES:(h + 1) * LANES]
        dz_ref[:, o_kr:o_kr + LANES] = _rope_t(dk, c_ref[...], sa_ref[...], sb_ref[...]).astype(BF16)

    return pl.pallas_call(
        body, name="split_bwd", grid=(s // ts,),
        in_specs=[_rows(ts, c), _rows(ts, zw), _rows(ts, ql), _rows(ts, kvl), _rows(ts, N_HEADS * LANES),
                  _vec(ql), _vec(kvl)] + [_rows(ts, LANES)] * 3,
        out_specs=[_rows(ts, zw), _vec(ql), _vec(kvl)],
        out_shape=[jax.ShapeDtypeStruct((s, zw), BF16), jax.ShapeDtypeStruct((1, ql), F32),
                   jax.ShapeDtypeStruct((1, kvl), F32)],
        compiler_params=_params(("arbitrary",)),
    )(du0, z, dqn, dkvn, dkpe_h, gq, gkv, *tabs)


def _q_rope(qpre, tabs, transpose, out_dtype, name):
    s, w = qpre.shape
    ts = _pick(s, (256, 128))
    rot = _rope_t if transpose else _rope

    def body(q_ref, c_ref, sa_ref, sb_ref, o_ref):
        cs, sa, sb = c_ref[...], sa_ref[...], sb_ref[...]
        for h in range(N_HEADS):
            lo = h * HEAD_PAD
            o_ref[:, lo:lo + QK_NOPE] = q_ref[:, lo:lo + QK_NOPE].astype(out_dtype)
            o_ref[:, lo + QK_NOPE:lo + HEAD_PAD] = rot(q_ref[:, lo + QK_NOPE:lo + HEAD_PAD], cs, sa, sb).astype(out_dtype)

    return pl.pallas_call(
        body, name=name, grid=(s // ts,),
        in_specs=[_rows(ts, w)] + [_rows(ts, LANES)] * 3, out_specs=_rows(ts, w),
        out_shape=jax.ShapeDtypeStruct((s, w), out_dtype),
        compiler_params=_params(("parallel",)),
    )(qpre, *tabs)


def _conv_fwd(u0, w, b):
    s, c = u0.shape
    tc = LANES
    rc = _pick(s, (256, 128))

    def body(u_ref, w_ref, b_ref, o_ref, pad_ref):
        pad_ref[0:CONV_PAD, :] = jnp.zeros((CONV_PAD, tc), F32)
        pad_ref[CONV_PAD:CONV_PAD + s, :] = u_ref[...]
        for r in range(s // rc):
            acc = jnp.broadcast_to(b_ref[...], (rc, tc))
            for k in range(CONV_K):
                lo = r * rc + CONV_PAD - (CONV_K - 1) + k
                acc = acc + w_ref[k:k + 1, :] * pad_ref[lo:lo + rc, :]
            o_ref[r * rc:(r + 1) * rc, :] = acc

    col = lambda j: (0, j)
    return pl.pallas_call(
        body, name="conv_fwd", grid=(c // tc,),
        in_specs=[pl.BlockSpec((s, tc), col), pl.BlockSpec((CONV_K, tc), col), pl.BlockSpec((1, tc), col)],
        out_specs=pl.BlockSpec((s, tc), col),
        out_shape=jax.ShapeDtypeStruct((s, c), F32),
        scratch_shapes=[pltpu.VMEM((s + CONV_PAD, tc), F32)],
        compiler_params=_params(("parallel",)),
    )(u0, w, b)


def _conv_bwd(du1, u0, w):
    s, c = u0.shape
    tc = LANES
    rc = _pick(s, (256, 128))

    def body(d_ref, u_ref, w_ref, du_ref, dw_ref, db_ref, upad_ref, dpad_ref):
        upad_ref[0:CONV_PAD, :] = jnp.zeros((CONV_PAD, tc), F32)
        upad_ref[CONV_PAD:CONV_PAD + s, :] = u_ref[...]
        dpad_ref[0:s, :] = d_ref[...]
        dpad_ref[s:s + CONV_PAD, :] = jnp.zeros((CONV_PAD, tc), F32)
        for r in range(s // rc):
            acc = jnp.zeros((rc, tc), F32)
            for k in range(CONV_K):
                lo = r * rc + (CONV_K - 1) - k
                acc = acc + w_ref[k:k + 1, :] * dpad_ref[lo:lo + rc, :]
            du_ref[r * rc:(r + 1) * rc, :] = acc
        for k in range(CONV_K):
            acc8 = jnp.zeros((8, tc), F32)
            for r in range(s // rc):
                lo = r * rc + CONV_PAD - (CONV_K - 1) + k
                prod = d_ref[r * rc:(r + 1) * rc, :] * upad_ref[lo:lo + rc, :]
                acc8 = acc8 + jnp.sum(prod.reshape(rc // 8, 8, tc), axis=0)
            dw_ref[k:k + 1, :] = jnp.sum(acc8, axis=0, keepdims=True)
        db_ref[...] = jnp.sum(d_ref[...], axis=0, keepdims=True)

    col = lambda j: (0, j)
    return pl.pallas_call(
        body, name="conv_bwd", grid=(c // tc,),
        in_specs=[pl.BlockSpec((s, tc), col), pl.BlockSpec((s, tc), col), pl.BlockSpec((CONV_K, tc), col)],
        out_specs=[pl.BlockSpec((s, tc), col), pl.BlockSpec((CONV_K, tc), col), pl.BlockSpec((1, tc), col)],
        out_shape=[jax.ShapeDtypeStruct((s, c), F32), jax.ShapeDtypeStruct((CONV_K, c), F32),
                   jax.ShapeDtypeStruct((1, c), F32)],
        scratch_shapes=[pltpu.VMEM((s + CONV_PAD, tc), F32), pltpu.VMEM((s + CONV_PAD, tc), F32)],
        compiler_params=_params(("parallel",)),
    )(du1, u0, w)


def _causal_mask(sc, qi, kj, tq, tk):
    rows = qi * tq + lax.broadcasted_iota(jnp.int32, sc.shape, 0)
    cols = kj * tk + lax.broadcasted_iota(jnp.int32, sc.shape, 1)
    return jnp.where(cols <= rows, sc, NEG)


def _attn_fwd(q, kv, kpe):
    s = q.shape[0]
    tq = tk = _pick(s, (256, 128))
    scale = QK_HEAD ** -0.5
    nt = (((1,), (1,)), ((), ()))

    def body(q_ref, kn_ref, v_ref, kpe_ref, o_ref, lse_ref, kf_ref, vb_ref):
        i = pl.program_id(1)

        @pl.when(i == 0)
        def _():
            kf_ref[:, 0:QK_NOPE] = kn_ref[...].astype(BF16)
            kf_ref[:, QK_NOPE:HEAD_PAD] = kpe_ref[...]
            vb_ref[...] = v_ref[...].astype(BF16)

        qb = q_ref[...]

        def step(j, carry):
            m, l, acc = carry
            off = pl.multiple_of(j * tk, tk)
            sc = lax.dot_general(qb, kf_ref[pl.ds(off, tk), :], nt, preferred_element_type=F32) * scale
            sc = _causal_mask(sc, i, j, tq, tk)
            m_new = jnp.maximum(m, jnp.max(sc, axis=1, keepdims=True))
            p = jnp.exp(sc - m_new)
            alpha = jnp.exp(m - m_new)
            l = alpha * l + jnp.sum(p, axis=1, keepdims=True)
            acc = alpha * acc + jnp.dot(p.astype(BF16), vb_ref[pl.ds(off, tk), :], preferred_element_type=F32)
            return m_new, l, acc

        init = (jnp.full((tq, 1), NEG, F32), jnp.zeros((tq, 1), F32), jnp.zeros((tq, V_HEAD), F32))
        m, l, acc = lax.fori_loop(0, i + 1, step, init)
        o_ref[...] = acc / l
        lse_ref[...] = jnp.broadcast_to(m + jnp.log(l), (tq, LANES))

    return pl.pallas_call(
        body, name="attn_fwd", grid=(N_HEADS, s // tq),
        in_specs=[pl.BlockSpec((tq, HEAD_PAD), lambda h, i: (i, h)),
                  pl.BlockSpec((s, QK_NOPE), lambda h, i: (0, 2 * h)),
                  pl.BlockSpec((s, V_HEAD), lambda h, i: (0, 2 * h + 1)),
                  pl.BlockSpec((s, LANES), lambda h, i: (0, 0))],
        out_specs=[pl.BlockSpec((tq, V_HEAD), lambda h, i: (i, h)),
                   pl.BlockSpec((tq, LANES), lambda h, i: (i, h))],
        out_shape=[jax.ShapeDtypeStruct((s, N_HEADS * V_HEAD), F32),
                   jax.ShapeDtypeStruct((s, N_HEADS * LANES), F32)],
        scratch_shapes=[pltpu.VMEM((s, HEAD_PAD), BF16), pltpu.VMEM((s, V_HEAD), BF16)],
        compiler_params=_params(("parallel", "arbitrary")),
    )(q, kv, kv, kpe)


def _attn_bwd(q, kv, kpe, o, do, lse):
    s = q.shape[0]
    tq = tk = _pick(s, (256, 128))
    nq = s // tq
    scale = QK_HEAD ** -0.5
    nt = (((1,), (1,)), ((), ()))
    tn = (((0,), (0,)), ((), ()))

    def body(q_ref, kn_ref, v_ref, kpe_ref, o_ref, do_ref, lse_ref, dq_ref, dkv_ref, dkpe_ref):
        j = pl.program_id(1)

        @pl.when(j == 0)
        def _():
            dq_ref[...] = jnp.zeros_like(dq_ref)

        kf = jnp.concatenate([kn_ref[...].astype(BF16), kpe_ref[...]], axis=1)
        vb = v_ref[...].astype(BF16)

        def step(i, carry):
            dk, dv = carry
            off = pl.multiple_of(i * tq, tq)
            qb = q_ref[pl.ds(off, tq), :]
            dob = do_ref[pl.ds(off, tq), :]
            delta = jnp.sum(dob * o_ref[pl.ds(off, tq), :], axis=1, keepdims=True)
            lse_b = lse_ref[pl.ds(off, tq), :][:, 0:1]
            sc = lax.dot_general(qb, kf, nt, preferred_element_type=F32) * scale
            sc = _causal_mask(sc, i, j, tq, tk)
            p = jnp.exp(sc - lse_b)
            dob16 = dob.astype(BF16)
            dv = dv + lax.dot_general(p.astype(BF16), dob16, tn, preferred_element_type=F32)
            dp = lax.dot_general(dob16, vb, nt, preferred_element_type=F32)
            ds = (p * (dp - delta) * scale).astype(BF16)
            dq_ref[pl.ds(off, tq), :] += jnp.dot(ds, kf, preferred_element_type=F32)
            dk = dk + lax.dot_general(ds, qb, tn, preferred_element_type=F32)
            return dk, dv

        dk, dv = lax.fori_loop(j, nq, step, (jnp.zeros((tk, HEAD_PAD), F32), jnp.zeros((tk, V_HEAD), F32)))
        dkv_ref[:, 0:QK_NOPE] = dk[:, 0:QK_NOPE]
        dkv_ref[:, QK_NOPE:HEAD_PAD] = dv
        dkpe_ref[...] = dk[:, QK_NOPE:HEAD_PAD]

    head_rows = lambda w: pl.BlockSpec((s, w), lambda h, j: (0, h))
    return pl.pallas_call(
        body, name="attn_bwd", grid=(N_HEADS, s // tk),
        in_specs=[head_rows(HEAD_PAD),
                  pl.BlockSpec((tk, QK_NOPE), lambda h, j: (j, 2 * h)),
                  pl.BlockSpec((tk, V_HEAD), lambda h, j: (j, 2 * h + 1)),
                  pl.BlockSpec((tk, LANES), lambda h, j: (j, 0)),
                  head_rows(V_HEAD), head_rows(V_HEAD), head_rows(LANES)],
        out_specs=[head_rows(HEAD_PAD),
                   pl.BlockSpec((tk, HEAD_PAD), lambda h, j: (j, h)),
                   pl.BlockSpec((tk, LANES), lambda h, j: (j, h))],
        out_shape=[jax.ShapeDtypeStruct((s, N_HEADS * HEAD_PAD), F32),
                   jax.ShapeDtypeStruct((s, N_HEADS * HEAD_PAD), F32),
                   jax.ShapeDtypeStruct((s, N_HEADS * LANES), F32)],
        compiler_params=_params(("parallel", "arbitrary")),
    )(q, kv, kv, kpe, o, do, lse)


def _mix_fwd(u1, lng, lnb, gcon, attn, gattn):
    s, c = u1.shape
    ac = attn.shape[1]
    ts = _pick(s, (256, 128))

    def body(u_ref, lg_ref, lb_ref, gc_ref, a_ref, ga_ref, o_ref):
        t3 = _silu(_ln(u_ref[...], lg_ref[...], lb_ref[...]))
        o_ref[:, 0:c] = _rms(t3, gc_ref[...]).astype(BF16)
        o_ref[:, c:c + ac] = _rms(a_ref[...], ga_ref[...]).astype(BF16)

    return pl.pallas_call(
        body, name="mix_fwd", grid=(s // ts,),
        in_specs=[_rows(ts, c), _vec(c), _vec(c), _vec(c), _rows(ts, ac), _vec(ac)],
        out_specs=_rows(ts, c + ac),
        out_shape=jax.ShapeDtypeStruct((s, c + ac), BF16),
        compiler_params=_params(("parallel",)),
    )(u1, lng, lnb, gcon, attn, gattn)


def _mix_bwd(dmixin, u1, lng, lnb, gcon, attn, gattn):
    s, c = u1.shape
    ac = attn.shape[1]
    ts = _pick(s, (256, 128))

    def body(d_ref, u_ref, lg_ref, lb_ref, gc_ref, a_ref, ga_ref,
             du_ref, da_ref, dlg_ref, dlb_ref, dgc_ref, dga_ref):
        _acc_init(pl.program_id(0), dlg_ref, dlb_ref, dgc_ref, dga_ref)
        u = u_ref[...]
        t2 = _ln(u, lg_ref[...], lb_ref[...])
        dt3, dgc = _rms_bwd(_silu(t2), gc_ref[...], d_ref[:, 0:c])
        du, dlg, dlb = _ln_bwd(u, lg_ref[...], dt3 * _silu_grad(t2))
        du_ref[...] = du
        dlg_ref[...] += dlg
        dlb_ref[...] += dlb
        dgc_ref[...] += dgc
        da, dga = _rms_bwd(a_ref[...], ga_ref[...], d_ref[:, c:c + ac])
        da_ref[...] = da
        dga_ref[...] += dga

    return pl.pallas_call(
        body, name="mix_bwd", grid=(s // ts,),
        in_specs=[_rows(ts, c + ac), _rows(ts, c), _vec(c), _vec(c), _vec(c), _rows(ts, ac), _vec(ac)],
        out_specs=[_rows(ts, c), _rows(ts, ac), _vec(c), _vec(c), _vec(c), _vec(ac)],
        out_shape=[jax.ShapeDtypeStruct((s, c), F32), jax.ShapeDtypeStruct((s, ac), F32),
                   jax.ShapeDtypeStruct((1, c), F32), jax.ShapeDtypeStruct((1, c), F32),
                   jax.ShapeDtypeStruct((1, c), F32), jax.ShapeDtypeStruct((1, ac), F32)],
        compiler_params=_params(("arbitrary",)),
    )(dmixin, u1, lng, lnb, gcon, attn, gattn)


def _post_mix_fwd(x, mix, gpost, gpre):
    s, d = x.shape
    ts = _pick(s, (256, 128))

    def body(x_ref, m_ref, gp_ref, gf_ref, x1_ref, hf_ref):
        x1 = x_ref[...] + _rms(m_ref[...], gp_ref[...])
        x1_ref[...] = x1
        hf_ref[...] = _rms(x1, gf_ref[...]).astype(BF16)

    return pl.pallas_call(
        body, name="post_mix_fwd", grid=(s // ts,),
        in_specs=[_rows(ts, d), _rows(ts, d), _vec(d), _vec(d)],
        out_specs=[_rows(ts, d), _rows(ts, d)],
        out_shape=[jax.ShapeDtypeStruct((s, d), F32), jax.ShapeDtypeStruct((s, d), BF16)],
        compiler_params=_params(("parallel",)),
    )(x, mix, gpost, gpre)


def _post_mix_bwd(dy, dhf, x1, gpre, mix, gpost):
    s, d = x1.shape
    ts = _pick(s, (256, 128))

    def body(dy_ref, dh_ref, x1_ref, gf_ref, m_ref, gp_ref, dx1_ref, dm_ref, dgf_ref, dgp_ref):
        _acc_init(pl.program_id(0), dgf_ref, dgp_ref)
        dxa, dgf = _rms_bwd(x1_ref[...], gf_ref[...], dh_ref[...])
        dx1 = dy_ref[...] + dxa
        dx1_ref[...] = dx1
        dgf_ref[...] += dgf
        dm, dgp = _rms_bwd(m_ref[...], gp_ref[...], dx1)
        dm_ref[...] = dm.astype(BF16)
        dgp_ref[...] += dgp

    return pl.pallas_call(
        body, name="post_mix_bwd", grid=(s // ts,),
        in_specs=[_rows(ts, d), _rows(ts, d), _rows(ts, d), _vec(d), _rows(ts, d), _vec(d)],
        out_specs=[_rows(ts, d), _rows(ts, d), _vec(d), _vec(d)],
        out_shape=[jax.ShapeDtypeStruct((s, d), F32), jax.ShapeDtypeStruct((s, d), BF16),
                   jax.ShapeDtypeStruct((1, d), F32), jax.ShapeDtypeStruct((1, d), F32)],
        compiler_params=_params(("arbitrary",)),
    )(dy, dhf, x1, gpre, mix, gpost)


def _ffn_act(gate, up):
    s, f = gate.shape
    ts = _pick(s, (256, 128))
    tf = _pick(f, (512, 256, 128))

    def body(g_ref, u_ref, o_ref):
        o_ref[...] = (_silu(g_ref[...]) * u_ref[...]).astype(BF16)

    return pl.pallas_call(
        body, name="ffn_act", grid=(s // ts, f // tf),
        in_specs=[pl.BlockSpec((ts, tf), lambda i, j: (i, j))] * 2,
        out_specs=pl.BlockSpec((ts, tf), lambda i, j: (i, j)),
        out_shape=jax.ShapeDtypeStruct((s, f), BF16),
        compiler_params=_params(("parallel", "parallel")),
    )(gate, up)


def _ffn_act_bwd(dact, gate, up):
    s, f = gate.shape
    ts = _pick(s, (256, 128))
    tf = _pick(f, (512, 256, 128))
    nf = f // tf

    def body(d_ref, g_ref, u_ref, dg_ref, du_ref):
        d, g = d_ref[...], g_ref[...]
        dg_ref[...] = (d * u_ref[...] * _silu_grad(g)).astype(BF16)
        du_ref[...] = (d * _silu(g)).astype(BF16)

    blk = pl.BlockSpec((ts, tf), lambda i, j: (i, j))
    return pl.pallas_call(
        body, name="ffn_act_bwd", grid=(s // ts, nf),
        in_specs=[blk] * 3, out_specs=[blk] * 2,
        out_shape=[jax.ShapeDtypeStruct((s, f), BF16)] * 2,
        compiler_params=_params(("parallel", "parallel")),
    )(dact, gate, up)


def _final(ff, x1, tgt, g):
    s, d = x1.shape
    ts = _pick(s, (256, 128))

    def body(ff_ref, x1_ref, t_ref, g_ref, loss_ref, dy_ref, dff_ref, dg_ref):
        _acc_init(pl.program_id(0), loss_ref, dg_ref)
        ff_v = ff_ref[...]
        err = x1_ref[...] + _rms(ff_v, g_ref[...]) - t_ref[...]
        tok = jnp.mean(err * err, axis=-1, keepdims=True)
        loss_ref[...] += 0.5 * jnp.sum(tok, axis=0, keepdims=True)
        dy = err * (1.0 / d)
        dy_ref[...] = dy
        dff, dg = _rms_bwd(ff_v, g_ref[...], dy)
        dff_ref[...] = dff.astype(BF16)
        dg_ref[...] += dg

    return pl.pallas_call(
        body, name="final", grid=(s // ts,),
        in_specs=[_rows(ts, d), _rows(ts, d), _rows(ts, d), _vec(d)],
        out_specs=[_vec(LANES), _rows(ts, d), _rows(ts, d), _vec(d)],
        out_shape=[jax.ShapeDtypeStruct((1, LANES), F32), jax.ShapeDtypeStruct((s, d), F32),
                   jax.ShapeDtypeStruct((s, d), BF16), jax.ShapeDtypeStruct((1, d), F32)],
        compiler_params=_params(("arbitrary",)),
    )(ff, x1, tgt, g)


def _pre_bwd(dx1, dh, x, g):
    s, d = x.shape
    ts = _pick(s, (256, 128))

    def body(dx1_ref, dh_ref, x_ref, g_ref, dx_ref, dg_ref):
        _acc_init(pl.program_id(0), dg_ref)
        dxa, dg = _rms_bwd(x_ref[...], g_ref[...], dh_ref[...])
        dx_ref[...] = dx1_ref[...] + dxa
        dg_ref[...] += dg

    return pl.pallas_call(
        body, name="pre_bwd", grid=(s // ts,),
        in_specs=[_rows(ts, d), _rows(ts, d), _rows(ts, d), _vec(d)],
        out_specs=[_rows(ts, d), _vec(d)],
        out_shape=[jax.ShapeDtypeStruct((s, d), F32), jax.ShapeDtypeStruct((1, d), F32)],
        compiler_params=_params(("arbitrary",)),
    )(dx1, dh, x, g)


def _local_step(x, pos, tgt, vecs, w_in_p, w_uq_p, w_ukv, conv_w, w_out, w_gate, w_up, w_down):
    c = vecs["conv_b"].shape[1]
    ql = vecs["q_norm"].shape[1]
    kvl = vecs["kv_norm"].shape[1]
    half = jnp.arange(0, QK_ROPE, 2, dtype=F32)
    freq = ROPE_THETA ** (-half / QK_ROPE)
    inv_freq = jnp.concatenate([freq, freq, jnp.zeros((LANES - QK_ROPE,), F32)])[None, :]
    tabs = _rope_tables(pos, inv_freq)

    h = _pre_fwd(x, vecs["pre_mix_norm"])
    z = _mm(h, w_in_p, "nn", "mm_z")
    u0, qn, kvn, kpe = _split_fwd(z, vecs["q_norm"], vecs["kv_norm"], tabs, c, ql, kvl)
    u1 = _conv_fwd(u0, conv_w, vecs["conv_b"])
    q = _q_rope(_mm(qn, w_uq_p, "nn", "mm_q"), tabs, False, BF16, "q_rope")
    kv = _mm(kvn, w_ukv, "nn", "mm_kv")
    attn, lse = _attn_fwd(q, kv, kpe)
    mixin = _mix_fwd(u1, vecs["conv_ln_g"], vecs["conv_ln_b"], vecs["conv_out_norm"], attn, vecs["attn_out_norm"])
    mix = _mm(mixin, w_out, "nn", "mm_mix")
    x1, hf = _post_mix_fwd(x, mix, vecs["post_mix_norm"], vecs["pre_ffn_norm"])
    gate = _mm(hf, w_gate, "nn", "mm_gate")
    up = _mm(hf, w_up, "nn", "mm_up")
    act = _ffn_act(gate, up)
    ff = _mm(act, w_down, "nn", "mm_ff")
    loss, dy, dff, d_post_ffn = _final(ff, x1, tgt, vecs["post_ffn_norm"])

    g = {"post_ffn_norm": d_post_ffn}
    dact = _mm(dff, w_down, "nt", "mm_dact")
    g["w_down"] = _mm(act, dff, "tn", "mm_dw_down", BF16)
    dgate, dup = _ffn_act_bwd(dact, gate, up)
    dhf = _mm(dup, w_up, "nt", "mm_dhf_up", add=_mm(dgate, w_gate, "nt", "mm_dhf_gate"))
    g["w_gate"] = _mm(hf, dgate, "tn", "mm_dw_gate", BF16)
    g["w_up"] = _mm(hf, dup, "tn", "mm_dw_up", BF16)
    dx1, dmix, g["pre_ffn_norm"], g["post_mix_norm"] = _post_mix_bwd(
        dy, dhf, x1, vecs["pre_ffn_norm"], mix, vecs["post_mix_norm"])
    dmixin = _mm(dmix, w_out, "nt", "mm_dmixin")
    g["w_out"] = _mm(mixin, dmix, "tn", "mm_dw_out", BF16)
    du1, dattn, g["conv_ln_g"], g["conv_ln_b"], g["conv_out_norm"], g["attn_out_norm"] = _mix_bwd(
        dmixin, u1, vecs["conv_ln_g"], vecs["conv_ln_b"], vecs["conv_out_norm"], attn, vecs["attn_out_norm"])
    du0, g["conv_w"], g["conv_b"] = _conv_bwd(du1, u0, conv_w)
    dq, dkv, dkpe_h = _attn_bwd(q, kv, kpe, attn, dattn, lse)
    dqpre = _q_rope(dq, tabs, True, BF16, "q_rope_bwd")
    dqn = _mm(dqpre, w_uq_p, "nt", "mm_dqn")
    g["w_uq_p"] = _mm(qn, dqpre, "tn", "mm_dw_uq", BF16)
    dkvn = _mm(dkv, w_ukv, "nt", "mm_dkvn")
    g["w_ukv"] = _mm(kvn, dkv, "tn", "mm_dw_ukv", BF16)
    dz, g["q_norm"], g["kv_norm"] = _split_bwd(du0, z, dqn, dkvn, dkpe_h, vecs["q_norm"], vecs["kv_norm"], tabs, c, ql, kvl)
    dh = _mm(dz, w_in_p, "nt", "mm_dh")
    g["w_in_p"] = _mm(h, dz, "tn", "mm_dw_in", BF16)
    grad_x, g["pre_mix_norm"] = _pre_bwd(dx1, dh, x, vecs["pre_mix_norm"])
    return loss, grad_x, g


def _my_index():
    return 4 * lax.axis_index("x") + 2 * lax.axis_index("y") + lax.axis_index("c")


def _coords(idx):
    return ((idx >> 2) & 1, (idx >> 1) & 1, idx & 1)


def _place():
    x, y, c = lax.axis_index("x"), lax.axis_index("y"), lax.axis_index("c")
    return (x, y, c), (x, y, 1 - c), [(1 - x, y), (x, 1 - y), (1 - x, 1 - y)]


def _comm_call(body, name, arrays, out_shapes, n_sems):
    n = len(arrays)
    any_spec = pl.BlockSpec(memory_space=pl.ANY)
    return pl.pallas_call(
        body, name=name,
        in_specs=[any_spec] * n, out_specs=[any_spec] * len(out_shapes), out_shape=out_shapes,
        scratch_shapes=[pltpu.SemaphoreType.DMA((n, n_sems))] * 3,
        compiler_params=pltpu.CompilerParams(has_side_effects=True),
    )(*arrays)


def _gather_direct(arrays, name):
    n = len(arrays)

    def body(*refs):
        ins, outs = refs[:n], refs[n:2 * n]
        send_sems, recv_sems, local_sems = refs[2 * n:]
        me = _my_index()
        local = [pltpu.make_async_copy(ins[k], outs[k].at[me], local_sems.at[k, 0]) for k in range(n)]
        for cp in local:
            cp.start()

        def copy(k, p, slot):
            return pltpu.make_async_remote_copy(
                src_ref=ins[k], dst_ref=outs[k].at[slot], send_sem=send_sems.at[k, p - 1],
                recv_sem=recv_sems.at[k, p - 1], device_id=_coords(me ^ p), device_id_type=MESH)

        sends = [copy(k, p, me) for p in range(1, N_DEV) for k in range(n)]
        for cp in sends:
            cp.start()
        for p in range(1, N_DEV):
            for k in range(n):
                copy(k, p, me ^ p).wait_recv()
        for cp in sends:
            cp.wait_send()
        for cp in local:
            cp.wait()

    out_shapes = [jax.ShapeDtypeStruct((N_DEV,) + a.shape, a.dtype) for a in arrays]
    return _comm_call(body, name, arrays, out_shapes, N_DEV - 1)


def _gather_weights(shards):
    n = len(shards)

    def body(*refs):
        ins, outs = refs[:n], refs[n:2 * n]
        send_sems, recv_sems, local_sems = refs[2 * n:]
        (x, y, c), sib, chips = _place()
        me = 4 * x + 2 * y + c

        def slot(chip, core):
            return 4 * chip[0] + 2 * chip[1] + core

        def copy(k, i, block, to, src=None):
            rows = outs[k].at[block]
            return pltpu.make_async_remote_copy(
                src_ref=rows if src is None else src, dst_ref=rows, send_sem=send_sems.at[k, i],
                recv_sem=recv_sems.at[k, i], device_id=to, device_id_type=MESH)

        local = [pltpu.make_async_copy(ins[k], outs[k].at[me], local_sems.at[k, 0]) for k in range(n)]
        for cp in local:
            cp.start()
        sends = []
        for k in range(n):
            sends.append(copy(k, 0, me, sib, src=ins[k]))
            sends += [copy(k, 1 + j, me, (*chip, c), src=ins[k]) for j, chip in enumerate(chips)]
        for cp in sends:
            cp.start()
        for j, chip in enumerate(chips):
            for k in range(n):
                copy(k, 1 + j, slot(chip, c), (x, y, c)).wait_recv()
                fwd = copy(k, 4 + j, slot(chip, c), sib)
                fwd.start()
                sends.append(fwd)
        for k in range(n):
            copy(k, 0, slot((x, y), 1 - c), (x, y, c)).wait_recv()
            for j, chip in enumerate(chips):
                copy(k, 4 + j, slot(chip, 1 - c), (x, y, c)).wait_recv()
        for cp in sends:
            cp.wait_send()
        for cp in local:
            cp.wait()

    out_shapes = [jax.ShapeDtypeStruct((N_DEV,) + a.shape, a.dtype) for a in shards]
    return _comm_call(body, "gather_weights", shards, out_shapes, 7)


def _scatter_to_sibling(blocks):
    n = len(blocks)

    def body(*refs):
        ins, mine, theirs = refs[:n], refs[n:2 * n], refs[2 * n:3 * n]
        send_sems, recv_sems, local_sems = refs[3 * n:]
        (x, y, c), sib, _ = _place()
        local, sends = [], []
        for k in range(n):
            for q in range(4):
                local.append(pltpu.make_async_copy(ins[k].at[2 * q + c], mine[k].at[q], local_sems.at[k, q]))
                sends.append(pltpu.make_async_remote_copy(
                    src_ref=ins[k].at[2 * q + 1 - c], dst_ref=theirs[k].at[q], send_sem=send_sems.at[k, q],
                    recv_sem=recv_sems.at[k, q], device_id=sib, device_id_type=MESH))
        for cp in local + sends:
            cp.start()
        for cp in sends:
            cp.wait_recv()
        for cp in sends:
            cp.wait_send()
        for cp in local:
            cp.wait()

    out_shapes = [jax.ShapeDtypeStruct((4,) + a.shape[1:], a.dtype) for a in blocks] * 2
    res = _comm_call(body, "scatter_to_sibling", blocks, out_shapes, 4)
    return res[:n], res[n:]


def _scatter_to_chips(pairs):
    n = len(pairs)

    def body(*refs):
        ins, outs = refs[:n], refs[n:2 * n]
        send_sems, recv_sems, local_sems = refs[2 * n:]
        (x, y, c), _, chips = _place()
        my_chip = 2 * x + y
        local = [pltpu.make_async_copy(ins[k].at[my_chip], outs[k].at[my_chip], local_sems.at[k, 0]) for k in range(n)]
        sends, recvs = [], []
        for k in range(n):
            for j, chip in enumerate(chips):
                q = 2 * chip[0] + chip[1]
                sends.append(pltpu.make_async_remote_copy(
                    src_ref=ins[k].at[q], dst_ref=outs[k].at[my_chip], send_sem=send_sems.at[k, j],
                    recv_sem=recv_sems.at[k, j], device_id=(*chip, c), device_id_type=MESH))
                recvs.append(pltpu.make_async_remote_copy(
                    src_ref=ins[k].at[q], dst_ref=outs[k].at[q], send_sem=send_sems.at[k, j],
                    recv_sem=recv_sems.at[k, j], device_id=(*chip, c), device_id_type=MESH))
        for cp in local + sends:
            cp.start()
        for cp in recvs:
            cp.wait_recv()
        for cp in sends:
            cp.wait_send()
        for cp in local:
            cp.wait()

    out_shapes = [jax.ShapeDtypeStruct(a.shape, a.dtype) for a in pairs]
    return _comm_call(body, "scatter_to_chips", pairs, out_shapes, 3)


def _pair_sum(mine, theirs, name):
    q, r, c = mine.shape
    tr = _pick(r, (256, 128, 64, 32, 16))

    def body(a_ref, b_ref, o_ref):
        o_ref[...] = (a_ref[...].astype(F32) + b_ref[...].astype(F32)).astype(o_ref.dtype)

    blk = pl.BlockSpec((1, tr, c), lambda i, j: (i, j, 0))
    return pl.pallas_call(
        body, name=name, grid=(q, r // tr), in_specs=[blk, blk], out_specs=blk,
        out_shape=jax.ShapeDtypeStruct(mine.shape, mine.dtype),
        compiler_params=_params(("parallel", "parallel")),
    )(mine, theirs)


def _reduce_adamw(parts, w, m, v, name):
    r, c = w.shape
    n_parts = parts.shape[0]
    tr = _pick(r, (256, 128, 64, 32, 16))
    c1 = 1.0 - ADAM_B1
    c2 = 1.0 - ADAM_B2
    bc1 = 1.0 - ADAM_B1 ** ADAM_STEP
    bc2 = 1.0 - ADAM_B2 ** ADAM_STEP

    def body(p_ref, w_ref, m_ref, v_ref, g_ref, d_ref, nm_ref, nv_ref):
        g = p_ref[0].astype(F32)
        for j in range(1, n_parts):
            g = g + p_ref[j].astype(F32)
        nm = ADAM_B1 * m_ref[...] + c1 * g
        nv = ADAM_B2 * v_ref[...] + c2 * (g * g)
        g_ref[...] = g
        nm_ref[...] = nm
        nv_ref[...] = nv
        d_ref[...] = -ADAM_LR * ((nm / bc1) / (jnp.sqrt(nv / bc2) + ADAM_EPS) + ADAM_WD * w_ref[...])

    blk = pl.BlockSpec((tr, c), lambda i: (i, 0))
    out = jax.ShapeDtypeStruct((r, c), F32)
    return pl.pallas_call(
        body, name=name, grid=(r // tr,),
        in_specs=[pl.BlockSpec((n_parts, tr, c), lambda i: (0, i, 0)), blk, blk, blk],
        out_specs=[blk] * 4, out_shape=[out] * 4,
        compiler_params=_params(("parallel",)),
    )(parts, w, m, v)


_BIG = ("w_in", "w_uq", "w_ukv", "conv_w", "w_out", "w_gate", "w_up", "w_down")
_SMALL = ("pre_mix_norm", "q_norm", "kv_norm", "conv_b", "conv_ln_g", "conv_ln_b", "conv_out_norm",
          "attn_out_norm", "post_mix_norm", "pre_ffn_norm", "post_ffn_norm")
_ORDER = ("pre_mix_norm", "w_in", "q_norm", "w_uq", "kv_norm", "w_ukv", "conv_w", "conv_b", "conv_ln_g",
          "conv_ln_b", "conv_out_norm", "attn_out_norm", "w_out", "post_mix_norm", "pre_ffn_norm", "w_gate",
          "w_up", "w_down", "post_ffn_norm")


def _cols_from_shards(g):
    return jnp.transpose(g, (1, 0, 2)).reshape(g.shape[1], N_DEV * g.shape[2])


def _cols_to_shards(w):
    k, n8 = w.shape
    return jnp.transpose(w.reshape(k, N_DEV, n8 // N_DEV), (1, 0, 2))


def _step(x, positions, loss_target, w, m, v):
    s, d = x.shape[1], x.shape[2]
    x2, tgt = x[0], loss_target[0]
    pos = positions.reshape(s, 1)
    vecs = {n: w[n] for n in _SMALL}

    shards = [w[n][0] if n == "conv_w" else w[n][0].astype(BF16) for n in _BIG]
    gath = dict(zip(_BIG, _gather_weights(shards)))
    w_in_f = _cols_from_shards(gath["w_in"])
    zw = w_in_f.shape[1] + LANES - QK_ROPE
    w_in_p = jnp.pad(w_in_f, ((0, 0), (0, zw - w_in_f.shape[1])))
    w_uq_p = _cols_from_shards(jnp.pad(gath["w_uq"], ((0, 0), (0, 0), (0, HEAD_PAD - QK_HEAD))))
    w_ukv = _cols_from_shards(gath["w_ukv"])
    conv_w_f = _cols_from_shards(gath["conv_w"])
    w_out_f = gath["w_out"].reshape(-1, d)
    w_gate_f = _cols_from_shards(gath["w_gate"])
    w_up_f = _cols_from_shards(gath["w_up"])
    w_down_f = gath["w_down"].reshape(-1, d)

    loss, grad_x, g = _local_step(x2, pos, tgt, vecs, w_in_p, w_uq_p, w_ukv, conv_w_f, w_out_f, w_gate_f, w_up_f, w_down_f)

    blocks = {
        "w_in": _cols_to_shards(g["w_in_p"][:, :w_in_f.shape[1]]),
        "w_uq": _cols_to_shards(g["w_uq_p"])[:, :, :QK_HEAD],
        "w_ukv": _cols_to_shards(g["w_ukv"]),
        "conv_w": _cols_to_shards(g["conv_w"]),
        "w_out": g["w_out"].reshape(N_DEV, -1, d),
        "w_gate": _cols_to_shards(g["w_gate"]),
        "w_up": _cols_to_shards(g["w_up"]),
        "w_down": g["w_down"].reshape(N_DEV, -1, d),
    }
    mine, theirs = _scatter_to_sibling([blocks[n] for n in _BIG])
    pairs = [_pair_sum(a, b, "pair_sum_" + n) for n, a, b in zip(_BIG, mine, theirs)]
    recv = dict(zip(_BIG, _scatter_to_chips(pairs)))
    small = jnp.concatenate([g[n] for n in _SMALL], axis=1)
    small_all = _gather_direct([small], "gather_small_grads")[0]

    res = {}
    for n in _BIG:
        res[n] = _reduce_adamw(recv[n], w[n][0], m[n][0], v[n][0], "adamw_" + n)
        res[n] = [t[None] for t in res[n]]
    cat = lambda t: jnp.concatenate([t[n] for n in _SMALL], axis=1)
    sg, sd, sm, sv = _reduce_adamw(small_all, cat(w), cat(m), cat(v), "adamw_small")
    off = 0
    for n in _SMALL:
        width = w[n].shape[1]
        res[n] = [t[:, off:off + width] for t in (sg, sd, sm, sv)]
        off += width

    total = lax.psum(loss[0, 0], ("x", "y", "c"))
    outs = [total, grad_x[None]]
    for part in range(4):
        outs.extend(res[n][part] for n in _ORDER)
    return tuple(outs)


def kernel(x, positions, pre_mix_norm, w_in, q_norm, w_uq, kv_norm, w_ukv, conv_w, conv_b, conv_ln_g, conv_ln_b, conv_out_norm, attn_out_norm, w_out, post_mix_norm, pre_ffn_norm, w_gate, w_up, w_down, post_ffn_norm, loss_target, m_pre_mix_norm, m_w_in, m_q_norm, m_w_uq, m_kv_norm, m_w_ukv, m_conv_w, m_conv_b, m_conv_ln_g, m_conv_ln_b, m_conv_out_norm, m_attn_out_norm, m_w_out, m_post_mix_norm, m_pre_ffn_norm, m_w_gate, m_w_up, m_w_down, m_post_ffn_norm, v_pre_mix_norm, v_w_in, v_q_norm, v_w_uq, v_kv_norm, v_w_ukv, v_conv_w, v_conv_b, v_conv_ln_g, v_conv_ln_b, v_conv_out_norm, v_attn_out_norm, v_w_out, v_post_mix_norm, v_pre_ffn_norm, v_w_gate, v_w_up, v_w_down, v_post_ffn_norm):
    w = dict(zip(_ORDER, (pre_mix_norm, w_in, q_norm, w_uq, kv_norm, w_ukv, conv_w, conv_b, conv_ln_g, conv_ln_b,
                          conv_out_norm, attn_out_norm, w_out, post_mix_norm, pre_ffn_norm, w_gate, w_up, w_down,
                          post_ffn_norm)))
    m = dict(zip(_ORDER, (m_pre_mix_norm, m_w_in, m_q_norm, m_w_uq, m_kv_norm, m_w_ukv, m_conv_w, m_conv_b,
                          m_conv_ln_g, m_conv_ln_b, m_conv_out_norm, m_attn_out_norm, m_w_out, m_post_mix_norm,
                          m_pre_ffn_norm, m_w_gate, m_w_up, m_w_down, m_post_ffn_norm)))
    v = dict(zip(_ORDER, (v_pre_mix_norm, v_w_in, v_q_norm, v_w_uq, v_kv_norm, v_w_ukv, v_conv_w, v_conv_b,
                          v_conv_ln_g, v_conv_ln_b, v_conv_out_norm, v_attn_out_norm, v_w_out, v_post_mix_norm,
                          v_pre_ffn_norm, v_w_gate, v_w_up, v_w_down, v_post_ffn_norm)))
    return _step(x, positions, loss_target, w, m, v)
```

```python
import functools

import jax
import jax.numpy as jnp
from jax import lax
from jax.experimental import pallas as pl
from jax.experimental.pallas import tpu as pltpu

N_DEV = 8
N_HEADS = 8
QK_NOPE = 128
QK_ROPE = 64
V_HEAD = 128
QK_HEAD = QK_NOPE + QK_ROPE
HEAD_PAD = 256
LANES = 128
CONV_K = 31
CONV_PAD = 32
EPS = 1e-6
ROPE_THETA = 10000.0
ADAM_LR = 0.001
ADAM_B1 = 0.9
ADAM_B2 = 0.999
ADAM_EPS = 1e-08
ADAM_WD = 0.01
ADAM_STEP = 10
VMEM_LIMIT = 56 * 1024 * 1024
F32 = jnp.float32
BF16 = jnp.bfloat16
MESH = pl.DeviceIdType.MESH
NEG = -1e30


def _pick(n, prefs):
    for p in prefs:
        if p <= n and n % p == 0:
            return p
    return n


def _params(sem):
    return pltpu.CompilerParams(dimension_semantics=sem, vmem_limit_bytes=VMEM_LIMIT)


_DIMS = {"nn": (((1,), (0,)), ((), ())), "nt": (((1,), (1,)), ((), ())), "tn": (((0,), (0,)), ((), ()))}


MM_VMEM_BUDGET = 40 * 1024 * 1024
MM_MAX_MACS = 3 * 1024 ** 3


V7X_HBM_BYTES_PER_S = 3.0e12
V7X_MXU_MACS_PER_S = 0.45e15
GRID_STEP_S = 0.35e-6


def _mm_tiles(m, n, k, size_a, size_b, size_o):
    best = None
    for tm in sorted({m, 1024, 512, 256, 128}, reverse=True):
        if tm > m or m % tm:
            continue
        for tn in sorted({n, 2048, 1024, 512, 384, 256, 128}, reverse=True):
            if tn > n or n % tn:
                continue
            vmem = 2 * (tm * k * size_a + k * tn * size_b + tm * tn * size_o)
            if vmem > MM_VMEM_BUDGET or tm * tn * k > MM_MAX_MACS:
                continue
            b_reads = 1 if tn == n else m // tm
            traffic = m * k * size_a + b_reads * k * n * size_b + m * n * size_o
            exposed = tm * k * size_a + k * tn * size_b + tm * tn * size_o
            steps = (m // tm) * (n // tn)
            key = (max(traffic / V7X_HBM_BYTES_PER_S, m * n * k / V7X_MXU_MACS_PER_S)
                   + exposed / V7X_HBM_BYTES_PER_S + steps * GRID_STEP_S)
            if best is None or key < best[0]:
                best = (key, tm, tn)
    assert best is not None, (m, n, k)
    return best[1], best[2]


def _mm(a, b, mode, name, out_dtype=F32, add=None):
    if mode == "nn":
        (m, k), (k2, n) = a.shape, b.shape
    elif mode == "nt":
        (m, k), (n, k2) = a.shape, b.shape
    else:
        (k, m), (k2, n) = a.shape, b.shape
    assert k == k2, (a.shape, b.shape, mode)
    tm, tn = _mm_tiles(m, n, k, a.dtype.itemsize, b.dtype.itemsize,
                       jnp.dtype(out_dtype).itemsize + (0 if add is None else 4))
    dims = _DIMS[mode]

    def body(a_ref, b_ref, *rest):
        acc = lax.dot_general(a_ref[...].astype(BF16), b_ref[...].astype(BF16), dims, preferred_element_type=F32)
        if add is not None:
            acc = acc + rest[0][...]
        rest[-1][...] = acc.astype(rest[-1].dtype)

    if mode == "tn":
        a_spec = pl.BlockSpec((k, tm), lambda i, j: (0, i))
    else:
        a_spec = pl.BlockSpec((tm, k), lambda i, j: (i, 0))
    if mode == "nt":
        b_spec = pl.BlockSpec((tn, k), lambda i, j: (j, 0))
    else:
        b_spec = pl.BlockSpec((k, tn), lambda i, j: (0, j))
    o_spec = pl.BlockSpec((tm, tn), lambda i, j: (i, j))
    extra = [] if add is None else [add]
    return pl.pallas_call(
        body, name=name,
        grid=(m // tm, n // tn),
        in_specs=[a_spec, b_spec] + [o_spec] * len(extra),
        out_specs=o_spec,
        out_shape=jax.ShapeDtypeStruct((m, n), out_dtype),
        compiler_params=_params(("parallel", "parallel")),
    )(a, b, *extra)


def _sigmoid(x):
    return 1.0 / (1.0 + jnp.exp(-x))


def _rms(x, g):
    r = lax.rsqrt(jnp.mean(x * x, axis=-1, keepdims=True) + EPS)
    return (x * r) * g


def _rms_bwd(x, g, dy):
    r = lax.rsqrt(jnp.mean(x * x, axis=-1, keepdims=True) + EPS)
    xh = x * r
    dyg = dy * g
    dx = r * (dyg - xh * jnp.mean(dyg * xh, axis=-1, keepdims=True))
    return dx, jnp.sum(dy * xh, axis=0, keepdims=True)


def _ln(x, g, b):
    mu = jnp.mean(x, axis=-1, keepdims=True)
    xc = x - mu
    rs = lax.rsqrt(jnp.mean(xc * xc, axis=-1, keepdims=True) + EPS)
    return (xc * rs) * g + b


def _ln_bwd(x, g, dy):
    mu = jnp.mean(x, axis=-1, keepdims=True)
    xc = x - mu
    rs = lax.rsqrt(jnp.mean(xc * xc, axis=-1, keepdims=True) + EPS)
    xh = xc * rs
    dyg = dy * g
    dx = rs * (dyg - jnp.mean(dyg, axis=-1, keepdims=True) - xh * jnp.mean(dyg * xh, axis=-1, keepdims=True))
    return dx, jnp.sum(dy * xh, axis=0, keepdims=True), jnp.sum(dy, axis=0, keepdims=True)


def _silu(x):
    return x * _sigmoid(x)


def _silu_grad(x):
    s = _sigmoid(x)
    return s * (1.0 + x * (1.0 - s))


def _rope(x, cos, sa, sb):
    return x * cos + pltpu.roll(x, 96, 1) * sa + pltpu.roll(x, 32, 1) * sb


def _rope_t(d, cos, sa, sb):
    return d * cos - pltpu.roll(d, 96, 1) * sa - pltpu.roll(d, 32, 1) * sb


def _rows(ts, w):
    return pl.BlockSpec((ts, w), lambda i: (i, 0))


def _vec(w):
    return pl.BlockSpec((1, w), lambda i: (0, 0))


def _acc_init(i, *refs):
    @pl.when(i == 0)
    def _():
        for r in refs:
            r[...] = jnp.zeros_like(r)


def _rope_tables(pos, inv_freq):
    s = pos.shape[0]
    ts = _pick(s, (512, 256, 128))

    def body(p_ref, f_ref, c_ref, sa_ref, sb_ref):
        ang = p_ref[...].astype(F32) * f_ref[...]
        lane = lax.broadcasted_iota(jnp.int32, ang.shape, 1)
        c, sn = jnp.cos(ang), jnp.sin(ang)
        c_ref[...] = jnp.where(lane < QK_ROPE, c, 0.0)
        sa_ref[...] = jnp.where(lane < QK_ROPE // 2, -sn, 0.0)
        sb_ref[...] = jnp.where((lane >= QK_ROPE // 2) & (lane < QK_ROPE), sn, 0.0)

    out = jax.ShapeDtypeStruct((s, LANES), F32)
    return pl.pallas_call(
        body, name="rope_tables", grid=(s // ts,),
        in_specs=[_rows(ts, 1), _vec(LANES)],
        out_specs=[_rows(ts, LANES)] * 3, out_shape=[out] * 3,
        compiler_params=_params(("parallel",)),
    )(pos, inv_freq)


def _pre_fwd(x, g):
    s, d = x.shape
    ts = _pick(s, (256, 128))

    def body(x_ref, g_ref, h_ref):
        h_ref[...] = _rms(x_ref[...], g_ref[...]).astype(BF16)

    return pl.pallas_call(
        body, name="pre_fwd", grid=(s // ts,),
        in_specs=[_rows(ts, d), _vec(d)], out_specs=_rows(ts, d),
        out_shape=jax.ShapeDtypeStruct((s, d), BF16),
        compiler_params=_params(("parallel",)),
    )(x, g)


def _split_fwd(z, gq, gkv, tabs, c, ql, kvl):
    s, zw = z.shape
    ts = _pick(s, (256, 128))
    o_q, o_kv, o_kr = 2 * c, 2 * c + ql, 2 * c + ql + kvl

    def body(z_ref, gq_ref, gkv_ref, c_ref, sa_ref, sb_ref, u0_ref, qn_ref, kvn_ref, kpe_ref):
        u0_ref[...] = z_ref[:, 0:c] * _sigmoid(z_ref[:, c:2 * c])
        qn_ref[...] = _rms(z_ref[:, o_q:o_kv], gq_ref[...]).astype(BF16)
        kvn_ref[...] = _rms(z_ref[:, o_kv:o_kr], gkv_ref[...]).astype(BF16)
        kpe_ref[...] = _rope(z_ref[:, o_kr:o_kr + LANES], c_ref[...], sa_ref[...], sb_ref[...]).astype(BF16)

    return pl.pallas_call(
        body, name="split_fwd", grid=(s // ts,),
        in_specs=[_rows(ts, zw), _vec(ql), _vec(kvl)] + [_rows(ts, LANES)] * 3,
        out_specs=[_rows(ts, c), _rows(ts, ql), _rows(ts, kvl), _rows(ts, LANES)],
        out_shape=[jax.ShapeDtypeStruct((s, c), F32), jax.ShapeDtypeStruct((s, ql), BF16),
                   jax.ShapeDtypeStruct((s, kvl), BF16), jax.ShapeDtypeStruct((s, LANES), BF16)],
        compiler_params=_params(("parallel",)),
    )(z, gq, gkv, *tabs)


def _split_bwd(du0, z, dqn, dkvn, dkpe_h, gq, gkv, tabs, c, ql, kvl):
    s, zw = z.shape
    ts = _pick(s, (256, 128))
    o_q, o_kv, o_kr = 2 * c, 2 * c + ql, 2 * c + ql + kvl

    def body(du0_ref, z_ref, dqn_ref, dkvn_ref, dkh_ref, gq_ref, gkv_ref, c_ref, sa_ref, sb_ref,
             dz_ref, dgq_ref, dgkv_ref):
        _acc_init(pl.program_id(0), dgq_ref, dgkv_ref)
        du0 = du0_ref[...]
        a = z_ref[:, 0:c]
        sg = _sigmoid(z_ref[:, c:2 * c])
        dz_ref[:, 0:c] = (du0 * sg).astype(BF16)
        dz_ref[:, c:2 * c] = (du0 * a * sg * (1.0 - sg)).astype(BF16)
        dq, dgq = _rms_bwd(z_ref[:, o_q:o_kv], gq_ref[...], dqn_ref[...])
        dz_ref[:, o_q:o_kv] = dq.astype(BF16)
        dgq_ref[...] += dgq
        dkv, dgkv = _rms_bwd(z_ref[:, o_kv:o_kr], gkv_ref[...], dkvn_ref[...])
        dz_ref[:, o_kv:o_kr] = dkv.astype(BF16)
        dgkv_ref[...] += dgkv
        dk = dkh_ref[:, 0:LANES]
        for h in range(1, N_HEADS):
            dk = dk + dkh_ref[:, h * LANES:(h + 1) * LANES]
        dz_ref[:, o_kr:o_kr + LANES] = _rope_t(dk, c_ref[...], sa_ref[...], sb_ref[...]).astype(BF16)

    return pl.pallas_call(
        body, name="split_bwd", grid=(s // ts,),
        in_specs=[_rows(ts, c), _rows(ts, zw), _rows(ts, ql), _rows(ts, kvl), _rows(ts, N_HEADS * LANES),
                  _vec(ql), _vec(kvl)] + [_rows(ts, LANES)] * 3,
        out_specs=[_rows(ts, zw), _vec(ql), _vec(kvl)],
        out_shape=[jax.ShapeDtypeStruct((s, zw), BF16), jax.ShapeDtypeStruct((1, ql), F32),
                   jax.ShapeDtypeStruct((1, kvl), F32)],
        compiler_params=_params(("arbitrary",)),
    )(du0, z, dqn, dkvn, dkpe_h, gq, gkv, *tabs)


def _q_rope(qpre, tabs, transpose, out_dtype, name):
    s, w = qpre.shape
    ts = _pick(s, (256, 128))
    rot = _rope_t if transpose else _rope

    def body(q_ref, c_ref, sa_ref, sb_ref, o_ref):
        cs, sa, sb = c_ref[...], sa_ref[...], sb_ref[...]
        for h in range(N_HEADS):
            lo = h * HEAD_PAD
            o_ref[:, lo:lo + QK_NOPE] = q_ref[:, lo:lo + QK_NOPE].astype(out_dtype)
            o_ref[:, lo + QK_NOPE:lo + HEAD_PAD] = rot(q_ref[:, lo + QK_NOPE:lo + HEAD_PAD], cs, sa, sb).astype(out_dtype)

    return pl.pallas_call(
        body, name=name, grid=(s // ts,),
        in_specs=[_rows(ts, w)] + [_rows(ts, LANES)] * 3, out_specs=_rows(ts, w),
        out_shape=jax.ShapeDtypeStruct((s, w), out_dtype),
        compiler_params=_params(("parallel",)),
    )(qpre, *tabs)


def _conv_fwd(u0, w, b):
    s, c = u0.shape
    tc = LANES
    rc = _pick(s, (256, 128))

    def body(u_ref, w_ref, b_ref, o_ref, pad_ref):
        pad_ref[0:CONV_PAD, :] = jnp.zeros((CONV_PAD, tc), F32)
        pad_ref[CONV_PAD:CONV_PAD + s, :] = u_ref[...]
        for r in range(s // rc):
            acc = jnp.broadcast_to(b_ref[...], (rc, tc))
            for k in range(CONV_K):
                lo = r * rc + CONV_PAD - (CONV_K - 1) + k
                acc = acc + w_ref[k:k + 1, :] * pad_ref[lo:lo + rc, :]
            o_ref[r * rc:(r + 1) * rc, :] = acc

    col = lambda j: (0, j)
    return pl.pallas_call(
        body, name="conv_fwd", grid=(c // tc,),
        in_specs=[pl.BlockSpec((s, tc), col), pl.BlockSpec((CONV_K, tc), col), pl.BlockSpec((1, tc), col)],
        out_specs=pl.BlockSpec((s, tc), col),
        out_shape=jax.ShapeDtypeStruct((s, c), F32),
        scratch_shapes=[pltpu.VMEM((s + CONV_PAD, tc), F32)],
        compiler_params=_params(("parallel",)),
    )(u0, w, b)


def _conv_bwd(du1, u0, w):
    s, c = u0.shape
    tc = LANES
    rc = _pick(s, (256, 128))

    def body(d_ref, u_ref, w_ref, du_ref, dw_ref, db_ref, upad_ref, dpad_ref):
        upad_ref[0:CONV_PAD, :] = jnp.zeros((CONV_PAD, tc), F32)
        upad_ref[CONV_PAD:CONV_PAD + s, :] = u_ref[...]
        dpad_ref[0:s, :] = d_ref[...]
        dpad_ref[s:s + CONV_PAD, :] = jnp.zeros((CONV_PAD, tc), F32)
        for r in range(s // rc):
            acc = jnp.zeros((rc, tc), F32)
            for k in range(CONV_K):
                lo = r * rc + (CONV_K - 1) - k
                acc = acc + w_ref[k:k + 1, :] * dpad_ref[lo:lo + rc, :]
            du_ref[r * rc:(r + 1) * rc, :] = acc
        for k in range(CONV_K):
            acc8 = jnp.zeros((8, tc), F32)
            for r in range(s // rc):
                lo = r * rc + CONV_PAD - (CONV_K - 1) + k
                prod = d_ref[r * rc:(r + 1) * rc, :] * upad_ref[lo:lo + rc, :]
                acc8 = acc8 + jnp.sum(prod.reshape(rc // 8, 8, tc), axis=0)
            dw_ref[k:k + 1, :] = jnp.sum(acc8, axis=0, keepdims=True)
        db_ref[...] = jnp.sum(d_ref[...], axis=0, keepdims=True)

    col = lambda j: (0, j)
    return pl.pallas_call(
        body, name="conv_bwd", grid=(c // tc,),
        in_specs=[pl.BlockSpec((s, tc), col), pl.BlockSpec((s, tc), col), pl.BlockSpec((CONV_K, tc), col)],
        out_specs=[pl.BlockSpec((s, tc), col), pl.BlockSpec((CONV_K, tc), col), pl.BlockSpec((1, tc), col)],
        out_shape=[jax.ShapeDtypeStruct((s, c), F32), jax.ShapeDtypeStruct((CONV_K, c), F32),
                   jax.ShapeDtypeStruct((1, c), F32)],
        scratch_shapes=[pltpu.VMEM((s + CONV_PAD, tc), F32), pltpu.VMEM((s + CONV_PAD, tc), F32)],
        compiler_params=_params(("parallel",)),
    )(du1, u0, w)


def _causal_mask(sc, qi, kj, tq, tk):
    rows = qi * tq + lax.broadcasted_iota(jnp.int32, sc.shape, 0)
    cols = kj * tk + lax.broadcasted_iota(jnp.int32, sc.shape, 1)
    return jnp.where(cols <= rows, sc, NEG)


def _attn_fwd(q, kv, kpe):
    s = q.shape[0]
    tq = tk = _pick(s, (256, 128))
    scale = QK_HEAD ** -0.5
    nt = (((1,), (1,)), ((), ()))

    def body(q_ref, kn_ref, v_ref, kpe_ref, o_ref, lse_ref, kf_ref, vb_ref):
        i = pl.program_id(1)

        @pl.when(i == 0)
        def _():
            kf_ref[:, 0:QK_NOPE] = kn_ref[...].astype(BF16)
            kf_ref[:, QK_NOPE:HEAD_PAD] = kpe_ref[...]
            vb_ref[...] = v_ref[...].astype(BF16)

        qb = q_ref[...]

        def step(j, carry):
            m, l, acc = carry
            off = pl.multiple_of(j * tk, tk)
            sc = lax.dot_general(qb, kf_ref[pl.ds(off, tk), :], nt, preferred_element_type=F32) * scale
            sc = _causal_mask(sc, i, j, tq, tk)
            m_new = jnp.maximum(m, jnp.max(sc, axis=1, keepdims=True))
            p = jnp.exp(sc - m_new)
            alpha = jnp.exp(m - m_new)
            l = alpha * l + jnp.sum(p, axis=1, keepdims=True)
            acc = alpha * acc + jnp.dot(p.astype(BF16), vb_ref[pl.ds(off, tk), :], preferred_element_type=F32)
            return m_new, l, acc

        init = (jnp.full((tq, 1), NEG, F32), jnp.zeros((tq, 1), F32), jnp.zeros((tq, V_HEAD), F32))
        m, l, acc = lax.fori_loop(0, i + 1, step, init)
        o_ref[...] = acc / l
        lse_ref[...] = jnp.broadcast_to(m + jnp.log(l), (tq, LANES))

    return pl.pallas_call(
        body, name="attn_fwd", grid=(N_HEADS, s // tq),
        in_specs=[pl.BlockSpec((tq, HEAD_PAD), lambda h, i: (i, h)),
                  pl.BlockSpec((s, QK_NOPE), lambda h, i: (0, 2 * h)),
                  pl.BlockSpec((s, V_HEAD), lambda h, i: (0, 2 * h + 1)),
                  pl.BlockSpec((s, LANES), lambda h, i: (0, 0))],
        out_specs=[pl.BlockSpec((tq, V_HEAD), lambda h, i: (i, h)),
                   pl.BlockSpec((tq, LANES), lambda h, i: (i, h))],
        out_shape=[jax.ShapeDtypeStruct((s, N_HEADS * V_HEAD), F32),
                   jax.ShapeDtypeStruct((s, N_HEADS * LANES), F32)],
        scratch_shapes=[pltpu.VMEM((s, HEAD_PAD), BF16), pltpu.VMEM((s, V_HEAD), BF16)],
        compiler_params=_params(("parallel", "arbitrary")),
    )(q, kv, kv, kpe)


def _attn_bwd(q, kv, kpe, o, do, lse):
    s = q.shape[0]
    tq = tk = _pick(s, (256, 128))
    nq = s // tq
    scale = QK_HEAD ** -0.5
    nt = (((1,), (1,)), ((), ()))
    tn = (((0,), (0,)), ((), ()))

    def body(q_ref, kn_ref, v_ref, kpe_ref, o_ref, do_ref, lse_ref, dq_ref, dkv_ref, dkpe_ref):
        j = pl.program_id(1)

        @pl.when(j == 0)
        def _():
            dq_ref[...] = jnp.zeros_like(dq_ref)

        kf = jnp.concatenate([kn_ref[...].astype(BF16), kpe_ref[...]], axis=1)
        vb = v_ref[...].astype(BF16)

        def step(i, carry):
            dk, dv = carry
            off = pl.multiple_of(i * tq, tq)
            qb = q_ref[pl.ds(off, tq), :]
            dob = do_ref[pl.ds(off, tq), :]
            delta = jnp.sum(dob * o_ref[pl.ds(off, tq), :], axis=1, keepdims=True)
            lse_b = lse_ref[pl.ds(off, tq), :][:, 0:1]
            sc = lax.dot_general(qb, kf, nt, preferred_element_type=F32) * scale
            sc = _causal_mask(sc, i, j, tq, tk)
            p = jnp.exp(sc - lse_b)
            dob16 = dob.astype(BF16)
            dv = dv + lax.dot_general(p.astype(BF16), dob16, tn, preferred_element_type=F32)
            dp = lax.dot_general(dob16, vb, nt, preferred_element_type=F32)
            ds = (p * (dp - delta) * scale).astype(BF16)
            dq_ref[pl.ds(off, tq), :] += jnp.dot(ds, kf, preferred_element_type=F32)
            dk = dk + lax.dot_general(ds, qb, tn, preferred_element_type=F32)
            return dk, dv

        dk, dv = lax.fori_loop(j, nq, step, (jnp.zeros((tk, HEAD_PAD), F32), jnp.zeros((tk, V_HEAD), F32)))
        dkv_ref[:, 0:QK_NOPE] = dk[:, 0:QK_NOPE]
        dkv_ref[:, QK_NOPE:HEAD_PAD] = dv
        dkpe_ref[...] = dk[:, QK_NOPE:HEAD_PAD]

    head_rows = lambda w: pl.BlockSpec((s, w), lambda h, j: (0, h))
    return pl.pallas_call(
        body, name="attn_bwd", grid=(N_HEADS, s // tk),
        in_specs=[head_rows(HEAD_PAD),
                  pl.BlockSpec((tk, QK_NOPE), lambda h, j: (j, 2 * h)),
                  pl.BlockSpec((tk, V_HEAD), lambda h, j: (j, 2 * h + 1)),
                  pl.BlockSpec((tk, LANES), lambda h, j: (j, 0)),
                  head_rows(V_HEAD), head_rows(V_HEAD), head_rows(LANES)],
        out_specs=[head_rows(HEAD_PAD),
                   pl.BlockSpec((tk, HEAD_PAD), lambda h, j: (j, h)),
                   pl.BlockSpec((tk, LANES), lambda h, j: (j, h))],
        out_shape=[jax.ShapeDtypeStruct((s, N_HEADS * HEAD_PAD), F32),
                   jax.ShapeDtypeStruct((s, N_HEADS * HEAD_PAD), F32),
                   jax.ShapeDtypeStruct((s, N_HEADS * LANES), F32)],
        compiler_params=_params(("parallel", "arbitrary")),
    )(q, kv, kv, kpe, o, do, lse)


def _mix_fwd(u1, lng, lnb, gcon, attn, gattn):
    s, c = u1.shape
    ac = attn.shape[1]
    ts = _pick(s, (256, 128))

    def body(u_ref, lg_ref, lb_ref, gc_ref, a_ref, ga_ref, o_ref):
        t3 = _silu(_ln(u_ref[...], lg_ref[...], lb_ref[...]))
        o_ref[:, 0:c] = _rms(t3, gc_ref[...]).astype(BF16)
        o_ref[:, c:c + ac] = _rms(a_ref[...], ga_ref[...]).astype(BF16)

    return pl.pallas_call(
        body, name="mix_fwd", grid=(s // ts,),
        in_specs=[_rows(ts, c), _vec(c), _vec(c), _vec(c), _rows(ts, ac), _vec(ac)],
        out_specs=_rows(ts, c + ac),
        out_shape=jax.ShapeDtypeStruct((s, c + ac), BF16),
        compiler_params=_params(("parallel",)),
    )(u1, lng, lnb, gcon, attn, gattn)


def _mix_bwd(dmixin, u1, lng, lnb, gcon, attn, gattn):
    s, c = u1.shape
    ac = attn.shape[1]
    ts = _pick(s, (256, 128))

    def body(d_ref, u_ref, lg_ref, lb_ref, gc_ref, a_ref, ga_ref,
             du_ref, da_ref, dlg_ref, dlb_ref, dgc_ref, dga_ref):
        _acc_init(pl.program_id(0), dlg_ref, dlb_ref, dgc_ref, dga_ref)
        u = u_ref[...]
        t2 = _ln(u, lg_ref[...], lb_ref[...])
        dt3, dgc = _rms_bwd(_silu(t2), gc_ref[...], d_ref[:, 0:c])
        du, dlg, dlb = _ln_bwd(u, lg_ref[...], dt3 * _silu_grad(t2))
        du_ref[...] = du
        dlg_ref[...] += dlg
        dlb_ref[...] += dlb
        dgc_ref[...] += dgc
        da, dga = _rms_bwd(a_ref[...], ga_ref[...], d_ref[:, c:c + ac])
        da_ref[...] = da
        dga_ref[...] += dga

    return pl.pallas_call(
        body, name="mix_bwd", grid=(s // ts,),
        in_specs=[_rows(ts, c + ac), _rows(ts, c), _vec(c), _vec(c), _vec(c), _rows(ts, ac), _vec(ac)],
        out_specs=[_rows(ts, c), _rows(ts, ac), _vec(c), _vec(c), _vec(c), _vec(ac)],
        out_shape=[jax.ShapeDtypeStruct((s, c), F32), jax.ShapeDtypeStruct((s, ac), F32),
                   jax.ShapeDtypeStruct((1, c), F32), jax.ShapeDtypeStruct((1, c), F32),
                   jax.ShapeDtypeStruct((1, c), F32), jax.ShapeDtypeStruct((1, ac), F32)],
        compiler_params=_params(("arbitrary",)),
    )(dmixin, u1, lng, lnb, gcon, attn, gattn)


def _post_mix_fwd(x, mix, gpost, gpre):
    s, d = x.shape
    ts = _pick(s, (256, 128))

    def body(x_ref, m_ref, gp_ref, gf_ref, x1_ref, hf_ref):
        x1 = x_ref[...] + _rms(m_ref[...], gp_ref[...])
        x1_ref[...] = x1
        hf_ref[...] = _rms(x1, gf_ref[...]).astype(BF16)

    return pl.pallas_call(
        body, name="post_mix_fwd", grid=(s // ts,),
        in_specs=[_rows(ts, d), _rows(ts, d), _vec(d), _vec(d)],
        out_specs=[_rows(ts, d), _rows(ts, d)],
        out_shape=[jax.ShapeDtypeStruct((s, d), F32), jax.ShapeDtypeStruct((s, d), BF16)],
        compiler_params=_params(("parallel",)),
    )(x, mix, gpost, gpre)


def _post_mix_bwd(dy, dhf, x1, gpre, mix, gpost):
    s, d = x1.shape
    ts = _pick(s, (256, 128))

    def body(dy_ref, dh_ref, x1_ref, gf_ref, m_ref, gp_ref, dx1_ref, dm_ref, dgf_ref, dgp_ref):
        _acc_init(pl.program_id(0), dgf_ref, dgp_ref)
        dxa, dgf = _rms_bwd(x1_ref[...], gf_ref[...], dh_ref[...])
        dx1 = dy_ref[...] + dxa
        dx1_ref[...] = dx1
        dgf_ref[...] += dgf
        dm, dgp = _rms_bwd(m_ref[...], gp_ref[...], dx1)
        dm_ref[...] = dm.astype(BF16)
        dgp_ref[...] += dgp

    return pl.pallas_call(
        body, name="post_mix_bwd", grid=(s // ts,),
        in_specs=[_rows(ts, d), _rows(ts, d), _rows(ts, d), _vec(d), _rows(ts, d), _vec(d)],
        out_specs=[_rows(ts, d), _rows(ts, d), _vec(d), _vec(d)],
        out_shape=[jax.ShapeDtypeStruct((s, d), F32), jax.ShapeDtypeStruct((s, d), BF16),
                   jax.ShapeDtypeStruct((1, d), F32), jax.ShapeDtypeStruct((1, d), F32)],
        compiler_params=_params(("arbitrary",)),
    )(dy, dhf, x1, gpre, mix, gpost)


def _ffn_act(gate, up):
    s, f = gate.shape
    ts = _pick(s, (256, 128))
    tf = _pick(f, (512, 256, 128))

    def body(g_ref, u_ref, o_ref):
        o_ref[...] = (_silu(g_ref[...]) * u_ref[...]).astype(BF16)

    return pl.pallas_call(
        body, name="ffn_act", grid=(s // ts, f // tf),
        in_specs=[pl.BlockSpec((ts, tf), lambda i, j: (i, j))] * 2,
        out_specs=pl.BlockSpec((ts, tf), lambda i, j: (i, j)),
        out_shape=jax.ShapeDtypeStruct((s, f), BF16),
        compiler_params=_params(("parallel", "parallel")),
    )(gate, up)


def _ffn_act_bwd(dact, gate, up):
    s, f = gate.shape
    ts = _pick(s, (256, 128))
    tf = _pick(f, (512, 256, 128))
    nf = f // tf

    def body(d_ref, g_ref, u_ref, dg_ref, du_ref):
        d, g = d_ref[...], g_ref[...]
        dg_ref[...] = (d * u_ref[...] * _silu_grad(g)).astype(BF16)
        du_ref[...] = (d * _silu(g)).astype(BF16)

    blk = pl.BlockSpec((ts, tf), lambda i, j: (i, j))
    return pl.pallas_call(
        body, name="ffn_act_bwd", grid=(s // ts, nf),
        in_specs=[blk] * 3, out_specs=[blk] * 2,
        out_shape=[jax.ShapeDtypeStruct((s, f), BF16)] * 2,
        compiler_params=_params(("parallel", "parallel")),
    )(dact, gate, up)


def _final(ff, x1, tgt, g):
    s, d = x1.shape
    ts = _pick(s, (256, 128))

    def body(ff_ref, x1_ref, t_ref, g_ref, loss_ref, dy_ref, dff_ref, dg_ref):
        _acc_init(pl.program_id(0), loss_ref, dg_ref)
        ff_v = ff_ref[...]
        err = x1_ref[...] + _rms(ff_v, g_ref[...]) - t_ref[...]
        tok = jnp.mean(err * err, axis=-1, keepdims=True)
        loss_ref[...] += 0.5 * jnp.sum(tok, axis=0, keepdims=True)
        dy = err * (1.0 / d)
        dy_ref[...] = dy
        dff, dg = _rms_bwd(ff_v, g_ref[...], dy)
        dff_ref[...] = dff.astype(BF16)
        dg_ref[...] += dg

    return pl.pallas_call(
        body, name="final", grid=(s // ts,),
        in_specs=[_rows(ts, d), _rows(ts, d), _rows(ts, d), _vec(d)],
        out_specs=[_vec(LANES), _rows(ts, d), _rows(ts, d), _vec(d)],
        out_shape=[jax.ShapeDtypeStruct((1, LANES), F32), jax.ShapeDtypeStruct((s, d), F32),
                   jax.ShapeDtypeStruct((s, d), BF16), jax.ShapeDtypeStruct((1, d), F32)],
        compiler_params=_params(("arbitrary",)),
    )(ff, x1, tgt, g)


def _pre_bwd(dx1, dh, x, g):
    s, d = x.shape
    ts = _pick(s, (256, 128))

    def body(dx1_ref, dh_ref, x_ref, g_ref, dx_ref, dg_ref):
        _acc_init(pl.program_id(0), dg_ref)
        dxa, dg = _rms_bwd(x_ref[...], g_ref[...], dh_ref[...])
        dx_ref[...] = dx1_ref[...] + dxa
        dg_ref[...] += dg

    return pl.pallas_call(
        body, name="pre_bwd", grid=(s // ts,),
        in_specs=[_rows(ts, d), _rows(ts, d), _rows(ts, d), _vec(d)],
        out_specs=[_rows(ts, d), _vec(d)],
        out_shape=[jax.ShapeDtypeStruct((s, d), F32), jax.ShapeDtypeStruct((1, d), F32)],
        compiler_params=_params(("arbitrary",)),
    )(dx1, dh, x, g)


def _local_step(x, pos, tgt, vecs, w_in_p, w_uq_p, w_ukv, conv_w, w_out, w_gate, w_up, w_down):
    c = vecs["conv_b"].shape[1]
    ql = vecs["q_norm"].shape[1]
    kvl = vecs["kv_norm"].shape[1]
    half = jnp.arange(0, QK_ROPE, 2, dtype=F32)
    freq = ROPE_THETA ** (-half / QK_ROPE)
    inv_freq = jnp.concatenate([freq, freq, jnp.zeros((LANES - QK_ROPE,), F32)])[None, :]
    tabs = _rope_tables(pos, inv_freq)

    h = _pre_fwd(x, vecs["pre_mix_norm"])
    z = _mm(h, w_in_p, "nn", "mm_z")
    u0, qn, kvn, kpe = _split_fwd(z, vecs["q_norm"], vecs["kv_norm"], tabs, c, ql, kvl)
    u1 = _conv_fwd(u0, conv_w, vecs["conv_b"])
    q = _q_rope(_mm(qn, w_uq_p, "nn", "mm_q"), tabs, False, BF16, "q_rope")
    kv = _mm(kvn, w_ukv, "nn", "mm_kv")
    attn, lse = _attn_fwd(q, kv, kpe)
    mixin = _mix_fwd(u1, vecs["conv_ln_g"], vecs["conv_ln_b"], vecs["conv_out_norm"], attn, vecs["attn_out_norm"])
    mix = _mm(mixin, w_out, "nn", "mm_mix")
    x1, hf = _post_mix_fwd(x, mix, vecs["post_mix_norm"], vecs["pre_ffn_norm"])
    gate = _mm(hf, w_gate, "nn", "mm_gate")
    up = _mm(hf, w_up, "nn", "mm_up")
    act = _ffn_act(gate, up)
    ff = _mm(act, w_down, "nn", "mm_ff")
    loss, dy, dff, d_post_ffn = _final(ff, x1, tgt, vecs["post_ffn_norm"])

    g = {"post_ffn_norm": d_post_ffn}
    dact = _mm(dff, w_down, "nt", "mm_dact")
    g["w_down"] = _mm(act, dff, "tn", "mm_dw_down", BF16)
    dgate, dup = _ffn_act_bwd(dact, gate, up)
    dhf = _mm(dup, w_up, "nt", "mm_dhf_up", add=_mm(dgate, w_gate, "nt", "mm_dhf_gate"))
    g["w_gate"] = _mm(hf, dgate, "tn", "mm_dw_gate", BF16)
    g["w_up"] = _mm(hf, dup, "tn", "mm_dw_up", BF16)
    dx1, dmix, g["pre_ffn_norm"], g["post_mix_norm"] = _post_mix_bwd(
        dy, dhf, x1, vecs["pre_ffn_norm"], mix, vecs["post_mix_norm"])
    dmixin = _mm(dmix, w_out, "nt", "mm_dmixin")
    g["w_out"] = _mm(mixin, dmix, "tn", "mm_dw_out", BF16)
    du1, dattn, g["conv_ln_g"], g["conv_ln_b"], g["conv_out_norm"], g["attn_out_norm"] = _mix_bwd(
        dmixin, u1, vecs["conv_ln_g"], vecs["conv_ln_b"], vecs["conv_out_norm"], attn, vecs["attn_out_norm"])
    du0, g["conv_w"], g["conv_b"] = _conv_bwd(du1, u0, conv_w)
    dq, dkv, dkpe_h = _attn_bwd(q, kv, kpe, attn, dattn, lse)
    dqpre = _q_rope(dq, tabs, True, BF16, "q_rope_bwd")
    dqn = _mm(dqpre, w_uq_p, "nt", "mm_dqn")
    g["w_uq_p"] = _mm(qn, dqpre, "tn", "mm_dw_uq", BF16)
    dkvn = _mm(dkv, w_ukv, "nt", "mm_dkvn")
    g["w_ukv"] = _mm(kvn, dkv, "tn", "mm_dw_ukv", BF16)
    dz, g["q_norm"], g["kv_norm"] = _split_bwd(du0, z, dqn, dkvn, dkpe_h, vecs["q_norm"], vecs["kv_norm"], tabs, c, ql, kvl)
    dh = _mm(dz, w_in_p, "nt", "mm_dh")
    g["w_in_p"] = _mm(h, dz, "tn", "mm_dw_in", BF16)
    grad_x, g["pre_mix_norm"] = _pre_bwd(dx1, dh, x, vecs["pre_mix_norm"])
    return loss, grad_x, g


def _my_index():
    return 4 * lax.axis_index("x") + 2 * lax.axis_index("y") + lax.axis_index("c")


def _coords(idx):
    return ((idx >> 2) & 1, (idx >> 1) & 1, idx & 1)


def _place():
    x, y, c = lax.axis_index("x"), lax.axis_index("y"), lax.axis_index("c")
    return (x, y, c), (x, y, 1 - c), [(1 - x, y), (x, 1 - y), (1 - x, 1 - y)]


def _comm_call(body, name, arrays, out_shapes, n_sems):
    n = len(arrays)
    any_spec = pl.BlockSpec(memory_space=pl.ANY)
    return pl.pallas_call(
        body, name=name,
        in_specs=[any_spec] * n, out_specs=[any_spec] * len(out_shapes), out_shape=out_shapes,
        scratch_shapes=[pltpu.SemaphoreType.DMA((n, n_sems))] * 3,
        compiler_params=pltpu.CompilerParams(has_side_effects=True),
    )(*arrays)


def _gather_direct(arrays, name):
    n = len(arrays)

    def body(*refs):
        ins, outs = refs[:n], refs[n:2 * n]
        send_sems, recv_sems, local_sems = refs[2 * n:]
        me = _my_index()
        local = [pltpu.make_async_copy(ins[k], outs[k].at[me], local_sems.at[k, 0]) for k in range(n)]
        for cp in local:
            cp.start()

        def copy(k, p, slot):
            return pltpu.make_async_remote_copy(
                src_ref=ins[k], dst_ref=outs[k].at[slot], send_sem=send_sems.at[k, p - 1],
                recv_sem=recv_sems.at[k, p - 1], device_id=_coords(me ^ p), device_id_type=MESH)

        sends = [copy(k, p, me) for p in range(1, N_DEV) for k in range(n)]
        for cp in sends:
            cp.start()
        for p in range(1, N_DEV):
            for k in range(n):
                copy(k, p, me ^ p).wait_recv()
        for cp in sends:
            cp.wait_send()
        for cp in local:
            cp.wait()

    out_shapes = [jax.ShapeDtypeStruct((N_DEV,) + a.shape, a.dtype) for a in arrays]
    return _comm_call(body, name, arrays, out_shapes, N_DEV - 1)


def _gather_weights(shards):
    n = len(shards)

    def body(*refs):
        ins, outs = refs[:n], refs[n:2 * n]
        send_sems, recv_sems, local_sems = refs[2 * n:]
        (x, y, c), sib, chips = _place()
        me = 4 * x + 2 * y + c

        def slot(chip, core):
            return 4 * chip[0] + 2 * chip[1] + core

        def copy(k, i, block, to, src=None):
            rows = outs[k].at[block]
            return pltpu.make_async_remote_copy(
                src_ref=rows if src is None else src, dst_ref=rows, send_sem=send_sems.at[k, i],
                recv_sem=recv_sems.at[k, i], device_id=to, device_id_type=MESH)

        sends = []
        for k in range(n):
            sends.append(copy(k, 0, me, sib, src=ins[k]))
            sends += [copy(k, 1 + j, me, (*chip, c), src=ins[k]) for j, chip in enumerate(chips)]
        for cp in sends:
            cp.start()
        for j, chip in enumerate(chips):
            for k in range(n):
                copy(k, 1 + j, slot(chip, c), (x, y, c)).wait_recv()
                fwd = copy(k, 4 + j, slot(chip, c), sib)
                fwd.start()
                sends.append(fwd)
        for k in range(n):
            copy(k, 0, slot((x, y), 1 - c), (x, y, c)).wait_recv()
            for j, chip in enumerate(chips):
                copy(k, 4 + j, slot(chip, 1 - c), (x, y, c)).wait_recv()
        for cp in sends:
            cp.wait_send()

    out_shapes = [jax.ShapeDtypeStruct((N_DEV,) + a.shape, a.dtype) for a in shards]
    got = _comm_call(body, "gather_weights", shards, out_shapes, 7)
    me = _my_index()
    return [lax.dynamic_update_slice(g, s[None], (me,) + (0,) * s.ndim) for g, s in zip(got, shards)]


def _scatter_to_sibling(blocks):
    n = len(blocks)

    def body(*refs):
        ins, theirs = refs[:n], refs[n:2 * n]
        send_sems, recv_sems, _ = refs[2 * n:]
        (x, y, c), sib, _ = _place()
        sends = []
        for k in range(n):
            for q in range(4):
                sends.append(pltpu.make_async_remote_copy(
                    src_ref=ins[k].at[2 * q + 1 - c], dst_ref=theirs[k].at[q], send_sem=send_sems.at[k, q],
                    recv_sem=recv_sems.at[k, q], device_id=sib, device_id_type=MESH))
        for cp in sends:
            cp.start()
        for cp in sends:
            cp.wait_recv()
        for cp in sends:
            cp.wait_send()

    out_shapes = [jax.ShapeDtypeStruct((4,) + a.shape[1:], a.dtype) for a in blocks]
    return _comm_call(body, "scatter_to_sibling", blocks, out_shapes, 4)


def _scatter_to_chips(pairs):
    n = len(pairs)

    def body(*refs):
        ins, outs = refs[:n], refs[n:2 * n]
        send_sems, recv_sems, _ = refs[2 * n:]
        (x, y, c), _, chips = _place()
        sends = []
        for k in range(n):
            for j, chip in enumerate(chips):
                sends.append(pltpu.make_async_remote_copy(
                    src_ref=ins[k].at[2 * chip[0] + chip[1]], dst_ref=outs[k].at[j], send_sem=send_sems.at[k, j],
                    recv_sem=recv_sems.at[k, j], device_id=(*chip, c), device_id_type=MESH))
        for cp in sends:
            cp.start()
        for cp in sends:
            cp.wait_recv()
        for cp in sends:
            cp.wait_send()

    out_shapes = [jax.ShapeDtypeStruct((3,) + a.shape[1:], a.dtype) for a in pairs]
    return _comm_call(body, "scatter_to_chips", pairs, out_shapes, 3)


def _pair_sum(core, blocks, theirs, name):
    q, r, c = theirs.shape
    tr = _pick(r, (256, 128, 64, 32, 16))

    def body(core_ref, a_ref, b_ref, o_ref):
        o_ref[...] = (a_ref[...].astype(F32) + b_ref[...].astype(F32)).astype(o_ref.dtype)

    blk = pl.BlockSpec((1, tr, c), lambda i, j, core_ref: (i, j, 0))
    mine = pl.BlockSpec((1, tr, c), lambda i, j, core_ref: (2 * i + core_ref[0], j, 0))
    return pl.pallas_call(
        body, name=name,
        grid_spec=pltpu.PrefetchScalarGridSpec(num_scalar_prefetch=1, grid=(q, r // tr),
                                               in_specs=[mine, blk], out_specs=blk),
        out_shape=jax.ShapeDtypeStruct(theirs.shape, theirs.dtype),
        compiler_params=_params(("parallel", "parallel")),
    )(core, blocks, theirs)


def _reduce_adamw(parts, w, m, v, name, own=None, own_slot=None):
    r, c = w.shape
    n_parts = parts.shape[0]
    tr = _pick(r, (256, 128, 64, 32, 16))
    c1 = 1.0 - ADAM_B1
    c2 = 1.0 - ADAM_B2
    bc1 = 1.0 - ADAM_B1 ** ADAM_STEP
    bc2 = 1.0 - ADAM_B2 ** ADAM_STEP

    def body(*refs):
        if own is None:
            p_ref, w_ref, m_ref, v_ref, g_ref, d_ref, nm_ref, nv_ref = refs
            g = p_ref[0].astype(F32)
            first = 1
        else:
            _, o_ref, p_ref, w_ref, m_ref, v_ref, g_ref, d_ref, nm_ref, nv_ref = refs
            g = o_ref[0].astype(F32)
            first = 0
        for j in range(first, n_parts):
            g = g + p_ref[j].astype(F32)
        nm = ADAM_B1 * m_ref[...] + c1 * g
        nv = ADAM_B2 * v_ref[...] + c2 * (g * g)
        g_ref[...] = g
        nm_ref[...] = nm
        nv_ref[...] = nv
        d_ref[...] = -ADAM_LR * ((nm / bc1) / (jnp.sqrt(nv / bc2) + ADAM_EPS) + ADAM_WD * w_ref[...])

    out = jax.ShapeDtypeStruct((r, c), F32)
    if own is None:
        blk = pl.BlockSpec((tr, c), lambda i: (i, 0))
        return pl.pallas_call(
            body, name=name, grid=(r // tr,),
            in_specs=[pl.BlockSpec((n_parts, tr, c), lambda i: (0, i, 0)), blk, blk, blk],
            out_specs=[blk] * 4, out_shape=[out] * 4,
            compiler_params=_params(("parallel",)),
        )(parts, w, m, v)
    blk = pl.BlockSpec((tr, c), lambda i, slot_ref: (i, 0))
    return pl.pallas_call(
        body, name=name,
        grid_spec=pltpu.PrefetchScalarGridSpec(
            num_scalar_prefetch=1, grid=(r // tr,),
            in_specs=[pl.BlockSpec((1, tr, c), lambda i, slot_ref: (slot_ref[0], i, 0)),
                      pl.BlockSpec((n_parts, tr, c), lambda i, slot_ref: (0, i, 0)), blk, blk, blk],
            out_specs=[blk] * 4),
        out_shape=[out] * 4,
        compiler_params=_params(("parallel",)),
    )(own_slot, own, parts, w, m, v)


_BIG = ("w_in", "w_uq", "w_ukv", "conv_w", "w_out", "w_gate", "w_up", "w_down")
_SMALL = ("pre_mix_norm", "q_norm", "kv_norm", "conv_b", "conv_ln_g", "conv_ln_b", "conv_out_norm",
          "attn_out_norm", "post_mix_norm", "pre_ffn_norm", "post_ffn_norm")
_ORDER = ("pre_mix_norm", "w_in", "q_norm", "w_uq", "kv_norm", "w_ukv", "conv_w", "conv_b", "conv_ln_g",
          "conv_ln_b", "conv_out_norm", "attn_out_norm", "w_out", "post_mix_norm", "pre_ffn_norm", "w_gate",
          "w_up", "w_down", "post_ffn_norm")


def _cols_from_shards(g):
    return jnp.transpose(g, (1, 0, 2)).reshape(g.shape[1], N_DEV * g.shape[2])


def _cols_to_shards(w):
    k, n8 = w.shape
    return jnp.transpose(w.reshape(k, N_DEV, n8 // N_DEV), (1, 0, 2))


def _step(x, positions, loss_target, w, m, v):
    s, d = x.shape[1], x.shape[2]
    x2, tgt = x[0], loss_target[0]
    pos = positions.reshape(s, 1)
    vecs = {n: w[n] for n in _SMALL}

    shards = [w[n][0] if n == "conv_w" else w[n][0].astype(BF16) for n in _BIG]
    gath = dict(zip(_BIG, _gather_weights(shards)))
    w_in_f = _cols_from_shards(gath["w_in"])
    zw = w_in_f.shape[1] + LANES - QK_ROPE
    w_in_p = jnp.pad(w_in_f, ((0, 0), (0, zw - w_in_f.shape[1])))
    w_uq_p = _cols_from_shards(jnp.pad(gath["w_uq"], ((0, 0), (0, 0), (0, HEAD_PAD - QK_HEAD))))
    w_ukv = _cols_from_shards(gath["w_ukv"])
    conv_w_f = _cols_from_shards(gath["conv_w"])
    w_out_f = gath["w_out"].reshape(-1, d)
    w_gate_f = _cols_from_shards(gath["w_gate"])
    w_up_f = _cols_from_shards(gath["w_up"])
    w_down_f = gath["w_down"].reshape(-1, d)

    loss, grad_x, g = _local_step(x2, pos, tgt, vecs, w_in_p, w_uq_p, w_ukv, conv_w_f, w_out_f, w_gate_f, w_up_f, w_down_f)

    blocks = {
        "w_in": _cols_to_shards(g["w_in_p"][:, :w_in_f.shape[1]]),
        "w_uq": _cols_to_shards(g["w_uq_p"])[:, :, :QK_HEAD],
        "w_ukv": _cols_to_shards(g["w_ukv"]),
        "conv_w": _cols_to_shards(g["conv_w"]),
        "w_out": g["w_out"].reshape(N_DEV, -1, d),
        "w_gate": _cols_to_shards(g["w_gate"]),
        "w_up": _cols_to_shards(g["w_up"]),
        "w_down": g["w_down"].reshape(N_DEV, -1, d),
    }
    theirs = _scatter_to_sibling([blocks[n] for n in _BIG])
    core = lax.axis_index("c").astype(jnp.int32).reshape(1)
    pairs = [_pair_sum(core, blocks[n], b, "pair_sum_" + n) for n, b in zip(_BIG, theirs)]
    recv = dict(zip(_BIG, _scatter_to_chips(pairs)))
    pairs = dict(zip(_BIG, pairs))
    my_chip = (2 * lax.axis_index("x") + lax.axis_index("y")).astype(jnp.int32).reshape(1)
    small = jnp.concatenate([g[n] for n in _SMALL], axis=1)
    small_all = _gather_direct([small], "gather_small_grads")[0]

    res = {}
    for n in _BIG:
        res[n] = _reduce_adamw(recv[n], w[n][0], m[n][0], v[n][0], "adamw_" + n, own=pairs[n], own_slot=my_chip)
        res[n] = [t[None] for t in res[n]]
    cat = lambda t: jnp.concatenate([t[n] for n in _SMALL], axis=1)
    sg, sd, sm, sv = _reduce_adamw(small_all, cat(w), cat(m), cat(v), "adamw_small")
    off = 0
    for n in _SMALL:
        width = w[n].shape[1]
        res[n] = [t[:, off:off + width] for t in (sg, sd, sm, sv)]
        off += width

    total = lax.psum(loss[0, 0], ("x", "y", "c"))
    outs = [total, grad_x[None]]
    for part in range(4):
        outs.extend(res[n][part] for n in _ORDER)
    return tuple(outs)


def kernel(x, positions, pre_mix_norm, w_in, q_norm, w_uq, kv_norm, w_ukv, conv_w, conv_b, conv_ln_g, conv_ln_b, conv_out_norm, attn_out_norm, w_out, post_mix_norm, pre_ffn_norm, w_gate, w_up, w_down, post_ffn_norm, loss_target, m_pre_mix_norm, m_w_in, m_q_norm, m_w_uq, m_kv_norm, m_w_ukv, m_conv_w, m_conv_b, m_conv_ln_g, m_conv_ln_b, m_conv_out_norm, m_attn_out_norm, m_w_out, m_post_mix_norm, m_pre_ffn_norm, m_w_gate, m_w_up, m_w_down, m_post_ffn_norm, v_pre_mix_norm, v_w_in, v_q_norm, v_w_uq, v_kv_norm, v_w_ukv, v_conv_w, v_conv_b, v_conv_ln_g, v_conv_ln_b, v_conv_out_norm, v_attn_out_norm, v_w_out, v_post_mix_norm, v_pre_ffn_norm, v_w_gate, v_w_up, v_w_down, v_post_ffn_norm):
    w = dict(zip(_ORDER, (pre_mix_norm, w_in, q_norm, w_uq, kv_norm, w_ukv, conv_w, conv_b, conv_ln_g, conv_ln_b,
                          conv_out_norm, attn_out_norm, w_out, post_mix_norm, pre_ffn_norm, w_gate, w_up, w_down,
                          post_ffn_norm)))
    m = dict(zip(_ORDER, (m_pre_mix_norm, m_w_in, m_q_norm, m_w_uq, m_kv_norm, m_w_ukv, m_conv_w, m_conv_b,
                          m_conv_ln_g, m_conv_ln_b, m_conv_out_norm, m_attn_out_norm, m_w_out, m_post_mix_norm,
                          m_pre_ffn_norm, m_w_gate, m_w_up, m_w_down, m_post_ffn_norm)))
    v = dict(zip(_ORDER, (v_pre_mix_norm, v_w_in, v_q_norm, v_w_uq, v_kv_norm, v_w_ukv, v_conv_w, v_conv_b,
                          v_conv_ln_g, v_conv_ln_b, v_conv_out_norm, v_attn_out_norm, v_w_out, v_post_mix_norm,
                          v_pre_ffn_norm, v_w_gate, v_w_up, v_w_down, v_post_ffn_norm)))
    return _step(x, positions, loss_target, w, m, v)
```

```python
import functools

import jax
import jax.numpy as jnp
from jax import lax
from jax.experimental import pallas as pl
from jax.experimental.pallas import tpu as pltpu

N_DEV = 8
N_HEADS = 8
QK_NOPE = 128
QK_ROPE = 64
V_HEAD = 128
QK_HEAD = QK_NOPE + QK_ROPE
HEAD_PAD = 256
LANES = 128
ATTN_BLOCK = 512
CONV_K = 31
CONV_PAD = 32
EPS = 1e-6
ROPE_THETA = 10000.0
ADAM_LR = 0.001
ADAM_B1 = 0.9
ADAM_B2 = 0.999
ADAM_EPS = 1e-08
ADAM_WD = 0.01
ADAM_STEP = 10
VMEM_LIMIT = 56 * 1024 * 1024
F32 = jnp.float32
BF16 = jnp.bfloat16
MESH = pl.DeviceIdType.MESH
NEG = -1e30


def _pick(n, prefs):
    for p in prefs:
        if p <= n and n % p == 0:
            return p
    return n


def _params(sem):
    return pltpu.CompilerParams(dimension_semantics=sem, vmem_limit_bytes=VMEM_LIMIT)


_DIMS = {"nn": (((1,), (0,)), ((), ())), "nt": (((1,), (1,)), ((), ())), "tn": (((0,), (0,)), ((), ()))}


MM_VMEM_BUDGET = 40 * 1024 * 1024
MM_MAX_MACS = 3 * 1024 ** 3


V7X_HBM_BYTES_PER_S = 3.0e12
V7X_MXU_MACS_PER_S = 0.45e15
GRID_STEP_S = 0.35e-6


def _mm_tiles(m, n, k, size_a, size_b, size_o):
    best = None
    for tm in sorted({m, 1024, 512, 256, 128}, reverse=True):
        if tm > m or m % tm:
            continue
        for tn in sorted({n, 2048, 1024, 512, 384, 256, 128}, reverse=True):
            if tn > n or n % tn:
                continue
            vmem = 2 * (tm * k * size_a + k * tn * size_b + tm * tn * size_o)
            if vmem > MM_VMEM_BUDGET or tm * tn * k > MM_MAX_MACS:
                continue
            b_reads = 1 if tn == n else m // tm
            traffic = m * k * size_a + b_reads * k * n * size_b + m * n * size_o
            exposed = tm * k * size_a + k * tn * size_b + tm * tn * size_o
            steps = (m // tm) * (n // tn)
            key = (max(traffic / V7X_HBM_BYTES_PER_S, m * n * k / V7X_MXU_MACS_PER_S)
                   + exposed / V7X_HBM_BYTES_PER_S + steps * GRID_STEP_S)
            if best is None or key < best[0]:
                best = (key, tm, tn)
    assert best is not None, (m, n, k)
    return best[1], best[2]


def _mm(a, b, mode, name, out_dtype=F32, add=None):
    if mode == "nn":
        (m, k), (k2, n) = a.shape, b.shape
    elif mode == "nt":
        (m, k), (n, k2) = a.shape, b.shape
    else:
        (k, m), (k2, n) = a.shape, b.shape
    assert k == k2, (a.shape, b.shape, mode)
    tm, tn = _mm_tiles(m, n, k, a.dtype.itemsize, b.dtype.itemsize,
                       jnp.dtype(out_dtype).itemsize + (0 if add is None else 4))
    dims = _DIMS[mode]

    def body(a_ref, b_ref, *rest):
        acc = lax.dot_general(a_ref[...].astype(BF16), b_ref[...].astype(BF16), dims, preferred_element_type=F32)
        if add is not None:
            acc = acc + rest[0][...]
        rest[-1][...] = acc.astype(rest[-1].dtype)

    if mode == "tn":
        a_spec = pl.BlockSpec((k, tm), lambda i, j: (0, i))
    else:
        a_spec = pl.BlockSpec((tm, k), lambda i, j: (i, 0))
    if mode == "nt":
        b_spec = pl.BlockSpec((tn, k), lambda i, j: (j, 0))
    else:
        b_spec = pl.BlockSpec((k, tn), lambda i, j: (0, j))
    o_spec = pl.BlockSpec((tm, tn), lambda i, j: (i, j))
    extra = [] if add is None else [add]
    return pl.pallas_call(
        body, name=name,
        grid=(m // tm, n // tn),
        in_specs=[a_spec, b_spec] + [o_spec] * len(extra),
        out_specs=o_spec,
        out_shape=jax.ShapeDtypeStruct((m, n), out_dtype),
        compiler_params=_params(("parallel", "parallel")),
    )(a, b, *extra)


def _sigmoid(x):
    return 1.0 / (1.0 + jnp.exp(-x))


def _rms(x, g):
    r = lax.rsqrt(jnp.mean(x * x, axis=-1, keepdims=True) + EPS)
    return (x * r) * g


def _rms_bwd(x, g, dy):
    r = lax.rsqrt(jnp.mean(x * x, axis=-1, keepdims=True) + EPS)
    xh = x * r
    dyg = dy * g
    dx = r * (dyg - xh * jnp.mean(dyg * xh, axis=-1, keepdims=True))
    return dx, jnp.sum(dy * xh, axis=0, keepdims=True)


def _ln(x, g, b):
    mu = jnp.mean(x, axis=-1, keepdims=True)
    xc = x - mu
    rs = lax.rsqrt(jnp.mean(xc * xc, axis=-1, keepdims=True) + EPS)
    return (xc * rs) * g + b


def _ln_bwd(x, g, dy):
    mu = jnp.mean(x, axis=-1, keepdims=True)
    xc = x - mu
    rs = lax.rsqrt(jnp.mean(xc * xc, axis=-1, keepdims=True) + EPS)
    xh = xc * rs
    dyg = dy * g
    dx = rs * (dyg - jnp.mean(dyg, axis=-1, keepdims=True) - xh * jnp.mean(dyg * xh, axis=-1, keepdims=True))
    return dx, jnp.sum(dy * xh, axis=0, keepdims=True), jnp.sum(dy, axis=0, keepdims=True)


def _silu(x):
    return x * _sigmoid(x)


def _silu_grad(x):
    s = _sigmoid(x)
    return s * (1.0 + x * (1.0 - s))


def _rope(x, cos, sa, sb):
    return x * cos + pltpu.roll(x, 96, 1) * sa + pltpu.roll(x, 32, 1) * sb


def _rope_t(d, cos, sa, sb):
    return d * cos - pltpu.roll(d, 96, 1) * sa - pltpu.roll(d, 32, 1) * sb


def _rows(ts, w):
    return pl.BlockSpec((ts, w), lambda i: (i, 0))


def _vec(w):
    return pl.BlockSpec((1, w), lambda i: (0, 0))


def _acc_init(i, *refs):
    @pl.when(i == 0)
    def _():
        for r in refs:
            r[...] = jnp.zeros_like(r)


def _rope_tables(pos, inv_freq):
    s = pos.shape[0]
    ts = _pick(s, (512, 256, 128))

    def body(p_ref, f_ref, c_ref, sa_ref, sb_ref):
        ang = p_ref[...].astype(F32) * f_ref[...]
        lane = lax.broadcasted_iota(jnp.int32, ang.shape, 1)
        c, sn = jnp.cos(ang), jnp.sin(ang)
        c_ref[...] = jnp.where(lane < QK_ROPE, c, 0.0)
        sa_ref[...] = jnp.where(lane < QK_ROPE // 2, -sn, 0.0)
        sb_ref[...] = jnp.where((lane >= QK_ROPE // 2) & (lane < QK_ROPE), sn, 0.0)

    out = jax.ShapeDtypeStruct((s, LANES), F32)
    return pl.pallas_call(
        body, name="rope_tables", grid=(s // ts,),
        in_specs=[_rows(ts, 1), _vec(LANES)],
        out_specs=[_rows(ts, LANES)] * 3, out_shape=[out] * 3,
        compiler_params=_params(("parallel",)),
    )(pos, inv_freq)


def _pre_fwd(x, g):
    s, d = x.shape
    ts = _pick(s, (256, 128))

    def body(x_ref, g_ref, h_ref):
        h_ref[...] = _rms(x_ref[...], g_ref[...]).astype(BF16)

    return pl.pallas_call(
        body, name="pre_fwd", grid=(s // ts,),
        in_specs=[_rows(ts, d), _vec(d)], out_specs=_rows(ts, d),
        out_shape=jax.ShapeDtypeStruct((s, d), BF16),
        compiler_params=_params(("parallel",)),
    )(x, g)


def _split_fwd(z, gq, gkv, tabs, c, ql, kvl):
    s, zw = z.shape
    ts = _pick(s, (256, 128))
    o_q, o_kv, o_kr = 2 * c, 2 * c + ql, 2 * c + ql + kvl

    def body(z_ref, gq_ref, gkv_ref, c_ref, sa_ref, sb_ref, u0_ref, qn_ref, kvn_ref, kpe_ref):
        u0_ref[...] = z_ref[:, 0:c] * _sigmoid(z_ref[:, c:2 * c])
        qn_ref[...] = _rms(z_ref[:, o_q:o_kv], gq_ref[...]).astype(BF16)
        kvn_ref[...] = _rms(z_ref[:, o_kv:o_kr], gkv_ref[...]).astype(BF16)
        kpe_ref[...] = _rope(z_ref[:, o_kr:o_kr + LANES], c_ref[...], sa_ref[...], sb_ref[...]).astype(BF16)

    return pl.pallas_call(
        body, name="split_fwd", grid=(s // ts,),
        in_specs=[_rows(ts, zw), _vec(ql), _vec(kvl)] + [_rows(ts, LANES)] * 3,
        out_specs=[_rows(ts, c), _rows(ts, ql), _rows(ts, kvl), _rows(ts, LANES)],
        out_shape=[jax.ShapeDtypeStruct((s, c), F32), jax.ShapeDtypeStruct((s, ql), BF16),
                   jax.ShapeDtypeStruct((s, kvl), BF16), jax.ShapeDtypeStruct((s, LANES), BF16)],
        compiler_params=_params(("parallel",)),
    )(z, gq, gkv, *tabs)


def _split_bwd(du0, z, dqn, dkvn, dkpe_h, gq, gkv, tabs, c, ql, kvl):
    s, zw = z.shape
    ts = _pick(s, (256, 128))
    o_q, o_kv, o_kr = 2 * c, 2 * c + ql, 2 * c + ql + kvl

    def body(du0_ref, z_ref, dqn_ref, dkvn_ref, dkh_ref, gq_ref, gkv_ref, c_ref, sa_ref, sb_ref,
             dz_ref, dgq_ref, dgkv_ref):
        _acc_init(pl.program_id(0), dgq_ref, dgkv_ref)
        du0 = du0_ref[...]
        a = z_ref[:, 0:c]
        sg = _sigmoid(z_ref[:, c:2 * c])
        dz_ref[:, 0:c] = (du0 * sg).astype(BF16)
        dz_ref[:, c:2 * c] = (du0 * a * sg * (1.0 - sg)).astype(BF16)
        dq, dgq = _rms_bwd(z_ref[:, o_q:o_kv], gq_ref[...], dqn_ref[...])
        dz_ref[:, o_q:o_kv] = dq.astype(BF16)
        dgq_ref[...] += dgq
        dkv, dgkv = _rms_bwd(z_ref[:, o_kv:o_kr], gkv_ref[...], dkvn_ref[...])
        dz_ref[:, o_kv:o_kr] = dkv.astype(BF16)
        dgkv_ref[...] += dgkv
        dk = dkh_ref[:, 0:LANES]
        for h in range(1, N_HEADS):
            dk = dk + dkh_ref[:, h * LANES:(h + 1) * LANES]
        dz_ref[:, o_kr:o_kr + LANES] = _rope_t(dk, c_ref[...], sa_ref[...], sb_ref[...]).astype(BF16)

    return pl.pallas_call(
        body, name="split_bwd", grid=(s // ts,),
        in_specs=[_rows(ts, c), _rows(ts, zw), _rows(ts, ql), _rows(ts, kvl), _rows(ts, N_HEADS * LANES),
                  _vec(ql), _vec(kvl)] + [_rows(ts, LANES)] * 3,
        out_specs=[_rows(ts, zw), _vec(ql), _vec(kvl)],
        out_shape=[jax.ShapeDtypeStruct((s, zw), BF16), jax.ShapeDtypeStruct((1, ql), F32),
                   jax.ShapeDtypeStruct((1, kvl), F32)],
        compiler_params=_params(("arbitrary",)),
    )(du0, z, dqn, dkvn, dkpe_h, gq, gkv, *tabs)


def _q_rope(qpre, tabs, transpose, out_dtype, name):
    s, w = qpre.shape
    ts = _pick(s, (256, 128))
    rot = _rope_t if transpose else _rope

    def body(q_ref, c_ref, sa_ref, sb_ref, o_ref):
        cs, sa, sb = c_ref[...], sa_ref[...], sb_ref[...]
        for h in range(N_HEADS):
            lo = h * HEAD_PAD
            o_ref[:, lo:lo + QK_NOPE] = q_ref[:, lo:lo + QK_NOPE].astype(out_dtype)
            o_ref[:, lo + QK_NOPE:lo + HEAD_PAD] = rot(q_ref[:, lo + QK_NOPE:lo + HEAD_PAD], cs, sa, sb).astype(out_dtype)

    return pl.pallas_call(
        body, name=name, grid=(s // ts,),
        in_specs=[_rows(ts, w)] + [_rows(ts, LANES)] * 3, out_specs=_rows(ts, w),
        out_shape=jax.ShapeDtypeStruct((s, w), out_dtype),
        compiler_params=_params(("parallel",)),
    )(qpre, *tabs)


def _conv_fwd(u0, w, b):
    s, c = u0.shape
    tc = LANES
    rc = _pick(s, (256, 128))

    def body(u_ref, w_ref, b_ref, o_ref, pad_ref):
        pad_ref[0:CONV_PAD, :] = jnp.zeros((CONV_PAD, tc), F32)
        pad_ref[CONV_PAD:CONV_PAD + s, :] = u_ref[...]
        for r in range(s // rc):
            acc = jnp.broadcast_to(b_ref[...], (rc, tc))
            for k in range(CONV_K):
                lo = r * rc + CONV_PAD - (CONV_K - 1) + k
                acc = acc + w_ref[k:k + 1, :] * pad_ref[lo:lo + rc, :]
            o_ref[r * rc:(r + 1) * rc, :] = acc

    col = lambda j: (0, j)
    return pl.pallas_call(
        body, name="conv_fwd", grid=(c // tc,),
        in_specs=[pl.BlockSpec((s, tc), col), pl.BlockSpec((CONV_K, tc), col), pl.BlockSpec((1, tc), col)],
        out_specs=pl.BlockSpec((s, tc), col),
        out_shape=jax.ShapeDtypeStruct((s, c), F32),
        scratch_shapes=[pltpu.VMEM((s + CONV_PAD, tc), F32)],
        compiler_params=_params(("parallel",)),
    )(u0, w, b)


def _conv_bwd(du1, u0, w):
    s, c = u0.shape
    tc = LANES
    rc = _pick(s, (256, 128))

    def body(d_ref, u_ref, w_ref, du_ref, dw_ref, db_ref, upad_ref, dpad_ref):
        upad_ref[0:CONV_PAD, :] = jnp.zeros((CONV_PAD, tc), F32)
        upad_ref[CONV_PAD:CONV_PAD + s, :] = u_ref[...]
        dpad_ref[0:s, :] = d_ref[...]
        dpad_ref[s:s + CONV_PAD, :] = jnp.zeros((CONV_PAD, tc), F32)
        for r in range(s // rc):
            acc = jnp.zeros((rc, tc), F32)
            for k in range(CONV_K):
                lo = r * rc + (CONV_K - 1) - k
                acc = acc + w_ref[k:k + 1, :] * dpad_ref[lo:lo + rc, :]
            du_ref[r * rc:(r + 1) * rc, :] = acc
        for k in range(CONV_K):
            acc8 = jnp.zeros((8, tc), F32)
            for r in range(s // rc):
                lo = r * rc + CONV_PAD - (CONV_K - 1) + k
                prod = d_ref[r * rc:(r + 1) * rc, :] * upad_ref[lo:lo + rc, :]
                acc8 = acc8 + jnp.sum(prod.reshape(rc // 8, 8, tc), axis=0)
            dw_ref[k:k + 1, :] = jnp.sum(acc8, axis=0, keepdims=True)
        db_ref[...] = jnp.sum(d_ref[...], axis=0, keepdims=True)

    col = lambda j: (0, j)
    return pl.pallas_call(
        body, name="conv_bwd", grid=(c // tc,),
        in_specs=[pl.BlockSpec((s, tc), col), pl.BlockSpec((s, tc), col), pl.BlockSpec((CONV_K, tc), col)],
        out_specs=[pl.BlockSpec((s, tc), col), pl.BlockSpec((CONV_K, tc), col), pl.BlockSpec((1, tc), col)],
        out_shape=[jax.ShapeDtypeStruct((s, c), F32), jax.ShapeDtypeStruct((CONV_K, c), F32),
                   jax.ShapeDtypeStruct((1, c), F32)],
        scratch_shapes=[pltpu.VMEM((s + CONV_PAD, tc), F32), pltpu.VMEM((s + CONV_PAD, tc), F32)],
        compiler_params=_params(("parallel",)),
    )(du1, u0, w)


def _causal_mask(sc, qi, kj, tq, tk):
    rows = qi * tq + lax.broadcasted_iota(jnp.int32, sc.shape, 0)
    cols = kj * tk + lax.broadcasted_iota(jnp.int32, sc.shape, 1)
    return jnp.where(cols <= rows, sc, NEG)


def _attn_fwd(q, kv, kpe):
    s = q.shape[0]
    tq = tk = _pick(s, (ATTN_BLOCK, 256, 128))
    reps = tk // LANES
    scale = QK_HEAD ** -0.5
    nt = (((1,), (1,)), ((), ()))

    def body(q_ref, kn_ref, v_ref, kpe_ref, o_ref, lse_ref, kf_ref, vb_ref, m_ref, l_ref, acc_ref):
        i = pl.program_id(1)

        @pl.when(i == 0)
        def _():
            kf_ref[:, 0:QK_NOPE] = kn_ref[...].astype(BF16)
            kf_ref[:, QK_NOPE:HEAD_PAD] = kpe_ref[...]
            vb_ref[...] = v_ref[...].astype(BF16)

        qb = q_ref[...]
        m_ref[...] = jnp.full((tq, LANES), NEG, F32)
        l_ref[...] = jnp.zeros((tq, LANES), F32)
        acc_ref[...] = jnp.zeros((tq, V_HEAD), F32)

        def block(j, diagonal):
            off = pl.multiple_of(j * tk, tk)
            sc = lax.dot_general(qb, kf_ref[pl.ds(off, tk), :], nt, preferred_element_type=F32) * scale
            if diagonal:
                sc = _causal_mask(sc, 0, 0, tq, tk)
            m_prev = m_ref[...]
            m_new = jnp.maximum(m_prev, jnp.max(sc, axis=1, keepdims=True))
            p = jnp.exp(sc - jnp.tile(m_new, (1, reps)))
            alpha = jnp.exp(m_prev - m_new)
            l_ref[...] = alpha * l_ref[...] + jnp.sum(p, axis=1, keepdims=True)
            acc_ref[...] = alpha * acc_ref[...] + jnp.dot(p.astype(BF16), vb_ref[pl.ds(off, tk), :],
                                                          preferred_element_type=F32)
            m_ref[...] = m_new

        def below_diagonal(j, carry):
            block(j, False)
            return carry

        lax.fori_loop(0, i, below_diagonal, 0)
        block(i, True)
        o_ref[...] = acc_ref[...] / l_ref[...]
        lse_ref[...] = m_ref[...] + jnp.log(l_ref[...])

    return pl.pallas_call(
        body, name="attn_fwd", grid=(N_HEADS, s // tq),
        in_specs=[pl.BlockSpec((tq, HEAD_PAD), lambda h, i: (i, h)),
                  pl.BlockSpec((s, QK_NOPE), lambda h, i: (0, 2 * h)),
                  pl.BlockSpec((s, V_HEAD), lambda h, i: (0, 2 * h + 1)),
                  pl.BlockSpec((s, LANES), lambda h, i: (0, 0))],
        out_specs=[pl.BlockSpec((tq, V_HEAD), lambda h, i: (i, h)),
                   pl.BlockSpec((tq, LANES), lambda h, i: (i, h))],
        out_shape=[jax.ShapeDtypeStruct((s, N_HEADS * V_HEAD), F32),
                   jax.ShapeDtypeStruct((s, N_HEADS * LANES), F32)],
        scratch_shapes=[pltpu.VMEM((s, HEAD_PAD), BF16), pltpu.VMEM((s, V_HEAD), BF16),
                        pltpu.VMEM((tq, LANES), F32), pltpu.VMEM((tq, LANES), F32), pltpu.VMEM((tq, V_HEAD), F32)],
        compiler_params=_params(("parallel", "arbitrary")),
    )(q, kv, kv, kpe)


def _attn_bwd(q, kv, kpe, o, do, lse):
    s = q.shape[0]
    tq = tk = _pick(s, (ATTN_BLOCK, 256, 128))
    nq = s // tq
    reps = tk // LANES
    scale = QK_HEAD ** -0.5
    nt = (((1,), (1,)), ((), ()))
    tn = (((0,), (0,)), ((), ()))

    def body(q_ref, kn_ref, v_ref, kpe_ref, o_ref, do_ref, lse_ref, dq_ref, dkv_ref, dkpe_ref,
             kf_ref, vb_ref, dk_ref, dv_ref):
        j = pl.program_id(1)

        @pl.when(j == 0)
        def _():
            dq_ref[...] = jnp.zeros_like(dq_ref)

        kf_ref[:, 0:QK_NOPE] = kn_ref[...].astype(BF16)
        kf_ref[:, QK_NOPE:HEAD_PAD] = kpe_ref[...]
        vb_ref[...] = v_ref[...].astype(BF16)
        dk_ref[...] = jnp.zeros_like(dk_ref)
        dv_ref[...] = jnp.zeros_like(dv_ref)

        def block(i, diagonal):
            off = pl.multiple_of(i * tq, tq)
            qb = q_ref[pl.ds(off, tq), :]
            dob = do_ref[pl.ds(off, tq), :]
            delta = jnp.sum(dob * o_ref[pl.ds(off, tq), :], axis=1, keepdims=True)
            sc = lax.dot_general(qb, kf_ref[...], nt, preferred_element_type=F32) * scale
            if diagonal:
                sc = _causal_mask(sc, 0, 0, tq, tk)
            p = jnp.exp(sc - jnp.tile(lse_ref[pl.ds(off, tq), :], (1, reps)))
            dob16 = dob.astype(BF16)
            dv_ref[...] += lax.dot_general(p.astype(BF16), dob16, tn, preferred_element_type=F32)
            dp = lax.dot_general(dob16, vb_ref[...], nt, preferred_element_type=F32)
            ds = (p * (dp - delta) * scale).astype(BF16)
            dq_ref[pl.ds(off, tq), :] += jnp.dot(ds, kf_ref[...], preferred_element_type=F32)
            dk_ref[...] += lax.dot_general(ds, qb, tn, preferred_element_type=F32)

        def above_diagonal(i, carry):
            block(i, False)
            return carry

        block(j, True)
        lax.fori_loop(j + 1, nq, above_diagonal, 0)
        dkv_ref[:, 0:QK_NOPE] = dk_ref[:, 0:QK_NOPE]
        dkv_ref[:, QK_NOPE:HEAD_PAD] = dv_ref[...]
        dkpe_ref[...] = dk_ref[:, QK_NOPE:HEAD_PAD]

    head_rows = lambda w: pl.BlockSpec((s, w), lambda h, j: (0, h))
    return pl.pallas_call(
        body, name="attn_bwd", grid=(N_HEADS, s // tk),
        in_specs=[head_rows(HEAD_PAD),
                  pl.BlockSpec((tk, QK_NOPE), lambda h, j: (j, 2 * h)),
                  pl.BlockSpec((tk, V_HEAD), lambda h, j: (j, 2 * h + 1)),
                  pl.BlockSpec((tk, LANES), lambda h, j: (j, 0)),
                  head_rows(V_HEAD), head_rows(V_HEAD), head_rows(LANES)],
        out_specs=[head_rows(HEAD_PAD),
                   pl.BlockSpec((tk, HEAD_PAD), lambda h, j: (j, h)),
                   pl.BlockSpec((tk, LANES), lambda h, j: (j, h))],
        out_shape=[jax.ShapeDtypeStruct((s, N_HEADS * HEAD_PAD), F32),
                   jax.ShapeDtypeStruct((s, N_HEADS * HEAD_PAD), F32),
                   jax.ShapeDtypeStruct((s, N_HEADS * LANES), F32)],
        scratch_shapes=[pltpu.VMEM((tk, HEAD_PAD), BF16), pltpu.VMEM((tk, V_HEAD), BF16),
                        pltpu.VMEM((tk, HEAD_PAD), F32), pltpu.VMEM((tk, V_HEAD), F32)],
        compiler_params=_params(("parallel", "arbitrary")),
    )(q, kv, kv, kpe, o, do, lse)


def _mix_fwd(u1, lng, lnb, gcon, attn, gattn):
    s, c = u1.shape
    ac = attn.shape[1]
    ts = _pick(s, (256, 128))

    def body(u_ref, lg_ref, lb_ref, gc_ref, a_ref, ga_ref, o_ref):
        t3 = _silu(_ln(u_ref[...], lg_ref[...], lb_ref[...]))
        o_ref[:, 0:c] = _rms(t3, gc_ref[...]).astype(BF16)
        o_ref[:, c:c + ac] = _rms(a_ref[...], ga_ref[...]).astype(BF16)

    return pl.pallas_call(
        body, name="mix_fwd", grid=(s // ts,),
        in_specs=[_rows(ts, c), _vec(c), _vec(c), _vec(c), _rows(ts, ac), _vec(ac)],
        out_specs=_rows(ts, c + ac),
        out_shape=jax.ShapeDtypeStruct((s, c + ac), BF16),
        compiler_params=_params(("parallel",)),
    )(u1, lng, lnb, gcon, attn, gattn)


def _mix_bwd(dmixin, u1, lng, lnb, gcon, attn, gattn):
    s, c = u1.shape
    ac = attn.shape[1]
    ts = _pick(s, (256, 128))

    def body(d_ref, u_ref, lg_ref, lb_ref, gc_ref, a_ref, ga_ref,
             du_ref, da_ref, dlg_ref, dlb_ref, dgc_ref, dga_ref):
        _acc_init(pl.program_id(0), dlg_ref, dlb_ref, dgc_ref, dga_ref)
        u = u_ref[...]
        t2 = _ln(u, lg_ref[...], lb_ref[...])
        dt3, dgc = _rms_bwd(_silu(t2), gc_ref[...], d_ref[:, 0:c])
        du, dlg, dlb = _ln_bwd(u, lg_ref[...], dt3 * _silu_grad(t2))
        du_ref[...] = du
        dlg_ref[...] += dlg
        dlb_ref[...] += dlb
        dgc_ref[...] += dgc
        da, dga = _rms_bwd(a_ref[...], ga_ref[...], d_ref[:, c:c + ac])
        da_ref[...] = da
        dga_ref[...] += dga

    return pl.pallas_call(
        body, name="mix_bwd", grid=(s // ts,),
        in_specs=[_rows(ts, c + ac), _rows(ts, c), _vec(c), _vec(c), _vec(c), _rows(ts, ac), _vec(ac)],
        out_specs=[_rows(ts, c), _rows(ts, ac), _vec(c), _vec(c), _vec(c), _vec(ac)],
        out_shape=[jax.ShapeDtypeStruct((s, c), F32), jax.ShapeDtypeStruct((s, ac), F32),
                   jax.ShapeDtypeStruct((1, c), F32), jax.ShapeDtypeStruct((1, c), F32),
                   jax.ShapeDtypeStruct((1, c), F32), jax.ShapeDtypeStruct((1, ac), F32)],
        compiler_params=_params(("arbitrary",)),
    )(dmixin, u1, lng, lnb, gcon, attn, gattn)


def _post_mix_fwd(x, mix, gpost, gpre):
    s, d = x.shape
    ts = _pick(s, (256, 128))

    def body(x_ref, m_ref, gp_ref, gf_ref, x1_ref, hf_ref):
        x1 = x_ref[...] + _rms(m_ref[...], gp_ref[...])
        x1_ref[...] = x1
        hf_ref[...] = _rms(x1, gf_ref[...]).astype(BF16)

    return pl.pallas_call(
        body, name="post_mix_fwd", grid=(s // ts,),
        in_specs=[_rows(ts, d), _rows(ts, d), _vec(d), _vec(d)],
        out_specs=[_rows(ts, d), _rows(ts, d)],
        out_shape=[jax.ShapeDtypeStruct((s, d), F32), jax.ShapeDtypeStruct((s, d), BF16)],
        compiler_params=_params(("parallel",)),
    )(x, mix, gpost, gpre)


def _post_mix_bwd(dy, dhf, x1, gpre, mix, gpost):
    s, d = x1.shape
    ts = _pick(s, (256, 128))

    def body(dy_ref, dh_ref, x1_ref, gf_ref, m_ref, gp_ref, dx1_ref, dm_ref, dgf_ref, dgp_ref):
        _acc_init(pl.program_id(0), dgf_ref, dgp_ref)
        dxa, dgf = _rms_bwd(x1_ref[...], gf_ref[...], dh_ref[...])
        dx1 = dy_ref[...] + dxa
        dx1_ref[...] = dx1
        dgf_ref[...] += dgf
        dm, dgp = _rms_bwd(m_ref[...], gp_ref[...], dx1)
        dm_ref[...] = dm.astype(BF16)
        dgp_ref[...] += dgp

    return pl.pallas_call(
        body, name="post_mix_bwd", grid=(s // ts,),
        in_specs=[_rows(ts, d), _rows(ts, d), _rows(ts, d), _vec(d), _rows(ts, d), _vec(d)],
        out_specs=[_rows(ts, d), _rows(ts, d), _vec(d), _vec(d)],
        out_shape=[jax.ShapeDtypeStruct((s, d), F32), jax.ShapeDtypeStruct((s, d), BF16),
                   jax.ShapeDtypeStruct((1, d), F32), jax.ShapeDtypeStruct((1, d), F32)],
        compiler_params=_params(("arbitrary",)),
    )(dy, dhf, x1, gpre, mix, gpost)


def _ffn_up(hf, wg, wu):
    s, d = hf.shape
    nsh, _, fs = wg.shape
    tm = _pick(s, (1024, 512, 256, 128))

    def body(h_ref, wg_ref, wu_ref, g_ref, u_ref, a_ref):
        h = h_ref[...]
        g = jnp.dot(h, wg_ref[...], preferred_element_type=F32)
        u = jnp.dot(h, wu_ref[...], preferred_element_type=F32)
        g_ref[...] = g
        u_ref[...] = u
        a_ref[...] = (_silu(g) * u).astype(BF16)

    w_spec = pl.BlockSpec((None, d, fs), lambda i, j: (j, 0, 0))
    o_spec = pl.BlockSpec((None, tm, fs), lambda i, j: (j, i, 0))
    return pl.pallas_call(
        body, name="ffn_up", grid=(s // tm, nsh),
        in_specs=[pl.BlockSpec((tm, d), lambda i, j: (i, 0)), w_spec, w_spec],
        out_specs=[o_spec] * 3,
        out_shape=[jax.ShapeDtypeStruct((nsh, s, fs), F32)] * 2 + [jax.ShapeDtypeStruct((nsh, s, fs), BF16)],
        compiler_params=_params(("parallel", "parallel")),
    )(hf, wg, wu)


def _ffn_down(act, wd):
    nsh, s, fs = act.shape
    d = wd.shape[2]
    tm = _pick(s, (1024, 512, 256, 128))
    tn = _pick(d, (512, 256, 128))

    def body(a_ref, w_ref, o_ref):
        acc = jnp.dot(a_ref[0], w_ref[0], preferred_element_type=F32)
        for j in range(1, nsh):
            acc = acc + jnp.dot(a_ref[j], w_ref[j], preferred_element_type=F32)
        o_ref[...] = acc

    return pl.pallas_call(
        body, name="ffn_down", grid=(s // tm, d // tn),
        in_specs=[pl.BlockSpec((nsh, tm, fs), lambda i, j: (0, i, 0)),
                  pl.BlockSpec((nsh, fs, tn), lambda i, j: (0, 0, j))],
        out_specs=pl.BlockSpec((tm, tn), lambda i, j: (i, j)),
        out_shape=jax.ShapeDtypeStruct((s, d), F32),
        compiler_params=_params(("parallel", "parallel")),
    )(act, wd)


def _ffn_down_bwd(dff, wd, gate, up):
    s, d = dff.shape
    nsh, fs, _ = wd.shape
    tm = _pick(s, (1024, 512, 256, 128))
    nt = (((1,), (1,)), ((), ()))

    def body(d_ref, w_ref, g_ref, u_ref, dg_ref, du_ref):
        dact = lax.dot_general(d_ref[...], w_ref[...], nt, preferred_element_type=F32)
        g = g_ref[...]
        dg_ref[...] = (dact * u_ref[...] * _silu_grad(g)).astype(BF16)
        du_ref[...] = (dact * _silu(g)).astype(BF16)

    h_spec = pl.BlockSpec((None, tm, fs), lambda i, j: (j, i, 0))
    return pl.pallas_call(
        body, name="ffn_down_bwd", grid=(s // tm, nsh),
        in_specs=[pl.BlockSpec((tm, d), lambda i, j: (i, 0)),
                  pl.BlockSpec((None, fs, d), lambda i, j: (j, 0, 0)), h_spec, h_spec],
        out_specs=[h_spec] * 2,
        out_shape=[jax.ShapeDtypeStruct((nsh, s, fs), BF16)] * 2,
        compiler_params=_params(("parallel", "parallel")),
    )(dff, wd, gate, up)


def _ffn_dw_down(act, dff):
    nsh, s, fs = act.shape
    d = dff.shape[1]
    tn = (((0,), (0,)), ((), ()))

    def body(a_ref, d_ref, o_ref):
        o_ref[...] = lax.dot_general(a_ref[...], d_ref[...], tn, preferred_element_type=F32).astype(BF16)

    return pl.pallas_call(
        body, name="ffn_dw_down", grid=(nsh,),
        in_specs=[pl.BlockSpec((None, s, fs), lambda j: (j, 0, 0)), pl.BlockSpec((s, d), lambda j: (0, 0))],
        out_specs=pl.BlockSpec((None, fs, d), lambda j: (j, 0, 0)),
        out_shape=jax.ShapeDtypeStruct((nsh, fs, d), BF16),
        compiler_params=_params(("parallel",)),
    )(act, dff)


def _ffn_dw_up(hf, dh, name):
    s, d = hf.shape
    nsh, _, fs = dh.shape
    tn = (((0,), (0,)), ((), ()))

    def body(h_ref, d_ref, o_ref):
        o_ref[...] = lax.dot_general(h_ref[...], d_ref[...], tn, preferred_element_type=F32).astype(BF16)

    return pl.pallas_call(
        body, name=name, grid=(nsh,),
        in_specs=[pl.BlockSpec((s, d), lambda j: (0, 0)), pl.BlockSpec((None, s, fs), lambda j: (j, 0, 0))],
        out_specs=pl.BlockSpec((None, d, fs), lambda j: (j, 0, 0)),
        out_shape=jax.ShapeDtypeStruct((nsh, d, fs), BF16),
        compiler_params=_params(("parallel",)),
    )(hf, dh)


def _ffn_up_bwd(dh, w, name, add=None):
    nsh, s, fs = dh.shape
    d = w.shape[1]
    tm = _pick(s, (1024, 512, 256, 128))
    tn = _pick(d, (256, 128))
    nt = (((1,), (1,)), ((), ()))

    def body(a_ref, w_ref, *rest):
        acc = lax.dot_general(a_ref[0], w_ref[0], nt, preferred_element_type=F32)
        for j in range(1, nsh):
            acc = acc + lax.dot_general(a_ref[j], w_ref[j], nt, preferred_element_type=F32)
        if add is not None:
            acc = acc + rest[0][...]
        rest[-1][...] = acc

    o_spec = pl.BlockSpec((tm, tn), lambda i, j: (i, j))
    extra = [] if add is None else [add]
    return pl.pallas_call(
        body, name=name, grid=(s // tm, d // tn),
        in_specs=[pl.BlockSpec((nsh, tm, fs), lambda i, j: (0, i, 0)),
                  pl.BlockSpec((nsh, tn, fs), lambda i, j: (0, j, 0))] + [o_spec] * len(extra),
        out_specs=o_spec,
        out_shape=jax.ShapeDtypeStruct((s, d), F32),
        compiler_params=_params(("parallel", "parallel")),
    )(dh, w, *extra)


def _final(ff, x1, tgt, g):
    s, d = x1.shape
    ts = _pick(s, (256, 128))

    def body(ff_ref, x1_ref, t_ref, g_ref, loss_ref, dy_ref, dff_ref, dg_ref):
        _acc_init(pl.program_id(0), loss_ref, dg_ref)
        ff_v = ff_ref[...]
        err = x1_ref[...] + _rms(ff_v, g_ref[...]) - t_ref[...]
        tok = jnp.mean(err * err, axis=-1, keepdims=True)
        loss_ref[...] += 0.5 * jnp.sum(tok, axis=0, keepdims=True)
        dy = err * (1.0 / d)
        dy_ref[...] = dy
        dff, dg = _rms_bwd(ff_v, g_ref[...], dy)
        dff_ref[...] = dff.astype(BF16)
        dg_ref[...] += dg

    return pl.pallas_call(
        body, name="final", grid=(s // ts,),
        in_specs=[_rows(ts, d), _rows(ts, d), _rows(ts, d), _vec(d)],
        out_specs=[_vec(LANES), _rows(ts, d), _rows(ts, d), _vec(d)],
        out_shape=[jax.ShapeDtypeStruct((1, LANES), F32), jax.ShapeDtypeStruct((s, d), F32),
                   jax.ShapeDtypeStruct((s, d), BF16), jax.ShapeDtypeStruct((1, d), F32)],
        compiler_params=_params(("arbitrary",)),
    )(ff, x1, tgt, g)


def _pre_bwd(dx1, dh, x, g):
    s, d = x.shape
    ts = _pick(s, (256, 128))

    def body(dx1_ref, dh_ref, x_ref, g_ref, dx_ref, dg_ref):
        _acc_init(pl.program_id(0), dg_ref)
        dxa, dg = _rms_bwd(x_ref[...], g_ref[...], dh_ref[...])
        dx_ref[...] = dx1_ref[...] + dxa
        dg_ref[...] += dg

    return pl.pallas_call(
        body, name="pre_bwd", grid=(s // ts,),
        in_specs=[_rows(ts, d), _rows(ts, d), _rows(ts, d), _vec(d)],
        out_specs=[_rows(ts, d), _vec(d)],
        out_shape=[jax.ShapeDtypeStruct((s, d), F32), jax.ShapeDtypeStruct((1, d), F32)],
        compiler_params=_params(("arbitrary",)),
    )(dx1, dh, x, g)


def _local_step(x, pos, tgt, vecs, w_in_p, w_uq_p, w_ukv, conv_w, w_out, w_gate, w_up, w_down):
    c = vecs["conv_b"].shape[1]
    ql = vecs["q_norm"].shape[1]
    kvl = vecs["kv_norm"].shape[1]
    half = jnp.arange(0, QK_ROPE, 2, dtype=F32)
    freq = ROPE_THETA ** (-half / QK_ROPE)
    inv_freq = jnp.concatenate([freq, freq, jnp.zeros((LANES - QK_ROPE,), F32)])[None, :]
    tabs = _rope_tables(pos, inv_freq)

    h = _pre_fwd(x, vecs["pre_mix_norm"])
    z = _mm(h, w_in_p, "nn", "mm_z")
    u0, qn, kvn, kpe = _split_fwd(z, vecs["q_norm"], vecs["kv_norm"], tabs, c, ql, kvl)
    u1 = _conv_fwd(u0, conv_w, vecs["conv_b"])
    q = _q_rope(_mm(qn, w_uq_p, "nn", "mm_q"), tabs, False, BF16, "q_rope")
    kv = _mm(kvn, w_ukv, "nn", "mm_kv")
    attn, lse = _attn_fwd(q, kv, kpe)
    mixin = _mix_fwd(u1, vecs["conv_ln_g"], vecs["conv_ln_b"], vecs["conv_out_norm"], attn, vecs["attn_out_norm"])
    mix = _mm(mixin, w_out, "nn", "mm_mix")
    x1, hf = _post_mix_fwd(x, mix, vecs["post_mix_norm"], vecs["pre_ffn_norm"])
    gate, up, act = _ffn_up(hf, w_gate, w_up)
    ff = _ffn_down(act, w_down)
    loss, dy, dff, d_post_ffn = _final(ff, x1, tgt, vecs["post_ffn_norm"])

    g = {"post_ffn_norm": d_post_ffn}
    g["w_down"] = _ffn_dw_down(act, dff)
    dgate, dup = _ffn_down_bwd(dff, w_down, gate, up)
    dhf = _ffn_up_bwd(dup, w_up, "ffn_dhf_up", add=_ffn_up_bwd(dgate, w_gate, "ffn_dhf_gate"))
    g["w_gate"] = _ffn_dw_up(hf, dgate, "ffn_dw_gate")
    g["w_up"] = _ffn_dw_up(hf, dup, "ffn_dw_up")
    dx1, dmix, g["pre_ffn_norm"], g["post_mix_norm"] = _post_mix_bwd(
        dy, dhf, x1, vecs["pre_ffn_norm"], mix, vecs["post_mix_norm"])
    dmixin = _mm(dmix, w_out, "nt", "mm_dmixin")
    g["w_out"] = _mm(mixin, dmix, "tn", "mm_dw_out", BF16)
    du1, dattn, g["conv_ln_g"], g["conv_ln_b"], g["conv_out_norm"], g["attn_out_norm"] = _mix_bwd(
        dmixin, u1, vecs["conv_ln_g"], vecs["conv_ln_b"], vecs["conv_out_norm"], attn, vecs["attn_out_norm"])
    du0, g["conv_w"], g["conv_b"] = _conv_bwd(du1, u0, conv_w)
    dq, dkv, dkpe_h = _attn_bwd(q, kv, kpe, attn, dattn, lse)
    dqpre = _q_rope(dq, tabs, True, BF16, "q_rope_bwd")
    dqn = _mm(dqpre, w_uq_p, "nt", "mm_dqn")
    g["w_uq_p"] = _mm(qn, dqpre, "tn", "mm_dw_uq", BF16)
    dkvn = _mm(dkv, w_ukv, "nt", "mm_dkvn")
    g["w_ukv"] = _mm(kvn, dkv, "tn", "mm_dw_ukv", BF16)
    dz, g["q_norm"], g["kv_norm"] = _split_bwd(du0, z, dqn, dkvn, dkpe_h, vecs["q_norm"], vecs["kv_norm"], tabs, c, ql, kvl)
    dh = _mm(dz, w_in_p, "nt", "mm_dh")
    g["w_in_p"] = _mm(h, dz, "tn", "mm_dw_in", BF16)
    grad_x, g["pre_mix_norm"] = _pre_bwd(dx1, dh, x, vecs["pre_mix_norm"])
    return loss, grad_x, g


def _my_index():
    return 4 * lax.axis_index("x") + 2 * lax.axis_index("y") + lax.axis_index("c")


def _coords(idx):
    return ((idx >> 2) & 1, (idx >> 1) & 1, idx & 1)


def _place():
    x, y, c = lax.axis_index("x"), lax.axis_index("y"), lax.axis_index("c")
    return (x, y, c), (x, y, 1 - c), [(1 - x, y), (x, 1 - y), (1 - x, 1 - y)]


def _comm_call(body, name, arrays, out_shapes, n_sems):
    n = len(arrays)
    any_spec = pl.BlockSpec(memory_space=pl.ANY)
    return pl.pallas_call(
        body, name=name,
        in_specs=[any_spec] * n, out_specs=[any_spec] * len(out_shapes), out_shape=out_shapes,
        scratch_shapes=[pltpu.SemaphoreType.DMA((n, n_sems))] * 3,
        compiler_params=pltpu.CompilerParams(has_side_effects=True),
    )(*arrays)


def _gather_direct(arrays, name):
    n = len(arrays)

    def body(*refs):
        ins, outs = refs[:n], refs[n:2 * n]
        send_sems, recv_sems, local_sems = refs[2 * n:]
        me = _my_index()
        local = [pltpu.make_async_copy(ins[k], outs[k].at[me], local_sems.at[k, 0]) for k in range(n)]
        for cp in local:
            cp.start()

        def copy(k, p, slot):
            return pltpu.make_async_remote_copy(
                src_ref=ins[k], dst_ref=outs[k].at[slot], send_sem=send_sems.at[k, p - 1],
                recv_sem=recv_sems.at[k, p - 1], device_id=_coords(me ^ p), device_id_type=MESH)

        sends = [copy(k, p, me) for p in range(1, N_DEV) for k in range(n)]
        for cp in sends:
            cp.start()
        for p in range(1, N_DEV):
            for k in range(n):
                copy(k, p, me ^ p).wait_recv()
        for cp in sends:
            cp.wait_send()
        for cp in local:
            cp.wait()

    out_shapes = [jax.ShapeDtypeStruct((N_DEV,) + a.shape, a.dtype) for a in arrays]
    return _comm_call(body, name, arrays, out_shapes, N_DEV - 1)


def _gather_weights(shards):
    n = len(shards)

    def body(*refs):
        ins, outs = refs[:n], refs[n:2 * n]
        send_sems, recv_sems, local_sems = refs[2 * n:]
        (x, y, c), sib, chips = _place()
        me = 4 * x + 2 * y + c

        def slot(chip, core):
            return 4 * chip[0] + 2 * chip[1] + core

        def copy(k, i, block, to, src=None):
            rows = outs[k].at[block]
            return pltpu.make_async_remote_copy(
                src_ref=rows if src is None else src, dst_ref=rows, send_sem=send_sems.at[k, i],
                recv_sem=recv_sems.at[k, i], device_id=to, device_id_type=MESH)

        sends = []
        for k in range(n):
            sends.append(copy(k, 0, me, sib, src=ins[k]))
            sends += [copy(k, 1 + j, me, (*chip, c), src=ins[k]) for j, chip in enumerate(chips)]
        for cp in sends:
            cp.start()
        for j, chip in enumerate(chips):
            for k in range(n):
                copy(k, 1 + j, slot(chip, c), (x, y, c)).wait_recv()
                fwd = copy(k, 4 + j, slot(chip, c), sib)
                fwd.start()
                sends.append(fwd)
        for k in range(n):
            copy(k, 0, slot((x, y), 1 - c), (x, y, c)).wait_recv()
            for j, chip in enumerate(chips):
                copy(k, 4 + j, slot(chip, 1 - c), (x, y, c)).wait_recv()
        for cp in sends:
            cp.wait_send()

    out_shapes = [jax.ShapeDtypeStruct((N_DEV,) + a.shape, a.dtype) for a in shards]
    got = _comm_call(body, "gather_weights", shards, out_shapes, 7)
    me = _my_index()
    return [lax.dynamic_update_slice(g, s[None], (me,) + (0,) * s.ndim) for g, s in zip(got, shards)]


def _scatter_to_sibling(blocks):
    n = len(blocks)

    def body(*refs):
        ins, theirs = refs[:n], refs[n:2 * n]
        send_sems, recv_sems, _ = refs[2 * n:]
        (x, y, c), sib, _ = _place()
        sends = []
        for k in range(n):
            for q in range(4):
                sends.append(pltpu.make_async_remote_copy(
                    src_ref=ins[k].at[2 * q + 1 - c], dst_ref=theirs[k].at[q], send_sem=send_sems.at[k, q],
                    recv_sem=recv_sems.at[k, q], device_id=sib, device_id_type=MESH))
        for cp in sends:
            cp.start()
        for cp in sends:
            cp.wait_recv()
        for cp in sends:
            cp.wait_send()

    out_shapes = [jax.ShapeDtypeStruct((4,) + a.shape[1:], a.dtype) for a in blocks]
    return _comm_call(body, "scatter_to_sibling", blocks, out_shapes, 4)


def _scatter_to_chips(pairs):
    n = len(pairs)

    def body(*refs):
        ins, outs = refs[:n], refs[n:2 * n]
        send_sems, recv_sems, _ = refs[2 * n:]
        (x, y, c), _, chips = _place()
        sends = []
        for k in range(n):
            for j, chip in enumerate(chips):
                sends.append(pltpu.make_async_remote_copy(
                    src_ref=ins[k].at[2 * chip[0] + chip[1]], dst_ref=outs[k].at[j], send_sem=send_sems.at[k, j],
                    recv_sem=recv_sems.at[k, j], device_id=(*chip, c), device_id_type=MESH))
        for cp in sends:
            cp.start()
        for cp in sends:
            cp.wait_recv()
        for cp in sends:
            cp.wait_send()

    out_shapes = [jax.ShapeDtypeStruct((3,) + a.shape[1:], a.dtype) for a in pairs]
    return _comm_call(body, "scatter_to_chips", pairs, out_shapes, 3)


def _pair_sum(core, blocks, theirs, name):
    q, r, c = theirs.shape
    tr = _pick(r, (256, 128, 64, 32, 16))

    def body(core_ref, a_ref, b_ref, o_ref):
        o_ref[...] = (a_ref[...].astype(F32) + b_ref[...].astype(F32)).astype(o_ref.dtype)

    blk = pl.BlockSpec((1, tr, c), lambda i, j, core_ref: (i, j, 0))
    mine = pl.BlockSpec((1, tr, c), lambda i, j, core_ref: (2 * i + core_ref[0], j, 0))
    return pl.pallas_call(
        body, name=name,
        grid_spec=pltpu.PrefetchScalarGridSpec(num_scalar_prefetch=1, grid=(q, r // tr),
                                               in_specs=[mine, blk], out_specs=blk),
        out_shape=jax.ShapeDtypeStruct(theirs.shape, theirs.dtype),
        compiler_params=_params(("parallel", "parallel")),
    )(core, blocks, theirs)


def _reduce_adamw(parts, w, m, v, name, own=None, own_slot=None):
    r, c = w.shape
    n_parts = parts.shape[0]
    tr = _pick(r, (256, 128, 64, 32, 16))
    c1 = 1.0 - ADAM_B1
    c2 = 1.0 - ADAM_B2
    bc1 = 1.0 - ADAM_B1 ** ADAM_STEP
    bc2 = 1.0 - ADAM_B2 ** ADAM_STEP

    def body(*refs):
        if own is None:
            p_ref, w_ref, m_ref, v_ref, g_ref, d_ref, nm_ref, nv_ref = refs
            g = p_ref[0].astype(F32)
            first = 1
        else:
            _, o_ref, p_ref, w_ref, m_ref, v_ref, g_ref, d_ref, nm_ref, nv_ref = refs
            g = o_ref[0].astype(F32)
            first = 0
        for j in range(first, n_parts):
            g = g + p_ref[j].astype(F32)
        nm = ADAM_B1 * m_ref[...] + c1 * g
        nv = ADAM_B2 * v_ref[...] + c2 * (g * g)
        g_ref[...] = g
        nm_ref[...] = nm
        nv_ref[...] = nv
        d_ref[...] = -ADAM_LR * ((nm / bc1) / (jnp.sqrt(nv / bc2) + ADAM_EPS) + ADAM_WD * w_ref[...])

    out = jax.ShapeDtypeStruct((r, c), F32)
    if own is None:
        blk = pl.BlockSpec((tr, c), lambda i: (i, 0))
        return pl.pallas_call(
            body, name=name, grid=(r // tr,),
            in_specs=[pl.BlockSpec((n_parts, tr, c), lambda i: (0, i, 0)), blk, blk, blk],
            out_specs=[blk] * 4, out_shape=[out] * 4,
            compiler_params=_params(("parallel",)),
        )(parts, w, m, v)
    blk = pl.BlockSpec((tr, c), lambda i, slot_ref: (i, 0))
    return pl.pallas_call(
        body, name=name,
        grid_spec=pltpu.PrefetchScalarGridSpec(
            num_scalar_prefetch=1, grid=(r // tr,),
            in_specs=[pl.BlockSpec((1, tr, c), lambda i, slot_ref: (slot_ref[0], i, 0)),
                      pl.BlockSpec((n_parts, tr, c), lambda i, slot_ref: (0, i, 0)), blk, blk, blk],
            out_specs=[blk] * 4),
        out_shape=[out] * 4,
        compiler_params=_params(("parallel",)),
    )(own_slot, own, parts, w, m, v)


_BIG = ("w_in", "w_uq", "w_ukv", "conv_w", "w_out", "w_gate", "w_up", "w_down")
_SMALL = ("pre_mix_norm", "q_norm", "kv_norm", "conv_b", "conv_ln_g", "conv_ln_b", "conv_out_norm",
          "attn_out_norm", "post_mix_norm", "pre_ffn_norm", "post_ffn_norm")
_ORDER = ("pre_mix_norm", "w_in", "q_norm", "w_uq", "kv_norm", "w_ukv", "conv_w", "conv_b", "conv_ln_g",
          "conv_ln_b", "conv_out_norm", "attn_out_norm", "w_out", "post_mix_norm", "pre_ffn_norm", "w_gate",
          "w_up", "w_down", "post_ffn_norm")


def _cols_from_shards(g):
    return jnp.transpose(g, (1, 0, 2)).reshape(g.shape[1], N_DEV * g.shape[2])


def _cols_to_shards(w):
    k, n8 = w.shape
    return jnp.transpose(w.reshape(k, N_DEV, n8 // N_DEV), (1, 0, 2))


def _step(x, positions, loss_target, w, m, v):
    s, d = x.shape[1], x.shape[2]
    x2, tgt = x[0], loss_target[0]
    pos = positions.reshape(s, 1)
    vecs = {n: w[n] for n in _SMALL}

    shards = [w[n][0] if n == "conv_w" else w[n][0].astype(BF16) for n in _BIG]
    gath = dict(zip(_BIG, _gather_weights(shards)))
    w_in_f = _cols_from_shards(gath["w_in"])
    zw = w_in_f.shape[1] + LANES - QK_ROPE
    w_in_p = jnp.pad(w_in_f, ((0, 0), (0, zw - w_in_f.shape[1])))
    w_uq_p = _cols_from_shards(jnp.pad(gath["w_uq"], ((0, 0), (0, 0), (0, HEAD_PAD - QK_HEAD))))
    w_ukv = _cols_from_shards(gath["w_ukv"])
    conv_w_f = _cols_from_shards(gath["conv_w"])
    w_out_f = gath["w_out"].reshape(-1, d)

    loss, grad_x, g = _local_step(x2, pos, tgt, vecs, w_in_p, w_uq_p, w_ukv, conv_w_f, w_out_f,
                                  gath["w_gate"], gath["w_up"], gath["w_down"])

    blocks = {
        "w_in": _cols_to_shards(g["w_in_p"][:, :w_in_f.shape[1]]),
        "w_uq": _cols_to_shards(g["w_uq_p"])[:, :, :QK_HEAD],
        "w_ukv": _cols_to_shards(g["w_ukv"]),
        "conv_w": _cols_to_shards(g["conv_w"]),
        "w_out": g["w_out"].reshape(N_DEV, -1, d),
        "w_gate": g["w_gate"],
        "w_up": g["w_up"],
        "w_down": g["w_down"],
    }
    theirs = _scatter_to_sibling([blocks[n] for n in _BIG])
    core = lax.axis_index("c").astype(jnp.int32).reshape(1)
    pairs = [_pair_sum(core, blocks[n], b, "pair_sum_" + n) for n, b in zip(_BIG, theirs)]
    recv = dict(zip(_BIG, _scatter_to_chips(pairs)))
    pairs = dict(zip(_BIG, pairs))
    my_chip = (2 * lax.axis_index("x") + lax.axis_index("y")).astype(jnp.int32).reshape(1)
    small = jnp.concatenate([g[n] for n in _SMALL], axis=1)
    small_all = _gather_direct([small], "gather_small_grads")[0]

    res = {}
    for n in _BIG:
        res[n] = _reduce_adamw(recv[n], w[n][0], m[n][0], v[n][0], "adamw_" + n, own=pairs[n], own_slot=my_chip)
        res[n] = [t[None] for t in res[n]]
    cat = lambda t: jnp.concatenate([t[n] for n in _SMALL], axis=1)
    sg, sd, sm, sv = _reduce_adamw(small_all, cat(w), cat(m), cat(v), "adamw_small")
    off = 0
    for n in _SMALL:
        width = w[n].shape[1]
        res[n] = [t[:, off:off + width] for t in (sg, sd, sm, sv)]
        off += width

    total = lax.psum(loss[0, 0], ("x", "y", "c"))
    outs = [total, grad_x[None]]
    for part in range(4):
        outs.extend(res[n][part] for n in _ORDER)
    return tuple(outs)


def kernel(x, positions, pre_mix_norm, w_in, q_norm, w_uq, kv_norm, w_ukv, conv_w, conv_b, conv_ln_g, conv_ln_b, conv_out_norm, attn_out_norm, w_out, post_mix_norm, pre_ffn_norm, w_gate, w_up, w_down, post_ffn_norm, loss_target, m_pre_mix_norm, m_w_in, m_q_norm, m_w_uq, m_kv_norm, m_w_ukv, m_conv_w, m_conv_b, m_conv_ln_g, m_conv_ln_b, m_conv_out_norm, m_attn_out_norm, m_w_out, m_post_mix_norm, m_pre_ffn_norm, m_w_gate, m_w_up, m_w_down, m_post_ffn_norm, v_pre_mix_norm, v_w_in, v_q_norm, v_w_uq, v_kv_norm, v_w_ukv, v_conv_w, v_conv_b, v_conv_ln_g, v_conv_ln_b, v_conv_out_norm, v_attn_out_norm, v_w_out, v_post_mix_norm, v_pre_ffn_norm, v_w_gate, v_w_up, v_w_down, v_post_ffn_norm):
    w = dict(zip(_ORDER, (pre_mix_norm, w_in, q_norm, w_uq, kv_norm, w_ukv, conv_w, conv_b, conv_ln_g, conv_ln_b,
                          conv_out_norm, attn_out_norm, w_out, post_mix_norm, pre_ffn_norm, w_gate, w_up, w_down,
                          post_ffn_norm)))
    m = dict(zip(_ORDER, (m_pre_mix_norm, m_w_in, m_q_norm, m_w_uq, m_kv_norm, m_w_ukv, m_conv_w, m_conv_b,
                          m_conv_ln_g, m_conv_ln_b, m_conv_out_norm, m_attn_out_norm, m_w_out, m_post_mix_norm,
                          m_pre_ffn_norm, m_w_gate, m_w_up, m_w_down, m_post_ffn_norm)))
    v = dict(zip(_ORDER, (v_pre_mix_norm, v_w_in, v_q_norm, v_w_uq, v_kv_norm, v_w_ukv, v_conv_w, v_conv_b,
                          v_conv_ln_g, v_conv_ln_b, v_conv_out_norm, v_attn_out_norm, v_w_out, v_post_mix_norm,
                          v_pre_ffn_norm, v_w_gate, v_w_up, v_w_down, v_post_ffn_norm)))
    return _step(x, positions, loss_target, w, m, v)
```

```python
import functools

import jax
import jax.numpy as jnp
from jax import lax
from jax.experimental import pallas as pl
from jax.experimental.pallas import tpu as pltpu

N_DEV = 8
N_HEADS = 8
QK_NOPE = 128
QK_ROPE = 64
V_HEAD = 128
QK_HEAD = QK_NOPE + QK_ROPE
HEAD_PAD = 256
LANES = 128
ATTN_BLOCK = 512
CONV_K = 31
CONV_PAD = 32
EPS = 1e-6
ROPE_THETA = 10000.0
ADAM_LR = 0.001
ADAM_B1 = 0.9
ADAM_B2 = 0.999
ADAM_EPS = 1e-08
ADAM_WD = 0.01
ADAM_STEP = 10
VMEM_LIMIT = 56 * 1024 * 1024
F32 = jnp.float32
BF16 = jnp.bfloat16
MESH = pl.DeviceIdType.MESH
NEG = -1e30


def _pick(n, prefs):
    for p in prefs:
        if p <= n and n % p == 0:
            return p
    return n


def _params(sem):
    return pltpu.CompilerParams(dimension_semantics=sem, vmem_limit_bytes=VMEM_LIMIT)


_DIMS = {"nn": (((1,), (0,)), ((), ())), "nt": (((1,), (1,)), ((), ())), "tn": (((0,), (0,)), ((), ()))}


MM_VMEM_BUDGET = 40 * 1024 * 1024
MM_MAX_MACS = 3 * 1024 ** 3


V7X_HBM_BYTES_PER_S = 3.0e12
V7X_MXU_MACS_PER_S = 0.45e15
GRID_STEP_S = 0.35e-6


def _mm_tiles(m, n, k, size_a, size_b, size_o):
    best = None
    for tm in sorted({m, 1024, 512, 256, 128}, reverse=True):
        if tm > m or m % tm:
            continue
        for tn in sorted({n, 2048, 1024, 512, 384, 256, 128}, reverse=True):
            if tn > n or n % tn:
                continue
            vmem = 2 * (tm * k * size_a + k * tn * size_b + tm * tn * size_o)
            if vmem > MM_VMEM_BUDGET or tm * tn * k > MM_MAX_MACS:
                continue
            b_reads = 1 if tn == n else m // tm
            traffic = m * k * size_a + b_reads * k * n * size_b + m * n * size_o
            exposed = tm * k * size_a + k * tn * size_b + tm * tn * size_o
            steps = (m // tm) * (n // tn)
            key = (max(traffic / V7X_HBM_BYTES_PER_S, m * n * k / V7X_MXU_MACS_PER_S)
                   + exposed / V7X_HBM_BYTES_PER_S + steps * GRID_STEP_S)
            if best is None or key < best[0]:
                best = (key, tm, tn)
    assert best is not None, (m, n, k)
    return best[1], best[2]


def _mm(a, b, mode, name, out_dtype=F32, add=None):
    if mode == "nn":
        (m, k), (k2, n) = a.shape, b.shape
    elif mode == "nt":
        (m, k), (n, k2) = a.shape, b.shape
    else:
        (k, m), (k2, n) = a.shape, b.shape
    assert k == k2, (a.shape, b.shape, mode)
    tm, tn = _mm_tiles(m, n, k, a.dtype.itemsize, b.dtype.itemsize,
                       jnp.dtype(out_dtype).itemsize + (0 if add is None else 4))
    dims = _DIMS[mode]

    def body(a_ref, b_ref, *rest):
        acc = lax.dot_general(a_ref[...].astype(BF16), b_ref[...].astype(BF16), dims, preferred_element_type=F32)
        if add is not None:
            acc = acc + rest[0][...]
        rest[-1][...] = acc.astype(rest[-1].dtype)

    if mode == "tn":
        a_spec = pl.BlockSpec((k, tm), lambda i, j: (0, i))
    else:
        a_spec = pl.BlockSpec((tm, k), lambda i, j: (i, 0))
    if mode == "nt":
        b_spec = pl.BlockSpec((tn, k), lambda i, j: (j, 0))
    else:
        b_spec = pl.BlockSpec((k, tn), lambda i, j: (0, j))
    o_spec = pl.BlockSpec((tm, tn), lambda i, j: (i, j))
    extra = [] if add is None else [add]
    return pl.pallas_call(
        body, name=name,
        grid=(m // tm, n // tn),
        in_specs=[a_spec, b_spec] + [o_spec] * len(extra),
        out_specs=o_spec,
        out_shape=jax.ShapeDtypeStruct((m, n), out_dtype),
        compiler_params=_params(("parallel", "parallel")),
    )(a, b, *extra)


def _sigmoid(x):
    return 1.0 / (1.0 + jnp.exp(-x))


def _rms(x, g):
    r = lax.rsqrt(jnp.mean(x * x, axis=-1, keepdims=True) + EPS)
    return (x * r) * g


def _rms_bwd(x, g, dy):
    r = lax.rsqrt(jnp.mean(x * x, axis=-1, keepdims=True) + EPS)
    xh = x * r
    dyg = dy * g
    dx = r * (dyg - xh * jnp.mean(dyg * xh, axis=-1, keepdims=True))
    return dx, jnp.sum(dy * xh, axis=0, keepdims=True)


def _ln(x, g, b):
    mu = jnp.mean(x, axis=-1, keepdims=True)
    xc = x - mu
    rs = lax.rsqrt(jnp.mean(xc * xc, axis=-1, keepdims=True) + EPS)
    return (xc * rs) * g + b


def _ln_bwd(x, g, dy):
    mu = jnp.mean(x, axis=-1, keepdims=True)
    xc = x - mu
    rs = lax.rsqrt(jnp.mean(xc * xc, axis=-1, keepdims=True) + EPS)
    xh = xc * rs
    dyg = dy * g
    dx = rs * (dyg - jnp.mean(dyg, axis=-1, keepdims=True) - xh * jnp.mean(dyg * xh, axis=-1, keepdims=True))
    return dx, jnp.sum(dy * xh, axis=0, keepdims=True), jnp.sum(dy, axis=0, keepdims=True)


def _silu(x):
    return x * _sigmoid(x)


def _silu_grad(x):
    s = _sigmoid(x)
    return s * (1.0 + x * (1.0 - s))


def _rope(x, cos, sa, sb):
    return x * cos + pltpu.roll(x, 96, 1) * sa + pltpu.roll(x, 32, 1) * sb


def _rope_t(d, cos, sa, sb):
    return d * cos - pltpu.roll(d, 96, 1) * sa - pltpu.roll(d, 32, 1) * sb


def _rows(ts, w):
    return pl.BlockSpec((ts, w), lambda i: (i, 0))


def _vec(w):
    return pl.BlockSpec((1, w), lambda i: (0, 0))


def _acc_init(i, *refs):
    @pl.when(i == 0)
    def _():
        for r in refs:
            r[...] = jnp.zeros_like(r)


def _rope_tables(pos, inv_freq):
    s = pos.shape[0]
    ts = _pick(s, (512, 256, 128))

    def body(p_ref, f_ref, c_ref, sa_ref, sb_ref):
        ang = p_ref[...].astype(F32) * f_ref[...]
        lane = lax.broadcasted_iota(jnp.int32, ang.shape, 1)
        c, sn = jnp.cos(ang), jnp.sin(ang)
        c_ref[...] = jnp.where(lane < QK_ROPE, c, 0.0)
        sa_ref[...] = jnp.where(lane < QK_ROPE // 2, -sn, 0.0)
        sb_ref[...] = jnp.where((lane >= QK_ROPE // 2) & (lane < QK_ROPE), sn, 0.0)

    out = jax.ShapeDtypeStruct((s, LANES), F32)
    return pl.pallas_call(
        body, name="rope_tables", grid=(s // ts,),
        in_specs=[_rows(ts, 1), _vec(LANES)],
        out_specs=[_rows(ts, LANES)] * 3, out_shape=[out] * 3,
        compiler_params=_params(("parallel",)),
    )(pos, inv_freq)


def _pre_fwd(x, g):
    s, d = x.shape
    ts = _pick(s, (256, 128))

    def body(x_ref, g_ref, h_ref):
        h_ref[...] = _rms(x_ref[...], g_ref[...]).astype(BF16)

    return pl.pallas_call(
        body, name="pre_fwd", grid=(s // ts,),
        in_specs=[_rows(ts, d), _vec(d)], out_specs=_rows(ts, d),
        out_shape=jax.ShapeDtypeStruct((s, d), BF16),
        compiler_params=_params(("parallel",)),
    )(x, g)


def _split_fwd(z, gq, gkv, tabs, c, ql, kvl):
    s, zw = z.shape
    ts = _pick(s, (256, 128))
    o_q, o_kv, o_kr = 2 * c, 2 * c + ql, 2 * c + ql + kvl

    def body(z_ref, gq_ref, gkv_ref, c_ref, sa_ref, sb_ref, u0_ref, qn_ref, kvn_ref, kpe_ref):
        u0_ref[...] = z_ref[:, 0:c] * _sigmoid(z_ref[:, c:2 * c])
        qn_ref[...] = _rms(z_ref[:, o_q:o_kv], gq_ref[...]).astype(BF16)
        kvn_ref[...] = _rms(z_ref[:, o_kv:o_kr], gkv_ref[...]).astype(BF16)
        kpe_ref[...] = _rope(z_ref[:, o_kr:o_kr + LANES], c_ref[...], sa_ref[...], sb_ref[...]).astype(BF16)

    return pl.pallas_call(
        body, name="split_fwd", grid=(s // ts,),
        in_specs=[_rows(ts, zw), _vec(ql), _vec(kvl)] + [_rows(ts, LANES)] * 3,
        out_specs=[_rows(ts, c), _rows(ts, ql), _rows(ts, kvl), _rows(ts, LANES)],
        out_shape=[jax.ShapeDtypeStruct((s, c), F32), jax.ShapeDtypeStruct((s, ql), BF16),
                   jax.ShapeDtypeStruct((s, kvl), BF16), jax.ShapeDtypeStruct((s, LANES), BF16)],
        compiler_params=_params(("parallel",)),
    )(z, gq, gkv, *tabs)


def _split_bwd(du0, z, dqn, dkvn, dkpe_h, gq, gkv, tabs, c, ql, kvl):
    s, zw = z.shape
    ts = _pick(s, (256, 128))
    o_q, o_kv, o_kr = 2 * c, 2 * c + ql, 2 * c + ql + kvl

    def body(du0_ref, z_ref, dqn_ref, dkvn_ref, dkh_ref, gq_ref, gkv_ref, c_ref, sa_ref, sb_ref,
             dz_ref, dgq_ref, dgkv_ref):
        _acc_init(pl.program_id(0), dgq_ref, dgkv_ref)
        du0 = du0_ref[...]
        a = z_ref[:, 0:c]
        sg = _sigmoid(z_ref[:, c:2 * c])
        dz_ref[:, 0:c] = (du0 * sg).astype(BF16)
        dz_ref[:, c:2 * c] = (du0 * a * sg * (1.0 - sg)).astype(BF16)
        dq, dgq = _rms_bwd(z_ref[:, o_q:o_kv], gq_ref[...], dqn_ref[...])
        dz_ref[:, o_q:o_kv] = dq.astype(BF16)
        dgq_ref[...] += dgq
        dkv, dgkv = _rms_bwd(z_ref[:, o_kv:o_kr], gkv_ref[...], dkvn_ref[...])
        dz_ref[:, o_kv:o_kr] = dkv.astype(BF16)
        dgkv_ref[...] += dgkv
        dk = dkh_ref[:, 0:LANES]
        for h in range(1, N_HEADS):
            dk = dk + dkh_ref[:, h * LANES:(h + 1) * LANES]
        dz_ref[:, o_kr:o_kr + LANES] = _rope_t(dk, c_ref[...], sa_ref[...], sb_ref[...]).astype(BF16)

    return pl.pallas_call(
        body, name="split_bwd", grid=(s // ts,),
        in_specs=[_rows(ts, c), _rows(ts, zw), _rows(ts, ql), _rows(ts, kvl), _rows(ts, N_HEADS * LANES),
                  _vec(ql), _vec(kvl)] + [_rows(ts, LANES)] * 3,
        out_specs=[_rows(ts, zw), _vec(ql), _vec(kvl)],
        out_shape=[jax.ShapeDtypeStruct((s, zw), BF16), jax.ShapeDtypeStruct((1, ql), F32),
                   jax.ShapeDtypeStruct((1, kvl), F32)],
        compiler_params=_params(("arbitrary",)),
    )(du0, z, dqn, dkvn, dkpe_h, gq, gkv, *tabs)


def _q_rope(qpre, tabs, transpose, out_dtype, name):
    s, w = qpre.shape
    ts = _pick(s, (256, 128))
    rot = _rope_t if transpose else _rope

    def body(q_ref, c_ref, sa_ref, sb_ref, o_ref):
        cs, sa, sb = c_ref[...], sa_ref[...], sb_ref[...]
        for h in range(N_HEADS):
            lo = h * HEAD_PAD
            o_ref[:, lo:lo + QK_NOPE] = q_ref[:, lo:lo + QK_NOPE].astype(out_dtype)
            o_ref[:, lo + QK_NOPE:lo + HEAD_PAD] = rot(q_ref[:, lo + QK_NOPE:lo + HEAD_PAD], cs, sa, sb).astype(out_dtype)

    return pl.pallas_call(
        body, name=name, grid=(s // ts,),
        in_specs=[_rows(ts, w)] + [_rows(ts, LANES)] * 3, out_specs=_rows(ts, w),
        out_shape=jax.ShapeDtypeStruct((s, w), out_dtype),
        compiler_params=_params(("parallel",)),
    )(qpre, *tabs)


def _conv_fwd(u0, w, b):
    s, c = u0.shape
    tc = LANES
    rc = _pick(s, (256, 128))

    def body(u_ref, w_ref, b_ref, o_ref, pad_ref):
        pad_ref[0:CONV_PAD, :] = jnp.zeros((CONV_PAD, tc), F32)
        pad_ref[CONV_PAD:CONV_PAD + s, :] = u_ref[...]
        for r in range(s // rc):
            acc = jnp.broadcast_to(b_ref[...], (rc, tc))
            for k in range(CONV_K):
                lo = r * rc + CONV_PAD - (CONV_K - 1) + k
                acc = acc + w_ref[k:k + 1, :] * pad_ref[lo:lo + rc, :]
            o_ref[r * rc:(r + 1) * rc, :] = acc

    col = lambda j: (0, j)
    return pl.pallas_call(
        body, name="conv_fwd", grid=(c // tc,),
        in_specs=[pl.BlockSpec((s, tc), col), pl.BlockSpec((CONV_K, tc), col), pl.BlockSpec((1, tc), col)],
        out_specs=pl.BlockSpec((s, tc), col),
        out_shape=jax.ShapeDtypeStruct((s, c), F32),
        scratch_shapes=[pltpu.VMEM((s + CONV_PAD, tc), F32)],
        compiler_params=_params(("parallel",)),
    )(u0, w, b)


def _conv_bwd(du1, u0, w):
    s, c = u0.shape
    tc = LANES
    rc = _pick(s, (256, 128))

    def body(d_ref, u_ref, w_ref, du_ref, dw_ref, db_ref, upad_ref, dpad_ref):
        upad_ref[0:CONV_PAD, :] = jnp.zeros((CONV_PAD, tc), F32)
        upad_ref[CONV_PAD:CONV_PAD + s, :] = u_ref[...]
        dpad_ref[0:s, :] = d_ref[...]
        dpad_ref[s:s + CONV_PAD, :] = jnp.zeros((CONV_PAD, tc), F32)
        for r in range(s // rc):
            acc = jnp.zeros((rc, tc), F32)
            for k in range(CONV_K):
                lo = r * rc + (CONV_K - 1) - k
                acc = acc + w_ref[k:k + 1, :] * dpad_ref[lo:lo + rc, :]
            du_ref[r * rc:(r + 1) * rc, :] = acc
        for k in range(CONV_K):
            acc8 = jnp.zeros((8, tc), F32)
            for r in range(s // rc):
                lo = r * rc + CONV_PAD - (CONV_K - 1) + k
                prod = d_ref[r * rc:(r + 1) * rc, :] * upad_ref[lo:lo + rc, :]
                acc8 = acc8 + jnp.sum(prod.reshape(rc // 8, 8, tc), axis=0)
            dw_ref[k:k + 1, :] = jnp.sum(acc8, axis=0, keepdims=True)
        db_ref[...] = jnp.sum(d_ref[...], axis=0, keepdims=True)

    col = lambda j: (0, j)
    return pl.pallas_call(
        body, name="conv_bwd", grid=(c // tc,),
        in_specs=[pl.BlockSpec((s, tc), col), pl.BlockSpec((s, tc), col), pl.BlockSpec((CONV_K, tc), col)],
        out_specs=[pl.BlockSpec((s, tc), col), pl.BlockSpec((CONV_K, tc), col), pl.BlockSpec((1, tc), col)],
        out_shape=[jax.ShapeDtypeStruct((s, c), F32), jax.ShapeDtypeStruct((CONV_K, c), F32),
                   jax.ShapeDtypeStruct((1, c), F32)],
        scratch_shapes=[pltpu.VMEM((s + CONV_PAD, tc), F32), pltpu.VMEM((s + CONV_PAD, tc), F32)],
        compiler_params=_params(("parallel",)),
    )(du1, u0, w)


def _causal_mask(sc, qi, kj, tq, tk):
    rows = qi * tq + lax.broadcasted_iota(jnp.int32, sc.shape, 0)
    cols = kj * tk + lax.broadcasted_iota(jnp.int32, sc.shape, 1)
    return jnp.where(cols <= rows, sc, NEG)


def _attn_fwd(q, kv, kpe):
    s = q.shape[0]
    tq = tk = _pick(s, (ATTN_BLOCK, 256, 128))
    reps = tk // LANES
    scale = QK_HEAD ** -0.5
    nt = (((1,), (1,)), ((), ()))

    def body(q_ref, kn_ref, v_ref, kpe_ref, o_ref, lse_ref, kf_ref, vb_ref, m_ref, l_ref, acc_ref):
        i = pl.program_id(1)

        @pl.when(i == 0)
        def _():
            kf_ref[:, 0:QK_NOPE] = kn_ref[...].astype(BF16)
            kf_ref[:, QK_NOPE:HEAD_PAD] = kpe_ref[...]
            vb_ref[...] = v_ref[...].astype(BF16)

        qb = q_ref[...]
        m_ref[...] = jnp.full((tq, LANES), NEG, F32)
        l_ref[...] = jnp.zeros((tq, LANES), F32)
        acc_ref[...] = jnp.zeros((tq, V_HEAD), F32)

        def block(j, diagonal):
            off = pl.multiple_of(j * tk, tk)
            sc = lax.dot_general(qb, kf_ref[pl.ds(off, tk), :], nt, preferred_element_type=F32) * scale
            if diagonal:
                sc = _causal_mask(sc, 0, 0, tq, tk)
            m_prev = m_ref[...]
            m_new = jnp.maximum(m_prev, jnp.max(sc, axis=1, keepdims=True))
            p = jnp.exp(sc - jnp.tile(m_new, (1, reps)))
            alpha = jnp.exp(m_prev - m_new)
            l_ref[...] = alpha * l_ref[...] + jnp.sum(p, axis=1, keepdims=True)
            acc_ref[...] = alpha * acc_ref[...] + jnp.dot(p.astype(BF16), vb_ref[pl.ds(off, tk), :],
                                                          preferred_element_type=F32)
            m_ref[...] = m_new

        def below_diagonal(j, carry):
            block(j, False)
            return carry

        lax.fori_loop(0, i, below_diagonal, 0)
        block(i, True)
        o_ref[...] = acc_ref[...] / l_ref[...]
        lse_ref[...] = m_ref[...] + jnp.log(l_ref[...])

    return pl.pallas_call(
        body, name="attn_fwd", grid=(N_HEADS, s // tq),
        in_specs=[pl.BlockSpec((tq, HEAD_PAD), lambda h, i: (i, h)),
                  pl.BlockSpec((s, QK_NOPE), lambda h, i: (0, 2 * h)),
                  pl.BlockSpec((s, V_HEAD), lambda h, i: (0, 2 * h + 1)),
                  pl.BlockSpec((s, LANES), lambda h, i: (0, 0))],
        out_specs=[pl.BlockSpec((tq, V_HEAD), lambda h, i: (i, h)),
                   pl.BlockSpec((tq, LANES), lambda h, i: (i, h))],
        out_shape=[jax.ShapeDtypeStruct((s, N_HEADS * V_HEAD), F32),
                   jax.ShapeDtypeStruct((s, N_HEADS * LANES), F32)],
        scratch_shapes=[pltpu.VMEM((s, HEAD_PAD), BF16), pltpu.VMEM((s, V_HEAD), BF16),
                        pltpu.VMEM((tq, LANES), F32), pltpu.VMEM((tq, LANES), F32), pltpu.VMEM((tq, V_HEAD), F32)],
        compiler_params=_params(("parallel", "arbitrary")),
    )(q, kv, kv, kpe)


def _attn_bwd(q, kv, kpe, o, do, lse):
    s = q.shape[0]
    tq = tk = _pick(s, (ATTN_BLOCK, 256, 128))
    nq = s // tq
    reps = tk // LANES
    scale = QK_HEAD ** -0.5
    nt = (((1,), (1,)), ((), ()))
    tn = (((0,), (0,)), ((), ()))

    def body(q_ref, kn_ref, v_ref, kpe_ref, o_ref, do_ref, lse_ref, dq_ref, dkv_ref, dkpe_ref,
             kf_ref, vb_ref, dk_ref, dv_ref):
        j = pl.program_id(1)

        @pl.when(j == 0)
        def _():
            dq_ref[...] = jnp.zeros_like(dq_ref)

        kf_ref[:, 0:QK_NOPE] = kn_ref[...].astype(BF16)
        kf_ref[:, QK_NOPE:HEAD_PAD] = kpe_ref[...]
        vb_ref[...] = v_ref[...].astype(BF16)
        dk_ref[...] = jnp.zeros_like(dk_ref)
        dv_ref[...] = jnp.zeros_like(dv_ref)

        def block(i, diagonal):
            off = pl.multiple_of(i * tq, tq)
            qb = q_ref[pl.ds(off, tq), :]
            dob = do_ref[pl.ds(off, tq), :]
            delta = jnp.sum(dob * o_ref[pl.ds(off, tq), :], axis=1, keepdims=True)
            sc = lax.dot_general(qb, kf_ref[...], nt, preferred_element_type=F32) * scale
            if diagonal:
                sc = _causal_mask(sc, 0, 0, tq, tk)
            p = jnp.exp(sc - jnp.tile(lse_ref[pl.ds(off, tq), :], (1, reps)))
            dob16 = dob.astype(BF16)
            dv_ref[...] += lax.dot_general(p.astype(BF16), dob16, tn, preferred_element_type=F32)
            dp = lax.dot_general(dob16, vb_ref[...], nt, preferred_element_type=F32)
            ds = (p * (dp - delta) * scale).astype(BF16)
            dq_ref[pl.ds(off, tq), :] += jnp.dot(ds, kf_ref[...], preferred_element_type=F32)
            dk_ref[...] += lax.dot_general(ds, qb, tn, preferred_element_type=F32)

        def above_diagonal(i, carry):
            block(i, False)
            return carry

        block(j, True)
        lax.fori_loop(j + 1, nq, above_diagonal, 0)
        dkv_ref[:, 0:QK_NOPE] = dk_ref[:, 0:QK_NOPE]
        dkv_ref[:, QK_NOPE:HEAD_PAD] = dv_ref[...]
        dkpe_ref[...] = dk_ref[:, QK_NOPE:HEAD_PAD]

    head_rows = lambda w: pl.BlockSpec((s, w), lambda h, j: (0, h))
    return pl.pallas_call(
        body, name="attn_bwd", grid=(N_HEADS, s // tk),
        in_specs=[head_rows(HEAD_PAD),
                  pl.BlockSpec((tk, QK_NOPE), lambda h, j: (j, 2 * h)),
                  pl.BlockSpec((tk, V_HEAD), lambda h, j: (j, 2 * h + 1)),
                  pl.BlockSpec((tk, LANES), lambda h, j: (j, 0)),
                  head_rows(V_HEAD), head_rows(V_HEAD), head_rows(LANES)],
        out_specs=[head_rows(HEAD_PAD),
                   pl.BlockSpec((tk, HEAD_PAD), lambda h, j: (j, h)),
                   pl.BlockSpec((tk, LANES), lambda h, j: (j, h))],
        out_shape=[jax.ShapeDtypeStruct((s, N_HEADS * HEAD_PAD), F32),
                   jax.ShapeDtypeStruct((s, N_HEADS * HEAD_PAD), F32),
                   jax.ShapeDtypeStruct((s, N_HEADS * LANES), F32)],
        scratch_shapes=[pltpu.VMEM((tk, HEAD_PAD), BF16), pltpu.VMEM((tk, V_HEAD), BF16),
                        pltpu.VMEM((tk, HEAD_PAD), F32), pltpu.VMEM((tk, V_HEAD), F32)],
        compiler_params=_params(("parallel", "arbitrary")),
    )(q, kv, kv, kpe, o, do, lse)


def _mix_fwd(u1, lng, lnb, gcon, attn, gattn):
    s, c = u1.shape
    ac = attn.shape[1]
    ts = _pick(s, (256, 128))

    def body(u_ref, lg_ref, lb_ref, gc_ref, a_ref, ga_ref, o_ref):
        t3 = _silu(_ln(u_ref[...], lg_ref[...], lb_ref[...]))
        o_ref[:, 0:c] = _rms(t3, gc_ref[...]).astype(BF16)
        o_ref[:, c:c + ac] = _rms(a_ref[...], ga_ref[...]).astype(BF16)

    return pl.pallas_call(
        body, name="mix_fwd", grid=(s // ts,),
        in_specs=[_rows(ts, c), _vec(c), _vec(c), _vec(c), _rows(ts, ac), _vec(ac)],
        out_specs=_rows(ts, c + ac),
        out_shape=jax.ShapeDtypeStruct((s, c + ac), BF16),
        compiler_params=_params(("parallel",)),
    )(u1, lng, lnb, gcon, attn, gattn)


def _mix_bwd(dmixin, u1, lng, lnb, gcon, attn, gattn):
    s, c = u1.shape
    ac = attn.shape[1]
    ts = _pick(s, (256, 128))

    def body(d_ref, u_ref, lg_ref, lb_ref, gc_ref, a_ref, ga_ref,
             du_ref, da_ref, dlg_ref, dlb_ref, dgc_ref, dga_ref):
        _acc_init(pl.program_id(0), dlg_ref, dlb_ref, dgc_ref, dga_ref)
        u = u_ref[...]
        t2 = _ln(u, lg_ref[...], lb_ref[...])
        dt3, dgc = _rms_bwd(_silu(t2), gc_ref[...], d_ref[:, 0:c])
        du, dlg, dlb = _ln_bwd(u, lg_ref[...], dt3 * _silu_grad(t2))
        du_ref[...] = du
        dlg_ref[...] += dlg
        dlb_ref[...] += dlb
        dgc_ref[...] += dgc
        da, dga = _rms_bwd(a_ref[...], ga_ref[...], d_ref[:, c:c + ac])
        da_ref[...] = da
        dga_ref[...] += dga

    return pl.pallas_call(
        body, name="mix_bwd", grid=(s // ts,),
        in_specs=[_rows(ts, c + ac), _rows(ts, c), _vec(c), _vec(c), _vec(c), _rows(ts, ac), _vec(ac)],
        out_specs=[_rows(ts, c), _rows(ts, ac), _vec(c), _vec(c), _vec(c), _vec(ac)],
        out_shape=[jax.ShapeDtypeStruct((s, c), F32), jax.ShapeDtypeStruct((s, ac), F32),
                   jax.ShapeDtypeStruct((1, c), F32), jax.ShapeDtypeStruct((1, c), F32),
                   jax.ShapeDtypeStruct((1, c), F32), jax.ShapeDtypeStruct((1, ac), F32)],
        compiler_params=_params(("arbitrary",)),
    )(dmixin, u1, lng, lnb, gcon, attn, gattn)


def _post_mix_fwd(x, mix, gpost, gpre):
    s, d = x.shape
    ts = _pick(s, (256, 128))

    def body(x_ref, m_ref, gp_ref, gf_ref, x1_ref, hf_ref):
        x1 = x_ref[...] + _rms(m_ref[...], gp_ref[...])
        x1_ref[...] = x1
        hf_ref[...] = _rms(x1, gf_ref[...]).astype(BF16)

    return pl.pallas_call(
        body, name="post_mix_fwd", grid=(s // ts,),
        in_specs=[_rows(ts, d), _rows(ts, d), _vec(d), _vec(d)],
        out_specs=[_rows(ts, d), _rows(ts, d)],
        out_shape=[jax.ShapeDtypeStruct((s, d), F32), jax.ShapeDtypeStruct((s, d), BF16)],
        compiler_params=_params(("parallel",)),
    )(x, mix, gpost, gpre)


def _post_mix_bwd(dy, dhf, x1, gpre, mix, gpost):
    s, d = x1.shape
    ts = _pick(s, (256, 128))

    def body(dy_ref, dh_ref, x1_ref, gf_ref, m_ref, gp_ref, dx1_ref, dm_ref, dgf_ref, dgp_ref):
        _acc_init(pl.program_id(0), dgf_ref, dgp_ref)
        dxa, dgf = _rms_bwd(x1_ref[...], gf_ref[...], dh_ref[...])
        dx1 = dy_ref[...] + dxa
        dx1_ref[...] = dx1
        dgf_ref[...] += dgf
        dm, dgp = _rms_bwd(m_ref[...], gp_ref[...], dx1)
        dm_ref[...] = dm.astype(BF16)
        dgp_ref[...] += dgp

    return pl.pallas_call(
        body, name="post_mix_bwd", grid=(s // ts,),
        in_specs=[_rows(ts, d), _rows(ts, d), _rows(ts, d), _vec(d), _rows(ts, d), _vec(d)],
        out_specs=[_rows(ts, d), _rows(ts, d), _vec(d), _vec(d)],
        out_shape=[jax.ShapeDtypeStruct((s, d), F32), jax.ShapeDtypeStruct((s, d), BF16),
                   jax.ShapeDtypeStruct((1, d), F32), jax.ShapeDtypeStruct((1, d), F32)],
        compiler_params=_params(("arbitrary",)),
    )(dy, dhf, x1, gpre, mix, gpost)


def _ffn_up(hf, wg, wu):
    s, d = hf.shape
    nsh, _, fs = wg.shape
    tm = _pick(s, (1024, 512, 256, 128))

    def body(h_ref, wg_ref, wu_ref, g_ref, u_ref, a_ref):
        h = h_ref[...]
        g = jnp.dot(h, wg_ref[...], preferred_element_type=F32)
        u = jnp.dot(h, wu_ref[...], preferred_element_type=F32)
        g_ref[...] = g
        u_ref[...] = u
        a_ref[...] = (_silu(g) * u).astype(BF16)

    w_spec = pl.BlockSpec((None, d, fs), lambda i, j: (j, 0, 0))
    o_spec = pl.BlockSpec((None, tm, fs), lambda i, j: (j, i, 0))
    return pl.pallas_call(
        body, name="ffn_up", grid=(s // tm, nsh),
        in_specs=[pl.BlockSpec((tm, d), lambda i, j: (i, 0)), w_spec, w_spec],
        out_specs=[o_spec] * 3,
        out_shape=[jax.ShapeDtypeStruct((nsh, s, fs), F32)] * 2 + [jax.ShapeDtypeStruct((nsh, s, fs), BF16)],
        compiler_params=_params(("parallel", "parallel")),
    )(hf, wg, wu)


def _ffn_down(act, wd):
    nsh, s, fs = act.shape
    d = wd.shape[2]
    tm = _pick(s, (1024, 512, 256, 128))
    tn = _pick(d, (512, 256, 128))

    def body(a_ref, w_ref, o_ref):
        acc = jnp.dot(a_ref[0], w_ref[0], preferred_element_type=F32)
        for j in range(1, nsh):
            acc = acc + jnp.dot(a_ref[j], w_ref[j], preferred_element_type=F32)
        o_ref[...] = acc

    return pl.pallas_call(
        body, name="ffn_down", grid=(s // tm, d // tn),
        in_specs=[pl.BlockSpec((nsh, tm, fs), lambda i, j: (0, i, 0)),
                  pl.BlockSpec((nsh, fs, tn), lambda i, j: (0, 0, j))],
        out_specs=pl.BlockSpec((tm, tn), lambda i, j: (i, j)),
        out_shape=jax.ShapeDtypeStruct((s, d), F32),
        compiler_params=_params(("parallel", "parallel")),
    )(act, wd)


def _ffn_down_bwd(dff, wd, gate, up):
    s, d = dff.shape
    nsh, fs, _ = wd.shape
    tm = _pick(s, (1024, 512, 256, 128))
    nt = (((1,), (1,)), ((), ()))

    def body(d_ref, w_ref, g_ref, u_ref, dg_ref, du_ref):
        dact = lax.dot_general(d_ref[...], w_ref[...], nt, preferred_element_type=F32)
        g = g_ref[...]
        dg_ref[...] = (dact * u_ref[...] * _silu_grad(g)).astype(BF16)
        du_ref[...] = (dact * _silu(g)).astype(BF16)

    h_spec = pl.BlockSpec((None, tm, fs), lambda i, j: (j, i, 0))
    return pl.pallas_call(
        body, name="ffn_down_bwd", grid=(s // tm, nsh),
        in_specs=[pl.BlockSpec((tm, d), lambda i, j: (i, 0)),
                  pl.BlockSpec((None, fs, d), lambda i, j: (j, 0, 0)), h_spec, h_spec],
        out_specs=[h_spec] * 2,
        out_shape=[jax.ShapeDtypeStruct((nsh, s, fs), BF16)] * 2,
        compiler_params=_params(("parallel", "parallel")),
    )(dff, wd, gate, up)


def _ffn_dw_down(act, dff):
    nsh, s, fs = act.shape
    d = dff.shape[1]
    tn = (((0,), (0,)), ((), ()))

    def body(a_ref, d_ref, o_ref):
        o_ref[...] = lax.dot_general(a_ref[...], d_ref[...], tn, preferred_element_type=F32).astype(BF16)

    return pl.pallas_call(
        body, name="ffn_dw_down", grid=(nsh,),
        in_specs=[pl.BlockSpec((None, s, fs), lambda j: (j, 0, 0)), pl.BlockSpec((s, d), lambda j: (0, 0))],
        out_specs=pl.BlockSpec((None, fs, d), lambda j: (j, 0, 0)),
        out_shape=jax.ShapeDtypeStruct((nsh, fs, d), BF16),
        compiler_params=_params(("parallel",)),
    )(act, dff)


def _ffn_dw_up(hf, dh, name):
    s, d = hf.shape
    nsh, _, fs = dh.shape
    tn = (((0,), (0,)), ((), ()))

    def body(h_ref, d_ref, o_ref):
        o_ref[...] = lax.dot_general(h_ref[...], d_ref[...], tn, preferred_element_type=F32).astype(BF16)

    return pl.pallas_call(
        body, name=name, grid=(nsh,),
        in_specs=[pl.BlockSpec((s, d), lambda j: (0, 0)), pl.BlockSpec((None, s, fs), lambda j: (j, 0, 0))],
        out_specs=pl.BlockSpec((None, d, fs), lambda j: (j, 0, 0)),
        out_shape=jax.ShapeDtypeStruct((nsh, d, fs), BF16),
        compiler_params=_params(("parallel",)),
    )(hf, dh)


def _ffn_up_bwd(dh, w, name, add=None):
    nsh, s, fs = dh.shape
    d = w.shape[1]
    tm = _pick(s, (1024, 512, 256, 128))
    tn = _pick(d, (256, 128))
    nt = (((1,), (1,)), ((), ()))

    def body(a_ref, w_ref, *rest):
        acc = lax.dot_general(a_ref[0], w_ref[0], nt, preferred_element_type=F32)
        for j in range(1, nsh):
            acc = acc + lax.dot_general(a_ref[j], w_ref[j], nt, preferred_element_type=F32)
        if add is not None:
            acc = acc + rest[0][...]
        rest[-1][...] = acc

    o_spec = pl.BlockSpec((tm, tn), lambda i, j: (i, j))
    extra = [] if add is None else [add]
    return pl.pallas_call(
        body, name=name, grid=(s // tm, d // tn),
        in_specs=[pl.BlockSpec((nsh, tm, fs), lambda i, j: (0, i, 0)),
                  pl.BlockSpec((nsh, tn, fs), lambda i, j: (0, j, 0))] + [o_spec] * len(extra),
        out_specs=o_spec,
        out_shape=jax.ShapeDtypeStruct((s, d), F32),
        compiler_params=_params(("parallel", "parallel")),
    )(dh, w, *extra)


def _final(ff, x1, tgt, g):
    s, d = x1.shape
    ts = _pick(s, (256, 128))

    def body(ff_ref, x1_ref, t_ref, g_ref, loss_ref, dy_ref, dff_ref, dg_ref):
        _acc_init(pl.program_id(0), loss_ref, dg_ref)
        ff_v = ff_ref[...]
        err = x1_ref[...] + _rms(ff_v, g_ref[...]) - t_ref[...]
        tok = jnp.mean(err * err, axis=-1, keepdims=True)
        loss_ref[...] += 0.5 * jnp.sum(tok, axis=0, keepdims=True)
        dy = err * (1.0 / d)
        dy_ref[...] = dy
        dff, dg = _rms_bwd(ff_v, g_ref[...], dy)
        dff_ref[...] = dff.astype(BF16)
        dg_ref[...] += dg

    return pl.pallas_call(
        body, name="final", grid=(s // ts,),
        in_specs=[_rows(ts, d), _rows(ts, d), _rows(ts, d), _vec(d)],
        out_specs=[_vec(LANES), _rows(ts, d), _rows(ts, d), _vec(d)],
        out_shape=[jax.ShapeDtypeStruct((1, LANES), F32), jax.ShapeDtypeStruct((s, d), F32),
                   jax.ShapeDtypeStruct((s, d), BF16), jax.ShapeDtypeStruct((1, d), F32)],
        compiler_params=_params(("arbitrary",)),
    )(ff, x1, tgt, g)


def _pre_bwd(dx1, dh, x, g):
    s, d = x.shape
    ts = _pick(s, (256, 128))

    def body(dx1_ref, dh_ref, x_ref, g_ref, dx_ref, dg_ref):
        _acc_init(pl.program_id(0), dg_ref)
        dxa, dg = _rms_bwd(x_ref[...], g_ref[...], dh_ref[...])
        dx_ref[...] = dx1_ref[...] + dxa
        dg_ref[...] += dg

    return pl.pallas_call(
        body, name="pre_bwd", grid=(s // ts,),
        in_specs=[_rows(ts, d), _rows(ts, d), _rows(ts, d), _vec(d)],
        out_specs=[_rows(ts, d), _vec(d)],
        out_shape=[jax.ShapeDtypeStruct((s, d), F32), jax.ShapeDtypeStruct((1, d), F32)],
        compiler_params=_params(("arbitrary",)),
    )(dx1, dh, x, g)


def _local_step(x, pos, tgt, vecs, mix_weights_fn, ffn_weights_fn, ffn_grads_fn):
    c = vecs["conv_b"].shape[1]
    ql = vecs["q_norm"].shape[1]
    kvl = vecs["kv_norm"].shape[1]
    half = jnp.arange(0, QK_ROPE, 2, dtype=F32)
    freq = ROPE_THETA ** (-half / QK_ROPE)
    inv_freq = jnp.concatenate([freq, freq, jnp.zeros((LANES - QK_ROPE,), F32)])[None, :]
    tabs = _rope_tables(pos, inv_freq)

    h = _pre_fwd(x, vecs["pre_mix_norm"])
    w_in_p, w_uq_p, w_ukv, conv_w, w_out, zero = mix_weights_fn(h)
    z = _mm(h, w_in_p, "nn", "mm_z")
    u0, qn, kvn, kpe = _split_fwd(z, vecs["q_norm"] + zero, vecs["kv_norm"], tabs, c, ql, kvl)
    u1 = _conv_fwd(u0, conv_w, vecs["conv_b"])
    q = _q_rope(_mm(qn, w_uq_p, "nn", "mm_q"), tabs, False, BF16, "q_rope")
    kv = _mm(kvn, w_ukv, "nn", "mm_kv")
    attn, lse = _attn_fwd(q, kv, kpe)
    mixin = _mix_fwd(u1, vecs["conv_ln_g"], vecs["conv_ln_b"], vecs["conv_out_norm"], attn, vecs["attn_out_norm"])
    mix = _mm(mixin, w_out, "nn", "mm_mix")
    x1, hf = _post_mix_fwd(x, mix, vecs["post_mix_norm"], vecs["pre_ffn_norm"])
    w_gate, w_up, w_down = ffn_weights_fn(mix)
    gate, up, act = _ffn_up(hf, w_gate, w_up)
    ff = _ffn_down(act, w_down)
    loss, dy, dff, d_post_ffn = _final(ff, x1, tgt, vecs["post_ffn_norm"])

    g = {"post_ffn_norm": d_post_ffn}
    dw_down = _ffn_dw_down(act, dff)
    dgate, dup = _ffn_down_bwd(dff, w_down, gate, up)
    dw_gate = _ffn_dw_up(hf, dgate, "ffn_dw_gate")
    dw_up = _ffn_dw_up(hf, dup, "ffn_dw_up")
    zero = ffn_grads_fn(dw_gate, dw_up, dw_down)
    dhf = _ffn_up_bwd(dup, w_up, "ffn_dhf_up", add=_ffn_up_bwd(dgate, w_gate, "ffn_dhf_gate"))
    dx1, dmix, g["pre_ffn_norm"], g["post_mix_norm"] = _post_mix_bwd(
        dy, dhf, x1, vecs["pre_ffn_norm"] + zero, mix, vecs["post_mix_norm"])
    dmixin = _mm(dmix, w_out, "nt", "mm_dmixin")
    g["w_out"] = _mm(mixin, dmix, "tn", "mm_dw_out", BF16)
    du1, dattn, g["conv_ln_g"], g["conv_ln_b"], g["conv_out_norm"], g["attn_out_norm"] = _mix_bwd(
        dmixin, u1, vecs["conv_ln_g"], vecs["conv_ln_b"], vecs["conv_out_norm"], attn, vecs["attn_out_norm"])
    du0, g["conv_w"], g["conv_b"] = _conv_bwd(du1, u0, conv_w)
    dq, dkv, dkpe_h = _attn_bwd(q, kv, kpe, attn, dattn, lse)
    dqpre = _q_rope(dq, tabs, True, BF16, "q_rope_bwd")
    dqn = _mm(dqpre, w_uq_p, "nt", "mm_dqn")
    g["w_uq_p"] = _mm(qn, dqpre, "tn", "mm_dw_uq", BF16)
    dkvn = _mm(dkv, w_ukv, "nt", "mm_dkvn")
    g["w_ukv"] = _mm(kvn, dkv, "tn", "mm_dw_ukv", BF16)
    dz, g["q_norm"], g["kv_norm"] = _split_bwd(du0, z, dqn, dkvn, dkpe_h, vecs["q_norm"], vecs["kv_norm"], tabs, c, ql, kvl)
    dh = _mm(dz, w_in_p, "nt", "mm_dh")
    g["w_in_p"] = _mm(h, dz, "tn", "mm_dw_in", BF16)
    grad_x, g["pre_mix_norm"] = _pre_bwd(dx1, dh, x, vecs["pre_mix_norm"])
    return loss, grad_x, g


def _my_index():
    return 4 * lax.axis_index("x") + 2 * lax.axis_index("y") + lax.axis_index("c")


def _coords(idx):
    return ((idx >> 2) & 1, (idx >> 1) & 1, idx & 1)


def _place():
    x, y, c = lax.axis_index("x"), lax.axis_index("y"), lax.axis_index("c")
    return (x, y, c), (x, y, 1 - c), [(1 - x, y), (x, 1 - y), (1 - x, 1 - y)]


def _comm_call(body, name, arrays, out_shapes, n_sems):
    n = len(arrays)
    any_spec = pl.BlockSpec(memory_space=pl.ANY)
    return pl.pallas_call(
        body, name=name,
        in_specs=[any_spec] * n, out_specs=[any_spec] * len(out_shapes), out_shape=out_shapes,
        scratch_shapes=[pltpu.SemaphoreType.DMA((n, n_sems))] * 3,
        compiler_params=pltpu.CompilerParams(has_side_effects=True),
    )(*arrays)


def _gather_direct(arrays, name):
    n = len(arrays)

    def body(*refs):
        ins, outs = refs[:n], refs[n:2 * n]
        send_sems, recv_sems, local_sems = refs[2 * n:]
        me = _my_index()
        local = [pltpu.make_async_copy(ins[k], outs[k].at[me], local_sems.at[k, 0]) for k in range(n)]
        for cp in local:
            cp.start()

        def copy(k, p, slot):
            return pltpu.make_async_remote_copy(
                src_ref=ins[k], dst_ref=outs[k].at[slot], send_sem=send_sems.at[k, p - 1],
                recv_sem=recv_sems.at[k, p - 1], device_id=_coords(me ^ p), device_id_type=MESH)

        sends = [copy(k, p, me) for p in range(1, N_DEV) for k in range(n)]
        for cp in sends:
            cp.start()
        for p in range(1, N_DEV):
            for k in range(n):
                copy(k, p, me ^ p).wait_recv()
        for cp in sends:
            cp.wait_send()
        for cp in local:
            cp.wait()

    out_shapes = [jax.ShapeDtypeStruct((N_DEV,) + a.shape, a.dtype) for a in arrays]
    return _comm_call(body, name, arrays, out_shapes, N_DEV - 1)


HBM_SPEC = pl.BlockSpec(memory_space=pltpu.HBM)
SEM_SPEC = pl.BlockSpec(memory_space=pltpu.SEMAPHORE)
DATAFLOW = pltpu.SideEffectType.DATAFLOW_SIDE_EFFECTING


def _split_start(name, copies_of, srcs, lands, after):
    n = len(srcs)

    def body(*refs):
        outs = refs[2 * n + 1:]
        for k in range(n):
            for cp in copies_of(refs[k], refs[n + k], outs[k], outs[n + k]):
                cp.start()
        outs[-1][...] = jnp.zeros_like(outs[-1])

    hbm = lambda a: pltpu.HBM(a.shape, a.dtype)
    out = pl.pallas_call(
        body, name=name,
        in_specs=[HBM_SPEC] * (2 * n) + [pl.BlockSpec(memory_space=pl.ANY)],
        out_specs=[SEM_SPEC] * (2 * n) + [HBM_SPEC] * (2 * n) + [pl.BlockSpec(memory_space=pltpu.VMEM)],
        out_shape=[pltpu.SemaphoreType.DMA(())] * (2 * n) + [hbm(a) for a in srcs] + [hbm(a) for a in lands]
        + [jax.ShapeDtypeStruct((8, LANES), F32)],
        input_output_aliases={k: 2 * n + k for k in range(2 * n)},
        compiler_params=pltpu.CompilerParams(has_side_effects=DATAFLOW),
    )(*[pltpu.with_memory_space_constraint(a, pltpu.HBM) for a in list(srcs) + list(lands)], after)
    return (out[:n], out[n:2 * n], out[2 * n:3 * n], out[3 * n:4 * n]), out[-1][0, 0]


def _split_wait(name, n_copies, started, after):
    send_sems, recv_sems, srcs, lands = started
    n = len(srcs)

    def body(*refs):
        for k in range(n):
            slots = refs[n + k].at[pl.ds(0, n_copies)]
            all_copies = pltpu.make_async_remote_copy(
                src_ref=slots, dst_ref=slots, send_sem=refs[2 * n + k], recv_sem=refs[3 * n + k],
                device_id=_place()[0], device_id_type=MESH)
            all_copies.wait_send()
            all_copies.wait_recv()

    hbm = lambda a: pltpu.HBM(a.shape, a.dtype)
    out = pl.pallas_call(
        body, name=name,
        in_specs=[HBM_SPEC] * (2 * n) + [SEM_SPEC] * (2 * n) + [pl.BlockSpec(memory_space=pl.ANY)],
        out_specs=[HBM_SPEC] * (2 * n),
        out_shape=[hbm(a) for a in srcs] + [hbm(a) for a in lands],
        input_output_aliases={k: k for k in range(2 * n)},
        compiler_params=pltpu.CompilerParams(has_side_effects=DATAFLOW),
    )(*srcs, *lands, *send_sems, *recv_sems, after)
    return out[:n], out[n:]


def _slot(chip, core):
    return 4 * chip[0] + 2 * chip[1] + core


def _gather_copies(src, land, send_sem, recv_sem):
    (x, y, c), sib, chips = _place()
    return [pltpu.make_async_remote_copy(src_ref=src, dst_ref=land.at[_slot((x, y), c)], send_sem=send_sem,
                                         recv_sem=recv_sem, device_id=to, device_id_type=MESH)
            for to in [sib] + [(*chip, c) for chip in chips]]


def _gather_pass_on(lands, name):
    n = len(lands)

    def body(*refs):
        ins, outs = refs[:n], refs[n:2 * n]
        send_sems, recv_sems = refs[2 * n:]
        (x, y, c), sib, chips = _place()
        sends = []
        for k in range(n):
            for j, chip in enumerate(chips):
                sends.append(pltpu.make_async_remote_copy(
                    src_ref=ins[k].at[_slot(chip, c)], dst_ref=outs[k].at[_slot(chip, c)],
                    send_sem=send_sems.at[k, j], recv_sem=recv_sems.at[k, j], device_id=sib, device_id_type=MESH))
        for cp in sends:
            cp.start()
        for cp in sends:
            cp.wait_recv()
        for cp in sends:
            cp.wait_send()

    any_spec = pl.BlockSpec(memory_space=pl.ANY)
    return pl.pallas_call(
        body, name=name,
        in_specs=[any_spec] * n, out_specs=[any_spec] * n,
        out_shape=[jax.ShapeDtypeStruct(a.shape, a.dtype) for a in lands],
        input_output_aliases={k: k for k in range(n)},
        scratch_shapes=[pltpu.SemaphoreType.DMA((n, 3))] * 2,
        compiler_params=pltpu.CompilerParams(has_side_effects=True),
    )(*lands)


def _chip_copies(src, land, send_sem, recv_sem):
    (x, y, c), _, chips = _place()
    return [pltpu.make_async_remote_copy(src_ref=src.at[2 * chip[0] + chip[1]], dst_ref=land.at[j], send_sem=send_sem,
                                         recv_sem=recv_sem, device_id=(*chip, c), device_id_type=MESH)
            for j, chip in enumerate(chips)]


def _scatter_to_sibling(blocks, name):
    n = len(blocks)

    def body(*refs):
        ins, theirs = refs[:n], refs[n:2 * n]
        send_sems, recv_sems, _ = refs[2 * n:]
        (x, y, c), sib, _ = _place()
        sends = []
        for k in range(n):
            for q in range(4):
                sends.append(pltpu.make_async_remote_copy(
                    src_ref=ins[k].at[2 * q + 1 - c], dst_ref=theirs[k].at[q], send_sem=send_sems.at[k, q],
                    recv_sem=recv_sems.at[k, q], device_id=sib, device_id_type=MESH))
        for cp in sends:
            cp.start()
        for cp in sends:
            cp.wait_recv()
        for cp in sends:
            cp.wait_send()

    out_shapes = [jax.ShapeDtypeStruct((4,) + a.shape[1:], a.dtype) for a in blocks]
    return _comm_call(body, name, blocks, out_shapes, 4)


def _pair_sum(core, blocks, theirs, name):
    q, r, c = theirs.shape
    tr = _pick(r, (256, 128, 64, 32, 16))

    def body(core_ref, a_ref, b_ref, o_ref):
        o_ref[...] = (a_ref[...].astype(F32) + b_ref[...].astype(F32)).astype(o_ref.dtype)

    blk = pl.BlockSpec((1, tr, c), lambda i, j, core_ref: (i, j, 0))
    mine = pl.BlockSpec((1, tr, c), lambda i, j, core_ref: (2 * i + core_ref[0], j, 0))
    return pl.pallas_call(
        body, name=name,
        grid_spec=pltpu.PrefetchScalarGridSpec(num_scalar_prefetch=1, grid=(q, r // tr),
                                               in_specs=[mine, blk], out_specs=blk),
        out_shape=jax.ShapeDtypeStruct(theirs.shape, theirs.dtype),
        compiler_params=_params(("parallel", "parallel")),
    )(core, blocks, theirs)


def _reduce_adamw(parts, w, m, v, name, own=None, own_slot=None):
    r, c = w.shape
    n_parts = parts.shape[0]
    tr = _pick(r, (256, 128, 64, 32, 16))
    c1 = 1.0 - ADAM_B1
    c2 = 1.0 - ADAM_B2
    bc1 = 1.0 - ADAM_B1 ** ADAM_STEP
    bc2 = 1.0 - ADAM_B2 ** ADAM_STEP

    def body(*refs):
        if own is None:
            p_ref, w_ref, m_ref, v_ref, g_ref, d_ref, nm_ref, nv_ref = refs
            g = p_ref[0].astype(F32)
            first = 1
        else:
            _, o_ref, p_ref, w_ref, m_ref, v_ref, g_ref, d_ref, nm_ref, nv_ref = refs
            g = o_ref[0].astype(F32)
            first = 0
        for j in range(first, n_parts):
            g = g + p_ref[j].astype(F32)
        nm = ADAM_B1 * m_ref[...] + c1 * g
        nv = ADAM_B2 * v_ref[...] + c2 * (g * g)
        g_ref[...] = g
        nm_ref[...] = nm
        nv_ref[...] = nv
        d_ref[...] = -ADAM_LR * ((nm / bc1) / (jnp.sqrt(nv / bc2) + ADAM_EPS) + ADAM_WD * w_ref[...])

    out = jax.ShapeDtypeStruct((r, c), F32)
    if own is None:
        blk = pl.BlockSpec((tr, c), lambda i: (i, 0))
        return pl.pallas_call(
            body, name=name, grid=(r // tr,),
            in_specs=[pl.BlockSpec((n_parts, tr, c), lambda i: (0, i, 0)), blk, blk, blk],
            out_specs=[blk] * 4, out_shape=[out] * 4,
            compiler_params=_params(("parallel",)),
        )(parts, w, m, v)
    blk = pl.BlockSpec((tr, c), lambda i, slot_ref: (i, 0))
    return pl.pallas_call(
        body, name=name,
        grid_spec=pltpu.PrefetchScalarGridSpec(
            num_scalar_prefetch=1, grid=(r // tr,),
            in_specs=[pl.BlockSpec((1, tr, c), lambda i, slot_ref: (slot_ref[0], i, 0)),
                      pl.BlockSpec((n_parts, tr, c), lambda i, slot_ref: (0, i, 0)), blk, blk, blk],
            out_specs=[blk] * 4),
        out_shape=[out] * 4,
        compiler_params=_params(("parallel",)),
    )(own_slot, own, parts, w, m, v)


_MIX = ("w_in", "w_uq", "w_ukv", "conv_w", "w_out")
_FFN = ("w_gate", "w_up", "w_down")
_BIG = _MIX + _FFN
_SMALL = ("pre_mix_norm", "q_norm", "kv_norm", "conv_b", "conv_ln_g", "conv_ln_b", "conv_out_norm",
          "attn_out_norm", "post_mix_norm", "pre_ffn_norm", "post_ffn_norm")
_ORDER = ("pre_mix_norm", "w_in", "q_norm", "w_uq", "kv_norm", "w_ukv", "conv_w", "conv_b", "conv_ln_g",
          "conv_ln_b", "conv_out_norm", "attn_out_norm", "w_out", "post_mix_norm", "pre_ffn_norm", "w_gate",
          "w_up", "w_down", "post_ffn_norm")


def _cols_from_shards(g):
    return jnp.transpose(g, (1, 0, 2)).reshape(g.shape[1], N_DEV * g.shape[2])


def _cols_to_shards(w):
    k, n8 = w.shape
    return jnp.transpose(w.reshape(k, N_DEV, n8 // N_DEV), (1, 0, 2))


def _step(x, positions, loss_target, w, m, v):
    s, d = x.shape[1], x.shape[2]
    x2, tgt = x[0], loss_target[0]
    pos = positions.reshape(s, 1)
    vecs = {n: w[n] for n in _SMALL}
    core = lax.axis_index("c").astype(jnp.int32).reshape(1)
    my_chip = (2 * lax.axis_index("x") + lax.axis_index("y")).astype(jnp.int32).reshape(1)
    n_in_cols = N_DEV * w["w_in"].shape[2]
    started = {}

    shards = {n: w[n][0] if n == "conv_w" else w[n][0].astype(BF16) for n in _BIG}

    def gather_start(names, tag, after):
        lands = [lax.empty((N_DEV,) + shards[n].shape, shards[n].dtype) for n in names]
        return _split_start("gather_" + tag + "_start", _gather_copies, [shards[n] for n in names], lands, after)

    def gather_finish(names, tag, after):
        srcs, lands = _split_wait("gather_" + tag + "_wait", 4, started[tag], after)
        lands = _gather_pass_on(lands, "gather_" + tag + "_pass_on")
        me = _my_index()
        return {n: lax.dynamic_update_slice(g, a[None], (me,) + (0,) * a.ndim) for n, g, a in zip(names, lands, srcs)}

    def scatter_start(names, blocks, tag):
        theirs = _scatter_to_sibling(blocks, "to_sibling_" + tag)
        pairs = [_pair_sum(core, b, t, "pair_sum_" + n) for n, b, t in zip(names, blocks, theirs)]
        lands = [lax.empty((3,) + p.shape[1:], p.dtype) for p in pairs]
        started[tag], zero = _split_start("scatter_" + tag + "_start", _chip_copies, pairs, lands, blocks[0])
        return zero

    def mix_weights_fn(h):
        gath = gather_finish(_MIX, "mix", h)
        started["ffn"], zero = gather_start(_FFN, "ffn", gath["w_in"])
        w_in_f = _cols_from_shards(gath["w_in"])
        w_in_p = jnp.pad(w_in_f, ((0, 0), (0, LANES - QK_ROPE)))
        w_uq_p = _cols_from_shards(jnp.pad(gath["w_uq"], ((0, 0), (0, 0), (0, HEAD_PAD - QK_HEAD))))
        return (w_in_p, w_uq_p, _cols_from_shards(gath["w_ukv"]), _cols_from_shards(gath["conv_w"]),
                gath["w_out"].reshape(-1, d), zero)

    def ffn_weights_fn(mix):
        gath = gather_finish(_FFN, "ffn", mix)
        return [gath[n] for n in _FFN]

    def ffn_grads_fn(dw_gate, dw_up, dw_down):
        return scatter_start(_FFN, [dw_gate, dw_up, dw_down], "ffn")

    started["mix"], zero = gather_start(_MIX, "mix", x2)
    vecs["pre_mix_norm"] = vecs["pre_mix_norm"] + zero
    loss, grad_x, g = _local_step(x2, pos, tgt, vecs, mix_weights_fn, ffn_weights_fn, ffn_grads_fn)

    mix_blocks = [_cols_to_shards(g["w_in_p"][:, :n_in_cols]), _cols_to_shards(g["w_uq_p"])[:, :, :QK_HEAD],
                  _cols_to_shards(g["w_ukv"]), _cols_to_shards(g["conv_w"]), g["w_out"].reshape(N_DEV, -1, d)]
    scatter_start(_MIX, mix_blocks, "mix")
    pairs, recv = {}, {}
    for names, tag, after in ((_FFN, "ffn", grad_x), (_MIX, "mix", grad_x)):
        srcs, lands = _split_wait("scatter_" + tag + "_wait", 3, started[tag], after)
        pairs.update(zip(names, srcs))
        recv.update(zip(names, lands))
    small = jnp.concatenate([g[n] for n in _SMALL], axis=1)
    small_all = _gather_direct([small], "gather_small_grads")[0]

    res = {}
    for n in _BIG:
        res[n] = _reduce_adamw(recv[n], w[n][0], m[n][0], v[n][0], "adamw_" + n, own=pairs[n], own_slot=my_chip)
        res[n] = [t[None] for t in res[n]]
    cat = lambda t: jnp.concatenate([t[n] for n in _SMALL], axis=1)
    sg, sd, sm, sv = _reduce_adamw(small_all, cat(w), cat(m), cat(v), "adamw_small")
    off = 0
    for n in _SMALL:
        width = w[n].shape[1]
        res[n] = [t[:, off:off + width] for t in (sg, sd, sm, sv)]
        off += width

    total = lax.psum(loss[0, 0], ("x", "y", "c"))
    outs = [total, grad_x[None]]
    for part in range(4):
        outs.extend(res[n][part] for n in _ORDER)
    return tuple(outs)


def kernel(x, positions, pre_mix_norm, w_in, q_norm, w_uq, kv_norm, w_ukv, conv_w, conv_b, conv_ln_g, conv_ln_b, conv_out_norm, attn_out_norm, w_out, post_mix_norm, pre_ffn_norm, w_gate, w_up, w_down, post_ffn_norm, loss_target, m_pre_mix_norm, m_w_in, m_q_norm, m_w_uq, m_kv_norm, m_w_ukv, m_conv_w, m_conv_b, m_conv_ln_g, m_conv_ln_b, m_conv_out_norm, m_attn_out_norm, m_w_out, m_post_mix_norm, m_pre_ffn_norm, m_w_gate, m_w_up, m_w_down, m_post_ffn_norm, v_pre_mix_norm, v_w_in, v_q_norm, v_w_uq, v_kv_norm, v_w_ukv, v_conv_w, v_conv_b, v_conv_ln_g, v_conv_ln_b, v_conv_out_norm, v_attn_out_norm, v_w_out, v_post_mix_norm, v_pre_ffn_norm, v_w_gate, v_w_up, v_w_down, v_post_ffn_norm):
    w = dict(zip(_ORDER, (pre_mix_norm, w_in, q_norm, w_uq, kv_norm, w_ukv, conv_w, conv_b, conv_ln_g, conv_ln_b,
                          conv_out_norm, attn_out_norm, w_out, post_mix_norm, pre_ffn_norm, w_gate, w_up, w_down,
                          post_ffn_norm)))
    m = dict(zip(_ORDER, (m_pre_mix_norm, m_w_in, m_q_norm, m_w_uq, m_kv_norm, m_w_ukv, m_conv_w, m_conv_b,
                          m_conv_ln_g, m_conv_ln_b, m_conv_out_norm, m_attn_out_norm, m_w_out, m_post_mix_norm,
                          m_pre_ffn_norm, m_w_gate, m_w_up, m_w_down, m_post_ffn_norm)))
    v = dict(zip(_ORDER, (v_pre_mix_norm, v_w_in, v_q_norm, v_w_uq, v_kv_norm, v_w_ukv, v_conv_w, v_conv_b,
                          v_conv_ln_g, v_conv_ln_b, v_conv_out_norm, v_attn_out_norm, v_w_out, v_post_mix_norm,
                          v_pre_ffn_norm, v_w_gate, v_w_up, v_w_down, v_post_ffn_norm)))
    return _step(x, positions, loss_target, w, m, v)
```

```python
import functools

import jax
import jax.numpy as jnp
from jax import lax
from jax.experimental import pallas as pl
from jax.experimental.pallas import tpu as pltpu

N_DEV = 8
N_HEADS = 8
QK_NOPE = 128
QK_ROPE = 64
V_HEAD = 128
QK_HEAD = QK_NOPE + QK_ROPE
HEAD_PAD = 256
LANES = 128
ATTN_BLOCK = 512
CONV_K = 31
CONV_PAD = 32
EPS = 1e-6
ROPE_THETA = 10000.0
ADAM_LR = 0.001
ADAM_B1 = 0.9
ADAM_B2 = 0.999
ADAM_EPS = 1e-08
ADAM_WD = 0.01
ADAM_STEP = 10
VMEM_LIMIT = 56 * 1024 * 1024
F32 = jnp.float32
BF16 = jnp.bfloat16
MESH = pl.DeviceIdType.MESH
NEG = -1e30


def _pick(n, prefs):
    for p in prefs:
        if p <= n and n % p == 0:
            return p
    return n


def _params(sem):
    return pltpu.CompilerParams(dimension_semantics=sem, vmem_limit_bytes=VMEM_LIMIT)


_DIMS = {"nn": (((1,), (0,)), ((), ())), "nt": (((1,), (1,)), ((), ())), "tn": (((0,), (0,)), ((), ()))}


MM_VMEM_BUDGET = 40 * 1024 * 1024
MM_MAX_MACS = 3 * 1024 ** 3


V7X_HBM_BYTES_PER_S = 3.0e12
V7X_MXU_MACS_PER_S = 0.45e15
GRID_STEP_S = 0.35e-6


def _mm_tiles(m, n, k, size_a, size_b, size_o):
    best = None
    for tm in sorted({m, 1024, 512, 256, 128}, reverse=True):
        if tm > m or m % tm:
            continue
        for tn in sorted({n, 2048, 1024, 512, 384, 256, 128}, reverse=True):
            if tn > n or n % tn:
                continue
            vmem = 2 * (tm * k * size_a + k * tn * size_b + tm * tn * size_o)
            if vmem > MM_VMEM_BUDGET or tm * tn * k > MM_MAX_MACS:
                continue
            b_reads = 1 if tn == n else m // tm
            traffic = m * k * size_a + b_reads * k * n * size_b + m * n * size_o
            exposed = tm * k * size_a + k * tn * size_b + tm * tn * size_o
            steps = (m // tm) * (n // tn)
            key = (max(traffic / V7X_HBM_BYTES_PER_S, m * n * k / V7X_MXU_MACS_PER_S)
                   + exposed / V7X_HBM_BYTES_PER_S + steps * GRID_STEP_S)
            if best is None or key < best[0]:
                best = (key, tm, tn)
    assert best is not None, (m, n, k)
    return best[1], best[2]


def _mm(a, b, mode, name, out_dtype=F32, add=None):
    if mode == "nn":
        (m, k), (k2, n) = a.shape, b.shape
    elif mode == "nt":
        (m, k), (n, k2) = a.shape, b.shape
    else:
        (k, m), (k2, n) = a.shape, b.shape
    assert k == k2, (a.shape, b.shape, mode)
    tm, tn = _mm_tiles(m, n, k, a.dtype.itemsize, b.dtype.itemsize,
                       jnp.dtype(out_dtype).itemsize + (0 if add is None else 4))
    dims = _DIMS[mode]

    def body(a_ref, b_ref, *rest):
        acc = lax.dot_general(a_ref[...].astype(BF16), b_ref[...].astype(BF16), dims, preferred_element_type=F32)
        if add is not None:
            acc = acc + rest[0][...]
        rest[-1][...] = acc.astype(rest[-1].dtype)

    if mode == "tn":
        a_spec = pl.BlockSpec((k, tm), lambda i, j: (0, i))
    else:
        a_spec = pl.BlockSpec((tm, k), lambda i, j: (i, 0))
    if mode == "nt":
        b_spec = pl.BlockSpec((tn, k), lambda i, j: (j, 0))
    else:
        b_spec = pl.BlockSpec((k, tn), lambda i, j: (0, j))
    o_spec = pl.BlockSpec((tm, tn), lambda i, j: (i, j))
    extra = [] if add is None else [add]
    return pl.pallas_call(
        body, name=name,
        grid=(m // tm, n // tn),
        in_specs=[a_spec, b_spec] + [o_spec] * len(extra),
        out_specs=o_spec,
        out_shape=jax.ShapeDtypeStruct((m, n), out_dtype),
        compiler_params=_params(("parallel", "parallel")),
    )(a, b, *extra)


def _sigmoid(x):
    return 1.0 / (1.0 + jnp.exp(-x))


def _rms(x, g):
    r = lax.rsqrt(jnp.mean(x * x, axis=-1, keepdims=True) + EPS)
    return (x * r) * g


def _rms_bwd(x, g, dy):
    r = lax.rsqrt(jnp.mean(x * x, axis=-1, keepdims=True) + EPS)
    xh = x * r
    dyg = dy * g
    dx = r * (dyg - xh * jnp.mean(dyg * xh, axis=-1, keepdims=True))
    return dx, jnp.sum(dy * xh, axis=0, keepdims=True)


def _ln(x, g, b):
    mu = jnp.mean(x, axis=-1, keepdims=True)
    xc = x - mu
    rs = lax.rsqrt(jnp.mean(xc * xc, axis=-1, keepdims=True) + EPS)
    return (xc * rs) * g + b


def _ln_bwd(x, g, dy):
    mu = jnp.mean(x, axis=-1, keepdims=True)
    xc = x - mu
    rs = lax.rsqrt(jnp.mean(xc * xc, axis=-1, keepdims=True) + EPS)
    xh = xc * rs
    dyg = dy * g
    dx = rs * (dyg - jnp.mean(dyg, axis=-1, keepdims=True) - xh * jnp.mean(dyg * xh, axis=-1, keepdims=True))
    return dx, jnp.sum(dy * xh, axis=0, keepdims=True), jnp.sum(dy, axis=0, keepdims=True)


def _silu(x):
    return x * _sigmoid(x)


def _silu_grad(x):
    s = _sigmoid(x)
    return s * (1.0 + x * (1.0 - s))


def _rope(x, cos, sa, sb):
    return x * cos + pltpu.roll(x, 96, 1) * sa + pltpu.roll(x, 32, 1) * sb


def _rope_t(d, cos, sa, sb):
    return d * cos - pltpu.roll(d, 96, 1) * sa - pltpu.roll(d, 32, 1) * sb


def _rows(ts, w):
    return pl.BlockSpec((ts, w), lambda i: (i, 0))


def _vec(w):
    return pl.BlockSpec((1, w), lambda i: (0, 0))


def _acc_init(i, *refs):
    @pl.when(i == 0)
    def _():
        for r in refs:
            r[...] = jnp.zeros_like(r)


def _rope_tables(pos, inv_freq):
    s = pos.shape[0]
    ts = _pick(s, (512, 256, 128))

    def body(p_ref, f_ref, c_ref, sa_ref, sb_ref):
        ang = p_ref[...].astype(F32) * f_ref[...]
        lane = lax.broadcasted_iota(jnp.int32, ang.shape, 1)
        c, sn = jnp.cos(ang), jnp.sin(ang)
        c_ref[...] = jnp.where(lane < QK_ROPE, c, 0.0)
        sa_ref[...] = jnp.where(lane < QK_ROPE // 2, -sn, 0.0)
        sb_ref[...] = jnp.where((lane >= QK_ROPE // 2) & (lane < QK_ROPE), sn, 0.0)

    out = jax.ShapeDtypeStruct((s, LANES), F32)
    return pl.pallas_call(
        body, name="rope_tables", grid=(s // ts,),
        in_specs=[_rows(ts, 1), _vec(LANES)],
        out_specs=[_rows(ts, LANES)] * 3, out_shape=[out] * 3,
        compiler_params=_params(("parallel",)),
    )(pos, inv_freq)


def _pre_fwd(x, g):
    s, d = x.shape
    ts = _pick(s, (256, 128))

    def body(x_ref, g_ref, h_ref):
        h_ref[...] = _rms(x_ref[...], g_ref[...]).astype(BF16)

    return pl.pallas_call(
        body, name="pre_fwd", grid=(s // ts,),
        in_specs=[_rows(ts, d), _vec(d)], out_specs=_rows(ts, d),
        out_shape=jax.ShapeDtypeStruct((s, d), BF16),
        compiler_params=_params(("parallel",)),
    )(x, g)


def _split_fwd(z, gq, gkv, tabs, c, ql, kvl):
    s, zw = z.shape
    ts = _pick(s, (256, 128))
    o_q, o_kv, o_kr = 2 * c, 2 * c + ql, 2 * c + ql + kvl

    def body(z_ref, gq_ref, gkv_ref, c_ref, sa_ref, sb_ref, u0_ref, qn_ref, kvn_ref, kpe_ref):
        u0_ref[...] = z_ref[:, 0:c] * _sigmoid(z_ref[:, c:2 * c])
        qn_ref[...] = _rms(z_ref[:, o_q:o_kv], gq_ref[...]).astype(BF16)
        kvn_ref[...] = _rms(z_ref[:, o_kv:o_kr], gkv_ref[...]).astype(BF16)
        kpe_ref[...] = _rope(z_ref[:, o_kr:o_kr + LANES], c_ref[...], sa_ref[...], sb_ref[...]).astype(BF16)

    return pl.pallas_call(
        body, name="split_fwd", grid=(s // ts,),
        in_specs=[_rows(ts, zw), _vec(ql), _vec(kvl)] + [_rows(ts, LANES)] * 3,
        out_specs=[_rows(ts, c), _rows(ts, ql), _rows(ts, kvl), _rows(ts, LANES)],
        out_shape=[jax.ShapeDtypeStruct((s, c), F32), jax.ShapeDtypeStruct((s, ql), BF16),
                   jax.ShapeDtypeStruct((s, kvl), BF16), jax.ShapeDtypeStruct((s, LANES), BF16)],
        compiler_params=_params(("parallel",)),
    )(z, gq, gkv, *tabs)


def _split_bwd(du0, z, dqn, dkvn, dkpe_h, gq, gkv, tabs, c, ql, kvl):
    s, zw = z.shape
    ts = _pick(s, (256, 128))
    o_q, o_kv, o_kr = 2 * c, 2 * c + ql, 2 * c + ql + kvl

    def body(du0_ref, z_ref, dqn_ref, dkvn_ref, dkh_ref, gq_ref, gkv_ref, c_ref, sa_ref, sb_ref,
             dz_ref, dgq_ref, dgkv_ref):
        _acc_init(pl.program_id(0), dgq_ref, dgkv_ref)
        du0 = du0_ref[...]
        a = z_ref[:, 0:c]
        sg = _sigmoid(z_ref[:, c:2 * c])
        dz_ref[:, 0:c] = (du0 * sg).astype(BF16)
        dz_ref[:, c:2 * c] = (du0 * a * sg * (1.0 - sg)).astype(BF16)
        dq, dgq = _rms_bwd(z_ref[:, o_q:o_kv], gq_ref[...], dqn_ref[...])
        dz_ref[:, o_q:o_kv] = dq.astype(BF16)
        dgq_ref[...] += dgq
        dkv, dgkv = _rms_bwd(z_ref[:, o_kv:o_kr], gkv_ref[...], dkvn_ref[...])
        dz_ref[:, o_kv:o_kr] = dkv.astype(BF16)
        dgkv_ref[...] += dgkv
        dk = dkh_ref[:, 0:LANES]
        for h in range(1, N_HEADS):
            dk = dk + dkh_ref[:, h * LANES:(h + 1) * LANES]
        dz_ref[:, o_kr:o_kr + LANES] = _rope_t(dk, c_ref[...], sa_ref[...], sb_ref[...]).astype(BF16)

    return pl.pallas_call(
        body, name="split_bwd", grid=(s // ts,),
        in_specs=[_rows(ts, c), _rows(ts, zw), _rows(ts, ql), _rows(ts, kvl), _rows(ts, N_HEADS * LANES),
                  _vec(ql), _vec(kvl)] + [_rows(ts, LANES)] * 3,
        out_specs=[_rows(ts, zw), _vec(ql), _vec(kvl)],
        out_shape=[jax.ShapeDtypeStruct((s, zw), BF16), jax.ShapeDtypeStruct((1, ql), F32),
                   jax.ShapeDtypeStruct((1, kvl), F32)],
        compiler_params=_params(("arbitrary",)),
    )(du0, z, dqn, dkvn, dkpe_h, gq, gkv, *tabs)


def _q_rope(qpre, tabs, transpose, out_dtype, name):
    s, w = qpre.shape
    ts = _pick(s, (256, 128))
    rot = _rope_t if transpose else _rope

    def body(q_ref, c_ref, sa_ref, sb_ref, o_ref):
        cs, sa, sb = c_ref[...], sa_ref[...], sb_ref[...]
        for h in range(N_HEADS):
            lo = h * HEAD_PAD
            o_ref[:, lo:lo + QK_NOPE] = q_ref[:, lo:lo + QK_NOPE].astype(out_dtype)
            o_ref[:, lo + QK_NOPE:lo + HEAD_PAD] = rot(q_ref[:, lo + QK_NOPE:lo + HEAD_PAD], cs, sa, sb).astype(out_dtype)

    return pl.pallas_call(
        body, name=name, grid=(s // ts,),
        in_specs=[_rows(ts, w)] + [_rows(ts, LANES)] * 3, out_specs=_rows(ts, w),
        out_shape=jax.ShapeDtypeStruct((s, w), out_dtype),
        compiler_params=_params(("parallel",)),
    )(qpre, *tabs)


def _conv_fwd(u0, w, b):
    s, c = u0.shape
    tc = LANES
    rc = _pick(s, (256, 128))

    def body(u_ref, w_ref, b_ref, o_ref, pad_ref):
        pad_ref[0:CONV_PAD, :] = jnp.zeros((CONV_PAD, tc), F32)
        pad_ref[CONV_PAD:CONV_PAD + s, :] = u_ref[...]
        for r in range(s // rc):
            acc = jnp.broadcast_to(b_ref[...], (rc, tc))
            for k in range(CONV_K):
                lo = r * rc + CONV_PAD - (CONV_K - 1) + k
                acc = acc + w_ref[k:k + 1, :] * pad_ref[lo:lo + rc, :]
            o_ref[r * rc:(r + 1) * rc, :] = acc

    col = lambda j: (0, j)
    return pl.pallas_call(
        body, name="conv_fwd", grid=(c // tc,),
        in_specs=[pl.BlockSpec((s, tc), col), pl.BlockSpec((CONV_K, tc), col), pl.BlockSpec((1, tc), col)],
        out_specs=pl.BlockSpec((s, tc), col),
        out_shape=jax.ShapeDtypeStruct((s, c), F32),
        scratch_shapes=[pltpu.VMEM((s + CONV_PAD, tc), F32)],
        compiler_params=_params(("parallel",)),
    )(u0, w, b)


def _conv_bwd(du1, u0, w):
    s, c = u0.shape
    tc = LANES
    rc = _pick(s, (256, 128))

    def body(d_ref, u_ref, w_ref, du_ref, dw_ref, db_ref, upad_ref, dpad_ref):
        upad_ref[0:CONV_PAD, :] = jnp.zeros((CONV_PAD, tc), F32)
        upad_ref[CONV_PAD:CONV_PAD + s, :] = u_ref[...]
        dpad_ref[0:s, :] = d_ref[...]
        dpad_ref[s:s + CONV_PAD, :] = jnp.zeros((CONV_PAD, tc), F32)
        for r in range(s // rc):
            acc = jnp.zeros((rc, tc), F32)
            for k in range(CONV_K):
                lo = r * rc + (CONV_K - 1) - k
                acc = acc + w_ref[k:k + 1, :] * dpad_ref[lo:lo + rc, :]
            du_ref[r * rc:(r + 1) * rc, :] = acc
        for k in range(CONV_K):
            acc8 = jnp.zeros((8, tc), F32)
            for r in range(s // rc):
                lo = r * rc + CONV_PAD - (CONV_K - 1) + k
                prod = d_ref[r * rc:(r + 1) * rc, :] * upad_ref[lo:lo + rc, :]
                acc8 = acc8 + jnp.sum(prod.reshape(rc // 8, 8, tc), axis=0)
            dw_ref[k:k + 1, :] = jnp.sum(acc8, axis=0, keepdims=True)
        db_ref[...] = jnp.sum(d_ref[...], axis=0, keepdims=True)

    col = lambda j: (0, j)
    return pl.pallas_call(
        body, name="conv_bwd", grid=(c // tc,),
        in_specs=[pl.BlockSpec((s, tc), col), pl.BlockSpec((s, tc), col), pl.BlockSpec((CONV_K, tc), col)],
        out_specs=[pl.BlockSpec((s, tc), col), pl.BlockSpec((CONV_K, tc), col), pl.BlockSpec((1, tc), col)],
        out_shape=[jax.ShapeDtypeStruct((s, c), F32), jax.ShapeDtypeStruct((CONV_K, c), F32),
                   jax.ShapeDtypeStruct((1, c), F32)],
        scratch_shapes=[pltpu.VMEM((s + CONV_PAD, tc), F32), pltpu.VMEM((s + CONV_PAD, tc), F32)],
        compiler_params=_params(("parallel",)),
    )(du1, u0, w)


def _causal_mask(sc, qi, kj, tq, tk):
    rows = qi * tq + lax.broadcasted_iota(jnp.int32, sc.shape, 0)
    cols = kj * tk + lax.broadcasted_iota(jnp.int32, sc.shape, 1)
    return jnp.where(cols <= rows, sc, NEG)


def _attn_fwd(q, kv, kpe):
    s = q.shape[0]
    tq = tk = _pick(s, (ATTN_BLOCK, 256, 128))
    reps = tk // LANES
    scale = QK_HEAD ** -0.5
    nt = (((1,), (1,)), ((), ()))

    def body(q_ref, kn_ref, v_ref, kpe_ref, o_ref, lse_ref, kf_ref, vb_ref, m_ref, l_ref, acc_ref):
        i = pl.program_id(1)

        @pl.when(i == 0)
        def _():
            kf_ref[:, 0:QK_NOPE] = kn_ref[...].astype(BF16)
            kf_ref[:, QK_NOPE:HEAD_PAD] = kpe_ref[...]
            vb_ref[...] = v_ref[...].astype(BF16)

        qb = q_ref[...]
        m_ref[...] = jnp.full((tq, LANES), NEG, F32)
        l_ref[...] = jnp.zeros((tq, LANES), F32)
        acc_ref[...] = jnp.zeros((tq, V_HEAD), F32)

        def block(j, diagonal):
            off = pl.multiple_of(j * tk, tk)
            sc = lax.dot_general(qb, kf_ref[pl.ds(off, tk), :], nt, preferred_element_type=F32) * scale
            if diagonal:
                sc = _causal_mask(sc, 0, 0, tq, tk)
            m_prev = m_ref[...]
            m_new = jnp.maximum(m_prev, jnp.max(sc, axis=1, keepdims=True))
            p = jnp.exp(sc - jnp.tile(m_new, (1, reps)))
            alpha = jnp.exp(m_prev - m_new)
            l_ref[...] = alpha * l_ref[...] + jnp.sum(p, axis=1, keepdims=True)
            acc_ref[...] = alpha * acc_ref[...] + jnp.dot(p.astype(BF16), vb_ref[pl.ds(off, tk), :],
                                                          preferred_element_type=F32)
            m_ref[...] = m_new

        def below_diagonal(j, carry):
            block(j, False)
            return carry

        lax.fori_loop(0, i, below_diagonal, 0)
        block(i, True)
        o_ref[...] = acc_ref[...] / l_ref[...]
        lse_ref[...] = m_ref[...] + jnp.log(l_ref[...])

    return pl.pallas_call(
        body, name="attn_fwd", grid=(N_HEADS, s // tq),
        in_specs=[pl.BlockSpec((tq, HEAD_PAD), lambda h, i: (i, h)),
                  pl.BlockSpec((s, QK_NOPE), lambda h, i: (0, 2 * h)),
                  pl.BlockSpec((s, V_HEAD), lambda h, i: (0, 2 * h + 1)),
                  pl.BlockSpec((s, LANES), lambda h, i: (0, 0))],
        out_specs=[pl.BlockSpec((tq, V_HEAD), lambda h, i: (i, h)),
                   pl.BlockSpec((tq, LANES), lambda h, i: (i, h))],
        out_shape=[jax.ShapeDtypeStruct((s, N_HEADS * V_HEAD), F32),
                   jax.ShapeDtypeStruct((s, N_HEADS * LANES), F32)],
        scratch_shapes=[pltpu.VMEM((s, HEAD_PAD), BF16), pltpu.VMEM((s, V_HEAD), BF16),
                        pltpu.VMEM((tq, LANES), F32), pltpu.VMEM((tq, LANES), F32), pltpu.VMEM((tq, V_HEAD), F32)],
        compiler_params=_params(("parallel", "arbitrary")),
    )(q, kv, kv, kpe)


def _attn_bwd(q, kv, kpe, o, do, lse):
    s = q.shape[0]
    tq = tk = _pick(s, (ATTN_BLOCK, 256, 128))
    nq = s // tq
    reps = tk // LANES
    scale = QK_HEAD ** -0.5
    nt = (((1,), (1,)), ((), ()))
    tn = (((0,), (0,)), ((), ()))

    def body(q_ref, kn_ref, v_ref, kpe_ref, o_ref, do_ref, lse_ref, dq_ref, dkv_ref, dkpe_ref,
             kf_ref, vb_ref, dk_ref, dv_ref):
        j = pl.program_id(1)

        @pl.when(j == 0)
        def _():
            dq_ref[...] = jnp.zeros_like(dq_ref)

        kf_ref[:, 0:QK_NOPE] = kn_ref[...].astype(BF16)
        kf_ref[:, QK_NOPE:HEAD_PAD] = kpe_ref[...]
        vb_ref[...] = v_ref[...].astype(BF16)
        dk_ref[...] = jnp.zeros_like(dk_ref)
        dv_ref[...] = jnp.zeros_like(dv_ref)

        def block(i, diagonal):
            off = pl.multiple_of(i * tq, tq)
            qb = q_ref[pl.ds(off, tq), :]
            dob = do_ref[pl.ds(off, tq), :]
            delta = jnp.sum(dob * o_ref[pl.ds(off, tq), :], axis=1, keepdims=True)
            sc = lax.dot_general(qb, kf_ref[...], nt, preferred_element_type=F32) * scale
            if diagonal:
                sc = _causal_mask(sc, 0, 0, tq, tk)
            p = jnp.exp(sc - jnp.tile(lse_ref[pl.ds(off, tq), :], (1, reps)))
            dob16 = dob.astype(BF16)
            dv_ref[...] += lax.dot_general(p.astype(BF16), dob16, tn, preferred_element_type=F32)
            dp = lax.dot_general(dob16, vb_ref[...], nt, preferred_element_type=F32)
            ds = (p * (dp - delta) * scale).astype(BF16)
            dq_ref[pl.ds(off, tq), :] += jnp.dot(ds, kf_ref[...], preferred_element_type=F32)
            dk_ref[...] += lax.dot_general(ds, qb, tn, preferred_element_type=F32)

        def above_diagonal(i, carry):
            block(i, False)
            return carry

        block(j, True)
        lax.fori_loop(j + 1, nq, above_diagonal, 0)
        dkv_ref[:, 0:QK_NOPE] = dk_ref[:, 0:QK_NOPE]
        dkv_ref[:, QK_NOPE:HEAD_PAD] = dv_ref[...]
        dkpe_ref[...] = dk_ref[:, QK_NOPE:HEAD_PAD]

    head_rows = lambda w: pl.BlockSpec((s, w), lambda h, j: (0, h))
    return pl.pallas_call(
        body, name="attn_bwd", grid=(N_HEADS, s // tk),
        in_specs=[head_rows(HEAD_PAD),
                  pl.BlockSpec((tk, QK_NOPE), lambda h, j: (j, 2 * h)),
                  pl.BlockSpec((tk, V_HEAD), lambda h, j: (j, 2 * h + 1)),
                  pl.BlockSpec((tk, LANES), lambda h, j: (j, 0)),
                  head_rows(V_HEAD), head_rows(V_HEAD), head_rows(LANES)],
        out_specs=[head_rows(HEAD_PAD),
                   pl.BlockSpec((tk, HEAD_PAD), lambda h, j: (j, h)),
                   pl.BlockSpec((tk, LANES), lambda h, j: (j, h))],
        out_shape=[jax.ShapeDtypeStruct((s, N_HEADS * HEAD_PAD), F32),
                   jax.ShapeDtypeStruct((s, N_HEADS * HEAD_PAD), F32),
                   jax.ShapeDtypeStruct((s, N_HEADS * LANES), F32)],
        scratch_shapes=[pltpu.VMEM((tk, HEAD_PAD), BF16), pltpu.VMEM((tk, V_HEAD), BF16),
                        pltpu.VMEM((tk, HEAD_PAD), F32), pltpu.VMEM((tk, V_HEAD), F32)],
        compiler_params=_params(("parallel", "arbitrary")),
    )(q, kv, kv, kpe, o, do, lse)


def _mix_fwd(u1, lng, lnb, gcon, attn, gattn):
    s, c = u1.shape
    ac = attn.shape[1]
    ts = _pick(s, (256, 128))

    def body(u_ref, lg_ref, lb_ref, gc_ref, a_ref, ga_ref, o_ref):
        t3 = _silu(_ln(u_ref[...], lg_ref[...], lb_ref[...]))
        o_ref[:, 0:c] = _rms(t3, gc_ref[...]).astype(BF16)
        o_ref[:, c:c + ac] = _rms(a_ref[...], ga_ref[...]).astype(BF16)

    return pl.pallas_call(
        body, name="mix_fwd", grid=(s // ts,),
        in_specs=[_rows(ts, c), _vec(c), _vec(c), _vec(c), _rows(ts, ac), _vec(ac)],
        out_specs=_rows(ts, c + ac),
        out_shape=jax.ShapeDtypeStruct((s, c + ac), BF16),
        compiler_params=_params(("parallel",)),
    )(u1, lng, lnb, gcon, attn, gattn)


def _mix_bwd(dmixin, u1, lng, lnb, gcon, attn, gattn):
    s, c = u1.shape
    ac = attn.shape[1]
    ts = _pick(s, (256, 128))

    def body(d_ref, u_ref, lg_ref, lb_ref, gc_ref, a_ref, ga_ref,
             du_ref, da_ref, dlg_ref, dlb_ref, dgc_ref, dga_ref):
        _acc_init(pl.program_id(0), dlg_ref, dlb_ref, dgc_ref, dga_ref)
        u = u_ref[...]
        t2 = _ln(u, lg_ref[...], lb_ref[...])
        dt3, dgc = _rms_bwd(_silu(t2), gc_ref[...], d_ref[:, 0:c])
        du, dlg, dlb = _ln_bwd(u, lg_ref[...], dt3 * _silu_grad(t2))
        du_ref[...] = du
        dlg_ref[...] += dlg
        dlb_ref[...] += dlb
        dgc_ref[...] += dgc
        da, dga = _rms_bwd(a_ref[...], ga_ref[...], d_ref[:, c:c + ac])
        da_ref[...] = da
        dga_ref[...] += dga

    return pl.pallas_call(
        body, name="mix_bwd", grid=(s // ts,),
        in_specs=[_rows(ts, c + ac), _rows(ts, c), _vec(c), _vec(c), _vec(c), _rows(ts, ac), _vec(ac)],
        out_specs=[_rows(ts, c), _rows(ts, ac), _vec(c), _vec(c), _vec(c), _vec(ac)],
        out_shape=[jax.ShapeDtypeStruct((s, c), F32), jax.ShapeDtypeStruct((s, ac), F32),
                   jax.ShapeDtypeStruct((1, c), F32), jax.ShapeDtypeStruct((1, c), F32),
                   jax.ShapeDtypeStruct((1, c), F32), jax.ShapeDtypeStruct((1, ac), F32)],
        compiler_params=_params(("arbitrary",)),
    )(dmixin, u1, lng, lnb, gcon, attn, gattn)


def _post_mix_fwd(x, mix, gpost, gpre):
    s, d = x.shape
    ts = _pick(s, (256, 128))

    def body(x_ref, m_ref, gp_ref, gf_ref, x1_ref, hf_ref):
        x1 = x_ref[...] + _rms(m_ref[...], gp_ref[...])
        x1_ref[...] = x1
        hf_ref[...] = _rms(x1, gf_ref[...]).astype(BF16)

    return pl.pallas_call(
        body, name="post_mix_fwd", grid=(s // ts,),
        in_specs=[_rows(ts, d), _rows(ts, d), _vec(d), _vec(d)],
        out_specs=[_rows(ts, d), _rows(ts, d)],
        out_shape=[jax.ShapeDtypeStruct((s, d), F32), jax.ShapeDtypeStruct((s, d), BF16)],
        compiler_params=_params(("parallel",)),
    )(x, mix, gpost, gpre)


def _post_mix_bwd(dy, dhf, x1, gpre, mix, gpost):
    s, d = x1.shape
    ts = _pick(s, (256, 128))

    def body(dy_ref, dh_ref, x1_ref, gf_ref, m_ref, gp_ref, dx1_ref, dm_ref, dgf_ref, dgp_ref):
        _acc_init(pl.program_id(0), dgf_ref, dgp_ref)
        dxa, dgf = _rms_bwd(x1_ref[...], gf_ref[...], dh_ref[...])
        dx1 = dy_ref[...] + dxa
        dx1_ref[...] = dx1
        dgf_ref[...] += dgf
        dm, dgp = _rms_bwd(m_ref[...], gp_ref[...], dx1)
        dm_ref[...] = dm.astype(BF16)
        dgp_ref[...] += dgp

    return pl.pallas_call(
        body, name="post_mix_bwd", grid=(s // ts,),
        in_specs=[_rows(ts, d), _rows(ts, d), _rows(ts, d), _vec(d), _rows(ts, d), _vec(d)],
        out_specs=[_rows(ts, d), _rows(ts, d), _vec(d), _vec(d)],
        out_shape=[jax.ShapeDtypeStruct((s, d), F32), jax.ShapeDtypeStruct((s, d), BF16),
                   jax.ShapeDtypeStruct((1, d), F32), jax.ShapeDtypeStruct((1, d), F32)],
        compiler_params=_params(("arbitrary",)),
    )(dy, dhf, x1, gpre, mix, gpost)


def _ffn_up(hf, wg, wu):
    s, d = hf.shape
    nsh, _, fs = wg.shape
    tm = _pick(s, (1024, 512, 256, 128))

    def body(h_ref, wg_ref, wu_ref, g_ref, u_ref, a_ref):
        h = h_ref[...]
        g = jnp.dot(h, wg_ref[...], preferred_element_type=F32)
        u = jnp.dot(h, wu_ref[...], preferred_element_type=F32)
        g_ref[...] = g
        u_ref[...] = u
        a_ref[...] = (_silu(g) * u).astype(BF16)

    w_spec = pl.BlockSpec((None, d, fs), lambda i, j: (j, 0, 0))
    o_spec = pl.BlockSpec((None, tm, fs), lambda i, j: (j, i, 0))
    return pl.pallas_call(
        body, name="ffn_up", grid=(s // tm, nsh),
        in_specs=[pl.BlockSpec((tm, d), lambda i, j: (i, 0)), w_spec, w_spec],
        out_specs=[o_spec] * 3,
        out_shape=[jax.ShapeDtypeStruct((nsh, s, fs), F32)] * 2 + [jax.ShapeDtypeStruct((nsh, s, fs), BF16)],
        compiler_params=_params(("parallel", "parallel")),
    )(hf, wg, wu)


def _ffn_down(act, wd):
    nsh, s, fs = act.shape
    d = wd.shape[2]
    tm = _pick(s, (1024, 512, 256, 128))
    tn = _pick(d, (512, 256, 128))

    def body(a_ref, w_ref, o_ref):
        acc = jnp.dot(a_ref[0], w_ref[0], preferred_element_type=F32)
        for j in range(1, nsh):
            acc = acc + jnp.dot(a_ref[j], w_ref[j], preferred_element_type=F32)
        o_ref[...] = acc

    return pl.pallas_call(
        body, name="ffn_down", grid=(s // tm, d // tn),
        in_specs=[pl.BlockSpec((nsh, tm, fs), lambda i, j: (0, i, 0)),
                  pl.BlockSpec((nsh, fs, tn), lambda i, j: (0, 0, j))],
        out_specs=pl.BlockSpec((tm, tn), lambda i, j: (i, j)),
        out_shape=jax.ShapeDtypeStruct((s, d), F32),
        compiler_params=_params(("parallel", "parallel")),
    )(act, wd)


def _ffn_down_bwd(dff, wd, gate, up, behind):
    s, d = dff.shape
    nsh, fs, _ = wd.shape
    tm = _pick(s, (1024, 512, 256, 128))
    nt = (((1,), (1,)), ((), ()))

    def body(d_ref, w_ref, g_ref, u_ref, _, dg_ref, du_ref):
        dact = lax.dot_general(d_ref[...], w_ref[...], nt, preferred_element_type=F32)
        g = g_ref[...]
        dg_ref[...] = (dact * u_ref[...] * _silu_grad(g)).astype(BF16)
        du_ref[...] = (dact * _silu(g)).astype(BF16)

    h_spec = pl.BlockSpec((None, tm, fs), lambda i, j: (j, i, 0))
    return pl.pallas_call(
        body, name="ffn_down_bwd", grid=(s // tm, nsh),
        in_specs=[pl.BlockSpec((tm, d), lambda i, j: (i, 0)),
                  pl.BlockSpec((None, fs, d), lambda i, j: (j, 0, 0)), h_spec, h_spec,
                  pl.BlockSpec((8, LANES), lambda i, j: (0, 0))],
        out_specs=[h_spec] * 2,
        out_shape=[jax.ShapeDtypeStruct((nsh, s, fs), BF16)] * 2,
        compiler_params=_params(("parallel", "parallel")),
    )(dff, wd, gate, up, behind)


def _ffn_dw_down(act, dff):
    nsh, s, fs = act.shape
    d = dff.shape[1]
    tn = (((0,), (0,)), ((), ()))

    def body(a_ref, d_ref, o_ref):
        o_ref[...] = lax.dot_general(a_ref[...], d_ref[...], tn, preferred_element_type=F32).astype(BF16)

    return pl.pallas_call(
        body, name="ffn_dw_down", grid=(nsh,),
        in_specs=[pl.BlockSpec((None, s, fs), lambda j: (j, 0, 0)), pl.BlockSpec((s, d), lambda j: (0, 0))],
        out_specs=pl.BlockSpec((None, fs, d), lambda j: (j, 0, 0)),
        out_shape=jax.ShapeDtypeStruct((nsh, fs, d), BF16),
        compiler_params=_params(("parallel",)),
    )(act, dff)


def _ffn_dw_up(hf, dh, name):
    s, d = hf.shape
    nsh, _, fs = dh.shape
    tn = (((0,), (0,)), ((), ()))

    def body(h_ref, d_ref, o_ref):
        o_ref[...] = lax.dot_general(h_ref[...], d_ref[...], tn, preferred_element_type=F32).astype(BF16)

    return pl.pallas_call(
        body, name=name, grid=(nsh,),
        in_specs=[pl.BlockSpec((s, d), lambda j: (0, 0)), pl.BlockSpec((None, s, fs), lambda j: (j, 0, 0))],
        out_specs=pl.BlockSpec((None, d, fs), lambda j: (j, 0, 0)),
        out_shape=jax.ShapeDtypeStruct((nsh, d, fs), BF16),
        compiler_params=_params(("parallel",)),
    )(hf, dh)


def _ffn_up_bwd(dh, w, name, add=None):
    nsh, s, fs = dh.shape
    d = w.shape[1]
    tm = _pick(s, (1024, 512, 256, 128))
    tn = _pick(d, (256, 128))
    nt = (((1,), (1,)), ((), ()))

    def body(a_ref, w_ref, *rest):
        acc = lax.dot_general(a_ref[0], w_ref[0], nt, preferred_element_type=F32)
        for j in range(1, nsh):
            acc = acc + lax.dot_general(a_ref[j], w_ref[j], nt, preferred_element_type=F32)
        if add is not None:
            acc = acc + rest[0][...]
        rest[-1][...] = acc

    o_spec = pl.BlockSpec((tm, tn), lambda i, j: (i, j))
    extra = [] if add is None else [add]
    return pl.pallas_call(
        body, name=name, grid=(s // tm, d // tn),
        in_specs=[pl.BlockSpec((nsh, tm, fs), lambda i, j: (0, i, 0)),
                  pl.BlockSpec((nsh, tn, fs), lambda i, j: (0, j, 0))] + [o_spec] * len(extra),
        out_specs=o_spec,
        out_shape=jax.ShapeDtypeStruct((s, d), F32),
        compiler_params=_params(("parallel", "parallel")),
    )(dh, w, *extra)


def _final(ff, x1, tgt, g):
    s, d = x1.shape
    ts = _pick(s, (256, 128))

    def body(ff_ref, x1_ref, t_ref, g_ref, loss_ref, dy_ref, dff_ref, dg_ref):
        _acc_init(pl.program_id(0), loss_ref, dg_ref)
        ff_v = ff_ref[...]
        err = x1_ref[...] + _rms(ff_v, g_ref[...]) - t_ref[...]
        tok = jnp.mean(err * err, axis=-1, keepdims=True)
        loss_ref[...] += 0.5 * jnp.sum(tok, axis=0, keepdims=True)
        dy = err * (1.0 / d)
        dy_ref[...] = dy
        dff, dg = _rms_bwd(ff_v, g_ref[...], dy)
        dff_ref[...] = dff.astype(BF16)
        dg_ref[...] += dg

    return pl.pallas_call(
        body, name="final", grid=(s // ts,),
        in_specs=[_rows(ts, d), _rows(ts, d), _rows(ts, d), _vec(d)],
        out_specs=[_vec(LANES), _rows(ts, d), _rows(ts, d), _vec(d)],
        out_shape=[jax.ShapeDtypeStruct((1, LANES), F32), jax.ShapeDtypeStruct((s, d), F32),
                   jax.ShapeDtypeStruct((s, d), BF16), jax.ShapeDtypeStruct((1, d), F32)],
        compiler_params=_params(("arbitrary",)),
    )(ff, x1, tgt, g)


def _pre_bwd(dx1, dh, x, g):
    s, d = x.shape
    ts = _pick(s, (256, 128))

    def body(dx1_ref, dh_ref, x_ref, g_ref, dx_ref, dg_ref):
        _acc_init(pl.program_id(0), dg_ref)
        dxa, dg = _rms_bwd(x_ref[...], g_ref[...], dh_ref[...])
        dx_ref[...] = dx1_ref[...] + dxa
        dg_ref[...] += dg

    return pl.pallas_call(
        body, name="pre_bwd", grid=(s // ts,),
        in_specs=[_rows(ts, d), _rows(ts, d), _rows(ts, d), _vec(d)],
        out_specs=[_rows(ts, d), _vec(d)],
        out_shape=[jax.ShapeDtypeStruct((s, d), F32), jax.ShapeDtypeStruct((1, d), F32)],
        compiler_params=_params(("arbitrary",)),
    )(dx1, dh, x, g)


def _local_step(x, pos, tgt, vecs, mix_weights_fn, up_weights_fn, down_weights_fn, grads_fn):
    c = vecs["conv_b"].shape[1]
    ql = vecs["q_norm"].shape[1]
    kvl = vecs["kv_norm"].shape[1]
    half = jnp.arange(0, QK_ROPE, 2, dtype=F32)
    freq = ROPE_THETA ** (-half / QK_ROPE)
    inv_freq = jnp.concatenate([freq, freq, jnp.zeros((LANES - QK_ROPE,), F32)])[None, :]
    tabs = _rope_tables(pos, inv_freq)

    h = _pre_fwd(x, vecs["pre_mix_norm"])
    w_in_p, w_uq_p, w_ukv, conv_w, w_out, zero = mix_weights_fn(h)
    z = _mm(h, w_in_p, "nn", "mm_z")
    u0, qn, kvn, kpe = _split_fwd(z, vecs["q_norm"] + zero, vecs["kv_norm"], tabs, c, ql, kvl)
    u1 = _conv_fwd(u0, conv_w, vecs["conv_b"])
    q = _q_rope(_mm(qn, w_uq_p, "nn", "mm_q"), tabs, False, BF16, "q_rope")
    kv = _mm(kvn, w_ukv, "nn", "mm_kv")
    attn, lse = _attn_fwd(q, kv, kpe)
    mixin = _mix_fwd(u1, vecs["conv_ln_g"], vecs["conv_ln_b"], vecs["conv_out_norm"], attn, vecs["attn_out_norm"])
    mix = _mm(mixin, w_out, "nn", "mm_mix")
    x1, hf = _post_mix_fwd(x, mix, vecs["post_mix_norm"], vecs["pre_ffn_norm"])
    w_gate, w_up = up_weights_fn(mix)
    gate, up, act = _ffn_up(hf, w_gate, w_up)
    w_down = down_weights_fn(act)
    ff = _ffn_down(act, w_down)
    loss, dy, dff, d_post_ffn = _final(ff, x1, tgt, vecs["post_ffn_norm"])

    g = {"post_ffn_norm": d_post_ffn}
    zero = grads_fn("down", ("w_down",), [_ffn_dw_down(act, dff)], dff)
    dgate, dup = _ffn_down_bwd(dff, w_down, gate, up, jnp.zeros((8, LANES), F32) + zero)
    dw_gate = _ffn_dw_up(hf, dgate, "ffn_dw_gate")
    dw_up = _ffn_dw_up(hf, dup, "ffn_dw_up")
    zero = zero + grads_fn("up", ("w_gate", "w_up"), [dw_gate, dw_up], dgate)
    dhf = _ffn_up_bwd(dup, w_up, "ffn_dhf_up", add=_ffn_up_bwd(dgate, w_gate, "ffn_dhf_gate"))
    dx1, dmix, g["pre_ffn_norm"], g["post_mix_norm"] = _post_mix_bwd(
        dy, dhf, x1, vecs["pre_ffn_norm"] + zero, mix, vecs["post_mix_norm"])
    dmixin = _mm(dmix, w_out, "nt", "mm_dmixin")
    dw_out = _mm(mixin, dmix, "tn", "mm_dw_out", BF16)
    zero = grads_fn("out", ("w_out",), [dw_out.reshape(N_DEV, -1, dw_out.shape[1])], dmix)
    du1, dattn, g["conv_ln_g"], g["conv_ln_b"], g["conv_out_norm"], g["attn_out_norm"] = _mix_bwd(
        dmixin, u1, vecs["conv_ln_g"] + zero, vecs["conv_ln_b"], vecs["conv_out_norm"], attn, vecs["attn_out_norm"])
    du0, g["conv_w"], g["conv_b"] = _conv_bwd(du1, u0, conv_w)
    dq, dkv, dkpe_h = _attn_bwd(q, kv, kpe, attn, dattn, lse)
    dqpre = _q_rope(dq, tabs, True, BF16, "q_rope_bwd")
    dqn = _mm(dqpre, w_uq_p, "nt", "mm_dqn")
    g["w_uq_p"] = _mm(qn, dqpre, "tn", "mm_dw_uq", BF16)
    dkvn = _mm(dkv, w_ukv, "nt", "mm_dkvn")
    g["w_ukv"] = _mm(kvn, dkv, "tn", "mm_dw_ukv", BF16)
    dz, g["q_norm"], g["kv_norm"] = _split_bwd(du0, z, dqn, dkvn, dkpe_h, vecs["q_norm"], vecs["kv_norm"], tabs, c, ql, kvl)
    dh = _mm(dz, w_in_p, "nt", "mm_dh")
    g["w_in_p"] = _mm(h, dz, "tn", "mm_dw_in", BF16)
    grad_x, g["pre_mix_norm"] = _pre_bwd(dx1, dh, x, vecs["pre_mix_norm"])
    return loss, grad_x, g


def _my_index():
    return 4 * lax.axis_index("x") + 2 * lax.axis_index("y") + lax.axis_index("c")


def _coords(idx):
    return ((idx >> 2) & 1, (idx >> 1) & 1, idx & 1)


def _place():
    x, y, c = lax.axis_index("x"), lax.axis_index("y"), lax.axis_index("c")
    return (x, y, c), (x, y, 1 - c), [(1 - x, y), (x, 1 - y), (1 - x, 1 - y)]


def _comm_call(body, name, arrays, out_shapes, n_sems):
    n = len(arrays)
    any_spec = pl.BlockSpec(memory_space=pl.ANY)
    return pl.pallas_call(
        body, name=name,
        in_specs=[any_spec] * n, out_specs=[any_spec] * len(out_shapes), out_shape=out_shapes,
        scratch_shapes=[pltpu.SemaphoreType.DMA((n, n_sems))] * 3,
        compiler_params=pltpu.CompilerParams(has_side_effects=True),
    )(*arrays)


def _gather_direct(arrays, name):
    n = len(arrays)

    def body(*refs):
        ins, outs = refs[:n], refs[n:2 * n]
        send_sems, recv_sems, local_sems = refs[2 * n:]
        me = _my_index()
        local = [pltpu.make_async_copy(ins[k], outs[k].at[me], local_sems.at[k, 0]) for k in range(n)]
        for cp in local:
            cp.start()

        def copy(k, p, slot):
            return pltpu.make_async_remote_copy(
                src_ref=ins[k], dst_ref=outs[k].at[slot], send_sem=send_sems.at[k, p - 1],
                recv_sem=recv_sems.at[k, p - 1], device_id=_coords(me ^ p), device_id_type=MESH)

        sends = [copy(k, p, me) for p in range(1, N_DEV) for k in range(n)]
        for cp in sends:
            cp.start()
        for p in range(1, N_DEV):
            for k in range(n):
                copy(k, p, me ^ p).wait_recv()
        for cp in sends:
            cp.wait_send()
        for cp in local:
            cp.wait()

    out_shapes = [jax.ShapeDtypeStruct((N_DEV,) + a.shape, a.dtype) for a in arrays]
    return _comm_call(body, name, arrays, out_shapes, N_DEV - 1)


HBM_SPEC = pl.BlockSpec(memory_space=pltpu.HBM)
SEM_SPEC = pl.BlockSpec(memory_space=pltpu.SEMAPHORE)
DATAFLOW = pltpu.SideEffectType.DATAFLOW_SIDE_EFFECTING


def _split_start(name, copies_of, srcs, lands, after):
    n = len(srcs)

    def body(*refs):
        outs = refs[2 * n + 1:]
        for k in range(n):
            for cp in copies_of(refs[k], refs[n + k], outs[k], outs[n + k]):
                cp.start()
        outs[-1][...] = jnp.zeros_like(outs[-1])

    hbm = lambda a: pltpu.HBM(a.shape, a.dtype)
    out = pl.pallas_call(
        body, name=name,
        in_specs=[HBM_SPEC] * (2 * n + 1),
        out_specs=[SEM_SPEC] * (2 * n) + [HBM_SPEC] * (2 * n) + [pl.BlockSpec(memory_space=pltpu.VMEM)],
        out_shape=[pltpu.SemaphoreType.DMA(())] * (2 * n) + [hbm(a) for a in srcs] + [hbm(a) for a in lands]
        + [jax.ShapeDtypeStruct((8, LANES), F32)],
        input_output_aliases={k: 2 * n + k for k in range(2 * n)},
        compiler_params=pltpu.CompilerParams(has_side_effects=DATAFLOW),
    )(*[pltpu.with_memory_space_constraint(a, pltpu.HBM) for a in list(srcs) + list(lands) + [after]])
    return (out[:n], out[n:2 * n], out[2 * n:3 * n], out[3 * n:4 * n]), out[-1][0, 0]


def _split_wait(name, n_copies, started, after):
    send_sems, recv_sems, srcs, lands = started
    n = len(srcs)

    def body(*refs):
        for k in range(n):
            slots = refs[n + k].at[pl.ds(0, n_copies)]
            all_copies = pltpu.make_async_remote_copy(
                src_ref=slots, dst_ref=slots, send_sem=refs[2 * n + k], recv_sem=refs[3 * n + k],
                device_id=_place()[0], device_id_type=MESH)
            all_copies.wait_send()
            all_copies.wait_recv()

    hbm = lambda a: pltpu.HBM(a.shape, a.dtype)
    out = pl.pallas_call(
        body, name=name,
        in_specs=[HBM_SPEC] * (2 * n) + [SEM_SPEC] * (2 * n) + [HBM_SPEC],
        out_specs=[HBM_SPEC] * (2 * n),
        out_shape=[hbm(a) for a in srcs] + [hbm(a) for a in lands],
        input_output_aliases={k: k for k in range(2 * n)},
        compiler_params=pltpu.CompilerParams(has_side_effects=DATAFLOW),
    )(*srcs, *lands, *send_sems, *recv_sems, pltpu.with_memory_space_constraint(after, pltpu.HBM))
    return out[:n], out[n:]


def _slot(chip, core):
    return 4 * chip[0] + 2 * chip[1] + core


def _gather_copies(src, land, send_sem, recv_sem):
    (x, y, c), sib, chips = _place()
    return [pltpu.make_async_remote_copy(src_ref=src, dst_ref=land.at[_slot((x, y), c)], send_sem=send_sem,
                                         recv_sem=recv_sem, device_id=to, device_id_type=MESH)
            for to in [sib] + [(*chip, c) for chip in chips]]


def _gather_pass_on(lands, name):
    n = len(lands)

    def body(*refs):
        ins, outs = refs[:n], refs[n:2 * n]
        send_sems, recv_sems = refs[2 * n:]
        (x, y, c), sib, chips = _place()
        sends = []
        for k in range(n):
            for j, chip in enumerate(chips):
                sends.append(pltpu.make_async_remote_copy(
                    src_ref=ins[k].at[_slot(chip, c)], dst_ref=outs[k].at[_slot(chip, c)],
                    send_sem=send_sems.at[k, j], recv_sem=recv_sems.at[k, j], device_id=sib, device_id_type=MESH))
        for cp in sends:
            cp.start()
        for cp in sends:
            cp.wait_recv()
        for cp in sends:
            cp.wait_send()

    any_spec = pl.BlockSpec(memory_space=pl.ANY)
    return pl.pallas_call(
        body, name=name,
        in_specs=[any_spec] * n, out_specs=[any_spec] * n,
        out_shape=[jax.ShapeDtypeStruct(a.shape, a.dtype) for a in lands],
        input_output_aliases={k: k for k in range(n)},
        scratch_shapes=[pltpu.SemaphoreType.DMA((n, 3))] * 2,
        compiler_params=pltpu.CompilerParams(has_side_effects=True),
    )(*lands)


def _chip_copies(src, land, send_sem, recv_sem):
    (x, y, c), _, chips = _place()
    return [pltpu.make_async_remote_copy(src_ref=src.at[2 * chip[0] + chip[1]], dst_ref=land.at[j], send_sem=send_sem,
                                         recv_sem=recv_sem, device_id=(*chip, c), device_id_type=MESH)
            for j, chip in enumerate(chips)]


ROW_TILE_BYTES = 14 * 1024 * 1024


def _row_tile(r, bytes_per_row):
    fits = [t for t in range(16, r, 16) if r % t == 0 and t * bytes_per_row <= ROW_TILE_BYTES]
    return r if r * bytes_per_row <= ROW_TILE_BYTES or not fits else max(fits)


def _sibling_copies(src, land, send_sem, recv_sem):
    (x, y, c), sib, _ = _place()
    return [pltpu.make_async_remote_copy(src_ref=src.at[2 * q + 1 - c], dst_ref=land.at[q], send_sem=send_sem,
                                         recv_sem=recv_sem, device_id=sib, device_id_type=MESH)
            for q in range(4)]


def _pair_sum(core, blocks, theirs, name):
    q, r, c = theirs.shape
    tr = _row_tile(r, 3 * c * theirs.dtype.itemsize)

    def body(core_ref, a_ref, b_ref, o_ref):
        o_ref[...] = (a_ref[...].astype(F32) + b_ref[...].astype(F32)).astype(o_ref.dtype)

    blk = pl.BlockSpec((1, tr, c), lambda i, j, core_ref: (i, j, 0))
    mine = pl.BlockSpec((1, tr, c), lambda i, j, core_ref: (2 * i + core_ref[0], j, 0))
    return pl.pallas_call(
        body, name=name,
        grid_spec=pltpu.PrefetchScalarGridSpec(num_scalar_prefetch=1, grid=(q, r // tr),
                                               in_specs=[mine, blk], out_specs=blk),
        out_shape=jax.ShapeDtypeStruct(theirs.shape, theirs.dtype),
        compiler_params=_params(("parallel", "parallel")),
    )(core, blocks, theirs)


def _reduce_adamw(parts, w, m, v, name, own=None, own_slot=None):
    r, c = w.shape
    n_parts = parts.shape[0]
    tr = _row_tile(r, c * ((n_parts + 1) * parts.dtype.itemsize + 7 * 4))
    c1 = 1.0 - ADAM_B1
    c2 = 1.0 - ADAM_B2
    bc1 = 1.0 - ADAM_B1 ** ADAM_STEP
    bc2 = 1.0 - ADAM_B2 ** ADAM_STEP

    def body(*refs):
        if own is None:
            p_ref, w_ref, m_ref, v_ref, g_ref, d_ref, nm_ref, nv_ref = refs
            g = p_ref[0].astype(F32)
            first = 1
        else:
            _, o_ref, p_ref, w_ref, m_ref, v_ref, g_ref, d_ref, nm_ref, nv_ref = refs
            g = o_ref[0].astype(F32)
            first = 0
        for j in range(first, n_parts):
            g = g + p_ref[j].astype(F32)
        nm = ADAM_B1 * m_ref[...] + c1 * g
        nv = ADAM_B2 * v_ref[...] + c2 * (g * g)
        g_ref[...] = g
        nm_ref[...] = nm
        nv_ref[...] = nv
        d_ref[...] = -ADAM_LR * ((nm / bc1) / (jnp.sqrt(nv / bc2) + ADAM_EPS) + ADAM_WD * w_ref[...])

    out = jax.ShapeDtypeStruct((r, c), F32)
    if own is None:
        blk = pl.BlockSpec((tr, c), lambda i: (i, 0))
        return pl.pallas_call(
            body, name=name, grid=(r // tr,),
            in_specs=[pl.BlockSpec((n_parts, tr, c), lambda i: (0, i, 0)), blk, blk, blk],
            out_specs=[blk] * 4, out_shape=[out] * 4,
            compiler_params=_params(("parallel",)),
        )(parts, w, m, v)
    blk = pl.BlockSpec((tr, c), lambda i, slot_ref: (i, 0))
    return pl.pallas_call(
        body, name=name,
        grid_spec=pltpu.PrefetchScalarGridSpec(
            num_scalar_prefetch=1, grid=(r // tr,),
            in_specs=[pl.BlockSpec((1, tr, c), lambda i, slot_ref: (slot_ref[0], i, 0)),
                      pl.BlockSpec((n_parts, tr, c), lambda i, slot_ref: (0, i, 0)), blk, blk, blk],
            out_specs=[blk] * 4),
        out_shape=[out] * 4,
        compiler_params=_params(("parallel",)),
    )(own_slot, own, parts, w, m, v)


_MIX = ("w_in", "w_uq", "w_ukv", "conv_w", "w_out")
_FFN = ("w_gate", "w_up", "w_down")
_BIG = _MIX + _FFN
_SMALL = ("pre_mix_norm", "q_norm", "kv_norm", "conv_b", "conv_ln_g", "conv_ln_b", "conv_out_norm",
          "attn_out_norm", "post_mix_norm", "pre_ffn_norm", "post_ffn_norm")
_ORDER = ("pre_mix_norm", "w_in", "q_norm", "w_uq", "kv_norm", "w_ukv", "conv_w", "conv_b", "conv_ln_g",
          "conv_ln_b", "conv_out_norm", "attn_out_norm", "w_out", "post_mix_norm", "pre_ffn_norm", "w_gate",
          "w_up", "w_down", "post_ffn_norm")


def _cols_from_shards(g):
    return jnp.transpose(g, (1, 0, 2)).reshape(g.shape[1], N_DEV * g.shape[2])


def _cols_to_shards(w):
    k, n8 = w.shape
    return jnp.transpose(w.reshape(k, N_DEV, n8 // N_DEV), (1, 0, 2))


def _step(x, positions, loss_target, w, m, v):
    s, d = x.shape[1], x.shape[2]
    x2, tgt = x[0], loss_target[0]
    pos = positions.reshape(s, 1)
    vecs = {n: w[n] for n in _SMALL}
    core = lax.axis_index("c").astype(jnp.int32).reshape(1)
    my_chip = (2 * lax.axis_index("x") + lax.axis_index("y")).astype(jnp.int32).reshape(1)
    n_in_cols = N_DEV * w["w_in"].shape[2]
    started, to_sibling = {}, []

    def gather_start(names, tag, after, zero=0.0):
        srcs = [w[n][0] if n == "conv_w" else (w[n][0] + zero).astype(BF16) for n in names]
        lands = [lax.empty((N_DEV,) + a.shape, a.dtype) for a in srcs]
        started[tag], zero = _split_start("gather_" + tag + "_start", _gather_copies, srcs, lands, after)
        return zero

    def gather_finish(names, tag, after):
        srcs, lands = _split_wait("gather_" + tag + "_wait", 4, started[tag], after)
        lands = _gather_pass_on(lands, "gather_" + tag + "_pass_on")
        me = _my_index()
        return {n: lax.dynamic_update_slice(g, a[None], (me,) + (0,) * a.ndim) for n, g, a in zip(names, lands, srcs)}

    def mix_weights_fn(h):
        gath = gather_finish(_MIX, "mix", h)
        zero = gather_start(("w_gate", "w_up"), "up", gath["w_in"])
        zero = gather_start(("w_down",), "down", gath["w_in"], zero)
        w_in_p = jnp.pad(_cols_from_shards(gath["w_in"]), ((0, 0), (0, LANES - QK_ROPE)))
        w_uq_p = _cols_from_shards(jnp.pad(gath["w_uq"], ((0, 0), (0, 0), (0, HEAD_PAD - QK_HEAD))))
        return (w_in_p, w_uq_p, _cols_from_shards(gath["w_ukv"]), _cols_from_shards(gath["conv_w"]),
                gath["w_out"].reshape(-1, d), zero)

    def up_weights_fn(mix):
        gath = gather_finish(("w_gate", "w_up"), "up", mix)
        return gath["w_gate"], gath["w_up"]

    def down_weights_fn(act):
        return gather_finish(("w_down",), "down", act)["w_down"]

    def scatter_advance(after):
        if not to_sibling:
            return 0.0
        names, tag, sent = to_sibling.pop()
        blocks, theirs = _split_wait("to_sibling_" + tag + "_wait", 4, sent, after)
        pairs = [_pair_sum(core, b, t, "pair_sum_" + n) for n, b, t in zip(names, blocks, theirs)]
        lands = [lax.empty((3,) + p.shape[1:], p.dtype) for p in pairs]
        started[tag], zero = _split_start("scatter_" + tag + "_start", _chip_copies, pairs, lands, theirs[0])
        return zero

    def grads_fn(tag, names, blocks, after):
        zero = scatter_advance(after)
        lands = [lax.empty((4,) + b.shape[1:], b.dtype) for b in blocks]
        sent, zero2 = _split_start("to_sibling_" + tag + "_start", _sibling_copies, blocks, lands, after)
        to_sibling.append((names, tag, sent))
        return zero + zero2

    vecs["pre_mix_norm"] = vecs["pre_mix_norm"] + gather_start(_MIX, "mix", x2)
    loss, grad_x, g = _local_step(x2, pos, tgt, vecs, mix_weights_fn, up_weights_fn, down_weights_fn, grads_fn)

    last = ("w_in", "w_uq", "w_ukv", "conv_w")
    last_blocks = [_cols_to_shards(g["w_in_p"][:, :n_in_cols]), _cols_to_shards(g["w_uq_p"])[:, :, :QK_HEAD],
                   _cols_to_shards(g["w_ukv"]), _cols_to_shards(g["conv_w"])]
    zero = grads_fn("in", last, last_blocks, grad_x) + scatter_advance(grad_x)
    small = jnp.concatenate([g[n] for n in _SMALL], axis=1) + zero
    small_all = _gather_direct([small], "gather_small_grads")[0]

    res = {}
    after = small_all
    for tag, names in (("down", ("w_down",)), ("up", ("w_gate", "w_up")), ("out", ("w_out",)), ("in", last)):
        pairs, recv = _split_wait("scatter_" + tag + "_wait", 3, started[tag], after)
        for n, own, parts in zip(names, pairs, recv):
            res[n] = _reduce_adamw(parts, w[n][0], m[n][0], v[n][0], "adamw_" + n, own=own, own_slot=my_chip)
            after = res[n][1]
            res[n] = [t[None] for t in res[n]]
    cat = lambda t: jnp.concatenate([t[n] for n in _SMALL], axis=1)
    sg, sd, sm, sv = _reduce_adamw(small_all, cat(w), cat(m), cat(v), "adamw_small")
    off = 0
    for n in _SMALL:
        width = w[n].shape[1]
        res[n] = [t[:, off:off + width] for t in (sg, sd, sm, sv)]
        off += width

    total = lax.psum(loss[0, 0], ("x", "y", "c"))
    outs = [total, grad_x[None]]
    for part in range(4):
        outs.extend(res[n][part] for n in _ORDER)
    return tuple(outs)


def kernel(x, positions, pre_mix_norm, w_in, q_norm, w_uq, kv_norm, w_ukv, conv_w, conv_b, conv_ln_g, conv_ln_b, conv_out_norm, attn_out_norm, w_out, post_mix_norm, pre_ffn_norm, w_gate, w_up, w_down, post_ffn_norm, loss_target, m_pre_mix_norm, m_w_in, m_q_norm, m_w_uq, m_kv_norm, m_w_ukv, m_conv_w, m_conv_b, m_conv_ln_g, m_conv_ln_b, m_conv_out_norm, m_attn_out_norm, m_w_out, m_post_mix_norm, m_pre_ffn_norm, m_w_gate, m_w_up, m_w_down, m_post_ffn_norm, v_pre_mix_norm, v_w_in, v_q_norm, v_w_uq, v_kv_norm, v_w_ukv, v_conv_w, v_conv_b, v_conv_ln_g, v_conv_ln_b, v_conv_out_norm, v_attn_out_norm, v_w_out, v_post_mix_norm, v_pre_ffn_norm, v_w_gate, v_w_up, v_w_down, v_post_ffn_norm):
    w = dict(zip(_ORDER, (pre_mix_norm, w_in, q_norm, w_uq, kv_norm, w_ukv, conv_w, conv_b, conv_ln_g, conv_ln_b,
                          conv_out_norm, attn_out_norm, w_out, post_mix_norm, pre_ffn_norm, w_gate, w_up, w_down,
                          post_ffn_norm)))
    m = dict(zip(_ORDER, (m_pre_mix_norm, m_w_in, m_q_norm, m_w_uq, m_kv_norm, m_w_ukv, m_conv_w, m_conv_b,
                          m_conv_ln_g, m_conv_ln_b, m_conv_out_norm, m_attn_out_norm, m_w_out, m_post_mix_norm,
                          m_pre_ffn_norm, m_w_gate, m_w_up, m_w_down, m_post_ffn_norm)))
    v = dict(zip(_ORDER, (v_pre_mix_norm, v_w_in, v_q_norm, v_w_uq, v_kv_norm, v_w_ukv, v_conv_w, v_conv_b,
                          v_conv_ln_g, v_conv_ln_b, v_conv_out_norm, v_attn_out_norm, v_w_out, v_post_mix_norm,
                          v_pre_ffn_norm, v_w_gate, v_w_up, v_w_down, v_post_ffn_norm)))
    return _step(x, positions, loss_target, w, m, v)
```

```python
import functools

import jax
import jax.numpy as jnp
from jax import lax
from jax.experimental import pallas as pl
from jax.experimental.pallas import tpu as pltpu

N_DEV = 8
N_HEADS = 8
QK_NOPE = 128
QK_ROPE = 64
V_HEAD = 128
QK_HEAD = QK_NOPE + QK_ROPE
HEAD_PAD = 256
LANES = 128
ATTN_BLOCK = 512
CONV_K = 31
CONV_PAD = 32
EPS = 1e-6
ROPE_THETA = 10000.0
ADAM_LR = 0.001
ADAM_B1 = 0.9
ADAM_B2 = 0.999
ADAM_EPS = 1e-08
ADAM_WD = 0.01
ADAM_STEP = 10
VMEM_LIMIT = 56 * 1024 * 1024
F32 = jnp.float32
BF16 = jnp.bfloat16
MESH = pl.DeviceIdType.MESH
NEG = -1e30


def _pick(n, prefs):
    for p in prefs:
        if p <= n and n % p == 0:
            return p
    return n


def _params(sem):
    return pltpu.CompilerParams(dimension_semantics=sem, vmem_limit_bytes=VMEM_LIMIT)


_DIMS = {"nn": (((1,), (0,)), ((), ())), "nt": (((1,), (1,)), ((), ())), "tn": (((0,), (0,)), ((), ()))}


MM_VMEM_BUDGET = 40 * 1024 * 1024
MM_MAX_MACS = 3 * 1024 ** 3


V7X_HBM_BYTES_PER_S = 3.0e12
V7X_MXU_MACS_PER_S = 0.45e15
GRID_STEP_S = 0.35e-6


def _mm_tiles(m, n, k, size_a, size_b, size_o):
    best = None
    for tm in sorted({m, 1024, 512, 256, 128}, reverse=True):
        if tm > m or m % tm:
            continue
        for tn in sorted({n, 2048, 1024, 512, 384, 256, 128}, reverse=True):
            if tn > n or n % tn:
                continue
            vmem = 2 * (tm * k * size_a + k * tn * size_b + tm * tn * size_o)
            if vmem > MM_VMEM_BUDGET or tm * tn * k > MM_MAX_MACS:
                continue
            b_reads = 1 if tn == n else m // tm
            traffic = m * k * size_a + b_reads * k * n * size_b + m * n * size_o
            exposed = tm * k * size_a + k * tn * size_b + tm * tn * size_o
            steps = (m // tm) * (n // tn)
            key = (max(traffic / V7X_HBM_BYTES_PER_S, m * n * k / V7X_MXU_MACS_PER_S)
                   + exposed / V7X_HBM_BYTES_PER_S + steps * GRID_STEP_S)
            if best is None or key < best[0]:
                best = (key, tm, tn)
    assert best is not None, (m, n, k)
    return best[1], best[2]


def _mm(a, b, mode, name, out_dtype=F32, add=None):
    if mode == "nn":
        (m, k), (k2, n) = a.shape, b.shape
    elif mode == "nt":
        (m, k), (n, k2) = a.shape, b.shape
    else:
        (k, m), (k2, n) = a.shape, b.shape
    assert k == k2, (a.shape, b.shape, mode)
    tm, tn = _mm_tiles(m, n, k, a.dtype.itemsize, b.dtype.itemsize,
                       jnp.dtype(out_dtype).itemsize + (0 if add is None else 4))
    dims = _DIMS[mode]

    def body(a_ref, b_ref, *rest):
        acc = lax.dot_general(a_ref[...].astype(BF16), b_ref[...].astype(BF16), dims, preferred_element_type=F32)
        if add is not None:
            acc = acc + rest[0][...]
        rest[-1][...] = acc.astype(rest[-1].dtype)

    if mode == "tn":
        a_spec = pl.BlockSpec((k, tm), lambda i, j: (0, i))
    else:
        a_spec = pl.BlockSpec((tm, k), lambda i, j: (i, 0))
    if mode == "nt":
        b_spec = pl.BlockSpec((tn, k), lambda i, j: (j, 0))
    else:
        b_spec = pl.BlockSpec((k, tn), lambda i, j: (0, j))
    o_spec = pl.BlockSpec((tm, tn), lambda i, j: (i, j))
    extra = [] if add is None else [add]
    return pl.pallas_call(
        body, name=name,
        grid=(m // tm, n // tn),
        in_specs=[a_spec, b_spec] + [o_spec] * len(extra),
        out_specs=o_spec,
        out_shape=jax.ShapeDtypeStruct((m, n), out_dtype),
        compiler_params=_params(("parallel", "parallel")),
    )(a, b, *extra)


def _sigmoid(x):
    return 1.0 / (1.0 + jnp.exp(-x))


def _rms(x, g):
    r = lax.rsqrt(jnp.mean(x * x, axis=-1, keepdims=True) + EPS)
    return (x * r) * g


def _rms_bwd(x, g, dy):
    r = lax.rsqrt(jnp.mean(x * x, axis=-1, keepdims=True) + EPS)
    xh = x * r
    dyg = dy * g
    dx = r * (dyg - xh * jnp.mean(dyg * xh, axis=-1, keepdims=True))
    return dx, jnp.sum(dy * xh, axis=0, keepdims=True)


def _ln(x, g, b):
    mu = jnp.mean(x, axis=-1, keepdims=True)
    xc = x - mu
    rs = lax.rsqrt(jnp.mean(xc * xc, axis=-1, keepdims=True) + EPS)
    return (xc * rs) * g + b


def _ln_bwd(x, g, dy):
    mu = jnp.mean(x, axis=-1, keepdims=True)
    xc = x - mu
    rs = lax.rsqrt(jnp.mean(xc * xc, axis=-1, keepdims=True) + EPS)
    xh = xc * rs
    dyg = dy * g
    dx = rs * (dyg - jnp.mean(dyg, axis=-1, keepdims=True) - xh * jnp.mean(dyg * xh, axis=-1, keepdims=True))
    return dx, jnp.sum(dy * xh, axis=0, keepdims=True), jnp.sum(dy, axis=0, keepdims=True)


def _silu(x):
    return x * _sigmoid(x)


def _silu_grad(x):
    s = _sigmoid(x)
    return s * (1.0 + x * (1.0 - s))


def _rope(x, cos, sa, sb):
    return x * cos + pltpu.roll(x, 96, 1) * sa + pltpu.roll(x, 32, 1) * sb


def _rope_t(d, cos, sa, sb):
    return d * cos - pltpu.roll(d, 96, 1) * sa - pltpu.roll(d, 32, 1) * sb


def _rows(ts, w):
    return pl.BlockSpec((ts, w), lambda i: (i, 0))


def _vec(w):
    return pl.BlockSpec((1, w), lambda i: (0, 0))


def _acc_init(i, *refs):
    @pl.when(i == 0)
    def _():
        for r in refs:
            r[...] = jnp.zeros_like(r)


def _rope_tables(pos, inv_freq):
    s = pos.shape[0]
    ts = _pick(s, (512, 256, 128))

    def body(p_ref, f_ref, c_ref, sa_ref, sb_ref):
        ang = p_ref[...].astype(F32) * f_ref[...]
        lane = lax.broadcasted_iota(jnp.int32, ang.shape, 1)
        c, sn = jnp.cos(ang), jnp.sin(ang)
        c_ref[...] = jnp.where(lane < QK_ROPE, c, 0.0)
        sa_ref[...] = jnp.where(lane < QK_ROPE // 2, -sn, 0.0)
        sb_ref[...] = jnp.where((lane >= QK_ROPE // 2) & (lane < QK_ROPE), sn, 0.0)

    out = jax.ShapeDtypeStruct((s, LANES), F32)
    return pl.pallas_call(
        body, name="rope_tables", grid=(s // ts,),
        in_specs=[_rows(ts, 1), _vec(LANES)],
        out_specs=[_rows(ts, LANES)] * 3, out_shape=[out] * 3,
        compiler_params=_params(("parallel",)),
    )(pos, inv_freq)


def _pre_fwd(x, g):
    s, d = x.shape
    ts = _pick(s, (256, 128))

    def body(x_ref, g_ref, h_ref):
        h_ref[...] = _rms(x_ref[...], g_ref[...]).astype(BF16)

    return pl.pallas_call(
        body, name="pre_fwd", grid=(s // ts,),
        in_specs=[_rows(ts, d), _vec(d)], out_specs=_rows(ts, d),
        out_shape=jax.ShapeDtypeStruct((s, d), BF16),
        compiler_params=_params(("parallel",)),
    )(x, g)


def _split_fwd(z, gq, gkv, tabs, c, ql, kvl):
    s, zw = z.shape
    ts = _pick(s, (256, 128))
    o_q, o_kv, o_kr = 2 * c, 2 * c + ql, 2 * c + ql + kvl

    def body(z_ref, gq_ref, gkv_ref, c_ref, sa_ref, sb_ref, u0_ref, qn_ref, kvn_ref, kpe_ref):
        u0_ref[...] = z_ref[:, 0:c] * _sigmoid(z_ref[:, c:2 * c])
        qn_ref[...] = _rms(z_ref[:, o_q:o_kv], gq_ref[...]).astype(BF16)
        kvn_ref[...] = _rms(z_ref[:, o_kv:o_kr], gkv_ref[...]).astype(BF16)
        kpe_ref[...] = _rope(z_ref[:, o_kr:o_kr + LANES], c_ref[...], sa_ref[...], sb_ref[...]).astype(BF16)

    return pl.pallas_call(
        body, name="split_fwd", grid=(s // ts,),
        in_specs=[_rows(ts, zw), _vec(ql), _vec(kvl)] + [_rows(ts, LANES)] * 3,
        out_specs=[_rows(ts, c), _rows(ts, ql), _rows(ts, kvl), _rows(ts, LANES)],
        out_shape=[jax.ShapeDtypeStruct((s, c), F32), jax.ShapeDtypeStruct((s, ql), BF16),
                   jax.ShapeDtypeStruct((s, kvl), BF16), jax.ShapeDtypeStruct((s, LANES), BF16)],
        compiler_params=_params(("parallel",)),
    )(z, gq, gkv, *tabs)


def _split_bwd(du0, z, dqn, dkvn, dkpe_h, gq, gkv, tabs, c, ql, kvl):
    s, zw = z.shape
    ts = _pick(s, (256, 128))
    o_q, o_kv, o_kr = 2 * c, 2 * c + ql, 2 * c + ql + kvl

    def body(du0_ref, z_ref, dqn_ref, dkvn_ref, dkh_ref, gq_ref, gkv_ref, c_ref, sa_ref, sb_ref,
             dz_ref, dgq_ref, dgkv_ref):
        _acc_init(pl.program_id(0), dgq_ref, dgkv_ref)
        du0 = du0_ref[...]
        a = z_ref[:, 0:c]
        sg = _sigmoid(z_ref[:, c:2 * c])
        dz_ref[:, 0:c] = (du0 * sg).astype(BF16)
        dz_ref[:, c:2 * c] = (du0 * a * sg * (1.0 - sg)).astype(BF16)
        dq, dgq = _rms_bwd(z_ref[:, o_q:o_kv], gq_ref[...], dqn_ref[...])
        dz_ref[:, o_q:o_kv] = dq.astype(BF16)
        dgq_ref[...] += dgq
        dkv, dgkv = _rms_bwd(z_ref[:, o_kv:o_kr], gkv_ref[...], dkvn_ref[...])
        dz_ref[:, o_kv:o_kr] = dkv.astype(BF16)
        dgkv_ref[...] += dgkv
        dk = dkh_ref[:, 0:LANES]
        for h in range(1, N_HEADS):
            dk = dk + dkh_ref[:, h * LANES:(h + 1) * LANES]
        dz_ref[:, o_kr:o_kr + LANES] = _rope_t(dk, c_ref[...], sa_ref[...], sb_ref[...]).astype(BF16)

    return pl.pallas_call(
        body, name="split_bwd", grid=(s // ts,),
        in_specs=[_rows(ts, c), _rows(ts, zw), _rows(ts, ql), _rows(ts, kvl), _rows(ts, N_HEADS * LANES),
                  _vec(ql), _vec(kvl)] + [_rows(ts, LANES)] * 3,
        out_specs=[_rows(ts, zw), _vec(ql), _vec(kvl)],
        out_shape=[jax.ShapeDtypeStruct((s, zw), BF16), jax.ShapeDtypeStruct((1, ql), F32),
                   jax.ShapeDtypeStruct((1, kvl), F32)],
        compiler_params=_params(("arbitrary",)),
    )(du0, z, dqn, dkvn, dkpe_h, gq, gkv, *tabs)


def _q_rope(qpre, tabs, transpose, out_dtype, name):
    s, w = qpre.shape
    ts = _pick(s, (256, 128))
    rot = _rope_t if transpose else _rope

    def body(q_ref, c_ref, sa_ref, sb_ref, o_ref):
        cs, sa, sb = c_ref[...], sa_ref[...], sb_ref[...]
        for h in range(N_HEADS):
            lo = h * HEAD_PAD
            o_ref[:, lo:lo + QK_NOPE] = q_ref[:, lo:lo + QK_NOPE].astype(out_dtype)
            o_ref[:, lo + QK_NOPE:lo + HEAD_PAD] = rot(q_ref[:, lo + QK_NOPE:lo + HEAD_PAD], cs, sa, sb).astype(out_dtype)

    return pl.pallas_call(
        body, name=name, grid=(s // ts,),
        in_specs=[_rows(ts, w)] + [_rows(ts, LANES)] * 3, out_specs=_rows(ts, w),
        out_shape=jax.ShapeDtypeStruct((s, w), out_dtype),
        compiler_params=_params(("parallel",)),
    )(qpre, *tabs)


def _conv_fwd(u0, w, b):
    s, c = u0.shape
    tc = LANES
    rc = _pick(s, (256, 128))

    def body(u_ref, w_ref, b_ref, o_ref, pad_ref):
        pad_ref[0:CONV_PAD, :] = jnp.zeros((CONV_PAD, tc), F32)
        pad_ref[CONV_PAD:CONV_PAD + s, :] = u_ref[...]
        for r in range(s // rc):
            acc = jnp.broadcast_to(b_ref[...], (rc, tc))
            for k in range(CONV_K):
                lo = r * rc + CONV_PAD - (CONV_K - 1) + k
                acc = acc + w_ref[k:k + 1, :] * pad_ref[lo:lo + rc, :]
            o_ref[r * rc:(r + 1) * rc, :] = acc

    col = lambda j: (0, j)
    return pl.pallas_call(
        body, name="conv_fwd", grid=(c // tc,),
        in_specs=[pl.BlockSpec((s, tc), col), pl.BlockSpec((CONV_K, tc), col), pl.BlockSpec((1, tc), col)],
        out_specs=pl.BlockSpec((s, tc), col),
        out_shape=jax.ShapeDtypeStruct((s, c), F32),
        scratch_shapes=[pltpu.VMEM((s + CONV_PAD, tc), F32)],
        compiler_params=_params(("parallel",)),
    )(u0, w, b)


def _conv_bwd(du1, u0, w):
    s, c = u0.shape
    tc = LANES
    rc = _pick(s, (256, 128))

    def body(d_ref, u_ref, w_ref, du_ref, dw_ref, db_ref, upad_ref, dpad_ref):
        upad_ref[0:CONV_PAD, :] = jnp.zeros((CONV_PAD, tc), F32)
        upad_ref[CONV_PAD:CONV_PAD + s, :] = u_ref[...]
        dpad_ref[0:s, :] = d_ref[...]
        dpad_ref[s:s + CONV_PAD, :] = jnp.zeros((CONV_PAD, tc), F32)
        for r in range(s // rc):
            acc = jnp.zeros((rc, tc), F32)
            for k in range(CONV_K):
                lo = r * rc + (CONV_K - 1) - k
                acc = acc + w_ref[k:k + 1, :] * dpad_ref[lo:lo + rc, :]
            du_ref[r * rc:(r + 1) * rc, :] = acc
        for k in range(CONV_K):
            acc8 = jnp.zeros((8, tc), F32)
            for r in range(s // rc):
                lo = r * rc + CONV_PAD - (CONV_K - 1) + k
                prod = d_ref[r * rc:(r + 1) * rc, :] * upad_ref[lo:lo + rc, :]
                acc8 = acc8 + jnp.sum(prod.reshape(rc // 8, 8, tc), axis=0)
            dw_ref[k:k + 1, :] = jnp.sum(acc8, axis=0, keepdims=True)
        db_ref[...] = jnp.sum(d_ref[...], axis=0, keepdims=True)

    col = lambda j: (0, j)
    return pl.pallas_call(
        body, name="conv_bwd", grid=(c // tc,),
        in_specs=[pl.BlockSpec((s, tc), col), pl.BlockSpec((s, tc), col), pl.BlockSpec((CONV_K, tc), col)],
        out_specs=[pl.BlockSpec((s, tc), col), pl.BlockSpec((CONV_K, tc), col), pl.BlockSpec((1, tc), col)],
        out_shape=[jax.ShapeDtypeStruct((s, c), F32), jax.ShapeDtypeStruct((CONV_K, c), F32),
                   jax.ShapeDtypeStruct((1, c), F32)],
        scratch_shapes=[pltpu.VMEM((s + CONV_PAD, tc), F32), pltpu.VMEM((s + CONV_PAD, tc), F32)],
        compiler_params=_params(("parallel",)),
    )(du1, u0, w)


def _causal_mask(sc, qi, kj, tq, tk):
    rows = qi * tq + lax.broadcasted_iota(jnp.int32, sc.shape, 0)
    cols = kj * tk + lax.broadcasted_iota(jnp.int32, sc.shape, 1)
    return jnp.where(cols <= rows, sc, NEG)


def _attn_fwd(q, kv, kpe):
    s = q.shape[0]
    tq = tk = _pick(s, (ATTN_BLOCK, 256, 128))
    reps = tk // LANES
    scale = QK_HEAD ** -0.5
    nt = (((1,), (1,)), ((), ()))

    def body(q_ref, kn_ref, v_ref, kpe_ref, o_ref, lse_ref, kf_ref, vb_ref, m_ref, l_ref, acc_ref):
        i = pl.program_id(1)

        @pl.when(i == 0)
        def _():
            kf_ref[:, 0:QK_NOPE] = kn_ref[...].astype(BF16)
            kf_ref[:, QK_NOPE:HEAD_PAD] = kpe_ref[...]
            vb_ref[...] = v_ref[...].astype(BF16)

        qb = q_ref[...]
        m_ref[...] = jnp.full((tq, LANES), NEG, F32)
        l_ref[...] = jnp.zeros((tq, LANES), F32)
        acc_ref[...] = jnp.zeros((tq, V_HEAD), F32)

        def block(j, diagonal):
            off = pl.multiple_of(j * tk, tk)
            sc = lax.dot_general(qb, kf_ref[pl.ds(off, tk), :], nt, preferred_element_type=F32) * scale
            if diagonal:
                sc = _causal_mask(sc, 0, 0, tq, tk)
            m_prev = m_ref[...]
            m_new = jnp.maximum(m_prev, jnp.max(sc, axis=1, keepdims=True))
            p = jnp.exp(sc - jnp.tile(m_new, (1, reps)))
            alpha = jnp.exp(m_prev - m_new)
            l_ref[...] = alpha * l_ref[...] + jnp.sum(p, axis=1, keepdims=True)
            acc_ref[...] = alpha * acc_ref[...] + jnp.dot(p.astype(BF16), vb_ref[pl.ds(off, tk), :],
                                                          preferred_element_type=F32)
            m_ref[...] = m_new

        def below_diagonal(j, carry):
            block(j, False)
            return carry

        lax.fori_loop(0, i, below_diagonal, 0)
        block(i, True)
        o_ref[...] = acc_ref[...] / l_ref[...]
        lse_ref[...] = m_ref[...] + jnp.log(l_ref[...])

    return pl.pallas_call(
        body, name="attn_fwd", grid=(N_HEADS, s // tq),
        in_specs=[pl.BlockSpec((tq, HEAD_PAD), lambda h, i: (i, h)),
                  pl.BlockSpec((s, QK_NOPE), lambda h, i: (0, 2 * h)),
                  pl.BlockSpec((s, V_HEAD), lambda h, i: (0, 2 * h + 1)),
                  pl.BlockSpec((s, LANES), lambda h, i: (0, 0))],
        out_specs=[pl.BlockSpec((tq, V_HEAD), lambda h, i: (i, h)),
                   pl.BlockSpec((tq, LANES), lambda h, i: (i, h))],
        out_shape=[jax.ShapeDtypeStruct((s, N_HEADS * V_HEAD), F32),
                   jax.ShapeDtypeStruct((s, N_HEADS * LANES), F32)],
        scratch_shapes=[pltpu.VMEM((s, HEAD_PAD), BF16), pltpu.VMEM((s, V_HEAD), BF16),
                        pltpu.VMEM((tq, LANES), F32), pltpu.VMEM((tq, LANES), F32), pltpu.VMEM((tq, V_HEAD), F32)],
        compiler_params=_params(("parallel", "arbitrary")),
    )(q, kv, kv, kpe)


def _attn_bwd(q, kv, kpe, o, do, lse):
    s = q.shape[0]
    tq = tk = _pick(s, (ATTN_BLOCK, 256, 128))
    nq = s // tq
    reps = tk // LANES
    scale = QK_HEAD ** -0.5
    nt = (((1,), (1,)), ((), ()))
    tn = (((0,), (0,)), ((), ()))

    def body(q_ref, kn_ref, v_ref, kpe_ref, o_ref, do_ref, lse_ref, dq_ref, dkv_ref, dkpe_ref,
             kf_ref, vb_ref, dk_ref, dv_ref):
        j = pl.program_id(1)

        @pl.when(j == 0)
        def _():
            dq_ref[...] = jnp.zeros_like(dq_ref)

        kf_ref[:, 0:QK_NOPE] = kn_ref[...].astype(BF16)
        kf_ref[:, QK_NOPE:HEAD_PAD] = kpe_ref[...]
        vb_ref[...] = v_ref[...].astype(BF16)
        dk_ref[...] = jnp.zeros_like(dk_ref)
        dv_ref[...] = jnp.zeros_like(dv_ref)

        def block(i, diagonal):
            off = pl.multiple_of(i * tq, tq)
            qb = q_ref[pl.ds(off, tq), :]
            dob = do_ref[pl.ds(off, tq), :]
            delta = jnp.sum(dob * o_ref[pl.ds(off, tq), :], axis=1, keepdims=True)
            sc = lax.dot_general(qb, kf_ref[...], nt, preferred_element_type=F32) * scale
            if diagonal:
                sc = _causal_mask(sc, 0, 0, tq, tk)
            p = jnp.exp(sc - jnp.tile(lse_ref[pl.ds(off, tq), :], (1, reps)))
            dob16 = dob.astype(BF16)
            dv_ref[...] += lax.dot_general(p.astype(BF16), dob16, tn, preferred_element_type=F32)
            dp = lax.dot_general(dob16, vb_ref[...], nt, preferred_element_type=F32)
            ds = (p * (dp - delta) * scale).astype(BF16)
            dq_ref[pl.ds(off, tq), :] += jnp.dot(ds, kf_ref[...], preferred_element_type=F32)
            dk_ref[...] += lax.dot_general(ds, qb, tn, preferred_element_type=F32)

        def above_diagonal(i, carry):
            block(i, False)
            return carry

        block(j, True)
        lax.fori_loop(j + 1, nq, above_diagonal, 0)
        dkv_ref[:, 0:QK_NOPE] = dk_ref[:, 0:QK_NOPE]
        dkv_ref[:, QK_NOPE:HEAD_PAD] = dv_ref[...]
        dkpe_ref[...] = dk_ref[:, QK_NOPE:HEAD_PAD]

    head_rows = lambda w: pl.BlockSpec((s, w), lambda h, j: (0, h))
    return pl.pallas_call(
        body, name="attn_bwd", grid=(N_HEADS, s // tk),
        in_specs=[head_rows(HEAD_PAD),
                  pl.BlockSpec((tk, QK_NOPE), lambda h, j: (j, 2 * h)),
                  pl.BlockSpec((tk, V_HEAD), lambda h, j: (j, 2 * h + 1)),
                  pl.BlockSpec((tk, LANES), lambda h, j: (j, 0)),
                  head_rows(V_HEAD), head_rows(V_HEAD), head_rows(LANES)],
        out_specs=[head_rows(HEAD_PAD),
                   pl.BlockSpec((tk, HEAD_PAD), lambda h, j: (j, h)),
                   pl.BlockSpec((tk, LANES), lambda h, j: (j, h))],
        out_shape=[jax.ShapeDtypeStruct((s, N_HEADS * HEAD_PAD), F32),
                   jax.ShapeDtypeStruct((s, N_HEADS * HEAD_PAD), F32),
                   jax.ShapeDtypeStruct((s, N_HEADS * LANES), F32)],
        scratch_shapes=[pltpu.VMEM((tk, HEAD_PAD), BF16), pltpu.VMEM((tk, V_HEAD), BF16),
                        pltpu.VMEM((tk, HEAD_PAD), F32), pltpu.VMEM((tk, V_HEAD), F32)],
        compiler_params=_params(("parallel", "arbitrary")),
    )(q, kv, kv, kpe, o, do, lse)


def _mix_fwd(u1, lng, lnb, gcon, attn, gattn):
    s, c = u1.shape
    ac = attn.shape[1]
    ts = _pick(s, (256, 128))

    def body(u_ref, lg_ref, lb_ref, gc_ref, a_ref, ga_ref, o_ref):
        t3 = _silu(_ln(u_ref[...], lg_ref[...], lb_ref[...]))
        o_ref[:, 0:c] = _rms(t3, gc_ref[...]).astype(BF16)
        o_ref[:, c:c + ac] = _rms(a_ref[...], ga_ref[...]).astype(BF16)

    return pl.pallas_call(
        body, name="mix_fwd", grid=(s // ts,),
        in_specs=[_rows(ts, c), _vec(c), _vec(c), _vec(c), _rows(ts, ac), _vec(ac)],
        out_specs=_rows(ts, c + ac),
        out_shape=jax.ShapeDtypeStruct((s, c + ac), BF16),
        compiler_params=_params(("parallel",)),
    )(u1, lng, lnb, gcon, attn, gattn)


def _mix_bwd(dmixin, u1, lng, lnb, gcon, attn, gattn):
    s, c = u1.shape
    ac = attn.shape[1]
    ts = _pick(s, (256, 128))

    def body(d_ref, u_ref, lg_ref, lb_ref, gc_ref, a_ref, ga_ref,
             du_ref, da_ref, dlg_ref, dlb_ref, dgc_ref, dga_ref):
        _acc_init(pl.program_id(0), dlg_ref, dlb_ref, dgc_ref, dga_ref)
        u = u_ref[...]
        t2 = _ln(u, lg_ref[...], lb_ref[...])
        dt3, dgc = _rms_bwd(_silu(t2), gc_ref[...], d_ref[:, 0:c])
        du, dlg, dlb = _ln_bwd(u, lg_ref[...], dt3 * _silu_grad(t2))
        du_ref[...] = du
        dlg_ref[...] += dlg
        dlb_ref[...] += dlb
        dgc_ref[...] += dgc
        da, dga = _rms_bwd(a_ref[...], ga_ref[...], d_ref[:, c:c + ac])
        da_ref[...] = da
        dga_ref[...] += dga

    return pl.pallas_call(
        body, name="mix_bwd", grid=(s // ts,),
        in_specs=[_rows(ts, c + ac), _rows(ts, c), _vec(c), _vec(c), _vec(c), _rows(ts, ac), _vec(ac)],
        out_specs=[_rows(ts, c), _rows(ts, ac), _vec(c), _vec(c), _vec(c), _vec(ac)],
        out_shape=[jax.ShapeDtypeStruct((s, c), F32), jax.ShapeDtypeStruct((s, ac), F32),
                   jax.ShapeDtypeStruct((1, c), F32), jax.ShapeDtypeStruct((1, c), F32),
                   jax.ShapeDtypeStruct((1, c), F32), jax.ShapeDtypeStruct((1, ac), F32)],
        compiler_params=_params(("arbitrary",)),
    )(dmixin, u1, lng, lnb, gcon, attn, gattn)


def _post_mix_fwd(x, mix, gpost, gpre):
    s, d = x.shape
    ts = _pick(s, (256, 128))

    def body(x_ref, m_ref, gp_ref, gf_ref, x1_ref, hf_ref):
        x1 = x_ref[...] + _rms(m_ref[...], gp_ref[...])
        x1_ref[...] = x1
        hf_ref[...] = _rms(x1, gf_ref[...]).astype(BF16)

    return pl.pallas_call(
        body, name="post_mix_fwd", grid=(s // ts,),
        in_specs=[_rows(ts, d), _rows(ts, d), _vec(d), _vec(d)],
        out_specs=[_rows(ts, d), _rows(ts, d)],
        out_shape=[jax.ShapeDtypeStruct((s, d), F32), jax.ShapeDtypeStruct((s, d), BF16)],
        compiler_params=_params(("parallel",)),
    )(x, mix, gpost, gpre)


def _post_mix_bwd(dy, dhf, x1, gpre, mix, gpost):
    s, d = x1.shape
    ts = _pick(s, (256, 128))

    def body(dy_ref, dh_ref, x1_ref, gf_ref, m_ref, gp_ref, dx1_ref, dm_ref, dgf_ref, dgp_ref):
        _acc_init(pl.program_id(0), dgf_ref, dgp_ref)
        dxa, dgf = _rms_bwd(x1_ref[...], gf_ref[...], dh_ref[...])
        dx1 = dy_ref[...] + dxa
        dx1_ref[...] = dx1
        dgf_ref[...] += dgf
        dm, dgp = _rms_bwd(m_ref[...], gp_ref[...], dx1)
        dm_ref[...] = dm.astype(BF16)
        dgp_ref[...] += dgp

    return pl.pallas_call(
        body, name="post_mix_bwd", grid=(s // ts,),
        in_specs=[_rows(ts, d), _rows(ts, d), _rows(ts, d), _vec(d), _rows(ts, d), _vec(d)],
        out_specs=[_rows(ts, d), _rows(ts, d), _vec(d), _vec(d)],
        out_shape=[jax.ShapeDtypeStruct((s, d), F32), jax.ShapeDtypeStruct((s, d), BF16),
                   jax.ShapeDtypeStruct((1, d), F32), jax.ShapeDtypeStruct((1, d), F32)],
        compiler_params=_params(("arbitrary",)),
    )(dy, dhf, x1, gpre, mix, gpost)


def _ffn_up(hf, wg, wu):
    s, d = hf.shape
    nsh, _, fs = wg.shape
    tm = _pick(s, (1024, 512, 256, 128))

    def body(h_ref, wg_ref, wu_ref, g_ref, u_ref, a_ref):
        h = h_ref[...]
        g = jnp.dot(h, wg_ref[...], preferred_element_type=F32)
        u = jnp.dot(h, wu_ref[...], preferred_element_type=F32)
        g_ref[...] = g
        u_ref[...] = u
        a_ref[...] = (_silu(g) * u).astype(BF16)

    w_spec = pl.BlockSpec((None, d, fs), lambda i, j: (j, 0, 0))
    o_spec = pl.BlockSpec((None, tm, fs), lambda i, j: (j, i, 0))
    return pl.pallas_call(
        body, name="ffn_up", grid=(s // tm, nsh),
        in_specs=[pl.BlockSpec((tm, d), lambda i, j: (i, 0)), w_spec, w_spec],
        out_specs=[o_spec] * 3,
        out_shape=[jax.ShapeDtypeStruct((nsh, s, fs), F32)] * 2 + [jax.ShapeDtypeStruct((nsh, s, fs), BF16)],
        compiler_params=_params(("parallel", "parallel")),
    )(hf, wg, wu)


def _ffn_down(act, wd):
    nsh, s, fs = act.shape
    d = wd.shape[2]
    tm = _pick(s, (1024, 512, 256, 128))
    tn = _pick(d, (512, 256, 128))

    def body(a_ref, w_ref, o_ref):
        acc = jnp.dot(a_ref[0], w_ref[0], preferred_element_type=F32)
        for j in range(1, nsh):
            acc = acc + jnp.dot(a_ref[j], w_ref[j], preferred_element_type=F32)
        o_ref[...] = acc

    return pl.pallas_call(
        body, name="ffn_down", grid=(s // tm, d // tn),
        in_specs=[pl.BlockSpec((nsh, tm, fs), lambda i, j: (0, i, 0)),
                  pl.BlockSpec((nsh, fs, tn), lambda i, j: (0, 0, j))],
        out_specs=pl.BlockSpec((tm, tn), lambda i, j: (i, j)),
        out_shape=jax.ShapeDtypeStruct((s, d), F32),
        compiler_params=_params(("parallel", "parallel")),
    )(act, wd)


def _ffn_down_bwd(dff, wd, gate, up, behind):
    s, d = dff.shape
    nsh, fs, _ = wd.shape
    tm = _pick(s, (1024, 512, 256, 128))
    nt = (((1,), (1,)), ((), ()))

    def body(d_ref, w_ref, g_ref, u_ref, _, dg_ref, du_ref):
        dact = lax.dot_general(d_ref[...], w_ref[...], nt, preferred_element_type=F32)
        g = g_ref[...]
        dg_ref[...] = (dact * u_ref[...] * _silu_grad(g)).astype(BF16)
        du_ref[...] = (dact * _silu(g)).astype(BF16)

    h_spec = pl.BlockSpec((None, tm, fs), lambda i, j: (j, i, 0))
    return pl.pallas_call(
        body, name="ffn_down_bwd", grid=(s // tm, nsh),
        in_specs=[pl.BlockSpec((tm, d), lambda i, j: (i, 0)),
                  pl.BlockSpec((None, fs, d), lambda i, j: (j, 0, 0)), h_spec, h_spec,
                  pl.BlockSpec((8, LANES), lambda i, j: (0, 0))],
        out_specs=[h_spec] * 2,
        out_shape=[jax.ShapeDtypeStruct((nsh, s, fs), BF16)] * 2,
        compiler_params=_params(("parallel", "parallel")),
    )(dff, wd, gate, up, behind)


def _ffn_dw_down(act, dff):
    nsh, s, fs = act.shape
    d = dff.shape[1]
    tn = (((0,), (0,)), ((), ()))

    def body(a_ref, d_ref, o_ref):
        o_ref[...] = lax.dot_general(a_ref[...], d_ref[...], tn, preferred_element_type=F32).astype(BF16)

    return pl.pallas_call(
        body, name="ffn_dw_down", grid=(nsh,),
        in_specs=[pl.BlockSpec((None, s, fs), lambda j: (j, 0, 0)), pl.BlockSpec((s, d), lambda j: (0, 0))],
        out_specs=pl.BlockSpec((None, fs, d), lambda j: (j, 0, 0)),
        out_shape=jax.ShapeDtypeStruct((nsh, fs, d), BF16),
        compiler_params=_params(("parallel",)),
    )(act, dff)


def _ffn_dw_up(hf, dh, name):
    s, d = hf.shape
    nsh, _, fs = dh.shape
    tn = (((0,), (0,)), ((), ()))

    def body(h_ref, d_ref, o_ref):
        o_ref[...] = lax.dot_general(h_ref[...], d_ref[...], tn, preferred_element_type=F32).astype(BF16)

    return pl.pallas_call(
        body, name=name, grid=(nsh,),
        in_specs=[pl.BlockSpec((s, d), lambda j: (0, 0)), pl.BlockSpec((None, s, fs), lambda j: (j, 0, 0))],
        out_specs=pl.BlockSpec((None, d, fs), lambda j: (j, 0, 0)),
        out_shape=jax.ShapeDtypeStruct((nsh, d, fs), BF16),
        compiler_params=_params(("parallel",)),
    )(hf, dh)


def _ffn_up_bwd(dh, w, name, add=None):
    nsh, s, fs = dh.shape
    d = w.shape[1]
    tm = _pick(s, (1024, 512, 256, 128))
    tn = _pick(d, (256, 128))
    nt = (((1,), (1,)), ((), ()))

    def body(a_ref, w_ref, *rest):
        acc = lax.dot_general(a_ref[0], w_ref[0], nt, preferred_element_type=F32)
        for j in range(1, nsh):
            acc = acc + lax.dot_general(a_ref[j], w_ref[j], nt, preferred_element_type=F32)
        if add is not None:
            acc = acc + rest[0][...]
        rest[-1][...] = acc

    o_spec = pl.BlockSpec((tm, tn), lambda i, j: (i, j))
    extra = [] if add is None else [add]
    return pl.pallas_call(
        body, name=name, grid=(s // tm, d // tn),
        in_specs=[pl.BlockSpec((nsh, tm, fs), lambda i, j: (0, i, 0)),
                  pl.BlockSpec((nsh, tn, fs), lambda i, j: (0, j, 0))] + [o_spec] * len(extra),
        out_specs=o_spec,
        out_shape=jax.ShapeDtypeStruct((s, d), F32),
        compiler_params=_params(("parallel", "parallel")),
    )(dh, w, *extra)


def _final(ff, x1, tgt, g):
    s, d = x1.shape
    ts = _pick(s, (256, 128))

    def body(ff_ref, x1_ref, t_ref, g_ref, loss_ref, dy_ref, dff_ref, dg_ref):
        _acc_init(pl.program_id(0), loss_ref, dg_ref)
        ff_v = ff_ref[...]
        err = x1_ref[...] + _rms(ff_v, g_ref[...]) - t_ref[...]
        tok = jnp.mean(err * err, axis=-1, keepdims=True)
        loss_ref[...] += 0.5 * jnp.sum(tok, axis=0, keepdims=True)
        dy = err * (1.0 / d)
        dy_ref[...] = dy
        dff, dg = _rms_bwd(ff_v, g_ref[...], dy)
        dff_ref[...] = dff.astype(BF16)
        dg_ref[...] += dg

    return pl.pallas_call(
        body, name="final", grid=(s // ts,),
        in_specs=[_rows(ts, d), _rows(ts, d), _rows(ts, d), _vec(d)],
        out_specs=[_vec(LANES), _rows(ts, d), _rows(ts, d), _vec(d)],
        out_shape=[jax.ShapeDtypeStruct((1, LANES), F32), jax.ShapeDtypeStruct((s, d), F32),
                   jax.ShapeDtypeStruct((s, d), BF16), jax.ShapeDtypeStruct((1, d), F32)],
        compiler_params=_params(("arbitrary",)),
    )(ff, x1, tgt, g)


def _pre_bwd(dx1, dh, x, g):
    s, d = x.shape
    ts = _pick(s, (256, 128))

    def body(dx1_ref, dh_ref, x_ref, g_ref, dx_ref, dg_ref):
        _acc_init(pl.program_id(0), dg_ref)
        dxa, dg = _rms_bwd(x_ref[...], g_ref[...], dh_ref[...])
        dx_ref[...] = dx1_ref[...] + dxa
        dg_ref[...] += dg

    return pl.pallas_call(
        body, name="pre_bwd", grid=(s // ts,),
        in_specs=[_rows(ts, d), _rows(ts, d), _rows(ts, d), _vec(d)],
        out_specs=[_rows(ts, d), _vec(d)],
        out_shape=[jax.ShapeDtypeStruct((s, d), F32), jax.ShapeDtypeStruct((1, d), F32)],
        compiler_params=_params(("arbitrary",)),
    )(dx1, dh, x, g)


def _local_step(x, pos, tgt, vecs, in_weights_fn, mix_weights_fn, up_weights_fn, down_weights_fn, grads_fn):
    c = vecs["conv_b"].shape[1]
    ql = vecs["q_norm"].shape[1]
    kvl = vecs["kv_norm"].shape[1]
    half = jnp.arange(0, QK_ROPE, 2, dtype=F32)
    freq = ROPE_THETA ** (-half / QK_ROPE)
    inv_freq = jnp.concatenate([freq, freq, jnp.zeros((LANES - QK_ROPE,), F32)])[None, :]
    tabs = _rope_tables(pos, inv_freq)

    h = _pre_fwd(x, vecs["pre_mix_norm"])
    w_in_p, zero = in_weights_fn(h)
    z = _mm(h, w_in_p, "nn", "mm_z")
    w_uq_p, w_ukv, conv_w, w_out = mix_weights_fn(z)
    u0, qn, kvn, kpe = _split_fwd(z, vecs["q_norm"] + zero, vecs["kv_norm"], tabs, c, ql, kvl)
    u1 = _conv_fwd(u0, conv_w, vecs["conv_b"])
    q = _q_rope(_mm(qn, w_uq_p, "nn", "mm_q"), tabs, False, BF16, "q_rope")
    kv = _mm(kvn, w_ukv, "nn", "mm_kv")
    attn, lse = _attn_fwd(q, kv, kpe)
    mixin = _mix_fwd(u1, vecs["conv_ln_g"], vecs["conv_ln_b"], vecs["conv_out_norm"], attn, vecs["attn_out_norm"])
    mix = _mm(mixin, w_out, "nn", "mm_mix")
    x1, hf = _post_mix_fwd(x, mix, vecs["post_mix_norm"], vecs["pre_ffn_norm"])
    w_gate, w_up = up_weights_fn(mix)
    gate, up, act = _ffn_up(hf, w_gate, w_up)
    w_down = down_weights_fn(act)
    ff = _ffn_down(act, w_down)
    loss, dy, dff, d_post_ffn = _final(ff, x1, tgt, vecs["post_ffn_norm"])

    g = {"post_ffn_norm": d_post_ffn}
    zero = grads_fn("down", ("w_down",), [_ffn_dw_down(act, dff)], dff)
    dgate, dup = _ffn_down_bwd(dff, w_down, gate, up, jnp.zeros((8, LANES), F32) + zero)
    dw_gate = _ffn_dw_up(hf, dgate, "ffn_dw_gate")
    dw_up = _ffn_dw_up(hf, dup, "ffn_dw_up")
    zero = zero + grads_fn("up", ("w_gate", "w_up"), [dw_gate, dw_up], dgate)
    dhf = _ffn_up_bwd(dup, w_up, "ffn_dhf_up", add=_ffn_up_bwd(dgate, w_gate, "ffn_dhf_gate"))
    dx1, dmix, g["pre_ffn_norm"], g["post_mix_norm"] = _post_mix_bwd(
        dy, dhf, x1, vecs["pre_ffn_norm"] + zero, mix, vecs["post_mix_norm"])
    dmixin = _mm(dmix, w_out, "nt", "mm_dmixin")
    dw_out = _mm(mixin, dmix, "tn", "mm_dw_out", BF16)
    zero = grads_fn("out", ("w_out",), [dw_out.reshape(N_DEV, -1, dw_out.shape[1])], dmix)
    du1, dattn, g["conv_ln_g"], g["conv_ln_b"], g["conv_out_norm"], g["attn_out_norm"] = _mix_bwd(
        dmixin, u1, vecs["conv_ln_g"] + zero, vecs["conv_ln_b"], vecs["conv_out_norm"], attn, vecs["attn_out_norm"])
    du0, g["conv_w"], g["conv_b"] = _conv_bwd(du1, u0, conv_w)
    dq, dkv, dkpe_h = _attn_bwd(q, kv, kpe, attn, dattn, lse)
    dqpre = _q_rope(dq, tabs, True, BF16, "q_rope_bwd")
    dqn = _mm(dqpre, w_uq_p, "nt", "mm_dqn")
    g["w_uq_p"] = _mm(qn, dqpre, "tn", "mm_dw_uq", BF16)
    dkvn = _mm(dkv, w_ukv, "nt", "mm_dkvn")
    g["w_ukv"] = _mm(kvn, dkv, "tn", "mm_dw_ukv", BF16)
    dz, g["q_norm"], g["kv_norm"] = _split_bwd(du0, z, dqn, dkvn, dkpe_h, vecs["q_norm"], vecs["kv_norm"], tabs, c, ql, kvl)
    dh = _mm(dz, w_in_p, "nt", "mm_dh")
    g["w_in_p"] = _mm(h, dz, "tn", "mm_dw_in", BF16)
    grad_x, g["pre_mix_norm"] = _pre_bwd(dx1, dh, x, vecs["pre_mix_norm"])
    return loss, grad_x, g


def _my_index():
    return 4 * lax.axis_index("x") + 2 * lax.axis_index("y") + lax.axis_index("c")


def _coords(idx):
    return ((idx >> 2) & 1, (idx >> 1) & 1, idx & 1)


def _place():
    x, y, c = lax.axis_index("x"), lax.axis_index("y"), lax.axis_index("c")
    return (x, y, c), (x, y, 1 - c), [(1 - x, y), (x, 1 - y), (1 - x, 1 - y)]


def _comm_call(body, name, arrays, out_shapes, n_sems):
    n = len(arrays)
    any_spec = pl.BlockSpec(memory_space=pl.ANY)
    return pl.pallas_call(
        body, name=name,
        in_specs=[any_spec] * n, out_specs=[any_spec] * len(out_shapes), out_shape=out_shapes,
        scratch_shapes=[pltpu.SemaphoreType.DMA((n, n_sems))] * 3,
        compiler_params=pltpu.CompilerParams(has_side_effects=True),
    )(*arrays)


def _gather_direct(arrays, name):
    n = len(arrays)

    def body(*refs):
        ins, outs = refs[:n], refs[n:2 * n]
        send_sems, recv_sems, local_sems = refs[2 * n:]
        me = _my_index()
        local = [pltpu.make_async_copy(ins[k], outs[k].at[me], local_sems.at[k, 0]) for k in range(n)]
        for cp in local:
            cp.start()

        def copy(k, p, slot):
            return pltpu.make_async_remote_copy(
                src_ref=ins[k], dst_ref=outs[k].at[slot], send_sem=send_sems.at[k, p - 1],
                recv_sem=recv_sems.at[k, p - 1], device_id=_coords(me ^ p), device_id_type=MESH)

        sends = [copy(k, p, me) for p in range(1, N_DEV) for k in range(n)]
        for cp in sends:
            cp.start()
        for p in range(1, N_DEV):
            for k in range(n):
                copy(k, p, me ^ p).wait_recv()
        for cp in sends:
            cp.wait_send()
        for cp in local:
            cp.wait()

    out_shapes = [jax.ShapeDtypeStruct((N_DEV,) + a.shape, a.dtype) for a in arrays]
    return _comm_call(body, name, arrays, out_shapes, N_DEV - 1)


HBM_SPEC = pl.BlockSpec(memory_space=pltpu.HBM)
SEM_SPEC = pl.BlockSpec(memory_space=pltpu.SEMAPHORE)
DATAFLOW = pltpu.SideEffectType.DATAFLOW_SIDE_EFFECTING


def _split_start(name, copies_of, srcs, lands, after):
    n = len(srcs)

    def body(*refs):
        outs = refs[2 * n + 1:]
        for k in range(n):
            for cp in copies_of(refs[k], refs[n + k], outs[k], outs[n + k]):
                cp.start()
        outs[-1][...] = jnp.zeros_like(outs[-1])

    hbm = lambda a: pltpu.HBM(a.shape, a.dtype)
    out = pl.pallas_call(
        body, name=name,
        in_specs=[HBM_SPEC] * (2 * n + 1),
        out_specs=[SEM_SPEC] * (2 * n) + [HBM_SPEC] * (2 * n) + [pl.BlockSpec(memory_space=pltpu.VMEM)],
        out_shape=[pltpu.SemaphoreType.DMA(())] * (2 * n) + [hbm(a) for a in srcs] + [hbm(a) for a in lands]
        + [jax.ShapeDtypeStruct((8, LANES), F32)],
        input_output_aliases={k: 2 * n + k for k in range(2 * n)},
        compiler_params=pltpu.CompilerParams(has_side_effects=DATAFLOW),
    )(*[pltpu.with_memory_space_constraint(a, pltpu.HBM) for a in list(srcs) + list(lands) + [after]])
    return (out[:n], out[n:2 * n], out[2 * n:3 * n], out[3 * n:4 * n]), out[-1][0, 0]


def _split_wait(name, n_copies, started, after):
    send_sems, recv_sems, srcs, lands = started
    n = len(srcs)

    def body(*refs):
        for k in range(n):
            slots = refs[n + k].at[pl.ds(0, n_copies)]
            all_copies = pltpu.make_async_remote_copy(
                src_ref=slots, dst_ref=slots, send_sem=refs[2 * n + k], recv_sem=refs[3 * n + k],
                device_id=_place()[0], device_id_type=MESH)
            all_copies.wait_send()
            all_copies.wait_recv()

    hbm = lambda a: pltpu.HBM(a.shape, a.dtype)
    out = pl.pallas_call(
        body, name=name,
        in_specs=[HBM_SPEC] * (2 * n) + [SEM_SPEC] * (2 * n) + [HBM_SPEC],
        out_specs=[HBM_SPEC] * (2 * n),
        out_shape=[hbm(a) for a in srcs] + [hbm(a) for a in lands],
        input_output_aliases={k: k for k in range(2 * n)},
        compiler_params=pltpu.CompilerParams(has_side_effects=DATAFLOW),
    )(*srcs, *lands, *send_sems, *recv_sems, pltpu.with_memory_space_constraint(after, pltpu.HBM))
    return out[:n], out[n:]


def _slot(chip, core):
    return 4 * chip[0] + 2 * chip[1] + core


def _gather_copies(src, land, send_sem, recv_sem):
    (x, y, c), sib, chips = _place()
    return [pltpu.make_async_remote_copy(src_ref=src, dst_ref=land.at[_slot((x, y), c)], send_sem=send_sem,
                                         recv_sem=recv_sem, device_id=to, device_id_type=MESH)
            for to in [sib] + [(*chip, c) for chip in chips]]


def _gather_pass_on(lands, name):
    n = len(lands)

    def body(*refs):
        ins, outs = refs[:n], refs[n:2 * n]
        send_sems, recv_sems = refs[2 * n:]
        (x, y, c), sib, chips = _place()
        sends = []
        for k in range(n):
            for j, chip in enumerate(chips):
                sends.append(pltpu.make_async_remote_copy(
                    src_ref=ins[k].at[_slot(chip, c)], dst_ref=outs[k].at[_slot(chip, c)],
                    send_sem=send_sems.at[k, j], recv_sem=recv_sems.at[k, j], device_id=sib, device_id_type=MESH))
        for cp in sends:
            cp.start()
        for cp in sends:
            cp.wait_recv()
        for cp in sends:
            cp.wait_send()

    any_spec = pl.BlockSpec(memory_space=pl.ANY)
    return pl.pallas_call(
        body, name=name,
        in_specs=[any_spec] * n, out_specs=[any_spec] * n,
        out_shape=[jax.ShapeDtypeStruct(a.shape, a.dtype) for a in lands],
        input_output_aliases={k: k for k in range(n)},
        scratch_shapes=[pltpu.SemaphoreType.DMA((n, 3))] * 2,
        compiler_params=pltpu.CompilerParams(has_side_effects=True),
    )(*lands)


def _chip_copies(src, land, send_sem, recv_sem):
    (x, y, c), _, chips = _place()
    return [pltpu.make_async_remote_copy(src_ref=src.at[2 * chip[0] + chip[1]], dst_ref=land.at[j], send_sem=send_sem,
                                         recv_sem=recv_sem, device_id=(*chip, c), device_id_type=MESH)
            for j, chip in enumerate(chips)]


ROW_TILE_BYTES = 14 * 1024 * 1024


def _row_tile(r, bytes_per_row):
    fits = [t for t in range(16, r, 16) if r % t == 0 and t * bytes_per_row <= ROW_TILE_BYTES]
    return r if r * bytes_per_row <= ROW_TILE_BYTES or not fits else max(fits)


def _sibling_copies(src, land, send_sem, recv_sem):
    (x, y, c), sib, _ = _place()
    return [pltpu.make_async_remote_copy(src_ref=src.at[2 * q + 1 - c], dst_ref=land.at[q], send_sem=send_sem,
                                         recv_sem=recv_sem, device_id=sib, device_id_type=MESH)
            for q in range(4)]


def _pair_sum(core, blocks, theirs, name):
    q, r, c = theirs.shape
    tr = _row_tile(r, 3 * c * theirs.dtype.itemsize)

    def body(core_ref, a_ref, b_ref, o_ref):
        o_ref[...] = (a_ref[...].astype(F32) + b_ref[...].astype(F32)).astype(o_ref.dtype)

    blk = pl.BlockSpec((1, tr, c), lambda i, j, core_ref: (i, j, 0))
    mine = pl.BlockSpec((1, tr, c), lambda i, j, core_ref: (2 * i + core_ref[0], j, 0))
    return pl.pallas_call(
        body, name=name,
        grid_spec=pltpu.PrefetchScalarGridSpec(num_scalar_prefetch=1, grid=(q, r // tr),
                                               in_specs=[mine, blk], out_specs=blk),
        out_shape=jax.ShapeDtypeStruct(theirs.shape, theirs.dtype),
        compiler_params=_params(("parallel", "parallel")),
    )(core, blocks, theirs)


def _reduce_adamw(parts, w, m, v, name, own=None, own_slot=None):
    r, c = w.shape
    n_parts = parts.shape[0]
    tr = _row_tile(r, c * ((n_parts + 1) * parts.dtype.itemsize + 7 * 4))
    c1 = 1.0 - ADAM_B1
    c2 = 1.0 - ADAM_B2
    bc1 = 1.0 - ADAM_B1 ** ADAM_STEP
    bc2 = 1.0 - ADAM_B2 ** ADAM_STEP

    def body(*refs):
        if own is None:
            p_ref, w_ref, m_ref, v_ref, g_ref, d_ref, nm_ref, nv_ref = refs
            g = p_ref[0].astype(F32)
            first = 1
        else:
            _, o_ref, p_ref, w_ref, m_ref, v_ref, g_ref, d_ref, nm_ref, nv_ref = refs
            g = o_ref[0].astype(F32)
            first = 0
        for j in range(first, n_parts):
            g = g + p_ref[j].astype(F32)
        nm = ADAM_B1 * m_ref[...] + c1 * g
        nv = ADAM_B2 * v_ref[...] + c2 * (g * g)
        g_ref[...] = g
        nm_ref[...] = nm
        nv_ref[...] = nv
        d_ref[...] = -ADAM_LR * ((nm / bc1) / (jnp.sqrt(nv / bc2) + ADAM_EPS) + ADAM_WD * w_ref[...])

    out = jax.ShapeDtypeStruct((r, c), F32)
    if own is None:
        blk = pl.BlockSpec((tr, c), lambda i: (i, 0))
        return pl.pallas_call(
            body, name=name, grid=(r // tr,),
            in_specs=[pl.BlockSpec((n_parts, tr, c), lambda i: (0, i, 0)), blk, blk, blk],
            out_specs=[blk] * 4, out_shape=[out] * 4,
            compiler_params=_params(("parallel",)),
        )(parts, w, m, v)
    blk = pl.BlockSpec((tr, c), lambda i, slot_ref: (i, 0))
    return pl.pallas_call(
        body, name=name,
        grid_spec=pltpu.PrefetchScalarGridSpec(
            num_scalar_prefetch=1, grid=(r // tr,),
            in_specs=[pl.BlockSpec((1, tr, c), lambda i, slot_ref: (slot_ref[0], i, 0)),
                      pl.BlockSpec((n_parts, tr, c), lambda i, slot_ref: (0, i, 0)), blk, blk, blk],
            out_specs=[blk] * 4),
        out_shape=[out] * 4,
        compiler_params=_params(("parallel",)),
    )(own_slot, own, parts, w, m, v)


_MIX = ("w_in", "w_uq", "w_ukv", "conv_w", "w_out")
_FFN = ("w_gate", "w_up", "w_down")
_BIG = _MIX + _FFN
_SMALL = ("pre_mix_norm", "q_norm", "kv_norm", "conv_b", "conv_ln_g", "conv_ln_b", "conv_out_norm",
          "attn_out_norm", "post_mix_norm", "pre_ffn_norm", "post_ffn_norm")
_ORDER = ("pre_mix_norm", "w_in", "q_norm", "w_uq", "kv_norm", "w_ukv", "conv_w", "conv_b", "conv_ln_g",
          "conv_ln_b", "conv_out_norm", "attn_out_norm", "w_out", "post_mix_norm", "pre_ffn_norm", "w_gate",
          "w_up", "w_down", "post_ffn_norm")


def _cols_from_shards(g):
    return jnp.transpose(g, (1, 0, 2)).reshape(g.shape[1], N_DEV * g.shape[2])


def _cols_to_shards(w):
    k, n8 = w.shape
    return jnp.transpose(w.reshape(k, N_DEV, n8 // N_DEV), (1, 0, 2))


def _step(x, positions, loss_target, w, m, v):
    s, d = x.shape[1], x.shape[2]
    x2, tgt = x[0], loss_target[0]
    pos = positions.reshape(s, 1)
    vecs = {n: w[n] for n in _SMALL}
    core = lax.axis_index("c").astype(jnp.int32).reshape(1)
    my_chip = (2 * lax.axis_index("x") + lax.axis_index("y")).astype(jnp.int32).reshape(1)
    n_in_cols = N_DEV * w["w_in"].shape[2]
    gathers, scatters, to_sibling = {}, {}, []

    def gather_start(names, tag, after, zero=0.0):
        srcs = [w[n][0] if n == "conv_w" else (w[n][0] + zero).astype(BF16) for n in names]
        lands = [lax.empty((N_DEV,) + a.shape, a.dtype) for a in srcs]
        gathers[tag], zero = _split_start("gather_" + tag + "_start", _gather_copies, srcs, lands, after)
        return zero

    def gather_finish(names, tag, after):
        srcs, lands = _split_wait("gather_" + tag + "_wait", 4, gathers[tag], after)
        lands = _gather_pass_on(lands, "gather_" + tag + "_pass_on")
        me = _my_index()
        return {n: lax.dynamic_update_slice(g, a[None], (me,) + (0,) * a.ndim) for n, g, a in zip(names, lands, srcs)}

    def in_weights_fn(h):
        w_in_g = gather_finish(("w_in",), "in", h)["w_in"]
        zero = gather_start(("w_gate", "w_up"), "up", w_in_g)
        zero = gather_start(("w_down",), "down", w_in_g, zero)
        return jnp.pad(_cols_from_shards(w_in_g), ((0, 0), (0, LANES - QK_ROPE))), zero

    def mix_weights_fn(z):
        gath = gather_finish(_MIX[1:], "mix", z)
        w_uq_p = _cols_from_shards(jnp.pad(gath["w_uq"], ((0, 0), (0, 0), (0, HEAD_PAD - QK_HEAD))))
        return (w_uq_p, _cols_from_shards(gath["w_ukv"]), _cols_from_shards(gath["conv_w"]),
                gath["w_out"].reshape(-1, d))

    def up_weights_fn(mix):
        gath = gather_finish(("w_gate", "w_up"), "up", mix)
        return gath["w_gate"], gath["w_up"]

    def down_weights_fn(act):
        return gather_finish(("w_down",), "down", act)["w_down"]

    def scatter_advance(after):
        if not to_sibling:
            return 0.0
        names, tag, sent = to_sibling.pop()
        blocks, theirs = _split_wait("to_sibling_" + tag + "_wait", 4, sent, after)
        pairs = [_pair_sum(core, b, t, "pair_sum_" + n) for n, b, t in zip(names, blocks, theirs)]
        lands = [lax.empty((3,) + p.shape[1:], p.dtype) for p in pairs]
        scatters[tag], zero = _split_start("scatter_" + tag + "_start", _chip_copies, pairs, lands, theirs[0])
        return zero

    def grads_fn(tag, names, blocks, after):
        zero = scatter_advance(after)
        lands = [lax.empty((4,) + b.shape[1:], b.dtype) for b in blocks]
        sent, zero2 = _split_start("to_sibling_" + tag + "_start", _sibling_copies, blocks, lands, after)
        to_sibling.append((names, tag, sent))
        return zero + zero2

    zero = gather_start(("w_in",), "in", x2)
    vecs["pre_mix_norm"] = vecs["pre_mix_norm"] + gather_start(_MIX[1:], "mix", x2, zero)
    loss, grad_x, g = _local_step(x2, pos, tgt, vecs, in_weights_fn, mix_weights_fn, up_weights_fn,
                                  down_weights_fn, grads_fn)

    last = ("w_in", "w_uq", "w_ukv", "conv_w")
    last_blocks = [_cols_to_shards(g["w_in_p"][:, :n_in_cols]), _cols_to_shards(g["w_uq_p"])[:, :, :QK_HEAD],
                   _cols_to_shards(g["w_ukv"]), _cols_to_shards(g["conv_w"])]
    small_all = _gather_direct([jnp.concatenate([g[n] for n in _SMALL], axis=1)], "gather_small_grads")[0]
    zero = grads_fn("in", last, last_blocks, small_all) + scatter_advance(grad_x)
    small_all = small_all + zero

    res = {}
    after = small_all
    for tag, names in (("down", ("w_down",)), ("up", ("w_gate", "w_up")), ("out", ("w_out",)), ("in", last)):
        pairs, recv = _split_wait("scatter_" + tag + "_wait", 3, scatters[tag], after)
        for n, own, parts in zip(names, pairs, recv):
            res[n] = _reduce_adamw(parts, w[n][0], m[n][0], v[n][0], "adamw_" + n, own=own, own_slot=my_chip)
            after = res[n][1]
            res[n] = [t[None] for t in res[n]]
    cat = lambda t: jnp.concatenate([t[n] for n in _SMALL], axis=1)
    sg, sd, sm, sv = _reduce_adamw(small_all, cat(w), cat(m), cat(v), "adamw_small")
    off = 0
    for n in _SMALL:
        width = w[n].shape[1]
        res[n] = [t[:, off:off + width] for t in (sg, sd, sm, sv)]
        off += width

    total = lax.psum(loss[0, 0], ("x", "y", "c"))
    outs = [total, grad_x[None]]
    for part in range(4):
        outs.extend(res[n][part] for n in _ORDER)
    return tuple(outs)


def kernel(x, positions, pre_mix_norm, w_in, q_norm, w_uq, kv_norm, w_ukv, conv_w, conv_b, conv_ln_g, conv_ln_b, conv_out_norm, attn_out_norm, w_out, post_mix_norm, pre_ffn_norm, w_gate, w_up, w_down, post_ffn_norm, loss_target, m_pre_mix_norm, m_w_in, m_q_norm, m_w_uq, m_kv_norm, m_w_ukv, m_conv_w, m_conv_b, m_conv_ln_g, m_conv_ln_b, m_conv_out_norm, m_attn_out_norm, m_w_out, m_post_mix_norm, m_pre_ffn_norm, m_w_gate, m_w_up, m_w_down, m_post_ffn_norm, v_pre_mix_norm, v_w_in, v_q_norm, v_w_uq, v_kv_norm, v_w_ukv, v_conv_w, v_conv_b, v_conv_ln_g, v_conv_ln_b, v_conv_out_norm, v_attn_out_norm, v_w_out, v_post_mix_norm, v_pre_ffn_norm, v_w_gate, v_w_up, v_w_down, v_post_ffn_norm):
    w = dict(zip(_ORDER, (pre_mix_norm, w_in, q_norm, w_uq, kv_norm, w_ukv, conv_w, conv_b, conv_ln_g, conv_ln_b,
                          conv_out_norm, attn_out_norm, w_out, post_mix_norm, pre_ffn_norm, w_gate, w_up, w_down,
                          post_ffn_norm)))
    m = dict(zip(_ORDER, (m_pre_mix_norm, m_w_in, m_q_norm, m_w_uq, m_kv_norm, m_w_ukv, m_conv_w, m_conv_b,
                          m_conv_ln_g, m_conv_ln_b, m_conv_out_norm, m_attn_out_norm, m_w_out, m_post_mix_norm,
                          m_pre_ffn_norm, m_w_gate, m_w_up, m_w_down, m_post_ffn_norm)))
    v = dict(zip(_ORDER, (v_pre_mix_norm, v_w_in, v_q_norm, v_w_uq, v_kv_norm, v_w_ukv, v_conv_w, v_conv_b,
                          v_conv_ln_g, v_conv_ln_b, v_conv_out_norm, v_attn_out_norm, v_w_out, v_post_mix_norm,
                          v_pre_ffn_norm, v_w_gate, v_w_up, v_w_down, v_post_ffn_norm)))
    return _step(x, positions, loss_target, w, m, v)
```

```python
import functools

import jax
import jax.numpy as jnp
from jax import lax
from jax.experimental import pallas as pl
from jax.experimental.pallas import tpu as pltpu

N_DEV = 8
N_HEADS = 8
QK_NOPE = 128
QK_ROPE = 64
V_HEAD = 128
QK_HEAD = QK_NOPE + QK_ROPE
HEAD_PAD = 256
LANES = 128
ATTN_BLOCK = 512
CONV_K = 31
CONV_PAD = 32
EPS = 1e-6
ROPE_THETA = 10000.0
ADAM_LR = 0.001
ADAM_B1 = 0.9
ADAM_B2 = 0.999
ADAM_EPS = 1e-08
ADAM_WD = 0.01
ADAM_STEP = 10
VMEM_LIMIT = 56 * 1024 * 1024
F32 = jnp.float32
BF16 = jnp.bfloat16
MESH = pl.DeviceIdType.MESH
NEG = -1e30


def _pick(n, prefs):
    for p in prefs:
        if p <= n and n % p == 0:
            return p
    return n


def _params(sem):
    return pltpu.CompilerParams(dimension_semantics=sem, vmem_limit_bytes=VMEM_LIMIT)


_DIMS = {"nn": (((1,), (0,)), ((), ())), "nt": (((1,), (1,)), ((), ())), "tn": (((0,), (0,)), ((), ()))}


MM_VMEM_BUDGET = 40 * 1024 * 1024
MM_MAX_MACS = 3 * 1024 ** 3


V7X_HBM_BYTES_PER_S = 3.0e12
V7X_MXU_MACS_PER_S = 0.45e15
GRID_STEP_S = 0.35e-6


def _mm_tiles(m, n, k, size_a, size_b, size_o):
    best = None
    for tm in sorted({m, 1024, 512, 256, 128}, reverse=True):
        if tm > m or m % tm:
            continue
        for tn in sorted({n, 2048, 1024, 512, 384, 256, 128}, reverse=True):
            if tn > n or n % tn:
                continue
            vmem = 2 * (tm * k * size_a + k * tn * size_b + tm * tn * size_o)
            if vmem > MM_VMEM_BUDGET or tm * tn * k > MM_MAX_MACS:
                continue
            b_reads = 1 if tn == n else m // tm
            traffic = m * k * size_a + b_reads * k * n * size_b + m * n * size_o
            exposed = tm * k * size_a + k * tn * size_b + tm * tn * size_o
            steps = (m // tm) * (n // tn)
            key = (max(traffic / V7X_HBM_BYTES_PER_S, m * n * k / V7X_MXU_MACS_PER_S)
                   + exposed / V7X_HBM_BYTES_PER_S + steps * GRID_STEP_S)
            if best is None or key < best[0]:
                best = (key, tm, tn)
    assert best is not None, (m, n, k)
    return best[1], best[2]


def _mm(a, b, mode, name, out_dtype=F32, add=None):
    if mode == "nn":
        (m, k), (k2, n) = a.shape, b.shape
    elif mode == "nt":
        (m, k), (n, k2) = a.shape, b.shape
    else:
        (k, m), (k2, n) = a.shape, b.shape
    assert k == k2, (a.shape, b.shape, mode)
    tm, tn = _mm_tiles(m, n, k, a.dtype.itemsize, b.dtype.itemsize,
                       jnp.dtype(out_dtype).itemsize + (0 if add is None else 4))
    dims = _DIMS[mode]

    def body(a_ref, b_ref, *rest):
        acc = lax.dot_general(a_ref[...].astype(BF16), b_ref[...].astype(BF16), dims, preferred_element_type=F32)
        if add is not None:
            acc = acc + rest[0][...]
        rest[-1][...] = acc.astype(rest[-1].dtype)

    if mode == "tn":
        a_spec = pl.BlockSpec((k, tm), lambda i, j: (0, i))
    else:
        a_spec = pl.BlockSpec((tm, k), lambda i, j: (i, 0))
    if mode == "nt":
        b_spec = pl.BlockSpec((tn, k), lambda i, j: (j, 0))
    else:
        b_spec = pl.BlockSpec((k, tn), lambda i, j: (0, j))
    o_spec = pl.BlockSpec((tm, tn), lambda i, j: (i, j))
    extra = [] if add is None else [add]
    return pl.pallas_call(
        body, name=name,
        grid=(m // tm, n // tn),
        in_specs=[a_spec, b_spec] + [o_spec] * len(extra),
        out_specs=o_spec,
        out_shape=jax.ShapeDtypeStruct((m, n), out_dtype),
        compiler_params=_params(("parallel", "parallel")),
    )(a, b, *extra)


def _sigmoid(x):
    return 1.0 / (1.0 + jnp.exp(-x))


def _rms(x, g):
    r = lax.rsqrt(jnp.mean(x * x, axis=-1, keepdims=True) + EPS)
    return (x * r) * g


def _rms_bwd(x, g, dy):
    r = lax.rsqrt(jnp.mean(x * x, axis=-1, keepdims=True) + EPS)
    xh = x * r
    dyg = dy * g
    dx = r * (dyg - xh * jnp.mean(dyg * xh, axis=-1, keepdims=True))
    return dx, jnp.sum(dy * xh, axis=0, keepdims=True)


def _ln(x, g, b):
    mu = jnp.mean(x, axis=-1, keepdims=True)
    xc = x - mu
    rs = lax.rsqrt(jnp.mean(xc * xc, axis=-1, keepdims=True) + EPS)
    return (xc * rs) * g + b


def _ln_bwd(x, g, dy):
    mu = jnp.mean(x, axis=-1, keepdims=True)
    xc = x - mu
    rs = lax.rsqrt(jnp.mean(xc * xc, axis=-1, keepdims=True) + EPS)
    xh = xc * rs
    dyg = dy * g
    dx = rs * (dyg - jnp.mean(dyg, axis=-1, keepdims=True) - xh * jnp.mean(dyg * xh, axis=-1, keepdims=True))
    return dx, jnp.sum(dy * xh, axis=0, keepdims=True), jnp.sum(dy, axis=0, keepdims=True)


def _silu(x):
    return x * _sigmoid(x)


def _silu_grad(x):
    s = _sigmoid(x)
    return s * (1.0 + x * (1.0 - s))


def _rope(x, cos, sa, sb):
    return x * cos + pltpu.roll(x, 96, 1) * sa + pltpu.roll(x, 32, 1) * sb


def _rope_t(d, cos, sa, sb):
    return d * cos - pltpu.roll(d, 96, 1) * sa - pltpu.roll(d, 32, 1) * sb


def _rows(ts, w):
    return pl.BlockSpec((ts, w), lambda i: (i, 0))


def _vec(w):
    return pl.BlockSpec((1, w), lambda i: (0, 0))


def _acc_init(i, *refs):
    @pl.when(i == 0)
    def _():
        for r in refs:
            r[...] = jnp.zeros_like(r)


def _rope_tables(pos, inv_freq):
    s = pos.shape[0]
    ts = _pick(s, (512, 256, 128))

    def body(p_ref, f_ref, c_ref, sa_ref, sb_ref):
        ang = p_ref[...].astype(F32) * f_ref[...]
        lane = lax.broadcasted_iota(jnp.int32, ang.shape, 1)
        c, sn = jnp.cos(ang), jnp.sin(ang)
        c_ref[...] = jnp.where(lane < QK_ROPE, c, 0.0)
        sa_ref[...] = jnp.where(lane < QK_ROPE // 2, -sn, 0.0)
        sb_ref[...] = jnp.where((lane >= QK_ROPE // 2) & (lane < QK_ROPE), sn, 0.0)

    out = jax.ShapeDtypeStruct((s, LANES), F32)
    return pl.pallas_call(
        body, name="rope_tables", grid=(s // ts,),
        in_specs=[_rows(ts, 1), _vec(LANES)],
        out_specs=[_rows(ts, LANES)] * 3, out_shape=[out] * 3,
        compiler_params=_params(("parallel",)),
    )(pos, inv_freq)


def _pre_fwd(x, g):
    s, d = x.shape
    ts = _pick(s, (256, 128))

    def body(x_ref, g_ref, h_ref):
        h_ref[...] = _rms(x_ref[...], g_ref[...]).astype(BF16)

    return pl.pallas_call(
        body, name="pre_fwd", grid=(s // ts,),
        in_specs=[_rows(ts, d), _vec(d)], out_specs=_rows(ts, d),
        out_shape=jax.ShapeDtypeStruct((s, d), BF16),
        compiler_params=_params(("parallel",)),
    )(x, g)


def _split_fwd(z, gq, gkv, tabs, c, ql, kvl):
    s, zw = z.shape
    ts = _pick(s, (256, 128))
    o_q, o_kv, o_kr = 2 * c, 2 * c + ql, 2 * c + ql + kvl

    def body(z_ref, gq_ref, gkv_ref, c_ref, sa_ref, sb_ref, u0_ref, qn_ref, kvn_ref, kpe_ref):
        u0_ref[...] = z_ref[:, 0:c] * _sigmoid(z_ref[:, c:2 * c])
        qn_ref[...] = _rms(z_ref[:, o_q:o_kv], gq_ref[...]).astype(BF16)
        kvn_ref[...] = _rms(z_ref[:, o_kv:o_kr], gkv_ref[...]).astype(BF16)
        kpe_ref[...] = _rope(z_ref[:, o_kr:o_kr + LANES], c_ref[...], sa_ref[...], sb_ref[...]).astype(BF16)

    return pl.pallas_call(
        body, name="split_fwd", grid=(s // ts,),
        in_specs=[_rows(ts, zw), _vec(ql), _vec(kvl)] + [_rows(ts, LANES)] * 3,
        out_specs=[_rows(ts, c), _rows(ts, ql), _rows(ts, kvl), _rows(ts, LANES)],
        out_shape=[jax.ShapeDtypeStruct((s, c), F32), jax.ShapeDtypeStruct((s, ql), BF16),
                   jax.ShapeDtypeStruct((s, kvl), BF16), jax.ShapeDtypeStruct((s, LANES), BF16)],
        compiler_params=_params(("parallel",)),
    )(z, gq, gkv, *tabs)


def _split_bwd(du0, z, dqn, dkvn, dkpe_h, gq, gkv, tabs, c, ql, kvl):
    s, zw = z.shape
    ts = _pick(s, (256, 128))
    o_q, o_kv, o_kr = 2 * c, 2 * c + ql, 2 * c + ql + kvl

    def body(du0_ref, z_ref, dqn_ref, dkvn_ref, dkh_ref, gq_ref, gkv_ref, c_ref, sa_ref, sb_ref,
             dz_ref, dgq_ref, dgkv_ref):
        _acc_init(pl.program_id(0), dgq_ref, dgkv_ref)
        du0 = du0_ref[...]
        a = z_ref[:, 0:c]
        sg = _sigmoid(z_ref[:, c:2 * c])
        dz_ref[:, 0:c] = (du0 * sg).astype(BF16)
        dz_ref[:, c:2 * c] = (du0 * a * sg * (1.0 - sg)).astype(BF16)
        dq, dgq = _rms_bwd(z_ref[:, o_q:o_kv], gq_ref[...], dqn_ref[...])
        dz_ref[:, o_q:o_kv] = dq.astype(BF16)
        dgq_ref[...] += dgq
        dkv, dgkv = _rms_bwd(z_ref[:, o_kv:o_kr], gkv_ref[...], dkvn_ref[...])
        dz_ref[:, o_kv:o_kr] = dkv.astype(BF16)
        dgkv_ref[...] += dgkv
        dk = dkh_ref[:, 0:LANES]
        for h in range(1, N_HEADS):
            dk = dk + dkh_ref[:, h * LANES:(h + 1) * LANES]
        dz_ref[:, o_kr:o_kr + LANES] = _rope_t(dk, c_ref[...], sa_ref[...], sb_ref[...]).astype(BF16)

    return pl.pallas_call(
        body, name="split_bwd", grid=(s // ts,),
        in_specs=[_rows(ts, c), _rows(ts, zw), _rows(ts, ql), _rows(ts, kvl), _rows(ts, N_HEADS * LANES),
                  _vec(ql), _vec(kvl)] + [_rows(ts, LANES)] * 3,
        out_specs=[_rows(ts, zw), _vec(ql), _vec(kvl)],
        out_shape=[jax.ShapeDtypeStruct((s, zw), BF16), jax.ShapeDtypeStruct((1, ql), F32),
                   jax.ShapeDtypeStruct((1, kvl), F32)],
        compiler_params=_params(("arbitrary",)),
    )(du0, z, dqn, dkvn, dkpe_h, gq, gkv, *tabs)


def _q_rope(qpre, tabs, transpose, out_dtype, name):
    s, w = qpre.shape
    ts = _pick(s, (256, 128))
    rot = _rope_t if transpose else _rope

    def body(q_ref, c_ref, sa_ref, sb_ref, o_ref):
        cs, sa, sb = c_ref[...], sa_ref[...], sb_ref[...]
        for h in range(N_HEADS):
            lo = h * HEAD_PAD
            o_ref[:, lo:lo + QK_NOPE] = q_ref[:, lo:lo + QK_NOPE].astype(out_dtype)
            o_ref[:, lo + QK_NOPE:lo + HEAD_PAD] = rot(q_ref[:, lo + QK_NOPE:lo + HEAD_PAD], cs, sa, sb).astype(out_dtype)

    return pl.pallas_call(
        body, name=name, grid=(s // ts,),
        in_specs=[_rows(ts, w)] + [_rows(ts, LANES)] * 3, out_specs=_rows(ts, w),
        out_shape=jax.ShapeDtypeStruct((s, w), out_dtype),
        compiler_params=_params(("parallel",)),
    )(qpre, *tabs)


def _conv_fwd(u0, w, b):
    s, c = u0.shape
    tc = LANES
    rc = _pick(s, (256, 128))

    def body(u_ref, w_ref, b_ref, o_ref, pad_ref):
        pad_ref[0:CONV_PAD, :] = jnp.zeros((CONV_PAD, tc), F32)
        pad_ref[CONV_PAD:CONV_PAD + s, :] = u_ref[...]
        for r in range(s // rc):
            acc = jnp.broadcast_to(b_ref[...], (rc, tc))
            for k in range(CONV_K):
                lo = r * rc + CONV_PAD - (CONV_K - 1) + k
                acc = acc + w_ref[k:k + 1, :] * pad_ref[lo:lo + rc, :]
            o_ref[r * rc:(r + 1) * rc, :] = acc

    col = lambda j: (0, j)
    return pl.pallas_call(
        body, name="conv_fwd", grid=(c // tc,),
        in_specs=[pl.BlockSpec((s, tc), col), pl.BlockSpec((CONV_K, tc), col), pl.BlockSpec((1, tc), col)],
        out_specs=pl.BlockSpec((s, tc), col),
        out_shape=jax.ShapeDtypeStruct((s, c), F32),
        scratch_shapes=[pltpu.VMEM((s + CONV_PAD, tc), F32)],
        compiler_params=_params(("parallel",)),
    )(u0, w, b)


def _conv_bwd(du1, u0, w):
    s, c = u0.shape
    tc = LANES
    rc = _pick(s, (256, 128))

    def body(d_ref, u_ref, w_ref, du_ref, dw_ref, db_ref, upad_ref, dpad_ref):
        upad_ref[0:CONV_PAD, :] = jnp.zeros((CONV_PAD, tc), F32)
        upad_ref[CONV_PAD:CONV_PAD + s, :] = u_ref[...]
        dpad_ref[0:s, :] = d_ref[...]
        dpad_ref[s:s + CONV_PAD, :] = jnp.zeros((CONV_PAD, tc), F32)
        for r in range(s // rc):
            acc = jnp.zeros((rc, tc), F32)
            for k in range(CONV_K):
                lo = r * rc + (CONV_K - 1) - k
                acc = acc + w_ref[k:k + 1, :] * dpad_ref[lo:lo + rc, :]
            du_ref[r * rc:(r + 1) * rc, :] = acc
        for k in range(CONV_K):
            acc8 = jnp.zeros((8, tc), F32)
            for r in range(s // rc):
                lo = r * rc + CONV_PAD - (CONV_K - 1) + k
                prod = d_ref[r * rc:(r + 1) * rc, :] * upad_ref[lo:lo + rc, :]
                acc8 = acc8 + jnp.sum(prod.reshape(rc // 8, 8, tc), axis=0)
            dw_ref[k:k + 1, :] = jnp.sum(acc8, axis=0, keepdims=True)
        db_ref[...] = jnp.sum(d_ref[...], axis=0, keepdims=True)

    col = lambda j: (0, j)
    return pl.pallas_call(
        body, name="conv_bwd", grid=(c // tc,),
        in_specs=[pl.BlockSpec((s, tc), col), pl.BlockSpec((s, tc), col), pl.BlockSpec((CONV_K, tc), col)],
        out_specs=[pl.BlockSpec((s, tc), col), pl.BlockSpec((CONV_K, tc), col), pl.BlockSpec((1, tc), col)],
        out_shape=[jax.ShapeDtypeStruct((s, c), F32), jax.ShapeDtypeStruct((CONV_K, c), F32),
                   jax.ShapeDtypeStruct((1, c), F32)],
        scratch_shapes=[pltpu.VMEM((s + CONV_PAD, tc), F32), pltpu.VMEM((s + CONV_PAD, tc), F32)],
        compiler_params=_params(("parallel",)),
    )(du1, u0, w)


def _causal_mask(sc, qi, kj, tq, tk):
    rows = qi * tq + lax.broadcasted_iota(jnp.int32, sc.shape, 0)
    cols = kj * tk + lax.broadcasted_iota(jnp.int32, sc.shape, 1)
    return jnp.where(cols <= rows, sc, NEG)


def _attn_fwd(q, kv, kpe):
    s = q.shape[0]
    tq = tk = _pick(s, (ATTN_BLOCK, 256, 128))
    reps = tk // LANES
    scale = QK_HEAD ** -0.5
    nt = (((1,), (1,)), ((), ()))

    def body(q_ref, kn_ref, v_ref, kpe_ref, o_ref, lse_ref, kf_ref, vb_ref, m_ref, l_ref, acc_ref):
        i = pl.program_id(1)

        @pl.when(i == 0)
        def _():
            kf_ref[:, 0:QK_NOPE] = kn_ref[...].astype(BF16)
            kf_ref[:, QK_NOPE:HEAD_PAD] = kpe_ref[...]
            vb_ref[...] = v_ref[...].astype(BF16)

        qb = q_ref[...]
        m_ref[...] = jnp.full((tq, LANES), NEG, F32)
        l_ref[...] = jnp.zeros((tq, LANES), F32)
        acc_ref[...] = jnp.zeros((tq, V_HEAD), F32)

        def block(j, diagonal):
            off = pl.multiple_of(j * tk, tk)
            sc = lax.dot_general(qb, kf_ref[pl.ds(off, tk), :], nt, preferred_element_type=F32) * scale
            if diagonal:
                sc = _causal_mask(sc, 0, 0, tq, tk)
            m_prev = m_ref[...]
            m_new = jnp.maximum(m_prev, jnp.max(sc, axis=1, keepdims=True))
            p = jnp.exp(sc - jnp.tile(m_new, (1, reps)))
            alpha = jnp.exp(m_prev - m_new)
            l_ref[...] = alpha * l_ref[...] + jnp.sum(p, axis=1, keepdims=True)
            acc_ref[...] = alpha * acc_ref[...] + jnp.dot(p.astype(BF16), vb_ref[pl.ds(off, tk), :],
                                                          preferred_element_type=F32)
            m_ref[...] = m_new

        def below_diagonal(j, carry):
            block(j, False)
            return carry

        lax.fori_loop(0, i, below_diagonal, 0)
        block(i, True)
        o_ref[...] = acc_ref[...] / l_ref[...]
        lse_ref[...] = m_ref[...] + jnp.log(l_ref[...])

    return pl.pallas_call(
        body, name="attn_fwd", grid=(N_HEADS, s // tq),
        in_specs=[pl.BlockSpec((tq, HEAD_PAD), lambda h, i: (i, h)),
                  pl.BlockSpec((s, QK_NOPE), lambda h, i: (0, 2 * h)),
                  pl.BlockSpec((s, V_HEAD), lambda h, i: (0, 2 * h + 1)),
                  pl.BlockSpec((s, LANES), lambda h, i: (0, 0))],
        out_specs=[pl.BlockSpec((tq, V_HEAD), lambda h, i: (i, h)),
                   pl.BlockSpec((tq, LANES), lambda h, i: (i, h))],
        out_shape=[jax.ShapeDtypeStruct((s, N_HEADS * V_HEAD), F32),
                   jax.ShapeDtypeStruct((s, N_HEADS * LANES), F32)],
        scratch_shapes=[pltpu.VMEM((s, HEAD_PAD), BF16), pltpu.VMEM((s, V_HEAD), BF16),
                        pltpu.VMEM((tq, LANES), F32), pltpu.VMEM((tq, LANES), F32), pltpu.VMEM((tq, V_HEAD), F32)],
        compiler_params=_params(("parallel", "arbitrary")),
    )(q, kv, kv, kpe)


def _attn_bwd(q, kv, kpe, o, do, lse):
    s = q.shape[0]
    tq = tk = _pick(s, (ATTN_BLOCK, 256, 128))
    nq = s // tq
    reps = tk // LANES
    scale = QK_HEAD ** -0.5
    nt = (((1,), (1,)), ((), ()))
    tn = (((0,), (0,)), ((), ()))

    def body(q_ref, kn_ref, v_ref, kpe_ref, o_ref, do_ref, lse_ref, dq_ref, dkv_ref, dkpe_ref,
             kf_ref, vb_ref, dk_ref, dv_ref):
        j = pl.program_id(1)

        @pl.when(j == 0)
        def _():
            dq_ref[...] = jnp.zeros_like(dq_ref)

        kf_ref[:, 0:QK_NOPE] = kn_ref[...].astype(BF16)
        kf_ref[:, QK_NOPE:HEAD_PAD] = kpe_ref[...]
        vb_ref[...] = v_ref[...].astype(BF16)
        dk_ref[...] = jnp.zeros_like(dk_ref)
        dv_ref[...] = jnp.zeros_like(dv_ref)

        def block(i, diagonal):
            off = pl.multiple_of(i * tq, tq)
            qb = q_ref[pl.ds(off, tq), :]
            dob = do_ref[pl.ds(off, tq), :]
            delta = jnp.sum(dob * o_ref[pl.ds(off, tq), :], axis=1, keepdims=True)
            sc = lax.dot_general(qb, kf_ref[...], nt, preferred_element_type=F32) * scale
            if diagonal:
                sc = _causal_mask(sc, 0, 0, tq, tk)
            p = jnp.exp(sc - jnp.tile(lse_ref[pl.ds(off, tq), :], (1, reps)))
            dob16 = dob.astype(BF16)
            dv_ref[...] += lax.dot_general(p.astype(BF16), dob16, tn, preferred_element_type=F32)
            dp = lax.dot_general(dob16, vb_ref[...], nt, preferred_element_type=F32)
            ds = (p * (dp - delta) * scale).astype(BF16)
            dq_ref[pl.ds(off, tq), :] += jnp.dot(ds, kf_ref[...], preferred_element_type=F32)
            dk_ref[...] += lax.dot_general(ds, qb, tn, preferred_element_type=F32)

        def above_diagonal(i, carry):
            block(i, False)
            return carry

        block(j, True)
        lax.fori_loop(j + 1, nq, above_diagonal, 0)
        dkv_ref[:, 0:QK_NOPE] = dk_ref[:, 0:QK_NOPE]
        dkv_ref[:, QK_NOPE:HEAD_PAD] = dv_ref[...]
        dkpe_ref[...] = dk_ref[:, QK_NOPE:HEAD_PAD]

    head_rows = lambda w: pl.BlockSpec((s, w), lambda h, j: (0, h))
    return pl.pallas_call(
        body, name="attn_bwd", grid=(N_HEADS, s // tk),
        in_specs=[head_rows(HEAD_PAD),
                  pl.BlockSpec((tk, QK_NOPE), lambda h, j: (j, 2 * h)),
                  pl.BlockSpec((tk, V_HEAD), lambda h, j: (j, 2 * h + 1)),
                  pl.BlockSpec((tk, LANES), lambda h, j: (j, 0)),
                  head_rows(V_HEAD), head_rows(V_HEAD), head_rows(LANES)],
        out_specs=[head_rows(HEAD_PAD),
                   pl.BlockSpec((tk, HEAD_PAD), lambda h, j: (j, h)),
                   pl.BlockSpec((tk, LANES), lambda h, j: (j, h))],
        out_shape=[jax.ShapeDtypeStruct((s, N_HEADS * HEAD_PAD), F32),
                   jax.ShapeDtypeStruct((s, N_HEADS * HEAD_PAD), F32),
                   jax.ShapeDtypeStruct((s, N_HEADS * LANES), F32)],
        scratch_shapes=[pltpu.VMEM((tk, HEAD_PAD), BF16), pltpu.VMEM((tk, V_HEAD), BF16),
                        pltpu.VMEM((tk, HEAD_PAD), F32), pltpu.VMEM((tk, V_HEAD), F32)],
        compiler_params=_params(("parallel", "arbitrary")),
    )(q, kv, kv, kpe, o, do, lse)


def _mix_fwd(u1, lng, lnb, gcon, attn, gattn):
    s, c = u1.shape
    ac = attn.shape[1]
    ts = _pick(s, (256, 128))

    def body(u_ref, lg_ref, lb_ref, gc_ref, a_ref, ga_ref, o_ref):
        t3 = _silu(_ln(u_ref[...], lg_ref[...], lb_ref[...]))
        o_ref[:, 0:c] = _rms(t3, gc_ref[...]).astype(BF16)
        o_ref[:, c:c + ac] = _rms(a_ref[...], ga_ref[...]).astype(BF16)

    return pl.pallas_call(
        body, name="mix_fwd", grid=(s // ts,),
        in_specs=[_rows(ts, c), _vec(c), _vec(c), _vec(c), _rows(ts, ac), _vec(ac)],
        out_specs=_rows(ts, c + ac),
        out_shape=jax.ShapeDtypeStruct((s, c + ac), BF16),
        compiler_params=_params(("parallel",)),
    )(u1, lng, lnb, gcon, attn, gattn)


def _mix_bwd(dmixin, u1, lng, lnb, gcon, attn, gattn):
    s, c = u1.shape
    ac = attn.shape[1]
    ts = _pick(s, (256, 128))

    def body(d_ref, u_ref, lg_ref, lb_ref, gc_ref, a_ref, ga_ref,
             du_ref, da_ref, dlg_ref, dlb_ref, dgc_ref, dga_ref):
        _acc_init(pl.program_id(0), dlg_ref, dlb_ref, dgc_ref, dga_ref)
        u = u_ref[...]
        t2 = _ln(u, lg_ref[...], lb_ref[...])
        dt3, dgc = _rms_bwd(_silu(t2), gc_ref[...], d_ref[:, 0:c])
        du, dlg, dlb = _ln_bwd(u, lg_ref[...], dt3 * _silu_grad(t2))
        du_ref[...] = du
        dlg_ref[...] += dlg
        dlb_ref[...] += dlb
        dgc_ref[...] += dgc
        da, dga = _rms_bwd(a_ref[...], ga_ref[...], d_ref[:, c:c + ac])
        da_ref[...] = da
        dga_ref[...] += dga

    return pl.pallas_call(
        body, name="mix_bwd", grid=(s // ts,),
        in_specs=[_rows(ts, c + ac), _rows(ts, c), _vec(c), _vec(c), _vec(c), _rows(ts, ac), _vec(ac)],
        out_specs=[_rows(ts, c), _rows(ts, ac), _vec(c), _vec(c), _vec(c), _vec(ac)],
        out_shape=[jax.ShapeDtypeStruct((s, c), F32), jax.ShapeDtypeStruct((s, ac), F32),
                   jax.ShapeDtypeStruct((1, c), F32), jax.ShapeDtypeStruct((1, c), F32),
                   jax.ShapeDtypeStruct((1, c), F32), jax.ShapeDtypeStruct((1, ac), F32)],
        compiler_params=_params(("arbitrary",)),
    )(dmixin, u1, lng, lnb, gcon, attn, gattn)


def _post_mix_fwd(x, mix, gpost, gpre):
    s, d = x.shape
    ts = _pick(s, (256, 128))

    def body(x_ref, m_ref, gp_ref, gf_ref, x1_ref, hf_ref):
        x1 = x_ref[...] + _rms(m_ref[...], gp_ref[...])
        x1_ref[...] = x1
        hf_ref[...] = _rms(x1, gf_ref[...]).astype(BF16)

    return pl.pallas_call(
        body, name="post_mix_fwd", grid=(s // ts,),
        in_specs=[_rows(ts, d), _rows(ts, d), _vec(d), _vec(d)],
        out_specs=[_rows(ts, d), _rows(ts, d)],
        out_shape=[jax.ShapeDtypeStruct((s, d), F32), jax.ShapeDtypeStruct((s, d), BF16)],
        compiler_params=_params(("parallel",)),
    )(x, mix, gpost, gpre)


def _post_mix_bwd(dy, dhf, x1, gpre, mix, gpost):
    s, d = x1.shape
    ts = _pick(s, (256, 128))

    def body(dy_ref, dh_ref, x1_ref, gf_ref, m_ref, gp_ref, dx1_ref, dm_ref, dgf_ref, dgp_ref):
        _acc_init(pl.program_id(0), dgf_ref, dgp_ref)
        dxa, dgf = _rms_bwd(x1_ref[...], gf_ref[...], dh_ref[...])
        dx1 = dy_ref[...] + dxa
        dx1_ref[...] = dx1
        dgf_ref[...] += dgf
        dm, dgp = _rms_bwd(m_ref[...], gp_ref[...], dx1)
        dm_ref[...] = dm.astype(BF16)
        dgp_ref[...] += dgp

    return pl.pallas_call(
        body, name="post_mix_bwd", grid=(s // ts,),
        in_specs=[_rows(ts, d), _rows(ts, d), _rows(ts, d), _vec(d), _rows(ts, d), _vec(d)],
        out_specs=[_rows(ts, d), _rows(ts, d), _vec(d), _vec(d)],
        out_shape=[jax.ShapeDtypeStruct((s, d), F32), jax.ShapeDtypeStruct((s, d), BF16),
                   jax.ShapeDtypeStruct((1, d), F32), jax.ShapeDtypeStruct((1, d), F32)],
        compiler_params=_params(("arbitrary",)),
    )(dy, dhf, x1, gpre, mix, gpost)


def _ffn_up(hf, wg, wu):
    s, d = hf.shape
    nsh, fs, _ = wg.shape
    tm = _pick(s, (1024, 512, 256, 128))
    nt = (((1,), (1,)), ((), ()))

    def body(h_ref, wg_ref, wu_ref, g_ref, u_ref, a_ref):
        h = h_ref[...]
        g = lax.dot_general(h, wg_ref[...], nt, preferred_element_type=F32)
        u = lax.dot_general(h, wu_ref[...], nt, preferred_element_type=F32)
        g_ref[...] = g
        u_ref[...] = u
        a_ref[...] = (_silu(g) * u).astype(BF16)

    w_spec = pl.BlockSpec((None, fs, d), lambda i, j: (j, 0, 0))
    o_spec = pl.BlockSpec((None, tm, fs), lambda i, j: (j, i, 0))
    return pl.pallas_call(
        body, name="ffn_up", grid=(s // tm, nsh),
        in_specs=[pl.BlockSpec((tm, d), lambda i, j: (i, 0)), w_spec, w_spec],
        out_specs=[o_spec] * 3,
        out_shape=[jax.ShapeDtypeStruct((nsh, s, fs), F32)] * 2 + [jax.ShapeDtypeStruct((nsh, s, fs), BF16)],
        compiler_params=_params(("parallel", "parallel")),
    )(hf, wg, wu)


def _ffn_down(act, wd, name, add=None):
    nsh, s, fs = act.shape
    d = wd.shape[2]
    tm = _pick(s, (1024, 512, 256, 128))
    tn = _pick(d, (512, 256, 128) if add is None else (256, 128))

    def body(a_ref, w_ref, *rest):
        acc = jnp.dot(a_ref[0], w_ref[0], preferred_element_type=F32)
        for j in range(1, nsh):
            acc = acc + jnp.dot(a_ref[j], w_ref[j], preferred_element_type=F32)
        if add is not None:
            acc = acc + rest[0][...]
        rest[-1][...] = acc

    o_spec = pl.BlockSpec((tm, tn), lambda i, j: (i, j))
    extra = [] if add is None else [add]
    return pl.pallas_call(
        body, name=name, grid=(s // tm, d // tn),
        in_specs=[pl.BlockSpec((nsh, tm, fs), lambda i, j: (0, i, 0)),
                  pl.BlockSpec((nsh, fs, tn), lambda i, j: (0, 0, j))] + [o_spec] * len(extra),
        out_specs=o_spec,
        out_shape=jax.ShapeDtypeStruct((s, d), F32),
        compiler_params=_params(("parallel", "parallel")),
    )(act, wd, *extra)


def _ffn_down_bwd(dff, wd, gate, up, behind):
    s, d = dff.shape
    nsh, fs, _ = wd.shape
    tm = _pick(s, (1024, 512, 256, 128))
    nt = (((1,), (1,)), ((), ()))

    def body(d_ref, w_ref, g_ref, u_ref, _, dg_ref, du_ref):
        dact = lax.dot_general(d_ref[...], w_ref[...], nt, preferred_element_type=F32)
        g = g_ref[...]
        dg_ref[...] = (dact * u_ref[...] * _silu_grad(g)).astype(BF16)
        du_ref[...] = (dact * _silu(g)).astype(BF16)

    h_spec = pl.BlockSpec((None, tm, fs), lambda i, j: (j, i, 0))
    return pl.pallas_call(
        body, name="ffn_down_bwd", grid=(s // tm, nsh),
        in_specs=[pl.BlockSpec((tm, d), lambda i, j: (i, 0)),
                  pl.BlockSpec((None, fs, d), lambda i, j: (j, 0, 0)), h_spec, h_spec,
                  pl.BlockSpec((8, LANES), lambda i, j: (0, 0))],
        out_specs=[h_spec] * 2,
        out_shape=[jax.ShapeDtypeStruct((nsh, s, fs), BF16)] * 2,
        compiler_params=_params(("parallel", "parallel")),
    )(dff, wd, gate, up, behind)


def _ffn_dw(hidden, other, name):
    nsh, s, fs = hidden.shape
    d = other.shape[1]
    tn = (((0,), (0,)), ((), ()))

    def body(a_ref, d_ref, o_ref):
        o_ref[...] = lax.dot_general(a_ref[...], d_ref[...], tn, preferred_element_type=F32).astype(BF16)

    return pl.pallas_call(
        body, name=name, grid=(nsh,),
        in_specs=[pl.BlockSpec((None, s, fs), lambda j: (j, 0, 0)), pl.BlockSpec((s, d), lambda j: (0, 0))],
        out_specs=pl.BlockSpec((None, fs, d), lambda j: (j, 0, 0)),
        out_shape=jax.ShapeDtypeStruct((nsh, fs, d), BF16),
        compiler_params=_params(("parallel",)),
    )(hidden, other)


def _final(ff, x1, tgt, g):
    s, d = x1.shape
    ts = _pick(s, (256, 128))

    def body(ff_ref, x1_ref, t_ref, g_ref, loss_ref, dy_ref, dff_ref, dg_ref):
        _acc_init(pl.program_id(0), loss_ref, dg_ref)
        ff_v = ff_ref[...]
        err = x1_ref[...] + _rms(ff_v, g_ref[...]) - t_ref[...]
        tok = jnp.mean(err * err, axis=-1, keepdims=True)
        loss_ref[...] += 0.5 * jnp.sum(tok, axis=0, keepdims=True)
        dy = err * (1.0 / d)
        dy_ref[...] = dy
        dff, dg = _rms_bwd(ff_v, g_ref[...], dy)
        dff_ref[...] = dff.astype(BF16)
        dg_ref[...] += dg

    return pl.pallas_call(
        body, name="final", grid=(s // ts,),
        in_specs=[_rows(ts, d), _rows(ts, d), _rows(ts, d), _vec(d)],
        out_specs=[_vec(LANES), _rows(ts, d), _rows(ts, d), _vec(d)],
        out_shape=[jax.ShapeDtypeStruct((1, LANES), F32), jax.ShapeDtypeStruct((s, d), F32),
                   jax.ShapeDtypeStruct((s, d), BF16), jax.ShapeDtypeStruct((1, d), F32)],
        compiler_params=_params(("arbitrary",)),
    )(ff, x1, tgt, g)


def _pre_bwd(dx1, dh, x, g):
    s, d = x.shape
    ts = _pick(s, (256, 128))

    def body(dx1_ref, dh_ref, x_ref, g_ref, dx_ref, dg_ref):
        _acc_init(pl.program_id(0), dg_ref)
        dxa, dg = _rms_bwd(x_ref[...], g_ref[...], dh_ref[...])
        dx_ref[...] = dx1_ref[...] + dxa
        dg_ref[...] += dg

    return pl.pallas_call(
        body, name="pre_bwd", grid=(s // ts,),
        in_specs=[_rows(ts, d), _rows(ts, d), _rows(ts, d), _vec(d)],
        out_specs=[_rows(ts, d), _vec(d)],
        out_shape=[jax.ShapeDtypeStruct((s, d), F32), jax.ShapeDtypeStruct((1, d), F32)],
        compiler_params=_params(("arbitrary",)),
    )(dx1, dh, x, g)


def _local_step(x, pos, tgt, vecs, in_weights_fn, mix_weights_fn, up_weights_fn, down_weights_fn, grads_fn):
    c = vecs["conv_b"].shape[1]
    ql = vecs["q_norm"].shape[1]
    kvl = vecs["kv_norm"].shape[1]
    half = jnp.arange(0, QK_ROPE, 2, dtype=F32)
    freq = ROPE_THETA ** (-half / QK_ROPE)
    inv_freq = jnp.concatenate([freq, freq, jnp.zeros((LANES - QK_ROPE,), F32)])[None, :]
    tabs = _rope_tables(pos, inv_freq)

    h = _pre_fwd(x, vecs["pre_mix_norm"])
    w_in_t, zero = in_weights_fn(h)
    z = _mm(h, w_in_t, "nt", "mm_z")
    w_uq_t, w_ukv, conv_w, w_out = mix_weights_fn(z)
    u0, qn, kvn, kpe = _split_fwd(z, vecs["q_norm"] + zero, vecs["kv_norm"], tabs, c, ql, kvl)
    u1 = _conv_fwd(u0, conv_w, vecs["conv_b"])
    q = _q_rope(_mm(qn, w_uq_t, "nt", "mm_q"), tabs, False, BF16, "q_rope")
    kv = _mm(kvn, w_ukv, "nn", "mm_kv")
    attn, lse = _attn_fwd(q, kv, kpe)
    mixin = _mix_fwd(u1, vecs["conv_ln_g"], vecs["conv_ln_b"], vecs["conv_out_norm"], attn, vecs["attn_out_norm"])
    mix = _mm(mixin, w_out, "nn", "mm_mix")
    x1, hf = _post_mix_fwd(x, mix, vecs["post_mix_norm"], vecs["pre_ffn_norm"])
    w_gate, w_up = up_weights_fn(mix)
    gate, up, act = _ffn_up(hf, w_gate, w_up)
    w_down = down_weights_fn(act)
    ff = _ffn_down(act, w_down, "ffn_down")
    loss, dy, dff, d_post_ffn = _final(ff, x1, tgt, vecs["post_ffn_norm"])

    g = {"post_ffn_norm": d_post_ffn}
    zero = grads_fn("down", ("w_down",), [_ffn_dw(act, dff, "ffn_dw_down")], dff)
    dgate, dup = _ffn_down_bwd(dff, w_down, gate, up, jnp.zeros((8, LANES), F32) + zero)
    dw_gate = _ffn_dw(dgate, hf, "ffn_dw_gate")
    dw_up = _ffn_dw(dup, hf, "ffn_dw_up")
    zero = zero + grads_fn("up", ("w_gate", "w_up"), [dw_gate, dw_up], dgate)
    dhf = _ffn_down(dup, w_up, "ffn_dhf_up", add=_ffn_down(dgate, w_gate, "ffn_dhf_gate"))
    dx1, dmix, g["pre_ffn_norm"], g["post_mix_norm"] = _post_mix_bwd(
        dy, dhf, x1, vecs["pre_ffn_norm"] + zero, mix, vecs["post_mix_norm"])
    dmixin = _mm(dmix, w_out, "nt", "mm_dmixin")
    dw_out = _mm(mixin, dmix, "tn", "mm_dw_out", BF16)
    zero = grads_fn("out", ("w_out",), [dw_out.reshape(N_DEV, -1, dw_out.shape[1])], dmix)
    du1, dattn, g["conv_ln_g"], g["conv_ln_b"], g["conv_out_norm"], g["attn_out_norm"] = _mix_bwd(
        dmixin, u1, vecs["conv_ln_g"] + zero, vecs["conv_ln_b"], vecs["conv_out_norm"], attn, vecs["attn_out_norm"])
    du0, g["conv_w"], g["conv_b"] = _conv_bwd(du1, u0, conv_w)
    dq, dkv, dkpe_h = _attn_bwd(q, kv, kpe, attn, dattn, lse)
    dqpre = _q_rope(dq, tabs, True, BF16, "q_rope_bwd")
    dqn = _mm(dqpre, w_uq_t, "nn", "mm_dqn")
    g["w_uq_t"] = _mm(dqpre, qn, "tn", "mm_dw_uq", BF16)
    dkvn = _mm(dkv, w_ukv, "nt", "mm_dkvn")
    g["w_ukv"] = _mm(kvn, dkv, "tn", "mm_dw_ukv", BF16)
    dz, g["q_norm"], g["kv_norm"] = _split_bwd(du0, z, dqn, dkvn, dkpe_h, vecs["q_norm"], vecs["kv_norm"], tabs, c, ql, kvl)
    dh = _mm(dz, w_in_t, "nn", "mm_dh")
    g["w_in_t"] = _mm(dz, h, "tn", "mm_dw_in", BF16)
    grad_x, g["pre_mix_norm"] = _pre_bwd(dx1, dh, x, vecs["pre_mix_norm"])
    return loss, grad_x, g


def _my_index():
    return 4 * lax.axis_index("x") + 2 * lax.axis_index("y") + lax.axis_index("c")


def _coords(idx):
    return ((idx >> 2) & 1, (idx >> 1) & 1, idx & 1)


def _place():
    x, y, c = lax.axis_index("x"), lax.axis_index("y"), lax.axis_index("c")
    return (x, y, c), (x, y, 1 - c), [(1 - x, y), (x, 1 - y), (1 - x, 1 - y)]


def _comm_call(body, name, arrays, out_shapes, n_sems):
    n = len(arrays)
    any_spec = pl.BlockSpec(memory_space=pl.ANY)
    return pl.pallas_call(
        body, name=name,
        in_specs=[any_spec] * n, out_specs=[any_spec] * len(out_shapes), out_shape=out_shapes,
        scratch_shapes=[pltpu.SemaphoreType.DMA((n, n_sems))] * 3,
        compiler_params=pltpu.CompilerParams(has_side_effects=True),
    )(*arrays)


def _gather_direct(arrays, name):
    n = len(arrays)

    def body(*refs):
        ins, outs = refs[:n], refs[n:2 * n]
        send_sems, recv_sems, local_sems = refs[2 * n:]
        me = _my_index()
        local = [pltpu.make_async_copy(ins[k], outs[k].at[me], local_sems.at[k, 0]) for k in range(n)]
        for cp in local:
            cp.start()

        def copy(k, p, slot):
            return pltpu.make_async_remote_copy(
                src_ref=ins[k], dst_ref=outs[k].at[slot], send_sem=send_sems.at[k, p - 1],
                recv_sem=recv_sems.at[k, p - 1], device_id=_coords(me ^ p), device_id_type=MESH)

        sends = [copy(k, p, me) for p in range(1, N_DEV) for k in range(n)]
        for cp in sends:
            cp.start()
        for p in range(1, N_DEV):
            for k in range(n):
                copy(k, p, me ^ p).wait_recv()
        for cp in sends:
            cp.wait_send()
        for cp in local:
            cp.wait()

    out_shapes = [jax.ShapeDtypeStruct((N_DEV,) + a.shape, a.dtype) for a in arrays]
    return _comm_call(body, name, arrays, out_shapes, N_DEV - 1)


HBM_SPEC = pl.BlockSpec(memory_space=pltpu.HBM)
SEM_SPEC = pl.BlockSpec(memory_space=pltpu.SEMAPHORE)
DATAFLOW = pltpu.SideEffectType.DATAFLOW_SIDE_EFFECTING


def _split_start(name, copies_of, srcs, lands, after):
    n = len(srcs)

    def body(*refs):
        outs = refs[2 * n + 1:]
        for k in range(n):
            for cp in copies_of(refs[k], refs[n + k], outs[k], outs[n + k]):
                cp.start()
        outs[-1][...] = jnp.zeros_like(outs[-1])

    hbm = lambda a: pltpu.HBM(a.shape, a.dtype)
    out = pl.pallas_call(
        body, name=name,
        in_specs=[HBM_SPEC] * (2 * n + 1),
        out_specs=[SEM_SPEC] * (2 * n) + [HBM_SPEC] * (2 * n) + [pl.BlockSpec(memory_space=pltpu.VMEM)],
        out_shape=[pltpu.SemaphoreType.DMA(())] * (2 * n) + [hbm(a) for a in srcs] + [hbm(a) for a in lands]
        + [jax.ShapeDtypeStruct((8, LANES), F32)],
        input_output_aliases={k: 2 * n + k for k in range(2 * n)},
        compiler_params=pltpu.CompilerParams(has_side_effects=DATAFLOW),
    )(*[pltpu.with_memory_space_constraint(a, pltpu.HBM) for a in list(srcs) + list(lands) + [after]])
    return (out[:n], out[n:2 * n], out[2 * n:3 * n], out[3 * n:4 * n]), out[-1][0, 0]


def _split_wait(name, n_copies, started, after):
    send_sems, recv_sems, srcs, lands = started
    n = len(srcs)

    def body(*refs):
        for k in range(n):
            slots = refs[n + k].at[pl.ds(0, n_copies)]
            all_copies = pltpu.make_async_remote_copy(
                src_ref=slots, dst_ref=slots, send_sem=refs[2 * n + k], recv_sem=refs[3 * n + k],
                device_id=_place()[0], device_id_type=MESH)
            all_copies.wait_send()
            all_copies.wait_recv()

    hbm = lambda a: pltpu.HBM(a.shape, a.dtype)
    out = pl.pallas_call(
        body, name=name,
        in_specs=[HBM_SPEC] * (2 * n) + [SEM_SPEC] * (2 * n) + [HBM_SPEC],
        out_specs=[HBM_SPEC] * (2 * n),
        out_shape=[hbm(a) for a in srcs] + [hbm(a) for a in lands],
        input_output_aliases={k: k for k in range(2 * n)},
        compiler_params=pltpu.CompilerParams(has_side_effects=DATAFLOW),
    )(*srcs, *lands, *send_sems, *recv_sems, pltpu.with_memory_space_constraint(after, pltpu.HBM))
    return out[:n], out[n:]


def _slot(chip, core):
    return 4 * chip[0] + 2 * chip[1] + core


def _gather_copies(src, land, send_sem, recv_sem):
    (x, y, c), sib, chips = _place()
    return [pltpu.make_async_remote_copy(src_ref=src, dst_ref=land.at[_slot((x, y), c)], send_sem=send_sem,
                                         recv_sem=recv_sem, device_id=to, device_id_type=MESH)
            for to in [sib] + [(*chip, c) for chip in chips]]


def _gather_pass_on(lands, name):
    n = len(lands)

    def body(*refs):
        ins, outs = refs[:n], refs[n:2 * n]
        send_sems, recv_sems = refs[2 * n:]
        (x, y, c), sib, chips = _place()
        sends = []
        for k in range(n):
            for j, chip in enumerate(chips):
                sends.append(pltpu.make_async_remote_copy(
                    src_ref=ins[k].at[_slot(chip, c)], dst_ref=outs[k].at[_slot(chip, c)],
                    send_sem=send_sems.at[k, j], recv_sem=recv_sems.at[k, j], device_id=sib, device_id_type=MESH))
        for cp in sends:
            cp.start()
        for cp in sends:
            cp.wait_recv()
        for cp in sends:
            cp.wait_send()

    any_spec = pl.BlockSpec(memory_space=pl.ANY)
    return pl.pallas_call(
        body, name=name,
        in_specs=[any_spec] * n, out_specs=[any_spec] * n,
        out_shape=[jax.ShapeDtypeStruct(a.shape, a.dtype) for a in lands],
        input_output_aliases={k: k for k in range(n)},
        scratch_shapes=[pltpu.SemaphoreType.DMA((n, 3))] * 2,
        compiler_params=pltpu.CompilerParams(has_side_effects=True),
    )(*lands)


def _chip_copies(src, land, send_sem, recv_sem):
    (x, y, c), _, chips = _place()
    return [pltpu.make_async_remote_copy(src_ref=src.at[2 * chip[0] + chip[1]], dst_ref=land.at[j], send_sem=send_sem,
                                         recv_sem=recv_sem, device_id=(*chip, c), device_id_type=MESH)
            for j, chip in enumerate(chips)]


ROW_TILE_BYTES = 14 * 1024 * 1024


def _stream_tile(r, c, bytes_per_elem):
    if r * c * bytes_per_elem <= ROW_TILE_BYTES:
        return r, c
    rows = [t for t in range(16, r, 16) if r % t == 0 and t * c * bytes_per_elem <= ROW_TILE_BYTES]
    if rows:
        return max(rows), c
    cols = [t for t in range(LANES, c, LANES) if c % t == 0 and r * t * bytes_per_elem <= ROW_TILE_BYTES]
    return r, max(cols)


def _sibling_copies(src, land, send_sem, recv_sem):
    (x, y, c), sib, _ = _place()
    return [pltpu.make_async_remote_copy(src_ref=src.at[2 * q + 1 - c], dst_ref=land.at[q], send_sem=send_sem,
                                         recv_sem=recv_sem, device_id=sib, device_id_type=MESH)
            for q in range(4)]


def _pair_sum(core, blocks, theirs, name):
    q, r, c = theirs.shape
    tr, tc = _stream_tile(r, c, 3 * theirs.dtype.itemsize)

    def body(core_ref, a_ref, b_ref, o_ref):
        o_ref[...] = (a_ref[...].astype(F32) + b_ref[...].astype(F32)).astype(o_ref.dtype)

    blk = pl.BlockSpec((1, tr, tc), lambda i, j, k, core_ref: (i, j, k))
    mine = pl.BlockSpec((1, tr, tc), lambda i, j, k, core_ref: (2 * i + core_ref[0], j, k))
    return pl.pallas_call(
        body, name=name,
        grid_spec=pltpu.PrefetchScalarGridSpec(num_scalar_prefetch=1, grid=(q, r // tr, c // tc),
                                               in_specs=[mine, blk], out_specs=blk),
        out_shape=jax.ShapeDtypeStruct(theirs.shape, theirs.dtype),
        compiler_params=_params(("parallel", "parallel", "parallel")),
    )(core, blocks, theirs)


def _reduce_adamw(parts, w, m, v, name, own=None, own_slot=None):
    r, c = w.shape
    n_parts = parts.shape[0]
    tr, tc = _stream_tile(r, c, (n_parts + 1) * parts.dtype.itemsize + 7 * 4)
    c1 = 1.0 - ADAM_B1
    c2 = 1.0 - ADAM_B2
    bc1 = 1.0 - ADAM_B1 ** ADAM_STEP
    bc2 = 1.0 - ADAM_B2 ** ADAM_STEP

    def body(*refs):
        if own is None:
            p_ref, w_ref, m_ref, v_ref, g_ref, d_ref, nm_ref, nv_ref = refs
            g = p_ref[0].astype(F32)
            first = 1
        else:
            _, o_ref, p_ref, w_ref, m_ref, v_ref, g_ref, d_ref, nm_ref, nv_ref = refs
            g = o_ref[0].astype(F32)
            first = 0
        for j in range(first, n_parts):
            g = g + p_ref[j].astype(F32)
        nm = ADAM_B1 * m_ref[...] + c1 * g
        nv = ADAM_B2 * v_ref[...] + c2 * (g * g)
        g_ref[...] = g
        nm_ref[...] = nm
        nv_ref[...] = nv
        d_ref[...] = -ADAM_LR * ((nm / bc1) / (jnp.sqrt(nv / bc2) + ADAM_EPS) + ADAM_WD * w_ref[...])

    out = jax.ShapeDtypeStruct((r, c), F32)
    grid = (r // tr, c // tc)
    if own is None:
        blk = pl.BlockSpec((tr, tc), lambda i, j: (i, j))
        return pl.pallas_call(
            body, name=name, grid=grid,
            in_specs=[pl.BlockSpec((n_parts, tr, tc), lambda i, j: (0, i, j)), blk, blk, blk],
            out_specs=[blk] * 4, out_shape=[out] * 4,
            compiler_params=_params(("parallel", "parallel")),
        )(parts, w, m, v)
    blk = pl.BlockSpec((tr, tc), lambda i, j, slot_ref: (i, j))
    return pl.pallas_call(
        body, name=name,
        grid_spec=pltpu.PrefetchScalarGridSpec(
            num_scalar_prefetch=1, grid=grid,
            in_specs=[pl.BlockSpec((1, tr, tc), lambda i, j, slot_ref: (slot_ref[0], i, j)),
                      pl.BlockSpec((n_parts, tr, tc), lambda i, j, slot_ref: (0, i, j)), blk, blk, blk],
            out_specs=[blk] * 4),
        out_shape=[out] * 4,
        compiler_params=_params(("parallel", "parallel")),
    )(own_slot, own, parts, w, m, v)


_MIX = ("w_in", "w_uq", "w_ukv", "conv_w", "w_out")
_FFN = ("w_gate", "w_up", "w_down")
_BIG = _MIX + _FFN
_TRANSPOSED = ("w_in", "w_uq", "w_gate", "w_up")
_SMALL = ("pre_mix_norm", "q_norm", "kv_norm", "conv_b", "conv_ln_g", "conv_ln_b", "conv_out_norm",
          "attn_out_norm", "post_mix_norm", "pre_ffn_norm", "post_ffn_norm")
_ORDER = ("pre_mix_norm", "w_in", "q_norm", "w_uq", "kv_norm", "w_ukv", "conv_w", "conv_b", "conv_ln_g",
          "conv_ln_b", "conv_out_norm", "attn_out_norm", "w_out", "post_mix_norm", "pre_ffn_norm", "w_gate",
          "w_up", "w_down", "post_ffn_norm")


def _cols_from_shards(g):
    return jnp.transpose(g, (1, 0, 2)).reshape(g.shape[1], N_DEV * g.shape[2])


def _cols_to_shards(w):
    k, n8 = w.shape
    return jnp.transpose(w.reshape(k, N_DEV, n8 // N_DEV), (1, 0, 2))


def _step(x, positions, loss_target, w, m, v):
    s, d = x.shape[1], x.shape[2]
    x2, tgt = x[0], loss_target[0]
    pos = positions.reshape(s, 1)
    vecs = {n: w[n] for n in _SMALL}
    core = lax.axis_index("c").astype(jnp.int32).reshape(1)
    my_chip = (2 * lax.axis_index("x") + lax.axis_index("y")).astype(jnp.int32).reshape(1)
    n_in_cols = N_DEV * w["w_in"].shape[2]
    gathers, scatters, to_sibling = {}, {}, []

    def shard(t, n):
        return t[n][0].T if n in _TRANSPOSED else t[n][0]

    def gather_start(names, tag, after, zero=0.0):
        srcs = [w[n][0] if n == "conv_w" else (shard(w, n) + zero).astype(BF16) for n in names]
        lands = [lax.empty((N_DEV,) + a.shape, a.dtype) for a in srcs]
        gathers[tag], zero = _split_start("gather_" + tag + "_start", _gather_copies, srcs, lands, after)
        return zero

    def gather_finish(names, tag, after):
        srcs, lands = _split_wait("gather_" + tag + "_wait", 4, gathers[tag], after)
        lands = _gather_pass_on(lands, "gather_" + tag + "_pass_on")
        me = _my_index()
        return {n: lax.dynamic_update_slice(g, a[None], (me,) + (0,) * a.ndim) for n, g, a in zip(names, lands, srcs)}

    def in_weights_fn(h):
        w_in_g = gather_finish(("w_in",), "in", h)["w_in"]
        zero = gather_start(("w_gate", "w_up"), "up", w_in_g)
        zero = gather_start(("w_down",), "down", w_in_g, zero)
        return jnp.pad(w_in_g.reshape(-1, d), ((0, LANES - QK_ROPE), (0, 0))), zero

    def mix_weights_fn(z):
        gath = gather_finish(_MIX[1:], "mix", z)
        w_uq_t = jnp.pad(gath["w_uq"], ((0, 0), (0, HEAD_PAD - QK_HEAD), (0, 0))).reshape(N_HEADS * HEAD_PAD, -1)
        return (w_uq_t, _cols_from_shards(gath["w_ukv"]), _cols_from_shards(gath["conv_w"]),
                gath["w_out"].reshape(-1, d))

    def up_weights_fn(mix):
        gath = gather_finish(("w_gate", "w_up"), "up", mix)
        return gath["w_gate"], gath["w_up"]

    def down_weights_fn(act):
        return gather_finish(("w_down",), "down", act)["w_down"]

    def scatter_advance(after):
        if not to_sibling:
            return 0.0
        names, tag, sent = to_sibling.pop()
        blocks, theirs = _split_wait("to_sibling_" + tag + "_wait", 4, sent, after)
        pairs = [_pair_sum(core, b, t, "pair_sum_" + n) for n, b, t in zip(names, blocks, theirs)]
        lands = [lax.empty((3,) + p.shape[1:], p.dtype) for p in pairs]
        scatters[tag], zero = _split_start("scatter_" + tag + "_start", _chip_copies, pairs, lands, theirs[0])
        return zero

    def grads_fn(tag, names, blocks, after):
        zero = scatter_advance(after)
        lands = [lax.empty((4,) + b.shape[1:], b.dtype) for b in blocks]
        sent, zero2 = _split_start("to_sibling_" + tag + "_start", _sibling_copies, blocks, lands, after)
        to_sibling.append((names, tag, sent))
        return zero + zero2

    zero = gather_start(("w_in",), "in", x2)
    vecs["pre_mix_norm"] = vecs["pre_mix_norm"] + gather_start(_MIX[1:], "mix", x2, zero)
    loss, grad_x, g = _local_step(x2, pos, tgt, vecs, in_weights_fn, mix_weights_fn, up_weights_fn,
                                  down_weights_fn, grads_fn)

    last = ("w_in", "w_uq", "w_ukv", "conv_w")
    last_blocks = [g["w_in_t"][:n_in_cols].reshape(N_DEV, -1, d),
                   g["w_uq_t"].reshape(N_HEADS, HEAD_PAD, -1)[:, :QK_HEAD],
                   _cols_to_shards(g["w_ukv"]), _cols_to_shards(g["conv_w"])]
    small_all = _gather_direct([jnp.concatenate([g[n] for n in _SMALL], axis=1)], "gather_small_grads")[0]
    zero = grads_fn("in", last, last_blocks, small_all) + scatter_advance(grad_x)
    small_all = small_all + zero

    res = {}
    after = small_all
    for tag, names in (("down", ("w_down",)), ("up", ("w_gate", "w_up")), ("out", ("w_out",)), ("in", last)):
        pairs, recv = _split_wait("scatter_" + tag + "_wait", 3, scatters[tag], after)
        for n, own, parts in zip(names, pairs, recv):
            res[n] = _reduce_adamw(parts, shard(w, n), shard(m, n), shard(v, n), "adamw_" + n, own=own,
                                   own_slot=my_chip)
            after = res[n][1]
            res[n] = [(t.T if n in _TRANSPOSED else t)[None] for t in res[n]]
    cat = lambda t: jnp.concatenate([t[n] for n in _SMALL], axis=1)
    sg, sd, sm, sv = _reduce_adamw(small_all, cat(w), cat(m), cat(v), "adamw_small")
    off = 0
    for n in _SMALL:
        width = w[n].shape[1]
        res[n] = [t[:, off:off + width] for t in (sg, sd, sm, sv)]
        off += width

    total = lax.psum(loss[0, 0], ("x", "y", "c"))
    outs = [total, grad_x[None]]
    for part in range(4):
        outs.extend(res[n][part] for n in _ORDER)
    return tuple(outs)


def kernel(x, positions, pre_mix_norm, w_in, q_norm, w_uq, kv_norm, w_ukv, conv_w, conv_b, conv_ln_g, conv_ln_b, conv_out_norm, attn_out_norm, w_out, post_mix_norm, pre_ffn_norm, w_gate, w_up, w_down, post_ffn_norm, loss_target, m_pre_mix_norm, m_w_in, m_q_norm, m_w_uq, m_kv_norm, m_w_ukv, m_conv_w, m_conv_b, m_conv_ln_g, m_conv_ln_b, m_conv_out_norm, m_attn_out_norm, m_w_out, m_post_mix_norm, m_pre_ffn_norm, m_w_gate, m_w_up, m_w_down, m_post_ffn_norm, v_pre_mix_norm, v_w_in, v_q_norm, v_w_uq, v_kv_norm, v_w_ukv, v_conv_w, v_conv_b, v_conv_ln_g, v_conv_ln_b, v_conv_out_norm, v_attn_out_norm, v_w_out, v_post_mix_norm, v_pre_ffn_norm, v_w_gate, v_w_up, v_w_down, v_post_ffn_norm):
    w = dict(zip(_ORDER, (pre_mix_norm, w_in, q_norm, w_uq, kv_norm, w_ukv, conv_w, conv_b, conv_ln_g, conv_ln_b,
                          conv_out_norm, attn_out_norm, w_out, post_mix_norm, pre_ffn_norm, w_gate, w_up, w_down,
                          post_ffn_norm)))
    m = dict(zip(_ORDER, (m_pre_mix_norm, m_w_in, m_q_norm, m_w_uq, m_kv_norm, m_w_ukv, m_conv_w, m_conv_b,
                          m_conv_ln_g, m_conv_ln_b, m_conv_out_norm, m_attn_out_norm, m_w_out, m_post_mix_norm,
                          m_pre_ffn_norm, m_w_gate, m_w_up, m_w_down, m_post_ffn_norm)))
    v = dict(zip(_ORDER, (v_pre_mix_norm, v_w_in, v_q_norm, v_w_uq, v_kv_norm, v_w_ukv, v_conv_w, v_conv_b,
                          v_conv_ln_g, v_conv_ln_b, v_conv_out_norm, v_attn_out_norm, v_w_out, v_post_mix_norm,
                          v_pre_ffn_norm, v_w_gate, v_w_up, v_w_down, v_post_ffn_norm)))
    return _step(x, positions, loss_target, w, m, v)
```

```python
import functools

import jax
import jax.numpy as jnp
from jax import lax
from jax.experimental import pallas as pl
from jax.experimental.pallas import tpu as pltpu

N_DEV = 8
N_HEADS = 8
QK_NOPE = 128
QK_ROPE = 64
V_HEAD = 128
QK_HEAD = QK_NOPE + QK_ROPE
HEAD_PAD = 256
LANES = 128
ATTN_BLOCK = 512
CONV_K = 31
CONV_PAD = 32
EPS = 1e-6
ROPE_THETA = 10000.0
ADAM_LR = 0.001
ADAM_B1 = 0.9
ADAM_B2 = 0.999
ADAM_EPS = 1e-08
ADAM_WD = 0.01
ADAM_STEP = 10
VMEM_LIMIT = 56 * 1024 * 1024
F32 = jnp.float32
BF16 = jnp.bfloat16
MESH = pl.DeviceIdType.MESH
NEG = -1e30


def _pick(n, prefs):
    for p in prefs:
        if p <= n and n % p == 0:
            return p
    return n


def _params(sem):
    return pltpu.CompilerParams(dimension_semantics=sem, vmem_limit_bytes=VMEM_LIMIT)


_DIMS = {"nn": (((1,), (0,)), ((), ())), "nt": (((1,), (1,)), ((), ())), "tn": (((0,), (0,)), ((), ()))}


MM_VMEM_BUDGET = 40 * 1024 * 1024
MM_MAX_MACS = 3 * 1024 ** 3


V7X_HBM_BYTES_PER_S = 3.0e12
V7X_MXU_MACS_PER_S = 0.45e15
GRID_STEP_S = 0.35e-6


def _mm_tiles(m, n, k, size_a, size_b, size_o):
    best = None
    for tm in sorted({m, 1024, 512, 256, 128}, reverse=True):
        if tm > m or m % tm:
            continue
        for tn in sorted({n, 2048, 1024, 512, 384, 256, 128}, reverse=True):
            if tn > n or n % tn:
                continue
            vmem = 2 * (tm * k * size_a + k * tn * size_b + tm * tn * size_o)
            if vmem > MM_VMEM_BUDGET or tm * tn * k > MM_MAX_MACS:
                continue
            b_reads = 1 if tn == n else m // tm
            traffic = m * k * size_a + b_reads * k * n * size_b + m * n * size_o
            exposed = tm * k * size_a + k * tn * size_b + tm * tn * size_o
            steps = (m // tm) * (n // tn)
            key = (max(traffic / V7X_HBM_BYTES_PER_S, m * n * k / V7X_MXU_MACS_PER_S)
                   + exposed / V7X_HBM_BYTES_PER_S + steps * GRID_STEP_S)
            if best is None or key < best[0]:
                best = (key, tm, tn)
    assert best is not None, (m, n, k)
    return best[1], best[2]


def _mm(a, b, mode, name, out_dtype=F32, add=None):
    if mode == "nn":
        (m, k), (k2, n) = a.shape, b.shape
    elif mode == "nt":
        (m, k), (n, k2) = a.shape, b.shape
    else:
        (k, m), (k2, n) = a.shape, b.shape
    assert k == k2, (a.shape, b.shape, mode)
    tm, tn = _mm_tiles(m, n, k, a.dtype.itemsize, b.dtype.itemsize,
                       jnp.dtype(out_dtype).itemsize + (0 if add is None else 4))
    dims = _DIMS[mode]

    def body(a_ref, b_ref, *rest):
        acc = lax.dot_general(a_ref[...].astype(BF16), b_ref[...].astype(BF16), dims, preferred_element_type=F32)
        if add is not None:
            acc = acc + rest[0][...]
        rest[-1][...] = acc.astype(rest[-1].dtype)

    if mode == "tn":
        a_spec = pl.BlockSpec((k, tm), lambda i, j: (0, i))
    else:
        a_spec = pl.BlockSpec((tm, k), lambda i, j: (i, 0))
    if mode == "nt":
        b_spec = pl.BlockSpec((tn, k), lambda i, j: (j, 0))
    else:
        b_spec = pl.BlockSpec((k, tn), lambda i, j: (0, j))
    o_spec = pl.BlockSpec((tm, tn), lambda i, j: (i, j))
    extra = [] if add is None else [add]
    return pl.pallas_call(
        body, name=name,
        grid=(m // tm, n // tn),
        in_specs=[a_spec, b_spec] + [o_spec] * len(extra),
        out_specs=o_spec,
        out_shape=jax.ShapeDtypeStruct((m, n), out_dtype),
        compiler_params=_params(("parallel", "parallel")),
    )(a, b, *extra)


def _sigmoid(x):
    return 1.0 / (1.0 + jnp.exp(-x))


def _rms(x, g):
    r = lax.rsqrt(jnp.mean(x * x, axis=-1, keepdims=True) + EPS)
    return (x * r) * g


def _rms_bwd(x, g, dy):
    r = lax.rsqrt(jnp.mean(x * x, axis=-1, keepdims=True) + EPS)
    xh = x * r
    dyg = dy * g
    dx = r * (dyg - xh * jnp.mean(dyg * xh, axis=-1, keepdims=True))
    return dx, jnp.sum(dy * xh, axis=0, keepdims=True)


def _ln(x, g, b):
    mu = jnp.mean(x, axis=-1, keepdims=True)
    xc = x - mu
    rs = lax.rsqrt(jnp.mean(xc * xc, axis=-1, keepdims=True) + EPS)
    return (xc * rs) * g + b


def _ln_bwd(x, g, dy):
    mu = jnp.mean(x, axis=-1, keepdims=True)
    xc = x - mu
    rs = lax.rsqrt(jnp.mean(xc * xc, axis=-1, keepdims=True) + EPS)
    xh = xc * rs
    dyg = dy * g
    dx = rs * (dyg - jnp.mean(dyg, axis=-1, keepdims=True) - xh * jnp.mean(dyg * xh, axis=-1, keepdims=True))
    return dx, jnp.sum(dy * xh, axis=0, keepdims=True), jnp.sum(dy, axis=0, keepdims=True)


def _silu(x):
    return x * _sigmoid(x)


def _silu_grad(x):
    s = _sigmoid(x)
    return s * (1.0 + x * (1.0 - s))


def _rope(x, cos, sa, sb):
    return x * cos + pltpu.roll(x, 96, 1) * sa + pltpu.roll(x, 32, 1) * sb


def _rope_t(d, cos, sa, sb):
    return d * cos - pltpu.roll(d, 96, 1) * sa - pltpu.roll(d, 32, 1) * sb


def _rows(ts, w):
    return pl.BlockSpec((ts, w), lambda i: (i, 0))


def _vec(w):
    return pl.BlockSpec((1, w), lambda i: (0, 0))


def _acc_init(i, *refs):
    @pl.when(i == 0)
    def _():
        for r in refs:
            r[...] = jnp.zeros_like(r)


def _rope_tables(pos, inv_freq):
    s = pos.shape[0]
    ts = _pick(s, (512, 256, 128))

    def body(p_ref, f_ref, c_ref, sa_ref, sb_ref):
        ang = p_ref[...].astype(F32) * f_ref[...]
        lane = lax.broadcasted_iota(jnp.int32, ang.shape, 1)
        c, sn = jnp.cos(ang), jnp.sin(ang)
        c_ref[...] = jnp.where(lane < QK_ROPE, c, 0.0)
        sa_ref[...] = jnp.where(lane < QK_ROPE // 2, -sn, 0.0)
        sb_ref[...] = jnp.where((lane >= QK_ROPE // 2) & (lane < QK_ROPE), sn, 0.0)

    out = jax.ShapeDtypeStruct((s, LANES), F32)
    return pl.pallas_call(
        body, name="rope_tables", grid=(s // ts,),
        in_specs=[_rows(ts, 1), _vec(LANES)],
        out_specs=[_rows(ts, LANES)] * 3, out_shape=[out] * 3,
        compiler_params=_params(("parallel",)),
    )(pos, inv_freq)


def _pre_fwd(x, g):
    s, d = x.shape
    ts = _pick(s, (256, 128))

    def body(x_ref, g_ref, h_ref):
        h_ref[...] = _rms(x_ref[...], g_ref[...]).astype(BF16)

    return pl.pallas_call(
        body, name="pre_fwd", grid=(s // ts,),
        in_specs=[_rows(ts, d), _vec(d)], out_specs=_rows(ts, d),
        out_shape=jax.ShapeDtypeStruct((s, d), BF16),
        compiler_params=_params(("parallel",)),
    )(x, g)


def _split_fwd(z, gq, gkv, tabs, c, ql, kvl):
    s, zw = z.shape
    ts = _pick(s, (256, 128))
    o_q, o_kv, o_kr = 2 * c, 2 * c + ql, 2 * c + ql + kvl

    def body(z_ref, gq_ref, gkv_ref, c_ref, sa_ref, sb_ref, u0_ref, qn_ref, kvn_ref, kpe_ref):
        u0_ref[...] = z_ref[:, 0:c] * _sigmoid(z_ref[:, c:2 * c])
        qn_ref[...] = _rms(z_ref[:, o_q:o_kv], gq_ref[...]).astype(BF16)
        kvn_ref[...] = _rms(z_ref[:, o_kv:o_kr], gkv_ref[...]).astype(BF16)
        kpe_ref[...] = _rope(z_ref[:, o_kr:o_kr + LANES], c_ref[...], sa_ref[...], sb_ref[...]).astype(BF16)

    return pl.pallas_call(
        body, name="split_fwd", grid=(s // ts,),
        in_specs=[_rows(ts, zw), _vec(ql), _vec(kvl)] + [_rows(ts, LANES)] * 3,
        out_specs=[_rows(ts, c), _rows(ts, ql), _rows(ts, kvl), _rows(ts, LANES)],
        out_shape=[jax.ShapeDtypeStruct((s, c), F32), jax.ShapeDtypeStruct((s, ql), BF16),
                   jax.ShapeDtypeStruct((s, kvl), BF16), jax.ShapeDtypeStruct((s, LANES), BF16)],
        compiler_params=_params(("parallel",)),
    )(z, gq, gkv, *tabs)


def _split_bwd(du0, z, dqn, dkvn, dkpe_h, gq, gkv, tabs, c, ql, kvl):
    s, zw = z.shape
    ts = _pick(s, (256, 128))
    o_q, o_kv, o_kr = 2 * c, 2 * c + ql, 2 * c + ql + kvl

    def body(du0_ref, z_ref, dqn_ref, dkvn_ref, dkh_ref, gq_ref, gkv_ref, c_ref, sa_ref, sb_ref,
             dz_ref, dgq_ref, dgkv_ref):
        _acc_init(pl.program_id(0), dgq_ref, dgkv_ref)
        du0 = du0_ref[...]
        a = z_ref[:, 0:c]
        sg = _sigmoid(z_ref[:, c:2 * c])
        dz_ref[:, 0:c] = (du0 * sg).astype(BF16)
        dz_ref[:, c:2 * c] = (du0 * a * sg * (1.0 - sg)).astype(BF16)
        dq, dgq = _rms_bwd(z_ref[:, o_q:o_kv], gq_ref[...], dqn_ref[...])
        dz_ref[:, o_q:o_kv] = dq.astype(BF16)
        dgq_ref[...] += dgq
        dkv, dgkv = _rms_bwd(z_ref[:, o_kv:o_kr], gkv_ref[...], dkvn_ref[...])
        dz_ref[:, o_kv:o_kr] = dkv.astype(BF16)
        dgkv_ref[...] += dgkv
        dk = dkh_ref[:, 0:LANES]
        for h in range(1, N_HEADS):
            dk = dk + dkh_ref[:, h * LANES:(h + 1) * LANES]
        dz_ref[:, o_kr:o_kr + LANES] = _rope_t(dk, c_ref[...], sa_ref[...], sb_ref[...]).astype(BF16)

    return pl.pallas_call(
        body, name="split_bwd", grid=(s // ts,),
        in_specs=[_rows(ts, c), _rows(ts, zw), _rows(ts, ql), _rows(ts, kvl), _rows(ts, N_HEADS * LANES),
                  _vec(ql), _vec(kvl)] + [_rows(ts, LANES)] * 3,
        out_specs=[_rows(ts, zw), _vec(ql), _vec(kvl)],
        out_shape=[jax.ShapeDtypeStruct((s, zw), BF16), jax.ShapeDtypeStruct((1, ql), F32),
                   jax.ShapeDtypeStruct((1, kvl), F32)],
        compiler_params=_params(("arbitrary",)),
    )(du0, z, dqn, dkvn, dkpe_h, gq, gkv, *tabs)


def _q_rope(qpre, tabs, transpose, out_dtype, name):
    s, w = qpre.shape
    ts = _pick(s, (256, 128))
    rot = _rope_t if transpose else _rope

    def body(q_ref, c_ref, sa_ref, sb_ref, o_ref):
        cs, sa, sb = c_ref[...], sa_ref[...], sb_ref[...]
        for h in range(N_HEADS):
            lo = h * HEAD_PAD
            o_ref[:, lo:lo + QK_NOPE] = q_ref[:, lo:lo + QK_NOPE].astype(out_dtype)
            o_ref[:, lo + QK_NOPE:lo + HEAD_PAD] = rot(q_ref[:, lo + QK_NOPE:lo + HEAD_PAD], cs, sa, sb).astype(out_dtype)

    return pl.pallas_call(
        body, name=name, grid=(s // ts,),
        in_specs=[_rows(ts, w)] + [_rows(ts, LANES)] * 3, out_specs=_rows(ts, w),
        out_shape=jax.ShapeDtypeStruct((s, w), out_dtype),
        compiler_params=_params(("parallel",)),
    )(qpre, *tabs)


def _conv_fwd(u0, w, b):
    s, c = u0.shape
    tc = LANES
    rc = _pick(s, (256, 128))

    def body(u_ref, w_ref, b_ref, o_ref, pad_ref):
        pad_ref[0:CONV_PAD, :] = jnp.zeros((CONV_PAD, tc), F32)
        pad_ref[CONV_PAD:CONV_PAD + s, :] = u_ref[...]
        for r in range(s // rc):
            acc = jnp.broadcast_to(b_ref[...], (rc, tc))
            for k in range(CONV_K):
                lo = r * rc + CONV_PAD - (CONV_K - 1) + k
                acc = acc + w_ref[k:k + 1, :] * pad_ref[lo:lo + rc, :]
            o_ref[r * rc:(r + 1) * rc, :] = acc

    col = lambda j: (0, j)
    return pl.pallas_call(
        body, name="conv_fwd", grid=(c // tc,),
        in_specs=[pl.BlockSpec((s, tc), col), pl.BlockSpec((CONV_K, tc), col), pl.BlockSpec((1, tc), col)],
        out_specs=pl.BlockSpec((s, tc), col),
        out_shape=jax.ShapeDtypeStruct((s, c), F32),
        scratch_shapes=[pltpu.VMEM((s + CONV_PAD, tc), F32)],
        compiler_params=_params(("parallel",)),
    )(u0, w, b)


def _conv_bwd(du1, u0, w):
    s, c = u0.shape
    tc = LANES
    rc = _pick(s, (256, 128))

    def body(d_ref, u_ref, w_ref, du_ref, dw_ref, db_ref, upad_ref, dpad_ref):
        upad_ref[0:CONV_PAD, :] = jnp.zeros((CONV_PAD, tc), F32)
        upad_ref[CONV_PAD:CONV_PAD + s, :] = u_ref[...]
        dpad_ref[0:s, :] = d_ref[...]
        dpad_ref[s:s + CONV_PAD, :] = jnp.zeros((CONV_PAD, tc), F32)
        for r in range(s // rc):
            acc = jnp.zeros((rc, tc), F32)
            for k in range(CONV_K):
                lo = r * rc + (CONV_K - 1) - k
                acc = acc + w_ref[k:k + 1, :] * dpad_ref[lo:lo + rc, :]
            du_ref[r * rc:(r + 1) * rc, :] = acc
        for k in range(CONV_K):
            acc8 = jnp.zeros((8, tc), F32)
            for r in range(s // rc):
                lo = r * rc + CONV_PAD - (CONV_K - 1) + k
                prod = d_ref[r * rc:(r + 1) * rc, :] * upad_ref[lo:lo + rc, :]
                acc8 = acc8 + jnp.sum(prod.reshape(rc // 8, 8, tc), axis=0)
            dw_ref[k:k + 1, :] = jnp.sum(acc8, axis=0, keepdims=True)
        db_ref[...] = jnp.sum(d_ref[...], axis=0, keepdims=True)

    col = lambda j: (0, j)
    return pl.pallas_call(
        body, name="conv_bwd", grid=(c // tc,),
        in_specs=[pl.BlockSpec((s, tc), col), pl.BlockSpec((s, tc), col), pl.BlockSpec((CONV_K, tc), col)],
        out_specs=[pl.BlockSpec((s, tc), col), pl.BlockSpec((CONV_K, tc), col), pl.BlockSpec((1, tc), col)],
        out_shape=[jax.ShapeDtypeStruct((s, c), F32), jax.ShapeDtypeStruct((CONV_K, c), F32),
                   jax.ShapeDtypeStruct((1, c), F32)],
        scratch_shapes=[pltpu.VMEM((s + CONV_PAD, tc), F32), pltpu.VMEM((s + CONV_PAD, tc), F32)],
        compiler_params=_params(("parallel",)),
    )(du1, u0, w)


def _causal_mask(sc, qi, kj, tq, tk):
    rows = qi * tq + lax.broadcasted_iota(jnp.int32, sc.shape, 0)
    cols = kj * tk + lax.broadcasted_iota(jnp.int32, sc.shape, 1)
    return jnp.where(cols <= rows, sc, NEG)


def _attn_fwd(q, kv, kpe):
    s = q.shape[0]
    tq = tk = _pick(s, (ATTN_BLOCK, 256, 128))
    reps = tk // LANES
    scale = QK_HEAD ** -0.5
    nt = (((1,), (1,)), ((), ()))

    def body(q_ref, kn_ref, v_ref, kpe_ref, o_ref, lse_ref, kf_ref, vb_ref, m_ref, l_ref, acc_ref):
        i = pl.program_id(1)

        @pl.when(i == 0)
        def _():
            kf_ref[:, 0:QK_NOPE] = kn_ref[...].astype(BF16)
            kf_ref[:, QK_NOPE:HEAD_PAD] = kpe_ref[...]
            vb_ref[...] = v_ref[...].astype(BF16)

        qb = q_ref[...]
        m_ref[...] = jnp.full((tq, LANES), NEG, F32)
        l_ref[...] = jnp.zeros((tq, LANES), F32)
        acc_ref[...] = jnp.zeros((tq, V_HEAD), F32)

        def block(j, diagonal):
            off = pl.multiple_of(j * tk, tk)
            sc = lax.dot_general(qb, kf_ref[pl.ds(off, tk), :], nt, preferred_element_type=F32) * scale
            if diagonal:
                sc = _causal_mask(sc, 0, 0, tq, tk)
            m_prev = m_ref[...]
            m_new = jnp.maximum(m_prev, jnp.max(sc, axis=1, keepdims=True))
            p = jnp.exp(sc - jnp.tile(m_new, (1, reps)))
            alpha = jnp.exp(m_prev - m_new)
            l_ref[...] = alpha * l_ref[...] + jnp.sum(p, axis=1, keepdims=True)
            acc_ref[...] = alpha * acc_ref[...] + jnp.dot(p.astype(BF16), vb_ref[pl.ds(off, tk), :],
                                                          preferred_element_type=F32)
            m_ref[...] = m_new

        def below_diagonal(j, carry):
            block(j, False)
            return carry

        lax.fori_loop(0, i, below_diagonal, 0)
        block(i, True)
        o_ref[...] = acc_ref[...] / l_ref[...]
        lse_ref[...] = m_ref[...] + jnp.log(l_ref[...])

    return pl.pallas_call(
        body, name="attn_fwd", grid=(N_HEADS, s // tq),
        in_specs=[pl.BlockSpec((tq, HEAD_PAD), lambda h, i: (i, h)),
                  pl.BlockSpec((s, QK_NOPE), lambda h, i: (0, 2 * h)),
                  pl.BlockSpec((s, V_HEAD), lambda h, i: (0, 2 * h + 1)),
                  pl.BlockSpec((s, LANES), lambda h, i: (0, 0))],
        out_specs=[pl.BlockSpec((tq, V_HEAD), lambda h, i: (i, h)),
                   pl.BlockSpec((tq, LANES), lambda h, i: (i, h))],
        out_shape=[jax.ShapeDtypeStruct((s, N_HEADS * V_HEAD), F32),
                   jax.ShapeDtypeStruct((s, N_HEADS * LANES), F32)],
        scratch_shapes=[pltpu.VMEM((s, HEAD_PAD), BF16), pltpu.VMEM((s, V_HEAD), BF16),
                        pltpu.VMEM((tq, LANES), F32), pltpu.VMEM((tq, LANES), F32), pltpu.VMEM((tq, V_HEAD), F32)],
        compiler_params=_params(("parallel", "arbitrary")),
    )(q, kv, kv, kpe)


def _attn_bwd(q, kv, kpe, o, do, lse):
    s = q.shape[0]
    tq = tk = _pick(s, (ATTN_BLOCK, 256, 128))
    nq = s // tq
    reps = tk // LANES
    scale = QK_HEAD ** -0.5
    nt = (((1,), (1,)), ((), ()))
    tn = (((0,), (0,)), ((), ()))

    def body(q_ref, kn_ref, v_ref, kpe_ref, o_ref, do_ref, lse_ref, dq_ref, dkv_ref, dkpe_ref,
             kf_ref, vb_ref, dk_ref, dv_ref):
        j = pl.program_id(1)

        @pl.when(j == 0)
        def _():
            dq_ref[...] = jnp.zeros_like(dq_ref)

        kf_ref[:, 0:QK_NOPE] = kn_ref[...].astype(BF16)
        kf_ref[:, QK_NOPE:HEAD_PAD] = kpe_ref[...]
        vb_ref[...] = v_ref[...].astype(BF16)
        dk_ref[...] = jnp.zeros_like(dk_ref)
        dv_ref[...] = jnp.zeros_like(dv_ref)

        def block(i, diagonal):
            off = pl.multiple_of(i * tq, tq)
            qb = q_ref[pl.ds(off, tq), :]
            dob = do_ref[pl.ds(off, tq), :]
            delta = jnp.sum(dob * o_ref[pl.ds(off, tq), :], axis=1, keepdims=True)
            sc = lax.dot_general(qb, kf_ref[...], nt, preferred_element_type=F32) * scale
            if diagonal:
                sc = _causal_mask(sc, 0, 0, tq, tk)
            p = jnp.exp(sc - jnp.tile(lse_ref[pl.ds(off, tq), :], (1, reps)))
            dob16 = dob.astype(BF16)
            dv_ref[...] += lax.dot_general(p.astype(BF16), dob16, tn, preferred_element_type=F32)
            dp = lax.dot_general(dob16, vb_ref[...], nt, preferred_element_type=F32)
            ds = (p * (dp - delta) * scale).astype(BF16)
            dq_ref[pl.ds(off, tq), :] += jnp.dot(ds, kf_ref[...], preferred_element_type=F32)
            dk_ref[...] += lax.dot_general(ds, qb, tn, preferred_element_type=F32)

        def above_diagonal(i, carry):
            block(i, False)
            return carry

        block(j, True)
        lax.fori_loop(j + 1, nq, above_diagonal, 0)
        dkv_ref[:, 0:QK_NOPE] = dk_ref[:, 0:QK_NOPE]
        dkv_ref[:, QK_NOPE:HEAD_PAD] = dv_ref[...]
        dkpe_ref[...] = dk_ref[:, QK_NOPE:HEAD_PAD]

    head_rows = lambda w: pl.BlockSpec((s, w), lambda h, j: (0, h))
    return pl.pallas_call(
        body, name="attn_bwd", grid=(N_HEADS, s // tk),
        in_specs=[head_rows(HEAD_PAD),
                  pl.BlockSpec((tk, QK_NOPE), lambda h, j: (j, 2 * h)),
                  pl.BlockSpec((tk, V_HEAD), lambda h, j: (j, 2 * h + 1)),
                  pl.BlockSpec((tk, LANES), lambda h, j: (j, 0)),
                  head_rows(V_HEAD), head_rows(V_HEAD), head_rows(LANES)],
        out_specs=[head_rows(HEAD_PAD),
                   pl.BlockSpec((tk, HEAD_PAD), lambda h, j: (j, h)),
                   pl.BlockSpec((tk, LANES), lambda h, j: (j, h))],
        out_shape=[jax.ShapeDtypeStruct((s, N_HEADS * HEAD_PAD), F32),
                   jax.ShapeDtypeStruct((s, N_HEADS * HEAD_PAD), F32),
                   jax.ShapeDtypeStruct((s, N_HEADS * LANES), F32)],
        scratch_shapes=[pltpu.VMEM((tk, HEAD_PAD), BF16), pltpu.VMEM((tk, V_HEAD), BF16),
                        pltpu.VMEM((tk, HEAD_PAD), F32), pltpu.VMEM((tk, V_HEAD), F32)],
        compiler_params=_params(("parallel", "arbitrary")),
    )(q, kv, kv, kpe, o, do, lse)


def _mix_fwd(u1, lng, lnb, gcon, attn, gattn):
    s, c = u1.shape
    ac = attn.shape[1]
    ts = _pick(s, (256, 128))

    def body(u_ref, lg_ref, lb_ref, gc_ref, a_ref, ga_ref, o_ref):
        t3 = _silu(_ln(u_ref[...], lg_ref[...], lb_ref[...]))
        o_ref[:, 0:c] = _rms(t3, gc_ref[...]).astype(BF16)
        o_ref[:, c:c + ac] = _rms(a_ref[...], ga_ref[...]).astype(BF16)

    return pl.pallas_call(
        body, name="mix_fwd", grid=(s // ts,),
        in_specs=[_rows(ts, c), _vec(c), _vec(c), _vec(c), _rows(ts, ac), _vec(ac)],
        out_specs=_rows(ts, c + ac),
        out_shape=jax.ShapeDtypeStruct((s, c + ac), BF16),
        compiler_params=_params(("parallel",)),
    )(u1, lng, lnb, gcon, attn, gattn)


def _mix_bwd(dmixin, u1, lng, lnb, gcon, attn, gattn):
    s, c = u1.shape
    ac = attn.shape[1]
    ts = _pick(s, (256, 128))

    def body(d_ref, u_ref, lg_ref, lb_ref, gc_ref, a_ref, ga_ref,
             du_ref, da_ref, dlg_ref, dlb_ref, dgc_ref, dga_ref):
        _acc_init(pl.program_id(0), dlg_ref, dlb_ref, dgc_ref, dga_ref)
        u = u_ref[...]
        t2 = _ln(u, lg_ref[...], lb_ref[...])
        dt3, dgc = _rms_bwd(_silu(t2), gc_ref[...], d_ref[:, 0:c])
        du, dlg, dlb = _ln_bwd(u, lg_ref[...], dt3 * _silu_grad(t2))
        du_ref[...] = du
        dlg_ref[...] += dlg
        dlb_ref[...] += dlb
        dgc_ref[...] += dgc
        da, dga = _rms_bwd(a_ref[...], ga_ref[...], d_ref[:, c:c + ac])
        da_ref[...] = da
        dga_ref[...] += dga

    return pl.pallas_call(
        body, name="mix_bwd", grid=(s // ts,),
        in_specs=[_rows(ts, c + ac), _rows(ts, c), _vec(c), _vec(c), _vec(c), _rows(ts, ac), _vec(ac)],
        out_specs=[_rows(ts, c), _rows(ts, ac), _vec(c), _vec(c), _vec(c), _vec(ac)],
        out_shape=[jax.ShapeDtypeStruct((s, c), F32), jax.ShapeDtypeStruct((s, ac), F32),
                   jax.ShapeDtypeStruct((1, c), F32), jax.ShapeDtypeStruct((1, c), F32),
                   jax.ShapeDtypeStruct((1, c), F32), jax.ShapeDtypeStruct((1, ac), F32)],
        compiler_params=_params(("arbitrary",)),
    )(dmixin, u1, lng, lnb, gcon, attn, gattn)


def _post_mix_fwd(x, mix, gpost, gpre):
    s, d = x.shape
    ts = _pick(s, (256, 128))

    def body(x_ref, m_ref, gp_ref, gf_ref, x1_ref, hf_ref):
        x1 = x_ref[...] + _rms(m_ref[...], gp_ref[...])
        x1_ref[...] = x1
        hf_ref[...] = _rms(x1, gf_ref[...]).astype(BF16)

    return pl.pallas_call(
        body, name="post_mix_fwd", grid=(s // ts,),
        in_specs=[_rows(ts, d), _rows(ts, d), _vec(d), _vec(d)],
        out_specs=[_rows(ts, d), _rows(ts, d)],
        out_shape=[jax.ShapeDtypeStruct((s, d), F32), jax.ShapeDtypeStruct((s, d), BF16)],
        compiler_params=_params(("parallel",)),
    )(x, mix, gpost, gpre)


def _post_mix_bwd(dy, dhf, x1, gpre, mix, gpost):
    s, d = x1.shape
    ts = _pick(s, (256, 128))

    def body(dy_ref, dh_ref, x1_ref, gf_ref, m_ref, gp_ref, dx1_ref, dm_ref, dgf_ref, dgp_ref):
        _acc_init(pl.program_id(0), dgf_ref, dgp_ref)
        dxa, dgf = _rms_bwd(x1_ref[...], gf_ref[...], dh_ref[...])
        dx1 = dy_ref[...] + dxa
        dx1_ref[...] = dx1
        dgf_ref[...] += dgf
        dm, dgp = _rms_bwd(m_ref[...], gp_ref[...], dx1)
        dm_ref[...] = dm.astype(BF16)
        dgp_ref[...] += dgp

    return pl.pallas_call(
        body, name="post_mix_bwd", grid=(s // ts,),
        in_specs=[_rows(ts, d), _rows(ts, d), _rows(ts, d), _vec(d), _rows(ts, d), _vec(d)],
        out_specs=[_rows(ts, d), _rows(ts, d), _vec(d), _vec(d)],
        out_shape=[jax.ShapeDtypeStruct((s, d), F32), jax.ShapeDtypeStruct((s, d), BF16),
                   jax.ShapeDtypeStruct((1, d), F32), jax.ShapeDtypeStruct((1, d), F32)],
        compiler_params=_params(("arbitrary",)),
    )(dy, dhf, x1, gpre, mix, gpost)


def _ffn_up(hf, wg, wu):
    s, d = hf.shape
    nsh, fs, _ = wg.shape
    tm = _pick(s, (1024, 512, 256, 128))
    nt = (((1,), (1,)), ((), ()))

    def body(h_ref, wg_ref, wu_ref, g_ref, u_ref, a_ref):
        h = h_ref[...]
        g = lax.dot_general(h, wg_ref[...], nt, preferred_element_type=F32)
        u = lax.dot_general(h, wu_ref[...], nt, preferred_element_type=F32)
        g_ref[...] = g
        u_ref[...] = u
        a_ref[...] = (_silu(g) * u).astype(BF16)

    w_spec = pl.BlockSpec((None, fs, d), lambda i, j: (j, 0, 0))
    o_spec = pl.BlockSpec((None, tm, fs), lambda i, j: (j, i, 0))
    return pl.pallas_call(
        body, name="ffn_up", grid=(s // tm, nsh),
        in_specs=[pl.BlockSpec((tm, d), lambda i, j: (i, 0)), w_spec, w_spec],
        out_specs=[o_spec] * 3,
        out_shape=[jax.ShapeDtypeStruct((nsh, s, fs), F32)] * 2 + [jax.ShapeDtypeStruct((nsh, s, fs), BF16)],
        compiler_params=_params(("parallel", "parallel")),
    )(hf, wg, wu)


def _ffn_down(acts, ws, name):
    n = len(acts)
    nsh, s, fs = acts[0].shape
    d = ws[0].shape[2]
    tm = _pick(s, (1024, 512, 256, 128) if n == 1 else (512, 256, 128))
    tn = _pick(d, (512, 256, 128) if n == 1 else (256, 128))

    def body(*refs):
        acc = None
        for a_ref, w_ref in zip(refs[:n], refs[n:2 * n]):
            for j in range(nsh):
                part = jnp.dot(a_ref[j], w_ref[j], preferred_element_type=F32)
                acc = part if acc is None else acc + part
        refs[-1][...] = acc

    return pl.pallas_call(
        body, name=name, grid=(s // tm, d // tn),
        in_specs=[pl.BlockSpec((nsh, tm, fs), lambda i, j: (0, i, 0))] * n
        + [pl.BlockSpec((nsh, fs, tn), lambda i, j: (0, 0, j))] * n,
        out_specs=pl.BlockSpec((tm, tn), lambda i, j: (i, j)),
        out_shape=jax.ShapeDtypeStruct((s, d), F32),
        compiler_params=_params(("parallel", "parallel")),
    )(*acts, *ws)


def _ffn_down_bwd(dff, wd, gate, up, behind):
    s, d = dff.shape
    nsh, fs, _ = wd.shape
    tm = _pick(s, (1024, 512, 256, 128))
    nt = (((1,), (1,)), ((), ()))

    def body(d_ref, w_ref, g_ref, u_ref, _, dg_ref, du_ref):
        dact = lax.dot_general(d_ref[...], w_ref[...], nt, preferred_element_type=F32)
        g = g_ref[...]
        dg_ref[...] = (dact * u_ref[...] * _silu_grad(g)).astype(BF16)
        du_ref[...] = (dact * _silu(g)).astype(BF16)

    h_spec = pl.BlockSpec((None, tm, fs), lambda i, j: (j, i, 0))
    return pl.pallas_call(
        body, name="ffn_down_bwd", grid=(s // tm, nsh),
        in_specs=[pl.BlockSpec((tm, d), lambda i, j: (i, 0)),
                  pl.BlockSpec((None, fs, d), lambda i, j: (j, 0, 0)), h_spec, h_spec,
                  pl.BlockSpec((8, LANES), lambda i, j: (0, 0))],
        out_specs=[h_spec] * 2,
        out_shape=[jax.ShapeDtypeStruct((nsh, s, fs), BF16)] * 2,
        compiler_params=_params(("parallel", "parallel")),
    )(dff, wd, gate, up, behind)


def _ffn_dw(hiddens, other, name):
    n = len(hiddens)
    nsh, s, fs = hiddens[0].shape
    d = other.shape[1]
    tn = (((0,), (0,)), ((), ()))

    def body(*refs):
        for a_ref, o_ref in zip(refs[:n], refs[n + 1:]):
            o_ref[...] = lax.dot_general(a_ref[...], refs[n][...], tn, preferred_element_type=F32).astype(BF16)

    return pl.pallas_call(
        body, name=name, grid=(nsh,),
        in_specs=[pl.BlockSpec((None, s, fs), lambda j: (j, 0, 0))] * n + [pl.BlockSpec((s, d), lambda j: (0, 0))],
        out_specs=[pl.BlockSpec((None, fs, d), lambda j: (j, 0, 0))] * n,
        out_shape=[jax.ShapeDtypeStruct((nsh, fs, d), BF16)] * n,
        compiler_params=_params(("parallel",)),
    )(*hiddens, other)


def _final(ff, x1, tgt, g):
    s, d = x1.shape
    ts = _pick(s, (256, 128))

    def body(ff_ref, x1_ref, t_ref, g_ref, loss_ref, dy_ref, dff_ref, dg_ref):
        _acc_init(pl.program_id(0), loss_ref, dg_ref)
        ff_v = ff_ref[...]
        err = x1_ref[...] + _rms(ff_v, g_ref[...]) - t_ref[...]
        tok = jnp.mean(err * err, axis=-1, keepdims=True)
        loss_ref[...] += 0.5 * jnp.sum(tok, axis=0, keepdims=True)
        dy = err * (1.0 / d)
        dy_ref[...] = dy
        dff, dg = _rms_bwd(ff_v, g_ref[...], dy)
        dff_ref[...] = dff.astype(BF16)
        dg_ref[...] += dg

    return pl.pallas_call(
        body, name="final", grid=(s // ts,),
        in_specs=[_rows(ts, d), _rows(ts, d), _rows(ts, d), _vec(d)],
        out_specs=[_vec(LANES), _rows(ts, d), _rows(ts, d), _vec(d)],
        out_shape=[jax.ShapeDtypeStruct((1, LANES), F32), jax.ShapeDtypeStruct((s, d), F32),
                   jax.ShapeDtypeStruct((s, d), BF16), jax.ShapeDtypeStruct((1, d), F32)],
        compiler_params=_params(("arbitrary",)),
    )(ff, x1, tgt, g)


def _pre_bwd(dx1, dh, x, g):
    s, d = x.shape
    ts = _pick(s, (256, 128))

    def body(dx1_ref, dh_ref, x_ref, g_ref, dx_ref, dg_ref):
        _acc_init(pl.program_id(0), dg_ref)
        dxa, dg = _rms_bwd(x_ref[...], g_ref[...], dh_ref[...])
        dx_ref[...] = dx1_ref[...] + dxa
        dg_ref[...] += dg

    return pl.pallas_call(
        body, name="pre_bwd", grid=(s // ts,),
        in_specs=[_rows(ts, d), _rows(ts, d), _rows(ts, d), _vec(d)],
        out_specs=[_rows(ts, d), _vec(d)],
        out_shape=[jax.ShapeDtypeStruct((s, d), F32), jax.ShapeDtypeStruct((1, d), F32)],
        compiler_params=_params(("arbitrary",)),
    )(dx1, dh, x, g)


def _local_step(x, pos, tgt, vecs, in_weights_fn, mix_weights_fn, up_weights_fn, down_weights_fn, grads_fn):
    c = vecs["conv_b"].shape[1]
    ql = vecs["q_norm"].shape[1]
    kvl = vecs["kv_norm"].shape[1]
    half = jnp.arange(0, QK_ROPE, 2, dtype=F32)
    freq = ROPE_THETA ** (-half / QK_ROPE)
    inv_freq = jnp.concatenate([freq, freq, jnp.zeros((LANES - QK_ROPE,), F32)])[None, :]
    tabs = _rope_tables(pos, inv_freq)

    h = _pre_fwd(x, vecs["pre_mix_norm"])
    w_in_t, zero = in_weights_fn(h)
    z = _mm(h, w_in_t, "nt", "mm_z")
    w_uq_t, w_ukv, conv_w, w_out = mix_weights_fn(z)
    u0, qn, kvn, kpe = _split_fwd(z, vecs["q_norm"] + zero, vecs["kv_norm"], tabs, c, ql, kvl)
    u1 = _conv_fwd(u0, conv_w, vecs["conv_b"])
    q = _q_rope(_mm(qn, w_uq_t, "nt", "mm_q"), tabs, False, BF16, "q_rope")
    kv = _mm(kvn, w_ukv, "nn", "mm_kv")
    attn, lse = _attn_fwd(q, kv, kpe)
    mixin = _mix_fwd(u1, vecs["conv_ln_g"], vecs["conv_ln_b"], vecs["conv_out_norm"], attn, vecs["attn_out_norm"])
    mix = _mm(mixin, w_out, "nn", "mm_mix")
    x1, hf = _post_mix_fwd(x, mix, vecs["post_mix_norm"], vecs["pre_ffn_norm"])
    w_gate, w_up = up_weights_fn(mix)
    gate, up, act = _ffn_up(hf, w_gate, w_up)
    w_down = down_weights_fn(act)
    ff = _ffn_down([act], [w_down], "ffn_down")
    loss, dy, dff, d_post_ffn = _final(ff, x1, tgt, vecs["post_ffn_norm"])

    g = {"post_ffn_norm": d_post_ffn}
    zero = grads_fn("down", ("w_down",), _ffn_dw([act], dff, "ffn_dw_down"), dff)
    dgate, dup = _ffn_down_bwd(dff, w_down, gate, up, jnp.zeros((8, LANES), F32) + zero)
    zero = zero + grads_fn("up", ("w_gate", "w_up"), _ffn_dw([dgate, dup], hf, "ffn_dw_up"), dgate)
    dhf = _ffn_down([dgate, dup], [w_gate, w_up], "ffn_dhf")
    dx1, dmix, g["pre_ffn_norm"], g["post_mix_norm"] = _post_mix_bwd(
        dy, dhf, x1, vecs["pre_ffn_norm"] + zero, mix, vecs["post_mix_norm"])
    dmixin = _mm(dmix, w_out, "nt", "mm_dmixin")
    dw_out = _mm(mixin, dmix, "tn", "mm_dw_out", BF16)
    zero = grads_fn("out", ("w_out",), [dw_out.reshape(N_DEV, -1, dw_out.shape[1])], dmix)
    du1, dattn, g["conv_ln_g"], g["conv_ln_b"], g["conv_out_norm"], g["attn_out_norm"] = _mix_bwd(
        dmixin, u1, vecs["conv_ln_g"] + zero, vecs["conv_ln_b"], vecs["conv_out_norm"], attn, vecs["attn_out_norm"])
    du0, g["conv_w"], g["conv_b"] = _conv_bwd(du1, u0, conv_w)
    dq, dkv, dkpe_h = _attn_bwd(q, kv, kpe, attn, dattn, lse)
    dqpre = _q_rope(dq, tabs, True, BF16, "q_rope_bwd")
    dqn = _mm(dqpre, w_uq_t, "nn", "mm_dqn")
    g["w_uq_t"] = _mm(dqpre, qn, "tn", "mm_dw_uq", BF16)
    dkvn = _mm(dkv, w_ukv, "nt", "mm_dkvn")
    g["w_ukv"] = _mm(kvn, dkv, "tn", "mm_dw_ukv", BF16)
    zero = grads_fn("", (), [], dkvn)
    dz, g["q_norm"], g["kv_norm"] = _split_bwd(du0, z, dqn, dkvn, dkpe_h, vecs["q_norm"] + zero, vecs["kv_norm"], tabs,
                                               c, ql, kvl)
    dh = _mm(dz, w_in_t, "nn", "mm_dh")
    g["w_in_t"] = _mm(dz, h, "tn", "mm_dw_in", BF16)
    grad_x, g["pre_mix_norm"] = _pre_bwd(dx1, dh, x, vecs["pre_mix_norm"])
    return loss, grad_x, g


def _my_index():
    return 4 * lax.axis_index("x") + 2 * lax.axis_index("y") + lax.axis_index("c")


def _coords(idx):
    return ((idx >> 2) & 1, (idx >> 1) & 1, idx & 1)


def _place():
    x, y, c = lax.axis_index("x"), lax.axis_index("y"), lax.axis_index("c")
    return (x, y, c), (x, y, 1 - c), [(1 - x, y), (x, 1 - y), (1 - x, 1 - y)]


def _comm_call(body, name, arrays, out_shapes, n_sems):
    n = len(arrays)
    any_spec = pl.BlockSpec(memory_space=pl.ANY)
    return pl.pallas_call(
        body, name=name,
        in_specs=[any_spec] * n, out_specs=[any_spec] * len(out_shapes), out_shape=out_shapes,
        scratch_shapes=[pltpu.SemaphoreType.DMA((n, n_sems))] * 3,
        compiler_params=pltpu.CompilerParams(has_side_effects=True),
    )(*arrays)


def _gather_direct(arrays, name):
    n = len(arrays)

    def body(*refs):
        ins, outs = refs[:n], refs[n:2 * n]
        send_sems, recv_sems, local_sems = refs[2 * n:]
        me = _my_index()
        local = [pltpu.make_async_copy(ins[k], outs[k].at[me], local_sems.at[k, 0]) for k in range(n)]
        for cp in local:
            cp.start()

        def copy(k, p, slot):
            return pltpu.make_async_remote_copy(
                src_ref=ins[k], dst_ref=outs[k].at[slot], send_sem=send_sems.at[k, p - 1],
                recv_sem=recv_sems.at[k, p - 1], device_id=_coords(me ^ p), device_id_type=MESH)

        sends = [copy(k, p, me) for p in range(1, N_DEV) for k in range(n)]
        for cp in sends:
            cp.start()
        for p in range(1, N_DEV):
            for k in range(n):
                copy(k, p, me ^ p).wait_recv()
        for cp in sends:
            cp.wait_send()
        for cp in local:
            cp.wait()

    out_shapes = [jax.ShapeDtypeStruct((N_DEV,) + a.shape, a.dtype) for a in arrays]
    return _comm_call(body, name, arrays, out_shapes, N_DEV - 1)


HBM_SPEC = pl.BlockSpec(memory_space=pltpu.HBM)
SEM_SPEC = pl.BlockSpec(memory_space=pltpu.SEMAPHORE)
DATAFLOW = pltpu.SideEffectType.DATAFLOW_SIDE_EFFECTING


def _split_start(name, copies_of, srcs, lands, after):
    n = len(srcs)

    def body(*refs):
        outs = refs[2 * n + 1:]
        for k in range(n):
            for cp in copies_of(refs[k], refs[n + k], outs[k], outs[n + k]):
                cp.start()
        outs[-1][...] = jnp.zeros_like(outs[-1])

    hbm = lambda a: pltpu.HBM(a.shape, a.dtype)
    out = pl.pallas_call(
        body, name=name,
        in_specs=[HBM_SPEC] * (2 * n + 1),
        out_specs=[SEM_SPEC] * (2 * n) + [HBM_SPEC] * (2 * n) + [pl.BlockSpec(memory_space=pltpu.VMEM)],
        out_shape=[pltpu.SemaphoreType.DMA(())] * (2 * n) + [hbm(a) for a in srcs] + [hbm(a) for a in lands]
        + [jax.ShapeDtypeStruct((8, LANES), F32)],
        input_output_aliases={k: 2 * n + k for k in range(2 * n)},
        compiler_params=pltpu.CompilerParams(has_side_effects=DATAFLOW),
    )(*[pltpu.with_memory_space_constraint(a, pltpu.HBM) for a in list(srcs) + list(lands) + [after]])
    return (out[:n], out[n:2 * n], out[2 * n:3 * n], out[3 * n:4 * n]), out[-1][0, 0]


def _split_wait(name, n_copies, started, after):
    send_sems, recv_sems, srcs, lands = started
    n = len(srcs)

    def body(*refs):
        for k in range(n):
            slots = refs[n + k].at[pl.ds(0, n_copies)]
            all_copies = pltpu.make_async_remote_copy(
                src_ref=slots, dst_ref=slots, send_sem=refs[2 * n + k], recv_sem=refs[3 * n + k],
                device_id=_place()[0], device_id_type=MESH)
            all_copies.wait_send()
            all_copies.wait_recv()

    hbm = lambda a: pltpu.HBM(a.shape, a.dtype)
    out = pl.pallas_call(
        body, name=name,
        in_specs=[HBM_SPEC] * (2 * n) + [SEM_SPEC] * (2 * n) + [HBM_SPEC],
        out_specs=[HBM_SPEC] * (2 * n),
        out_shape=[hbm(a) for a in srcs] + [hbm(a) for a in lands],
        input_output_aliases={k: k for k in range(2 * n)},
        compiler_params=pltpu.CompilerParams(has_side_effects=DATAFLOW),
    )(*srcs, *lands, *send_sems, *recv_sems, pltpu.with_memory_space_constraint(after, pltpu.HBM))
    return out[:n], out[n:]


def _slot(chip, core):
    return 4 * chip[0] + 2 * chip[1] + core


def _gather_copies(src, land, send_sem, recv_sem):
    (x, y, c), sib, chips = _place()
    return [pltpu.make_async_remote_copy(src_ref=src, dst_ref=land.at[_slot((x, y), c)], send_sem=send_sem,
                                         recv_sem=recv_sem, device_id=to, device_id_type=MESH)
            for to in [sib] + [(*chip, c) for chip in chips]]


def _gather_pass_on(lands, name):
    n = len(lands)

    def body(*refs):
        ins, outs = refs[:n], refs[n:2 * n]
        send_sems, recv_sems = refs[2 * n:]
        (x, y, c), sib, chips = _place()
        sends = []
        for k in range(n):
            for j, chip in enumerate(chips):
                sends.append(pltpu.make_async_remote_copy(
                    src_ref=ins[k].at[_slot(chip, c)], dst_ref=outs[k].at[_slot(chip, c)],
                    send_sem=send_sems.at[k, j], recv_sem=recv_sems.at[k, j], device_id=sib, device_id_type=MESH))
        for cp in sends:
            cp.start()
        for cp in sends:
            cp.wait_recv()
        for cp in sends:
            cp.wait_send()

    any_spec = pl.BlockSpec(memory_space=pl.ANY)
    return pl.pallas_call(
        body, name=name,
        in_specs=[any_spec] * n, out_specs=[any_spec] * n,
        out_shape=[jax.ShapeDtypeStruct(a.shape, a.dtype) for a in lands],
        input_output_aliases={k: k for k in range(n)},
        scratch_shapes=[pltpu.SemaphoreType.DMA((n, 3))] * 2,
        compiler_params=pltpu.CompilerParams(has_side_effects=True),
    )(*lands)


def _chip_copies(src, land, send_sem, recv_sem):
    (x, y, c), _, chips = _place()
    return [pltpu.make_async_remote_copy(src_ref=src.at[2 * chip[0] + chip[1]], dst_ref=land.at[j], send_sem=send_sem,
                                         recv_sem=recv_sem, device_id=(*chip, c), device_id_type=MESH)
            for j, chip in enumerate(chips)]


ROW_TILE_BYTES = 14 * 1024 * 1024


def _stream_tile(r, c, bytes_per_elem):
    if r * c * bytes_per_elem <= ROW_TILE_BYTES:
        return r, c
    rows = [t for t in range(16, r, 16) if r % t == 0 and t * c * bytes_per_elem <= ROW_TILE_BYTES]
    if rows:
        return max(rows), c
    cols = [t for t in range(LANES, c, LANES) if c % t == 0 and r * t * bytes_per_elem <= ROW_TILE_BYTES]
    return r, max(cols)


def _sibling_copies(src, land, send_sem, recv_sem):
    (x, y, c), sib, _ = _place()
    return [pltpu.make_async_remote_copy(src_ref=src.at[2 * q + 1 - c], dst_ref=land.at[q], send_sem=send_sem,
                                         recv_sem=recv_sem, device_id=sib, device_id_type=MESH)
            for q in range(4)]


def _pair_sum(core, blocks, theirs, name):
    q, r, c = theirs.shape
    tr, tc = _stream_tile(r, c, 3 * theirs.dtype.itemsize)

    def body(core_ref, a_ref, b_ref, o_ref):
        o_ref[...] = (a_ref[...].astype(F32) + b_ref[...].astype(F32)).astype(o_ref.dtype)

    blk = pl.BlockSpec((1, tr, tc), lambda i, j, k, core_ref: (i, j, k))
    mine = pl.BlockSpec((1, tr, tc), lambda i, j, k, core_ref: (2 * i + core_ref[0], j, k))
    return pl.pallas_call(
        body, name=name,
        grid_spec=pltpu.PrefetchScalarGridSpec(num_scalar_prefetch=1, grid=(q, r // tr, c // tc),
                                               in_specs=[mine, blk], out_specs=blk),
        out_shape=jax.ShapeDtypeStruct(theirs.shape, theirs.dtype),
        compiler_params=_params(("parallel", "parallel", "parallel")),
    )(core, blocks, theirs)


def _reduce_adamw(parts, w, m, v, name, own=None, own_slot=None):
    r, c = w.shape
    n_parts = parts.shape[0]
    tr, tc = _stream_tile(r, c, (n_parts + 1) * parts.dtype.itemsize + 7 * 4)
    c1 = 1.0 - ADAM_B1
    c2 = 1.0 - ADAM_B2
    bc1 = 1.0 - ADAM_B1 ** ADAM_STEP
    bc2 = 1.0 - ADAM_B2 ** ADAM_STEP

    def body(*refs):
        if own is None:
            p_ref, w_ref, m_ref, v_ref, g_ref, d_ref, nm_ref, nv_ref = refs
            g = p_ref[0].astype(F32)
            first = 1
        else:
            _, o_ref, p_ref, w_ref, m_ref, v_ref, g_ref, d_ref, nm_ref, nv_ref = refs
            g = o_ref[0].astype(F32)
            first = 0
        for j in range(first, n_parts):
            g = g + p_ref[j].astype(F32)
        nm = ADAM_B1 * m_ref[...] + c1 * g
        nv = ADAM_B2 * v_ref[...] + c2 * (g * g)
        g_ref[...] = g
        nm_ref[...] = nm
        nv_ref[...] = nv
        d_ref[...] = -ADAM_LR * ((nm / bc1) / (jnp.sqrt(nv / bc2) + ADAM_EPS) + ADAM_WD * w_ref[...])

    out = jax.ShapeDtypeStruct((r, c), F32)
    grid = (r // tr, c // tc)
    if own is None:
        blk = pl.BlockSpec((tr, tc), lambda i, j: (i, j))
        return pl.pallas_call(
            body, name=name, grid=grid,
            in_specs=[pl.BlockSpec((n_parts, tr, tc), lambda i, j: (0, i, j)), blk, blk, blk],
            out_specs=[blk] * 4, out_shape=[out] * 4,
            compiler_params=_params(("parallel", "parallel")),
        )(parts, w, m, v)
    blk = pl.BlockSpec((tr, tc), lambda i, j, slot_ref: (i, j))
    return pl.pallas_call(
        body, name=name,
        grid_spec=pltpu.PrefetchScalarGridSpec(
            num_scalar_prefetch=1, grid=grid,
            in_specs=[pl.BlockSpec((1, tr, tc), lambda i, j, slot_ref: (slot_ref[0], i, j)),
                      pl.BlockSpec((n_parts, tr, tc), lambda i, j, slot_ref: (0, i, j)), blk, blk, blk],
            out_specs=[blk] * 4),
        out_shape=[out] * 4,
        compiler_params=_params(("parallel", "parallel")),
    )(own_slot, own, parts, w, m, v)


_MIX = ("w_in", "w_uq", "w_ukv", "conv_w", "w_out")
_FFN = ("w_gate", "w_up", "w_down")
_BIG = _MIX + _FFN
_TRANSPOSED = ("w_in", "w_uq", "w_gate", "w_up")
_SMALL = ("pre_mix_norm", "q_norm", "kv_norm", "conv_b", "conv_ln_g", "conv_ln_b", "conv_out_norm",
          "attn_out_norm", "post_mix_norm", "pre_ffn_norm", "post_ffn_norm")
_ORDER = ("pre_mix_norm", "w_in", "q_norm", "w_uq", "kv_norm", "w_ukv", "conv_w", "conv_b", "conv_ln_g",
          "conv_ln_b", "conv_out_norm", "attn_out_norm", "w_out", "post_mix_norm", "pre_ffn_norm", "w_gate",
          "w_up", "w_down", "post_ffn_norm")


def _cols_from_shards(g):
    return jnp.transpose(g, (1, 0, 2)).reshape(g.shape[1], N_DEV * g.shape[2])


def _cols_to_shards(w):
    k, n8 = w.shape
    return jnp.transpose(w.reshape(k, N_DEV, n8 // N_DEV), (1, 0, 2))


def _step(x, positions, loss_target, w, m, v):
    s, d = x.shape[1], x.shape[2]
    x2, tgt = x[0], loss_target[0]
    pos = positions.reshape(s, 1)
    vecs = {n: w[n] for n in _SMALL}
    core = lax.axis_index("c").astype(jnp.int32).reshape(1)
    my_chip = (2 * lax.axis_index("x") + lax.axis_index("y")).astype(jnp.int32).reshape(1)
    n_in_cols = N_DEV * w["w_in"].shape[2]
    gathers, scatters, to_sibling = {}, {}, []

    def shard(t, n):
        return t[n][0].T if n in _TRANSPOSED else t[n][0]

    def gather_start(names, tag, after, zero=0.0):
        srcs = [w[n][0] if n == "conv_w" else (shard(w, n) + zero).astype(BF16) for n in names]
        lands = [lax.empty((N_DEV,) + a.shape, a.dtype) for a in srcs]
        gathers[tag], zero = _split_start("gather_" + tag + "_start", _gather_copies, srcs, lands, after)
        return zero

    def gather_finish(names, tag, after):
        srcs, lands = _split_wait("gather_" + tag + "_wait", 4, gathers[tag], after)
        lands = _gather_pass_on(lands, "gather_" + tag + "_pass_on")
        me = _my_index()
        return {n: lax.dynamic_update_slice(g, a[None], (me,) + (0,) * a.ndim) for n, g, a in zip(names, lands, srcs)}

    def in_weights_fn(h):
        w_in_g = gather_finish(("w_in",), "in", h)["w_in"]
        zero = gather_start(("w_gate", "w_up"), "up", w_in_g)
        zero = gather_start(("w_down",), "down", w_in_g, zero)
        return jnp.pad(w_in_g.reshape(-1, d), ((0, LANES - QK_ROPE), (0, 0))), zero

    def mix_weights_fn(z):
        gath = gather_finish(_MIX[1:], "mix", z)
        w_uq_t = jnp.pad(gath["w_uq"], ((0, 0), (0, HEAD_PAD - QK_HEAD), (0, 0))).reshape(N_HEADS * HEAD_PAD, -1)
        return (w_uq_t, _cols_from_shards(gath["w_ukv"]), _cols_from_shards(gath["conv_w"]),
                gath["w_out"].reshape(-1, d))

    def up_weights_fn(mix):
        gath = gather_finish(("w_gate", "w_up"), "up", mix)
        return gath["w_gate"], gath["w_up"]

    def down_weights_fn(act):
        return gather_finish(("w_down",), "down", act)["w_down"]

    def scatter_advance(after):
        if not to_sibling:
            return 0.0
        names, tag, sent = to_sibling.pop()
        blocks, theirs = _split_wait("to_sibling_" + tag + "_wait", 4, sent, after)
        pairs = [_pair_sum(core, b, t, "pair_sum_" + n) for n, b, t in zip(names, blocks, theirs)]
        lands = [lax.empty((3,) + p.shape[1:], p.dtype) for p in pairs]
        scatters[tag], zero = _split_start("scatter_" + tag + "_start", _chip_copies, pairs, lands, theirs[0])
        return zero

    def grads_fn(tag, names, blocks, after):
        zero = scatter_advance(after)
        if not blocks:
            return zero
        lands = [lax.empty((4,) + b.shape[1:], b.dtype) for b in blocks]
        sent, zero2 = _split_start("to_sibling_" + tag + "_start", _sibling_copies, blocks, lands, after)
        to_sibling.append((names, tag, sent))
        return zero + zero2

    zero = gather_start(("w_in",), "in", x2)
    vecs["pre_mix_norm"] = vecs["pre_mix_norm"] + gather_start(_MIX[1:], "mix", x2, zero)
    loss, grad_x, g = _local_step(x2, pos, tgt, vecs, in_weights_fn, mix_weights_fn, up_weights_fn,
                                  down_weights_fn, grads_fn)

    last = ("w_in", "w_uq", "w_ukv", "conv_w")
    last_blocks = [g["w_in_t"][:n_in_cols].reshape(N_DEV, -1, d),
                   g["w_uq_t"].reshape(N_HEADS, HEAD_PAD, -1)[:, :QK_HEAD],
                   _cols_to_shards(g["w_ukv"]), _cols_to_shards(g["conv_w"])]
    small_all = _gather_direct([jnp.concatenate([g[n] for n in _SMALL], axis=1)], "gather_small_grads")[0]
    zero = grads_fn("in", last, last_blocks, small_all) + scatter_advance(grad_x)
    small_all = small_all + zero

    res = {}
    after = small_all
    for tag, names in (("down", ("w_down",)), ("up", ("w_gate", "w_up")), ("out", ("w_out",)), ("in", last)):
        pairs, recv = _split_wait("scatter_" + tag + "_wait", 3, scatters[tag], after)
        for n, own, parts in zip(names, pairs, recv):
            res[n] = _reduce_adamw(parts, shard(w, n), shard(m, n), shard(v, n), "adamw_" + n, own=own,
                                   own_slot=my_chip)
            after = res[n][1]
            res[n] = [(t.T if n in _TRANSPOSED else t)[None] for t in res[n]]
    cat = lambda t: jnp.concatenate([t[n] for n in _SMALL], axis=1)
    sg, sd, sm, sv = _reduce_adamw(small_all, cat(w), cat(m), cat(v), "adamw_small")
    off = 0
    for n in _SMALL:
        width = w[n].shape[1]
        res[n] = [t[:, off:off + width] for t in (sg, sd, sm, sv)]
        off += width

    total = lax.psum(loss[0, 0], ("x", "y", "c"))
    outs = [total, grad_x[None]]
    for part in range(4):
        outs.extend(res[n][part] for n in _ORDER)
    return tuple(outs)


def kernel(x, positions, pre_mix_norm, w_in, q_norm, w_uq, kv_norm, w_ukv, conv_w, conv_b, conv_ln_g, conv_ln_b, conv_out_norm, attn_out_norm, w_out, post_mix_norm, pre_ffn_norm, w_gate, w_up, w_down, post_ffn_norm, loss_target, m_pre_mix_norm, m_w_in, m_q_norm, m_w_uq, m_kv_norm, m_w_ukv, m_conv_w, m_conv_b, m_conv_ln_g, m_conv_ln_b, m_conv_out_norm, m_attn_out_norm, m_w_out, m_post_mix_norm, m_pre_ffn_norm, m_w_gate, m_w_up, m_w_down, m_post_ffn_norm, v_pre_mix_norm, v_w_in, v_q_norm, v_w_uq, v_kv_norm, v_w_ukv, v_conv_w, v_conv_b, v_conv_ln_g, v_conv_ln_b, v_conv_out_norm, v_attn_out_norm, v_w_out, v_post_mix_norm, v_pre_ffn_norm, v_w_gate, v_w_up, v_w_down, v_post_ffn_norm):
    w = dict(zip(_ORDER, (pre_mix_norm, w_in, q_norm, w_uq, kv_norm, w_ukv, conv_w, conv_b, conv_ln_g, conv_ln_b,
                          conv_out_norm, attn_out_norm, w_out, post_mix_norm, pre_ffn_norm, w_gate, w_up, w_down,
                          post_ffn_norm)))
    m = dict(zip(_ORDER, (m_pre_mix_norm, m_w_in, m_q_norm, m_w_uq, m_kv_norm, m_w_ukv, m_conv_w, m_conv_b,
                          m_conv_ln_g, m_conv_ln_b, m_conv_out_norm, m_attn_out_norm, m_w_out, m_post_mix_norm,
                          m_pre_ffn_norm, m_w_gate, m_w_up, m_w_down, m_post_ffn_norm)))
    v = dict(zip(_ORDER, (v_pre_mix_norm, v_w_in, v_q_norm, v_w_uq, v_kv_norm, v_w_ukv, v_conv_w, v_conv_b,
                          v_conv_ln_g, v_conv_ln_b, v_conv_out_norm, v_attn_out_norm, v_w_out, v_post_mix_norm,
                          v_pre_ffn_norm, v_w_gate, v_w_up, v_w_down, v_post_ffn_norm)))
    return _step(x, positions, loss_target, w, m, v)
```

```python
import functools

import jax
import jax.numpy as jnp
from jax import lax
from jax.experimental import pallas as pl
from jax.experimental.pallas import tpu as pltpu

N_DEV = 8
N_HEADS = 8
QK_NOPE = 128
QK_ROPE = 64
V_HEAD = 128
QK_HEAD = QK_NOPE + QK_ROPE
HEAD_PAD = 256
LANES = 128
ATTN_BLOCK = 512
CONV_K = 31
CONV_PAD = 32
EPS = 1e-6
ROPE_THETA = 10000.0
ADAM_LR = 0.001
ADAM_B1 = 0.9
ADAM_B2 = 0.999
ADAM_EPS = 1e-08
ADAM_WD = 0.01
ADAM_STEP = 10
VMEM_LIMIT = 56 * 1024 * 1024
F32 = jnp.float32
BF16 = jnp.bfloat16
MESH = pl.DeviceIdType.MESH
NEG = -1e30


def _pick(n, prefs):
    for p in prefs:
        if p <= n and n % p == 0:
            return p
    return n


def _params(sem):
    return pltpu.CompilerParams(dimension_semantics=sem, vmem_limit_bytes=VMEM_LIMIT)


_DIMS = {"nn": (((1,), (0,)), ((), ())), "nt": (((1,), (1,)), ((), ())), "tn": (((0,), (0,)), ((), ()))}


MM_VMEM_BUDGET = 40 * 1024 * 1024
MM_MAX_MACS = 3 * 1024 ** 3


V7X_HBM_BYTES_PER_S = 3.0e12
V7X_MXU_MACS_PER_S = 0.45e15
GRID_STEP_S = 0.35e-6


def _mm_tiles(m, n, k, size_a, size_b, size_o):
    best = None
    for tm in sorted({m, 1024, 512, 256, 128}, reverse=True):
        if tm > m or m % tm:
            continue
        for tn in sorted({n, 2048, 1024, 512, 384, 256, 128}, reverse=True):
            if tn > n or n % tn:
                continue
            vmem = 2 * (tm * k * size_a + k * tn * size_b + tm * tn * size_o)
            if vmem > MM_VMEM_BUDGET or tm * tn * k > MM_MAX_MACS:
                continue
            b_reads = 1 if tn == n else m // tm
            traffic = m * k * size_a + b_reads * k * n * size_b + m * n * size_o
            exposed = tm * k * size_a + k * tn * size_b + tm * tn * size_o
            steps = (m // tm) * (n // tn)
            key = (max(traffic / V7X_HBM_BYTES_PER_S, m * n * k / V7X_MXU_MACS_PER_S)
                   + exposed / V7X_HBM_BYTES_PER_S + steps * GRID_STEP_S)
            if best is None or key < best[0]:
                best = (key, tm, tn)
    assert best is not None, (m, n, k)
    return best[1], best[2]


def _mm(a, b, mode, name, out_dtype=F32, add=None):
    if mode == "nn":
        (m, k), (k2, n) = a.shape, b.shape
    elif mode == "nt":
        (m, k), (n, k2) = a.shape, b.shape
    else:
        (k, m), (k2, n) = a.shape, b.shape
    assert k == k2, (a.shape, b.shape, mode)
    tm, tn = _mm_tiles(m, n, k, a.dtype.itemsize, b.dtype.itemsize,
                       jnp.dtype(out_dtype).itemsize + (0 if add is None else 4))
    dims = _DIMS[mode]

    def body(a_ref, b_ref, *rest):
        acc = lax.dot_general(a_ref[...].astype(BF16), b_ref[...].astype(BF16), dims, preferred_element_type=F32)
        if add is not None:
            acc = acc + rest[0][...]
        rest[-1][...] = acc.astype(rest[-1].dtype)

    if mode == "tn":
        a_spec = pl.BlockSpec((k, tm), lambda i, j: (0, i))
    else:
        a_spec = pl.BlockSpec((tm, k), lambda i, j: (i, 0))
    if mode == "nt":
        b_spec = pl.BlockSpec((tn, k), lambda i, j: (j, 0))
    else:
        b_spec = pl.BlockSpec((k, tn), lambda i, j: (0, j))
    o_spec = pl.BlockSpec((tm, tn), lambda i, j: (i, j))
    extra = [] if add is None else [add]
    return pl.pallas_call(
        body, name=name,
        grid=(m // tm, n // tn),
        in_specs=[a_spec, b_spec] + [o_spec] * len(extra),
        out_specs=o_spec,
        out_shape=jax.ShapeDtypeStruct((m, n), out_dtype),
        compiler_params=_params(("parallel", "parallel")),
    )(a, b, *extra)


def _sigmoid(x):
    return 1.0 / (1.0 + jnp.exp(-x))


def _rms(x, g):
    r = lax.rsqrt(jnp.mean(x * x, axis=-1, keepdims=True) + EPS)
    return (x * r) * g


def _rms_bwd(x, g, dy):
    r = lax.rsqrt(jnp.mean(x * x, axis=-1, keepdims=True) + EPS)
    xh = x * r
    dyg = dy * g
    dx = r * (dyg - xh * jnp.mean(dyg * xh, axis=-1, keepdims=True))
    return dx, jnp.sum(dy * xh, axis=0, keepdims=True)


def _ln(x, g, b):
    mu = jnp.mean(x, axis=-1, keepdims=True)
    xc = x - mu
    rs = lax.rsqrt(jnp.mean(xc * xc, axis=-1, keepdims=True) + EPS)
    return (xc * rs) * g + b


def _ln_bwd(x, g, dy):
    mu = jnp.mean(x, axis=-1, keepdims=True)
    xc = x - mu
    rs = lax.rsqrt(jnp.mean(xc * xc, axis=-1, keepdims=True) + EPS)
    xh = xc * rs
    dyg = dy * g
    dx = rs * (dyg - jnp.mean(dyg, axis=-1, keepdims=True) - xh * jnp.mean(dyg * xh, axis=-1, keepdims=True))
    return dx, jnp.sum(dy * xh, axis=0, keepdims=True), jnp.sum(dy, axis=0, keepdims=True)


def _silu(x):
    return x * _sigmoid(x)


def _silu_grad(x):
    s = _sigmoid(x)
    return s * (1.0 + x * (1.0 - s))


def _rope(x, cos, sa, sb):
    return x * cos + pltpu.roll(x, 96, 1) * sa + pltpu.roll(x, 32, 1) * sb


def _rope_t(d, cos, sa, sb):
    return d * cos - pltpu.roll(d, 96, 1) * sa - pltpu.roll(d, 32, 1) * sb


def _rows(ts, w):
    return pl.BlockSpec((ts, w), lambda i: (i, 0))


def _vec(w):
    return pl.BlockSpec((1, w), lambda i: (0, 0))


def _acc_init(i, *refs):
    @pl.when(i == 0)
    def _():
        for r in refs:
            r[...] = jnp.zeros_like(r)


def _rope_tables(pos, inv_freq):
    s = pos.shape[0]
    ts = _pick(s, (512, 256, 128))

    def body(p_ref, f_ref, c_ref, sa_ref, sb_ref):
        ang = p_ref[...].astype(F32) * f_ref[...]
        lane = lax.broadcasted_iota(jnp.int32, ang.shape, 1)
        c, sn = jnp.cos(ang), jnp.sin(ang)
        c_ref[...] = jnp.where(lane < QK_ROPE, c, 0.0)
        sa_ref[...] = jnp.where(lane < QK_ROPE // 2, -sn, 0.0)
        sb_ref[...] = jnp.where((lane >= QK_ROPE // 2) & (lane < QK_ROPE), sn, 0.0)

    out = jax.ShapeDtypeStruct((s, LANES), F32)
    return pl.pallas_call(
        body, name="rope_tables", grid=(s // ts,),
        in_specs=[_rows(ts, 1), _vec(LANES)],
        out_specs=[_rows(ts, LANES)] * 3, out_shape=[out] * 3,
        compiler_params=_params(("parallel",)),
    )(pos, inv_freq)


def _pre_fwd(x, g):
    s, d = x.shape
    ts = _pick(s, (256, 128))

    def body(x_ref, g_ref, h_ref):
        h_ref[...] = _rms(x_ref[...], g_ref[...]).astype(BF16)

    return pl.pallas_call(
        body, name="pre_fwd", grid=(s // ts,),
        in_specs=[_rows(ts, d), _vec(d)], out_specs=_rows(ts, d),
        out_shape=jax.ShapeDtypeStruct((s, d), BF16),
        compiler_params=_params(("parallel",)),
    )(x, g)


def _split_fwd(z, gq, gkv, tabs, c, ql, kvl):
    s, zw = z.shape
    ts = _pick(s, (256, 128))
    o_q, o_kv, o_kr = 2 * c, 2 * c + ql, 2 * c + ql + kvl

    def body(z_ref, gq_ref, gkv_ref, c_ref, sa_ref, sb_ref, u0_ref, qn_ref, kvn_ref, kpe_ref):
        u0_ref[...] = z_ref[:, 0:c] * _sigmoid(z_ref[:, c:2 * c])
        qn_ref[...] = _rms(z_ref[:, o_q:o_kv], gq_ref[...]).astype(BF16)
        kvn_ref[...] = _rms(z_ref[:, o_kv:o_kr], gkv_ref[...]).astype(BF16)
        kpe_ref[...] = _rope(z_ref[:, o_kr:o_kr + LANES], c_ref[...], sa_ref[...], sb_ref[...]).astype(BF16)

    return pl.pallas_call(
        body, name="split_fwd", grid=(s // ts,),
        in_specs=[_rows(ts, zw), _vec(ql), _vec(kvl)] + [_rows(ts, LANES)] * 3,
        out_specs=[_rows(ts, c), _rows(ts, ql), _rows(ts, kvl), _rows(ts, LANES)],
        out_shape=[jax.ShapeDtypeStruct((s, c), F32), jax.ShapeDtypeStruct((s, ql), BF16),
                   jax.ShapeDtypeStruct((s, kvl), BF16), jax.ShapeDtypeStruct((s, LANES), BF16)],
        compiler_params=_params(("parallel",)),
    )(z, gq, gkv, *tabs)


def _split_bwd(du0, z, dqn, dkvn, dkpe_h, gq, gkv, tabs, c, ql, kvl):
    s, zw = z.shape
    ts = _pick(s, (256, 128))
    o_q, o_kv, o_kr = 2 * c, 2 * c + ql, 2 * c + ql + kvl

    def body(du0_ref, z_ref, dqn_ref, dkvn_ref, dkh_ref, gq_ref, gkv_ref, c_ref, sa_ref, sb_ref,
             dz_ref, dgq_ref, dgkv_ref):
        _acc_init(pl.program_id(0), dgq_ref, dgkv_ref)
        du0 = du0_ref[...]
        a = z_ref[:, 0:c]
        sg = _sigmoid(z_ref[:, c:2 * c])
        dz_ref[:, 0:c] = (du0 * sg).astype(BF16)
        dz_ref[:, c:2 * c] = (du0 * a * sg * (1.0 - sg)).astype(BF16)
        dq, dgq = _rms_bwd(z_ref[:, o_q:o_kv], gq_ref[...], dqn_ref[...])
        dz_ref[:, o_q:o_kv] = dq.astype(BF16)
        dgq_ref[...] += dgq
        dkv, dgkv = _rms_bwd(z_ref[:, o_kv:o_kr], gkv_ref[...], dkvn_ref[...])
        dz_ref[:, o_kv:o_kr] = dkv.astype(BF16)
        dgkv_ref[...] += dgkv
        dk = dkh_ref[:, 0:LANES]
        for h in range(1, N_HEADS):
            dk = dk + dkh_ref[:, h * LANES:(h + 1) * LANES]
        dz_ref[:, o_kr:o_kr + LANES] = _rope_t(dk, c_ref[...], sa_ref[...], sb_ref[...]).astype(BF16)

    return pl.pallas_call(
        body, name="split_bwd", grid=(s // ts,),
        in_specs=[_rows(ts, c), _rows(ts, zw), _rows(ts, ql), _rows(ts, kvl), _rows(ts, N_HEADS * LANES),
                  _vec(ql), _vec(kvl)] + [_rows(ts, LANES)] * 3,
        out_specs=[_rows(ts, zw), _vec(ql), _vec(kvl)],
        out_shape=[jax.ShapeDtypeStruct((s, zw), BF16), jax.ShapeDtypeStruct((1, ql), F32),
                   jax.ShapeDtypeStruct((1, kvl), F32)],
        compiler_params=_params(("arbitrary",)),
    )(du0, z, dqn, dkvn, dkpe_h, gq, gkv, *tabs)


def _q_rope(qpre, tabs, transpose, out_dtype, name):
    s, w = qpre.shape
    ts = _pick(s, (256, 128))
    rot = _rope_t if transpose else _rope

    def body(q_ref, c_ref, sa_ref, sb_ref, o_ref):
        cs, sa, sb = c_ref[...], sa_ref[...], sb_ref[...]
        for h in range(N_HEADS):
            lo = h * HEAD_PAD
            o_ref[:, lo:lo + QK_NOPE] = q_ref[:, lo:lo + QK_NOPE].astype(out_dtype)
            o_ref[:, lo + QK_NOPE:lo + HEAD_PAD] = rot(q_ref[:, lo + QK_NOPE:lo + HEAD_PAD], cs, sa, sb).astype(out_dtype)

    return pl.pallas_call(
        body, name=name, grid=(s // ts,),
        in_specs=[_rows(ts, w)] + [_rows(ts, LANES)] * 3, out_specs=_rows(ts, w),
        out_shape=jax.ShapeDtypeStruct((s, w), out_dtype),
        compiler_params=_params(("parallel",)),
    )(qpre, *tabs)


def _conv_fwd(u0, w, b):
    s, c = u0.shape
    tc = LANES
    rc = _pick(s, (256, 128))

    def body(u_ref, w_ref, b_ref, o_ref, pad_ref):
        pad_ref[0:CONV_PAD, :] = jnp.zeros((CONV_PAD, tc), F32)
        pad_ref[CONV_PAD:CONV_PAD + s, :] = u_ref[...]
        for r in range(s // rc):
            acc = jnp.broadcast_to(b_ref[...], (rc, tc))
            for k in range(CONV_K):
                lo = r * rc + CONV_PAD - (CONV_K - 1) + k
                acc = acc + w_ref[k:k + 1, :] * pad_ref[lo:lo + rc, :]
            o_ref[r * rc:(r + 1) * rc, :] = acc

    col = lambda j: (0, j)
    return pl.pallas_call(
        body, name="conv_fwd", grid=(c // tc,),
        in_specs=[pl.BlockSpec((s, tc), col), pl.BlockSpec((CONV_K, tc), col), pl.BlockSpec((1, tc), col)],
        out_specs=pl.BlockSpec((s, tc), col),
        out_shape=jax.ShapeDtypeStruct((s, c), F32),
        scratch_shapes=[pltpu.VMEM((s + CONV_PAD, tc), F32)],
        compiler_params=_params(("parallel",)),
    )(u0, w, b)


def _conv_bwd(du1, u0, w):
    s, c = u0.shape
    tc = LANES
    rc = _pick(s, (256, 128))

    def body(d_ref, u_ref, w_ref, du_ref, dw_ref, db_ref, upad_ref, dpad_ref):
        upad_ref[0:CONV_PAD, :] = jnp.zeros((CONV_PAD, tc), F32)
        upad_ref[CONV_PAD:CONV_PAD + s, :] = u_ref[...]
        dpad_ref[0:s, :] = d_ref[...]
        dpad_ref[s:s + CONV_PAD, :] = jnp.zeros((CONV_PAD, tc), F32)
        for r in range(s // rc):
            acc = jnp.zeros((rc, tc), F32)
            for k in range(CONV_K):
                lo = r * rc + (CONV_K - 1) - k
                acc = acc + w_ref[k:k + 1, :] * dpad_ref[lo:lo + rc, :]
            du_ref[r * rc:(r + 1) * rc, :] = acc
        for k in range(CONV_K):
            acc8 = jnp.zeros((8, tc), F32)
            for r in range(s // rc):
                lo = r * rc + CONV_PAD - (CONV_K - 1) + k
                prod = d_ref[r * rc:(r + 1) * rc, :] * upad_ref[lo:lo + rc, :]
                acc8 = acc8 + jnp.sum(prod.reshape(rc // 8, 8, tc), axis=0)
            dw_ref[k:k + 1, :] = jnp.sum(acc8, axis=0, keepdims=True)
        db_ref[...] = jnp.sum(d_ref[...], axis=0, keepdims=True)

    col = lambda j: (0, j)
    return pl.pallas_call(
        body, name="conv_bwd", grid=(c // tc,),
        in_specs=[pl.BlockSpec((s, tc), col), pl.BlockSpec((s, tc), col), pl.BlockSpec((CONV_K, tc), col)],
        out_specs=[pl.BlockSpec((s, tc), col), pl.BlockSpec((CONV_K, tc), col), pl.BlockSpec((1, tc), col)],
        out_shape=[jax.ShapeDtypeStruct((s, c), F32), jax.ShapeDtypeStruct((CONV_K, c), F32),
                   jax.ShapeDtypeStruct((1, c), F32)],
        scratch_shapes=[pltpu.VMEM((s + CONV_PAD, tc), F32), pltpu.VMEM((s + CONV_PAD, tc), F32)],
        compiler_params=_params(("parallel",)),
    )(du1, u0, w)


def _causal_mask(sc, qi, kj, tq, tk):
    rows = qi * tq + lax.broadcasted_iota(jnp.int32, sc.shape, 0)
    cols = kj * tk + lax.broadcasted_iota(jnp.int32, sc.shape, 1)
    return jnp.where(cols <= rows, sc, NEG)


def _attn_fwd(q, kv, kpe):
    s = q.shape[0]
    tq = tk = _pick(s, (ATTN_BLOCK, 256, 128))
    reps = tk // LANES
    scale = QK_HEAD ** -0.5
    nt = (((1,), (1,)), ((), ()))

    def body(q_ref, kn_ref, v_ref, kpe_ref, o_ref, lse_ref, kf_ref, vb_ref, m_ref, l_ref, acc_ref):
        i = pl.program_id(1)

        @pl.when(i == 0)
        def _():
            kf_ref[:, 0:QK_NOPE] = kn_ref[...].astype(BF16)
            kf_ref[:, QK_NOPE:HEAD_PAD] = kpe_ref[...]
            vb_ref[...] = v_ref[...].astype(BF16)

        qb = q_ref[...]
        m_ref[...] = jnp.full((tq, LANES), NEG, F32)
        l_ref[...] = jnp.zeros((tq, LANES), F32)
        acc_ref[...] = jnp.zeros((tq, V_HEAD), F32)

        def block(j, diagonal):
            off = pl.multiple_of(j * tk, tk)
            sc = lax.dot_general(qb, kf_ref[pl.ds(off, tk), :], nt, preferred_element_type=F32) * scale
            if diagonal:
                sc = _causal_mask(sc, 0, 0, tq, tk)
            m_prev = m_ref[...]
            m_new = jnp.maximum(m_prev, jnp.max(sc, axis=1, keepdims=True))
            p = jnp.exp(sc - jnp.tile(m_new, (1, reps)))
            alpha = jnp.exp(m_prev - m_new)
            l_ref[...] = alpha * l_ref[...] + jnp.sum(p, axis=1, keepdims=True)
            acc_ref[...] = alpha * acc_ref[...] + jnp.dot(p.astype(BF16), vb_ref[pl.ds(off, tk), :],
                                                          preferred_element_type=F32)
            m_ref[...] = m_new

        def below_diagonal(j, carry):
            block(j, False)
            return carry

        lax.fori_loop(0, i, below_diagonal, 0)
        block(i, True)
        o_ref[...] = acc_ref[...] / l_ref[...]
        lse_ref[...] = m_ref[...] + jnp.log(l_ref[...])

    return pl.pallas_call(
        body, name="attn_fwd", grid=(N_HEADS, s // tq),
        in_specs=[pl.BlockSpec((tq, HEAD_PAD), lambda h, i: (i, h)),
                  pl.BlockSpec((s, QK_NOPE), lambda h, i: (0, 2 * h)),
                  pl.BlockSpec((s, V_HEAD), lambda h, i: (0, 2 * h + 1)),
                  pl.BlockSpec((s, LANES), lambda h, i: (0, 0))],
        out_specs=[pl.BlockSpec((tq, V_HEAD), lambda h, i: (i, h)),
                   pl.BlockSpec((tq, LANES), lambda h, i: (i, h))],
        out_shape=[jax.ShapeDtypeStruct((s, N_HEADS * V_HEAD), F32),
                   jax.ShapeDtypeStruct((s, N_HEADS * LANES), F32)],
        scratch_shapes=[pltpu.VMEM((s, HEAD_PAD), BF16), pltpu.VMEM((s, V_HEAD), BF16),
                        pltpu.VMEM((tq, LANES), F32), pltpu.VMEM((tq, LANES), F32), pltpu.VMEM((tq, V_HEAD), F32)],
        compiler_params=_params(("parallel", "arbitrary")),
    )(q, kv, kv, kpe)


def _attn_bwd(q, kv, kpe, o, do, lse):
    s = q.shape[0]
    tq = tk = _pick(s, (ATTN_BLOCK, 256, 128))
    nq = s // tq
    reps = tk // LANES
    scale = QK_HEAD ** -0.5
    nt = (((1,), (1,)), ((), ()))
    tn = (((0,), (0,)), ((), ()))

    def body(q_ref, kn_ref, v_ref, kpe_ref, o_ref, do_ref, lse_ref, dq_ref, dkv_ref, dkpe_ref,
             kf_ref, vb_ref, dk_ref, dv_ref):
        j = pl.program_id(1)

        @pl.when(j == 0)
        def _():
            dq_ref[...] = jnp.zeros_like(dq_ref)

        kf_ref[:, 0:QK_NOPE] = kn_ref[...].astype(BF16)
        kf_ref[:, QK_NOPE:HEAD_PAD] = kpe_ref[...]
        vb_ref[...] = v_ref[...].astype(BF16)
        dk_ref[...] = jnp.zeros_like(dk_ref)
        dv_ref[...] = jnp.zeros_like(dv_ref)

        def block(i, diagonal):
            off = pl.multiple_of(i * tq, tq)
            qb = q_ref[pl.ds(off, tq), :]
            dob = do_ref[pl.ds(off, tq), :]
            delta = jnp.sum(dob * o_ref[pl.ds(off, tq), :], axis=1, keepdims=True)
            sc = lax.dot_general(qb, kf_ref[...], nt, preferred_element_type=F32) * scale
            if diagonal:
                sc = _causal_mask(sc, 0, 0, tq, tk)
            p = jnp.exp(sc - jnp.tile(lse_ref[pl.ds(off, tq), :], (1, reps)))
            dob16 = dob.astype(BF16)
            dv_ref[...] += lax.dot_general(p.astype(BF16), dob16, tn, preferred_element_type=F32)
            dp = lax.dot_general(dob16, vb_ref[...], nt, preferred_element_type=F32)
            ds = (p * (dp - delta) * scale).astype(BF16)
            dq_ref[pl.ds(off, tq), :] += jnp.dot(ds, kf_ref[...], preferred_element_type=F32)
            dk_ref[...] += lax.dot_general(ds, qb, tn, preferred_element_type=F32)

        def above_diagonal(i, carry):
            block(i, False)
            return carry

        block(j, True)
        lax.fori_loop(j + 1, nq, above_diagonal, 0)
        dkv_ref[:, 0:QK_NOPE] = dk_ref[:, 0:QK_NOPE]
        dkv_ref[:, QK_NOPE:HEAD_PAD] = dv_ref[...]
        dkpe_ref[...] = dk_ref[:, QK_NOPE:HEAD_PAD]

    head_rows = lambda w: pl.BlockSpec((s, w), lambda h, j: (0, h))
    return pl.pallas_call(
        body, name="attn_bwd", grid=(N_HEADS, s // tk),
        in_specs=[head_rows(HEAD_PAD),
                  pl.BlockSpec((tk, QK_NOPE), lambda h, j: (j, 2 * h)),
                  pl.BlockSpec((tk, V_HEAD), lambda h, j: (j, 2 * h + 1)),
                  pl.BlockSpec((tk, LANES), lambda h, j: (j, 0)),
                  head_rows(V_HEAD), head_rows(V_HEAD), head_rows(LANES)],
        out_specs=[head_rows(HEAD_PAD),
                   pl.BlockSpec((tk, HEAD_PAD), lambda h, j: (j, h)),
                   pl.BlockSpec((tk, LANES), lambda h, j: (j, h))],
        out_shape=[jax.ShapeDtypeStruct((s, N_HEADS * HEAD_PAD), F32),
                   jax.ShapeDtypeStruct((s, N_HEADS * HEAD_PAD), F32),
                   jax.ShapeDtypeStruct((s, N_HEADS * LANES), F32)],
        scratch_shapes=[pltpu.VMEM((tk, HEAD_PAD), BF16), pltpu.VMEM((tk, V_HEAD), BF16),
                        pltpu.VMEM((tk, HEAD_PAD), F32), pltpu.VMEM((tk, V_HEAD), F32)],
        compiler_params=_params(("parallel", "arbitrary")),
    )(q, kv, kv, kpe, o, do, lse)


def _mix_fwd(u1, lng, lnb, gcon, attn, gattn):
    s, c = u1.shape
    ac = attn.shape[1]
    ts = _pick(s, (256, 128))

    def body(u_ref, lg_ref, lb_ref, gc_ref, a_ref, ga_ref, o_ref):
        t3 = _silu(_ln(u_ref[...], lg_ref[...], lb_ref[...]))
        o_ref[:, 0:c] = _rms(t3, gc_ref[...]).astype(BF16)
        o_ref[:, c:c + ac] = _rms(a_ref[...], ga_ref[...]).astype(BF16)

    return pl.pallas_call(
        body, name="mix_fwd", grid=(s // ts,),
        in_specs=[_rows(ts, c), _vec(c), _vec(c), _vec(c), _rows(ts, ac), _vec(ac)],
        out_specs=_rows(ts, c + ac),
        out_shape=jax.ShapeDtypeStruct((s, c + ac), BF16),
        compiler_params=_params(("parallel",)),
    )(u1, lng, lnb, gcon, attn, gattn)


def _mix_bwd(dmixin, u1, lng, lnb, gcon, attn, gattn):
    s, c = u1.shape
    ac = attn.shape[1]
    ts = _pick(s, (256, 128))

    def body(d_ref, u_ref, lg_ref, lb_ref, gc_ref, a_ref, ga_ref,
             du_ref, da_ref, dlg_ref, dlb_ref, dgc_ref, dga_ref):
        _acc_init(pl.program_id(0), dlg_ref, dlb_ref, dgc_ref, dga_ref)
        u = u_ref[...]
        t2 = _ln(u, lg_ref[...], lb_ref[...])
        dt3, dgc = _rms_bwd(_silu(t2), gc_ref[...], d_ref[:, 0:c])
        du, dlg, dlb = _ln_bwd(u, lg_ref[...], dt3 * _silu_grad(t2))
        du_ref[...] = du
        dlg_ref[...] += dlg
        dlb_ref[...] += dlb
        dgc_ref[...] += dgc
        da, dga = _rms_bwd(a_ref[...], ga_ref[...], d_ref[:, c:c + ac])
        da_ref[...] = da
        dga_ref[...] += dga

    return pl.pallas_call(
        body, name="mix_bwd", grid=(s // ts,),
        in_specs=[_rows(ts, c + ac), _rows(ts, c), _vec(c), _vec(c), _vec(c), _rows(ts, ac), _vec(ac)],
        out_specs=[_rows(ts, c), _rows(ts, ac), _vec(c), _vec(c), _vec(c), _vec(ac)],
        out_shape=[jax.ShapeDtypeStruct((s, c), F32), jax.ShapeDtypeStruct((s, ac), F32),
                   jax.ShapeDtypeStruct((1, c), F32), jax.ShapeDtypeStruct((1, c), F32),
                   jax.ShapeDtypeStruct((1, c), F32), jax.ShapeDtypeStruct((1, ac), F32)],
        compiler_params=_params(("arbitrary",)),
    )(dmixin, u1, lng, lnb, gcon, attn, gattn)


def _post_mix_fwd(x, mix, gpost, gpre):
    s, d = x.shape
    ts = _pick(s, (256, 128))

    def body(x_ref, m_ref, gp_ref, gf_ref, x1_ref, hf_ref):
        x1 = x_ref[...] + _rms(m_ref[...], gp_ref[...])
        x1_ref[...] = x1
        hf_ref[...] = _rms(x1, gf_ref[...]).astype(BF16)

    return pl.pallas_call(
        body, name="post_mix_fwd", grid=(s // ts,),
        in_specs=[_rows(ts, d), _rows(ts, d), _vec(d), _vec(d)],
        out_specs=[_rows(ts, d), _rows(ts, d)],
        out_shape=[jax.ShapeDtypeStruct((s, d), F32), jax.ShapeDtypeStruct((s, d), BF16)],
        compiler_params=_params(("parallel",)),
    )(x, mix, gpost, gpre)


def _post_mix_bwd(dy, dhf, x1, gpre, mix, gpost):
    s, d = x1.shape
    ts = _pick(s, (256, 128))

    def body(dy_ref, dh_ref, x1_ref, gf_ref, m_ref, gp_ref, dx1_ref, dm_ref, dgf_ref, dgp_ref):
        _acc_init(pl.program_id(0), dgf_ref, dgp_ref)
        dxa, dgf = _rms_bwd(x1_ref[...], gf_ref[...], dh_ref[...])
        dx1 = dy_ref[...] + dxa
        dx1_ref[...] = dx1
        dgf_ref[...] += dgf
        dm, dgp = _rms_bwd(m_ref[...], gp_ref[...], dx1)
        dm_ref[...] = dm.astype(BF16)
        dgp_ref[...] += dgp

    return pl.pallas_call(
        body, name="post_mix_bwd", grid=(s // ts,),
        in_specs=[_rows(ts, d), _rows(ts, d), _rows(ts, d), _vec(d), _rows(ts, d), _vec(d)],
        out_specs=[_rows(ts, d), _rows(ts, d), _vec(d), _vec(d)],
        out_shape=[jax.ShapeDtypeStruct((s, d), F32), jax.ShapeDtypeStruct((s, d), BF16),
                   jax.ShapeDtypeStruct((1, d), F32), jax.ShapeDtypeStruct((1, d), F32)],
        compiler_params=_params(("arbitrary",)),
    )(dy, dhf, x1, gpre, mix, gpost)


def _ffn_up(hf, wg, wu):
    s, d = hf.shape
    nsh, fs, _ = wg.shape
    tm = _pick(s, (1024, 512, 256, 128))
    nt = (((1,), (1,)), ((), ()))

    def body(h_ref, wg_ref, wu_ref, g_ref, u_ref, a_ref):
        h = h_ref[...]
        g = lax.dot_general(h, wg_ref[...], nt, preferred_element_type=F32)
        u = lax.dot_general(h, wu_ref[...], nt, preferred_element_type=F32)
        g_ref[...] = g
        u_ref[...] = u
        a_ref[...] = (_silu(g) * u).astype(BF16)

    w_spec = pl.BlockSpec((None, fs, d), lambda i, j: (j, 0, 0))
    o_spec = pl.BlockSpec((None, tm, fs), lambda i, j: (j, i, 0))
    return pl.pallas_call(
        body, name="ffn_up", grid=(s // tm, nsh),
        in_specs=[pl.BlockSpec((tm, d), lambda i, j: (i, 0)), w_spec, w_spec],
        out_specs=[o_spec] * 3,
        out_shape=[jax.ShapeDtypeStruct((nsh, s, fs), F32)] * 2 + [jax.ShapeDtypeStruct((nsh, s, fs), BF16)],
        compiler_params=_params(("parallel", "parallel")),
    )(hf, wg, wu)


def _ffn_down(acts, ws, name):
    n = len(acts)
    nsh, s, fs = acts[0].shape
    d = ws[0].shape[2]
    tm = _pick(s, (1024, 512, 256, 128))
    tn = _pick(d, (512, 256, 128) if n == 1 else (256, 128))
    a_mode = None if n == 1 else pl.Buffered(1)

    def body(*refs):
        acc = None
        for a_ref, w_ref in zip(refs[:n], refs[n:2 * n]):
            for j in range(nsh):
                part = jnp.dot(a_ref[j], w_ref[j], preferred_element_type=F32)
                acc = part if acc is None else acc + part
        refs[-1][...] = acc

    return pl.pallas_call(
        body, name=name, grid=(s // tm, d // tn),
        in_specs=[pl.BlockSpec((nsh, tm, fs), lambda i, j: (0, i, 0), pipeline_mode=a_mode)] * n
        + [pl.BlockSpec((nsh, fs, tn), lambda i, j: (0, 0, j))] * n,
        out_specs=pl.BlockSpec((tm, tn), lambda i, j: (i, j)),
        out_shape=jax.ShapeDtypeStruct((s, d), F32),
        compiler_params=_params(("parallel", "parallel")),
    )(*acts, *ws)


def _ffn_down_bwd(dff, wd, gate, up, behind):
    s, d = dff.shape
    nsh, fs, _ = wd.shape
    tm = _pick(s, (1024, 512, 256, 128))
    nt = (((1,), (1,)), ((), ()))

    def body(d_ref, w_ref, g_ref, u_ref, _, dg_ref, du_ref):
        dact = lax.dot_general(d_ref[...], w_ref[...], nt, preferred_element_type=F32)
        g = g_ref[...]
        dg_ref[...] = (dact * u_ref[...] * _silu_grad(g)).astype(BF16)
        du_ref[...] = (dact * _silu(g)).astype(BF16)

    h_spec = pl.BlockSpec((None, tm, fs), lambda i, j: (j, i, 0))
    return pl.pallas_call(
        body, name="ffn_down_bwd", grid=(s // tm, nsh),
        in_specs=[pl.BlockSpec((tm, d), lambda i, j: (i, 0)),
                  pl.BlockSpec((None, fs, d), lambda i, j: (j, 0, 0)), h_spec, h_spec,
                  pl.BlockSpec((8, LANES), lambda i, j: (0, 0))],
        out_specs=[h_spec] * 2,
        out_shape=[jax.ShapeDtypeStruct((nsh, s, fs), BF16)] * 2,
        compiler_params=_params(("parallel", "parallel")),
    )(dff, wd, gate, up, behind)


def _ffn_dw(hiddens, other, name):
    n = len(hiddens)
    nsh, s, fs = hiddens[0].shape
    d = other.shape[1]
    tn = (((0,), (0,)), ((), ()))

    def body(*refs):
        for a_ref, o_ref in zip(refs[:n], refs[n + 1:]):
            o_ref[...] = lax.dot_general(a_ref[...], refs[n][...], tn, preferred_element_type=F32).astype(BF16)

    return pl.pallas_call(
        body, name=name, grid=(nsh,),
        in_specs=[pl.BlockSpec((None, s, fs), lambda j: (j, 0, 0))] * n + [pl.BlockSpec((s, d), lambda j: (0, 0))],
        out_specs=[pl.BlockSpec((None, fs, d), lambda j: (j, 0, 0))] * n,
        out_shape=[jax.ShapeDtypeStruct((nsh, fs, d), BF16)] * n,
        compiler_params=_params(("parallel",)),
    )(*hiddens, other)


def _final(ff, x1, tgt, g):
    s, d = x1.shape
    ts = _pick(s, (256, 128))

    def body(ff_ref, x1_ref, t_ref, g_ref, loss_ref, dy_ref, dff_ref, dg_ref):
        _acc_init(pl.program_id(0), loss_ref, dg_ref)
        ff_v = ff_ref[...]
        err = x1_ref[...] + _rms(ff_v, g_ref[...]) - t_ref[...]
        tok = jnp.mean(err * err, axis=-1, keepdims=True)
        loss_ref[...] += 0.5 * jnp.sum(tok, axis=0, keepdims=True)
        dy = err * (1.0 / d)
        dy_ref[...] = dy
        dff, dg = _rms_bwd(ff_v, g_ref[...], dy)
        dff_ref[...] = dff.astype(BF16)
        dg_ref[...] += dg

    return pl.pallas_call(
        body, name="final", grid=(s // ts,),
        in_specs=[_rows(ts, d), _rows(ts, d), _rows(ts, d), _vec(d)],
        out_specs=[_vec(LANES), _rows(ts, d), _rows(ts, d), _vec(d)],
        out_shape=[jax.ShapeDtypeStruct((1, LANES), F32), jax.ShapeDtypeStruct((s, d), F32),
                   jax.ShapeDtypeStruct((s, d), BF16), jax.ShapeDtypeStruct((1, d), F32)],
        compiler_params=_params(("arbitrary",)),
    )(ff, x1, tgt, g)


def _pre_bwd(dx1, dh, x, g):
    s, d = x.shape
    ts = _pick(s, (256, 128))

    def body(dx1_ref, dh_ref, x_ref, g_ref, dx_ref, dg_ref):
        _acc_init(pl.program_id(0), dg_ref)
        dxa, dg = _rms_bwd(x_ref[...], g_ref[...], dh_ref[...])
        dx_ref[...] = dx1_ref[...] + dxa
        dg_ref[...] += dg

    return pl.pallas_call(
        body, name="pre_bwd", grid=(s // ts,),
        in_specs=[_rows(ts, d), _rows(ts, d), _rows(ts, d), _vec(d)],
        out_specs=[_rows(ts, d), _vec(d)],
        out_shape=[jax.ShapeDtypeStruct((s, d), F32), jax.ShapeDtypeStruct((1, d), F32)],
        compiler_params=_params(("arbitrary",)),
    )(dx1, dh, x, g)


def _local_step(x, pos, tgt, vecs, in_weights_fn, mix_weights_fn, up_weights_fn, down_weights_fn, grads_fn):
    c = vecs["conv_b"].shape[1]
    ql = vecs["q_norm"].shape[1]
    kvl = vecs["kv_norm"].shape[1]
    half = jnp.arange(0, QK_ROPE, 2, dtype=F32)
    freq = ROPE_THETA ** (-half / QK_ROPE)
    inv_freq = jnp.concatenate([freq, freq, jnp.zeros((LANES - QK_ROPE,), F32)])[None, :]
    tabs = _rope_tables(pos, inv_freq)

    h = _pre_fwd(x, vecs["pre_mix_norm"])
    w_in_t, zero = in_weights_fn(h)
    z = _mm(h, w_in_t, "nt", "mm_z")
    w_uq_t, w_ukv, conv_w, w_out = mix_weights_fn(z)
    u0, qn, kvn, kpe = _split_fwd(z, vecs["q_norm"] + zero, vecs["kv_norm"], tabs, c, ql, kvl)
    u1 = _conv_fwd(u0, conv_w, vecs["conv_b"])
    q = _q_rope(_mm(qn, w_uq_t, "nt", "mm_q"), tabs, False, BF16, "q_rope")
    kv = _mm(kvn, w_ukv, "nn", "mm_kv")
    attn, lse = _attn_fwd(q, kv, kpe)
    mixin = _mix_fwd(u1, vecs["conv_ln_g"], vecs["conv_ln_b"], vecs["conv_out_norm"], attn, vecs["attn_out_norm"])
    mix = _mm(mixin, w_out, "nn", "mm_mix")
    x1, hf = _post_mix_fwd(x, mix, vecs["post_mix_norm"], vecs["pre_ffn_norm"])
    w_gate, w_up = up_weights_fn(mix)
    gate, up, act = _ffn_up(hf, w_gate, w_up)
    w_down = down_weights_fn(act)
    ff = _ffn_down([act], [w_down], "ffn_down")
    loss, dy, dff, d_post_ffn = _final(ff, x1, tgt, vecs["post_ffn_norm"])

    g = {"post_ffn_norm": d_post_ffn}
    zero = grads_fn("down", ("w_down",), _ffn_dw([act], dff, "ffn_dw_down"), dff)
    dgate, dup = _ffn_down_bwd(dff, w_down, gate, up, jnp.zeros((8, LANES), F32) + zero)
    zero = zero + grads_fn("up", ("w_gate", "w_up"), _ffn_dw([dgate, dup], hf, "ffn_dw_up"), dgate)
    dhf = _ffn_down([dgate, dup], [w_gate, w_up], "ffn_dhf")
    dx1, dmix, g["pre_ffn_norm"], g["post_mix_norm"] = _post_mix_bwd(
        dy, dhf, x1, vecs["pre_ffn_norm"] + zero, mix, vecs["post_mix_norm"])
    dmixin = _mm(dmix, w_out, "nt", "mm_dmixin")
    dw_out = _mm(mixin, dmix, "tn", "mm_dw_out", BF16)
    zero = grads_fn("out", ("w_out",), [dw_out], dmix)
    du1, dattn, g["conv_ln_g"], g["conv_ln_b"], g["conv_out_norm"], g["attn_out_norm"] = _mix_bwd(
        dmixin, u1, vecs["conv_ln_g"] + zero, vecs["conv_ln_b"], vecs["conv_out_norm"], attn, vecs["attn_out_norm"])
    du0, dw_conv, g["conv_b"] = _conv_bwd(du1, u0, conv_w)
    dq, dkv, dkpe_h = _attn_bwd(q, kv, kpe, attn, dattn, lse)
    dqpre = _q_rope(dq, tabs, True, BF16, "q_rope_bwd")
    dqn = _mm(dqpre, w_uq_t, "nn", "mm_dqn")
    dw_uq = _mm(dqpre, qn, "tn", "mm_dw_uq", BF16)
    dkvn = _mm(dkv, w_ukv, "nt", "mm_dkvn")
    dw_ukv = _mm(kvn, dkv, "tn", "mm_dw_ukv", BF16)
    zero = grads_fn("", (), [], dkvn)
    dz, g["q_norm"], g["kv_norm"] = _split_bwd(du0, z, dqn, dkvn, dkpe_h, vecs["q_norm"] + zero, vecs["kv_norm"], tabs,
                                               c, ql, kvl)
    dw_in = _mm(dz, h, "tn", "mm_dw_in", BF16)
    zero = grads_fn("in", ("w_in", "w_uq", "w_ukv", "conv_w"), [dw_in, dw_uq, dw_ukv, dw_conv], dz)
    dh = _mm(dz, w_in_t, "nn", "mm_dh")
    zero = zero + grads_fn("", (), [], dh)
    grad_x, g["pre_mix_norm"] = _pre_bwd(dx1, dh, x, vecs["pre_mix_norm"] + zero)
    return loss, grad_x, g


def _my_index():
    return 4 * lax.axis_index("x") + 2 * lax.axis_index("y") + lax.axis_index("c")


def _coords(idx):
    return ((idx >> 2) & 1, (idx >> 1) & 1, idx & 1)


def _place():
    x, y, c = lax.axis_index("x"), lax.axis_index("y"), lax.axis_index("c")
    return (x, y, c), (x, y, 1 - c), [(1 - x, y), (x, 1 - y), (1 - x, 1 - y)]


def _small_copies(src, land, send_sem, recv_sem):
    me = _my_index()
    return [pltpu.make_async_remote_copy(src_ref=src, dst_ref=land.at[me], send_sem=send_sem, recv_sem=recv_sem,
                                         device_id=_coords(me ^ p), device_id_type=MESH)
            for p in range(1, N_DEV)]


HBM_SPEC = pl.BlockSpec(memory_space=pltpu.HBM)
SEM_SPEC = pl.BlockSpec(memory_space=pltpu.SEMAPHORE)
DATAFLOW = pltpu.SideEffectType.DATAFLOW_SIDE_EFFECTING


def _split_start(name, copies_of, srcs, lands, after):
    n = len(srcs)

    def body(*refs):
        outs = refs[2 * n + 1:]
        for k in range(n):
            for cp in copies_of(refs[k], refs[n + k], outs[k], outs[n + k]):
                cp.start()
        outs[-1][...] = jnp.zeros_like(outs[-1])

    hbm = lambda a: pltpu.HBM(a.shape, a.dtype)
    out = pl.pallas_call(
        body, name=name,
        in_specs=[HBM_SPEC] * (2 * n + 1),
        out_specs=[SEM_SPEC] * (2 * n) + [HBM_SPEC] * (2 * n) + [pl.BlockSpec(memory_space=pltpu.VMEM)],
        out_shape=[pltpu.SemaphoreType.DMA(())] * (2 * n) + [hbm(a) for a in srcs] + [hbm(a) for a in lands]
        + [jax.ShapeDtypeStruct((8, LANES), F32)],
        input_output_aliases={k: 2 * n + k for k in range(2 * n)},
        compiler_params=pltpu.CompilerParams(has_side_effects=DATAFLOW),
    )(*[pltpu.with_memory_space_constraint(a, pltpu.HBM) for a in list(srcs) + list(lands) + [after]])
    return (out[:n], out[n:2 * n], out[2 * n:3 * n], out[3 * n:4 * n]), out[-1][0, 0]


def _split_wait(name, n_copies, started, after):
    send_sems, recv_sems, srcs, lands = started
    n = len(srcs)

    def body(*refs):
        for k in range(n):
            slots = refs[n + k].at[pl.ds(0, n_copies)]
            all_copies = pltpu.make_async_remote_copy(
                src_ref=slots, dst_ref=slots, send_sem=refs[2 * n + k], recv_sem=refs[3 * n + k],
                device_id=_place()[0], device_id_type=MESH)
            all_copies.wait_send()
            all_copies.wait_recv()

    hbm = lambda a: pltpu.HBM(a.shape, a.dtype)
    out = pl.pallas_call(
        body, name=name,
        in_specs=[HBM_SPEC] * (2 * n) + [SEM_SPEC] * (2 * n) + [HBM_SPEC],
        out_specs=[HBM_SPEC] * (2 * n),
        out_shape=[hbm(a) for a in srcs] + [hbm(a) for a in lands],
        input_output_aliases={k: k for k in range(2 * n)},
        compiler_params=pltpu.CompilerParams(has_side_effects=DATAFLOW),
    )(*srcs, *lands, *send_sems, *recv_sems, pltpu.with_memory_space_constraint(after, pltpu.HBM))
    return out[:n], out[n:]


def _slot(chip, core):
    return 4 * chip[0] + 2 * chip[1] + core


def _gather_copies(src, land, send_sem, recv_sem):
    (x, y, c), sib, chips = _place()
    return [pltpu.make_async_remote_copy(src_ref=src, dst_ref=land.at[_slot((x, y), c)], send_sem=send_sem,
                                         recv_sem=recv_sem, device_id=to, device_id_type=MESH)
            for to in [sib] + [(*chip, c) for chip in chips]]


def _gather_pass_on(lands, name):
    n = len(lands)

    def body(*refs):
        ins, outs = refs[:n], refs[n:2 * n]
        send_sems, recv_sems = refs[2 * n:]
        (x, y, c), sib, chips = _place()
        sends = []
        for k in range(n):
            for j, chip in enumerate(chips):
                sends.append(pltpu.make_async_remote_copy(
                    src_ref=ins[k].at[_slot(chip, c)], dst_ref=outs[k].at[_slot(chip, c)],
                    send_sem=send_sems.at[k, j], recv_sem=recv_sems.at[k, j], device_id=sib, device_id_type=MESH))
        for cp in sends:
            cp.start()
        for cp in sends:
            cp.wait_recv()
        for cp in sends:
            cp.wait_send()

    any_spec = pl.BlockSpec(memory_space=pl.ANY)
    return pl.pallas_call(
        body, name=name,
        in_specs=[any_spec] * n, out_specs=[any_spec] * n,
        out_shape=[jax.ShapeDtypeStruct(a.shape, a.dtype) for a in lands],
        input_output_aliases={k: k for k in range(n)},
        scratch_shapes=[pltpu.SemaphoreType.DMA((n, 3))] * 2,
        compiler_params=pltpu.CompilerParams(has_side_effects=True),
    )(*lands)


def _chip_copies(src, land, send_sem, recv_sem):
    (x, y, c), _, chips = _place()
    return [pltpu.make_async_remote_copy(src_ref=src.at[2 * chip[0] + chip[1]], dst_ref=land.at[j], send_sem=send_sem,
                                         recv_sem=recv_sem, device_id=(*chip, c), device_id_type=MESH)
            for j, chip in enumerate(chips)]


ROW_TILE_BYTES = 14 * 1024 * 1024


def _stream_tile(r, c, bytes_per_elem):
    if r * c * bytes_per_elem <= ROW_TILE_BYTES:
        return r, c
    rows = [t for t in range(16, r, 16) if r % t == 0 and t * c * bytes_per_elem <= ROW_TILE_BYTES]
    if rows:
        return max(rows), c
    cols = [t for t in range(LANES, c, LANES) if c % t == 0 and r * t * bytes_per_elem <= ROW_TILE_BYTES]
    return r, max(cols)


def _sibling_copies(src, land, send_sem, recv_sem):
    (x, y, c), sib, _ = _place()
    return [pltpu.make_async_remote_copy(src_ref=src.at[2 * q + 1 - c], dst_ref=land.at[q], send_sem=send_sem,
                                         recv_sem=recv_sem, device_id=sib, device_id_type=MESH)
            for q in range(4)]


def _pair_sum(core, blocks, theirs, name):
    q, r, c = theirs.shape
    tr, tc = _stream_tile(r, c, 3 * theirs.dtype.itemsize)

    def body(core_ref, a_ref, b_ref, o_ref):
        o_ref[...] = (a_ref[...].astype(F32) + b_ref[...].astype(F32)).astype(o_ref.dtype)

    blk = pl.BlockSpec((1, tr, tc), lambda i, j, k, core_ref: (i, j, k))
    mine = pl.BlockSpec((1, tr, tc), lambda i, j, k, core_ref: (2 * i + core_ref[0], j, k))
    return pl.pallas_call(
        body, name=name,
        grid_spec=pltpu.PrefetchScalarGridSpec(num_scalar_prefetch=1, grid=(q, r // tr, c // tc),
                                               in_specs=[mine, blk], out_specs=blk),
        out_shape=jax.ShapeDtypeStruct(theirs.shape, theirs.dtype),
        compiler_params=_params(("parallel", "parallel", "parallel")),
    )(core, blocks, theirs)


def _reduce_adamw(parts, w, m, v, name, own=None, own_slot=None):
    r, c = w.shape
    n_parts = parts.shape[0]
    tr, tc = _stream_tile(r, c, (n_parts + 1) * parts.dtype.itemsize + 7 * 4)
    c1 = 1.0 - ADAM_B1
    c2 = 1.0 - ADAM_B2
    bc1 = 1.0 - ADAM_B1 ** ADAM_STEP
    bc2 = 1.0 - ADAM_B2 ** ADAM_STEP

    def body(*refs):
        if own is None:
            p_ref, w_ref, m_ref, v_ref, g_ref, d_ref, nm_ref, nv_ref = refs
            g = p_ref[0].astype(F32)
            first = 1
        else:
            _, o_ref, p_ref, w_ref, m_ref, v_ref, g_ref, d_ref, nm_ref, nv_ref = refs
            g = o_ref[0].astype(F32)
            first = 0
        for j in range(first, n_parts):
            g = g + p_ref[j].astype(F32)
        nm = ADAM_B1 * m_ref[...] + c1 * g
        nv = ADAM_B2 * v_ref[...] + c2 * (g * g)
        g_ref[...] = g
        nm_ref[...] = nm
        nv_ref[...] = nv
        d_ref[...] = -ADAM_LR * ((nm / bc1) / (jnp.sqrt(nv / bc2) + ADAM_EPS) + ADAM_WD * w_ref[...])

    out = jax.ShapeDtypeStruct((r, c), F32)
    grid = (r // tr, c // tc)
    if own is None:
        blk = pl.BlockSpec((tr, tc), lambda i, j: (i, j))
        return pl.pallas_call(
            body, name=name, grid=grid,
            in_specs=[pl.BlockSpec((n_parts, tr, tc), lambda i, j: (0, i, j)), blk, blk, blk],
            out_specs=[blk] * 4, out_shape=[out] * 4,
            compiler_params=_params(("parallel", "parallel")),
        )(parts, w, m, v)
    blk = pl.BlockSpec((tr, tc), lambda i, j, slot_ref: (i, j))
    return pl.pallas_call(
        body, name=name,
        grid_spec=pltpu.PrefetchScalarGridSpec(
            num_scalar_prefetch=1, grid=grid,
            in_specs=[pl.BlockSpec((1, tr, tc), lambda i, j, slot_ref: (slot_ref[0], i, j)),
                      pl.BlockSpec((n_parts, tr, tc), lambda i, j, slot_ref: (0, i, j)), blk, blk, blk],
            out_specs=[blk] * 4),
        out_shape=[out] * 4,
        compiler_params=_params(("parallel", "parallel")),
    )(own_slot, own, parts, w, m, v)


_MIX = ("w_in", "w_uq", "w_ukv", "conv_w", "w_out")
_FFN = ("w_gate", "w_up", "w_down")
_BIG = _MIX + _FFN
_TRANSPOSED = ("w_in", "w_uq", "w_gate", "w_up")
_SMALL = ("pre_mix_norm", "q_norm", "kv_norm", "conv_b", "conv_ln_g", "conv_ln_b", "conv_out_norm",
          "attn_out_norm", "post_mix_norm", "pre_ffn_norm", "post_ffn_norm")
_ORDER = ("pre_mix_norm", "w_in", "q_norm", "w_uq", "kv_norm", "w_ukv", "conv_w", "conv_b", "conv_ln_g",
          "conv_ln_b", "conv_out_norm", "attn_out_norm", "w_out", "post_mix_norm", "pre_ffn_norm", "w_gate",
          "w_up", "w_down", "post_ffn_norm")


def _cols_from_shards(g):
    return jnp.transpose(g, (1, 0, 2)).reshape(g.shape[1], N_DEV * g.shape[2])


def _cols_to_shards(w):
    k, n8 = w.shape
    return jnp.transpose(w.reshape(k, N_DEV, n8 // N_DEV), (1, 0, 2))


def _step(x, positions, loss_target, w, m, v):
    s, d = x.shape[1], x.shape[2]
    x2, tgt = x[0], loss_target[0]
    pos = positions.reshape(s, 1)
    vecs = {n: w[n] for n in _SMALL}
    core = lax.axis_index("c").astype(jnp.int32).reshape(1)
    my_chip = (2 * lax.axis_index("x") + lax.axis_index("y")).astype(jnp.int32).reshape(1)
    n_in_cols = N_DEV * w["w_in"].shape[2]
    gathers, scatters, to_sibling = {}, {}, []

    def shard(t, n):
        return t[n][0].T if n in _TRANSPOSED else t[n][0]

    def gather_start(names, tag, after, zero=0.0):
        srcs = [w[n][0] if n == "conv_w" else (shard(w, n) + zero).astype(BF16) for n in names]
        lands = [lax.empty((N_DEV,) + a.shape, a.dtype) for a in srcs]
        gathers[tag], zero = _split_start("gather_" + tag + "_start", _gather_copies, srcs, lands, after)
        return zero

    def gather_finish(names, tag, after):
        srcs, lands = _split_wait("gather_" + tag + "_wait", 4, gathers[tag], after)
        lands = _gather_pass_on(lands, "gather_" + tag + "_pass_on")
        me = _my_index()
        return {n: lax.dynamic_update_slice(g, a[None], (me,) + (0,) * a.ndim) for n, g, a in zip(names, lands, srcs)}

    def in_weights_fn(h):
        w_in_g = gather_finish(("w_in",), "in", h)["w_in"]
        return jnp.pad(w_in_g.reshape(-1, d), ((0, LANES - QK_ROPE), (0, 0))), 0.0

    def mix_weights_fn(z):
        gath = gather_finish(_MIX[1:], "mix", z)
        w_uq_t = jnp.pad(gath["w_uq"], ((0, 0), (0, HEAD_PAD - QK_HEAD), (0, 0))).reshape(N_HEADS * HEAD_PAD, -1)
        return (w_uq_t, _cols_from_shards(gath["w_ukv"]), _cols_from_shards(gath["conv_w"]),
                gath["w_out"].reshape(-1, d))

    def up_weights_fn(mix):
        gath = gather_finish(("w_gate", "w_up"), "up", mix)
        return gath["w_gate"], gath["w_up"]

    def down_weights_fn(act):
        return gather_finish(("w_down",), "down", act)["w_down"]

    def scatter_advance(after):
        if not to_sibling:
            return 0.0
        names, tag, sent = to_sibling.pop()
        blocks, theirs = _split_wait("to_sibling_" + tag + "_wait", 4, sent, after)
        pairs = [_pair_sum(core, b, t, "pair_sum_" + n) for n, b, t in zip(names, blocks, theirs)]
        lands = [lax.empty((3,) + p.shape[1:], p.dtype) for p in pairs]
        scatters[tag], zero = _split_start("scatter_" + tag + "_start", _chip_copies, pairs, lands, theirs[0])
        return zero

    to_blocks = {
        "w_in": lambda a: a[:n_in_cols].reshape(N_DEV, -1, d),
        "w_uq": lambda a: a.reshape(N_HEADS, HEAD_PAD, -1)[:, :QK_HEAD],
        "w_ukv": _cols_to_shards, "conv_w": _cols_to_shards,
        "w_out": lambda a: a.reshape(N_DEV, -1, d),
    }

    def grads_fn(tag, names, grads, after):
        zero = scatter_advance(after)
        if not grads:
            return zero
        blocks = [to_blocks.get(n, lambda a: a)(a) for n, a in zip(names, grads)]
        lands = [lax.empty((4,) + b.shape[1:], b.dtype) for b in blocks]
        sent, zero2 = _split_start("to_sibling_" + tag + "_start", _sibling_copies, blocks, lands, after)
        to_sibling.append((names, tag, sent))
        return zero + zero2

    zero = gather_start(("w_in",), "in", x2)
    zero = gather_start(_MIX[1:], "mix", x2, zero)
    zero = gather_start(("w_gate", "w_up"), "up", x2, zero)
    vecs["pre_mix_norm"] = vecs["pre_mix_norm"] + gather_start(("w_down",), "down", x2, zero)
    loss, grad_x, g = _local_step(x2, pos, tgt, vecs, in_weights_fn, mix_weights_fn, up_weights_fn,
                                  down_weights_fn, grads_fn)

    small = jnp.concatenate([g[n] for n in _SMALL], axis=1)
    small_started, zero = _split_start("gather_small_start", _small_copies, [small],
                                       [lax.empty((N_DEV,) + small.shape, F32)], grad_x)

    res = {}
    after = grad_x
    my_chip = my_chip + zero.astype(jnp.int32)
    for tag, names in (("down", ("w_down",)), ("up", ("w_gate", "w_up")), ("out", ("w_out",)),
                       ("in", ("w_in", "w_uq", "w_ukv", "conv_w"))):
        pairs, recv = _split_wait("scatter_" + tag + "_wait", 3, scatters[tag], after)
        for n, own, parts in zip(names, pairs, recv):
            res[n] = _reduce_adamw(parts, shard(w, n), shard(m, n), shard(v, n), "adamw_" + n, own=own,
                                   own_slot=my_chip)
            after = res[n][1]
            res[n] = [(t.T if n in _TRANSPOSED else t)[None] for t in res[n]]
    (small,), (small_all,) = _split_wait("gather_small_wait", N_DEV - 1, small_started, after)
    small_all = lax.dynamic_update_slice(small_all, small[None], (_my_index(), 0, 0))
    cat = lambda t: jnp.concatenate([t[n] for n in _SMALL], axis=1)
    sg, sd, sm, sv = _reduce_adamw(small_all, cat(w), cat(m), cat(v), "adamw_small")
    off = 0
    for n in _SMALL:
        width = w[n].shape[1]
        res[n] = [t[:, off:off + width] for t in (sg, sd, sm, sv)]
        off += width

    total = lax.psum(loss[0, 0], ("x", "y", "c"))
    outs = [total, grad_x[None]]
    for part in range(4):
        outs.extend(res[n][part] for n in _ORDER)
    return tuple(outs)


def kernel(x, positions, pre_mix_norm, w_in, q_norm, w_uq, kv_norm, w_ukv, conv_w, conv_b, conv_ln_g, conv_ln_b, conv_out_norm, attn_out_norm, w_out, post_mix_norm, pre_ffn_norm, w_gate, w_up, w_down, post_ffn_norm, loss_target, m_pre_mix_norm, m_w_in, m_q_norm, m_w_uq, m_kv_norm, m_w_ukv, m_conv_w, m_conv_b, m_conv_ln_g, m_conv_ln_b, m_conv_out_norm, m_attn_out_norm, m_w_out, m_post_mix_norm, m_pre_ffn_norm, m_w_gate, m_w_up, m_w_down, m_post_ffn_norm, v_pre_mix_norm, v_w_in, v_q_norm, v_w_uq, v_kv_norm, v_w_ukv, v_conv_w, v_conv_b, v_conv_ln_g, v_conv_ln_b, v_conv_out_norm, v_attn_out_norm, v_w_out, v_post_mix_norm, v_pre_ffn_norm, v_w_gate, v_w_up, v_w_down, v_post_ffn_norm):
    w = dict(zip(_ORDER, (pre_mix_norm, w_in, q_norm, w_uq, kv_norm, w_ukv, conv_w, conv_b, conv_ln_g, conv_ln_b,
                          conv_out_norm, attn_out_norm, w_out, post_mix_norm, pre_ffn_norm, w_gate, w_up, w_down,
                          post_ffn_norm)))
    m = dict(zip(_ORDER, (m_pre_mix_norm, m_w_in, m_q_norm, m_w_uq, m_kv_norm, m_w_ukv, m_conv_w, m_conv_b,
                          m_conv_ln_g, m_conv_ln_b, m_conv_out_norm, m_attn_out_norm, m_w_out, m_post_mix_norm,
                          m_pre_ffn_norm, m_w_gate, m_w_up, m_w_down, m_post_ffn_norm)))
    v = dict(zip(_ORDER, (v_pre_mix_norm, v_w_in, v_q_norm, v_w_uq, v_kv_norm, v_w_ukv, v_conv_w, v_conv_b,
                          v_conv_ln_g, v_conv_ln_b, v_conv_out_norm, v_attn_out_norm, v_w_out, v_post_mix_norm,
                          v_pre_ffn_norm, v_w_gate, v_w_up, v_w_down, v_post_ffn_norm)))
    return _step(x, positions, loss_target, w, m, v)
```

```python
import functools

import jax
import jax.numpy as jnp
from jax import lax
from jax.experimental import pallas as pl
from jax.experimental.pallas import tpu as pltpu

N_DEV = 8
N_HEADS = 8
QK_NOPE = 128
QK_ROPE = 64
V_HEAD = 128
QK_HEAD = QK_NOPE + QK_ROPE
HEAD_PAD = 256
LANES = 128
ATTN_BLOCK = 512
CONV_K = 31
CONV_PAD = 32
EPS = 1e-6
ROPE_THETA = 10000.0
ADAM_LR = 0.001
ADAM_B1 = 0.9
ADAM_B2 = 0.999
ADAM_EPS = 1e-08
ADAM_WD = 0.01
ADAM_STEP = 10
VMEM_LIMIT = 56 * 1024 * 1024
F32 = jnp.float32
BF16 = jnp.bfloat16
MESH = pl.DeviceIdType.MESH
NEG = -1e30


def _pick(n, prefs):
    for p in prefs:
        if p <= n and n % p == 0:
            return p
    return n


def _params(sem):
    return pltpu.CompilerParams(dimension_semantics=sem, vmem_limit_bytes=VMEM_LIMIT)


_DIMS = {"nn": (((1,), (0,)), ((), ())), "nt": (((1,), (1,)), ((), ())), "tn": (((0,), (0,)), ((), ()))}


MM_VMEM_BUDGET = 40 * 1024 * 1024
MM_MAX_MACS = 3 * 1024 ** 3


V7X_HBM_BYTES_PER_S = 3.0e12
V7X_MXU_MACS_PER_S = 0.45e15
GRID_STEP_S = 0.35e-6


def _mm_tiles(m, n, k, size_a, size_b, size_o):
    best = None
    for tm in sorted({m, 1024, 512, 256, 128}, reverse=True):
        if tm > m or m % tm:
            continue
        for tn in sorted({n, 2048, 1024, 512, 384, 256, 128}, reverse=True):
            if tn > n or n % tn:
                continue
            vmem = 2 * (tm * k * size_a + k * tn * size_b + tm * tn * size_o)
            if vmem > MM_VMEM_BUDGET or tm * tn * k > MM_MAX_MACS:
                continue
            b_reads = 1 if tn == n else m // tm
            traffic = m * k * size_a + b_reads * k * n * size_b + m * n * size_o
            exposed = tm * k * size_a + k * tn * size_b + tm * tn * size_o
            steps = (m // tm) * (n // tn)
            key = (max(traffic / V7X_HBM_BYTES_PER_S, m * n * k / V7X_MXU_MACS_PER_S)
                   + exposed / V7X_HBM_BYTES_PER_S + steps * GRID_STEP_S)
            if best is None or key < best[0]:
                best = (key, tm, tn)
    assert best is not None, (m, n, k)
    return best[1], best[2]


def _mm(a, b, mode, name, out_dtype=F32, add=None):
    if mode == "nn":
        (m, k), (k2, n) = a.shape, b.shape
    elif mode == "nt":
        (m, k), (n, k2) = a.shape, b.shape
    else:
        (k, m), (k2, n) = a.shape, b.shape
    assert k == k2, (a.shape, b.shape, mode)
    tm, tn = _mm_tiles(m, n, k, a.dtype.itemsize, b.dtype.itemsize,
                       jnp.dtype(out_dtype).itemsize + (0 if add is None else 4))
    dims = _DIMS[mode]

    def body(a_ref, b_ref, *rest):
        acc = lax.dot_general(a_ref[...].astype(BF16), b_ref[...].astype(BF16), dims, preferred_element_type=F32)
        if add is not None:
            acc = acc + rest[0][...]
        rest[-1][...] = acc.astype(rest[-1].dtype)

    if mode == "tn":
        a_spec = pl.BlockSpec((k, tm), lambda i, j: (0, i))
    else:
        a_spec = pl.BlockSpec((tm, k), lambda i, j: (i, 0))
    if mode == "nt":
        b_spec = pl.BlockSpec((tn, k), lambda i, j: (j, 0))
    else:
        b_spec = pl.BlockSpec((k, tn), lambda i, j: (0, j))
    o_spec = pl.BlockSpec((tm, tn), lambda i, j: (i, j))
    extra = [] if add is None else [add]
    return pl.pallas_call(
        body, name=name,
        grid=(m // tm, n // tn),
        in_specs=[a_spec, b_spec] + [o_spec] * len(extra),
        out_specs=o_spec,
        out_shape=jax.ShapeDtypeStruct((m, n), out_dtype),
        compiler_params=_params(("parallel", "parallel")),
    )(a, b, *extra)


def _sigmoid(x):
    return 1.0 / (1.0 + jnp.exp(-x))


def _rms(x, g):
    r = lax.rsqrt(jnp.mean(x * x, axis=-1, keepdims=True) + EPS)
    return (x * r) * g


def _rms_bwd(x, g, dy):
    r = lax.rsqrt(jnp.mean(x * x, axis=-1, keepdims=True) + EPS)
    xh = x * r
    dyg = dy * g
    dx = r * (dyg - xh * jnp.mean(dyg * xh, axis=-1, keepdims=True))
    return dx, jnp.sum(dy * xh, axis=0, keepdims=True)


def _ln(x, g, b):
    mu = jnp.mean(x, axis=-1, keepdims=True)
    xc = x - mu
    rs = lax.rsqrt(jnp.mean(xc * xc, axis=-1, keepdims=True) + EPS)
    return (xc * rs) * g + b


def _ln_bwd(x, g, dy):
    mu = jnp.mean(x, axis=-1, keepdims=True)
    xc = x - mu
    rs = lax.rsqrt(jnp.mean(xc * xc, axis=-1, keepdims=True) + EPS)
    xh = xc * rs
    dyg = dy * g
    dx = rs * (dyg - jnp.mean(dyg, axis=-1, keepdims=True) - xh * jnp.mean(dyg * xh, axis=-1, keepdims=True))
    return dx, jnp.sum(dy * xh, axis=0, keepdims=True), jnp.sum(dy, axis=0, keepdims=True)


def _silu(x):
    return x * _sigmoid(x)


def _silu_grad(x):
    s = _sigmoid(x)
    return s * (1.0 + x * (1.0 - s))


def _rope(x, cos, sa, sb):
    return x * cos + pltpu.roll(x, 96, 1) * sa + pltpu.roll(x, 32, 1) * sb


def _rope_t(d, cos, sa, sb):
    return d * cos - pltpu.roll(d, 96, 1) * sa - pltpu.roll(d, 32, 1) * sb


def _rows(ts, w):
    return pl.BlockSpec((ts, w), lambda i: (i, 0))


def _vec(w):
    return pl.BlockSpec((1, w), lambda i: (0, 0))


def _acc_init(i, *refs):
    @pl.when(i == 0)
    def _():
        for r in refs:
            r[...] = jnp.zeros_like(r)


def _rope_tables(pos, inv_freq):
    s = pos.shape[0]
    ts = _pick(s, (512, 256, 128))

    def body(p_ref, f_ref, c_ref, sa_ref, sb_ref):
        ang = p_ref[...].astype(F32) * f_ref[...]
        lane = lax.broadcasted_iota(jnp.int32, ang.shape, 1)
        c, sn = jnp.cos(ang), jnp.sin(ang)
        c_ref[...] = jnp.where(lane < QK_ROPE, c, 0.0)
        sa_ref[...] = jnp.where(lane < QK_ROPE // 2, -sn, 0.0)
        sb_ref[...] = jnp.where((lane >= QK_ROPE // 2) & (lane < QK_ROPE), sn, 0.0)

    out = jax.ShapeDtypeStruct((s, LANES), F32)
    return pl.pallas_call(
        body, name="rope_tables", grid=(s // ts,),
        in_specs=[_rows(ts, 1), _vec(LANES)],
        out_specs=[_rows(ts, LANES)] * 3, out_shape=[out] * 3,
        compiler_params=_params(("parallel",)),
    )(pos, inv_freq)


def _pre_fwd(x, g):
    s, d = x.shape
    ts = _pick(s, (256, 128))

    def body(x_ref, g_ref, h_ref):
        h_ref[...] = _rms(x_ref[...], g_ref[...]).astype(BF16)

    return pl.pallas_call(
        body, name="pre_fwd", grid=(s // ts,),
        in_specs=[_rows(ts, d), _vec(d)], out_specs=_rows(ts, d),
        out_shape=jax.ShapeDtypeStruct((s, d), BF16),
        compiler_params=_params(("parallel",)),
    )(x, g)


def _split_fwd(z, gq, gkv, tabs, c, ql, kvl):
    s, zw = z.shape
    ts = _pick(s, (256, 128))
    o_q, o_kv, o_kr = 2 * c, 2 * c + ql, 2 * c + ql + kvl

    def body(z_ref, gq_ref, gkv_ref, c_ref, sa_ref, sb_ref, u0_ref, qn_ref, kvn_ref, kpe_ref):
        u0_ref[...] = z_ref[:, 0:c] * _sigmoid(z_ref[:, c:2 * c])
        qn_ref[...] = _rms(z_ref[:, o_q:o_kv], gq_ref[...]).astype(BF16)
        kvn_ref[...] = _rms(z_ref[:, o_kv:o_kr], gkv_ref[...]).astype(BF16)
        kpe_ref[...] = _rope(z_ref[:, o_kr:o_kr + LANES], c_ref[...], sa_ref[...], sb_ref[...]).astype(BF16)

    return pl.pallas_call(
        body, name="split_fwd", grid=(s // ts,),
        in_specs=[_rows(ts, zw), _vec(ql), _vec(kvl)] + [_rows(ts, LANES)] * 3,
        out_specs=[_rows(ts, c), _rows(ts, ql), _rows(ts, kvl), _rows(ts, LANES)],
        out_shape=[jax.ShapeDtypeStruct((s, c), F32), jax.ShapeDtypeStruct((s, ql), BF16),
                   jax.ShapeDtypeStruct((s, kvl), BF16), jax.ShapeDtypeStruct((s, LANES), BF16)],
        compiler_params=_params(("parallel",)),
    )(z, gq, gkv, *tabs)


def _split_bwd(du0, z, dqn, dkvn, dkpe_h, gq, gkv, tabs, c, ql, kvl):
    s, zw = z.shape
    ts = _pick(s, (256, 128))
    o_q, o_kv, o_kr = 2 * c, 2 * c + ql, 2 * c + ql + kvl

    def body(du0_ref, z_ref, dqn_ref, dkvn_ref, dkh_ref, gq_ref, gkv_ref, c_ref, sa_ref, sb_ref,
             dz_ref, dgq_ref, dgkv_ref):
        _acc_init(pl.program_id(0), dgq_ref, dgkv_ref)
        du0 = du0_ref[...]
        a = z_ref[:, 0:c]
        sg = _sigmoid(z_ref[:, c:2 * c])
        dz_ref[:, 0:c] = (du0 * sg).astype(BF16)
        dz_ref[:, c:2 * c] = (du0 * a * sg * (1.0 - sg)).astype(BF16)
        dq, dgq = _rms_bwd(z_ref[:, o_q:o_kv], gq_ref[...], dqn_ref[...])
        dz_ref[:, o_q:o_kv] = dq.astype(BF16)
        dgq_ref[...] += dgq
        dkv, dgkv = _rms_bwd(z_ref[:, o_kv:o_kr], gkv_ref[...], dkvn_ref[...])
        dz_ref[:, o_kv:o_kr] = dkv.astype(BF16)
        dgkv_ref[...] += dgkv
        dk = dkh_ref[:, 0:LANES]
        for h in range(1, N_HEADS):
            dk = dk + dkh_ref[:, h * LANES:(h + 1) * LANES]
        dz_ref[:, o_kr:o_kr + LANES] = _rope_t(dk, c_ref[...], sa_ref[...], sb_ref[...]).astype(BF16)

    return pl.pallas_call(
        body, name="split_bwd", grid=(s // ts,),
        in_specs=[_rows(ts, c), _rows(ts, zw), _rows(ts, ql), _rows(ts, kvl), _rows(ts, N_HEADS * LANES),
                  _vec(ql), _vec(kvl)] + [_rows(ts, LANES)] * 3,
        out_specs=[_rows(ts, zw), _vec(ql), _vec(kvl)],
        out_shape=[jax.ShapeDtypeStruct((s, zw), BF16), jax.ShapeDtypeStruct((1, ql), F32),
                   jax.ShapeDtypeStruct((1, kvl), F32)],
        compiler_params=_params(("arbitrary",)),
    )(du0, z, dqn, dkvn, dkpe_h, gq, gkv, *tabs)


def _q_rope(qpre, tabs, transpose, out_dtype, name):
    s, w = qpre.shape
    ts = _pick(s, (256, 128))
    rot = _rope_t if transpose else _rope

    def body(q_ref, c_ref, sa_ref, sb_ref, o_ref):
        cs, sa, sb = c_ref[...], sa_ref[...], sb_ref[...]
        for h in range(N_HEADS):
            lo = h * HEAD_PAD
            o_ref[:, lo:lo + QK_NOPE] = q_ref[:, lo:lo + QK_NOPE].astype(out_dtype)
            o_ref[:, lo + QK_NOPE:lo + HEAD_PAD] = rot(q_ref[:, lo + QK_NOPE:lo + HEAD_PAD], cs, sa, sb).astype(out_dtype)

    return pl.pallas_call(
        body, name=name, grid=(s // ts,),
        in_specs=[_rows(ts, w)] + [_rows(ts, LANES)] * 3, out_specs=_rows(ts, w),
        out_shape=jax.ShapeDtypeStruct((s, w), out_dtype),
        compiler_params=_params(("parallel",)),
    )(qpre, *tabs)


def _conv_fwd(u0, w, b):
    s, c = u0.shape
    tc = LANES
    rc = _pick(s, (256, 128))

    def body(u_ref, w_ref, b_ref, o_ref, pad_ref):
        pad_ref[0:CONV_PAD, :] = jnp.zeros((CONV_PAD, tc), F32)
        pad_ref[CONV_PAD:CONV_PAD + s, :] = u_ref[...]
        for r in range(s // rc):
            acc = jnp.broadcast_to(b_ref[...], (rc, tc))
            for k in range(CONV_K):
                lo = r * rc + CONV_PAD - (CONV_K - 1) + k
                acc = acc + w_ref[k:k + 1, :] * pad_ref[lo:lo + rc, :]
            o_ref[r * rc:(r + 1) * rc, :] = acc

    col = lambda j: (0, j)
    return pl.pallas_call(
        body, name="conv_fwd", grid=(c // tc,),
        in_specs=[pl.BlockSpec((s, tc), col), pl.BlockSpec((CONV_K, tc), col), pl.BlockSpec((1, tc), col)],
        out_specs=pl.BlockSpec((s, tc), col),
        out_shape=jax.ShapeDtypeStruct((s, c), F32),
        scratch_shapes=[pltpu.VMEM((s + CONV_PAD, tc), F32)],
        compiler_params=_params(("parallel",)),
    )(u0, w, b)


def _conv_bwd(du1, u0, w):
    s, c = u0.shape
    tc = LANES
    rc = _pick(s, (256, 128))

    def body(d_ref, u_ref, w_ref, du_ref, dw_ref, db_ref, upad_ref, dpad_ref):
        upad_ref[0:CONV_PAD, :] = jnp.zeros((CONV_PAD, tc), F32)
        upad_ref[CONV_PAD:CONV_PAD + s, :] = u_ref[...]
        dpad_ref[0:s, :] = d_ref[...]
        dpad_ref[s:s + CONV_PAD, :] = jnp.zeros((CONV_PAD, tc), F32)
        for r in range(s // rc):
            acc = jnp.zeros((rc, tc), F32)
            for k in range(CONV_K):
                lo = r * rc + (CONV_K - 1) - k
                acc = acc + w_ref[k:k + 1, :] * dpad_ref[lo:lo + rc, :]
            du_ref[r * rc:(r + 1) * rc, :] = acc
        for k in range(CONV_K):
            acc8 = jnp.zeros((8, tc), F32)
            for r in range(s // rc):
                lo = r * rc + CONV_PAD - (CONV_K - 1) + k
                prod = d_ref[r * rc:(r + 1) * rc, :] * upad_ref[lo:lo + rc, :]
                acc8 = acc8 + jnp.sum(prod.reshape(rc // 8, 8, tc), axis=0)
            dw_ref[k:k + 1, :] = jnp.sum(acc8, axis=0, keepdims=True)
        db_ref[...] = jnp.sum(d_ref[...], axis=0, keepdims=True)

    col = lambda j: (0, j)
    return pl.pallas_call(
        body, name="conv_bwd", grid=(c // tc,),
        in_specs=[pl.BlockSpec((s, tc), col), pl.BlockSpec((s, tc), col), pl.BlockSpec((CONV_K, tc), col)],
        out_specs=[pl.BlockSpec((s, tc), col), pl.BlockSpec((CONV_K, tc), col), pl.BlockSpec((1, tc), col)],
        out_shape=[jax.ShapeDtypeStruct((s, c), F32), jax.ShapeDtypeStruct((CONV_K, c), F32),
                   jax.ShapeDtypeStruct((1, c), F32)],
        scratch_shapes=[pltpu.VMEM((s + CONV_PAD, tc), F32), pltpu.VMEM((s + CONV_PAD, tc), F32)],
        compiler_params=_params(("parallel",)),
    )(du1, u0, w)


def _causal_mask(sc, qi, kj, tq, tk):
    rows = qi * tq + lax.broadcasted_iota(jnp.int32, sc.shape, 0)
    cols = kj * tk + lax.broadcasted_iota(jnp.int32, sc.shape, 1)
    return jnp.where(cols <= rows, sc, NEG)


def _attn_fwd(q, kv, kpe):
    s = q.shape[0]
    tq = tk = _pick(s, (ATTN_BLOCK, 256, 128))
    reps = tk // LANES
    scale = QK_HEAD ** -0.5
    nt = (((1,), (1,)), ((), ()))

    def body(q_ref, kn_ref, v_ref, kpe_ref, o_ref, lse_ref, kf_ref, vb_ref, m_ref, l_ref, acc_ref):
        i = pl.program_id(1)

        @pl.when(i == 0)
        def _():
            kf_ref[:, 0:QK_NOPE] = kn_ref[...].astype(BF16)
            kf_ref[:, QK_NOPE:HEAD_PAD] = kpe_ref[...]
            vb_ref[...] = v_ref[...].astype(BF16)

        qb = q_ref[...]
        m_ref[...] = jnp.full((tq, LANES), NEG, F32)
        l_ref[...] = jnp.zeros((tq, LANES), F32)
        acc_ref[...] = jnp.zeros((tq, V_HEAD), F32)

        def block(j, diagonal):
            off = j * tk
            sc = lax.dot_general(qb, kf_ref[pl.ds(off, tk), :], nt, preferred_element_type=F32) * scale
            if diagonal:
                sc = _causal_mask(sc, 0, 0, tq, tk)
            m_prev = m_ref[...]
            m_new = jnp.maximum(m_prev, jnp.max(sc, axis=1, keepdims=True))
            p = jnp.exp(sc - jnp.tile(m_new, (1, reps)))
            alpha = jnp.exp(m_prev - m_new)
            l_ref[...] = alpha * l_ref[...] + jnp.sum(p, axis=1, keepdims=True)
            acc_ref[...] = alpha * acc_ref[...] + jnp.dot(p.astype(BF16), vb_ref[pl.ds(off, tk), :],
                                                          preferred_element_type=F32)
            m_ref[...] = m_new

        for qi in range(s // tq):
            @pl.when(i == qi)
            def _(qi=qi):
                for j in range(qi):
                    block(j, False)
                block(qi, True)

        o_ref[...] = acc_ref[...] / l_ref[...]
        lse_ref[...] = m_ref[...] + jnp.log(l_ref[...])

    return pl.pallas_call(
        body, name="attn_fwd", grid=(N_HEADS, s // tq),
        in_specs=[pl.BlockSpec((tq, HEAD_PAD), lambda h, i: (i, h)),
                  pl.BlockSpec((s, QK_NOPE), lambda h, i: (0, 2 * h)),
                  pl.BlockSpec((s, V_HEAD), lambda h, i: (0, 2 * h + 1)),
                  pl.BlockSpec((s, LANES), lambda h, i: (0, 0))],
        out_specs=[pl.BlockSpec((tq, V_HEAD), lambda h, i: (i, h)),
                   pl.BlockSpec((tq, LANES), lambda h, i: (i, h))],
        out_shape=[jax.ShapeDtypeStruct((s, N_HEADS * V_HEAD), F32),
                   jax.ShapeDtypeStruct((s, N_HEADS * LANES), F32)],
        scratch_shapes=[pltpu.VMEM((s, HEAD_PAD), BF16), pltpu.VMEM((s, V_HEAD), BF16),
                        pltpu.VMEM((tq, LANES), F32), pltpu.VMEM((tq, LANES), F32), pltpu.VMEM((tq, V_HEAD), F32)],
        compiler_params=_params(("parallel", "arbitrary")),
    )(q, kv, kv, kpe)


def _attn_bwd(q, kv, kpe, o, do, lse):
    s = q.shape[0]
    tq = tk = _pick(s, (ATTN_BLOCK, 256, 128))
    nq = s // tq
    reps = tk // LANES
    scale = QK_HEAD ** -0.5
    nt = (((1,), (1,)), ((), ()))
    tn = (((0,), (0,)), ((), ()))

    def body(q_ref, kn_ref, v_ref, kpe_ref, o_ref, do_ref, lse_ref, dq_ref, dkv_ref, dkpe_ref,
             kf_ref, vb_ref, dk_ref, dv_ref):
        j = pl.program_id(1)

        @pl.when(j == 0)
        def _():
            dq_ref[...] = jnp.zeros_like(dq_ref)

        kf_ref[:, 0:QK_NOPE] = kn_ref[...].astype(BF16)
        kf_ref[:, QK_NOPE:HEAD_PAD] = kpe_ref[...]
        vb_ref[...] = v_ref[...].astype(BF16)
        dk_ref[...] = jnp.zeros_like(dk_ref)
        dv_ref[...] = jnp.zeros_like(dv_ref)

        def block(i, diagonal):
            off = i * tq
            qb = q_ref[pl.ds(off, tq), :]
            dob = do_ref[pl.ds(off, tq), :]
            delta = jnp.sum(dob * o_ref[pl.ds(off, tq), :], axis=1, keepdims=True)
            sc = lax.dot_general(qb, kf_ref[...], nt, preferred_element_type=F32) * scale
            if diagonal:
                sc = _causal_mask(sc, 0, 0, tq, tk)
            p = jnp.exp(sc - jnp.tile(lse_ref[pl.ds(off, tq), :], (1, reps)))
            dob16 = dob.astype(BF16)
            dv_ref[...] += lax.dot_general(p.astype(BF16), dob16, tn, preferred_element_type=F32)
            dp = lax.dot_general(dob16, vb_ref[...], nt, preferred_element_type=F32)
            ds = (p * (dp - delta) * scale).astype(BF16)
            dq_ref[pl.ds(off, tq), :] += jnp.dot(ds, kf_ref[...], preferred_element_type=F32)
            dk_ref[...] += lax.dot_general(ds, qb, tn, preferred_element_type=F32)

        for kj in range(nq):
            @pl.when(j == kj)
            def _(kj=kj):
                block(kj, True)
                for i in range(kj + 1, nq):
                    block(i, False)

        dkv_ref[:, 0:QK_NOPE] = dk_ref[:, 0:QK_NOPE]
        dkv_ref[:, QK_NOPE:HEAD_PAD] = dv_ref[...]
        dkpe_ref[...] = dk_ref[:, QK_NOPE:HEAD_PAD]

    head_rows = lambda w: pl.BlockSpec((s, w), lambda h, j: (0, h))
    return pl.pallas_call(
        body, name="attn_bwd", grid=(N_HEADS, s // tk),
        in_specs=[head_rows(HEAD_PAD),
                  pl.BlockSpec((tk, QK_NOPE), lambda h, j: (j, 2 * h)),
                  pl.BlockSpec((tk, V_HEAD), lambda h, j: (j, 2 * h + 1)),
                  pl.BlockSpec((tk, LANES), lambda h, j: (j, 0)),
                  head_rows(V_HEAD), head_rows(V_HEAD), head_rows(LANES)],
        out_specs=[head_rows(HEAD_PAD),
                   pl.BlockSpec((tk, HEAD_PAD), lambda h, j: (j, h)),
                   pl.BlockSpec((tk, LANES), lambda h, j: (j, h))],
        out_shape=[jax.ShapeDtypeStruct((s, N_HEADS * HEAD_PAD), F32),
                   jax.ShapeDtypeStruct((s, N_HEADS * HEAD_PAD), F32),
                   jax.ShapeDtypeStruct((s, N_HEADS * LANES), F32)],
        scratch_shapes=[pltpu.VMEM((tk, HEAD_PAD), BF16), pltpu.VMEM((tk, V_HEAD), BF16),
                        pltpu.VMEM((tk, HEAD_PAD), F32), pltpu.VMEM((tk, V_HEAD), F32)],
        compiler_params=_params(("parallel", "arbitrary")),
    )(q, kv, kv, kpe, o, do, lse)


def _mix_fwd(u1, lng, lnb, gcon, attn, gattn):
    s, c = u1.shape
    ac = attn.shape[1]
    ts = _pick(s, (256, 128))

    def body(u_ref, lg_ref, lb_ref, gc_ref, a_ref, ga_ref, o_ref):
        t3 = _silu(_ln(u_ref[...], lg_ref[...], lb_ref[...]))
        o_ref[:, 0:c] = _rms(t3, gc_ref[...]).astype(BF16)
        o_ref[:, c:c + ac] = _rms(a_ref[...], ga_ref[...]).astype(BF16)

    return pl.pallas_call(
        body, name="mix_fwd", grid=(s // ts,),
        in_specs=[_rows(ts, c), _vec(c), _vec(c), _vec(c), _rows(ts, ac), _vec(ac)],
        out_specs=_rows(ts, c + ac),
        out_shape=jax.ShapeDtypeStruct((s, c + ac), BF16),
        compiler_params=_params(("parallel",)),
    )(u1, lng, lnb, gcon, attn, gattn)


def _mix_bwd(dmixin, u1, lng, lnb, gcon, attn, gattn):
    s, c = u1.shape
    ac = attn.shape[1]
    ts = _pick(s, (256, 128))

    def body(d_ref, u_ref, lg_ref, lb_ref, gc_ref, a_ref, ga_ref,
             du_ref, da_ref, dlg_ref, dlb_ref, dgc_ref, dga_ref):
        _acc_init(pl.program_id(0), dlg_ref, dlb_ref, dgc_ref, dga_ref)
        u = u_ref[...]
        t2 = _ln(u, lg_ref[...], lb_ref[...])
        dt3, dgc = _rms_bwd(_silu(t2), gc_ref[...], d_ref[:, 0:c])
        du, dlg, dlb = _ln_bwd(u, lg_ref[...], dt3 * _silu_grad(t2))
        du_ref[...] = du
        dlg_ref[...] += dlg
        dlb_ref[...] += dlb
        dgc_ref[...] += dgc
        da, dga = _rms_bwd(a_ref[...], ga_ref[...], d_ref[:, c:c + ac])
        da_ref[...] = da
        dga_ref[...] += dga

    return pl.pallas_call(
        body, name="mix_bwd", grid=(s // ts,),
        in_specs=[_rows(ts, c + ac), _rows(ts, c), _vec(c), _vec(c), _vec(c), _rows(ts, ac), _vec(ac)],
        out_specs=[_rows(ts, c), _rows(ts, ac), _vec(c), _vec(c), _vec(c), _vec(ac)],
        out_shape=[jax.ShapeDtypeStruct((s, c), F32), jax.ShapeDtypeStruct((s, ac), F32),
                   jax.ShapeDtypeStruct((1, c), F32), jax.ShapeDtypeStruct((1, c), F32),
                   jax.ShapeDtypeStruct((1, c), F32), jax.ShapeDtypeStruct((1, ac), F32)],
        compiler_params=_params(("arbitrary",)),
    )(dmixin, u1, lng, lnb, gcon, attn, gattn)


def _post_mix_fwd(x, mix, gpost, gpre):
    s, d = x.shape
    ts = _pick(s, (256, 128))

    def body(x_ref, m_ref, gp_ref, gf_ref, x1_ref, hf_ref):
        x1 = x_ref[...] + _rms(m_ref[...], gp_ref[...])
        x1_ref[...] = x1
        hf_ref[...] = _rms(x1, gf_ref[...]).astype(BF16)

    return pl.pallas_call(
        body, name="post_mix_fwd", grid=(s // ts,),
        in_specs=[_rows(ts, d), _rows(ts, d), _vec(d), _vec(d)],
        out_specs=[_rows(ts, d), _rows(ts, d)],
        out_shape=[jax.ShapeDtypeStruct((s, d), F32), jax.ShapeDtypeStruct((s, d), BF16)],
        compiler_params=_params(("parallel",)),
    )(x, mix, gpost, gpre)


def _post_mix_bwd(dy, dhf, x1, gpre, mix, gpost):
    s, d = x1.shape
    ts = _pick(s, (256, 128))

    def body(dy_ref, dh_ref, x1_ref, gf_ref, m_ref, gp_ref, dx1_ref, dm_ref, dgf_ref, dgp_ref):
        _acc_init(pl.program_id(0), dgf_ref, dgp_ref)
        dxa, dgf = _rms_bwd(x1_ref[...], gf_ref[...], dh_ref[...])
        dx1 = dy_ref[...] + dxa
        dx1_ref[...] = dx1
        dgf_ref[...] += dgf
        dm, dgp = _rms_bwd(m_ref[...], gp_ref[...], dx1)
        dm_ref[...] = dm.astype(BF16)
        dgp_ref[...] += dgp

    return pl.pallas_call(
        body, name="post_mix_bwd", grid=(s // ts,),
        in_specs=[_rows(ts, d), _rows(ts, d), _rows(ts, d), _vec(d), _rows(ts, d), _vec(d)],
        out_specs=[_rows(ts, d), _rows(ts, d), _vec(d), _vec(d)],
        out_shape=[jax.ShapeDtypeStruct((s, d), F32), jax.ShapeDtypeStruct((s, d), BF16),
                   jax.ShapeDtypeStruct((1, d), F32), jax.ShapeDtypeStruct((1, d), F32)],
        compiler_params=_params(("arbitrary",)),
    )(dy, dhf, x1, gpre, mix, gpost)


def _ffn_up(hf, wg, wu):
    s, d = hf.shape
    nsh, fs, _ = wg.shape
    tm = _pick(s, (1024, 512, 256, 128))
    nt = (((1,), (1,)), ((), ()))

    def body(h_ref, wg_ref, wu_ref, g_ref, u_ref, a_ref):
        h = h_ref[...]
        g = lax.dot_general(h, wg_ref[...], nt, preferred_element_type=F32)
        u = lax.dot_general(h, wu_ref[...], nt, preferred_element_type=F32)
        g_ref[...] = g
        u_ref[...] = u
        a_ref[...] = (_silu(g) * u).astype(BF16)

    w_spec = pl.BlockSpec((None, fs, d), lambda i, j: (j, 0, 0))
    o_spec = pl.BlockSpec((None, tm, fs), lambda i, j: (j, i, 0))
    return pl.pallas_call(
        body, name="ffn_up", grid=(s // tm, nsh),
        in_specs=[pl.BlockSpec((tm, d), lambda i, j: (i, 0)), w_spec, w_spec],
        out_specs=[o_spec] * 3,
        out_shape=[jax.ShapeDtypeStruct((nsh, s, fs), F32)] * 2 + [jax.ShapeDtypeStruct((nsh, s, fs), BF16)],
        compiler_params=_params(("parallel", "parallel")),
    )(hf, wg, wu)


def _ffn_down(acts, ws, name):
    n = len(acts)
    nsh, s, fs = acts[0].shape
    d = ws[0].shape[2]
    tm = _pick(s, (1024, 512, 256, 128))
    tn = _pick(d, (512, 256, 128) if n == 1 else (256, 128))
    a_mode = None if n == 1 else pl.Buffered(1)

    def body(*refs):
        acc = None
        for a_ref, w_ref in zip(refs[:n], refs[n:2 * n]):
            for j in range(nsh):
                part = jnp.dot(a_ref[j], w_ref[j], preferred_element_type=F32)
                acc = part if acc is None else acc + part
        refs[-1][...] = acc

    return pl.pallas_call(
        body, name=name, grid=(s // tm, d // tn),
        in_specs=[pl.BlockSpec((nsh, tm, fs), lambda i, j: (0, i, 0), pipeline_mode=a_mode)] * n
        + [pl.BlockSpec((nsh, fs, tn), lambda i, j: (0, 0, j))] * n,
        out_specs=pl.BlockSpec((tm, tn), lambda i, j: (i, j)),
        out_shape=jax.ShapeDtypeStruct((s, d), F32),
        compiler_params=_params(("parallel", "parallel")),
    )(*acts, *ws)


def _ffn_down_bwd(dff, wd, gate, up, behind):
    s, d = dff.shape
    nsh, fs, _ = wd.shape
    tm = _pick(s, (1024, 512, 256, 128))
    nt = (((1,), (1,)), ((), ()))

    def body(d_ref, w_ref, g_ref, u_ref, _, dg_ref, du_ref):
        dact = lax.dot_general(d_ref[...], w_ref[...], nt, preferred_element_type=F32)
        g = g_ref[...]
        dg_ref[...] = (dact * u_ref[...] * _silu_grad(g)).astype(BF16)
        du_ref[...] = (dact * _silu(g)).astype(BF16)

    h_spec = pl.BlockSpec((None, tm, fs), lambda i, j: (j, i, 0))
    return pl.pallas_call(
        body, name="ffn_down_bwd", grid=(s // tm, nsh),
        in_specs=[pl.BlockSpec((tm, d), lambda i, j: (i, 0)),
                  pl.BlockSpec((None, fs, d), lambda i, j: (j, 0, 0)), h_spec, h_spec,
                  pl.BlockSpec((8, LANES), lambda i, j: (0, 0))],
        out_specs=[h_spec] * 2,
        out_shape=[jax.ShapeDtypeStruct((nsh, s, fs), BF16)] * 2,
        compiler_params=_params(("parallel", "parallel")),
    )(dff, wd, gate, up, behind)


def _ffn_dw(hiddens, other, name):
    n = len(hiddens)
    nsh, s, fs = hiddens[0].shape
    d = other.shape[1]
    tn = (((0,), (0,)), ((), ()))

    def body(*refs):
        for a_ref, o_ref in zip(refs[:n], refs[n + 1:]):
            o_ref[...] = lax.dot_general(a_ref[...], refs[n][...], tn, preferred_element_type=F32).astype(BF16)

    return pl.pallas_call(
        body, name=name, grid=(nsh,),
        in_specs=[pl.BlockSpec((None, s, fs), lambda j: (j, 0, 0))] * n + [pl.BlockSpec((s, d), lambda j: (0, 0))],
        out_specs=[pl.BlockSpec((None, fs, d), lambda j: (j, 0, 0))] * n,
        out_shape=[jax.ShapeDtypeStruct((nsh, fs, d), BF16)] * n,
        compiler_params=_params(("parallel",)),
    )(*hiddens, other)


def _final(ff, x1, tgt, g):
    s, d = x1.shape
    ts = _pick(s, (256, 128))

    def body(ff_ref, x1_ref, t_ref, g_ref, loss_ref, dy_ref, dff_ref, dg_ref):
        _acc_init(pl.program_id(0), loss_ref, dg_ref)
        ff_v = ff_ref[...]
        err = x1_ref[...] + _rms(ff_v, g_ref[...]) - t_ref[...]
        tok = jnp.mean(err * err, axis=-1, keepdims=True)
        loss_ref[...] += 0.5 * jnp.sum(tok, axis=0, keepdims=True)
        dy = err * (1.0 / d)
        dy_ref[...] = dy
        dff, dg = _rms_bwd(ff_v, g_ref[...], dy)
        dff_ref[...] = dff.astype(BF16)
        dg_ref[...] += dg

    return pl.pallas_call(
        body, name="final", grid=(s // ts,),
        in_specs=[_rows(ts, d), _rows(ts, d), _rows(ts, d), _vec(d)],
        out_specs=[_vec(LANES), _rows(ts, d), _rows(ts, d), _vec(d)],
        out_shape=[jax.ShapeDtypeStruct((1, LANES), F32), jax.ShapeDtypeStruct((s, d), F32),
                   jax.ShapeDtypeStruct((s, d), BF16), jax.ShapeDtypeStruct((1, d), F32)],
        compiler_params=_params(("arbitrary",)),
    )(ff, x1, tgt, g)


def _pre_bwd(dx1, dh, x, g):
    s, d = x.shape
    ts = _pick(s, (256, 128))

    def body(dx1_ref, dh_ref, x_ref, g_ref, dx_ref, dg_ref):
        _acc_init(pl.program_id(0), dg_ref)
        dxa, dg = _rms_bwd(x_ref[...], g_ref[...], dh_ref[...])
        dx_ref[...] = dx1_ref[...] + dxa
        dg_ref[...] += dg

    return pl.pallas_call(
        body, name="pre_bwd", grid=(s // ts,),
        in_specs=[_rows(ts, d), _rows(ts, d), _rows(ts, d), _vec(d)],
        out_specs=[_rows(ts, d), _vec(d)],
        out_shape=[jax.ShapeDtypeStruct((s, d), F32), jax.ShapeDtypeStruct((1, d), F32)],
        compiler_params=_params(("arbitrary",)),
    )(dx1, dh, x, g)


def _local_step(x, pos, tgt, vecs, in_weights_fn, mix_weights_fn, up_weights_fn, down_weights_fn, grads_fn):
    c = vecs["conv_b"].shape[1]
    ql = vecs["q_norm"].shape[1]
    kvl = vecs["kv_norm"].shape[1]
    half = jnp.arange(0, QK_ROPE, 2, dtype=F32)
    freq = ROPE_THETA ** (-half / QK_ROPE)
    inv_freq = jnp.concatenate([freq, freq, jnp.zeros((LANES - QK_ROPE,), F32)])[None, :]
    tabs = _rope_tables(pos, inv_freq)

    h = _pre_fwd(x, vecs["pre_mix_norm"])
    w_in_t, zero = in_weights_fn(h)
    z = _mm(h, w_in_t, "nt", "mm_z")
    w_uq_t, w_ukv, conv_w, w_out = mix_weights_fn(z)
    u0, qn, kvn, kpe = _split_fwd(z, vecs["q_norm"] + zero, vecs["kv_norm"], tabs, c, ql, kvl)
    u1 = _conv_fwd(u0, conv_w, vecs["conv_b"])
    q = _q_rope(_mm(qn, w_uq_t, "nt", "mm_q"), tabs, False, BF16, "q_rope")
    kv = _mm(kvn, w_ukv, "nn", "mm_kv")
    attn, lse = _attn_fwd(q, kv, kpe)
    mixin = _mix_fwd(u1, vecs["conv_ln_g"], vecs["conv_ln_b"], vecs["conv_out_norm"], attn, vecs["attn_out_norm"])
    mix = _mm(mixin, w_out, "nn", "mm_mix")
    x1, hf = _post_mix_fwd(x, mix, vecs["post_mix_norm"], vecs["pre_ffn_norm"])
    w_gate, w_up = up_weights_fn(mix)
    gate, up, act = _ffn_up(hf, w_gate, w_up)
    w_down = down_weights_fn(act)
    ff = _ffn_down([act], [w_down], "ffn_down")
    loss, dy, dff, d_post_ffn = _final(ff, x1, tgt, vecs["post_ffn_norm"])

    g = {"post_ffn_norm": d_post_ffn}
    zero = grads_fn("down", ("w_down",), _ffn_dw([act], dff, "ffn_dw_down"), dff)
    dgate, dup = _ffn_down_bwd(dff, w_down, gate, up, jnp.zeros((8, LANES), F32) + zero)
    zero = zero + grads_fn("up", ("w_gate", "w_up"), _ffn_dw([dgate, dup], hf, "ffn_dw_up"), dgate)
    dhf = _ffn_down([dgate, dup], [w_gate, w_up], "ffn_dhf")
    dx1, dmix, g["pre_ffn_norm"], g["post_mix_norm"] = _post_mix_bwd(
        dy, dhf, x1, vecs["pre_ffn_norm"] + zero, mix, vecs["post_mix_norm"])
    dmixin = _mm(dmix, w_out, "nt", "mm_dmixin")
    dw_out = _mm(mixin, dmix, "tn", "mm_dw_out", BF16)
    zero = grads_fn("out", ("w_out",), [dw_out], dmix)
    du1, dattn, g["conv_ln_g"], g["conv_ln_b"], g["conv_out_norm"], g["attn_out_norm"] = _mix_bwd(
        dmixin, u1, vecs["conv_ln_g"] + zero, vecs["conv_ln_b"], vecs["conv_out_norm"], attn, vecs["attn_out_norm"])
    du0, dw_conv, g["conv_b"] = _conv_bwd(du1, u0, conv_w)
    dq, dkv, dkpe_h = _attn_bwd(q, kv, kpe, attn, dattn, lse)
    dqpre = _q_rope(dq, tabs, True, BF16, "q_rope_bwd")
    dqn = _mm(dqpre, w_uq_t, "nn", "mm_dqn")
    dw_uq = _mm(dqpre, qn, "tn", "mm_dw_uq", BF16)
    dkvn = _mm(dkv, w_ukv, "nt", "mm_dkvn")
    dw_ukv = _mm(kvn, dkv, "tn", "mm_dw_ukv", BF16)
    zero = grads_fn("", (), [], dkvn)
    dz, g["q_norm"], g["kv_norm"] = _split_bwd(du0, z, dqn, dkvn, dkpe_h, vecs["q_norm"] + zero, vecs["kv_norm"], tabs,
                                               c, ql, kvl)
    dw_in = _mm(dz, h, "tn", "mm_dw_in", BF16)
    zero = grads_fn("in", ("w_in", "w_uq", "w_ukv", "conv_w"), [dw_in, dw_uq, dw_ukv, dw_conv], dz)
    dh = _mm(dz, w_in_t, "nn", "mm_dh")
    zero = zero + grads_fn("", (), [], dh)
    grad_x, g["pre_mix_norm"] = _pre_bwd(dx1, dh, x, vecs["pre_mix_norm"] + zero)
    return loss, grad_x, g


def _my_index():
    return 4 * lax.axis_index("x") + 2 * lax.axis_index("y") + lax.axis_index("c")


def _coords(idx):
    return ((idx >> 2) & 1, (idx >> 1) & 1, idx & 1)


def _place():
    x, y, c = lax.axis_index("x"), lax.axis_index("y"), lax.axis_index("c")
    return (x, y, c), (x, y, 1 - c), [(1 - x, y), (x, 1 - y), (1 - x, 1 - y)]


def _small_copies(src, land, send_sem, recv_sem):
    me = _my_index()
    return [pltpu.make_async_remote_copy(src_ref=src, dst_ref=land.at[me], send_sem=send_sem, recv_sem=recv_sem,
                                         device_id=_coords(me ^ p), device_id_type=MESH)
            for p in range(1, N_DEV)]


HBM_SPEC = pl.BlockSpec(memory_space=pltpu.HBM)
SEM_SPEC = pl.BlockSpec(memory_space=pltpu.SEMAPHORE)
DATAFLOW = pltpu.SideEffectType.DATAFLOW_SIDE_EFFECTING


def _split_start(name, copies_of, srcs, lands, after):
    n = len(srcs)

    def body(*refs):
        outs = refs[2 * n + 1:]
        for k in range(n):
            for cp in copies_of(refs[k], refs[n + k], outs[k], outs[n + k]):
                cp.start()
        outs[-1][...] = jnp.zeros_like(outs[-1])

    hbm = lambda a: pltpu.HBM(a.shape, a.dtype)
    out = pl.pallas_call(
        body, name=name,
        in_specs=[HBM_SPEC] * (2 * n + 1),
        out_specs=[SEM_SPEC] * (2 * n) + [HBM_SPEC] * (2 * n) + [pl.BlockSpec(memory_space=pltpu.VMEM)],
        out_shape=[pltpu.SemaphoreType.DMA(())] * (2 * n) + [hbm(a) for a in srcs] + [hbm(a) for a in lands]
        + [jax.ShapeDtypeStruct((8, LANES), F32)],
        input_output_aliases={k: 2 * n + k for k in range(2 * n)},
        compiler_params=pltpu.CompilerParams(has_side_effects=DATAFLOW),
    )(*[pltpu.with_memory_space_constraint(a, pltpu.HBM) for a in list(srcs) + list(lands) + [after]])
    return (out[:n], out[n:2 * n], out[2 * n:3 * n], out[3 * n:4 * n]), out[-1][0, 0]


def _split_wait(name, n_copies, started, after):
    send_sems, recv_sems, srcs, lands = started
    n = len(srcs)

    def body(*refs):
        for k in range(n):
            slots = refs[n + k].at[pl.ds(0, n_copies)]
            all_copies = pltpu.make_async_remote_copy(
                src_ref=slots, dst_ref=slots, send_sem=refs[2 * n + k], recv_sem=refs[3 * n + k],
                device_id=_place()[0], device_id_type=MESH)
            all_copies.wait_send()
            all_copies.wait_recv()

    hbm = lambda a: pltpu.HBM(a.shape, a.dtype)
    out = pl.pallas_call(
        body, name=name,
        in_specs=[HBM_SPEC] * (2 * n) + [SEM_SPEC] * (2 * n) + [HBM_SPEC],
        out_specs=[HBM_SPEC] * (2 * n),
        out_shape=[hbm(a) for a in srcs] + [hbm(a) for a in lands],
        input_output_aliases={k: k for k in range(2 * n)},
        compiler_params=pltpu.CompilerParams(has_side_effects=DATAFLOW),
    )(*srcs, *lands, *send_sems, *recv_sems, pltpu.with_memory_space_constraint(after, pltpu.HBM))
    return out[:n], out[n:]


def _slot(chip, core):
    return 4 * chip[0] + 2 * chip[1] + core


def _gather_copies(src, land, send_sem, recv_sem):
    (x, y, c), sib, chips = _place()
    return [pltpu.make_async_remote_copy(src_ref=src, dst_ref=land.at[_slot((x, y), c)], send_sem=send_sem,
                                         recv_sem=recv_sem, device_id=to, device_id_type=MESH)
            for to in [sib] + [(*chip, c) for chip in chips]]


def _gather_pass_on(lands, name):
    n = len(lands)

    def body(*refs):
        ins, outs = refs[:n], refs[n:2 * n]
        send_sems, recv_sems = refs[2 * n:]
        (x, y, c), sib, chips = _place()
        sends = []
        for k in range(n):
            for j, chip in enumerate(chips):
                sends.append(pltpu.make_async_remote_copy(
                    src_ref=ins[k].at[_slot(chip, c)], dst_ref=outs[k].at[_slot(chip, c)],
                    send_sem=send_sems.at[k, j], recv_sem=recv_sems.at[k, j], device_id=sib, device_id_type=MESH))
        for cp in sends:
            cp.start()
        for cp in sends:
            cp.wait_recv()
        for cp in sends:
            cp.wait_send()

    any_spec = pl.BlockSpec(memory_space=pl.ANY)
    return pl.pallas_call(
        body, name=name,
        in_specs=[any_spec] * n, out_specs=[any_spec] * n,
        out_shape=[jax.ShapeDtypeStruct(a.shape, a.dtype) for a in lands],
        input_output_aliases={k: k for k in range(n)},
        scratch_shapes=[pltpu.SemaphoreType.DMA((n, 3))] * 2,
        compiler_params=pltpu.CompilerParams(has_side_effects=True),
    )(*lands)


def _chip_copies(src, land, send_sem, recv_sem):
    (x, y, c), _, chips = _place()
    return [pltpu.make_async_remote_copy(src_ref=src.at[2 * chip[0] + chip[1]], dst_ref=land.at[j], send_sem=send_sem,
                                         recv_sem=recv_sem, device_id=(*chip, c), device_id_type=MESH)
            for j, chip in enumerate(chips)]


ROW_TILE_BYTES = 14 * 1024 * 1024


def _stream_tile(r, c, bytes_per_elem):
    if r * c * bytes_per_elem <= ROW_TILE_BYTES:
        return r, c
    rows = [t for t in range(16, r, 16) if r % t == 0 and t * c * bytes_per_elem <= ROW_TILE_BYTES]
    if rows:
        return max(rows), c
    cols = [t for t in range(LANES, c, LANES) if c % t == 0 and r * t * bytes_per_elem <= ROW_TILE_BYTES]
    return r, max(cols)


def _sibling_copies(src, land, send_sem, recv_sem):
    (x, y, c), sib, _ = _place()
    return [pltpu.make_async_remote_copy(src_ref=src.at[2 * q + 1 - c], dst_ref=land.at[q], send_sem=send_sem,
                                         recv_sem=recv_sem, device_id=sib, device_id_type=MESH)
            for q in range(4)]


def _pair_sum(core, blocks, theirs, name):
    q, r, c = theirs.shape
    tr, tc = _stream_tile(r, c, 3 * theirs.dtype.itemsize)

    def body(core_ref, a_ref, b_ref, o_ref):
        o_ref[...] = (a_ref[...].astype(F32) + b_ref[...].astype(F32)).astype(o_ref.dtype)

    blk = pl.BlockSpec((1, tr, tc), lambda i, j, k, core_ref: (i, j, k))
    mine = pl.BlockSpec((1, tr, tc), lambda i, j, k, core_ref: (2 * i + core_ref[0], j, k))
    return pl.pallas_call(
        body, name=name,
        grid_spec=pltpu.PrefetchScalarGridSpec(num_scalar_prefetch=1, grid=(q, r // tr, c // tc),
                                               in_specs=[mine, blk], out_specs=blk),
        out_shape=jax.ShapeDtypeStruct(theirs.shape, theirs.dtype),
        compiler_params=_params(("parallel", "parallel", "parallel")),
    )(core, blocks, theirs)


def _reduce_adamw(parts, w, m, v, name, own=None, own_slot=None):
    r, c = w.shape
    n_parts = parts.shape[0]
    tr, tc = _stream_tile(r, c, (n_parts + 1) * parts.dtype.itemsize + 7 * 4)
    c1 = 1.0 - ADAM_B1
    c2 = 1.0 - ADAM_B2
    bc1 = 1.0 - ADAM_B1 ** ADAM_STEP
    bc2 = 1.0 - ADAM_B2 ** ADAM_STEP

    def body(*refs):
        if own is None:
            p_ref, w_ref, m_ref, v_ref, g_ref, d_ref, nm_ref, nv_ref = refs
            g = p_ref[0].astype(F32)
            first = 1
        else:
            _, o_ref, p_ref, w_ref, m_ref, v_ref, g_ref, d_ref, nm_ref, nv_ref = refs
            g = o_ref[0].astype(F32)
            first = 0
        for j in range(first, n_parts):
            g = g + p_ref[j].astype(F32)
        nm = ADAM_B1 * m_ref[...] + c1 * g
        nv = ADAM_B2 * v_ref[...] + c2 * (g * g)
        g_ref[...] = g
        nm_ref[...] = nm
        nv_ref[...] = nv
        d_ref[...] = -ADAM_LR * ((nm / bc1) / (jnp.sqrt(nv / bc2) + ADAM_EPS) + ADAM_WD * w_ref[...])

    out = jax.ShapeDtypeStruct((r, c), F32)
    grid = (r // tr, c // tc)
    if own is None:
        blk = pl.BlockSpec((tr, tc), lambda i, j: (i, j))
        return pl.pallas_call(
            body, name=name, grid=grid,
            in_specs=[pl.BlockSpec((n_parts, tr, tc), lambda i, j: (0, i, j)), blk, blk, blk],
            out_specs=[blk] * 4, out_shape=[out] * 4,
            compiler_params=_params(("parallel", "parallel")),
        )(parts, w, m, v)
    blk = pl.BlockSpec((tr, tc), lambda i, j, slot_ref: (i, j))
    return pl.pallas_call(
        body, name=name,
        grid_spec=pltpu.PrefetchScalarGridSpec(
            num_scalar_prefetch=1, grid=grid,
            in_specs=[pl.BlockSpec((1, tr, tc), lambda i, j, slot_ref: (slot_ref[0], i, j)),
                      pl.BlockSpec((n_parts, tr, tc), lambda i, j, slot_ref: (0, i, j)), blk, blk, blk],
            out_specs=[blk] * 4),
        out_shape=[out] * 4,
        compiler_params=_params(("parallel", "parallel")),
    )(own_slot, own, parts, w, m, v)


_MIX = ("w_in", "w_uq", "w_ukv", "conv_w", "w_out")
_FFN = ("w_gate", "w_up", "w_down")
_BIG = _MIX + _FFN
_TRANSPOSED = ("w_in", "w_uq", "w_gate", "w_up")
_SMALL = ("pre_mix_norm", "q_norm", "kv_norm", "conv_b", "conv_ln_g", "conv_ln_b", "conv_out_norm",
          "attn_out_norm", "post_mix_norm", "pre_ffn_norm", "post_ffn_norm")
_ORDER = ("pre_mix_norm", "w_in", "q_norm", "w_uq", "kv_norm", "w_ukv", "conv_w", "conv_b", "conv_ln_g",
          "conv_ln_b", "conv_out_norm", "attn_out_norm", "w_out", "post_mix_norm", "pre_ffn_norm", "w_gate",
          "w_up", "w_down", "post_ffn_norm")


def _cols_from_shards(g):
    return jnp.transpose(g, (1, 0, 2)).reshape(g.shape[1], N_DEV * g.shape[2])


def _cols_to_shards(w):
    k, n8 = w.shape
    return jnp.transpose(w.reshape(k, N_DEV, n8 // N_DEV), (1, 0, 2))


def _step(x, positions, loss_target, w, m, v):
    s, d = x.shape[1], x.shape[2]
    x2, tgt = x[0], loss_target[0]
    pos = positions.reshape(s, 1)
    vecs = {n: w[n] for n in _SMALL}
    core = lax.axis_index("c").astype(jnp.int32).reshape(1)
    my_chip = (2 * lax.axis_index("x") + lax.axis_index("y")).astype(jnp.int32).reshape(1)
    n_in_cols = N_DEV * w["w_in"].shape[2]
    gathers, scatters, to_sibling = {}, {}, []

    def shard(t, n):
        return t[n][0].T if n in _TRANSPOSED else t[n][0]

    def gather_start(names, tag, after, zero=0.0):
        srcs = [w[n][0] if n == "conv_w" else (shard(w, n) + zero).astype(BF16) for n in names]
        lands = [lax.empty((N_DEV,) + a.shape, a.dtype) for a in srcs]
        gathers[tag], zero = _split_start("gather_" + tag + "_start", _gather_copies, srcs, lands, after)
        return zero

    def gather_finish(names, tag, after):
        srcs, lands = _split_wait("gather_" + tag + "_wait", 4, gathers[tag], after)
        lands = _gather_pass_on(lands, "gather_" + tag + "_pass_on")
        me = _my_index()
        return {n: lax.dynamic_update_slice(g, a[None], (me,) + (0,) * a.ndim) for n, g, a in zip(names, lands, srcs)}

    def in_weights_fn(h):
        w_in_g = gather_finish(("w_in",), "in", h)["w_in"]
        return jnp.pad(w_in_g.reshape(-1, d), ((0, LANES - QK_ROPE), (0, 0))), 0.0

    def mix_weights_fn(z):
        gath = gather_finish(_MIX[1:], "mix", z)
        w_uq_t = jnp.pad(gath["w_uq"], ((0, 0), (0, HEAD_PAD - QK_HEAD), (0, 0))).reshape(N_HEADS * HEAD_PAD, -1)
        return (w_uq_t, _cols_from_shards(gath["w_ukv"]), _cols_from_shards(gath["conv_w"]),
                gath["w_out"].reshape(-1, d))

    def up_weights_fn(mix):
        gath = gather_finish(("w_gate", "w_up"), "up", mix)
        return gath["w_gate"], gath["w_up"]

    def down_weights_fn(act):
        return gather_finish(("w_down",), "down", act)["w_down"]

    def scatter_advance(after):
        if not to_sibling:
            return 0.0
        names, tag, sent = to_sibling.pop()
        blocks, theirs = _split_wait("to_sibling_" + tag + "_wait", 4, sent, after)
        pairs = [_pair_sum(core, b, t, "pair_sum_" + n) for n, b, t in zip(names, blocks, theirs)]
        lands = [lax.empty((3,) + p.shape[1:], p.dtype) for p in pairs]
        scatters[tag], zero = _split_start("scatter_" + tag + "_start", _chip_copies, pairs, lands, theirs[0])
        return zero

    to_blocks = {
        "w_in": lambda a: a[:n_in_cols].reshape(N_DEV, -1, d),
        "w_uq": lambda a: a.reshape(N_HEADS, HEAD_PAD, -1)[:, :QK_HEAD],
        "w_ukv": _cols_to_shards, "conv_w": _cols_to_shards,
        "w_out": lambda a: a.reshape(N_DEV, -1, d),
    }

    def grads_fn(tag, names, grads, after):
        zero = scatter_advance(after)
        if not grads:
            return zero
        blocks = [to_blocks.get(n, lambda a: a)(a) for n, a in zip(names, grads)]
        lands = [lax.empty((4,) + b.shape[1:], b.dtype) for b in blocks]
        sent, zero2 = _split_start("to_sibling_" + tag + "_start", _sibling_copies, blocks, lands, after)
        to_sibling.append((names, tag, sent))
        return zero + zero2

    zero = gather_start(("w_in",), "in", x2)
    zero = gather_start(_MIX[1:], "mix", x2, zero)
    zero = gather_start(("w_gate", "w_up"), "up", x2, zero)
    vecs["pre_mix_norm"] = vecs["pre_mix_norm"] + gather_start(("w_down",), "down", x2, zero)
    loss, grad_x, g = _local_step(x2, pos, tgt, vecs, in_weights_fn, mix_weights_fn, up_weights_fn,
                                  down_weights_fn, grads_fn)

    small = jnp.concatenate([g[n] for n in _SMALL], axis=1)
    small_started, zero = _split_start("gather_small_start", _small_copies, [small],
                                       [lax.empty((N_DEV,) + small.shape, F32)], grad_x)

    res = {}
    after = grad_x
    my_chip = my_chip + zero.astype(jnp.int32)
    for tag, names in (("down", ("w_down",)), ("up", ("w_gate", "w_up")), ("out", ("w_out",)),
                       ("in", ("w_in", "w_uq", "w_ukv", "conv_w"))):
        pairs, recv = _split_wait("scatter_" + tag + "_wait", 3, scatters[tag], after)
        for n, own, parts in zip(names, pairs, recv):
            res[n] = _reduce_adamw(parts, shard(w, n), shard(m, n), shard(v, n), "adamw_" + n, own=own,
                                   own_slot=my_chip)
            after = res[n][1]
            res[n] = [(t.T if n in _TRANSPOSED else t)[None] for t in res[n]]
    (small,), (small_all,) = _split_wait("gather_small_wait", N_DEV - 1, small_started, after)
    small_all = lax.dynamic_update_slice(small_all, small[None], (_my_index(), 0, 0))
    cat = lambda t: jnp.concatenate([t[n] for n in _SMALL], axis=1)
    sg, sd, sm, sv = _reduce_adamw(small_all, cat(w), cat(m), cat(v), "adamw_small")
    off = 0
    for n in _SMALL:
        width = w[n].shape[1]
        res[n] = [t[:, off:off + width] for t in (sg, sd, sm, sv)]
        off += width

    total = lax.psum(loss[0, 0], ("x", "y", "c"))
    outs = [total, grad_x[None]]
    for part in range(4):
        outs.extend(res[n][part] for n in _ORDER)
    return tuple(outs)


def kernel(x, positions, pre_mix_norm, w_in, q_norm, w_uq, kv_norm, w_ukv, conv_w, conv_b, conv_ln_g, conv_ln_b, conv_out_norm, attn_out_norm, w_out, post_mix_norm, pre_ffn_norm, w_gate, w_up, w_down, post_ffn_norm, loss_target, m_pre_mix_norm, m_w_in, m_q_norm, m_w_uq, m_kv_norm, m_w_ukv, m_conv_w, m_conv_b, m_conv_ln_g, m_conv_ln_b, m_conv_out_norm, m_attn_out_norm, m_w_out, m_post_mix_norm, m_pre_ffn_norm, m_w_gate, m_w_up, m_w_down, m_post_ffn_norm, v_pre_mix_norm, v_w_in, v_q_norm, v_w_uq, v_kv_norm, v_w_ukv, v_conv_w, v_conv_b, v_conv_ln_g, v_conv_ln_b, v_conv_out_norm, v_attn_out_norm, v_w_out, v_post_mix_norm, v_pre_ffn_norm, v_w_gate, v_w_up, v_w_down, v_post_ffn_norm):
    w = dict(zip(_ORDER, (pre_mix_norm, w_in, q_norm, w_uq, kv_norm, w_ukv, conv_w, conv_b, conv_ln_g, conv_ln_b,
                          conv_out_norm, attn_out_norm, w_out, post_mix_norm, pre_ffn_norm, w_gate, w_up, w_down,
                          post_ffn_norm)))
    m = dict(zip(_ORDER, (m_pre_mix_norm, m_w_in, m_q_norm, m_w_uq, m_kv_norm, m_w_ukv, m_conv_w, m_conv_b,
                          m_conv_ln_g, m_conv_ln_b, m_conv_out_norm, m_attn_out_norm, m_w_out, m_post_mix_norm,
                          m_pre_ffn_norm, m_w_gate, m_w_up, m_w_down, m_post_ffn_norm)))
    v = dict(zip(_ORDER, (v_pre_mix_norm, v_w_in, v_q_norm, v_w_uq, v_kv_norm, v_w_ukv, v_conv_w, v_conv_b,
                          v_conv_ln_g, v_conv_ln_b, v_conv_out_norm, v_attn_out_norm, v_w_out, v_post_mix_norm,
                          v_pre_ffn_norm, v_w_gate, v_w_up, v_w_down, v_post_ffn_norm)))
    return _step(x, positions, loss_target, w, m, v)
```

```python
import functools

import jax
import jax.numpy as jnp
from jax import lax
from jax.experimental import pallas as pl
from jax.experimental.pallas import tpu as pltpu

N_DEV = 8
N_HEADS = 8
QK_NOPE = 128
QK_ROPE = 64
V_HEAD = 128
QK_HEAD = QK_NOPE + QK_ROPE
HEAD_PAD = 256
LANES = 128
ATTN_BLOCK = 512
CONV_K = 31
CONV_PAD = 32
EPS = 1e-6
ROPE_THETA = 10000.0
ADAM_LR = 0.001
ADAM_B1 = 0.9
ADAM_B2 = 0.999
ADAM_EPS = 1e-08
ADAM_WD = 0.01
ADAM_STEP = 10
VMEM_LIMIT = 56 * 1024 * 1024
F32 = jnp.float32
BF16 = jnp.bfloat16
MESH = pl.DeviceIdType.MESH
NEG = -1e30


def _pick(n, prefs):
    for p in prefs:
        if p <= n and n % p == 0:
            return p
    return n


def _params(sem):
    return pltpu.CompilerParams(dimension_semantics=sem, vmem_limit_bytes=VMEM_LIMIT)


_DIMS = {"nn": (((1,), (0,)), ((), ())), "nt": (((1,), (1,)), ((), ())), "tn": (((0,), (0,)), ((), ()))}


MM_VMEM_BUDGET = 40 * 1024 * 1024
MM_MAX_MACS = 3 * 1024 ** 3


V7X_HBM_BYTES_PER_S = 3.0e12
V7X_MXU_MACS_PER_S = 0.45e15
GRID_STEP_S = 0.35e-6


def _mm_tiles(m, n, k, size_a, size_b, size_o):
    best = None
    for tm in sorted({m, 1024, 512, 256, 128}, reverse=True):
        if tm > m or m % tm:
            continue
        for tn in sorted({n, 2048, 1024, 512, 384, 256, 128}, reverse=True):
            if tn > n or n % tn:
                continue
            vmem = 2 * (tm * k * size_a + k * tn * size_b + tm * tn * size_o)
            if vmem > MM_VMEM_BUDGET or tm * tn * k > MM_MAX_MACS:
                continue
            b_reads = 1 if tn == n else m // tm
            traffic = m * k * size_a + b_reads * k * n * size_b + m * n * size_o
            exposed = tm * k * size_a + k * tn * size_b + tm * tn * size_o
            steps = (m // tm) * (n // tn)
            key = (max(traffic / V7X_HBM_BYTES_PER_S, m * n * k / V7X_MXU_MACS_PER_S)
                   + exposed / V7X_HBM_BYTES_PER_S + steps * GRID_STEP_S)
            if best is None or key < best[0]:
                best = (key, tm, tn)
    assert best is not None, (m, n, k)
    return best[1], best[2]


def _mm(a, b, mode, name, out_dtype=F32, add=None):
    if mode == "nn":
        (m, k), (k2, n) = a.shape, b.shape
    elif mode == "nt":
        (m, k), (n, k2) = a.shape, b.shape
    else:
        (k, m), (k2, n) = a.shape, b.shape
    assert k == k2, (a.shape, b.shape, mode)
    tm, tn = _mm_tiles(m, n, k, a.dtype.itemsize, b.dtype.itemsize,
                       jnp.dtype(out_dtype).itemsize + (0 if add is None else 4))
    dims = _DIMS[mode]

    def body(a_ref, b_ref, *rest):
        acc = lax.dot_general(a_ref[...].astype(BF16), b_ref[...].astype(BF16), dims, preferred_element_type=F32)
        if add is not None:
            acc = acc + rest[0][...]
        rest[-1][...] = acc.astype(rest[-1].dtype)

    if mode == "tn":
        a_spec = pl.BlockSpec((k, tm), lambda i, j: (0, i))
    else:
        a_spec = pl.BlockSpec((tm, k), lambda i, j: (i, 0))
    if mode == "nt":
        b_spec = pl.BlockSpec((tn, k), lambda i, j: (j, 0))
    else:
        b_spec = pl.BlockSpec((k, tn), lambda i, j: (0, j))
    o_spec = pl.BlockSpec((tm, tn), lambda i, j: (i, j))
    extra = [] if add is None else [add]
    return pl.pallas_call(
        body, name=name,
        grid=(m // tm, n // tn),
        in_specs=[a_spec, b_spec] + [o_spec] * len(extra),
        out_specs=o_spec,
        out_shape=jax.ShapeDtypeStruct((m, n), out_dtype),
        compiler_params=_params(("parallel", "parallel")),
    )(a, b, *extra)


def _sigmoid(x):
    return 1.0 / (1.0 + jnp.exp(-x))


def _rms(x, g):
    r = lax.rsqrt(jnp.mean(x * x, axis=-1, keepdims=True) + EPS)
    return (x * r) * g


def _rms_bwd(x, g, dy):
    r = lax.rsqrt(jnp.mean(x * x, axis=-1, keepdims=True) + EPS)
    xh = x * r
    dyg = dy * g
    dx = r * (dyg - xh * jnp.mean(dyg * xh, axis=-1, keepdims=True))
    return dx, jnp.sum(dy * xh, axis=0, keepdims=True)


def _ln(x, g, b):
    mu = jnp.mean(x, axis=-1, keepdims=True)
    xc = x - mu
    rs = lax.rsqrt(jnp.mean(xc * xc, axis=-1, keepdims=True) + EPS)
    return (xc * rs) * g + b


def _ln_bwd(x, g, dy):
    mu = jnp.mean(x, axis=-1, keepdims=True)
    xc = x - mu
    rs = lax.rsqrt(jnp.mean(xc * xc, axis=-1, keepdims=True) + EPS)
    xh = xc * rs
    dyg = dy * g
    dx = rs * (dyg - jnp.mean(dyg, axis=-1, keepdims=True) - xh * jnp.mean(dyg * xh, axis=-1, keepdims=True))
    return dx, jnp.sum(dy * xh, axis=0, keepdims=True), jnp.sum(dy, axis=0, keepdims=True)


def _silu(x):
    return x * _sigmoid(x)


def _silu_grad(x):
    s = _sigmoid(x)
    return s * (1.0 + x * (1.0 - s))


def _rope(x, cos, sa, sb):
    return x * cos + pltpu.roll(x, 96, 1) * sa + pltpu.roll(x, 32, 1) * sb


def _rope_t(d, cos, sa, sb):
    return d * cos - pltpu.roll(d, 96, 1) * sa - pltpu.roll(d, 32, 1) * sb


def _rows(ts, w):
    return pl.BlockSpec((ts, w), lambda i: (i, 0))


def _vec(w):
    return pl.BlockSpec((1, w), lambda i: (0, 0))


def _acc_init(i, *refs):
    @pl.when(i == 0)
    def _():
        for r in refs:
            r[...] = jnp.zeros_like(r)


def _rope_tables(pos, inv_freq):
    s = pos.shape[0]
    ts = _pick(s, (512, 256, 128))

    def body(p_ref, f_ref, c_ref, sa_ref, sb_ref):
        ang = p_ref[...].astype(F32) * f_ref[...]
        lane = lax.broadcasted_iota(jnp.int32, ang.shape, 1)
        c, sn = jnp.cos(ang), jnp.sin(ang)
        c_ref[...] = jnp.where(lane < QK_ROPE, c, 0.0)
        sa_ref[...] = jnp.where(lane < QK_ROPE // 2, -sn, 0.0)
        sb_ref[...] = jnp.where((lane >= QK_ROPE // 2) & (lane < QK_ROPE), sn, 0.0)

    out = jax.ShapeDtypeStruct((s, LANES), F32)
    return pl.pallas_call(
        body, name="rope_tables", grid=(s // ts,),
        in_specs=[_rows(ts, 1), _vec(LANES)],
        out_specs=[_rows(ts, LANES)] * 3, out_shape=[out] * 3,
        compiler_params=_params(("parallel",)),
    )(pos, inv_freq)


def _pre_fwd(x, g):
    s, d = x.shape
    ts = _pick(s, (256, 128))

    def body(x_ref, g_ref, h_ref):
        h_ref[...] = _rms(x_ref[...], g_ref[...]).astype(BF16)

    return pl.pallas_call(
        body, name="pre_fwd", grid=(s // ts,),
        in_specs=[_rows(ts, d), _vec(d)], out_specs=_rows(ts, d),
        out_shape=jax.ShapeDtypeStruct((s, d), BF16),
        compiler_params=_params(("parallel",)),
    )(x, g)


def _split_fwd(z, gq, gkv, tabs, c, ql, kvl):
    s, zw = z.shape
    ts = _pick(s, (256, 128))
    o_q, o_kv, o_kr = 2 * c, 2 * c + ql, 2 * c + ql + kvl

    def body(z_ref, gq_ref, gkv_ref, c_ref, sa_ref, sb_ref, u0_ref, qn_ref, kvn_ref, kpe_ref):
        u0_ref[...] = z_ref[:, 0:c] * _sigmoid(z_ref[:, c:2 * c])
        qn_ref[...] = _rms(z_ref[:, o_q:o_kv], gq_ref[...]).astype(BF16)
        kvn_ref[...] = _rms(z_ref[:, o_kv:o_kr], gkv_ref[...]).astype(BF16)
        kpe_ref[...] = _rope(z_ref[:, o_kr:o_kr + LANES], c_ref[...], sa_ref[...], sb_ref[...]).astype(BF16)

    return pl.pallas_call(
        body, name="split_fwd", grid=(s // ts,),
        in_specs=[_rows(ts, zw), _vec(ql), _vec(kvl)] + [_rows(ts, LANES)] * 3,
        out_specs=[_rows(ts, c), _rows(ts, ql), _rows(ts, kvl), _rows(ts, LANES)],
        out_shape=[jax.ShapeDtypeStruct((s, c), F32), jax.ShapeDtypeStruct((s, ql), BF16),
                   jax.ShapeDtypeStruct((s, kvl), BF16), jax.ShapeDtypeStruct((s, LANES), BF16)],
        compiler_params=_params(("parallel",)),
    )(z, gq, gkv, *tabs)


def _split_bwd(du0, z, dqn, dkvn, dkpe_h, gq, gkv, tabs, c, ql, kvl):
    s, zw = z.shape
    ts = _pick(s, (256, 128))
    o_q, o_kv, o_kr = 2 * c, 2 * c + ql, 2 * c + ql + kvl

    def body(du0_ref, z_ref, dqn_ref, dkvn_ref, dkh_ref, gq_ref, gkv_ref, c_ref, sa_ref, sb_ref,
             dz_ref, dgq_ref, dgkv_ref):
        _acc_init(pl.program_id(0), dgq_ref, dgkv_ref)
        du0 = du0_ref[...]
        a = z_ref[:, 0:c]
        sg = _sigmoid(z_ref[:, c:2 * c])
        dz_ref[:, 0:c] = (du0 * sg).astype(BF16)
        dz_ref[:, c:2 * c] = (du0 * a * sg * (1.0 - sg)).astype(BF16)
        dq, dgq = _rms_bwd(z_ref[:, o_q:o_kv], gq_ref[...], dqn_ref[...])
        dz_ref[:, o_q:o_kv] = dq.astype(BF16)
        dgq_ref[...] += dgq
        dkv, dgkv = _rms_bwd(z_ref[:, o_kv:o_kr], gkv_ref[...], dkvn_ref[...])
        dz_ref[:, o_kv:o_kr] = dkv.astype(BF16)
        dgkv_ref[...] += dgkv
        dk = dkh_ref[:, 0:LANES]
        for h in range(1, N_HEADS):
            dk = dk + dkh_ref[:, h * LANES:(h + 1) * LANES]
        dz_ref[:, o_kr:o_kr + LANES] = _rope_t(dk, c_ref[...], sa_ref[...], sb_ref[...]).astype(BF16)

    return pl.pallas_call(
        body, name="split_bwd", grid=(s // ts,),
        in_specs=[_rows(ts, c), _rows(ts, zw), _rows(ts, ql), _rows(ts, kvl), _rows(ts, N_HEADS * LANES),
                  _vec(ql), _vec(kvl)] + [_rows(ts, LANES)] * 3,
        out_specs=[_rows(ts, zw), _vec(ql), _vec(kvl)],
        out_shape=[jax.ShapeDtypeStruct((s, zw), BF16), jax.ShapeDtypeStruct((1, ql), F32),
                   jax.ShapeDtypeStruct((1, kvl), F32)],
        compiler_params=_params(("arbitrary",)),
    )(du0, z, dqn, dkvn, dkpe_h, gq, gkv, *tabs)


def _q_rope(qpre, tabs, transpose, out_dtype, name):
    s, w = qpre.shape
    ts = _pick(s, (256, 128))
    rot = _rope_t if transpose else _rope

    def body(q_ref, c_ref, sa_ref, sb_ref, o_ref):
        cs, sa, sb = c_ref[...], sa_ref[...], sb_ref[...]
        for h in range(N_HEADS):
            lo = h * HEAD_PAD
            o_ref[:, lo:lo + QK_NOPE] = q_ref[:, lo:lo + QK_NOPE].astype(out_dtype)
            o_ref[:, lo + QK_NOPE:lo + HEAD_PAD] = rot(q_ref[:, lo + QK_NOPE:lo + HEAD_PAD], cs, sa, sb).astype(out_dtype)

    return pl.pallas_call(
        body, name=name, grid=(s // ts,),
        in_specs=[_rows(ts, w)] + [_rows(ts, LANES)] * 3, out_specs=_rows(ts, w),
        out_shape=jax.ShapeDtypeStruct((s, w), out_dtype),
        compiler_params=_params(("parallel",)),
    )(qpre, *tabs)


def _conv_fwd(u0, w, b):
    s, c = u0.shape
    tc = LANES
    rc = _pick(s, (256, 128))

    def body(u_ref, w_ref, b_ref, o_ref, pad_ref):
        pad_ref[0:CONV_PAD, :] = jnp.zeros((CONV_PAD, tc), F32)
        pad_ref[CONV_PAD:CONV_PAD + s, :] = u_ref[...]
        for r in range(s // rc):
            acc = jnp.broadcast_to(b_ref[...], (rc, tc))
            for k in range(CONV_K):
                lo = r * rc + CONV_PAD - (CONV_K - 1) + k
                acc = acc + w_ref[k:k + 1, :] * pad_ref[lo:lo + rc, :]
            o_ref[r * rc:(r + 1) * rc, :] = acc

    col = lambda j: (0, j)
    return pl.pallas_call(
        body, name="conv_fwd", grid=(c // tc,),
        in_specs=[pl.BlockSpec((s, tc), col), pl.BlockSpec((CONV_K, tc), col), pl.BlockSpec((1, tc), col)],
        out_specs=pl.BlockSpec((s, tc), col),
        out_shape=jax.ShapeDtypeStruct((s, c), F32),
        scratch_shapes=[pltpu.VMEM((s + CONV_PAD, tc), F32)],
        compiler_params=_params(("parallel",)),
    )(u0, w, b)


def _conv_bwd(du1, u0, w):
    s, c = u0.shape
    tc = LANES
    rc = _pick(s, (256, 128))

    def body(d_ref, u_ref, w_ref, du_ref, dw_ref, db_ref, upad_ref, dpad_ref):
        upad_ref[0:CONV_PAD, :] = jnp.zeros((CONV_PAD, tc), F32)
        upad_ref[CONV_PAD:CONV_PAD + s, :] = u_ref[...]
        dpad_ref[0:s, :] = d_ref[...]
        dpad_ref[s:s + CONV_PAD, :] = jnp.zeros((CONV_PAD, tc), F32)
        for r in range(s // rc):
            acc = jnp.zeros((rc, tc), F32)
            for k in range(CONV_K):
                lo = r * rc + (CONV_K - 1) - k
                acc = acc + w_ref[k:k + 1, :] * dpad_ref[lo:lo + rc, :]
            du_ref[r * rc:(r + 1) * rc, :] = acc
        for k in range(CONV_K):
            acc8 = jnp.zeros((8, tc), F32)
            for r in range(s // rc):
                lo = r * rc + CONV_PAD - (CONV_K - 1) + k
                prod = d_ref[r * rc:(r + 1) * rc, :] * upad_ref[lo:lo + rc, :]
                acc8 = acc8 + jnp.sum(prod.reshape(rc // 8, 8, tc), axis=0)
            dw_ref[k:k + 1, :] = jnp.sum(acc8, axis=0, keepdims=True)
        db_ref[...] = jnp.sum(d_ref[...], axis=0, keepdims=True)

    col = lambda j: (0, j)
    return pl.pallas_call(
        body, name="conv_bwd", grid=(c // tc,),
        in_specs=[pl.BlockSpec((s, tc), col), pl.BlockSpec((s, tc), col), pl.BlockSpec((CONV_K, tc), col)],
        out_specs=[pl.BlockSpec((s, tc), col), pl.BlockSpec((CONV_K, tc), col), pl.BlockSpec((1, tc), col)],
        out_shape=[jax.ShapeDtypeStruct((s, c), F32), jax.ShapeDtypeStruct((CONV_K, c), F32),
                   jax.ShapeDtypeStruct((1, c), F32)],
        scratch_shapes=[pltpu.VMEM((s + CONV_PAD, tc), F32), pltpu.VMEM((s + CONV_PAD, tc), F32)],
        compiler_params=_params(("parallel",)),
    )(du1, u0, w)


def _causal_mask(sc, qi, kj, tq, tk):
    rows = qi * tq + lax.broadcasted_iota(jnp.int32, sc.shape, 0)
    cols = kj * tk + lax.broadcasted_iota(jnp.int32, sc.shape, 1)
    return jnp.where(cols <= rows, sc, NEG)


def _attn_fwd(q, kv, kpe):
    s = q.shape[0]
    tq = tk = _pick(s, (ATTN_BLOCK, 256, 128))
    reps = tk // LANES
    scale = QK_HEAD ** -0.5
    nt = (((1,), (1,)), ((), ()))

    def body(q_ref, kn_ref, v_ref, kpe_ref, o_ref, lse_ref, kf_ref, vb_ref, m_ref, l_ref, acc_ref):
        i = pl.program_id(1)

        @pl.when(i == 0)
        def _():
            kf_ref[:, 0:QK_NOPE] = kn_ref[...].astype(BF16)
            kf_ref[:, QK_NOPE:HEAD_PAD] = kpe_ref[...]
            vb_ref[...] = v_ref[...].astype(BF16)

        qb = q_ref[...]
        m_ref[...] = jnp.full((tq, LANES), NEG, F32)
        l_ref[...] = jnp.zeros((tq, LANES), F32)
        acc_ref[...] = jnp.zeros((tq, V_HEAD), F32)

        def block(j, diagonal):
            off = j * tk
            sc = lax.dot_general(qb, kf_ref[pl.ds(off, tk), :], nt, preferred_element_type=F32) * scale
            if diagonal:
                sc = _causal_mask(sc, 0, 0, tq, tk)
            m_prev = m_ref[...]
            m_new = jnp.maximum(m_prev, jnp.max(sc, axis=1, keepdims=True))
            p = jnp.exp(sc - jnp.tile(m_new, (1, reps)))
            alpha = jnp.exp(m_prev - m_new)
            l_ref[...] = alpha * l_ref[...] + jnp.sum(p, axis=1, keepdims=True)
            acc_ref[...] = alpha * acc_ref[...] + jnp.dot(p.astype(BF16), vb_ref[pl.ds(off, tk), :],
                                                          preferred_element_type=F32)
            m_ref[...] = m_new

        for qi in range(s // tq):
            @pl.when(i == qi)
            def _(qi=qi):
                for j in range(qi):
                    block(j, False)
                block(qi, True)

        o_ref[...] = acc_ref[...] / l_ref[...]
        lse_ref[...] = m_ref[...] + jnp.log(l_ref[...])

    return pl.pallas_call(
        body, name="attn_fwd", grid=(N_HEADS, s // tq),
        in_specs=[pl.BlockSpec((tq, HEAD_PAD), lambda h, i: (i, h)),
                  pl.BlockSpec((s, QK_NOPE), lambda h, i: (0, 2 * h)),
                  pl.BlockSpec((s, V_HEAD), lambda h, i: (0, 2 * h + 1)),
                  pl.BlockSpec((s, LANES), lambda h, i: (0, 0))],
        out_specs=[pl.BlockSpec((tq, V_HEAD), lambda h, i: (i, h)),
                   pl.BlockSpec((tq, LANES), lambda h, i: (i, h))],
        out_shape=[jax.ShapeDtypeStruct((s, N_HEADS * V_HEAD), F32),
                   jax.ShapeDtypeStruct((s, N_HEADS * LANES), F32)],
        scratch_shapes=[pltpu.VMEM((s, HEAD_PAD), BF16), pltpu.VMEM((s, V_HEAD), BF16),
                        pltpu.VMEM((tq, LANES), F32), pltpu.VMEM((tq, LANES), F32), pltpu.VMEM((tq, V_HEAD), F32)],
        compiler_params=_params(("parallel", "arbitrary")),
    )(q, kv, kv, kpe)


def _attn_bwd(q, kv, kpe, o, do, lse):
    s = q.shape[0]
    tq = tk = _pick(s, (ATTN_BLOCK, 256, 128))
    nq = s // tq
    reps = tk // LANES
    scale = QK_HEAD ** -0.5
    nt = (((1,), (1,)), ((), ()))
    tn = (((0,), (0,)), ((), ()))

    def body(q_ref, kn_ref, v_ref, kpe_ref, o_ref, do_ref, lse_ref, dq_ref, dkv_ref, dkpe_ref,
             kf_ref, vb_ref, dk_ref, dv_ref):
        j = pl.program_id(1)

        @pl.when(j == 0)
        def _():
            dq_ref[...] = jnp.zeros_like(dq_ref)

        kf_ref[:, 0:QK_NOPE] = kn_ref[...].astype(BF16)
        kf_ref[:, QK_NOPE:HEAD_PAD] = kpe_ref[...]
        vb_ref[...] = v_ref[...].astype(BF16)
        dk_ref[...] = jnp.zeros_like(dk_ref)
        dv_ref[...] = jnp.zeros_like(dv_ref)

        def block(i, diagonal):
            off = i * tq
            qb = q_ref[pl.ds(off, tq), :]
            dob = do_ref[pl.ds(off, tq), :]
            delta = jnp.sum(dob * o_ref[pl.ds(off, tq), :], axis=1, keepdims=True)
            sc = lax.dot_general(qb, kf_ref[...], nt, preferred_element_type=F32) * scale
            if diagonal:
                sc = _causal_mask(sc, 0, 0, tq, tk)
            p = jnp.exp(sc - jnp.tile(lse_ref[pl.ds(off, tq), :], (1, reps)))
            dob16 = dob.astype(BF16)
            dv_ref[...] += lax.dot_general(p.astype(BF16), dob16, tn, preferred_element_type=F32)
            dp = lax.dot_general(dob16, vb_ref[...], nt, preferred_element_type=F32)
            ds = (p * (dp - delta) * scale).astype(BF16)
            dq_ref[pl.ds(off, tq), :] += jnp.dot(ds, kf_ref[...], preferred_element_type=F32)
            dk_ref[...] += lax.dot_general(ds, qb, tn, preferred_element_type=F32)

        for kj in range(nq):
            @pl.when(j == kj)
            def _(kj=kj):
                block(kj, True)
                for i in range(kj + 1, nq):
                    block(i, False)

        dkv_ref[:, 0:QK_NOPE] = dk_ref[:, 0:QK_NOPE]
        dkv_ref[:, QK_NOPE:HEAD_PAD] = dv_ref[...]
        dkpe_ref[...] = dk_ref[:, QK_NOPE:HEAD_PAD]

    head_rows = lambda w: pl.BlockSpec((s, w), lambda h, j: (0, h))
    return pl.pallas_call(
        body, name="attn_bwd", grid=(N_HEADS, s // tk),
        in_specs=[head_rows(HEAD_PAD),
                  pl.BlockSpec((tk, QK_NOPE), lambda h, j: (j, 2 * h)),
                  pl.BlockSpec((tk, V_HEAD), lambda h, j: (j, 2 * h + 1)),
                  pl.BlockSpec((tk, LANES), lambda h, j: (j, 0)),
                  head_rows(V_HEAD), head_rows(V_HEAD), head_rows(LANES)],
        out_specs=[head_rows(HEAD_PAD),
                   pl.BlockSpec((tk, HEAD_PAD), lambda h, j: (j, h)),
                   pl.BlockSpec((tk, LANES), lambda h, j: (j, h))],
        out_shape=[jax.ShapeDtypeStruct((s, N_HEADS * HEAD_PAD), F32),
                   jax.ShapeDtypeStruct((s, N_HEADS * HEAD_PAD), F32),
                   jax.ShapeDtypeStruct((s, N_HEADS * LANES), F32)],
        scratch_shapes=[pltpu.VMEM((tk, HEAD_PAD), BF16), pltpu.VMEM((tk, V_HEAD), BF16),
                        pltpu.VMEM((tk, HEAD_PAD), F32), pltpu.VMEM((tk, V_HEAD), F32)],
        compiler_params=_params(("parallel", "arbitrary")),
    )(q, kv, kv, kpe, o, do, lse)


def _mix_fwd(u1, lng, lnb, gcon, attn, gattn):
    s, c = u1.shape
    ac = attn.shape[1]
    ts = _pick(s, (256, 128))

    def body(u_ref, lg_ref, lb_ref, gc_ref, a_ref, ga_ref, o_ref):
        t3 = _silu(_ln(u_ref[...], lg_ref[...], lb_ref[...]))
        o_ref[:, 0:c] = _rms(t3, gc_ref[...]).astype(BF16)
        o_ref[:, c:c + ac] = _rms(a_ref[...], ga_ref[...]).astype(BF16)

    return pl.pallas_call(
        body, name="mix_fwd", grid=(s // ts,),
        in_specs=[_rows(ts, c), _vec(c), _vec(c), _vec(c), _rows(ts, ac), _vec(ac)],
        out_specs=_rows(ts, c + ac),
        out_shape=jax.ShapeDtypeStruct((s, c + ac), BF16),
        compiler_params=_params(("parallel",)),
    )(u1, lng, lnb, gcon, attn, gattn)


def _mix_bwd(dmixin, u1, lng, lnb, gcon, attn, gattn):
    s, c = u1.shape
    ac = attn.shape[1]
    ts = _pick(s, (256, 128))

    def body(d_ref, u_ref, lg_ref, lb_ref, gc_ref, a_ref, ga_ref,
             du_ref, da_ref, dlg_ref, dlb_ref, dgc_ref, dga_ref):
        _acc_init(pl.program_id(0), dlg_ref, dlb_ref, dgc_ref, dga_ref)
        u = u_ref[...]
        t2 = _ln(u, lg_ref[...], lb_ref[...])
        dt3, dgc = _rms_bwd(_silu(t2), gc_ref[...], d_ref[:, 0:c])
        du, dlg, dlb = _ln_bwd(u, lg_ref[...], dt3 * _silu_grad(t2))
        du_ref[...] = du
        dlg_ref[...] += dlg
        dlb_ref[...] += dlb
        dgc_ref[...] += dgc
        da, dga = _rms_bwd(a_ref[...], ga_ref[...], d_ref[:, c:c + ac])
        da_ref[...] = da
        dga_ref[...] += dga

    return pl.pallas_call(
        body, name="mix_bwd", grid=(s // ts,),
        in_specs=[_rows(ts, c + ac), _rows(ts, c), _vec(c), _vec(c), _vec(c), _rows(ts, ac), _vec(ac)],
        out_specs=[_rows(ts, c), _rows(ts, ac), _vec(c), _vec(c), _vec(c), _vec(ac)],
        out_shape=[jax.ShapeDtypeStruct((s, c), F32), jax.ShapeDtypeStruct((s, ac), F32),
                   jax.ShapeDtypeStruct((1, c), F32), jax.ShapeDtypeStruct((1, c), F32),
                   jax.ShapeDtypeStruct((1, c), F32), jax.ShapeDtypeStruct((1, ac), F32)],
        compiler_params=_params(("arbitrary",)),
    )(dmixin, u1, lng, lnb, gcon, attn, gattn)


def _post_mix_fwd(x, mix, gpost, gpre):
    s, d = x.shape
    ts = _pick(s, (256, 128))

    def body(x_ref, m_ref, gp_ref, gf_ref, x1_ref, hf_ref):
        x1 = x_ref[...] + _rms(m_ref[...], gp_ref[...])
        x1_ref[...] = x1
        hf_ref[...] = _rms(x1, gf_ref[...]).astype(BF16)

    return pl.pallas_call(
        body, name="post_mix_fwd", grid=(s // ts,),
        in_specs=[_rows(ts, d), _rows(ts, d), _vec(d), _vec(d)],
        out_specs=[_rows(ts, d), _rows(ts, d)],
        out_shape=[jax.ShapeDtypeStruct((s, d), F32), jax.ShapeDtypeStruct((s, d), BF16)],
        compiler_params=_params(("parallel",)),
    )(x, mix, gpost, gpre)


def _post_mix_bwd(dy, dhf, x1, gpre, mix, gpost):
    s, d = x1.shape
    ts = _pick(s, (256, 128))

    def body(dy_ref, dh_ref, x1_ref, gf_ref, m_ref, gp_ref, dx1_ref, dm_ref, dgf_ref, dgp_ref):
        _acc_init(pl.program_id(0), dgf_ref, dgp_ref)
        dxa, dgf = _rms_bwd(x1_ref[...], gf_ref[...], dh_ref[...])
        dx1 = dy_ref[...] + dxa
        dx1_ref[...] = dx1
        dgf_ref[...] += dgf
        dm, dgp = _rms_bwd(m_ref[...], gp_ref[...], dx1)
        dm_ref[...] = dm.astype(BF16)
        dgp_ref[...] += dgp

    return pl.pallas_call(
        body, name="post_mix_bwd", grid=(s // ts,),
        in_specs=[_rows(ts, d), _rows(ts, d), _rows(ts, d), _vec(d), _rows(ts, d), _vec(d)],
        out_specs=[_rows(ts, d), _rows(ts, d), _vec(d), _vec(d)],
        out_shape=[jax.ShapeDtypeStruct((s, d), F32), jax.ShapeDtypeStruct((s, d), BF16),
                   jax.ShapeDtypeStruct((1, d), F32), jax.ShapeDtypeStruct((1, d), F32)],
        compiler_params=_params(("arbitrary",)),
    )(dy, dhf, x1, gpre, mix, gpost)


def _ffn_up(hf, wg, wu):
    s, d = hf.shape
    nsh, fs, _ = wg.shape
    tm = _pick(s, (1024, 512, 256, 128))
    nt = (((1,), (1,)), ((), ()))

    def body(h_ref, wg_ref, wu_ref, dg_ref, du_ref, a_ref):
        h = h_ref[...]
        g = lax.dot_general(h, wg_ref[...], nt, preferred_element_type=F32)
        u = lax.dot_general(h, wu_ref[...], nt, preferred_element_type=F32)
        sg = _sigmoid(g)
        silu = g * sg
        dg_ref[...] = u * (sg * (1.0 + g * (1.0 - sg)))
        du_ref[...] = silu
        a_ref[...] = (silu * u).astype(BF16)

    w_spec = pl.BlockSpec((None, fs, d), lambda i, j: (j, 0, 0))
    o_spec = pl.BlockSpec((None, tm, fs), lambda i, j: (j, i, 0))
    return pl.pallas_call(
        body, name="ffn_up", grid=(s // tm, nsh),
        in_specs=[pl.BlockSpec((tm, d), lambda i, j: (i, 0)), w_spec, w_spec],
        out_specs=[o_spec] * 3,
        out_shape=[jax.ShapeDtypeStruct((nsh, s, fs), F32)] * 2 + [jax.ShapeDtypeStruct((nsh, s, fs), BF16)],
        compiler_params=_params(("parallel", "parallel")),
    )(hf, wg, wu)


def _ffn_down(acts, ws, name):
    n = len(acts)
    nsh, s, fs = acts[0].shape
    d = ws[0].shape[2]
    tm = _pick(s, (1024, 512, 256, 128))
    tn = _pick(d, (512, 256, 128) if n == 1 else (256, 128))
    a_mode = None if n == 1 else pl.Buffered(1)

    def body(*refs):
        acc = None
        for a_ref, w_ref in zip(refs[:n], refs[n:2 * n]):
            for j in range(nsh):
                part = jnp.dot(a_ref[j], w_ref[j], preferred_element_type=F32)
                acc = part if acc is None else acc + part
        refs[-1][...] = acc

    return pl.pallas_call(
        body, name=name, grid=(s // tm, d // tn),
        in_specs=[pl.BlockSpec((nsh, tm, fs), lambda i, j: (0, i, 0), pipeline_mode=a_mode)] * n
        + [pl.BlockSpec((nsh, fs, tn), lambda i, j: (0, 0, j))] * n,
        out_specs=pl.BlockSpec((tm, tn), lambda i, j: (i, j)),
        out_shape=jax.ShapeDtypeStruct((s, d), F32),
        compiler_params=_params(("parallel", "parallel")),
    )(*acts, *ws)


def _ffn_down_bwd(dff, wd, act_dgate, act_dup, behind):
    s, d = dff.shape
    nsh, fs, _ = wd.shape
    tm = _pick(s, (1024, 512, 256, 128))
    nt = (((1,), (1,)), ((), ()))

    def body(d_ref, w_ref, pg_ref, pu_ref, _, dg_ref, du_ref):
        dact = lax.dot_general(d_ref[...], w_ref[...], nt, preferred_element_type=F32)
        dg_ref[...] = (dact * pg_ref[...]).astype(BF16)
        du_ref[...] = (dact * pu_ref[...]).astype(BF16)

    h_spec = pl.BlockSpec((None, tm, fs), lambda i, j: (j, i, 0))
    return pl.pallas_call(
        body, name="ffn_down_bwd", grid=(s // tm, nsh),
        in_specs=[pl.BlockSpec((tm, d), lambda i, j: (i, 0)),
                  pl.BlockSpec((None, fs, d), lambda i, j: (j, 0, 0)), h_spec, h_spec,
                  pl.BlockSpec((8, LANES), lambda i, j: (0, 0))],
        out_specs=[h_spec] * 2,
        out_shape=[jax.ShapeDtypeStruct((nsh, s, fs), BF16)] * 2,
        compiler_params=_params(("parallel", "parallel")),
    )(dff, wd, act_dgate, act_dup, behind)


def _ffn_dw(hiddens, other, name):
    n = len(hiddens)
    nsh, s, fs = hiddens[0].shape
    d = other.shape[1]
    tn = (((0,), (0,)), ((), ()))

    def body(*refs):
        for a_ref, o_ref in zip(refs[:n], refs[n + 1:]):
            o_ref[...] = lax.dot_general(a_ref[...], refs[n][...], tn, preferred_element_type=F32).astype(BF16)

    return pl.pallas_call(
        body, name=name, grid=(nsh,),
        in_specs=[pl.BlockSpec((None, s, fs), lambda j: (j, 0, 0))] * n + [pl.BlockSpec((s, d), lambda j: (0, 0))],
        out_specs=[pl.BlockSpec((None, fs, d), lambda j: (j, 0, 0))] * n,
        out_shape=[jax.ShapeDtypeStruct((nsh, fs, d), BF16)] * n,
        compiler_params=_params(("parallel",)),
    )(*hiddens, other)


def _final(ff, x1, tgt, g):
    s, d = x1.shape
    ts = _pick(s, (256, 128))

    def body(ff_ref, x1_ref, t_ref, g_ref, loss_ref, dy_ref, dff_ref, dg_ref):
        _acc_init(pl.program_id(0), loss_ref, dg_ref)
        ff_v = ff_ref[...]
        err = x1_ref[...] + _rms(ff_v, g_ref[...]) - t_ref[...]
        tok = jnp.mean(err * err, axis=-1, keepdims=True)
        loss_ref[...] += 0.5 * jnp.sum(tok, axis=0, keepdims=True)
        dy = err * (1.0 / d)
        dy_ref[...] = dy
        dff, dg = _rms_bwd(ff_v, g_ref[...], dy)
        dff_ref[...] = dff.astype(BF16)
        dg_ref[...] += dg

    return pl.pallas_call(
        body, name="final", grid=(s // ts,),
        in_specs=[_rows(ts, d), _rows(ts, d), _rows(ts, d), _vec(d)],
        out_specs=[_vec(LANES), _rows(ts, d), _rows(ts, d), _vec(d)],
        out_shape=[jax.ShapeDtypeStruct((1, LANES), F32), jax.ShapeDtypeStruct((s, d), F32),
                   jax.ShapeDtypeStruct((s, d), BF16), jax.ShapeDtypeStruct((1, d), F32)],
        compiler_params=_params(("arbitrary",)),
    )(ff, x1, tgt, g)


def _pre_bwd(dx1, dh, x, g):
    s, d = x.shape
    ts = _pick(s, (256, 128))

    def body(dx1_ref, dh_ref, x_ref, g_ref, dx_ref, dg_ref):
        _acc_init(pl.program_id(0), dg_ref)
        dxa, dg = _rms_bwd(x_ref[...], g_ref[...], dh_ref[...])
        dx_ref[...] = dx1_ref[...] + dxa
        dg_ref[...] += dg

    return pl.pallas_call(
        body, name="pre_bwd", grid=(s // ts,),
        in_specs=[_rows(ts, d), _rows(ts, d), _rows(ts, d), _vec(d)],
        out_specs=[_rows(ts, d), _vec(d)],
        out_shape=[jax.ShapeDtypeStruct((s, d), F32), jax.ShapeDtypeStruct((1, d), F32)],
        compiler_params=_params(("arbitrary",)),
    )(dx1, dh, x, g)


def _local_step(x, pos, tgt, vecs, in_weights_fn, mix_weights_fn, up_weights_fn, down_weights_fn, loss_fn,
                grads_fn):
    c = vecs["conv_b"].shape[1]
    ql = vecs["q_norm"].shape[1]
    kvl = vecs["kv_norm"].shape[1]
    half = jnp.arange(0, QK_ROPE, 2, dtype=F32)
    freq = ROPE_THETA ** (-half / QK_ROPE)
    inv_freq = jnp.concatenate([freq, freq, jnp.zeros((LANES - QK_ROPE,), F32)])[None, :]
    tabs = _rope_tables(pos, inv_freq)

    h = _pre_fwd(x, vecs["pre_mix_norm"])
    w_in_t, zero = in_weights_fn(h)
    z = _mm(h, w_in_t, "nt", "mm_z")
    w_uq_t, w_ukv, conv_w, w_out = mix_weights_fn(z)
    u0, qn, kvn, kpe = _split_fwd(z, vecs["q_norm"] + zero, vecs["kv_norm"], tabs, c, ql, kvl)
    u1 = _conv_fwd(u0, conv_w, vecs["conv_b"])
    q = _q_rope(_mm(qn, w_uq_t, "nt", "mm_q"), tabs, False, BF16, "q_rope")
    kv = _mm(kvn, w_ukv, "nn", "mm_kv")
    attn, lse = _attn_fwd(q, kv, kpe)
    mixin = _mix_fwd(u1, vecs["conv_ln_g"], vecs["conv_ln_b"], vecs["conv_out_norm"], attn, vecs["attn_out_norm"])
    mix = _mm(mixin, w_out, "nn", "mm_mix")
    x1, hf = _post_mix_fwd(x, mix, vecs["post_mix_norm"], vecs["pre_ffn_norm"])
    w_gate, w_up = up_weights_fn(mix)
    act_dgate, act_dup, act = _ffn_up(hf, w_gate, w_up)
    w_down = down_weights_fn(act)
    ff = _ffn_down([act], [w_down], "ffn_down")
    loss, dy, dff, d_post_ffn = _final(ff, x1, tgt, vecs["post_ffn_norm"])

    g = {"post_ffn_norm": d_post_ffn}
    zero = loss_fn(loss) + grads_fn("down", ("w_down",), _ffn_dw([act], dff, "ffn_dw_down"), dff)
    dgate, dup = _ffn_down_bwd(dff, w_down, act_dgate, act_dup, jnp.zeros((8, LANES), F32) + zero)
    zero = zero + grads_fn("up", ("w_gate", "w_up"), _ffn_dw([dgate, dup], hf, "ffn_dw_up"), dgate)
    dhf = _ffn_down([dgate, dup], [w_gate, w_up], "ffn_dhf")
    dx1, dmix, g["pre_ffn_norm"], g["post_mix_norm"] = _post_mix_bwd(
        dy, dhf, x1, vecs["pre_ffn_norm"] + zero, mix, vecs["post_mix_norm"])
    dmixin = _mm(dmix, w_out, "nt", "mm_dmixin")
    dw_out = _mm(mixin, dmix, "tn", "mm_dw_out", BF16)
    zero = grads_fn("out", ("w_out",), [dw_out], dmix)
    du1, dattn, g["conv_ln_g"], g["conv_ln_b"], g["conv_out_norm"], g["attn_out_norm"] = _mix_bwd(
        dmixin, u1, vecs["conv_ln_g"] + zero, vecs["conv_ln_b"], vecs["conv_out_norm"], attn, vecs["attn_out_norm"])
    du0, dw_conv, g["conv_b"] = _conv_bwd(du1, u0, conv_w)
    dq, dkv, dkpe_h = _attn_bwd(q, kv, kpe, attn, dattn, lse)
    dqpre = _q_rope(dq, tabs, True, BF16, "q_rope_bwd")
    dqn = _mm(dqpre, w_uq_t, "nn", "mm_dqn")
    dw_uq = _mm(dqpre, qn, "tn", "mm_dw_uq", BF16)
    dkvn = _mm(dkv, w_ukv, "nt", "mm_dkvn")
    dw_ukv = _mm(kvn, dkv, "tn", "mm_dw_ukv", BF16)
    zero = grads_fn("", (), [], dkvn)
    dz, g["q_norm"], g["kv_norm"] = _split_bwd(du0, z, dqn, dkvn, dkpe_h, vecs["q_norm"] + zero, vecs["kv_norm"], tabs,
                                               c, ql, kvl)
    dw_in = _mm(dz, h, "tn", "mm_dw_in", BF16)
    zero = grads_fn("in", ("w_in", "w_uq", "w_ukv", "conv_w"), [dw_in, dw_uq, dw_ukv, dw_conv], dz)
    dh = _mm(dz, w_in_t, "nn", "mm_dh")
    zero = zero + grads_fn("", (), [], dh)
    grad_x, g["pre_mix_norm"] = _pre_bwd(dx1, dh, x, vecs["pre_mix_norm"] + zero)
    return loss, grad_x, g


def _my_index():
    return 4 * lax.axis_index("x") + 2 * lax.axis_index("y") + lax.axis_index("c")


def _coords(idx):
    return ((idx >> 2) & 1, (idx >> 1) & 1, idx & 1)


def _place():
    x, y, c = lax.axis_index("x"), lax.axis_index("y"), lax.axis_index("c")
    return (x, y, c), (x, y, 1 - c), [(1 - x, y), (x, 1 - y), (1 - x, 1 - y)]


def _small_copies(src, land, send_sem, recv_sem):
    me = _my_index()
    return [pltpu.make_async_remote_copy(src_ref=src, dst_ref=land.at[me], send_sem=send_sem, recv_sem=recv_sem,
                                         device_id=_coords(me ^ p), device_id_type=MESH)
            for p in range(1, N_DEV)]


HBM_SPEC = pl.BlockSpec(memory_space=pltpu.HBM)
SEM_SPEC = pl.BlockSpec(memory_space=pltpu.SEMAPHORE)
DATAFLOW = pltpu.SideEffectType.DATAFLOW_SIDE_EFFECTING


def _split_start(name, copies_of, srcs, lands, after):
    n = len(srcs)

    def body(*refs):
        outs = refs[2 * n + 1:]
        for k in range(n):
            for cp in copies_of(refs[k], refs[n + k], outs[k], outs[n + k]):
                cp.start()
        outs[-1][...] = jnp.zeros_like(outs[-1])

    hbm = lambda a: pltpu.HBM(a.shape, a.dtype)
    out = pl.pallas_call(
        body, name=name,
        in_specs=[HBM_SPEC] * (2 * n + 1),
        out_specs=[SEM_SPEC] * (2 * n) + [HBM_SPEC] * (2 * n) + [pl.BlockSpec(memory_space=pltpu.VMEM)],
        out_shape=[pltpu.SemaphoreType.DMA(())] * (2 * n) + [hbm(a) for a in srcs] + [hbm(a) for a in lands]
        + [jax.ShapeDtypeStruct((8, LANES), F32)],
        input_output_aliases={k: 2 * n + k for k in range(2 * n)},
        compiler_params=pltpu.CompilerParams(has_side_effects=DATAFLOW),
    )(*[pltpu.with_memory_space_constraint(a, pltpu.HBM) for a in list(srcs) + list(lands) + [after]])
    return (out[:n], out[n:2 * n], out[2 * n:3 * n], out[3 * n:4 * n]), out[-1][0, 0]


def _split_wait(name, n_copies, started, after):
    send_sems, recv_sems, srcs, lands = started
    n = len(srcs)

    def body(*refs):
        for k in range(n):
            slots = refs[n + k].at[pl.ds(0, n_copies)]
            all_copies = pltpu.make_async_remote_copy(
                src_ref=slots, dst_ref=slots, send_sem=refs[2 * n + k], recv_sem=refs[3 * n + k],
                device_id=_place()[0], device_id_type=MESH)
            all_copies.wait_send()
            all_copies.wait_recv()

    hbm = lambda a: pltpu.HBM(a.shape, a.dtype)
    out = pl.pallas_call(
        body, name=name,
        in_specs=[HBM_SPEC] * (2 * n) + [SEM_SPEC] * (2 * n) + [HBM_SPEC],
        out_specs=[HBM_SPEC] * (2 * n),
        out_shape=[hbm(a) for a in srcs] + [hbm(a) for a in lands],
        input_output_aliases={k: k for k in range(2 * n)},
        compiler_params=pltpu.CompilerParams(has_side_effects=DATAFLOW),
    )(*srcs, *lands, *send_sems, *recv_sems, pltpu.with_memory_space_constraint(after, pltpu.HBM))
    return out[:n], out[n:]


def _slot(chip, core):
    return 4 * chip[0] + 2 * chip[1] + core


def _gather_copies(src, land, send_sem, recv_sem):
    (x, y, c), sib, chips = _place()
    return [pltpu.make_async_remote_copy(src_ref=src, dst_ref=land.at[_slot((x, y), c)], send_sem=send_sem,
                                         recv_sem=recv_sem, device_id=to, device_id_type=MESH)
            for to in [sib] + [(*chip, c) for chip in chips]]


def _gather_pass_on(lands, name):
    n = len(lands)

    def body(*refs):
        ins, outs = refs[:n], refs[n:2 * n]
        send_sems, recv_sems = refs[2 * n:]
        (x, y, c), sib, chips = _place()
        sends = []
        for k in range(n):
            for j, chip in enumerate(chips):
                sends.append(pltpu.make_async_remote_copy(
                    src_ref=ins[k].at[_slot(chip, c)], dst_ref=outs[k].at[_slot(chip, c)],
                    send_sem=send_sems.at[k, j], recv_sem=recv_sems.at[k, j], device_id=sib, device_id_type=MESH))
        for cp in sends:
            cp.start()
        for cp in sends:
            cp.wait_recv()
        for cp in sends:
            cp.wait_send()

    any_spec = pl.BlockSpec(memory_space=pl.ANY)
    return pl.pallas_call(
        body, name=name,
        in_specs=[any_spec] * n, out_specs=[any_spec] * n,
        out_shape=[jax.ShapeDtypeStruct(a.shape, a.dtype) for a in lands],
        input_output_aliases={k: k for k in range(n)},
        scratch_shapes=[pltpu.SemaphoreType.DMA((n, 3))] * 2,
        compiler_params=pltpu.CompilerParams(has_side_effects=True),
    )(*lands)


def _chip_copies(src, land, send_sem, recv_sem):
    (x, y, c), _, chips = _place()
    return [pltpu.make_async_remote_copy(src_ref=src.at[2 * chip[0] + chip[1]], dst_ref=land.at[j], send_sem=send_sem,
                                         recv_sem=recv_sem, device_id=(*chip, c), device_id_type=MESH)
            for j, chip in enumerate(chips)]


ROW_TILE_BYTES = 14 * 1024 * 1024


def _stream_tile(r, c, bytes_per_elem):
    if r * c * bytes_per_elem <= ROW_TILE_BYTES:
        return r, c
    rows = [t for t in range(16, r, 16) if r % t == 0 and t * c * bytes_per_elem <= ROW_TILE_BYTES]
    if rows:
        return max(rows), c
    cols = [t for t in range(LANES, c, LANES) if c % t == 0 and r * t * bytes_per_elem <= ROW_TILE_BYTES]
    return r, max(cols)


def _sibling_copies(src, land, send_sem, recv_sem):
    (x, y, c), sib, _ = _place()
    return [pltpu.make_async_remote_copy(src_ref=src.at[2 * q + 1 - c], dst_ref=land.at[q], send_sem=send_sem,
                                         recv_sem=recv_sem, device_id=sib, device_id_type=MESH)
            for q in range(4)]


def _pair_sum(core, blocks, theirs, name):
    q, r, c = theirs.shape
    tr, tc = _stream_tile(r, c, 3 * theirs.dtype.itemsize)

    def body(core_ref, a_ref, b_ref, o_ref):
        o_ref[...] = (a_ref[...].astype(F32) + b_ref[...].astype(F32)).astype(o_ref.dtype)

    blk = pl.BlockSpec((1, tr, tc), lambda i, j, k, core_ref: (i, j, k))
    mine = pl.BlockSpec((1, tr, tc), lambda i, j, k, core_ref: (2 * i + core_ref[0], j, k))
    return pl.pallas_call(
        body, name=name,
        grid_spec=pltpu.PrefetchScalarGridSpec(num_scalar_prefetch=1, grid=(q, r // tr, c // tc),
                                               in_specs=[mine, blk], out_specs=blk),
        out_shape=jax.ShapeDtypeStruct(theirs.shape, theirs.dtype),
        compiler_params=_params(("parallel", "parallel", "parallel")),
    )(core, blocks, theirs)


def _adamw_small(parts, ws, ms, vs):
    n = len(ws)
    c1 = 1.0 - ADAM_B1
    c2 = 1.0 - ADAM_B2
    bc1 = 1.0 - ADAM_B1 ** ADAM_STEP
    bc2 = 1.0 - ADAM_B2 ** ADAM_STEP

    def body(*refs):
        p_ref, outs = refs[0], refs[1 + 3 * n:]
        off = 0
        for k in range(n):
            w_ref, m_ref, v_ref = refs[1 + k], refs[1 + n + k], refs[1 + 2 * n + k]
            width = w_ref.shape[1]
            g = p_ref[0, :, off:off + width]
            for j in range(1, N_DEV):
                g = g + p_ref[j, :, off:off + width]
            nm = ADAM_B1 * m_ref[...] + c1 * g
            nv = ADAM_B2 * v_ref[...] + c2 * (g * g)
            outs[k][...] = g
            outs[n + k][...] = -ADAM_LR * ((nm / bc1) / (jnp.sqrt(nv / bc2) + ADAM_EPS) + ADAM_WD * w_ref[...])
            outs[2 * n + k][...] = nm
            outs[3 * n + k][...] = nv
            off += width

    out = pl.pallas_call(
        body, name="adamw_small",
        out_shape=[jax.ShapeDtypeStruct(a.shape, F32) for a in ws] * 4,
        compiler_params=pltpu.CompilerParams(vmem_limit_bytes=VMEM_LIMIT),
    )(parts, *ws, *ms, *vs)
    return out[:n], out[n:2 * n], out[2 * n:3 * n], out[3 * n:]


def _reduce_adamw(parts, w, m, v, name, own, own_slot):
    r, c = w.shape
    n_parts = parts.shape[0]
    tr, tc = _stream_tile(r, c, (n_parts + 1) * parts.dtype.itemsize + 7 * 4)
    c1 = 1.0 - ADAM_B1
    c2 = 1.0 - ADAM_B2
    bc1 = 1.0 - ADAM_B1 ** ADAM_STEP
    bc2 = 1.0 - ADAM_B2 ** ADAM_STEP

    def body(_, o_ref, p_ref, w_ref, m_ref, v_ref, g_ref, d_ref, nm_ref, nv_ref):
        g = o_ref[0].astype(F32)
        for j in range(n_parts):
            g = g + p_ref[j].astype(F32)
        nm = ADAM_B1 * m_ref[...] + c1 * g
        nv = ADAM_B2 * v_ref[...] + c2 * (g * g)
        g_ref[...] = g
        nm_ref[...] = nm
        nv_ref[...] = nv
        d_ref[...] = -ADAM_LR * ((nm / bc1) / (jnp.sqrt(nv / bc2) + ADAM_EPS) + ADAM_WD * w_ref[...])

    out = jax.ShapeDtypeStruct((r, c), F32)
    grid = (r // tr, c // tc)
    blk = pl.BlockSpec((tr, tc), lambda i, j, slot_ref: (i, j))
    return pl.pallas_call(
        body, name=name,
        grid_spec=pltpu.PrefetchScalarGridSpec(
            num_scalar_prefetch=1, grid=grid,
            in_specs=[pl.BlockSpec((1, tr, tc), lambda i, j, slot_ref: (slot_ref[0], i, j)),
                      pl.BlockSpec((n_parts, tr, tc), lambda i, j, slot_ref: (0, i, j)), blk, blk, blk],
            out_specs=[blk] * 4),
        out_shape=[out] * 4,
        compiler_params=_params(("parallel", "parallel")),
    )(own_slot, own, parts, w, m, v)


_MIX = ("w_in", "w_uq", "w_ukv", "conv_w", "w_out")
_FFN = ("w_gate", "w_up", "w_down")
_BIG = _MIX + _FFN
_TRANSPOSED = ("w_in", "w_uq", "w_gate", "w_up")
_SMALL = ("pre_mix_norm", "q_norm", "kv_norm", "conv_b", "conv_ln_g", "conv_ln_b", "conv_out_norm",
          "attn_out_norm", "post_mix_norm", "pre_ffn_norm", "post_ffn_norm")
_ORDER = ("pre_mix_norm", "w_in", "q_norm", "w_uq", "kv_norm", "w_ukv", "conv_w", "conv_b", "conv_ln_g",
          "conv_ln_b", "conv_out_norm", "attn_out_norm", "w_out", "post_mix_norm", "pre_ffn_norm", "w_gate",
          "w_up", "w_down", "post_ffn_norm")


def _cols_from_shards(g):
    return jnp.transpose(g, (1, 0, 2)).reshape(g.shape[1], N_DEV * g.shape[2])


def _cols_to_shards(w):
    k, n8 = w.shape
    return jnp.transpose(w.reshape(k, N_DEV, n8 // N_DEV), (1, 0, 2))


def _step(x, positions, loss_target, w, m, v):
    s, d = x.shape[1], x.shape[2]
    x2, tgt = x[0], loss_target[0]
    pos = positions.reshape(s, 1)
    vecs = {n: w[n] for n in _SMALL}
    core = lax.axis_index("c").astype(jnp.int32).reshape(1)
    my_chip = (2 * lax.axis_index("x") + lax.axis_index("y")).astype(jnp.int32).reshape(1)
    n_in_cols = N_DEV * w["w_in"].shape[2]
    gathers, scatters, to_sibling = {}, {}, []

    def shard(t, n):
        return t[n][0].T if n in _TRANSPOSED else t[n][0]

    def gather_start(names, tag, after, zero=0.0):
        srcs = [w[n][0] if n == "conv_w" else (shard(w, n) + zero).astype(BF16) for n in names]
        lands = [lax.empty((N_DEV,) + a.shape, a.dtype) for a in srcs]
        gathers[tag], zero = _split_start("gather_" + tag + "_start", _gather_copies, srcs, lands, after)
        return zero

    def gather_finish(names, tag, after):
        srcs, lands = _split_wait("gather_" + tag + "_wait", 4, gathers[tag], after)
        lands = _gather_pass_on(lands, "gather_" + tag + "_pass_on")
        me = _my_index()
        return {n: lax.dynamic_update_slice(g, a[None], (me,) + (0,) * a.ndim) for n, g, a in zip(names, lands, srcs)}

    def in_weights_fn(h):
        w_in_g = gather_finish(("w_in",), "in", h)["w_in"]
        return jnp.pad(w_in_g.reshape(-1, d), ((0, LANES - QK_ROPE), (0, 0))), 0.0

    def mix_weights_fn(z):
        gath = gather_finish(_MIX[1:], "mix", z)
        w_uq_t = jnp.pad(gath["w_uq"], ((0, 0), (0, HEAD_PAD - QK_HEAD), (0, 0))).reshape(N_HEADS * HEAD_PAD, -1)
        return (w_uq_t, _cols_from_shards(gath["w_ukv"]), _cols_from_shards(gath["conv_w"]),
                gath["w_out"].reshape(-1, d))

    def up_weights_fn(mix):
        gath = gather_finish(("w_gate", "w_up"), "up", mix)
        return gath["w_gate"], gath["w_up"]

    def down_weights_fn(act):
        return gather_finish(("w_down",), "down", act)["w_down"]

    def scatter_advance(after):
        if not to_sibling:
            return 0.0
        names, tag, sent = to_sibling.pop()
        blocks, theirs = _split_wait("to_sibling_" + tag + "_wait", 4, sent, after)
        pairs = [_pair_sum(core, b, t, "pair_sum_" + n) for n, b, t in zip(names, blocks, theirs)]
        lands = [lax.empty((3,) + p.shape[1:], p.dtype) for p in pairs]
        scatters[tag], zero = _split_start("scatter_" + tag + "_start", _chip_copies, pairs, lands, theirs[0])
        return zero

    to_blocks = {
        "w_in": lambda a: a[:n_in_cols].reshape(N_DEV, -1, d),
        "w_uq": lambda a: a.reshape(N_HEADS, HEAD_PAD, -1)[:, :QK_HEAD],
        "w_ukv": _cols_to_shards, "conv_w": _cols_to_shards,
        "w_out": lambda a: a.reshape(N_DEV, -1, d),
    }

    def grads_fn(tag, names, grads, after):
        zero = scatter_advance(after)
        if not grads:
            return zero
        blocks = [to_blocks.get(n, lambda a: a)(a) for n, a in zip(names, grads)]
        lands = [lax.empty((4,) + b.shape[1:], b.dtype) for b in blocks]
        sent, zero2 = _split_start("to_sibling_" + tag + "_start", _sibling_copies, blocks, lands, after)
        to_sibling.append((names, tag, sent))
        return zero + zero2

    zero = gather_start(("w_in",), "in", x2)
    zero = gather_start(_MIX[1:], "mix", x2, zero)
    zero = gather_start(("w_gate", "w_up"), "up", x2, zero)
    vecs["pre_mix_norm"] = vecs["pre_mix_norm"] + gather_start(("w_down",), "down", x2, zero)
    totals = []

    def loss_fn(loss):
        totals.append(lax.psum(loss[0, 0], ("x", "y", "c")))
        return jnp.minimum(jnp.abs(totals[0]), 0.0)

    _, grad_x, g = _local_step(x2, pos, tgt, vecs, in_weights_fn, mix_weights_fn, up_weights_fn, down_weights_fn,
                               loss_fn, grads_fn)
    total = totals[0]

    small = jnp.concatenate([g[n] for n in _SMALL], axis=1)
    small_started, zero = _split_start("gather_small_start", _small_copies, [small],
                                       [lax.empty((N_DEV,) + small.shape, F32)], grad_x)

    res = {}
    after = grad_x
    my_chip = my_chip + zero.astype(jnp.int32)
    for tag, names in (("down", ("w_down",)), ("up", ("w_gate", "w_up")), ("out", ("w_out",)),
                       ("in", ("w_in", "w_uq", "w_ukv", "conv_w"))):
        pairs, recv = _split_wait("scatter_" + tag + "_wait", 3, scatters[tag], after)
        for n, own, parts in zip(names, pairs, recv):
            res[n] = _reduce_adamw(parts, shard(w, n), shard(m, n), shard(v, n), "adamw_" + n, own=own,
                                   own_slot=my_chip)
            after = res[n][1]
            res[n] = [(t.T if n in _TRANSPOSED else t)[None] for t in res[n]]
    (small,), (small_all,) = _split_wait("gather_small_wait", N_DEV - 1, small_started, after)
    small_all = lax.dynamic_update_slice(small_all, small[None], (_my_index(), 0, 0))
    small_res = _adamw_small(small_all, *[[t[n] for n in _SMALL] for t in (w, m, v)])
    for k, n in enumerate(_SMALL):
        res[n] = [part[k] for part in small_res]

    outs = [total, grad_x[None]]
    for part in range(4):
        outs.extend(res[n][part] for n in _ORDER)
    return tuple(outs)


def kernel(x, positions, pre_mix_norm, w_in, q_norm, w_uq, kv_norm, w_ukv, conv_w, conv_b, conv_ln_g, conv_ln_b, conv_out_norm, attn_out_norm, w_out, post_mix_norm, pre_ffn_norm, w_gate, w_up, w_down, post_ffn_norm, loss_target, m_pre_mix_norm, m_w_in, m_q_norm, m_w_uq, m_kv_norm, m_w_ukv, m_conv_w, m_conv_b, m_conv_ln_g, m_conv_ln_b, m_conv_out_norm, m_attn_out_norm, m_w_out, m_post_mix_norm, m_pre_ffn_norm, m_w_gate, m_w_up, m_w_down, m_post_ffn_norm, v_pre_mix_norm, v_w_in, v_q_norm, v_w_uq, v_kv_norm, v_w_ukv, v_conv_w, v_conv_b, v_conv_ln_g, v_conv_ln_b, v_conv_out_norm, v_attn_out_norm, v_w_out, v_post_mix_norm, v_pre_ffn_norm, v_w_gate, v_w_up, v_w_down, v_post_ffn_norm):
    w = dict(zip(_ORDER, (pre_mix_norm, w_in, q_norm, w_uq, kv_norm, w_ukv, conv_w, conv_b, conv_ln_g, conv_ln_b,
                          conv_out_norm, attn_out_norm, w_out, post_mix_norm, pre_ffn_norm, w_gate, w_up, w_down,
                          post_ffn_norm)))
    m = dict(zip(_ORDER, (m_pre_mix_norm, m_w_in, m_q_norm, m_w_uq, m_kv_norm, m_w_ukv, m_conv_w, m_conv_b,
                          m_conv_ln_g, m_conv_ln_b, m_conv_out_norm, m_attn_out_norm, m_w_out, m_post_mix_norm,
                          m_pre_ffn_norm, m_w_gate, m_w_up, m_w_down, m_post_ffn_norm)))
    v = dict(zip(_ORDER, (v_pre_mix_norm, v_w_in, v_q_norm, v_w_uq, v_kv_norm, v_w_ukv, v_conv_w, v_conv_b,
                          v_conv_ln_g, v_conv_ln_b, v_conv_out_norm, v_attn_out_norm, v_w_out, v_post_mix_norm,
                          v_pre_ffn_norm, v_w_gate, v_w_up, v_w_down, v_post_ffn_norm)))
    return _step(x, positions, loss_target, w, m, v)
```

```python
import functools

import jax
import jax.numpy as jnp
from jax import lax
from jax.experimental import pallas as pl
from jax.experimental.pallas import tpu as pltpu

N_DEV = 8
N_HEADS = 8
QK_NOPE = 128
QK_ROPE = 64
V_HEAD = 128
QK_HEAD = QK_NOPE + QK_ROPE
HEAD_PAD = 256
LANES = 128
ATTN_BLOCK = 512
CONV_K = 31
CONV_PAD = 32
EPS = 1e-6
ROPE_THETA = 10000.0
ADAM_LR = 0.001
ADAM_B1 = 0.9
ADAM_B2 = 0.999
ADAM_EPS = 1e-08
ADAM_WD = 0.01
ADAM_STEP = 10
VMEM_LIMIT = 56 * 1024 * 1024
F32 = jnp.float32
BF16 = jnp.bfloat16
MESH = pl.DeviceIdType.MESH
NEG = -1e30


def _pick(n, prefs):
    for p in prefs:
        if p <= n and n % p == 0:
            return p
    return n


def _params(sem):
    return pltpu.CompilerParams(dimension_semantics=sem, vmem_limit_bytes=VMEM_LIMIT)


_DIMS = {"nn": (((1,), (0,)), ((), ())), "nt": (((1,), (1,)), ((), ())), "tn": (((0,), (0,)), ((), ()))}


MM_VMEM_BUDGET = 40 * 1024 * 1024
MM_MAX_MACS = 3 * 1024 ** 3


V7X_HBM_BYTES_PER_S = 3.0e12
V7X_MXU_MACS_PER_S = 0.45e15
GRID_STEP_S = 0.35e-6


def _mm_tiles(m, n, k, size_a, size_b, size_o):
    best = None
    for tm in sorted({m, 1024, 512, 256, 128}, reverse=True):
        if tm > m or m % tm:
            continue
        for tn in sorted({n, 2048, 1024, 512, 384, 256, 128}, reverse=True):
            if tn > n or n % tn:
                continue
            vmem = 2 * (tm * k * size_a + k * tn * size_b + tm * tn * size_o)
            if vmem > MM_VMEM_BUDGET or tm * tn * k > MM_MAX_MACS:
                continue
            b_reads = 1 if tn == n else m // tm
            traffic = m * k * size_a + b_reads * k * n * size_b + m * n * size_o
            exposed = tm * k * size_a + k * tn * size_b + tm * tn * size_o
            steps = (m // tm) * (n // tn)
            key = (max(traffic / V7X_HBM_BYTES_PER_S, m * n * k / V7X_MXU_MACS_PER_S)
                   + exposed / V7X_HBM_BYTES_PER_S + steps * GRID_STEP_S)
            if best is None or key < best[0]:
                best = (key, tm, tn)
    assert best is not None, (m, n, k)
    return best[1], best[2]


def _mm(a, b, mode, name, out_dtype=F32, add=None):
    if mode == "nn":
        (m, k), (k2, n) = a.shape, b.shape
    elif mode == "nt":
        (m, k), (n, k2) = a.shape, b.shape
    else:
        (k, m), (k2, n) = a.shape, b.shape
    assert k == k2, (a.shape, b.shape, mode)
    tm, tn = _mm_tiles(m, n, k, a.dtype.itemsize, b.dtype.itemsize,
                       jnp.dtype(out_dtype).itemsize + (0 if add is None else 4))
    dims = _DIMS[mode]

    def body(a_ref, b_ref, *rest):
        acc = lax.dot_general(a_ref[...].astype(BF16), b_ref[...].astype(BF16), dims, preferred_element_type=F32)
        if add is not None:
            acc = acc + rest[0][...]
        rest[-1][...] = acc.astype(rest[-1].dtype)

    if mode == "tn":
        a_spec = pl.BlockSpec((k, tm), lambda i, j: (0, i))
    else:
        a_spec = pl.BlockSpec((tm, k), lambda i, j: (i, 0))
    if mode == "nt":
        b_spec = pl.BlockSpec((tn, k), lambda i, j: (j, 0))
    else:
        b_spec = pl.BlockSpec((k, tn), lambda i, j: (0, j))
    o_spec = pl.BlockSpec((tm, tn), lambda i, j: (i, j))
    extra = [] if add is None else [add]
    return pl.pallas_call(
        body, name=name,
        grid=(m // tm, n // tn),
        in_specs=[a_spec, b_spec] + [o_spec] * len(extra),
        out_specs=o_spec,
        out_shape=jax.ShapeDtypeStruct((m, n), out_dtype),
        compiler_params=_params(("parallel", "parallel")),
    )(a, b, *extra)


def _sigmoid(x):
    return 1.0 / (1.0 + jnp.exp(-x))


def _rms(x, g):
    r = lax.rsqrt(jnp.mean(x * x, axis=-1, keepdims=True) + EPS)
    return (x * r) * g


def _rms_bwd(x, g, dy):
    r = lax.rsqrt(jnp.mean(x * x, axis=-1, keepdims=True) + EPS)
    xh = x * r
    dyg = dy * g
    dx = r * (dyg - xh * jnp.mean(dyg * xh, axis=-1, keepdims=True))
    return dx, jnp.sum(dy * xh, axis=0, keepdims=True)


def _ln(x, g, b):
    mu = jnp.mean(x, axis=-1, keepdims=True)
    xc = x - mu
    rs = lax.rsqrt(jnp.mean(xc * xc, axis=-1, keepdims=True) + EPS)
    return (xc * rs) * g + b


def _ln_bwd(x, g, dy):
    mu = jnp.mean(x, axis=-1, keepdims=True)
    xc = x - mu
    rs = lax.rsqrt(jnp.mean(xc * xc, axis=-1, keepdims=True) + EPS)
    xh = xc * rs
    dyg = dy * g
    dx = rs * (dyg - jnp.mean(dyg, axis=-1, keepdims=True) - xh * jnp.mean(dyg * xh, axis=-1, keepdims=True))
    return dx, jnp.sum(dy * xh, axis=0, keepdims=True), jnp.sum(dy, axis=0, keepdims=True)


def _silu(x):
    return x * _sigmoid(x)


def _silu_grad(x):
    s = _sigmoid(x)
    return s * (1.0 + x * (1.0 - s))


def _rope(x, cos, sa, sb):
    return x * cos + pltpu.roll(x, 96, 1) * sa + pltpu.roll(x, 32, 1) * sb


def _rope_t(d, cos, sa, sb):
    return d * cos - pltpu.roll(d, 96, 1) * sa - pltpu.roll(d, 32, 1) * sb


def _rows(ts, w):
    return pl.BlockSpec((ts, w), lambda i: (i, 0))


def _vec(w):
    return pl.BlockSpec((1, w), lambda i: (0, 0))


def _acc_init(i, *refs):
    @pl.when(i == 0)
    def _():
        for r in refs:
            r[...] = jnp.zeros_like(r)


def _rope_tables(pos, inv_freq):
    s = pos.shape[0]
    ts = _pick(s, (512, 256, 128))

    def body(p_ref, f_ref, c_ref, sa_ref, sb_ref):
        ang = p_ref[...].astype(F32) * f_ref[...]
        lane = lax.broadcasted_iota(jnp.int32, ang.shape, 1)
        c, sn = jnp.cos(ang), jnp.sin(ang)
        c_ref[...] = jnp.where(lane < QK_ROPE, c, 0.0)
        sa_ref[...] = jnp.where(lane < QK_ROPE // 2, -sn, 0.0)
        sb_ref[...] = jnp.where((lane >= QK_ROPE // 2) & (lane < QK_ROPE), sn, 0.0)

    out = jax.ShapeDtypeStruct((s, LANES), F32)
    return pl.pallas_call(
        body, name="rope_tables", grid=(s // ts,),
        in_specs=[_rows(ts, 1), _vec(LANES)],
        out_specs=[_rows(ts, LANES)] * 3, out_shape=[out] * 3,
        compiler_params=_params(("parallel",)),
    )(pos, inv_freq)


def _pre_fwd(x, g):
    s, d = x.shape
    ts = _pick(s, (256, 128))

    def body(x_ref, g_ref, h_ref):
        h_ref[...] = _rms(x_ref[...], g_ref[...]).astype(BF16)

    return pl.pallas_call(
        body, name="pre_fwd", grid=(s // ts,),
        in_specs=[_rows(ts, d), _vec(d)], out_specs=_rows(ts, d),
        out_shape=jax.ShapeDtypeStruct((s, d), BF16),
        compiler_params=_params(("parallel",)),
    )(x, g)


def _split_fwd(z, gq, gkv, tabs, c, ql, kvl):
    s, zw = z.shape
    ts = _pick(s, (256, 128))
    o_q, o_kv, o_kr = 2 * c, 2 * c + ql, 2 * c + ql + kvl

    def body(z_ref, gq_ref, gkv_ref, c_ref, sa_ref, sb_ref, u0_ref, qn_ref, kvn_ref, kpe_ref):
        u0_ref[...] = z_ref[:, 0:c] * _sigmoid(z_ref[:, c:2 * c])
        qn_ref[...] = _rms(z_ref[:, o_q:o_kv], gq_ref[...]).astype(BF16)
        kvn_ref[...] = _rms(z_ref[:, o_kv:o_kr], gkv_ref[...]).astype(BF16)
        kpe_ref[...] = _rope(z_ref[:, o_kr:o_kr + LANES], c_ref[...], sa_ref[...], sb_ref[...]).astype(BF16)

    return pl.pallas_call(
        body, name="split_fwd", grid=(s // ts,),
        in_specs=[_rows(ts, zw), _vec(ql), _vec(kvl)] + [_rows(ts, LANES)] * 3,
        out_specs=[_rows(ts, c), _rows(ts, ql), _rows(ts, kvl), _rows(ts, LANES)],
        out_shape=[jax.ShapeDtypeStruct((s, c), F32), jax.ShapeDtypeStruct((s, ql), BF16),
                   jax.ShapeDtypeStruct((s, kvl), BF16), jax.ShapeDtypeStruct((s, LANES), BF16)],
        compiler_params=_params(("parallel",)),
    )(z, gq, gkv, *tabs)


def _split_bwd(du0, z, dqn, dkvn, dkpe_h, gq, gkv, tabs, c, ql, kvl):
    s, zw = z.shape
    ts = _pick(s, (256, 128))
    o_q, o_kv, o_kr = 2 * c, 2 * c + ql, 2 * c + ql + kvl

    def body(du0_ref, z_ref, dqn_ref, dkvn_ref, dkh_ref, gq_ref, gkv_ref, c_ref, sa_ref, sb_ref,
             dz_ref, dgq_ref, dgkv_ref):
        _acc_init(pl.program_id(0), dgq_ref, dgkv_ref)
        du0 = du0_ref[...]
        a = z_ref[:, 0:c]
        sg = _sigmoid(z_ref[:, c:2 * c])
        dz_ref[:, 0:c] = (du0 * sg).astype(BF16)
        dz_ref[:, c:2 * c] = (du0 * a * sg * (1.0 - sg)).astype(BF16)
        dq, dgq = _rms_bwd(z_ref[:, o_q:o_kv], gq_ref[...], dqn_ref[...])
        dz_ref[:, o_q:o_kv] = dq.astype(BF16)
        dgq_ref[...] += dgq
        dkv, dgkv = _rms_bwd(z_ref[:, o_kv:o_kr], gkv_ref[...], dkvn_ref[...])
        dz_ref[:, o_kv:o_kr] = dkv.astype(BF16)
        dgkv_ref[...] += dgkv
        dk = dkh_ref[:, 0:LANES]
        for h in range(1, N_HEADS):
            dk = dk + dkh_ref[:, h * LANES:(h + 1) * LANES]
        dz_ref[:, o_kr:o_kr + LANES] = _rope_t(dk, c_ref[...], sa_ref[...], sb_ref[...]).astype(BF16)

    return pl.pallas_call(
        body, name="split_bwd", grid=(s // ts,),
        in_specs=[_rows(ts, c), _rows(ts, zw), _rows(ts, ql), _rows(ts, kvl), _rows(ts, N_HEADS * LANES),
                  _vec(ql), _vec(kvl)] + [_rows(ts, LANES)] * 3,
        out_specs=[_rows(ts, zw), _vec(ql), _vec(kvl)],
        out_shape=[jax.ShapeDtypeStruct((s, zw), BF16), jax.ShapeDtypeStruct((1, ql), F32),
                   jax.ShapeDtypeStruct((1, kvl), F32)],
        compiler_params=_params(("arbitrary",)),
    )(du0, z, dqn, dkvn, dkpe_h, gq, gkv, *tabs)


def _q_rope(qpre, tabs):
    s, w = qpre.shape
    ts = _pick(s, (256, 128))

    def body(q_ref, c_ref, sa_ref, sb_ref, o_ref):
        cs, sa, sb = c_ref[...], sa_ref[...], sb_ref[...]
        for h in range(N_HEADS):
            lo = h * HEAD_PAD
            o_ref[:, lo:lo + QK_NOPE] = q_ref[:, lo:lo + QK_NOPE].astype(BF16)
            o_ref[:, lo + QK_NOPE:lo + HEAD_PAD] = _rope(q_ref[:, lo + QK_NOPE:lo + HEAD_PAD], cs, sa, sb).astype(BF16)

    return pl.pallas_call(
        body, name="q_rope", grid=(s // ts,),
        in_specs=[_rows(ts, w)] + [_rows(ts, LANES)] * 3, out_specs=_rows(ts, w),
        out_shape=jax.ShapeDtypeStruct((s, w), BF16),
        compiler_params=_params(("parallel",)),
    )(qpre, *tabs)


def _conv_fwd(u0, w, b):
    s, c = u0.shape
    tc = LANES
    rc = _pick(s, (256, 128))

    def body(u_ref, w_ref, b_ref, o_ref, pad_ref):
        pad_ref[0:CONV_PAD, :] = jnp.zeros((CONV_PAD, tc), F32)
        pad_ref[CONV_PAD:CONV_PAD + s, :] = u_ref[...]
        for r in range(s // rc):
            acc = jnp.broadcast_to(b_ref[...], (rc, tc))
            for k in range(CONV_K):
                lo = r * rc + CONV_PAD - (CONV_K - 1) + k
                acc = acc + w_ref[k:k + 1, :] * pad_ref[lo:lo + rc, :]
            o_ref[r * rc:(r + 1) * rc, :] = acc

    col = lambda j: (0, j)
    return pl.pallas_call(
        body, name="conv_fwd", grid=(c // tc,),
        in_specs=[pl.BlockSpec((s, tc), col), pl.BlockSpec((CONV_K, tc), col), pl.BlockSpec((1, tc), col)],
        out_specs=pl.BlockSpec((s, tc), col),
        out_shape=jax.ShapeDtypeStruct((s, c), F32),
        scratch_shapes=[pltpu.VMEM((s + CONV_PAD, tc), F32)],
        compiler_params=_params(("parallel",)),
    )(u0, w, b)


def _conv_bwd(du1, u0, w):
    s, c = u0.shape
    tc = LANES
    rc = _pick(s, (256, 128))

    def body(d_ref, u_ref, w_ref, du_ref, dw_ref, db_ref, upad_ref, dpad_ref):
        upad_ref[0:CONV_PAD, :] = jnp.zeros((CONV_PAD, tc), F32)
        upad_ref[CONV_PAD:CONV_PAD + s, :] = u_ref[...]
        dpad_ref[0:s, :] = d_ref[...]
        dpad_ref[s:s + CONV_PAD, :] = jnp.zeros((CONV_PAD, tc), F32)
        for r in range(s // rc):
            acc = jnp.zeros((rc, tc), F32)
            for k in range(CONV_K):
                lo = r * rc + (CONV_K - 1) - k
                acc = acc + w_ref[k:k + 1, :] * dpad_ref[lo:lo + rc, :]
            du_ref[r * rc:(r + 1) * rc, :] = acc
        for k in range(CONV_K):
            acc8 = jnp.zeros((8, tc), F32)
            for r in range(s // rc):
                lo = r * rc + CONV_PAD - (CONV_K - 1) + k
                prod = d_ref[r * rc:(r + 1) * rc, :] * upad_ref[lo:lo + rc, :]
                acc8 = acc8 + jnp.sum(prod.reshape(rc // 8, 8, tc), axis=0)
            dw_ref[k:k + 1, :] = jnp.sum(acc8, axis=0, keepdims=True)
        db_ref[...] = jnp.sum(d_ref[...], axis=0, keepdims=True)

    col = lambda j: (0, j)
    return pl.pallas_call(
        body, name="conv_bwd", grid=(c // tc,),
        in_specs=[pl.BlockSpec((s, tc), col), pl.BlockSpec((s, tc), col), pl.BlockSpec((CONV_K, tc), col)],
        out_specs=[pl.BlockSpec((s, tc), col), pl.BlockSpec((CONV_K, tc), col), pl.BlockSpec((1, tc), col)],
        out_shape=[jax.ShapeDtypeStruct((s, c), F32), jax.ShapeDtypeStruct((CONV_K, c), F32),
                   jax.ShapeDtypeStruct((1, c), F32)],
        scratch_shapes=[pltpu.VMEM((s + CONV_PAD, tc), F32), pltpu.VMEM((s + CONV_PAD, tc), F32)],
        compiler_params=_params(("parallel",)),
    )(du1, u0, w)


def _causal_mask(sc, qi, kj, tq, tk):
    rows = qi * tq + lax.broadcasted_iota(jnp.int32, sc.shape, 0)
    cols = kj * tk + lax.broadcasted_iota(jnp.int32, sc.shape, 1)
    return jnp.where(cols <= rows, sc, NEG)


def _attn_fwd(q, kv, kpe):
    s = q.shape[0]
    tq = tk = _pick(s, (ATTN_BLOCK, 256, 128))
    reps = tk // LANES
    scale = QK_HEAD ** -0.5
    nt = (((1,), (1,)), ((), ()))

    def body(q_ref, kn_ref, v_ref, kpe_ref, o_ref, lse_ref, kf_ref, vb_ref, m_ref, l_ref, acc_ref):
        i = pl.program_id(1)

        @pl.when(i == 0)
        def _():
            kf_ref[:, 0:QK_NOPE] = kn_ref[...].astype(BF16)
            kf_ref[:, QK_NOPE:HEAD_PAD] = kpe_ref[...]
            vb_ref[...] = v_ref[...].astype(BF16)

        qb = q_ref[...]
        m_ref[...] = jnp.full((tq, LANES), NEG, F32)
        l_ref[...] = jnp.zeros((tq, LANES), F32)
        acc_ref[...] = jnp.zeros((tq, V_HEAD), F32)

        def block(j, diagonal):
            off = j * tk
            sc = lax.dot_general(qb, kf_ref[pl.ds(off, tk), :], nt, preferred_element_type=F32) * scale
            if diagonal:
                sc = _causal_mask(sc, 0, 0, tq, tk)
            m_prev = m_ref[...]
            m_new = jnp.maximum(m_prev, jnp.max(sc, axis=1, keepdims=True))
            p = jnp.exp(sc - jnp.tile(m_new, (1, reps)))
            alpha = jnp.exp(m_prev - m_new)
            l_ref[...] = alpha * l_ref[...] + jnp.sum(p, axis=1, keepdims=True)
            acc_ref[...] = alpha * acc_ref[...] + jnp.dot(p.astype(BF16), vb_ref[pl.ds(off, tk), :],
                                                          preferred_element_type=F32)
            m_ref[...] = m_new

        for qi in range(s // tq):
            @pl.when(i == qi)
            def _(qi=qi):
                for j in range(qi):
                    block(j, False)
                block(qi, True)

        o_ref[...] = acc_ref[...] / l_ref[...]
        lse_ref[...] = m_ref[...] + jnp.log(l_ref[...])

    return pl.pallas_call(
        body, name="attn_fwd", grid=(N_HEADS, s // tq),
        in_specs=[pl.BlockSpec((tq, HEAD_PAD), lambda h, i: (i, h)),
                  pl.BlockSpec((s, QK_NOPE), lambda h, i: (0, 2 * h)),
                  pl.BlockSpec((s, V_HEAD), lambda h, i: (0, 2 * h + 1)),
                  pl.BlockSpec((s, LANES), lambda h, i: (0, 0))],
        out_specs=[pl.BlockSpec((tq, V_HEAD), lambda h, i: (i, h)),
                   pl.BlockSpec((tq, LANES), lambda h, i: (i, h))],
        out_shape=[jax.ShapeDtypeStruct((s, N_HEADS * V_HEAD), F32),
                   jax.ShapeDtypeStruct((s, N_HEADS * LANES), F32)],
        scratch_shapes=[pltpu.VMEM((s, HEAD_PAD), BF16), pltpu.VMEM((s, V_HEAD), BF16),
                        pltpu.VMEM((tq, LANES), F32), pltpu.VMEM((tq, LANES), F32), pltpu.VMEM((tq, V_HEAD), F32)],
        compiler_params=_params(("parallel", "arbitrary")),
    )(q, kv, kv, kpe)


def _attn_bwd(q, kv, kpe, o, do, lse, tabs):
    s = q.shape[0]
    tq = tk = _pick(s, (ATTN_BLOCK, 256, 128))
    nq = s // tq
    reps = tk // LANES
    scale = QK_HEAD ** -0.5
    nt = (((1,), (1,)), ((), ()))
    tn = (((0,), (0,)), ((), ()))

    def body(q_ref, kn_ref, v_ref, kpe_ref, o_ref, do_ref, lse_ref, c_ref, sa_ref, sb_ref, dqpre_ref, dkv_ref,
             dkpe_ref, kf_ref, vb_ref, dk_ref, dv_ref, dq_ref):
        j = pl.program_id(1)

        @pl.when(j == 0)
        def _():
            dq_ref[...] = jnp.zeros_like(dq_ref)

        kf_ref[:, 0:QK_NOPE] = kn_ref[...].astype(BF16)
        kf_ref[:, QK_NOPE:HEAD_PAD] = kpe_ref[...]
        vb_ref[...] = v_ref[...].astype(BF16)
        dk_ref[...] = jnp.zeros_like(dk_ref)
        dv_ref[...] = jnp.zeros_like(dv_ref)

        def block(i, diagonal):
            off = i * tq
            qb = q_ref[pl.ds(off, tq), :]
            dob = do_ref[pl.ds(off, tq), :]
            delta = jnp.sum(dob * o_ref[pl.ds(off, tq), :], axis=1, keepdims=True)
            sc = lax.dot_general(qb, kf_ref[...], nt, preferred_element_type=F32) * scale
            if diagonal:
                sc = _causal_mask(sc, 0, 0, tq, tk)
            p = jnp.exp(sc - jnp.tile(lse_ref[pl.ds(off, tq), :], (1, reps)))
            dob16 = dob.astype(BF16)
            dv_ref[...] += lax.dot_general(p.astype(BF16), dob16, tn, preferred_element_type=F32)
            dp = lax.dot_general(dob16, vb_ref[...], nt, preferred_element_type=F32)
            ds = (p * (dp - delta) * scale).astype(BF16)
            dq_ref[pl.ds(off, tq), :] += jnp.dot(ds, kf_ref[...], preferred_element_type=F32)
            dk_ref[...] += lax.dot_general(ds, qb, tn, preferred_element_type=F32)

        for kj in range(nq):
            @pl.when(j == kj)
            def _(kj=kj):
                block(kj, True)
                for i in range(kj + 1, nq):
                    block(i, False)

        dkv_ref[:, 0:QK_NOPE] = dk_ref[:, 0:QK_NOPE]
        dkv_ref[:, QK_NOPE:HEAD_PAD] = dv_ref[...]
        dkpe_ref[...] = dk_ref[:, QK_NOPE:HEAD_PAD]

        @pl.when(j == nq - 1)
        def _():
            dqpre_ref[:, 0:QK_NOPE] = dq_ref[:, 0:QK_NOPE].astype(BF16)
            dqpre_ref[:, QK_NOPE:HEAD_PAD] = _rope_t(dq_ref[:, QK_NOPE:HEAD_PAD], c_ref[...], sa_ref[...],
                                                     sb_ref[...]).astype(BF16)

    head_rows = lambda w: pl.BlockSpec((s, w), lambda h, j: (0, h))
    table = pl.BlockSpec((s, LANES), lambda h, j: (0, 0))
    return pl.pallas_call(
        body, name="attn_bwd", grid=(N_HEADS, s // tk),
        in_specs=[head_rows(HEAD_PAD),
                  pl.BlockSpec((tk, QK_NOPE), lambda h, j: (j, 2 * h)),
                  pl.BlockSpec((tk, V_HEAD), lambda h, j: (j, 2 * h + 1)),
                  pl.BlockSpec((tk, LANES), lambda h, j: (j, 0)),
                  head_rows(V_HEAD), head_rows(V_HEAD), head_rows(LANES), table, table, table],
        out_specs=[head_rows(HEAD_PAD),
                   pl.BlockSpec((tk, HEAD_PAD), lambda h, j: (j, h)),
                   pl.BlockSpec((tk, LANES), lambda h, j: (j, h))],
        out_shape=[jax.ShapeDtypeStruct((s, N_HEADS * HEAD_PAD), BF16),
                   jax.ShapeDtypeStruct((s, N_HEADS * HEAD_PAD), F32),
                   jax.ShapeDtypeStruct((s, N_HEADS * LANES), F32)],
        scratch_shapes=[pltpu.VMEM((tk, HEAD_PAD), BF16), pltpu.VMEM((tk, V_HEAD), BF16),
                        pltpu.VMEM((tk, HEAD_PAD), F32), pltpu.VMEM((tk, V_HEAD), F32),
                        pltpu.VMEM((s, HEAD_PAD), F32)],
        compiler_params=_params(("parallel", "arbitrary")),
    )(q, kv, kv, kpe, o, do, lse, *tabs)


def _mix_fwd(u1, lng, lnb, gcon, attn, gattn):
    s, c = u1.shape
    ac = attn.shape[1]
    ts = _pick(s, (256, 128))

    def body(u_ref, lg_ref, lb_ref, gc_ref, a_ref, ga_ref, o_ref):
        t3 = _silu(_ln(u_ref[...], lg_ref[...], lb_ref[...]))
        o_ref[:, 0:c] = _rms(t3, gc_ref[...]).astype(BF16)
        o_ref[:, c:c + ac] = _rms(a_ref[...], ga_ref[...]).astype(BF16)

    return pl.pallas_call(
        body, name="mix_fwd", grid=(s // ts,),
        in_specs=[_rows(ts, c), _vec(c), _vec(c), _vec(c), _rows(ts, ac), _vec(ac)],
        out_specs=_rows(ts, c + ac),
        out_shape=jax.ShapeDtypeStruct((s, c + ac), BF16),
        compiler_params=_params(("parallel",)),
    )(u1, lng, lnb, gcon, attn, gattn)


def _mix_bwd(dmixin, u1, lng, lnb, gcon, attn, gattn):
    s, c = u1.shape
    ac = attn.shape[1]
    ts = _pick(s, (256, 128))

    def body(d_ref, u_ref, lg_ref, lb_ref, gc_ref, a_ref, ga_ref,
             du_ref, da_ref, dlg_ref, dlb_ref, dgc_ref, dga_ref):
        _acc_init(pl.program_id(0), dlg_ref, dlb_ref, dgc_ref, dga_ref)
        u = u_ref[...]
        t2 = _ln(u, lg_ref[...], lb_ref[...])
        dt3, dgc = _rms_bwd(_silu(t2), gc_ref[...], d_ref[:, 0:c])
        du, dlg, dlb = _ln_bwd(u, lg_ref[...], dt3 * _silu_grad(t2))
        du_ref[...] = du
        dlg_ref[...] += dlg
        dlb_ref[...] += dlb
        dgc_ref[...] += dgc
        da, dga = _rms_bwd(a_ref[...], ga_ref[...], d_ref[:, c:c + ac])
        da_ref[...] = da
        dga_ref[...] += dga

    return pl.pallas_call(
        body, name="mix_bwd", grid=(s // ts,),
        in_specs=[_rows(ts, c + ac), _rows(ts, c), _vec(c), _vec(c), _vec(c), _rows(ts, ac), _vec(ac)],
        out_specs=[_rows(ts, c), _rows(ts, ac), _vec(c), _vec(c), _vec(c), _vec(ac)],
        out_shape=[jax.ShapeDtypeStruct((s, c), F32), jax.ShapeDtypeStruct((s, ac), F32),
                   jax.ShapeDtypeStruct((1, c), F32), jax.ShapeDtypeStruct((1, c), F32),
                   jax.ShapeDtypeStruct((1, c), F32), jax.ShapeDtypeStruct((1, ac), F32)],
        compiler_params=_params(("arbitrary",)),
    )(dmixin, u1, lng, lnb, gcon, attn, gattn)


def _post_mix_fwd(x, mix, gpost, gpre):
    s, d = x.shape
    ts = _pick(s, (256, 128))

    def body(x_ref, m_ref, gp_ref, gf_ref, x1_ref, hf_ref):
        x1 = x_ref[...] + _rms(m_ref[...], gp_ref[...])
        x1_ref[...] = x1
        hf_ref[...] = _rms(x1, gf_ref[...]).astype(BF16)

    return pl.pallas_call(
        body, name="post_mix_fwd", grid=(s // ts,),
        in_specs=[_rows(ts, d), _rows(ts, d), _vec(d), _vec(d)],
        out_specs=[_rows(ts, d), _rows(ts, d)],
        out_shape=[jax.ShapeDtypeStruct((s, d), F32), jax.ShapeDtypeStruct((s, d), BF16)],
        compiler_params=_params(("parallel",)),
    )(x, mix, gpost, gpre)


def _post_mix_bwd(dy, dhf, x1, gpre, mix, gpost):
    s, d = x1.shape
    ts = _pick(s, (256, 128))

    def body(dy_ref, dh_ref, x1_ref, gf_ref, m_ref, gp_ref, dx1_ref, dm_ref, dgf_ref, dgp_ref):
        _acc_init(pl.program_id(0), dgf_ref, dgp_ref)
        dxa, dgf = _rms_bwd(x1_ref[...], gf_ref[...], dh_ref[...])
        dx1 = dy_ref[...] + dxa
        dx1_ref[...] = dx1
        dgf_ref[...] += dgf
        dm, dgp = _rms_bwd(m_ref[...], gp_ref[...], dx1)
        dm_ref[...] = dm.astype(BF16)
        dgp_ref[...] += dgp

    return pl.pallas_call(
        body, name="post_mix_bwd", grid=(s // ts,),
        in_specs=[_rows(ts, d), _rows(ts, d), _rows(ts, d), _vec(d), _rows(ts, d), _vec(d)],
        out_specs=[_rows(ts, d), _rows(ts, d), _vec(d), _vec(d)],
        out_shape=[jax.ShapeDtypeStruct((s, d), F32), jax.ShapeDtypeStruct((s, d), BF16),
                   jax.ShapeDtypeStruct((1, d), F32), jax.ShapeDtypeStruct((1, d), F32)],
        compiler_params=_params(("arbitrary",)),
    )(dy, dhf, x1, gpre, mix, gpost)


def _ffn_up(hf, wg, wu):
    s, d = hf.shape
    nsh, fs, _ = wg.shape
    tm = _pick(s, (1024, 512, 256, 128))
    nt = (((1,), (1,)), ((), ()))

    def body(h_ref, wg_ref, wu_ref, dg_ref, du_ref, a_ref):
        h = h_ref[...]
        g = lax.dot_general(h, wg_ref[...], nt, preferred_element_type=F32)
        u = lax.dot_general(h, wu_ref[...], nt, preferred_element_type=F32)
        sg = _sigmoid(g)
        silu = g * sg
        dg_ref[...] = u * (sg * (1.0 + g * (1.0 - sg)))
        du_ref[...] = silu
        a_ref[...] = (silu * u).astype(BF16)

    w_spec = pl.BlockSpec((None, fs, d), lambda i, j: (j, 0, 0))
    o_spec = pl.BlockSpec((None, tm, fs), lambda i, j: (j, i, 0))
    return pl.pallas_call(
        body, name="ffn_up", grid=(s // tm, nsh),
        in_specs=[pl.BlockSpec((tm, d), lambda i, j: (i, 0)), w_spec, w_spec],
        out_specs=[o_spec] * 3,
        out_shape=[jax.ShapeDtypeStruct((nsh, s, fs), F32)] * 2 + [jax.ShapeDtypeStruct((nsh, s, fs), BF16)],
        compiler_params=_params(("parallel", "parallel")),
    )(hf, wg, wu)


def _ffn_down(acts, ws, name):
    n = len(acts)
    nsh, s, fs = acts[0].shape
    d = ws[0].shape[2]
    tm = _pick(s, (1024, 512, 256, 128))
    tn = _pick(d, (256, 128))
    a_mode = pl.Buffered(1)

    def body(*refs):
        acc = None
        for a_ref, w_ref in zip(refs[:n], refs[n:2 * n]):
            for j in range(nsh):
                part = jnp.dot(a_ref[j], w_ref[j], preferred_element_type=F32)
                acc = part if acc is None else acc + part
        refs[-1][...] = acc

    return pl.pallas_call(
        body, name=name, grid=(s // tm, d // tn),
        in_specs=[pl.BlockSpec((nsh, tm, fs), lambda i, j: (0, i, 0), pipeline_mode=a_mode)] * n
        + [pl.BlockSpec((nsh, fs, tn), lambda i, j: (0, 0, j))] * n,
        out_specs=pl.BlockSpec((tm, tn), lambda i, j: (i, j)),
        out_shape=jax.ShapeDtypeStruct((s, d), F32),
        compiler_params=_params(("parallel", "parallel")),
    )(*acts, *ws)


def _ffn_down_bwd(dff, wd, act_dgate, act_dup, behind):
    s, d = dff.shape
    nsh, fs, _ = wd.shape
    tm = _pick(s, (1024, 512, 256, 128))
    nt = (((1,), (1,)), ((), ()))

    def body(d_ref, w_ref, pg_ref, pu_ref, _, dg_ref, du_ref):
        dact = lax.dot_general(d_ref[...], w_ref[...], nt, preferred_element_type=F32)
        dg_ref[...] = (dact * pg_ref[...]).astype(BF16)
        du_ref[...] = (dact * pu_ref[...]).astype(BF16)

    h_spec = pl.BlockSpec((None, tm, fs), lambda i, j: (j, i, 0))
    return pl.pallas_call(
        body, name="ffn_down_bwd", grid=(s // tm, nsh),
        in_specs=[pl.BlockSpec((tm, d), lambda i, j: (i, 0)),
                  pl.BlockSpec((None, fs, d), lambda i, j: (j, 0, 0)), h_spec, h_spec,
                  pl.BlockSpec((8, LANES), lambda i, j: (0, 0))],
        out_specs=[h_spec] * 2,
        out_shape=[jax.ShapeDtypeStruct((nsh, s, fs), BF16)] * 2,
        compiler_params=_params(("parallel", "parallel")),
    )(dff, wd, act_dgate, act_dup, behind)


def _ffn_dw(hiddens, other, name):
    n = len(hiddens)
    nsh, s, fs = hiddens[0].shape
    d = other.shape[1]
    tn = (((0,), (0,)), ((), ()))

    def body(*refs):
        for a_ref, o_ref in zip(refs[:n], refs[n + 1:]):
            o_ref[...] = lax.dot_general(a_ref[...], refs[n][...], tn, preferred_element_type=F32).astype(BF16)

    return pl.pallas_call(
        body, name=name, grid=(nsh,),
        in_specs=[pl.BlockSpec((None, s, fs), lambda j: (j, 0, 0))] * n + [pl.BlockSpec((s, d), lambda j: (0, 0))],
        out_specs=[pl.BlockSpec((None, fs, d), lambda j: (j, 0, 0))] * n,
        out_shape=[jax.ShapeDtypeStruct((nsh, fs, d), BF16)] * n,
        compiler_params=_params(("parallel",)),
    )(*hiddens, other)


def _ffn_down_final(act, wd, x1, tgt, g):
    nsh, s, fs = act.shape
    d = x1.shape[1]
    ts = _pick(s, (256, 128))

    def body(a_ref, w_ref, x1_ref, t_ref, g_ref, loss_ref, dy_ref, dff_ref, dg_ref):
        _acc_init(pl.program_id(0), loss_ref, dg_ref)
        ff_v = jnp.dot(a_ref[0], w_ref[0], preferred_element_type=F32)
        for j in range(1, nsh):
            ff_v = ff_v + jnp.dot(a_ref[j], w_ref[j], preferred_element_type=F32)
        err = x1_ref[...] + _rms(ff_v, g_ref[...]) - t_ref[...]
        tok = jnp.mean(err * err, axis=-1, keepdims=True)
        loss_ref[...] += 0.5 * jnp.sum(tok, axis=0, keepdims=True)
        dy = err * (1.0 / d)
        dy_ref[...] = dy
        dff, dg = _rms_bwd(ff_v, g_ref[...], dy)
        dff_ref[...] = dff.astype(BF16)
        dg_ref[...] += dg

    return pl.pallas_call(
        body, name="ffn_down_final", grid=(s // ts,),
        in_specs=[pl.BlockSpec((nsh, ts, fs), lambda i: (0, i, 0)),
                  pl.BlockSpec((nsh, fs, d), lambda i: (0, 0, 0), pipeline_mode=pl.Buffered(1)),
                  _rows(ts, d), _rows(ts, d), _vec(d)],
        out_specs=[_vec(LANES), _rows(ts, d), _rows(ts, d), _vec(d)],
        out_shape=[jax.ShapeDtypeStruct((1, LANES), F32), jax.ShapeDtypeStruct((s, d), F32),
                   jax.ShapeDtypeStruct((s, d), BF16), jax.ShapeDtypeStruct((1, d), F32)],
        compiler_params=_params(("arbitrary",)),
    )(act, wd, x1, tgt, g)


def _pre_bwd(dx1, dh, x, g):
    s, d = x.shape
    ts = _pick(s, (256, 128))

    def body(dx1_ref, dh_ref, x_ref, g_ref, dx_ref, dg_ref):
        _acc_init(pl.program_id(0), dg_ref)
        dxa, dg = _rms_bwd(x_ref[...], g_ref[...], dh_ref[...])
        dx_ref[...] = dx1_ref[...] + dxa
        dg_ref[...] += dg

    return pl.pallas_call(
        body, name="pre_bwd", grid=(s // ts,),
        in_specs=[_rows(ts, d), _rows(ts, d), _rows(ts, d), _vec(d)],
        out_specs=[_rows(ts, d), _vec(d)],
        out_shape=[jax.ShapeDtypeStruct((s, d), F32), jax.ShapeDtypeStruct((1, d), F32)],
        compiler_params=_params(("arbitrary",)),
    )(dx1, dh, x, g)


def _local_step(x, pos, tgt, vecs, in_weights_fn, mix_weights_fn, up_weights_fn, down_weights_fn, loss_fn,
                grads_fn):
    c = vecs["conv_b"].shape[1]
    ql = vecs["q_norm"].shape[1]
    kvl = vecs["kv_norm"].shape[1]
    half = jnp.arange(0, QK_ROPE, 2, dtype=F32)
    freq = ROPE_THETA ** (-half / QK_ROPE)
    inv_freq = jnp.concatenate([freq, freq, jnp.zeros((LANES - QK_ROPE,), F32)])[None, :]
    tabs = _rope_tables(pos, inv_freq)

    h = _pre_fwd(x, vecs["pre_mix_norm"])
    w_in_t, zero = in_weights_fn(h)
    z = _mm(h, w_in_t, "nt", "mm_z")
    w_uq_t, w_ukv, conv_w, w_out = mix_weights_fn(z)
    u0, qn, kvn, kpe = _split_fwd(z, vecs["q_norm"] + zero, vecs["kv_norm"], tabs, c, ql, kvl)
    u1 = _conv_fwd(u0, conv_w, vecs["conv_b"])
    q = _q_rope(_mm(qn, w_uq_t, "nt", "mm_q"), tabs)
    kv = _mm(kvn, w_ukv, "nn", "mm_kv")
    attn, lse = _attn_fwd(q, kv, kpe)
    mixin = _mix_fwd(u1, vecs["conv_ln_g"], vecs["conv_ln_b"], vecs["conv_out_norm"], attn, vecs["attn_out_norm"])
    mix = _mm(mixin, w_out, "nn", "mm_mix")
    x1, hf = _post_mix_fwd(x, mix, vecs["post_mix_norm"], vecs["pre_ffn_norm"])
    w_gate, w_up = up_weights_fn(mix)
    act_dgate, act_dup, act = _ffn_up(hf, w_gate, w_up)
    w_down = down_weights_fn(act)
    loss, dy, dff, d_post_ffn = _ffn_down_final(act, w_down, x1, tgt, vecs["post_ffn_norm"])

    g = {"post_ffn_norm": d_post_ffn}
    zero = loss_fn(loss) + grads_fn("down", ("w_down",), _ffn_dw([act], dff, "ffn_dw_down"), dff)
    dgate, dup = _ffn_down_bwd(dff, w_down, act_dgate, act_dup, jnp.zeros((8, LANES), F32) + zero)
    zero = zero + grads_fn("up", ("w_gate", "w_up"), _ffn_dw([dgate, dup], hf, "ffn_dw_up"), dgate)
    dhf = _ffn_down([dgate, dup], [w_gate, w_up], "ffn_dhf")
    dx1, dmix, g["pre_ffn_norm"], g["post_mix_norm"] = _post_mix_bwd(
        dy, dhf, x1, vecs["pre_ffn_norm"] + zero, mix, vecs["post_mix_norm"])
    dmixin = _mm(dmix, w_out, "nt", "mm_dmixin")
    dw_out = _mm(mixin, dmix, "tn", "mm_dw_out", BF16)
    zero = grads_fn("out", ("w_out",), [dw_out], dmix)
    du1, dattn, g["conv_ln_g"], g["conv_ln_b"], g["conv_out_norm"], g["attn_out_norm"] = _mix_bwd(
        dmixin, u1, vecs["conv_ln_g"] + zero, vecs["conv_ln_b"], vecs["conv_out_norm"], attn, vecs["attn_out_norm"])
    du0, dw_conv, g["conv_b"] = _conv_bwd(du1, u0, conv_w)
    dqpre, dkv, dkpe_h = _attn_bwd(q, kv, kpe, attn, dattn, lse, tabs)
    dqn = _mm(dqpre, w_uq_t, "nn", "mm_dqn")
    dw_uq = _mm(dqpre, qn, "tn", "mm_dw_uq", BF16)
    dkvn = _mm(dkv, w_ukv, "nt", "mm_dkvn")
    dw_ukv = _mm(kvn, dkv, "tn", "mm_dw_ukv", BF16)
    zero = grads_fn("", (), [], dkvn)
    dz, g["q_norm"], g["kv_norm"] = _split_bwd(du0, z, dqn, dkvn, dkpe_h, vecs["q_norm"] + zero, vecs["kv_norm"], tabs,
                                               c, ql, kvl)
    dw_in = _mm(dz, h, "tn", "mm_dw_in", BF16)
    zero = grads_fn("in", ("w_in", "w_uq", "w_ukv", "conv_w"), [dw_in, dw_uq, dw_ukv, dw_conv], dz)
    dh = _mm(dz, w_in_t, "nn", "mm_dh")
    zero = zero + grads_fn("", (), [], dh)
    grad_x, g["pre_mix_norm"] = _pre_bwd(dx1, dh, x, vecs["pre_mix_norm"] + zero)
    return loss, grad_x, g


def _my_index():
    return 4 * lax.axis_index("x") + 2 * lax.axis_index("y") + lax.axis_index("c")


def _coords(idx):
    return ((idx >> 2) & 1, (idx >> 1) & 1, idx & 1)


def _place():
    x, y, c = lax.axis_index("x"), lax.axis_index("y"), lax.axis_index("c")
    return (x, y, c), (x, y, 1 - c), [(1 - x, y), (x, 1 - y), (1 - x, 1 - y)]


def _small_copies(src, land, send_sem, recv_sem):
    me = _my_index()
    return [pltpu.make_async_remote_copy(src_ref=src, dst_ref=land.at[me], send_sem=send_sem, recv_sem=recv_sem,
                                         device_id=_coords(me ^ p), device_id_type=MESH)
            for p in range(1, N_DEV)]


HBM_SPEC = pl.BlockSpec(memory_space=pltpu.HBM)
SEM_SPEC = pl.BlockSpec(memory_space=pltpu.SEMAPHORE)
DATAFLOW = pltpu.SideEffectType.DATAFLOW_SIDE_EFFECTING


def _split_start(name, copies_of, srcs, lands, after):
    n = len(srcs)

    def body(*refs):
        outs = refs[2 * n + 1:]
        for k in range(n):
            for cp in copies_of(refs[k], refs[n + k], outs[k], outs[n + k]):
                cp.start()
        outs[-1][...] = jnp.zeros_like(outs[-1])

    hbm = lambda a: pltpu.HBM(a.shape, a.dtype)
    out = pl.pallas_call(
        body, name=name,
        in_specs=[HBM_SPEC] * (2 * n + 1),
        out_specs=[SEM_SPEC] * (2 * n) + [HBM_SPEC] * (2 * n) + [pl.BlockSpec(memory_space=pltpu.VMEM)],
        out_shape=[pltpu.SemaphoreType.DMA(())] * (2 * n) + [hbm(a) for a in srcs] + [hbm(a) for a in lands]
        + [jax.ShapeDtypeStruct((8, LANES), F32)],
        input_output_aliases={k: 2 * n + k for k in range(2 * n)},
        compiler_params=pltpu.CompilerParams(has_side_effects=DATAFLOW),
    )(*[pltpu.with_memory_space_constraint(a, pltpu.HBM) for a in list(srcs) + list(lands) + [after]])
    return (out[:n], out[n:2 * n], out[2 * n:3 * n], out[3 * n:4 * n]), out[-1][0, 0]


def _split_wait(name, n_copies, started, after):
    send_sems, recv_sems, srcs, lands = started
    n = len(srcs)

    def body(*refs):
        for k in range(n):
            slots = refs[n + k].at[pl.ds(0, n_copies)]
            all_copies = pltpu.make_async_remote_copy(
                src_ref=slots, dst_ref=slots, send_sem=refs[2 * n + k], recv_sem=refs[3 * n + k],
                device_id=_place()[0], device_id_type=MESH)
            all_copies.wait_send()
            all_copies.wait_recv()

    hbm = lambda a: pltpu.HBM(a.shape, a.dtype)
    out = pl.pallas_call(
        body, name=name,
        in_specs=[HBM_SPEC] * (2 * n) + [SEM_SPEC] * (2 * n) + [HBM_SPEC],
        out_specs=[HBM_SPEC] * (2 * n),
        out_shape=[hbm(a) for a in srcs] + [hbm(a) for a in lands],
        input_output_aliases={k: k for k in range(2 * n)},
        compiler_params=pltpu.CompilerParams(has_side_effects=DATAFLOW),
    )(*srcs, *lands, *send_sems, *recv_sems, pltpu.with_memory_space_constraint(after, pltpu.HBM))
    return out[:n], out[n:]


def _slot(chip, core):
    return 4 * chip[0] + 2 * chip[1] + core


def _gather_copies(src, land, send_sem, recv_sem):
    (x, y, c), sib, chips = _place()
    return [pltpu.make_async_remote_copy(src_ref=src, dst_ref=land.at[_slot((x, y), c)], send_sem=send_sem,
                                         recv_sem=recv_sem, device_id=to, device_id_type=MESH)
            for to in [sib] + [(*chip, c) for chip in chips]]


def _gather_pass_on(lands, name):
    n = len(lands)

    def body(*refs):
        ins, outs = refs[:n], refs[n:2 * n]
        send_sems, recv_sems = refs[2 * n:]
        (x, y, c), sib, chips = _place()
        sends = []
        for k in range(n):
            for j, chip in enumerate(chips):
                sends.append(pltpu.make_async_remote_copy(
                    src_ref=ins[k].at[_slot(chip, c)], dst_ref=outs[k].at[_slot(chip, c)],
                    send_sem=send_sems.at[k, j], recv_sem=recv_sems.at[k, j], device_id=sib, device_id_type=MESH))
        for cp in sends:
            cp.start()
        for cp in sends:
            cp.wait_recv()
        for cp in sends:
            cp.wait_send()

    any_spec = pl.BlockSpec(memory_space=pl.ANY)
    return pl.pallas_call(
        body, name=name,
        in_specs=[any_spec] * n, out_specs=[any_spec] * n,
        out_shape=[jax.ShapeDtypeStruct(a.shape, a.dtype) for a in lands],
        input_output_aliases={k: k for k in range(n)},
        scratch_shapes=[pltpu.SemaphoreType.DMA((n, 3))] * 2,
        compiler_params=pltpu.CompilerParams(has_side_effects=True),
    )(*lands)


def _chip_copies(src, land, send_sem, recv_sem):
    (x, y, c), _, chips = _place()
    return [pltpu.make_async_remote_copy(src_ref=src.at[2 * chip[0] + chip[1]], dst_ref=land.at[j], send_sem=send_sem,
                                         recv_sem=recv_sem, device_id=(*chip, c), device_id_type=MESH)
            for j, chip in enumerate(chips)]


ROW_TILE_BYTES = 14 * 1024 * 1024


def _stream_tile(r, c, bytes_per_elem):
    if r * c * bytes_per_elem <= ROW_TILE_BYTES:
        return r, c
    rows = [t for t in range(16, r, 16) if r % t == 0 and t * c * bytes_per_elem <= ROW_TILE_BYTES]
    if rows:
        return max(rows), c
    cols = [t for t in range(LANES, c, LANES) if c % t == 0 and r * t * bytes_per_elem <= ROW_TILE_BYTES]
    return r, max(cols)


def _sibling_copies(src, land, send_sem, recv_sem):
    (x, y, c), sib, _ = _place()
    return [pltpu.make_async_remote_copy(src_ref=src.at[2 * q + 1 - c], dst_ref=land.at[q], send_sem=send_sem,
                                         recv_sem=recv_sem, device_id=sib, device_id_type=MESH)
            for q in range(4)]


def _pair_sum(core, blocks, theirs, name):
    q, r, c = theirs.shape
    tr, tc = _stream_tile(r, c, 3 * theirs.dtype.itemsize)

    def body(core_ref, a_ref, b_ref, o_ref):
        o_ref[...] = (a_ref[...].astype(F32) + b_ref[...].astype(F32)).astype(o_ref.dtype)

    blk = pl.BlockSpec((1, tr, tc), lambda i, j, k, core_ref: (i, j, k))
    mine = pl.BlockSpec((1, tr, tc), lambda i, j, k, core_ref: (2 * i + core_ref[0], j, k))
    return pl.pallas_call(
        body, name=name,
        grid_spec=pltpu.PrefetchScalarGridSpec(num_scalar_prefetch=1, grid=(q, r // tr, c // tc),
                                               in_specs=[mine, blk], out_specs=blk),
        out_shape=jax.ShapeDtypeStruct(theirs.shape, theirs.dtype),
        compiler_params=_params(("parallel", "parallel", "parallel")),
    )(core, blocks, theirs)


def _adamw_small(parts, ws, ms, vs):
    n = len(ws)
    c1 = 1.0 - ADAM_B1
    c2 = 1.0 - ADAM_B2
    bc1 = 1.0 - ADAM_B1 ** ADAM_STEP
    bc2 = 1.0 - ADAM_B2 ** ADAM_STEP

    def body(*refs):
        p_ref, outs = refs[0], refs[1 + 3 * n:]
        off = 0
        for k in range(n):
            w_ref, m_ref, v_ref = refs[1 + k], refs[1 + n + k], refs[1 + 2 * n + k]
            width = w_ref.shape[1]
            g = p_ref[0, :, off:off + width]
            for j in range(1, N_DEV):
                g = g + p_ref[j, :, off:off + width]
            nm = ADAM_B1 * m_ref[...] + c1 * g
            nv = ADAM_B2 * v_ref[...] + c2 * (g * g)
            outs[k][...] = g
            outs[n + k][...] = -ADAM_LR * ((nm / bc1) / (jnp.sqrt(nv / bc2) + ADAM_EPS) + ADAM_WD * w_ref[...])
            outs[2 * n + k][...] = nm
            outs[3 * n + k][...] = nv
            off += width

    out = pl.pallas_call(
        body, name="adamw_small",
        out_shape=[jax.ShapeDtypeStruct(a.shape, F32) for a in ws] * 4,
        compiler_params=pltpu.CompilerParams(vmem_limit_bytes=VMEM_LIMIT),
    )(parts, *ws, *ms, *vs)
    return out[:n], out[n:2 * n], out[2 * n:3 * n], out[3 * n:]


def _reduce_adamw(parts, w, m, v, name, own, own_slot):
    r, c = w.shape
    n_parts = parts.shape[0]
    tr, tc = _stream_tile(r, c, (n_parts + 1) * parts.dtype.itemsize + 7 * 4)
    c1 = 1.0 - ADAM_B1
    c2 = 1.0 - ADAM_B2
    bc1 = 1.0 - ADAM_B1 ** ADAM_STEP
    bc2 = 1.0 - ADAM_B2 ** ADAM_STEP

    def body(_, o_ref, p_ref, w_ref, m_ref, v_ref, g_ref, d_ref, nm_ref, nv_ref):
        g = o_ref[0].astype(F32)
        for j in range(n_parts):
            g = g + p_ref[j].astype(F32)
        nm = ADAM_B1 * m_ref[...] + c1 * g
        nv = ADAM_B2 * v_ref[...] + c2 * (g * g)
        g_ref[...] = g
        nm_ref[...] = nm
        nv_ref[...] = nv
        d_ref[...] = -ADAM_LR * ((nm / bc1) / (jnp.sqrt(nv / bc2) + ADAM_EPS) + ADAM_WD * w_ref[...])

    out = jax.ShapeDtypeStruct((r, c), F32)
    grid = (r // tr, c // tc)
    blk = pl.BlockSpec((tr, tc), lambda i, j, slot_ref: (i, j))
    return pl.pallas_call(
        body, name=name,
        grid_spec=pltpu.PrefetchScalarGridSpec(
            num_scalar_prefetch=1, grid=grid,
            in_specs=[pl.BlockSpec((1, tr, tc), lambda i, j, slot_ref: (slot_ref[0], i, j)),
                      pl.BlockSpec((n_parts, tr, tc), lambda i, j, slot_ref: (0, i, j)), blk, blk, blk],
            out_specs=[blk] * 4),
        out_shape=[out] * 4,
        compiler_params=_params(("parallel", "parallel")),
    )(own_slot, own, parts, w, m, v)


_MIX = ("w_in", "w_uq", "w_ukv", "conv_w", "w_out")
_FFN = ("w_gate", "w_up", "w_down")
_BIG = _MIX + _FFN
_TRANSPOSED = ("w_in", "w_uq", "w_gate", "w_up")
_SMALL = ("pre_mix_norm", "q_norm", "kv_norm", "conv_b", "conv_ln_g", "conv_ln_b", "conv_out_norm",
          "attn_out_norm", "post_mix_norm", "pre_ffn_norm", "post_ffn_norm")
_ORDER = ("pre_mix_norm", "w_in", "q_norm", "w_uq", "kv_norm", "w_ukv", "conv_w", "conv_b", "conv_ln_g",
          "conv_ln_b", "conv_out_norm", "attn_out_norm", "w_out", "post_mix_norm", "pre_ffn_norm", "w_gate",
          "w_up", "w_down", "post_ffn_norm")


def _cols_from_shards(g):
    return jnp.transpose(g, (1, 0, 2)).reshape(g.shape[1], N_DEV * g.shape[2])


def _cols_to_shards(w):
    k, n8 = w.shape
    return jnp.transpose(w.reshape(k, N_DEV, n8 // N_DEV), (1, 0, 2))


def _step(x, positions, loss_target, w, m, v):
    s, d = x.shape[1], x.shape[2]
    x2, tgt = x[0], loss_target[0]
    pos = positions.reshape(s, 1)
    vecs = {n: w[n] for n in _SMALL}
    core = lax.axis_index("c").astype(jnp.int32).reshape(1)
    my_chip = (2 * lax.axis_index("x") + lax.axis_index("y")).astype(jnp.int32).reshape(1)
    n_in_cols = N_DEV * w["w_in"].shape[2]
    gathers, scatters, to_sibling = {}, {}, []

    def shard(t, n):
        return t[n][0].T if n in _TRANSPOSED else t[n][0]

    def gather_start(names, tag, after, zero=0.0):
        srcs = [w[n][0] if n == "conv_w" else (shard(w, n) + zero).astype(BF16) for n in names]
        lands = [lax.empty((N_DEV,) + a.shape, a.dtype) for a in srcs]
        gathers[tag], zero = _split_start("gather_" + tag + "_start", _gather_copies, srcs, lands, after)
        return zero

    def gather_finish(names, tag, after):
        srcs, lands = _split_wait("gather_" + tag + "_wait", 4, gathers[tag], after)
        lands = _gather_pass_on(lands, "gather_" + tag + "_pass_on")
        me = _my_index()
        return {n: lax.dynamic_update_slice(g, a[None], (me,) + (0,) * a.ndim) for n, g, a in zip(names, lands, srcs)}

    def in_weights_fn(h):
        w_in_g = gather_finish(("w_in",), "in", h)["w_in"]
        return jnp.pad(w_in_g.reshape(-1, d), ((0, LANES - QK_ROPE), (0, 0))), 0.0

    def mix_weights_fn(z):
        gath = gather_finish(_MIX[1:], "mix", z)
        w_uq_t = jnp.pad(gath["w_uq"], ((0, 0), (0, HEAD_PAD - QK_HEAD), (0, 0))).reshape(N_HEADS * HEAD_PAD, -1)
        return (w_uq_t, _cols_from_shards(gath["w_ukv"]), _cols_from_shards(gath["conv_w"]),
                gath["w_out"].reshape(-1, d))

    def up_weights_fn(mix):
        gath = gather_finish(("w_gate", "w_up"), "up", mix)
        return gath["w_gate"], gath["w_up"]

    def down_weights_fn(act):
        return gather_finish(("w_down",), "down", act)["w_down"]

    def scatter_advance(after):
        if not to_sibling:
            return 0.0
        names, tag, sent = to_sibling.pop()
        blocks, theirs = _split_wait("to_sibling_" + tag + "_wait", 4, sent, after)
        pairs = [_pair_sum(core, b, t, "pair_sum_" + n) for n, b, t in zip(names, blocks, theirs)]
        lands = [lax.empty((3,) + p.shape[1:], p.dtype) for p in pairs]
        scatters[tag], zero = _split_start("scatter_" + tag + "_start", _chip_copies, pairs, lands, theirs[0])
        return zero

    to_blocks = {
        "w_in": lambda a: a[:n_in_cols].reshape(N_DEV, -1, d),
        "w_uq": lambda a: a.reshape(N_HEADS, HEAD_PAD, -1)[:, :QK_HEAD],
        "w_ukv": _cols_to_shards, "conv_w": _cols_to_shards,
        "w_out": lambda a: a.reshape(N_DEV, -1, d),
    }

    def grads_fn(tag, names, grads, after):
        zero = scatter_advance(after)
        if not grads:
            return zero
        blocks = [to_blocks.get(n, lambda a: a)(a) for n, a in zip(names, grads)]
        lands = [lax.empty((4,) + b.shape[1:], b.dtype) for b in blocks]
        sent, zero2 = _split_start("to_sibling_" + tag + "_start", _sibling_copies, blocks, lands, after)
        to_sibling.append((names, tag, sent))
        return zero + zero2

    zero = gather_start(("w_in",), "in", x2)
    zero = gather_start(_MIX[1:], "mix", x2, zero)
    zero = gather_start(("w_gate", "w_up"), "up", x2, zero)
    vecs["pre_mix_norm"] = vecs["pre_mix_norm"] + gather_start(("w_down",), "down", x2, zero)
    totals = []

    def loss_fn(loss):
        totals.append(lax.psum(loss[0, 0], ("x", "y", "c")))
        return jnp.minimum(jnp.abs(totals[0]), 0.0)

    _, grad_x, g = _local_step(x2, pos, tgt, vecs, in_weights_fn, mix_weights_fn, up_weights_fn, down_weights_fn,
                               loss_fn, grads_fn)
    total = totals[0]

    small = jnp.concatenate([g[n] for n in _SMALL], axis=1)
    small_started, zero = _split_start("gather_small_start", _small_copies, [small],
                                       [lax.empty((N_DEV,) + small.shape, F32)], grad_x)

    res = {}
    after = grad_x
    my_chip = my_chip + zero.astype(jnp.int32)
    for tag, names in (("down", ("w_down",)), ("up", ("w_gate", "w_up")), ("out", ("w_out",)),
                       ("in", ("w_in", "w_uq", "w_ukv", "conv_w"))):
        pairs, recv = _split_wait("scatter_" + tag + "_wait", 3, scatters[tag], after)
        for n, own, parts in zip(names, pairs, recv):
            res[n] = _reduce_adamw(parts, shard(w, n), shard(m, n), shard(v, n), "adamw_" + n, own=own,
                                   own_slot=my_chip)
            after = res[n][1]
            res[n] = [(t.T if n in _TRANSPOSED else t)[None] for t in res[n]]
    (small,), (small_all,) = _split_wait("gather_small_wait", N_DEV - 1, small_started, after)
    small_all = lax.dynamic_update_slice(small_all, small[None], (_my_index(), 0, 0))
    small_res = _adamw_small(small_all, *[[t[n] for n in _SMALL] for t in (w, m, v)])
    for k, n in enumerate(_SMALL):
        res[n] = [part[k] for part in small_res]

    outs = [total, grad_x[None]]
    for part in range(4):
        outs.extend(res[n][part] for n in _ORDER)
    return tuple(outs)


def kernel(x, positions, pre_mix_norm, w_in, q_norm, w_uq, kv_norm, w_ukv, conv_w, conv_b, conv_ln_g, conv_ln_b, conv_out_norm, attn_out_norm, w_out, post_mix_norm, pre_ffn_norm, w_gate, w_up, w_down, post_ffn_norm, loss_target, m_pre_mix_norm, m_w_in, m_q_norm, m_w_uq, m_kv_norm, m_w_ukv, m_conv_w, m_conv_b, m_conv_ln_g, m_conv_ln_b, m_conv_out_norm, m_attn_out_norm, m_w_out, m_post_mix_norm, m_pre_ffn_norm, m_w_gate, m_w_up, m_w_down, m_post_ffn_norm, v_pre_mix_norm, v_w_in, v_q_norm, v_w_uq, v_kv_norm, v_w_ukv, v_conv_w, v_conv_b, v_conv_ln_g, v_conv_ln_b, v_conv_out_norm, v_attn_out_norm, v_w_out, v_post_mix_norm, v_pre_ffn_norm, v_w_gate, v_w_up, v_w_down, v_post_ffn_norm):
    w = dict(zip(_ORDER, (pre_mix_norm, w_in, q_norm, w_uq, kv_norm, w_ukv, conv_w, conv_b, conv_ln_g, conv_ln_b,
                          conv_out_norm, attn_out_norm, w_out, post_mix_norm, pre_ffn_norm, w_gate, w_up, w_down,
                          post_ffn_norm)))
    m = dict(zip(_ORDER, (m_pre_mix_norm, m_w_in, m_q_norm, m_w_uq, m_kv_norm, m_w_ukv, m_conv_w, m_conv_b,
                          m_conv_ln_g, m_conv_ln_b, m_conv_out_norm, m_attn_out_norm, m_w_out, m_post_mix_norm,
                          m_pre_ffn_norm, m_w_gate, m_w_up, m_w_down, m_post_ffn_norm)))
    v = dict(zip(_ORDER, (v_pre_mix_norm, v_w_in, v_q_norm, v_w_uq, v_kv_norm, v_w_ukv, v_conv_w, v_conv_b,
                          v_conv_ln_g, v_conv_ln_b, v_conv_out_norm, v_attn_out_norm, v_w_out, v_post_mix_norm,
                          v_pre_ffn_norm, v_w_gate, v_w_up, v_w_down, v_post_ffn_norm)))
    return _step(x, positions, loss_target, w, m, v)
```

```python
import functools

import jax
import jax.numpy as jnp
from jax import lax
from jax.experimental import pallas as pl
from jax.experimental.pallas import tpu as pltpu

N_DEV = 8
N_HEADS = 8
QK_NOPE = 128
QK_ROPE = 64
V_HEAD = 128
QK_HEAD = QK_NOPE + QK_ROPE
HEAD_PAD = 256
LANES = 128
ATTN_BLOCK = 512
CONV_K = 31
CONV_PAD = 32
EPS = 1e-6
ROPE_THETA = 10000.0
ADAM_LR = 0.001
ADAM_B1 = 0.9
ADAM_B2 = 0.999
ADAM_EPS = 1e-08
ADAM_WD = 0.01
ADAM_STEP = 10
VMEM_LIMIT = 56 * 1024 * 1024
F32 = jnp.float32
BF16 = jnp.bfloat16
MESH = pl.DeviceIdType.MESH
NEG = -1e30


def _pick(n, prefs):
    for p in prefs:
        if p <= n and n % p == 0:
            return p
    return n


def _params(sem):
    return pltpu.CompilerParams(dimension_semantics=sem, vmem_limit_bytes=VMEM_LIMIT)


_DIMS = {"nn": (((1,), (0,)), ((), ())), "nt": (((1,), (1,)), ((), ())), "tn": (((0,), (0,)), ((), ()))}


MM_VMEM_BUDGET = 40 * 1024 * 1024
MM_MAX_MACS = 3 * 1024 ** 3


V7X_HBM_BYTES_PER_S = 3.0e12
V7X_MXU_MACS_PER_S = 0.45e15
GRID_STEP_S = 0.35e-6


def _mm_tiles(m, n, k, size_a, size_b, size_o):
    best = None
    for tm in sorted({m, 1024, 512, 256, 128}, reverse=True):
        if tm > m or m % tm:
            continue
        for tn in sorted({n, 2048, 1024, 512, 384, 256, 128}, reverse=True):
            if tn > n or n % tn:
                continue
            vmem = 2 * (tm * k * size_a + k * tn * size_b + tm * tn * size_o)
            if vmem > MM_VMEM_BUDGET or tm * tn * k > MM_MAX_MACS:
                continue
            b_reads = 1 if tn == n else m // tm
            traffic = m * k * size_a + b_reads * k * n * size_b + m * n * size_o
            exposed = tm * k * size_a + k * tn * size_b + tm * tn * size_o
            steps = (m // tm) * (n // tn)
            key = (max(traffic / V7X_HBM_BYTES_PER_S, m * n * k / V7X_MXU_MACS_PER_S)
                   + exposed / V7X_HBM_BYTES_PER_S + steps * GRID_STEP_S)
            if best is None or key < best[0]:
                best = (key, tm, tn)
    assert best is not None, (m, n, k)
    return best[1], best[2]


def _mm(a, b, mode, name, out_dtype=F32, behind=None):
    if mode == "nn":
        (m, k), (k2, n) = a.shape, b.shape
    elif mode == "nt":
        (m, k), (n, k2) = a.shape, b.shape
    else:
        (k, m), (k2, n) = a.shape, b.shape
    assert k == k2, (a.shape, b.shape, mode)
    tm, tn = _mm_tiles(m, n, k, a.dtype.itemsize, b.dtype.itemsize, jnp.dtype(out_dtype).itemsize)
    dims = _DIMS[mode]

    def body(a_ref, b_ref, *rest):
        acc = lax.dot_general(a_ref[...].astype(BF16), b_ref[...].astype(BF16), dims, preferred_element_type=F32)
        rest[-1][...] = acc.astype(rest[-1].dtype)

    if mode == "tn":
        a_spec = pl.BlockSpec((k, tm), lambda i, j: (0, i))
    else:
        a_spec = pl.BlockSpec((tm, k), lambda i, j: (i, 0))
    if mode == "nt":
        b_spec = pl.BlockSpec((tn, k), lambda i, j: (j, 0))
    else:
        b_spec = pl.BlockSpec((k, tn), lambda i, j: (0, j))
    o_spec = pl.BlockSpec((tm, tn), lambda i, j: (i, j))
    extra = [] if behind is None else [behind]
    return pl.pallas_call(
        body, name=name,
        grid=(m // tm, n // tn),
        in_specs=[a_spec, b_spec] + [pl.BlockSpec((8, LANES), lambda i, j: (0, 0))] * len(extra),
        out_specs=o_spec,
        out_shape=jax.ShapeDtypeStruct((m, n), out_dtype),
        compiler_params=_params(("parallel", "parallel")),
    )(a, b, *extra)


def _sigmoid(x):
    return 1.0 / (1.0 + jnp.exp(-x))


def _rms(x, g):
    r = lax.rsqrt(jnp.mean(x * x, axis=-1, keepdims=True) + EPS)
    return (x * r) * g


def _rms_bwd(x, g, dy):
    r = lax.rsqrt(jnp.mean(x * x, axis=-1, keepdims=True) + EPS)
    xh = x * r
    dyg = dy * g
    dx = r * (dyg - xh * jnp.mean(dyg * xh, axis=-1, keepdims=True))
    return dx, jnp.sum(dy * xh, axis=0, keepdims=True)


def _ln(x, g, b):
    mu = jnp.mean(x, axis=-1, keepdims=True)
    xc = x - mu
    rs = lax.rsqrt(jnp.mean(xc * xc, axis=-1, keepdims=True) + EPS)
    return (xc * rs) * g + b


def _ln_bwd(x, g, dy):
    mu = jnp.mean(x, axis=-1, keepdims=True)
    xc = x - mu
    rs = lax.rsqrt(jnp.mean(xc * xc, axis=-1, keepdims=True) + EPS)
    xh = xc * rs
    dyg = dy * g
    dx = rs * (dyg - jnp.mean(dyg, axis=-1, keepdims=True) - xh * jnp.mean(dyg * xh, axis=-1, keepdims=True))
    return dx, jnp.sum(dy * xh, axis=0, keepdims=True), jnp.sum(dy, axis=0, keepdims=True)


def _silu(x):
    return x * _sigmoid(x)


def _silu_grad(x):
    s = _sigmoid(x)
    return s * (1.0 + x * (1.0 - s))


def _rope(x, cos, sa, sb):
    return x * cos + pltpu.roll(x, 96, 1) * sa + pltpu.roll(x, 32, 1) * sb


def _rope_t(d, cos, sa, sb):
    return d * cos - pltpu.roll(d, 96, 1) * sa - pltpu.roll(d, 32, 1) * sb


def _rows(ts, w):
    return pl.BlockSpec((ts, w), lambda i: (i, 0))


def _vec(w):
    return pl.BlockSpec((1, w), lambda i: (0, 0))


def _acc_init(i, *refs):
    @pl.when(i == 0)
    def _():
        for r in refs:
            r[...] = jnp.zeros_like(r)


def _rope_tables(pos, inv_freq):
    s = pos.shape[0]
    ts = _pick(s, (512, 256, 128))

    def body(p_ref, f_ref, c_ref, sa_ref, sb_ref):
        ang = p_ref[...].astype(F32) * f_ref[...]
        lane = lax.broadcasted_iota(jnp.int32, ang.shape, 1)
        c, sn = jnp.cos(ang), jnp.sin(ang)
        c_ref[...] = jnp.where(lane < QK_ROPE, c, 0.0)
        sa_ref[...] = jnp.where(lane < QK_ROPE // 2, -sn, 0.0)
        sb_ref[...] = jnp.where((lane >= QK_ROPE // 2) & (lane < QK_ROPE), sn, 0.0)

    out = jax.ShapeDtypeStruct((s, LANES), F32)
    return pl.pallas_call(
        body, name="rope_tables", grid=(s // ts,),
        in_specs=[_rows(ts, 1), _vec(LANES)],
        out_specs=[_rows(ts, LANES)] * 3, out_shape=[out] * 3,
        compiler_params=_params(("parallel",)),
    )(pos, inv_freq)


def _pre_fwd(x, g):
    s, d = x.shape
    ts = _pick(s, (256, 128))

    def body(x_ref, g_ref, h_ref):
        h_ref[...] = _rms(x_ref[...], g_ref[...]).astype(BF16)

    return pl.pallas_call(
        body, name="pre_fwd", grid=(s // ts,),
        in_specs=[_rows(ts, d), _vec(d)], out_specs=_rows(ts, d),
        out_shape=jax.ShapeDtypeStruct((s, d), BF16),
        compiler_params=_params(("parallel",)),
    )(x, g)


def _split_fwd(z, gq, gkv, tabs, c, ql, kvl):
    s, zw = z.shape
    ts = _pick(s, (256, 128))
    o_q, o_kv, o_kr = 2 * c, 2 * c + ql, 2 * c + ql + kvl

    def body(z_ref, gq_ref, gkv_ref, c_ref, sa_ref, sb_ref, u0_ref, qn_ref, kvn_ref, kpe_ref):
        u0_ref[...] = z_ref[:, 0:c] * _sigmoid(z_ref[:, c:2 * c])
        qn_ref[...] = _rms(z_ref[:, o_q:o_kv], gq_ref[...]).astype(BF16)
        kvn_ref[...] = _rms(z_ref[:, o_kv:o_kr], gkv_ref[...]).astype(BF16)
        kpe_ref[...] = _rope(z_ref[:, o_kr:o_kr + LANES], c_ref[...], sa_ref[...], sb_ref[...]).astype(BF16)

    return pl.pallas_call(
        body, name="split_fwd", grid=(s // ts,),
        in_specs=[_rows(ts, zw), _vec(ql), _vec(kvl)] + [_rows(ts, LANES)] * 3,
        out_specs=[_rows(ts, c), _rows(ts, ql), _rows(ts, kvl), _rows(ts, LANES)],
        out_shape=[jax.ShapeDtypeStruct((s, c), F32), jax.ShapeDtypeStruct((s, ql), BF16),
                   jax.ShapeDtypeStruct((s, kvl), BF16), jax.ShapeDtypeStruct((s, LANES), BF16)],
        compiler_params=_params(("parallel",)),
    )(z, gq, gkv, *tabs)


def _split_bwd(du0, z, dqn, dkvn, dkpe_h, gq, gkv, tabs, c, ql, kvl):
    s, zw = z.shape
    ts = _pick(s, (256, 128))
    o_q, o_kv, o_kr = 2 * c, 2 * c + ql, 2 * c + ql + kvl

    def body(du0_ref, z_ref, dqn_ref, dkvn_ref, dkh_ref, gq_ref, gkv_ref, c_ref, sa_ref, sb_ref,
             dz_ref, dgq_ref, dgkv_ref):
        _acc_init(pl.program_id(0), dgq_ref, dgkv_ref)
        du0 = du0_ref[...]
        a = z_ref[:, 0:c]
        sg = _sigmoid(z_ref[:, c:2 * c])
        dz_ref[:, 0:c] = (du0 * sg).astype(BF16)
        dz_ref[:, c:2 * c] = (du0 * a * sg * (1.0 - sg)).astype(BF16)
        dq, dgq = _rms_bwd(z_ref[:, o_q:o_kv], gq_ref[...], dqn_ref[...])
        dz_ref[:, o_q:o_kv] = dq.astype(BF16)
        dgq_ref[...] += dgq
        dkv, dgkv = _rms_bwd(z_ref[:, o_kv:o_kr], gkv_ref[...], dkvn_ref[...])
        dz_ref[:, o_kv:o_kr] = dkv.astype(BF16)
        dgkv_ref[...] += dgkv
        dk = dkh_ref[:, 0:LANES]
        for h in range(1, N_HEADS):
            dk = dk + dkh_ref[:, h * LANES:(h + 1) * LANES]
        dz_ref[:, o_kr:o_kr + LANES] = _rope_t(dk, c_ref[...], sa_ref[...], sb_ref[...]).astype(BF16)

    return pl.pallas_call(
        body, name="split_bwd", grid=(s // ts,),
        in_specs=[_rows(ts, c), _rows(ts, zw), _rows(ts, ql), _rows(ts, kvl), _rows(ts, N_HEADS * LANES),
                  _vec(ql), _vec(kvl)] + [_rows(ts, LANES)] * 3,
        out_specs=[_rows(ts, zw), _vec(ql), _vec(kvl)],
        out_shape=[jax.ShapeDtypeStruct((s, zw), BF16), jax.ShapeDtypeStruct((1, ql), F32),
                   jax.ShapeDtypeStruct((1, kvl), F32)],
        compiler_params=_params(("arbitrary",)),
    )(du0, z, dqn, dkvn, dkpe_h, gq, gkv, *tabs)


def _q_rope(qpre, tabs):
    s, w = qpre.shape
    ts = _pick(s, (256, 128))

    def body(q_ref, c_ref, sa_ref, sb_ref, o_ref):
        cs, sa, sb = c_ref[...], sa_ref[...], sb_ref[...]
        for h in range(N_HEADS):
            lo = h * HEAD_PAD
            o_ref[:, lo:lo + QK_NOPE] = q_ref[:, lo:lo + QK_NOPE].astype(BF16)
            o_ref[:, lo + QK_NOPE:lo + HEAD_PAD] = _rope(q_ref[:, lo + QK_NOPE:lo + HEAD_PAD], cs, sa, sb).astype(BF16)

    return pl.pallas_call(
        body, name="q_rope", grid=(s // ts,),
        in_specs=[_rows(ts, w)] + [_rows(ts, LANES)] * 3, out_specs=_rows(ts, w),
        out_shape=jax.ShapeDtypeStruct((s, w), BF16),
        compiler_params=_params(("parallel",)),
    )(qpre, *tabs)


def _conv_fwd(u0, w, b):
    s, c = u0.shape
    tc = LANES
    rc = _pick(s, (256, 128))

    def body(u_ref, w_ref, b_ref, o_ref, pad_ref):
        pad_ref[0:CONV_PAD, :] = jnp.zeros((CONV_PAD, tc), F32)
        pad_ref[CONV_PAD:CONV_PAD + s, :] = u_ref[...]
        for r in range(s // rc):
            acc = jnp.broadcast_to(b_ref[...], (rc, tc))
            for k in range(CONV_K):
                lo = r * rc + CONV_PAD - (CONV_K - 1) + k
                acc = acc + w_ref[k:k + 1, :] * pad_ref[lo:lo + rc, :]
            o_ref[r * rc:(r + 1) * rc, :] = acc

    col = lambda j: (0, j)
    return pl.pallas_call(
        body, name="conv_fwd", grid=(c // tc,),
        in_specs=[pl.BlockSpec((s, tc), col), pl.BlockSpec((CONV_K, tc), col), pl.BlockSpec((1, tc), col)],
        out_specs=pl.BlockSpec((s, tc), col),
        out_shape=jax.ShapeDtypeStruct((s, c), F32),
        scratch_shapes=[pltpu.VMEM((s + CONV_PAD, tc), F32)],
        compiler_params=_params(("parallel",)),
    )(u0, w, b)


def _conv_bwd(du1, u0, w):
    s, c = u0.shape
    tc = LANES
    rc = _pick(s, (256, 128))

    def body(d_ref, u_ref, w_ref, du_ref, dw_ref, db_ref, upad_ref, dpad_ref):
        upad_ref[0:CONV_PAD, :] = jnp.zeros((CONV_PAD, tc), F32)
        upad_ref[CONV_PAD:CONV_PAD + s, :] = u_ref[...]
        dpad_ref[0:s, :] = d_ref[...]
        dpad_ref[s:s + CONV_PAD, :] = jnp.zeros((CONV_PAD, tc), F32)
        for r in range(s // rc):
            acc = jnp.zeros((rc, tc), F32)
            for k in range(CONV_K):
                lo = r * rc + (CONV_K - 1) - k
                acc = acc + w_ref[k:k + 1, :] * dpad_ref[lo:lo + rc, :]
            du_ref[r * rc:(r + 1) * rc, :] = acc
        for k in range(CONV_K):
            acc8 = jnp.zeros((8, tc), F32)
            for r in range(s // rc):
                lo = r * rc + CONV_PAD - (CONV_K - 1) + k
                prod = d_ref[r * rc:(r + 1) * rc, :] * upad_ref[lo:lo + rc, :]
                acc8 = acc8 + jnp.sum(prod.reshape(rc // 8, 8, tc), axis=0)
            dw_ref[k:k + 1, :] = jnp.sum(acc8, axis=0, keepdims=True)
        db_ref[...] = jnp.sum(d_ref[...], axis=0, keepdims=True)

    col = lambda j: (0, j)
    return pl.pallas_call(
        body, name="conv_bwd", grid=(c // tc,),
        in_specs=[pl.BlockSpec((s, tc), col), pl.BlockSpec((s, tc), col), pl.BlockSpec((CONV_K, tc), col)],
        out_specs=[pl.BlockSpec((s, tc), col), pl.BlockSpec((CONV_K, tc), col), pl.BlockSpec((1, tc), col)],
        out_shape=[jax.ShapeDtypeStruct((s, c), F32), jax.ShapeDtypeStruct((CONV_K, c), F32),
                   jax.ShapeDtypeStruct((1, c), F32)],
        scratch_shapes=[pltpu.VMEM((s + CONV_PAD, tc), F32), pltpu.VMEM((s + CONV_PAD, tc), F32)],
        compiler_params=_params(("parallel",)),
    )(du1, u0, w)


def _causal_mask(sc, qi, kj, tq, tk):
    rows = qi * tq + lax.broadcasted_iota(jnp.int32, sc.shape, 0)
    cols = kj * tk + lax.broadcasted_iota(jnp.int32, sc.shape, 1)
    return jnp.where(cols <= rows, sc, NEG)


def _attn_fwd(q, kv, kpe):
    s = q.shape[0]
    tq = tk = _pick(s, (ATTN_BLOCK, 256, 128))
    reps = tk // LANES
    scale = QK_HEAD ** -0.5
    nt = (((1,), (1,)), ((), ()))

    def body(q_ref, kn_ref, v_ref, kpe_ref, o_ref, lse_ref, kf_ref, vb_ref, m_ref, l_ref, acc_ref):
        i = pl.program_id(1)

        @pl.when(i == 0)
        def _():
            kf_ref[:, 0:QK_NOPE] = kn_ref[...].astype(BF16)
            kf_ref[:, QK_NOPE:HEAD_PAD] = kpe_ref[...]
            vb_ref[...] = v_ref[...].astype(BF16)

        qb = q_ref[...]
        m_ref[...] = jnp.full((tq, LANES), NEG, F32)
        l_ref[...] = jnp.zeros((tq, LANES), F32)
        acc_ref[...] = jnp.zeros((tq, V_HEAD), F32)

        def block(j, diagonal):
            off = j * tk
            sc = lax.dot_general(qb, kf_ref[pl.ds(off, tk), :], nt, preferred_element_type=F32) * scale
            if diagonal:
                sc = _causal_mask(sc, 0, 0, tq, tk)
            m_prev = m_ref[...]
            m_new = jnp.maximum(m_prev, jnp.max(sc, axis=1, keepdims=True))
            p = jnp.exp(sc - jnp.tile(m_new, (1, reps)))
            alpha = jnp.exp(m_prev - m_new)
            l_ref[...] = alpha * l_ref[...] + jnp.sum(p, axis=1, keepdims=True)
            acc_ref[...] = alpha * acc_ref[...] + jnp.dot(p.astype(BF16), vb_ref[pl.ds(off, tk), :],
                                                          preferred_element_type=F32)
            m_ref[...] = m_new

        for qi in range(s // tq):
            @pl.when(i == qi)
            def _(qi=qi):
                for j in range(qi):
                    block(j, False)
                block(qi, True)

        o_ref[...] = acc_ref[...] / l_ref[...]
        lse_ref[...] = m_ref[...] + jnp.log(l_ref[...])

    return pl.pallas_call(
        body, name="attn_fwd", grid=(N_HEADS, s // tq),
        in_specs=[pl.BlockSpec((tq, HEAD_PAD), lambda h, i: (i, h)),
                  pl.BlockSpec((s, QK_NOPE), lambda h, i: (0, 2 * h)),
                  pl.BlockSpec((s, V_HEAD), lambda h, i: (0, 2 * h + 1)),
                  pl.BlockSpec((s, LANES), lambda h, i: (0, 0))],
        out_specs=[pl.BlockSpec((tq, V_HEAD), lambda h, i: (i, h)),
                   pl.BlockSpec((tq, LANES), lambda h, i: (i, h))],
        out_shape=[jax.ShapeDtypeStruct((s, N_HEADS * V_HEAD), F32),
                   jax.ShapeDtypeStruct((s, N_HEADS * LANES), F32)],
        scratch_shapes=[pltpu.VMEM((s, HEAD_PAD), BF16), pltpu.VMEM((s, V_HEAD), BF16),
                        pltpu.VMEM((tq, LANES), F32), pltpu.VMEM((tq, LANES), F32), pltpu.VMEM((tq, V_HEAD), F32)],
        compiler_params=_params(("parallel", "arbitrary")),
    )(q, kv, kv, kpe)


def _attn_bwd(q, kv, kpe, o, do, lse, tabs):
    s = q.shape[0]
    tq = tk = _pick(s, (ATTN_BLOCK, 256, 128))
    nq = s // tq
    reps = tk // LANES
    scale = QK_HEAD ** -0.5
    nt = (((1,), (1,)), ((), ()))
    tn = (((0,), (0,)), ((), ()))

    def body(q_ref, kn_ref, v_ref, kpe_ref, o_ref, do_ref, lse_ref, c_ref, sa_ref, sb_ref, dqpre_ref, dkv_ref,
             dkpe_ref, kf_ref, vb_ref, dk_ref, dv_ref, dq_ref):
        j = pl.program_id(1)

        @pl.when(j == 0)
        def _():
            dq_ref[...] = jnp.zeros_like(dq_ref)

        kf_ref[:, 0:QK_NOPE] = kn_ref[...].astype(BF16)
        kf_ref[:, QK_NOPE:HEAD_PAD] = kpe_ref[...]
        vb_ref[...] = v_ref[...].astype(BF16)
        dk_ref[...] = jnp.zeros_like(dk_ref)
        dv_ref[...] = jnp.zeros_like(dv_ref)

        def block(i, diagonal):
            off = i * tq
            qb = q_ref[pl.ds(off, tq), :]
            dob = do_ref[pl.ds(off, tq), :]
            delta = jnp.sum(dob * o_ref[pl.ds(off, tq), :], axis=1, keepdims=True)
            sc = lax.dot_general(qb, kf_ref[...], nt, preferred_element_type=F32) * scale
            if diagonal:
                sc = _causal_mask(sc, 0, 0, tq, tk)
            p = jnp.exp(sc - jnp.tile(lse_ref[pl.ds(off, tq), :], (1, reps)))
            dob16 = dob.astype(BF16)
            dv_ref[...] += lax.dot_general(p.astype(BF16), dob16, tn, preferred_element_type=F32)
            dp = lax.dot_general(dob16, vb_ref[...], nt, preferred_element_type=F32)
            ds = (p * (dp - delta) * scale).astype(BF16)
            dq_ref[pl.ds(off, tq), :] += jnp.dot(ds, kf_ref[...], preferred_element_type=F32)
            dk_ref[...] += lax.dot_general(ds, qb, tn, preferred_element_type=F32)

        for kj in range(nq):
            @pl.when(j == kj)
            def _(kj=kj):
                block(kj, True)
                for i in range(kj + 1, nq):
                    block(i, False)

        dkv_ref[:, 0:QK_NOPE] = dk_ref[:, 0:QK_NOPE]
        dkv_ref[:, QK_NOPE:HEAD_PAD] = dv_ref[...]
        dkpe_ref[...] = dk_ref[:, QK_NOPE:HEAD_PAD]

        @pl.when(j == nq - 1)
        def _():
            dqpre_ref[:, 0:QK_NOPE] = dq_ref[:, 0:QK_NOPE].astype(BF16)
            dqpre_ref[:, QK_NOPE:HEAD_PAD] = _rope_t(dq_ref[:, QK_NOPE:HEAD_PAD], c_ref[...], sa_ref[...],
                                                     sb_ref[...]).astype(BF16)

    head_rows = lambda w: pl.BlockSpec((s, w), lambda h, j: (0, h))
    table = pl.BlockSpec((s, LANES), lambda h, j: (0, 0))
    return pl.pallas_call(
        body, name="attn_bwd", grid=(N_HEADS, s // tk),
        in_specs=[head_rows(HEAD_PAD),
                  pl.BlockSpec((tk, QK_NOPE), lambda h, j: (j, 2 * h)),
                  pl.BlockSpec((tk, V_HEAD), lambda h, j: (j, 2 * h + 1)),
                  pl.BlockSpec((tk, LANES), lambda h, j: (j, 0)),
                  head_rows(V_HEAD), head_rows(V_HEAD), head_rows(LANES), table, table, table],
        out_specs=[head_rows(HEAD_PAD),
                   pl.BlockSpec((tk, HEAD_PAD), lambda h, j: (j, h)),
                   pl.BlockSpec((tk, LANES), lambda h, j: (j, h))],
        out_shape=[jax.ShapeDtypeStruct((s, N_HEADS * HEAD_PAD), BF16),
                   jax.ShapeDtypeStruct((s, N_HEADS * HEAD_PAD), F32),
                   jax.ShapeDtypeStruct((s, N_HEADS * LANES), F32)],
        scratch_shapes=[pltpu.VMEM((tk, HEAD_PAD), BF16), pltpu.VMEM((tk, V_HEAD), BF16),
                        pltpu.VMEM((tk, HEAD_PAD), F32), pltpu.VMEM((tk, V_HEAD), F32),
                        pltpu.VMEM((s, HEAD_PAD), F32)],
        compiler_params=_params(("parallel", "arbitrary")),
    )(q, kv, kv, kpe, o, do, lse, *tabs)


def _mix_fwd(u1, lng, lnb, gcon, attn, gattn):
    s, c = u1.shape
    ac = attn.shape[1]
    ts = _pick(s, (256, 128))

    def body(u_ref, lg_ref, lb_ref, gc_ref, a_ref, ga_ref, o_ref):
        t3 = _silu(_ln(u_ref[...], lg_ref[...], lb_ref[...]))
        o_ref[:, 0:c] = _rms(t3, gc_ref[...]).astype(BF16)
        o_ref[:, c:c + ac] = _rms(a_ref[...], ga_ref[...]).astype(BF16)

    return pl.pallas_call(
        body, name="mix_fwd", grid=(s // ts,),
        in_specs=[_rows(ts, c), _vec(c), _vec(c), _vec(c), _rows(ts, ac), _vec(ac)],
        out_specs=_rows(ts, c + ac),
        out_shape=jax.ShapeDtypeStruct((s, c + ac), BF16),
        compiler_params=_params(("parallel",)),
    )(u1, lng, lnb, gcon, attn, gattn)


def _mix_bwd(dmixin, u1, lng, lnb, gcon, attn, gattn):
    s, c = u1.shape
    ac = attn.shape[1]
    ts = _pick(s, (256, 128))

    def body(d_ref, u_ref, lg_ref, lb_ref, gc_ref, a_ref, ga_ref,
             du_ref, da_ref, dlg_ref, dlb_ref, dgc_ref, dga_ref):
        _acc_init(pl.program_id(0), dlg_ref, dlb_ref, dgc_ref, dga_ref)
        u = u_ref[...]
        t2 = _ln(u, lg_ref[...], lb_ref[...])
        dt3, dgc = _rms_bwd(_silu(t2), gc_ref[...], d_ref[:, 0:c])
        du, dlg, dlb = _ln_bwd(u, lg_ref[...], dt3 * _silu_grad(t2))
        du_ref[...] = du
        dlg_ref[...] += dlg
        dlb_ref[...] += dlb
        dgc_ref[...] += dgc
        da, dga = _rms_bwd(a_ref[...], ga_ref[...], d_ref[:, c:c + ac])
        da_ref[...] = da
        dga_ref[...] += dga

    return pl.pallas_call(
        body, name="mix_bwd", grid=(s // ts,),
        in_specs=[_rows(ts, c + ac), _rows(ts, c), _vec(c), _vec(c), _vec(c), _rows(ts, ac), _vec(ac)],
        out_specs=[_rows(ts, c), _rows(ts, ac), _vec(c), _vec(c), _vec(c), _vec(ac)],
        out_shape=[jax.ShapeDtypeStruct((s, c), F32), jax.ShapeDtypeStruct((s, ac), F32),
                   jax.ShapeDtypeStruct((1, c), F32), jax.ShapeDtypeStruct((1, c), F32),
                   jax.ShapeDtypeStruct((1, c), F32), jax.ShapeDtypeStruct((1, ac), F32)],
        compiler_params=_params(("arbitrary",)),
    )(dmixin, u1, lng, lnb, gcon, attn, gattn)


def _post_mix_fwd(x, mix, gpost, gpre):
    s, d = x.shape
    ts = _pick(s, (256, 128))

    def body(x_ref, m_ref, gp_ref, gf_ref, x1_ref, hf_ref):
        x1 = x_ref[...] + _rms(m_ref[...], gp_ref[...])
        x1_ref[...] = x1
        hf_ref[...] = _rms(x1, gf_ref[...]).astype(BF16)

    return pl.pallas_call(
        body, name="post_mix_fwd", grid=(s // ts,),
        in_specs=[_rows(ts, d), _rows(ts, d), _vec(d), _vec(d)],
        out_specs=[_rows(ts, d), _rows(ts, d)],
        out_shape=[jax.ShapeDtypeStruct((s, d), F32), jax.ShapeDtypeStruct((s, d), BF16)],
        compiler_params=_params(("parallel",)),
    )(x, mix, gpost, gpre)


def _post_mix_bwd(dy, dhf, x1, gpre, mix, gpost):
    s, d = x1.shape
    ts = _pick(s, (256, 128))

    def body(dy_ref, dh_ref, x1_ref, gf_ref, m_ref, gp_ref, dx1_ref, dm_ref, dgf_ref, dgp_ref):
        _acc_init(pl.program_id(0), dgf_ref, dgp_ref)
        dxa, dgf = _rms_bwd(x1_ref[...], gf_ref[...], dh_ref[...])
        dx1 = dy_ref[...] + dxa
        dx1_ref[...] = dx1
        dgf_ref[...] += dgf
        dm, dgp = _rms_bwd(m_ref[...], gp_ref[...], dx1)
        dm_ref[...] = dm.astype(BF16)
        dgp_ref[...] += dgp

    return pl.pallas_call(
        body, name="post_mix_bwd", grid=(s // ts,),
        in_specs=[_rows(ts, d), _rows(ts, d), _rows(ts, d), _vec(d), _rows(ts, d), _vec(d)],
        out_specs=[_rows(ts, d), _rows(ts, d), _vec(d), _vec(d)],
        out_shape=[jax.ShapeDtypeStruct((s, d), F32), jax.ShapeDtypeStruct((s, d), BF16),
                   jax.ShapeDtypeStruct((1, d), F32), jax.ShapeDtypeStruct((1, d), F32)],
        compiler_params=_params(("arbitrary",)),
    )(dy, dhf, x1, gpre, mix, gpost)


def _ffn_up(hf, wg, wu):
    s, d = hf.shape
    nsh, fs, _ = wg.shape
    tm = _pick(s, (1024, 512, 256, 128))
    nt = (((1,), (1,)), ((), ()))

    def body(h_ref, wg_ref, wu_ref, dg_ref, du_ref, a_ref):
        h = h_ref[...]
        g = lax.dot_general(h, wg_ref[...], nt, preferred_element_type=F32)
        u = lax.dot_general(h, wu_ref[...], nt, preferred_element_type=F32)
        sg = _sigmoid(g)
        silu = g * sg
        dg_ref[...] = u * (sg * (1.0 + g * (1.0 - sg)))
        du_ref[...] = silu
        a_ref[...] = (silu * u).astype(BF16)

    w_spec = pl.BlockSpec((None, fs, d), lambda i, j: (j, 0, 0))
    o_spec = pl.BlockSpec((None, tm, fs), lambda i, j: (j, i, 0))
    return pl.pallas_call(
        body, name="ffn_up", grid=(s // tm, nsh),
        in_specs=[pl.BlockSpec((tm, d), lambda i, j: (i, 0)), w_spec, w_spec],
        out_specs=[o_spec] * 3,
        out_shape=[jax.ShapeDtypeStruct((nsh, s, fs), F32)] * 2 + [jax.ShapeDtypeStruct((nsh, s, fs), BF16)],
        compiler_params=_params(("parallel", "parallel")),
    )(hf, wg, wu)


def _ffn_down(acts, ws, name):
    n = len(acts)
    nsh, s, fs = acts[0].shape
    d = ws[0].shape[2]
    tm = _pick(s, (1024, 512, 256, 128))
    tn = _pick(d, (256, 128))
    a_mode = pl.Buffered(1)

    def body(*refs):
        acc = None
        for a_ref, w_ref in zip(refs[:n], refs[n:2 * n]):
            for j in range(nsh):
                part = jnp.dot(a_ref[j], w_ref[j], preferred_element_type=F32)
                acc = part if acc is None else acc + part
        refs[-1][...] = acc

    return pl.pallas_call(
        body, name=name, grid=(s // tm, d // tn),
        in_specs=[pl.BlockSpec((nsh, tm, fs), lambda i, j: (0, i, 0), pipeline_mode=a_mode)] * n
        + [pl.BlockSpec((nsh, fs, tn), lambda i, j: (0, 0, j))] * n,
        out_specs=pl.BlockSpec((tm, tn), lambda i, j: (i, j)),
        out_shape=jax.ShapeDtypeStruct((s, d), F32),
        compiler_params=_params(("parallel", "parallel")),
    )(*acts, *ws)


def _ffn_down_bwd(dff, wd, act_dgate, act_dup, behind):
    s, d = dff.shape
    nsh, fs, _ = wd.shape
    tm = _pick(s, (1024, 512, 256, 128))
    nt = (((1,), (1,)), ((), ()))

    def body(d_ref, w_ref, pg_ref, pu_ref, _, dg_ref, du_ref):
        dact = lax.dot_general(d_ref[...], w_ref[...], nt, preferred_element_type=F32)
        dg_ref[...] = (dact * pg_ref[...]).astype(BF16)
        du_ref[...] = (dact * pu_ref[...]).astype(BF16)

    h_spec = pl.BlockSpec((None, tm, fs), lambda i, j: (j, i, 0))
    return pl.pallas_call(
        body, name="ffn_down_bwd", grid=(s // tm, nsh),
        in_specs=[pl.BlockSpec((tm, d), lambda i, j: (i, 0)),
                  pl.BlockSpec((None, fs, d), lambda i, j: (j, 0, 0)), h_spec, h_spec,
                  pl.BlockSpec((8, LANES), lambda i, j: (0, 0))],
        out_specs=[h_spec] * 2,
        out_shape=[jax.ShapeDtypeStruct((nsh, s, fs), BF16)] * 2,
        compiler_params=_params(("parallel", "parallel")),
    )(dff, wd, act_dgate, act_dup, behind)


def _ffn_dw(hiddens, other, name):
    n = len(hiddens)
    nsh, s, fs = hiddens[0].shape
    d = other.shape[1]
    tn = (((0,), (0,)), ((), ()))

    def body(*refs):
        for a_ref, o_ref in zip(refs[:n], refs[n + 1:]):
            o_ref[...] = lax.dot_general(a_ref[...], refs[n][...], tn, preferred_element_type=F32).astype(BF16)

    return pl.pallas_call(
        body, name=name, grid=(nsh,),
        in_specs=[pl.BlockSpec((None, s, fs), lambda j: (j, 0, 0))] * n + [pl.BlockSpec((s, d), lambda j: (0, 0))],
        out_specs=[pl.BlockSpec((None, fs, d), lambda j: (j, 0, 0))] * n,
        out_shape=[jax.ShapeDtypeStruct((nsh, fs, d), BF16)] * n,
        compiler_params=_params(("parallel",)),
    )(*hiddens, other)


def _ffn_down_final(act, wd, x1, tgt, g):
    nsh, s, fs = act.shape
    d = x1.shape[1]
    ts = _pick(s, (256, 128))

    def body(a_ref, w_ref, x1_ref, t_ref, g_ref, loss_ref, dy_ref, dff_ref, dg_ref):
        _acc_init(pl.program_id(0), loss_ref, dg_ref)
        ff_v = jnp.dot(a_ref[0], w_ref[0], preferred_element_type=F32)
        for j in range(1, nsh):
            ff_v = ff_v + jnp.dot(a_ref[j], w_ref[j], preferred_element_type=F32)
        err = x1_ref[...] + _rms(ff_v, g_ref[...]) - t_ref[...]
        tok = jnp.mean(err * err, axis=-1, keepdims=True)
        loss_ref[...] += 0.5 * jnp.sum(tok, axis=0, keepdims=True)
        dy = err * (1.0 / d)
        dy_ref[...] = dy
        dff, dg = _rms_bwd(ff_v, g_ref[...], dy)
        dff_ref[...] = dff.astype(BF16)
        dg_ref[...] += dg

    return pl.pallas_call(
        body, name="ffn_down_final", grid=(s // ts,),
        in_specs=[pl.BlockSpec((nsh, ts, fs), lambda i: (0, i, 0)),
                  pl.BlockSpec((nsh, fs, d), lambda i: (0, 0, 0), pipeline_mode=pl.Buffered(1)),
                  _rows(ts, d), _rows(ts, d), _vec(d)],
        out_specs=[_vec(LANES), _rows(ts, d), _rows(ts, d), _vec(d)],
        out_shape=[jax.ShapeDtypeStruct((1, LANES), F32), jax.ShapeDtypeStruct((s, d), F32),
                   jax.ShapeDtypeStruct((s, d), BF16), jax.ShapeDtypeStruct((1, d), F32)],
        compiler_params=_params(("arbitrary",)),
    )(act, wd, x1, tgt, g)


def _pre_bwd(dx1, dh, x, g):
    s, d = x.shape
    ts = _pick(s, (256, 128))

    def body(dx1_ref, dh_ref, x_ref, g_ref, dx_ref, dg_ref):
        _acc_init(pl.program_id(0), dg_ref)
        dxa, dg = _rms_bwd(x_ref[...], g_ref[...], dh_ref[...])
        dx_ref[...] = dx1_ref[...] + dxa
        dg_ref[...] += dg

    return pl.pallas_call(
        body, name="pre_bwd", grid=(s // ts,),
        in_specs=[_rows(ts, d), _rows(ts, d), _rows(ts, d), _vec(d)],
        out_specs=[_rows(ts, d), _vec(d)],
        out_shape=[jax.ShapeDtypeStruct((s, d), F32), jax.ShapeDtypeStruct((1, d), F32)],
        compiler_params=_params(("arbitrary",)),
    )(dx1, dh, x, g)


def _local_step(x, pos, tgt, vecs, in_weights_fn, mix_weights_fn, up_weights_fn, down_weights_fn, loss_fn,
                grads_fn):
    c = vecs["conv_b"].shape[1]
    ql = vecs["q_norm"].shape[1]
    kvl = vecs["kv_norm"].shape[1]
    half = jnp.arange(0, QK_ROPE, 2, dtype=F32)
    freq = ROPE_THETA ** (-half / QK_ROPE)
    inv_freq = jnp.concatenate([freq, freq, jnp.zeros((LANES - QK_ROPE,), F32)])[None, :]
    tabs = _rope_tables(pos, inv_freq)

    h = _pre_fwd(x, vecs["pre_mix_norm"])
    w_in_t, zero = in_weights_fn(h)
    z = _mm(h, w_in_t, "nt", "mm_z")
    w_uq_t, w_ukv, conv_w, w_out = mix_weights_fn(z)
    u0, qn, kvn, kpe = _split_fwd(z, vecs["q_norm"] + zero, vecs["kv_norm"], tabs, c, ql, kvl)
    u1 = _conv_fwd(u0, conv_w, vecs["conv_b"])
    q = _q_rope(_mm(qn, w_uq_t, "nt", "mm_q"), tabs)
    kv = _mm(kvn, w_ukv, "nn", "mm_kv")
    attn, lse = _attn_fwd(q, kv, kpe)
    mixin = _mix_fwd(u1, vecs["conv_ln_g"], vecs["conv_ln_b"], vecs["conv_out_norm"], attn, vecs["attn_out_norm"])
    mix = _mm(mixin, w_out, "nn", "mm_mix")
    x1, hf = _post_mix_fwd(x, mix, vecs["post_mix_norm"], vecs["pre_ffn_norm"])
    w_gate, w_up = up_weights_fn(mix)
    act_dgate, act_dup, act = _ffn_up(hf, w_gate, w_up)
    w_down = down_weights_fn(act)
    loss, dy, dff, d_post_ffn = _ffn_down_final(act, w_down, x1, tgt, vecs["post_ffn_norm"])

    g = {"post_ffn_norm": d_post_ffn}
    zero = loss_fn(loss) + grads_fn("down", ("w_down",), _ffn_dw([act], dff, "ffn_dw_down"), dff)
    dgate, dup = _ffn_down_bwd(dff, w_down, act_dgate, act_dup, jnp.zeros((8, LANES), F32) + zero)
    zero = zero + grads_fn("up", ("w_gate", "w_up"), _ffn_dw([dgate, dup], hf, "ffn_dw_up"), dgate)
    dhf = _ffn_down([dgate, dup], [w_gate, w_up], "ffn_dhf")
    dx1, dmix, g["pre_ffn_norm"], g["post_mix_norm"] = _post_mix_bwd(
        dy, dhf, x1, vecs["pre_ffn_norm"] + zero, mix, vecs["post_mix_norm"])
    dmixin = _mm(dmix, w_out, "nt", "mm_dmixin")
    dw_out = _mm(mixin, dmix, "tn", "mm_dw_out", BF16)
    zero = grads_fn("out", ("w_out",), [dw_out], dmix)
    du1, dattn, g["conv_ln_g"], g["conv_ln_b"], g["conv_out_norm"], g["attn_out_norm"] = _mix_bwd(
        dmixin, u1, vecs["conv_ln_g"] + zero, vecs["conv_ln_b"], vecs["conv_out_norm"], attn, vecs["attn_out_norm"])
    du0, dw_conv, g["conv_b"] = _conv_bwd(du1, u0, conv_w)
    dqpre, dkv, dkpe_h = _attn_bwd(q, kv, kpe, attn, dattn, lse, tabs)
    dqn = _mm(dqpre, w_uq_t, "nn", "mm_dqn")
    dw_uq = _mm(dqpre, qn, "tn", "mm_dw_uq", BF16)
    dkvn = _mm(dkv, w_ukv, "nt", "mm_dkvn")
    dw_ukv = _mm(kvn, dkv, "tn", "mm_dw_ukv", BF16)
    zero = grads_fn("", (), [], dkvn)
    dz, g["q_norm"], g["kv_norm"] = _split_bwd(du0, z, dqn, dkvn, dkpe_h, vecs["q_norm"] + zero, vecs["kv_norm"], tabs,
                                               c, ql, kvl)
    dw_in = _mm(dz, h, "tn", "mm_dw_in", BF16)
    zero = grads_fn("in", ("w_in", "w_uq", "w_ukv", "conv_w"), [dw_in, dw_uq, dw_ukv, dw_conv], dz)
    zero = zero + grads_fn("", (), [], dz)
    dh = _mm(dz, w_in_t, "nn", "mm_dh", behind=jnp.zeros((8, LANES), F32) + zero)
    grad_x, g["pre_mix_norm"] = _pre_bwd(dx1, dh, x, vecs["pre_mix_norm"] + zero)
    return loss, grad_x, g


def _my_index():
    return 4 * lax.axis_index("x") + 2 * lax.axis_index("y") + lax.axis_index("c")


def _coords(idx):
    return ((idx >> 2) & 1, (idx >> 1) & 1, idx & 1)


def _place():
    x, y, c = lax.axis_index("x"), lax.axis_index("y"), lax.axis_index("c")
    return (x, y, c), (x, y, 1 - c), [(1 - x, y), (x, 1 - y), (1 - x, 1 - y)]


def _small_copies(src, land, send_sem, recv_sem):
    me = _my_index()
    return [pltpu.make_async_remote_copy(src_ref=src, dst_ref=land.at[me], send_sem=send_sem, recv_sem=recv_sem,
                                         device_id=_coords(me ^ p), device_id_type=MESH)
            for p in range(1, N_DEV)]


HBM_SPEC = pl.BlockSpec(memory_space=pltpu.HBM)
SEM_SPEC = pl.BlockSpec(memory_space=pltpu.SEMAPHORE)
DATAFLOW = pltpu.SideEffectType.DATAFLOW_SIDE_EFFECTING


def _split_start(name, copies_of, srcs, lands, after):
    n = len(srcs)

    def body(*refs):
        outs = refs[2 * n + 1:]
        for k in range(n):
            for cp in copies_of(refs[k], refs[n + k], outs[k], outs[n + k]):
                cp.start()
        outs[-1][...] = jnp.zeros_like(outs[-1])

    hbm = lambda a: pltpu.HBM(a.shape, a.dtype)
    out = pl.pallas_call(
        body, name=name,
        in_specs=[HBM_SPEC] * (2 * n + 1),
        out_specs=[SEM_SPEC] * (2 * n) + [HBM_SPEC] * (2 * n) + [pl.BlockSpec(memory_space=pltpu.VMEM)],
        out_shape=[pltpu.SemaphoreType.DMA(())] * (2 * n) + [hbm(a) for a in srcs] + [hbm(a) for a in lands]
        + [jax.ShapeDtypeStruct((8, LANES), F32)],
        input_output_aliases={k: 2 * n + k for k in range(2 * n)},
        compiler_params=pltpu.CompilerParams(has_side_effects=DATAFLOW),
    )(*[pltpu.with_memory_space_constraint(a, pltpu.HBM) for a in list(srcs) + list(lands) + [after]])
    return (out[:n], out[n:2 * n], out[2 * n:3 * n], out[3 * n:4 * n]), out[-1][0, 0]


def _split_wait(name, n_copies, started, after):
    send_sems, recv_sems, srcs, lands = started
    n = len(srcs)

    def body(*refs):
        for k in range(n):
            slots = refs[n + k].at[pl.ds(0, n_copies)]
            all_copies = pltpu.make_async_remote_copy(
                src_ref=slots, dst_ref=slots, send_sem=refs[2 * n + k], recv_sem=refs[3 * n + k],
                device_id=_place()[0], device_id_type=MESH)
            all_copies.wait_send()
            all_copies.wait_recv()

    hbm = lambda a: pltpu.HBM(a.shape, a.dtype)
    out = pl.pallas_call(
        body, name=name,
        in_specs=[HBM_SPEC] * (2 * n) + [SEM_SPEC] * (2 * n) + [HBM_SPEC],
        out_specs=[HBM_SPEC] * (2 * n),
        out_shape=[hbm(a) for a in srcs] + [hbm(a) for a in lands],
        input_output_aliases={k: k for k in range(2 * n)},
        compiler_params=pltpu.CompilerParams(has_side_effects=DATAFLOW),
    )(*srcs, *lands, *send_sems, *recv_sems, pltpu.with_memory_space_constraint(after, pltpu.HBM))
    return out[:n], out[n:]


def _slot(chip, core):
    return 4 * chip[0] + 2 * chip[1] + core


def _gather_copies(src, land, send_sem, recv_sem):
    (x, y, c), sib, chips = _place()
    return [pltpu.make_async_remote_copy(src_ref=src, dst_ref=land.at[_slot((x, y), c)], send_sem=send_sem,
                                         recv_sem=recv_sem, device_id=to, device_id_type=MESH)
            for to in [sib] + [(*chip, c) for chip in chips]]


def _gather_pass_on(lands, name):
    n = len(lands)

    def body(*refs):
        ins, outs = refs[:n], refs[n:2 * n]
        send_sems, recv_sems = refs[2 * n:]
        (x, y, c), sib, chips = _place()
        sends = []
        for k in range(n):
            for j, chip in enumerate(chips):
                sends.append(pltpu.make_async_remote_copy(
                    src_ref=ins[k].at[_slot(chip, c)], dst_ref=outs[k].at[_slot(chip, c)],
                    send_sem=send_sems.at[k, j], recv_sem=recv_sems.at[k, j], device_id=sib, device_id_type=MESH))
        for cp in sends:
            cp.start()
        for cp in sends:
            cp.wait_recv()
        for cp in sends:
            cp.wait_send()

    any_spec = pl.BlockSpec(memory_space=pl.ANY)
    return pl.pallas_call(
        body, name=name,
        in_specs=[any_spec] * n, out_specs=[any_spec] * n,
        out_shape=[jax.ShapeDtypeStruct(a.shape, a.dtype) for a in lands],
        input_output_aliases={k: k for k in range(n)},
        scratch_shapes=[pltpu.SemaphoreType.DMA((n, 3))] * 2,
        compiler_params=pltpu.CompilerParams(has_side_effects=True),
    )(*lands)


def _chip_copies(src, land, send_sem, recv_sem):
    (x, y, c), _, chips = _place()
    return [pltpu.make_async_remote_copy(src_ref=src.at[j], dst_ref=land.at[j], send_sem=send_sem,
                                         recv_sem=recv_sem, device_id=(*chip, c), device_id_type=MESH)
            for j, chip in enumerate(chips)]


ROW_TILE_BYTES = 14 * 1024 * 1024


def _stream_tile(r, c, bytes_per_elem):
    if r * c * bytes_per_elem <= ROW_TILE_BYTES:
        return r, c
    rows = [t for t in range(16, r, 16) if r % t == 0 and t * c * bytes_per_elem <= ROW_TILE_BYTES]
    if rows:
        return max(rows), c
    cols = [t for t in range(LANES, c, LANES) if c % t == 0 and r * t * bytes_per_elem <= ROW_TILE_BYTES]
    return r, max(cols)


def _sibling_copies(src, land, send_sem, recv_sem):
    (x, y, c), sib, _ = _place()
    return [pltpu.make_async_remote_copy(src_ref=src.at[2 * q + 1 - c], dst_ref=land.at[q], send_sem=send_sem,
                                         recv_sem=recv_sem, device_id=sib, device_id_type=MESH)
            for q in range(4)]


def _pair_sum(slots, blocks, theirs, name):
    _, r, c = theirs.shape
    tr, tc = _stream_tile(r, c, 3 * theirs.dtype.itemsize)

    def body(slot_ref, a_ref, b_ref, o_ref):
        o_ref[...] = (a_ref[...].astype(F32) + b_ref[...].astype(F32)).astype(o_ref.dtype)

    return pl.pallas_call(
        body, name=name,
        grid_spec=pltpu.PrefetchScalarGridSpec(
            num_scalar_prefetch=1, grid=(3, r // tr, c // tc),
            in_specs=[pl.BlockSpec((1, tr, tc), lambda j, a, b, slot_ref: (slot_ref[j], a, b)),
                      pl.BlockSpec((1, tr, tc), lambda j, a, b, slot_ref: (slot_ref[3 + j], a, b))],
            out_specs=pl.BlockSpec((1, tr, tc), lambda j, a, b, slot_ref: (j, a, b))),
        out_shape=jax.ShapeDtypeStruct((3, r, c), theirs.dtype),
        compiler_params=_params(("parallel", "parallel", "parallel")),
    )(slots, blocks, theirs)


def _adamw_small(parts, ws, ms, vs):
    n = len(ws)
    c1 = 1.0 - ADAM_B1
    c2 = 1.0 - ADAM_B2
    bc1 = 1.0 - ADAM_B1 ** ADAM_STEP
    bc2 = 1.0 - ADAM_B2 ** ADAM_STEP

    def body(*refs):
        p_ref, outs = refs[0], refs[1 + 3 * n:]
        off = 0
        for k in range(n):
            w_ref, m_ref, v_ref = refs[1 + k], refs[1 + n + k], refs[1 + 2 * n + k]
            width = w_ref.shape[1]
            g = p_ref[0, :, off:off + width]
            for j in range(1, N_DEV):
                g = g + p_ref[j, :, off:off + width]
            nm = ADAM_B1 * m_ref[...] + c1 * g
            nv = ADAM_B2 * v_ref[...] + c2 * (g * g)
            outs[k][...] = g
            outs[n + k][...] = -ADAM_LR * ((nm / bc1) / (jnp.sqrt(nv / bc2) + ADAM_EPS) + ADAM_WD * w_ref[...])
            outs[2 * n + k][...] = nm
            outs[3 * n + k][...] = nv
            off += width

    out = pl.pallas_call(
        body, name="adamw_small",
        out_shape=[jax.ShapeDtypeStruct(a.shape, F32) for a in ws] * 4,
        compiler_params=pltpu.CompilerParams(vmem_limit_bytes=VMEM_LIMIT),
    )(parts, *ws, *ms, *vs)
    return out[:n], out[n:2 * n], out[2 * n:3 * n], out[3 * n:]


def _reduce_adamw(parts, w, m, v, name, blocks, theirs, slots):
    r, c = w.shape
    n_parts = parts.shape[0]
    tr, tc = _stream_tile(r, c, (n_parts + 2) * parts.dtype.itemsize + 7 * 4)
    c1 = 1.0 - ADAM_B1
    c2 = 1.0 - ADAM_B2
    bc1 = 1.0 - ADAM_B1 ** ADAM_STEP
    bc2 = 1.0 - ADAM_B2 ** ADAM_STEP

    def body(_, b_ref, t_ref, p_ref, w_ref, m_ref, v_ref, g_ref, d_ref, nm_ref, nv_ref):
        g = b_ref[0].astype(F32) + t_ref[0].astype(F32)
        for j in range(n_parts):
            g = g + p_ref[j].astype(F32)
        nm = ADAM_B1 * m_ref[...] + c1 * g
        nv = ADAM_B2 * v_ref[...] + c2 * (g * g)
        g_ref[...] = g
        nm_ref[...] = nm
        nv_ref[...] = nv
        d_ref[...] = -ADAM_LR * ((nm / bc1) / (jnp.sqrt(nv / bc2) + ADAM_EPS) + ADAM_WD * w_ref[...])

    out = jax.ShapeDtypeStruct((r, c), F32)
    grid = (r // tr, c // tc)
    blk = pl.BlockSpec((tr, tc), lambda i, j, slot_ref: (i, j))
    return pl.pallas_call(
        body, name=name,
        grid_spec=pltpu.PrefetchScalarGridSpec(
            num_scalar_prefetch=1, grid=grid,
            in_specs=[pl.BlockSpec((1, tr, tc), lambda i, j, slot_ref: (slot_ref[0], i, j)),
                      pl.BlockSpec((1, tr, tc), lambda i, j, slot_ref: (slot_ref[1], i, j)),
                      pl.BlockSpec((n_parts, tr, tc), lambda i, j, slot_ref: (0, i, j)), blk, blk, blk],
            out_specs=[blk] * 4),
        out_shape=[out] * 4,
        compiler_params=_params(("parallel", "parallel")),
    )(slots, blocks, theirs, parts, w, m, v)


_MIX = ("w_in", "w_uq", "w_ukv", "conv_w", "w_out")
_FFN = ("w_gate", "w_up", "w_down")
_BIG = _MIX + _FFN
_TRANSPOSED = ("w_in", "w_uq", "w_gate", "w_up")
_SMALL = ("pre_mix_norm", "q_norm", "kv_norm", "conv_b", "conv_ln_g", "conv_ln_b", "conv_out_norm",
          "attn_out_norm", "post_mix_norm", "pre_ffn_norm", "post_ffn_norm")
_ORDER = ("pre_mix_norm", "w_in", "q_norm", "w_uq", "kv_norm", "w_ukv", "conv_w", "conv_b", "conv_ln_g",
          "conv_ln_b", "conv_out_norm", "attn_out_norm", "w_out", "post_mix_norm", "pre_ffn_norm", "w_gate",
          "w_up", "w_down", "post_ffn_norm")


def _cols_from_shards(g):
    return jnp.transpose(g, (1, 0, 2)).reshape(g.shape[1], N_DEV * g.shape[2])


def _cols_to_shards(w):
    k, n8 = w.shape
    return jnp.transpose(w.reshape(k, N_DEV, n8 // N_DEV), (1, 0, 2))


def _step(x, positions, loss_target, w, m, v):
    s, d = x.shape[1], x.shape[2]
    x2, tgt = x[0], loss_target[0]
    pos = positions.reshape(s, 1)
    vecs = {n: w[n] for n in _SMALL}
    core = lax.axis_index("c").astype(jnp.int32)
    my_chip = (2 * lax.axis_index("x") + lax.axis_index("y")).astype(jnp.int32)
    other_chips = [my_chip ^ 2, my_chip ^ 1, my_chip ^ 3]
    pair_slots = jnp.stack([2 * q + core for q in other_chips] + other_chips)
    own_slots = jnp.stack([2 * my_chip + core, my_chip])
    own = {}
    n_in_cols = N_DEV * w["w_in"].shape[2]
    gathers, scatters, to_sibling = {}, {}, []

    def shard(t, n):
        return t[n][0].T if n in _TRANSPOSED else t[n][0]

    def gather_start(names, tag, after, zero=0.0):
        srcs = [w[n][0] if n == "conv_w" else (shard(w, n) + zero).astype(BF16) for n in names]
        lands = [lax.empty((N_DEV,) + a.shape, a.dtype) for a in srcs]
        gathers[tag], zero = _split_start("gather_" + tag + "_start", _gather_copies, srcs, lands, after)
        return zero

    def gather_finish(names, tag, after):
        srcs, lands = _split_wait("gather_" + tag + "_wait", 4, gathers[tag], after)
        lands = _gather_pass_on(lands, "gather_" + tag + "_pass_on")
        me = _my_index()
        return {n: lax.dynamic_update_slice(g, a[None], (me,) + (0,) * a.ndim) for n, g, a in zip(names, lands, srcs)}

    def in_weights_fn(h):
        w_in_g = gather_finish(("w_in",), "in", h)["w_in"]
        return jnp.pad(w_in_g.reshape(-1, d), ((0, LANES - QK_ROPE), (0, 0))), 0.0

    def mix_weights_fn(z):
        gath = gather_finish(_MIX[1:], "mix", z)
        w_uq_t = jnp.pad(gath["w_uq"], ((0, 0), (0, HEAD_PAD - QK_HEAD), (0, 0))).reshape(N_HEADS * HEAD_PAD, -1)
        return (w_uq_t, _cols_from_shards(gath["w_ukv"]), _cols_from_shards(gath["conv_w"]),
                gath["w_out"].reshape(-1, d))

    def up_weights_fn(mix):
        gath = gather_finish(("w_gate", "w_up"), "up", mix)
        return gath["w_gate"], gath["w_up"]

    def down_weights_fn(act):
        return gather_finish(("w_down",), "down", act)["w_down"]

    def scatter_advance(after):
        if not to_sibling:
            return 0.0
        names, tag, sent = to_sibling.pop()
        blocks, theirs = _split_wait("to_sibling_" + tag + "_wait", 4, sent, after)
        own.update(zip(names, zip(blocks, theirs)))
        pairs = [_pair_sum(pair_slots, b, t, "pair_sum_" + n) for n, b, t in zip(names, blocks, theirs)]
        lands = [lax.empty((3,) + p.shape[1:], p.dtype) for p in pairs]
        scatters[tag], zero = _split_start("scatter_" + tag + "_start", _chip_copies, pairs, lands, theirs[0])
        return zero

    to_blocks = {
        "w_in": lambda a: a[:n_in_cols].reshape(N_DEV, -1, d),
        "w_uq": lambda a: a.reshape(N_HEADS, HEAD_PAD, -1)[:, :QK_HEAD],
        "w_ukv": _cols_to_shards, "conv_w": _cols_to_shards,
        "w_out": lambda a: a.reshape(N_DEV, -1, d),
    }

    def grads_fn(tag, names, grads, after):
        zero = scatter_advance(after)
        if not grads:
            return zero
        blocks = [to_blocks.get(n, lambda a: a)(a) for n, a in zip(names, grads)]
        lands = [lax.empty((4,) + b.shape[1:], b.dtype) for b in blocks]
        sent, zero2 = _split_start("to_sibling_" + tag + "_start", _sibling_copies, blocks, lands, after)
        to_sibling.append((names, tag, sent))
        return zero + zero2

    zero = gather_start(("w_in",), "in", x2)
    zero = gather_start(_MIX[1:], "mix", x2, zero)
    zero = gather_start(("w_gate", "w_up"), "up", x2, zero)
    vecs["pre_mix_norm"] = vecs["pre_mix_norm"] + gather_start(("w_down",), "down", x2, zero)
    totals = []

    def loss_fn(loss):
        totals.append(lax.psum(loss[0, 0], ("x", "y", "c")))
        return jnp.minimum(jnp.abs(totals[0]), 0.0)

    _, grad_x, g = _local_step(x2, pos, tgt, vecs, in_weights_fn, mix_weights_fn, up_weights_fn, down_weights_fn,
                               loss_fn, grads_fn)
    total = totals[0]

    small = jnp.concatenate([g[n] for n in _SMALL], axis=1)
    small_started, zero = _split_start("gather_small_start", _small_copies, [small],
                                       [lax.empty((N_DEV,) + small.shape, F32)], grad_x)

    res = {}
    after = grad_x
    own_slots = own_slots + zero.astype(jnp.int32)
    for tag, names in (("down", ("w_down",)), ("up", ("w_gate", "w_up")), ("out", ("w_out",)),
                       ("in", ("w_in", "w_uq", "w_ukv", "conv_w"))):
        _, recv = _split_wait("scatter_" + tag + "_wait", 3, scatters[tag], after)
        for n, parts in zip(names, recv):
            res[n] = _reduce_adamw(parts, shard(w, n), shard(m, n), shard(v, n), "adamw_" + n, *own[n], own_slots)
            after = res[n][1]
            res[n] = [(t.T if n in _TRANSPOSED else t)[None] for t in res[n]]
    (small,), (small_all,) = _split_wait("gather_small_wait", N_DEV - 1, small_started, after)
    small_all = lax.dynamic_update_slice(small_all, small[None], (_my_index(), 0, 0))
    small_res = _adamw_small(small_all, *[[t[n] for n in _SMALL] for t in (w, m, v)])
    for k, n in enumerate(_SMALL):
        res[n] = [part[k] for part in small_res]

    outs = [total, grad_x[None]]
    for part in range(4):
        outs.extend(res[n][part] for n in _ORDER)
    return tuple(outs)


def kernel(x, positions, pre_mix_norm, w_in, q_norm, w_uq, kv_norm, w_ukv, conv_w, conv_b, conv_ln_g, conv_ln_b, conv_out_norm, attn_out_norm, w_out, post_mix_norm, pre_ffn_norm, w_gate, w_up, w_down, post_ffn_norm, loss_target, m_pre_mix_norm, m_w_in, m_q_norm, m_w_uq, m_kv_norm, m_w_ukv, m_conv_w, m_conv_b, m_conv_ln_g, m_conv_ln_b, m_conv_out_norm, m_attn_out_norm, m_w_out, m_post_mix_norm, m_pre_ffn_norm, m_w_gate, m_w_up, m_w_down, m_post_ffn_norm, v_pre_mix_norm, v_w_in, v_q_norm, v_w_uq, v_kv_norm, v_w_ukv, v_conv_w, v_conv_b, v_conv_ln_g, v_conv_ln_b, v_conv_out_norm, v_attn_out_norm, v_w_out, v_post_mix_norm, v_pre_ffn_norm, v_w_gate, v_w_up, v_w_down, v_post_ffn_norm):
    w = dict(zip(_ORDER, (pre_mix_norm, w_in, q_norm, w_uq, kv_norm, w_ukv, conv_w, conv_b, conv_ln_g, conv_ln_b,
                          conv_out_norm, attn_out_norm, w_out, post_mix_norm, pre_ffn_norm, w_gate, w_up, w_down,
                          post_ffn_norm)))
    m = dict(zip(_ORDER, (m_pre_mix_norm, m_w_in, m_q_norm, m_w_uq, m_kv_norm, m_w_ukv, m_conv_w, m_conv_b,
                          m_conv_ln_g, m_conv_ln_b, m_conv_out_norm, m_attn_out_norm, m_w_out, m_post_mix_norm,
                          m_pre_ffn_norm, m_w_gate, m_w_up, m_w_down, m_post_ffn_norm)))
    v = dict(zip(_ORDER, (v_pre_mix_norm, v_w_in, v_q_norm, v_w_uq, v_kv_norm, v_w_ukv, v_conv_w, v_conv_b,
                          v_conv_ln_g, v_conv_ln_b, v_conv_out_norm, v_attn_out_norm, v_w_out, v_post_mix_norm,
                          v_pre_ffn_norm, v_w_gate, v_w_up, v_w_down, v_post_ffn_norm)))
    return _step(x, positions, loss_target, w, m, v)
```

```python
import functools

import jax
import jax.numpy as jnp
from jax import lax
from jax.experimental import pallas as pl
from jax.experimental.pallas import tpu as pltpu

N_DEV = 8
N_HEADS = 8
QK_NOPE = 128
QK_ROPE = 64
V_HEAD = 128
QK_HEAD = QK_NOPE + QK_ROPE
HEAD_PAD = 256
LANES = 128
ATTN_BLOCK = 512
CONV_K = 31
CONV_PAD = 32
EPS = 1e-6
ROPE_THETA = 10000.0
ADAM_LR = 0.001
ADAM_B1 = 0.9
ADAM_B2 = 0.999
ADAM_EPS = 1e-08
ADAM_WD = 0.01
ADAM_STEP = 10
VMEM_LIMIT = 56 * 1024 * 1024
F32 = jnp.float32
BF16 = jnp.bfloat16
MESH = pl.DeviceIdType.MESH
NEG = -1e30


def _pick(n, prefs):
    for p in prefs:
        if p <= n and n % p == 0:
            return p
    return n


def _params(sem):
    return pltpu.CompilerParams(dimension_semantics=sem, vmem_limit_bytes=VMEM_LIMIT)


_DIMS = {"nn": (((1,), (0,)), ((), ())), "nt": (((1,), (1,)), ((), ())), "tn": (((0,), (0,)), ((), ()))}


MM_VMEM_BUDGET = 40 * 1024 * 1024
MM_MAX_MACS = 3 * 1024 ** 3


V7X_HBM_BYTES_PER_S = 3.0e12
V7X_MXU_MACS_PER_S = 0.45e15
GRID_STEP_S = 0.35e-6


def _mm_tiles(m, n, k, size_a, size_b, size_o):
    best = None
    for tm in sorted({m, 1024, 512, 256, 128}, reverse=True):
        if tm > m or m % tm:
            continue
        for tn in sorted({n, 2048, 1024, 512, 384, 256, 128}, reverse=True):
            if tn > n or n % tn:
                continue
            vmem = 2 * (tm * k * size_a + k * tn * size_b + tm * tn * size_o)
            if vmem > MM_VMEM_BUDGET or tm * tn * k > MM_MAX_MACS:
                continue
            b_reads = 1 if tn == n else m // tm
            traffic = m * k * size_a + b_reads * k * n * size_b + m * n * size_o
            exposed = tm * k * size_a + k * tn * size_b + tm * tn * size_o
            steps = (m // tm) * (n // tn)
            key = (max(traffic / V7X_HBM_BYTES_PER_S, m * n * k / V7X_MXU_MACS_PER_S)
                   + exposed / V7X_HBM_BYTES_PER_S + steps * GRID_STEP_S)
            if best is None or key < best[0]:
                best = (key, tm, tn)
    assert best is not None, (m, n, k)
    return best[1], best[2]


def _mm(a, b, mode, name, out_dtype=F32, behind=None):
    if mode == "nn":
        (m, k), (k2, n) = a.shape, b.shape
    elif mode == "nt":
        (m, k), (n, k2) = a.shape, b.shape
    else:
        (k, m), (k2, n) = a.shape, b.shape
    assert k == k2, (a.shape, b.shape, mode)
    tm, tn = _mm_tiles(m, n, k, a.dtype.itemsize, b.dtype.itemsize, jnp.dtype(out_dtype).itemsize)
    dims = _DIMS[mode]

    def body(a_ref, b_ref, *rest):
        acc = lax.dot_general(a_ref[...].astype(BF16), b_ref[...].astype(BF16), dims, preferred_element_type=F32)
        rest[-1][...] = acc.astype(rest[-1].dtype)

    if mode == "tn":
        a_spec = pl.BlockSpec((k, tm), lambda i, j: (0, i))
    else:
        a_spec = pl.BlockSpec((tm, k), lambda i, j: (i, 0))
    if mode == "nt":
        b_spec = pl.BlockSpec((tn, k), lambda i, j: (j, 0))
    else:
        b_spec = pl.BlockSpec((k, tn), lambda i, j: (0, j))
    o_spec = pl.BlockSpec((tm, tn), lambda i, j: (i, j))
    extra = [] if behind is None else [behind]
    return pl.pallas_call(
        body, name=name,
        grid=(m // tm, n // tn),
        in_specs=[a_spec, b_spec] + [pl.BlockSpec((8, LANES), lambda i, j: (0, 0))] * len(extra),
        out_specs=o_spec,
        out_shape=jax.ShapeDtypeStruct((m, n), out_dtype),
        compiler_params=_params(("parallel", "parallel")),
    )(a, b, *extra)


def _sigmoid(x):
    return 1.0 / (1.0 + jnp.exp(-x))


def _rms(x, g):
    r = lax.rsqrt(jnp.mean(x * x, axis=-1, keepdims=True) + EPS)
    return (x * r) * g


def _rms_bwd(x, g, dy):
    r = lax.rsqrt(jnp.mean(x * x, axis=-1, keepdims=True) + EPS)
    xh = x * r
    dyg = dy * g
    dx = r * (dyg - xh * jnp.mean(dyg * xh, axis=-1, keepdims=True))
    return dx, jnp.sum(dy * xh, axis=0, keepdims=True)


def _ln(x, g, b):
    mu = jnp.mean(x, axis=-1, keepdims=True)
    xc = x - mu
    rs = lax.rsqrt(jnp.mean(xc * xc, axis=-1, keepdims=True) + EPS)
    return (xc * rs) * g + b


def _ln_bwd(x, g, dy):
    mu = jnp.mean(x, axis=-1, keepdims=True)
    xc = x - mu
    rs = lax.rsqrt(jnp.mean(xc * xc, axis=-1, keepdims=True) + EPS)
    xh = xc * rs
    dyg = dy * g
    dx = rs * (dyg - jnp.mean(dyg, axis=-1, keepdims=True) - xh * jnp.mean(dyg * xh, axis=-1, keepdims=True))
    return dx, jnp.sum(dy * xh, axis=0, keepdims=True), jnp.sum(dy, axis=0, keepdims=True)


def _silu(x):
    return x * _sigmoid(x)


def _silu_grad(x):
    s = _sigmoid(x)
    return s * (1.0 + x * (1.0 - s))


def _rope(x, cos, sa, sb):
    return x * cos + pltpu.roll(x, 96, 1) * sa + pltpu.roll(x, 32, 1) * sb


def _rope_t(d, cos, sa, sb):
    return d * cos - pltpu.roll(d, 96, 1) * sa - pltpu.roll(d, 32, 1) * sb


def _rows(ts, w):
    return pl.BlockSpec((ts, w), lambda i: (i, 0))


def _vec(w):
    return pl.BlockSpec((1, w), lambda i: (0, 0))


def _acc_init(i, *refs):
    @pl.when(i == 0)
    def _():
        for r in refs:
            r[...] = jnp.zeros_like(r)


def _rope_tables(pos, inv_freq):
    s = pos.shape[0]
    ts = _pick(s, (512, 256, 128))

    def body(p_ref, f_ref, c_ref, sa_ref, sb_ref):
        ang = p_ref[...].astype(F32) * f_ref[...]
        lane = lax.broadcasted_iota(jnp.int32, ang.shape, 1)
        c, sn = jnp.cos(ang), jnp.sin(ang)
        c_ref[...] = jnp.where(lane < QK_ROPE, c, 0.0)
        sa_ref[...] = jnp.where(lane < QK_ROPE // 2, -sn, 0.0)
        sb_ref[...] = jnp.where((lane >= QK_ROPE // 2) & (lane < QK_ROPE), sn, 0.0)

    out = jax.ShapeDtypeStruct((s, LANES), F32)
    return pl.pallas_call(
        body, name="rope_tables", grid=(s // ts,),
        in_specs=[_rows(ts, 1), _vec(LANES)],
        out_specs=[_rows(ts, LANES)] * 3, out_shape=[out] * 3,
        compiler_params=_params(("parallel",)),
    )(pos, inv_freq)


def _pre_fwd(x, g):
    s, d = x.shape
    ts = _pick(s, (256, 128))

    def body(x_ref, g_ref, h_ref):
        h_ref[...] = _rms(x_ref[...], g_ref[...]).astype(BF16)

    return pl.pallas_call(
        body, name="pre_fwd", grid=(s // ts,),
        in_specs=[_rows(ts, d), _vec(d)], out_specs=_rows(ts, d),
        out_shape=jax.ShapeDtypeStruct((s, d), BF16),
        compiler_params=_params(("parallel",)),
    )(x, g)


def _split_fwd(z, gq, gkv, tabs, c, ql, kvl):
    s, zw = z.shape
    ts = _pick(s, (256, 128))
    o_q, o_kv, o_kr = 2 * c, 2 * c + ql, 2 * c + ql + kvl

    def body(z_ref, gq_ref, gkv_ref, c_ref, sa_ref, sb_ref, u0_ref, qn_ref, kvn_ref, kpe_ref):
        u0_ref[...] = z_ref[:, 0:c] * _sigmoid(z_ref[:, c:2 * c])
        qn_ref[...] = _rms(z_ref[:, o_q:o_kv], gq_ref[...]).astype(BF16)
        kvn_ref[...] = _rms(z_ref[:, o_kv:o_kr], gkv_ref[...]).astype(BF16)
        kpe_ref[...] = _rope(z_ref[:, o_kr:o_kr + LANES], c_ref[...], sa_ref[...], sb_ref[...]).astype(BF16)

    return pl.pallas_call(
        body, name="split_fwd", grid=(s // ts,),
        in_specs=[_rows(ts, zw), _vec(ql), _vec(kvl)] + [_rows(ts, LANES)] * 3,
        out_specs=[_rows(ts, c), _rows(ts, ql), _rows(ts, kvl), _rows(ts, LANES)],
        out_shape=[jax.ShapeDtypeStruct((s, c), F32), jax.ShapeDtypeStruct((s, ql), BF16),
                   jax.ShapeDtypeStruct((s, kvl), BF16), jax.ShapeDtypeStruct((s, LANES), BF16)],
        compiler_params=_params(("parallel",)),
    )(z, gq, gkv, *tabs)


def _split_bwd(du0, z, dqn, dkvn, dkpe_h, gq, gkv, tabs, c, ql, kvl):
    s, zw = z.shape
    ts = _pick(s, (256, 128))
    o_q, o_kv, o_kr = 2 * c, 2 * c + ql, 2 * c + ql + kvl

    def body(du0_ref, z_ref, dqn_ref, dkvn_ref, dkh_ref, gq_ref, gkv_ref, c_ref, sa_ref, sb_ref,
             dz_ref, dgq_ref, dgkv_ref):
        _acc_init(pl.program_id(0), dgq_ref, dgkv_ref)
        du0 = du0_ref[...]
        a = z_ref[:, 0:c]
        sg = _sigmoid(z_ref[:, c:2 * c])
        dz_ref[:, 0:c] = (du0 * sg).astype(BF16)
        dz_ref[:, c:2 * c] = (du0 * a * sg * (1.0 - sg)).astype(BF16)
        dq, dgq = _rms_bwd(z_ref[:, o_q:o_kv], gq_ref[...], dqn_ref[...])
        dz_ref[:, o_q:o_kv] = dq.astype(BF16)
        dgq_ref[...] += dgq
        dkv, dgkv = _rms_bwd(z_ref[:, o_kv:o_kr], gkv_ref[...], dkvn_ref[...])
        dz_ref[:, o_kv:o_kr] = dkv.astype(BF16)
        dgkv_ref[...] += dgkv
        dk = dkh_ref[:, 0:LANES]
        for h in range(1, N_HEADS):
            dk = dk + dkh_ref[:, h * LANES:(h + 1) * LANES]
        dz_ref[:, o_kr:o_kr + LANES] = _rope_t(dk, c_ref[...], sa_ref[...], sb_ref[...]).astype(BF16)

    return pl.pallas_call(
        body, name="split_bwd", grid=(s // ts,),
        in_specs=[_rows(ts, c), _rows(ts, zw), _rows(ts, ql), _rows(ts, kvl), _rows(ts, N_HEADS * LANES),
                  _vec(ql), _vec(kvl)] + [_rows(ts, LANES)] * 3,
        out_specs=[_rows(ts, zw), _vec(ql), _vec(kvl)],
        out_shape=[jax.ShapeDtypeStruct((s, zw), BF16), jax.ShapeDtypeStruct((1, ql), F32),
                   jax.ShapeDtypeStruct((1, kvl), F32)],
        compiler_params=_params(("arbitrary",)),
    )(du0, z, dqn, dkvn, dkpe_h, gq, gkv, *tabs)


def _q_rope(qpre, tabs):
    s, w = qpre.shape
    ts = _pick(s, (256, 128))

    def body(q_ref, c_ref, sa_ref, sb_ref, o_ref):
        cs, sa, sb = c_ref[...], sa_ref[...], sb_ref[...]
        for h in range(N_HEADS):
            lo = h * HEAD_PAD
            o_ref[:, lo:lo + QK_NOPE] = q_ref[:, lo:lo + QK_NOPE].astype(BF16)
            o_ref[:, lo + QK_NOPE:lo + HEAD_PAD] = _rope(q_ref[:, lo + QK_NOPE:lo + HEAD_PAD], cs, sa, sb).astype(BF16)

    return pl.pallas_call(
        body, name="q_rope", grid=(s // ts,),
        in_specs=[_rows(ts, w)] + [_rows(ts, LANES)] * 3, out_specs=_rows(ts, w),
        out_shape=jax.ShapeDtypeStruct((s, w), BF16),
        compiler_params=_params(("parallel",)),
    )(qpre, *tabs)


def _conv_fwd(u0, w, b):
    s, c = u0.shape
    tc = LANES
    rc = _pick(s, (256, 128))

    def body(u_ref, w_ref, b_ref, o_ref, pad_ref):
        pad_ref[0:CONV_PAD, :] = jnp.zeros((CONV_PAD, tc), F32)
        pad_ref[CONV_PAD:CONV_PAD + s, :] = u_ref[...]
        for r in range(s // rc):
            acc = jnp.broadcast_to(b_ref[...], (rc, tc))
            for k in range(CONV_K):
                lo = r * rc + CONV_PAD - (CONV_K - 1) + k
                acc = acc + w_ref[k:k + 1, :] * pad_ref[lo:lo + rc, :]
            o_ref[r * rc:(r + 1) * rc, :] = acc

    col = lambda j: (0, j)
    return pl.pallas_call(
        body, name="conv_fwd", grid=(c // tc,),
        in_specs=[pl.BlockSpec((s, tc), col), pl.BlockSpec((CONV_K, tc), col), pl.BlockSpec((1, tc), col)],
        out_specs=pl.BlockSpec((s, tc), col),
        out_shape=jax.ShapeDtypeStruct((s, c), F32),
        scratch_shapes=[pltpu.VMEM((s + CONV_PAD, tc), F32)],
        compiler_params=_params(("parallel",)),
    )(u0, w, b)


def _conv_bwd(du1, u0, w):
    s, c = u0.shape
    tc = LANES
    rc = _pick(s, (256, 128))

    def body(d_ref, u_ref, w_ref, du_ref, dw_ref, db_ref, upad_ref, dpad_ref):
        upad_ref[0:CONV_PAD, :] = jnp.zeros((CONV_PAD, tc), F32)
        upad_ref[CONV_PAD:CONV_PAD + s, :] = u_ref[...]
        dpad_ref[0:s, :] = d_ref[...]
        dpad_ref[s:s + CONV_PAD, :] = jnp.zeros((CONV_PAD, tc), F32)
        for r in range(s // rc):
            acc = jnp.zeros((rc, tc), F32)
            for k in range(CONV_K):
                lo = r * rc + (CONV_K - 1) - k
                acc = acc + w_ref[k:k + 1, :] * dpad_ref[lo:lo + rc, :]
            du_ref[r * rc:(r + 1) * rc, :] = acc
        for k in range(CONV_K):
            acc8 = jnp.zeros((8, tc), F32)
            for r in range(s // rc):
                lo = r * rc + CONV_PAD - (CONV_K - 1) + k
                prod = d_ref[r * rc:(r + 1) * rc, :] * upad_ref[lo:lo + rc, :]
                acc8 = acc8 + jnp.sum(prod.reshape(rc // 8, 8, tc), axis=0)
            dw_ref[k:k + 1, :] = jnp.sum(acc8, axis=0, keepdims=True)
        db_ref[...] = jnp.sum(d_ref[...], axis=0, keepdims=True)

    col = lambda j: (0, j)
    return pl.pallas_call(
        body, name="conv_bwd", grid=(c // tc,),
        in_specs=[pl.BlockSpec((s, tc), col), pl.BlockSpec((s, tc), col), pl.BlockSpec((CONV_K, tc), col)],
        out_specs=[pl.BlockSpec((s, tc), col), pl.BlockSpec((CONV_K, tc), col), pl.BlockSpec((1, tc), col)],
        out_shape=[jax.ShapeDtypeStruct((s, c), F32), jax.ShapeDtypeStruct((CONV_K, c), F32),
                   jax.ShapeDtypeStruct((1, c), F32)],
        scratch_shapes=[pltpu.VMEM((s + CONV_PAD, tc), F32), pltpu.VMEM((s + CONV_PAD, tc), F32)],
        compiler_params=_params(("parallel",)),
    )(du1, u0, w)


def _causal_mask(sc, qi, kj, tq, tk):
    rows = qi * tq + lax.broadcasted_iota(jnp.int32, sc.shape, 0)
    cols = kj * tk + lax.broadcasted_iota(jnp.int32, sc.shape, 1)
    return jnp.where(cols <= rows, sc, NEG)


def _attn_fwd(q, kv, kpe):
    s = q.shape[0]
    tq = tk = _pick(s, (ATTN_BLOCK, 256, 128))
    reps = tk // LANES
    scale = QK_HEAD ** -0.5
    nt = (((1,), (1,)), ((), ()))

    def body(q_ref, kn_ref, v_ref, kpe_ref, o_ref, lse_ref, kf_ref, vb_ref, m_ref, l_ref, acc_ref):
        i = pl.program_id(1)

        @pl.when(i == 0)
        def _():
            kf_ref[:, 0:QK_NOPE] = kn_ref[...].astype(BF16)
            kf_ref[:, QK_NOPE:HEAD_PAD] = kpe_ref[...]
            vb_ref[...] = v_ref[...].astype(BF16)

        qb = q_ref[...]
        m_ref[...] = jnp.full((tq, LANES), NEG, F32)
        l_ref[...] = jnp.zeros((tq, LANES), F32)
        acc_ref[...] = jnp.zeros((tq, V_HEAD), F32)

        def block(j, diagonal):
            off = j * tk
            sc = lax.dot_general(qb, kf_ref[pl.ds(off, tk), :], nt, preferred_element_type=F32) * scale
            if diagonal:
                sc = _causal_mask(sc, 0, 0, tq, tk)
            m_prev = m_ref[...]
            m_new = jnp.maximum(m_prev, jnp.max(sc, axis=1, keepdims=True))
            p = jnp.exp(sc - jnp.tile(m_new, (1, reps)))
            alpha = jnp.exp(m_prev - m_new)
            l_ref[...] = alpha * l_ref[...] + jnp.sum(p, axis=1, keepdims=True)
            acc_ref[...] = alpha * acc_ref[...] + jnp.dot(p.astype(BF16), vb_ref[pl.ds(off, tk), :],
                                                          preferred_element_type=F32)
            m_ref[...] = m_new

        for qi in range(s // tq):
            @pl.when(i == qi)
            def _(qi=qi):
                for j in range(qi):
                    block(j, False)
                block(qi, True)

        o_ref[...] = acc_ref[...] / l_ref[...]
        lse_ref[...] = m_ref[...] + jnp.log(l_ref[...])

    return pl.pallas_call(
        body, name="attn_fwd", grid=(N_HEADS, s // tq),
        in_specs=[pl.BlockSpec((tq, HEAD_PAD), lambda h, i: (i, h)),
                  pl.BlockSpec((s, QK_NOPE), lambda h, i: (0, 2 * h)),
                  pl.BlockSpec((s, V_HEAD), lambda h, i: (0, 2 * h + 1)),
                  pl.BlockSpec((s, LANES), lambda h, i: (0, 0))],
        out_specs=[pl.BlockSpec((tq, V_HEAD), lambda h, i: (i, h)),
                   pl.BlockSpec((tq, LANES), lambda h, i: (i, h))],
        out_shape=[jax.ShapeDtypeStruct((s, N_HEADS * V_HEAD), F32),
                   jax.ShapeDtypeStruct((s, N_HEADS * LANES), F32)],
        scratch_shapes=[pltpu.VMEM((s, HEAD_PAD), BF16), pltpu.VMEM((s, V_HEAD), BF16),
                        pltpu.VMEM((tq, LANES), F32), pltpu.VMEM((tq, LANES), F32), pltpu.VMEM((tq, V_HEAD), F32)],
        compiler_params=_params(("parallel", "arbitrary")),
    )(q, kv, kv, kpe)


def _attn_bwd(q, kv, kpe, o, do, lse, tabs):
    s = q.shape[0]
    tq = tk = _pick(s, (ATTN_BLOCK, 256, 128))
    nq = s // tq
    reps = tk // LANES
    scale = QK_HEAD ** -0.5
    nt = (((1,), (1,)), ((), ()))
    tn = (((0,), (0,)), ((), ()))

    def body(q_ref, kn_ref, v_ref, kpe_ref, o_ref, do_ref, lse_ref, c_ref, sa_ref, sb_ref, dqpre_ref, dkv_ref,
             dkpe_ref, kf_ref, vb_ref, dk_ref, dv_ref, dq_ref):
        j = pl.program_id(1)

        @pl.when(j == 0)
        def _():
            dq_ref[...] = jnp.zeros_like(dq_ref)

        kf_ref[:, 0:QK_NOPE] = kn_ref[...].astype(BF16)
        kf_ref[:, QK_NOPE:HEAD_PAD] = kpe_ref[...]
        vb_ref[...] = v_ref[...].astype(BF16)
        dk_ref[...] = jnp.zeros_like(dk_ref)
        dv_ref[...] = jnp.zeros_like(dv_ref)

        def block(i, diagonal):
            off = i * tq
            qb = q_ref[pl.ds(off, tq), :]
            dob = do_ref[pl.ds(off, tq), :]
            delta = jnp.sum(dob * o_ref[pl.ds(off, tq), :], axis=1, keepdims=True)
            sc = lax.dot_general(qb, kf_ref[...], nt, preferred_element_type=F32) * scale
            if diagonal:
                sc = _causal_mask(sc, 0, 0, tq, tk)
            p = jnp.exp(sc - jnp.tile(lse_ref[pl.ds(off, tq), :], (1, reps)))
            dob16 = dob.astype(BF16)
            dv_ref[...] += lax.dot_general(p.astype(BF16), dob16, tn, preferred_element_type=F32)
            dp = lax.dot_general(dob16, vb_ref[...], nt, preferred_element_type=F32)
            ds = (p * (dp - delta) * scale).astype(BF16)
            dq_ref[pl.ds(off, tq), :] += jnp.dot(ds, kf_ref[...], preferred_element_type=F32)
            dk_ref[...] += lax.dot_general(ds, qb, tn, preferred_element_type=F32)

        for kj in range(nq):
            @pl.when(j == kj)
            def _(kj=kj):
                block(kj, True)
                for i in range(kj + 1, nq):
                    block(i, False)

        dkv_ref[:, 0:QK_NOPE] = dk_ref[:, 0:QK_NOPE]
        dkv_ref[:, QK_NOPE:HEAD_PAD] = dv_ref[...]
        dkpe_ref[...] = dk_ref[:, QK_NOPE:HEAD_PAD]

        @pl.when(j == nq - 1)
        def _():
            dqpre_ref[:, 0:QK_NOPE] = dq_ref[:, 0:QK_NOPE].astype(BF16)
            dqpre_ref[:, QK_NOPE:HEAD_PAD] = _rope_t(dq_ref[:, QK_NOPE:HEAD_PAD], c_ref[...], sa_ref[...],
                                                     sb_ref[...]).astype(BF16)

    head_rows = lambda w: pl.BlockSpec((s, w), lambda h, j: (0, h))
    table = pl.BlockSpec((s, LANES), lambda h, j: (0, 0))
    return pl.pallas_call(
        body, name="attn_bwd", grid=(N_HEADS, s // tk),
        in_specs=[head_rows(HEAD_PAD),
                  pl.BlockSpec((tk, QK_NOPE), lambda h, j: (j, 2 * h)),
                  pl.BlockSpec((tk, V_HEAD), lambda h, j: (j, 2 * h + 1)),
                  pl.BlockSpec((tk, LANES), lambda h, j: (j, 0)),
                  head_rows(V_HEAD), head_rows(V_HEAD), head_rows(LANES), table, table, table],
        out_specs=[head_rows(HEAD_PAD),
                   pl.BlockSpec((tk, HEAD_PAD), lambda h, j: (j, h)),
                   pl.BlockSpec((tk, LANES), lambda h, j: (j, h))],
        out_shape=[jax.ShapeDtypeStruct((s, N_HEADS * HEAD_PAD), BF16),
                   jax.ShapeDtypeStruct((s, N_HEADS * HEAD_PAD), F32),
                   jax.ShapeDtypeStruct((s, N_HEADS * LANES), F32)],
        scratch_shapes=[pltpu.VMEM((tk, HEAD_PAD), BF16), pltpu.VMEM((tk, V_HEAD), BF16),
                        pltpu.VMEM((tk, HEAD_PAD), F32), pltpu.VMEM((tk, V_HEAD), F32),
                        pltpu.VMEM((s, HEAD_PAD), F32)],
        compiler_params=_params(("parallel", "arbitrary")),
    )(q, kv, kv, kpe, o, do, lse, *tabs)


def _mix_fwd(u1, lng, lnb, gcon, attn, gattn):
    s, c = u1.shape
    ac = attn.shape[1]
    ts = _pick(s, (256, 128))

    def body(u_ref, lg_ref, lb_ref, gc_ref, a_ref, ga_ref, o_ref):
        t3 = _silu(_ln(u_ref[...], lg_ref[...], lb_ref[...]))
        o_ref[:, 0:c] = _rms(t3, gc_ref[...]).astype(BF16)
        o_ref[:, c:c + ac] = _rms(a_ref[...], ga_ref[...]).astype(BF16)

    return pl.pallas_call(
        body, name="mix_fwd", grid=(s // ts,),
        in_specs=[_rows(ts, c), _vec(c), _vec(c), _vec(c), _rows(ts, ac), _vec(ac)],
        out_specs=_rows(ts, c + ac),
        out_shape=jax.ShapeDtypeStruct((s, c + ac), BF16),
        compiler_params=_params(("parallel",)),
    )(u1, lng, lnb, gcon, attn, gattn)


def _mix_bwd(dmixin, u1, lng, lnb, gcon, attn, gattn):
    s, c = u1.shape
    ac = attn.shape[1]
    ts = _pick(s, (256, 128))

    def body(d_ref, u_ref, lg_ref, lb_ref, gc_ref, a_ref, ga_ref,
             du_ref, da_ref, dlg_ref, dlb_ref, dgc_ref, dga_ref):
        _acc_init(pl.program_id(0), dlg_ref, dlb_ref, dgc_ref, dga_ref)
        u = u_ref[...]
        t2 = _ln(u, lg_ref[...], lb_ref[...])
        dt3, dgc = _rms_bwd(_silu(t2), gc_ref[...], d_ref[:, 0:c])
        du, dlg, dlb = _ln_bwd(u, lg_ref[...], dt3 * _silu_grad(t2))
        du_ref[...] = du
        dlg_ref[...] += dlg
        dlb_ref[...] += dlb
        dgc_ref[...] += dgc
        da, dga = _rms_bwd(a_ref[...], ga_ref[...], d_ref[:, c:c + ac])
        da_ref[...] = da
        dga_ref[...] += dga

    return pl.pallas_call(
        body, name="mix_bwd", grid=(s // ts,),
        in_specs=[_rows(ts, c + ac), _rows(ts, c), _vec(c), _vec(c), _vec(c), _rows(ts, ac), _vec(ac)],
        out_specs=[_rows(ts, c), _rows(ts, ac), _vec(c), _vec(c), _vec(c), _vec(ac)],
        out_shape=[jax.ShapeDtypeStruct((s, c), F32), jax.ShapeDtypeStruct((s, ac), F32),
                   jax.ShapeDtypeStruct((1, c), F32), jax.ShapeDtypeStruct((1, c), F32),
                   jax.ShapeDtypeStruct((1, c), F32), jax.ShapeDtypeStruct((1, ac), F32)],
        compiler_params=_params(("arbitrary",)),
    )(dmixin, u1, lng, lnb, gcon, attn, gattn)


def _post_mix_fwd(x, mix, gpost, gpre):
    s, d = x.shape
    ts = _pick(s, (256, 128))

    def body(x_ref, m_ref, gp_ref, gf_ref, x1_ref, hf_ref):
        x1 = x_ref[...] + _rms(m_ref[...], gp_ref[...])
        x1_ref[...] = x1
        hf_ref[...] = _rms(x1, gf_ref[...]).astype(BF16)

    return pl.pallas_call(
        body, name="post_mix_fwd", grid=(s // ts,),
        in_specs=[_rows(ts, d), _rows(ts, d), _vec(d), _vec(d)],
        out_specs=[_rows(ts, d), _rows(ts, d)],
        out_shape=[jax.ShapeDtypeStruct((s, d), F32), jax.ShapeDtypeStruct((s, d), BF16)],
        compiler_params=_params(("parallel",)),
    )(x, mix, gpost, gpre)


def _post_mix_bwd(dy, dhf, x1, gpre, mix, gpost):
    s, d = x1.shape
    ts = _pick(s, (256, 128))

    def body(dy_ref, dh_ref, x1_ref, gf_ref, m_ref, gp_ref, dx1_ref, dm_ref, dgf_ref, dgp_ref):
        _acc_init(pl.program_id(0), dgf_ref, dgp_ref)
        dxa, dgf = _rms_bwd(x1_ref[...], gf_ref[...], dh_ref[...])
        dx1 = dy_ref[...] + dxa
        dx1_ref[...] = dx1
        dgf_ref[...] += dgf
        dm, dgp = _rms_bwd(m_ref[...], gp_ref[...], dx1)
        dm_ref[...] = dm.astype(BF16)
        dgp_ref[...] += dgp

    return pl.pallas_call(
        body, name="post_mix_bwd", grid=(s // ts,),
        in_specs=[_rows(ts, d), _rows(ts, d), _rows(ts, d), _vec(d), _rows(ts, d), _vec(d)],
        out_specs=[_rows(ts, d), _rows(ts, d), _vec(d), _vec(d)],
        out_shape=[jax.ShapeDtypeStruct((s, d), F32), jax.ShapeDtypeStruct((s, d), BF16),
                   jax.ShapeDtypeStruct((1, d), F32), jax.ShapeDtypeStruct((1, d), F32)],
        compiler_params=_params(("arbitrary",)),
    )(dy, dhf, x1, gpre, mix, gpost)


def _ffn_up(hf, wg, wu):
    s, d = hf.shape
    nsh, fs, _ = wg.shape
    tm = _pick(s, (1024, 512, 256, 128))
    nt = (((1,), (1,)), ((), ()))

    def body(h_ref, wg_ref, wu_ref, dg_ref, du_ref, a_ref):
        h = h_ref[...]
        g = lax.dot_general(h, wg_ref[...], nt, preferred_element_type=F32)
        u = lax.dot_general(h, wu_ref[...], nt, preferred_element_type=F32)
        sg = _sigmoid(g)
        silu = g * sg
        dg_ref[...] = u * (sg * (1.0 + g * (1.0 - sg)))
        du_ref[...] = silu
        a_ref[...] = (silu * u).astype(BF16)

    w_spec = pl.BlockSpec((None, fs, d), lambda i, j: (j, 0, 0))
    o_spec = pl.BlockSpec((None, tm, fs), lambda i, j: (j, i, 0))
    return pl.pallas_call(
        body, name="ffn_up", grid=(s // tm, nsh),
        in_specs=[pl.BlockSpec((tm, d), lambda i, j: (i, 0)), w_spec, w_spec],
        out_specs=[o_spec] * 3,
        out_shape=[jax.ShapeDtypeStruct((nsh, s, fs), F32)] * 2 + [jax.ShapeDtypeStruct((nsh, s, fs), BF16)],
        compiler_params=_params(("parallel", "parallel")),
    )(hf, wg, wu)


def _ffn_down(acts, ws, name):
    n = len(acts)
    nsh, s, fs = acts[0].shape
    d = ws[0].shape[2]
    tm = _pick(s, (1024, 512, 256, 128))
    tn = _pick(d, (256, 128))
    a_mode = pl.Buffered(1)

    def body(*refs):
        acc = None
        for a_ref, w_ref in zip(refs[:n], refs[n:2 * n]):
            for j in range(nsh):
                part = jnp.dot(a_ref[j], w_ref[j], preferred_element_type=F32)
                acc = part if acc is None else acc + part
        refs[-1][...] = acc

    return pl.pallas_call(
        body, name=name, grid=(s // tm, d // tn),
        in_specs=[pl.BlockSpec((nsh, tm, fs), lambda i, j: (0, i, 0), pipeline_mode=a_mode)] * n
        + [pl.BlockSpec((nsh, fs, tn), lambda i, j: (0, 0, j))] * n,
        out_specs=pl.BlockSpec((tm, tn), lambda i, j: (i, j)),
        out_shape=jax.ShapeDtypeStruct((s, d), F32),
        compiler_params=_params(("parallel", "parallel")),
    )(*acts, *ws)


def _ffn_down_bwd(dff, wd, act_dgate, act_dup, behind):
    s, d = dff.shape
    nsh, fs, _ = wd.shape
    tm = _pick(s, (1024, 512, 256, 128))
    nt = (((1,), (1,)), ((), ()))

    def body(d_ref, w_ref, pg_ref, pu_ref, _, dg_ref, du_ref):
        dact = lax.dot_general(d_ref[...], w_ref[...], nt, preferred_element_type=F32)
        dg_ref[...] = (dact * pg_ref[...]).astype(BF16)
        du_ref[...] = (dact * pu_ref[...]).astype(BF16)

    h_spec = pl.BlockSpec((None, tm, fs), lambda i, j: (j, i, 0))
    return pl.pallas_call(
        body, name="ffn_down_bwd", grid=(s // tm, nsh),
        in_specs=[pl.BlockSpec((tm, d), lambda i, j: (i, 0)),
                  pl.BlockSpec((None, fs, d), lambda i, j: (j, 0, 0)), h_spec, h_spec,
                  pl.BlockSpec((8, LANES), lambda i, j: (0, 0))],
        out_specs=[h_spec] * 2,
        out_shape=[jax.ShapeDtypeStruct((nsh, s, fs), BF16)] * 2,
        compiler_params=_params(("parallel", "parallel")),
    )(dff, wd, act_dgate, act_dup, behind)


def _ffn_dw(hiddens, other, name):
    n = len(hiddens)
    nsh, s, fs = hiddens[0].shape
    d = other.shape[1]
    tn = (((0,), (0,)), ((), ()))

    def body(*refs):
        for a_ref, o_ref in zip(refs[:n], refs[n + 1:]):
            o_ref[...] = lax.dot_general(a_ref[...], refs[n][...], tn, preferred_element_type=F32).astype(BF16)

    return pl.pallas_call(
        body, name=name, grid=(nsh,),
        in_specs=[pl.BlockSpec((None, s, fs), lambda j: (j, 0, 0))] * n + [pl.BlockSpec((s, d), lambda j: (0, 0))],
        out_specs=[pl.BlockSpec((None, fs, d), lambda j: (j, 0, 0))] * n,
        out_shape=[jax.ShapeDtypeStruct((nsh, fs, d), BF16)] * n,
        compiler_params=_params(("parallel",)),
    )(*hiddens, other)


def _ffn_down_final(act, wd, x1, tgt, g):
    nsh, s, fs = act.shape
    d = x1.shape[1]
    ts = _pick(s, (256, 128))

    def body(a_ref, w_ref, x1_ref, t_ref, g_ref, loss_ref, dy_ref, dff_ref, dg_ref):
        _acc_init(pl.program_id(0), loss_ref, dg_ref)
        ff_v = jnp.dot(a_ref[0], w_ref[0], preferred_element_type=F32)
        for j in range(1, nsh):
            ff_v = ff_v + jnp.dot(a_ref[j], w_ref[j], preferred_element_type=F32)
        err = x1_ref[...] + _rms(ff_v, g_ref[...]) - t_ref[...]
        tok = jnp.mean(err * err, axis=-1, keepdims=True)
        loss_ref[...] += 0.5 * jnp.sum(tok, axis=0, keepdims=True)
        dy = err * (1.0 / d)
        dy_ref[...] = dy
        dff, dg = _rms_bwd(ff_v, g_ref[...], dy)
        dff_ref[...] = dff.astype(BF16)
        dg_ref[...] += dg

    return pl.pallas_call(
        body, name="ffn_down_final", grid=(s // ts,),
        in_specs=[pl.BlockSpec((nsh, ts, fs), lambda i: (0, i, 0)),
                  pl.BlockSpec((nsh, fs, d), lambda i: (0, 0, 0), pipeline_mode=pl.Buffered(1)),
                  _rows(ts, d), _rows(ts, d), _vec(d)],
        out_specs=[_vec(LANES), _rows(ts, d), _rows(ts, d), _vec(d)],
        out_shape=[jax.ShapeDtypeStruct((1, LANES), F32), jax.ShapeDtypeStruct((s, d), F32),
                   jax.ShapeDtypeStruct((s, d), BF16), jax.ShapeDtypeStruct((1, d), F32)],
        compiler_params=_params(("arbitrary",)),
    )(act, wd, x1, tgt, g)


def _pre_bwd(dx1, dh, x, g):
    s, d = x.shape
    ts = _pick(s, (256, 128))

    def body(dx1_ref, dh_ref, x_ref, g_ref, dx_ref, dg_ref):
        _acc_init(pl.program_id(0), dg_ref)
        dxa, dg = _rms_bwd(x_ref[...], g_ref[...], dh_ref[...])
        dx_ref[...] = dx1_ref[...] + dxa
        dg_ref[...] += dg

    return pl.pallas_call(
        body, name="pre_bwd", grid=(s // ts,),
        in_specs=[_rows(ts, d), _rows(ts, d), _rows(ts, d), _vec(d)],
        out_specs=[_rows(ts, d), _vec(d)],
        out_shape=[jax.ShapeDtypeStruct((s, d), F32), jax.ShapeDtypeStruct((1, d), F32)],
        compiler_params=_params(("arbitrary",)),
    )(dx1, dh, x, g)


def _local_step(x, pos, tgt, vecs, in_weights_fn, mix_weights_fn, up_weights_fn, down_weights_fn, loss_fn,
                grads_fn):
    c = vecs["conv_b"].shape[1]
    ql = vecs["q_norm"].shape[1]
    kvl = vecs["kv_norm"].shape[1]
    half = jnp.arange(0, QK_ROPE, 2, dtype=F32)
    freq = ROPE_THETA ** (-half / QK_ROPE)
    inv_freq = jnp.concatenate([freq, freq, jnp.zeros((LANES - QK_ROPE,), F32)])[None, :]
    tabs = _rope_tables(pos, inv_freq)

    h = _pre_fwd(x, vecs["pre_mix_norm"])
    w_in_t, zero = in_weights_fn(h)
    z = _mm(h, w_in_t, "nt", "mm_z")
    w_uq_t, w_ukv, conv_w, w_out = mix_weights_fn(z)
    u0, qn, kvn, kpe = _split_fwd(z, vecs["q_norm"] + zero, vecs["kv_norm"], tabs, c, ql, kvl)
    u1 = _conv_fwd(u0, conv_w, vecs["conv_b"])
    q = _q_rope(_mm(qn, w_uq_t, "nt", "mm_q"), tabs)
    kv = _mm(kvn, w_ukv, "nn", "mm_kv")
    attn, lse = _attn_fwd(q, kv, kpe)
    mixin = _mix_fwd(u1, vecs["conv_ln_g"], vecs["conv_ln_b"], vecs["conv_out_norm"], attn, vecs["attn_out_norm"])
    mix = _mm(mixin, w_out, "nn", "mm_mix")
    x1, hf = _post_mix_fwd(x, mix, vecs["post_mix_norm"], vecs["pre_ffn_norm"])
    w_gate, w_up = up_weights_fn(hf)
    act_dgate, act_dup, act = _ffn_up(hf, w_gate, w_up)
    w_down = down_weights_fn(act)
    loss, dy, dff, d_post_ffn = _ffn_down_final(act, w_down, x1, tgt, vecs["post_ffn_norm"])

    g = {"post_ffn_norm": d_post_ffn}
    zero = loss_fn(loss) + grads_fn("down", ("w_down",), _ffn_dw([act], dff, "ffn_dw_down"), dff)
    dgate, dup = _ffn_down_bwd(dff, w_down, act_dgate, act_dup, jnp.zeros((8, LANES), F32) + zero)
    zero = zero + grads_fn("up", ("w_gate", "w_up"), _ffn_dw([dgate, dup], hf, "ffn_dw_up"), dgate)
    dhf = _ffn_down([dgate, dup], [w_gate, w_up], "ffn_dhf")
    dx1, dmix, g["pre_ffn_norm"], g["post_mix_norm"] = _post_mix_bwd(
        dy, dhf, x1, vecs["pre_ffn_norm"] + zero, mix, vecs["post_mix_norm"])
    dmixin = _mm(dmix, w_out, "nt", "mm_dmixin")
    dw_out = _mm(mixin, dmix, "tn", "mm_dw_out", BF16)
    zero = grads_fn("out", ("w_out",), [dw_out], dmix)
    du1, dattn, g["conv_ln_g"], g["conv_ln_b"], g["conv_out_norm"], g["attn_out_norm"] = _mix_bwd(
        dmixin, u1, vecs["conv_ln_g"] + zero, vecs["conv_ln_b"], vecs["conv_out_norm"], attn, vecs["attn_out_norm"])
    du0, dw_conv, g["conv_b"] = _conv_bwd(du1, u0, conv_w)
    dqpre, dkv, dkpe_h = _attn_bwd(q, kv, kpe, attn, dattn, lse, tabs)
    dqn = _mm(dqpre, w_uq_t, "nn", "mm_dqn")
    dw_uq = _mm(dqpre, qn, "tn", "mm_dw_uq", BF16)
    dkvn = _mm(dkv, w_ukv, "nt", "mm_dkvn")
    dw_ukv = _mm(kvn, dkv, "tn", "mm_dw_ukv", BF16)
    zero = grads_fn("", (), [], dkvn)
    dz, g["q_norm"], g["kv_norm"] = _split_bwd(du0, z, dqn, dkvn, dkpe_h, vecs["q_norm"] + zero, vecs["kv_norm"], tabs,
                                               c, ql, kvl)
    dw_in = _mm(dz, h, "tn", "mm_dw_in", BF16)
    zero = grads_fn("in", ("w_in", "w_uq", "w_ukv", "conv_w"), [dw_in, dw_uq, dw_ukv, dw_conv], dz)
    dh = _mm(dz, w_in_t, "nn", "mm_dh")
    zero = zero + grads_fn("", (), [], dh)
    grad_x, g["pre_mix_norm"] = _pre_bwd(dx1, dh, x, vecs["pre_mix_norm"] + zero)
    return loss, grad_x, g


def _my_index():
    return 4 * lax.axis_index("x") + 2 * lax.axis_index("y") + lax.axis_index("c")


def _coords(idx):
    return ((idx >> 2) & 1, (idx >> 1) & 1, idx & 1)


def _place():
    x, y, c = lax.axis_index("x"), lax.axis_index("y"), lax.axis_index("c")
    return (x, y, c), (x, y, 1 - c), [(1 - x, y), (x, 1 - y), (1 - x, 1 - y)]


def _small_copies(src, land, send_sem, recv_sem):
    me = _my_index()
    return [pltpu.make_async_remote_copy(src_ref=src, dst_ref=land.at[me], send_sem=send_sem, recv_sem=recv_sem,
                                         device_id=_coords(me ^ p), device_id_type=MESH)
            for p in range(1, N_DEV)]


HBM_SPEC = pl.BlockSpec(memory_space=pltpu.HBM)
SEM_SPEC = pl.BlockSpec(memory_space=pltpu.SEMAPHORE)
DATAFLOW = pltpu.SideEffectType.DATAFLOW_SIDE_EFFECTING


def _split_start(name, copies_of, srcs, lands, after):
    n = len(srcs)

    def body(*refs):
        outs = refs[2 * n + 1:]
        for k in range(n):
            for cp in copies_of(refs[k], refs[n + k], outs[k], outs[n + k]):
                cp.start()
        outs[-1][...] = jnp.zeros_like(outs[-1])

    hbm = lambda a: pltpu.HBM(a.shape, a.dtype)
    out = pl.pallas_call(
        body, name=name,
        in_specs=[HBM_SPEC] * (2 * n + 1),
        out_specs=[SEM_SPEC] * (2 * n) + [HBM_SPEC] * (2 * n) + [pl.BlockSpec(memory_space=pltpu.VMEM)],
        out_shape=[pltpu.SemaphoreType.DMA(())] * (2 * n) + [hbm(a) for a in srcs] + [hbm(a) for a in lands]
        + [jax.ShapeDtypeStruct((8, LANES), F32)],
        input_output_aliases={k: 2 * n + k for k in range(2 * n)},
        compiler_params=pltpu.CompilerParams(has_side_effects=DATAFLOW),
    )(*[pltpu.with_memory_space_constraint(a, pltpu.HBM) for a in list(srcs) + list(lands) + [after]])
    return (out[:n], out[n:2 * n], out[2 * n:3 * n], out[3 * n:4 * n]), out[-1][0, 0]


def _split_wait(name, n_copies, started, after):
    send_sems, recv_sems, srcs, lands = started
    n = len(srcs)

    def body(*refs):
        for k in range(n):
            slots = refs[n + k].at[pl.ds(0, n_copies)]
            all_copies = pltpu.make_async_remote_copy(
                src_ref=slots, dst_ref=slots, send_sem=refs[2 * n + k], recv_sem=refs[3 * n + k],
                device_id=_place()[0], device_id_type=MESH)
            all_copies.wait_send()
            all_copies.wait_recv()

    hbm = lambda a: pltpu.HBM(a.shape, a.dtype)
    out = pl.pallas_call(
        body, name=name,
        in_specs=[HBM_SPEC] * (2 * n) + [SEM_SPEC] * (2 * n) + [HBM_SPEC],
        out_specs=[HBM_SPEC] * (2 * n),
        out_shape=[hbm(a) for a in srcs] + [hbm(a) for a in lands],
        input_output_aliases={k: k for k in range(2 * n)},
        compiler_params=pltpu.CompilerParams(has_side_effects=DATAFLOW),
    )(*srcs, *lands, *send_sems, *recv_sems, pltpu.with_memory_space_constraint(after, pltpu.HBM))
    return out[:n], out[n:]


def _slot(chip, core):
    return 4 * chip[0] + 2 * chip[1] + core


def _gather_copies(src, land, send_sem, recv_sem):
    (x, y, c), sib, chips = _place()
    return [pltpu.make_async_remote_copy(src_ref=src, dst_ref=land.at[_slot((x, y), c)], send_sem=send_sem,
                                         recv_sem=recv_sem, device_id=to, device_id_type=MESH)
            for to in [sib] + [(*chip, c) for chip in chips]]


def _gather_pass_on(lands, name):
    n = len(lands)

    def body(*refs):
        ins, outs = refs[:n], refs[n:2 * n]
        send_sems, recv_sems = refs[2 * n:]
        (x, y, c), sib, chips = _place()
        sends = []
        for k in range(n):
            for j, chip in enumerate(chips):
                sends.append(pltpu.make_async_remote_copy(
                    src_ref=ins[k].at[_slot(chip, c)], dst_ref=outs[k].at[_slot(chip, c)],
                    send_sem=send_sems.at[k, j], recv_sem=recv_sems.at[k, j], device_id=sib, device_id_type=MESH))
        for cp in sends:
            cp.start()
        for cp in sends:
            cp.wait_recv()
        for cp in sends:
            cp.wait_send()

    any_spec = pl.BlockSpec(memory_space=pl.ANY)
    return pl.pallas_call(
        body, name=name,
        in_specs=[any_spec] * n, out_specs=[any_spec] * n,
        out_shape=[jax.ShapeDtypeStruct(a.shape, a.dtype) for a in lands],
        input_output_aliases={k: k for k in range(n)},
        scratch_shapes=[pltpu.SemaphoreType.DMA((n, 3))] * 2,
        compiler_params=pltpu.CompilerParams(has_side_effects=True),
    )(*lands)


def _chip_copies(src, land, send_sem, recv_sem):
    (x, y, c), _, chips = _place()
    return [pltpu.make_async_remote_copy(src_ref=src.at[j], dst_ref=land.at[j], send_sem=send_sem,
                                         recv_sem=recv_sem, device_id=(*chip, c), device_id_type=MESH)
            for j, chip in enumerate(chips)]


ROW_TILE_BYTES = 14 * 1024 * 1024


def _stream_tile(r, c, bytes_per_elem):
    if r * c * bytes_per_elem <= ROW_TILE_BYTES:
        return r, c
    rows = [t for t in range(16, r, 16) if r % t == 0 and t * c * bytes_per_elem <= ROW_TILE_BYTES]
    if rows:
        return max(rows), c
    cols = [t for t in range(LANES, c, LANES) if c % t == 0 and r * t * bytes_per_elem <= ROW_TILE_BYTES]
    return r, max(cols)


def _sibling_copies(src, land, send_sem, recv_sem):
    (x, y, c), sib, _ = _place()
    return [pltpu.make_async_remote_copy(src_ref=src.at[2 * q + 1 - c], dst_ref=land.at[q], send_sem=send_sem,
                                         recv_sem=recv_sem, device_id=sib, device_id_type=MESH)
            for q in range(4)]


def _pair_sum(slots, blocks, theirs, name):
    _, r, c = theirs.shape
    tr, tc = _stream_tile(r, c, 3 * theirs.dtype.itemsize)

    def body(slot_ref, a_ref, b_ref, o_ref):
        o_ref[...] = (a_ref[...].astype(F32) + b_ref[...].astype(F32)).astype(o_ref.dtype)

    return pl.pallas_call(
        body, name=name,
        grid_spec=pltpu.PrefetchScalarGridSpec(
            num_scalar_prefetch=1, grid=(3, r // tr, c // tc),
            in_specs=[pl.BlockSpec((1, tr, tc), lambda j, a, b, slot_ref: (slot_ref[j], a, b)),
                      pl.BlockSpec((1, tr, tc), lambda j, a, b, slot_ref: (slot_ref[3 + j], a, b))],
            out_specs=pl.BlockSpec((1, tr, tc), lambda j, a, b, slot_ref: (j, a, b))),
        out_shape=jax.ShapeDtypeStruct((3, r, c), theirs.dtype),
        compiler_params=_params(("parallel", "parallel", "parallel")),
    )(slots, blocks, theirs)


def _adamw_small(parts, ws, ms, vs):
    n = len(ws)
    c1 = 1.0 - ADAM_B1
    c2 = 1.0 - ADAM_B2
    bc1 = 1.0 - ADAM_B1 ** ADAM_STEP
    bc2 = 1.0 - ADAM_B2 ** ADAM_STEP

    def body(*refs):
        p_ref, outs = refs[0], refs[1 + 3 * n:]
        off = 0
        for k in range(n):
            w_ref, m_ref, v_ref = refs[1 + k], refs[1 + n + k], refs[1 + 2 * n + k]
            width = w_ref.shape[1]
            g = p_ref[0, :, off:off + width]
            for j in range(1, N_DEV):
                g = g + p_ref[j, :, off:off + width]
            nm = ADAM_B1 * m_ref[...] + c1 * g
            nv = ADAM_B2 * v_ref[...] + c2 * (g * g)
            outs[k][...] = g
            outs[n + k][...] = -ADAM_LR * ((nm / bc1) / (jnp.sqrt(nv / bc2) + ADAM_EPS) + ADAM_WD * w_ref[...])
            outs[2 * n + k][...] = nm
            outs[3 * n + k][...] = nv
            off += width

    out = pl.pallas_call(
        body, name="adamw_small",
        out_shape=[jax.ShapeDtypeStruct(a.shape, F32) for a in ws] * 4,
        compiler_params=pltpu.CompilerParams(vmem_limit_bytes=VMEM_LIMIT),
    )(parts, *ws, *ms, *vs)
    return out[:n], out[n:2 * n], out[2 * n:3 * n], out[3 * n:]


def _reduce_adamw(parts, w, m, v, name, blocks, theirs, slots):
    r, c = w.shape
    n_parts = parts.shape[0]
    tr, tc = _stream_tile(r, c, (n_parts + 2) * parts.dtype.itemsize + 7 * 4)
    c1 = 1.0 - ADAM_B1
    c2 = 1.0 - ADAM_B2
    bc1 = 1.0 - ADAM_B1 ** ADAM_STEP
    bc2 = 1.0 - ADAM_B2 ** ADAM_STEP

    def body(_, b_ref, t_ref, p_ref, w_ref, m_ref, v_ref, g_ref, d_ref, nm_ref, nv_ref):
        g = b_ref[0].astype(F32) + t_ref[0].astype(F32)
        for j in range(n_parts):
            g = g + p_ref[j].astype(F32)
        nm = ADAM_B1 * m_ref[...] + c1 * g
        nv = ADAM_B2 * v_ref[...] + c2 * (g * g)
        g_ref[...] = g
        nm_ref[...] = nm
        nv_ref[...] = nv
        d_ref[...] = -ADAM_LR * ((nm / bc1) / (jnp.sqrt(nv / bc2) + ADAM_EPS) + ADAM_WD * w_ref[...])

    out = jax.ShapeDtypeStruct((r, c), F32)
    grid = (r // tr, c // tc)
    blk = pl.BlockSpec((tr, tc), lambda i, j, slot_ref: (i, j))
    return pl.pallas_call(
        body, name=name,
        grid_spec=pltpu.PrefetchScalarGridSpec(
            num_scalar_prefetch=1, grid=grid,
            in_specs=[pl.BlockSpec((1, tr, tc), lambda i, j, slot_ref: (slot_ref[0], i, j)),
                      pl.BlockSpec((1, tr, tc), lambda i, j, slot_ref: (slot_ref[1], i, j)),
                      pl.BlockSpec((n_parts, tr, tc), lambda i, j, slot_ref: (0, i, j)), blk, blk, blk],
            out_specs=[blk] * 4),
        out_shape=[out] * 4,
        compiler_params=_params(("parallel", "parallel")),
    )(slots, blocks, theirs, parts, w, m, v)


_MIX = ("w_in", "w_uq", "w_ukv", "conv_w", "w_out")
_FFN = ("w_gate", "w_up", "w_down")
_BIG = _MIX + _FFN
_TRANSPOSED = ("w_in", "w_uq", "w_gate", "w_up")
_SMALL = ("pre_mix_norm", "q_norm", "kv_norm", "conv_b", "conv_ln_g", "conv_ln_b", "conv_out_norm",
          "attn_out_norm", "post_mix_norm", "pre_ffn_norm", "post_ffn_norm")
_ORDER = ("pre_mix_norm", "w_in", "q_norm", "w_uq", "kv_norm", "w_ukv", "conv_w", "conv_b", "conv_ln_g",
          "conv_ln_b", "conv_out_norm", "attn_out_norm", "w_out", "post_mix_norm", "pre_ffn_norm", "w_gate",
          "w_up", "w_down", "post_ffn_norm")


def _cols_from_shards(g):
    return jnp.transpose(g, (1, 0, 2)).reshape(g.shape[1], N_DEV * g.shape[2])


def _cols_to_shards(w):
    k, n8 = w.shape
    return jnp.transpose(w.reshape(k, N_DEV, n8 // N_DEV), (1, 0, 2))


def _step(x, positions, loss_target, w, m, v):
    s, d = x.shape[1], x.shape[2]
    x2, tgt = x[0], loss_target[0]
    pos = positions.reshape(s, 1)
    vecs = {n: w[n] for n in _SMALL}
    core = lax.axis_index("c").astype(jnp.int32)
    my_chip = (2 * lax.axis_index("x") + lax.axis_index("y")).astype(jnp.int32)
    other_chips = [my_chip ^ 2, my_chip ^ 1, my_chip ^ 3]
    pair_slots = jnp.stack([2 * q + core for q in other_chips] + other_chips)
    own_slots = jnp.stack([2 * my_chip + core, my_chip])
    own = {}
    n_in_cols = N_DEV * w["w_in"].shape[2]
    gathers, scatters, to_sibling = {}, {}, []

    def shard(t, n):
        return t[n][0].T if n in _TRANSPOSED else t[n][0]

    def gather_start(names, tag, after, zero=0.0):
        srcs = [w[n][0] if n == "conv_w" else (shard(w, n) + zero).astype(BF16) for n in names]
        lands = [lax.empty((N_DEV,) + a.shape, a.dtype) for a in srcs]
        gathers[tag], zero = _split_start("gather_" + tag + "_start", _gather_copies, srcs, lands, after)
        return zero

    def gather_finish(names, tag, after):
        srcs, lands = _split_wait("gather_" + tag + "_wait", 4, gathers[tag], after)
        lands = _gather_pass_on(lands, "gather_" + tag + "_pass_on")
        me = _my_index()
        return {n: lax.dynamic_update_slice(g, a[None], (me,) + (0,) * a.ndim) for n, g, a in zip(names, lands, srcs)}

    def in_weights_fn(h):
        w_in_g = gather_finish(("w_in",), "in", h)["w_in"]
        return jnp.pad(w_in_g.reshape(-1, d), ((0, LANES - QK_ROPE), (0, 0))), 0.0

    def mix_weights_fn(z):
        gath = gather_finish(_MIX[1:], "mix", z)
        w_uq_t = jnp.pad(gath["w_uq"], ((0, 0), (0, HEAD_PAD - QK_HEAD), (0, 0))).reshape(N_HEADS * HEAD_PAD, -1)
        return (w_uq_t, _cols_from_shards(gath["w_ukv"]), _cols_from_shards(gath["conv_w"]),
                gath["w_out"].reshape(-1, d))

    def up_weights_fn(hf):
        gath = gather_finish(("w_gate", "w_up"), "up", hf)
        return gath["w_gate"], gath["w_up"]

    def down_weights_fn(act):
        return gather_finish(("w_down",), "down", act)["w_down"]

    def scatter_advance(after):
        if not to_sibling:
            return 0.0
        names, tag, sent = to_sibling.pop()
        blocks, theirs = _split_wait("to_sibling_" + tag + "_wait", 4, sent, after)
        own.update(zip(names, zip(blocks, theirs)))
        pairs = [_pair_sum(pair_slots, b, t, "pair_sum_" + n) for n, b, t in zip(names, blocks, theirs)]
        lands = [lax.empty((3,) + p.shape[1:], p.dtype) for p in pairs]
        scatters[tag], zero = _split_start("scatter_" + tag + "_start", _chip_copies, pairs, lands, theirs[0])
        return zero

    to_blocks = {
        "w_in": lambda a: a[:n_in_cols].reshape(N_DEV, -1, d),
        "w_uq": lambda a: a.reshape(N_HEADS, HEAD_PAD, -1)[:, :QK_HEAD],
        "w_ukv": _cols_to_shards, "conv_w": _cols_to_shards,
        "w_out": lambda a: a.reshape(N_DEV, -1, d),
    }

    def grads_fn(tag, names, grads, after):
        zero = scatter_advance(after)
        if not grads:
            return zero
        blocks = [to_blocks.get(n, lambda a: a)(a) for n, a in zip(names, grads)]
        lands = [lax.empty((4,) + b.shape[1:], b.dtype) for b in blocks]
        sent, zero2 = _split_start("to_sibling_" + tag + "_start", _sibling_copies, blocks, lands, after)
        to_sibling.append((names, tag, sent))
        return zero + zero2

    zero = gather_start(("w_in",), "in", x2)
    zero = gather_start(_MIX[1:], "mix", x2, zero)
    zero = gather_start(("w_gate", "w_up"), "up", x2, zero)
    vecs["pre_mix_norm"] = vecs["pre_mix_norm"] + gather_start(("w_down",), "down", x2, zero)
    totals = []

    def loss_fn(loss):
        totals.append(lax.psum(loss[0, 0], ("x", "y", "c")))
        return jnp.minimum(jnp.abs(totals[0]), 0.0)

    _, grad_x, g = _local_step(x2, pos, tgt, vecs, in_weights_fn, mix_weights_fn, up_weights_fn, down_weights_fn,
                               loss_fn, grads_fn)
    total = totals[0]

    small = jnp.concatenate([g[n] for n in _SMALL], axis=1)
    small_started, zero = _split_start("gather_small_start", _small_copies, [small],
                                       [lax.empty((N_DEV,) + small.shape, F32)], grad_x)

    res = {}
    after = grad_x
    own_slots = own_slots + zero.astype(jnp.int32)
    for tag, names in (("down", ("w_down",)), ("up", ("w_gate", "w_up")), ("out", ("w_out",)),
                       ("in", ("w_in", "w_uq", "w_ukv", "conv_w"))):
        _, recv = _split_wait("scatter_" + tag + "_wait", 3, scatters[tag], after)
        for n, parts in zip(names, recv):
            res[n] = _reduce_adamw(parts, shard(w, n), shard(m, n), shard(v, n), "adamw_" + n, *own[n], own_slots)
            after = res[n][1]
            res[n] = [(t.T if n in _TRANSPOSED else t)[None] for t in res[n]]
    (small,), (small_all,) = _split_wait("gather_small_wait", N_DEV - 1, small_started, after)
    small_all = lax.dynamic_update_slice(small_all, small[None], (_my_index(), 0, 0))
    small_res = _adamw_small(small_all, *[[t[n] for n in _SMALL] for t in (w, m, v)])
    for k, n in enumerate(_SMALL):
        res[n] = [part[k] for part in small_res]

    outs = [total, grad_x[None]]
    for part in range(4):
        outs.extend(res[n][part] for n in _ORDER)
    return tuple(outs)


def kernel(x, positions, pre_mix_norm, w_in, q_norm, w_uq, kv_norm, w_ukv, conv_w, conv_b, conv_ln_g, conv_ln_b, conv_out_norm, attn_out_norm, w_out, post_mix_norm, pre_ffn_norm, w_gate, w_up, w_down, post_ffn_norm, loss_target, m_pre_mix_norm, m_w_in, m_q_norm, m_w_uq, m_kv_norm, m_w_ukv, m_conv_w, m_conv_b, m_conv_ln_g, m_conv_ln_b, m_conv_out_norm, m_attn_out_norm, m_w_out, m_post_mix_norm, m_pre_ffn_norm, m_w_gate, m_w_up, m_w_down, m_post_ffn_norm, v_pre_mix_norm, v_w_in, v_q_norm, v_w_uq, v_kv_norm, v_w_ukv, v_conv_w, v_conv_b, v_conv_ln_g, v_conv_ln_b, v_conv_out_norm, v_attn_out_norm, v_w_out, v_post_mix_norm, v_pre_ffn_norm, v_w_gate, v_w_up, v_w_down, v_post_ffn_norm):
    w = dict(zip(_ORDER, (pre_mix_norm, w_in, q_norm, w_uq, kv_norm, w_ukv, conv_w, conv_b, conv_ln_g, conv_ln_b,
                          conv_out_norm, attn_out_norm, w_out, post_mix_norm, pre_ffn_norm, w_gate, w_up, w_down,
                          post_ffn_norm)))
    m = dict(zip(_ORDER, (m_pre_mix_norm, m_w_in, m_q_norm, m_w_uq, m_kv_norm, m_w_ukv, m_conv_w, m_conv_b,
                          m_conv_ln_g, m_conv_ln_b, m_conv_out_norm, m_attn_out_norm, m_w_out, m_post_mix_norm,
                          m_pre_ffn_norm, m_w_gate, m_w_up, m_w_down, m_post_ffn_norm)))
    v = dict(zip(_ORDER, (v_pre_mix_norm, v_w_in, v_q_norm, v_w_uq, v_kv_norm, v_w_ukv, v_conv_w, v_conv_b,
                          v_conv_ln_g, v_conv_ln_b, v_conv_out_norm, v_attn_out_norm, v_w_out, v_post_mix_norm,
                          v_pre_ffn_norm, v_w_gate, v_w_up, v_w_down, v_post_ffn_norm)))
    return _step(x, positions, loss_target, w, m, v)
```

```python
import functools

import jax
import jax.numpy as jnp
from jax import lax
from jax.experimental import pallas as pl
from jax.experimental.pallas import tpu as pltpu

N_DEV = 8
N_HEADS = 8
QK_NOPE = 128
QK_ROPE = 64
V_HEAD = 128
QK_HEAD = QK_NOPE + QK_ROPE
HEAD_PAD = 256
LANES = 128
ATTN_BLOCK = 512
CONV_K = 31
CONV_PAD = 32
EPS = 1e-6
ROPE_THETA = 10000.0
ADAM_LR = 0.001
ADAM_B1 = 0.9
ADAM_B2 = 0.999
ADAM_EPS = 1e-08
ADAM_WD = 0.01
ADAM_STEP = 10
VMEM_LIMIT = 56 * 1024 * 1024
F32 = jnp.float32
BF16 = jnp.bfloat16
MESH = pl.DeviceIdType.MESH
NEG = -1e30


def _pick(n, prefs):
    for p in prefs:
        if p <= n and n % p == 0:
            return p
    return n


def _params(sem):
    return pltpu.CompilerParams(dimension_semantics=sem, vmem_limit_bytes=VMEM_LIMIT)


_DIMS = {"nn": (((1,), (0,)), ((), ())), "nt": (((1,), (1,)), ((), ())), "tn": (((0,), (0,)), ((), ()))}


MM_VMEM_BUDGET = 40 * 1024 * 1024
MM_MAX_MACS = 3 * 1024 ** 3


V7X_HBM_BYTES_PER_S = 3.0e12
V7X_MXU_MACS_PER_S = 0.45e15
GRID_STEP_S = 0.35e-6


def _mm_tiles(m, n, k, size_a, size_b, size_o):
    best = None
    for tm in sorted({m, 1024, 512, 256, 128}, reverse=True):
        if tm > m or m % tm:
            continue
        for tn in sorted({n, 2048, 1024, 512, 384, 256, 128}, reverse=True):
            if tn > n or n % tn:
                continue
            vmem = 2 * (tm * k * size_a + k * tn * size_b + tm * tn * size_o)
            if vmem > MM_VMEM_BUDGET or tm * tn * k > MM_MAX_MACS:
                continue
            b_reads = 1 if tn == n else m // tm
            traffic = m * k * size_a + b_reads * k * n * size_b + m * n * size_o
            exposed = tm * k * size_a + k * tn * size_b + tm * tn * size_o
            steps = (m // tm) * (n // tn)
            key = (max(traffic / V7X_HBM_BYTES_PER_S, m * n * k / V7X_MXU_MACS_PER_S)
                   + exposed / V7X_HBM_BYTES_PER_S + steps * GRID_STEP_S)
            if best is None or key < best[0]:
                best = (key, tm, tn)
    assert best is not None, (m, n, k)
    return best[1], best[2]


def _mm(a, b, mode, name, out_dtype=F32, behind=None):
    if mode == "nn":
        (m, k), (k2, n) = a.shape, b.shape
    elif mode == "nt":
        (m, k), (n, k2) = a.shape, b.shape
    else:
        (k, m), (k2, n) = a.shape, b.shape
    assert k == k2, (a.shape, b.shape, mode)
    tm, tn = _mm_tiles(m, n, k, a.dtype.itemsize, b.dtype.itemsize, jnp.dtype(out_dtype).itemsize)
    dims = _DIMS[mode]

    def body(a_ref, b_ref, *rest):
        acc = lax.dot_general(a_ref[...].astype(BF16), b_ref[...].astype(BF16), dims, preferred_element_type=F32)
        rest[-1][...] = acc.astype(rest[-1].dtype)

    if mode == "tn":
        a_spec = pl.BlockSpec((k, tm), lambda i, j: (0, i))
    else:
        a_spec = pl.BlockSpec((tm, k), lambda i, j: (i, 0))
    if mode == "nt":
        b_spec = pl.BlockSpec((tn, k), lambda i, j: (j, 0))
    else:
        b_spec = pl.BlockSpec((k, tn), lambda i, j: (0, j))
    o_spec = pl.BlockSpec((tm, tn), lambda i, j: (i, j))
    extra = [] if behind is None else [behind]
    return pl.pallas_call(
        body, name=name,
        grid=(m // tm, n // tn),
        in_specs=[a_spec, b_spec] + [pl.BlockSpec((8, LANES), lambda i, j: (0, 0))] * len(extra),
        out_specs=o_spec,
        out_shape=jax.ShapeDtypeStruct((m, n), out_dtype),
        compiler_params=_params(("parallel", "parallel")),
    )(a, b, *extra)


def _sigmoid(x):
    return 1.0 / (1.0 + jnp.exp(-x))


def _rms(x, g):
    r = lax.rsqrt(jnp.mean(x * x, axis=-1, keepdims=True) + EPS)
    return (x * r) * g


def _rms_bwd(x, g, dy):
    r = lax.rsqrt(jnp.mean(x * x, axis=-1, keepdims=True) + EPS)
    xh = x * r
    dyg = dy * g
    dx = r * (dyg - xh * jnp.mean(dyg * xh, axis=-1, keepdims=True))
    return dx, jnp.sum(dy * xh, axis=0, keepdims=True)


def _ln(x, g, b):
    mu = jnp.mean(x, axis=-1, keepdims=True)
    xc = x - mu
    rs = lax.rsqrt(jnp.mean(xc * xc, axis=-1, keepdims=True) + EPS)
    return (xc * rs) * g + b


def _ln_bwd(x, g, dy):
    mu = jnp.mean(x, axis=-1, keepdims=True)
    xc = x - mu
    rs = lax.rsqrt(jnp.mean(xc * xc, axis=-1, keepdims=True) + EPS)
    xh = xc * rs
    dyg = dy * g
    dx = rs * (dyg - jnp.mean(dyg, axis=-1, keepdims=True) - xh * jnp.mean(dyg * xh, axis=-1, keepdims=True))
    return dx, jnp.sum(dy * xh, axis=0, keepdims=True), jnp.sum(dy, axis=0, keepdims=True)


def _silu(x):
    return x * _sigmoid(x)


def _silu_grad(x):
    s = _sigmoid(x)
    return s * (1.0 + x * (1.0 - s))


def _rope(x, cos, sa, sb):
    return x * cos + pltpu.roll(x, 96, 1) * sa + pltpu.roll(x, 32, 1) * sb


def _rope_t(d, cos, sa, sb):
    return d * cos - pltpu.roll(d, 96, 1) * sa - pltpu.roll(d, 32, 1) * sb


def _rows(ts, w):
    return pl.BlockSpec((ts, w), lambda i: (i, 0))


def _vec(w):
    return pl.BlockSpec((1, w), lambda i: (0, 0))


def _acc_init(i, *refs):
    @pl.when(i == 0)
    def _():
        for r in refs:
            r[...] = jnp.zeros_like(r)


def _rope_tables(pos, inv_freq):
    s = pos.shape[0]
    ts = _pick(s, (512, 256, 128))

    def body(p_ref, f_ref, c_ref, sa_ref, sb_ref):
        ang = p_ref[...].astype(F32) * f_ref[...]
        lane = lax.broadcasted_iota(jnp.int32, ang.shape, 1)
        c, sn = jnp.cos(ang), jnp.sin(ang)
        c_ref[...] = jnp.where(lane < QK_ROPE, c, 0.0)
        sa_ref[...] = jnp.where(lane < QK_ROPE // 2, -sn, 0.0)
        sb_ref[...] = jnp.where((lane >= QK_ROPE // 2) & (lane < QK_ROPE), sn, 0.0)

    out = jax.ShapeDtypeStruct((s, LANES), F32)
    return pl.pallas_call(
        body, name="rope_tables", grid=(s // ts,),
        in_specs=[_rows(ts, 1), _vec(LANES)],
        out_specs=[_rows(ts, LANES)] * 3, out_shape=[out] * 3,
        compiler_params=_params(("parallel",)),
    )(pos, inv_freq)


def _pre_fwd(x, g):
    s, d = x.shape
    ts = _pick(s, (256, 128))

    def body(x_ref, g_ref, h_ref):
        h_ref[...] = _rms(x_ref[...], g_ref[...]).astype(BF16)

    return pl.pallas_call(
        body, name="pre_fwd", grid=(s // ts,),
        in_specs=[_rows(ts, d), _vec(d)], out_specs=_rows(ts, d),
        out_shape=jax.ShapeDtypeStruct((s, d), BF16),
        compiler_params=_params(("parallel",)),
    )(x, g)


def _split_fwd(z, gq, gkv, tabs, c, ql, kvl):
    s, zw = z.shape
    ts = _pick(s, (256, 128))
    o_q, o_kv, o_kr = 2 * c, 2 * c + ql, 2 * c + ql + kvl

    def body(z_ref, gq_ref, gkv_ref, c_ref, sa_ref, sb_ref, u0_ref, qn_ref, kvn_ref, kpe_ref):
        u0_ref[...] = z_ref[:, 0:c] * _sigmoid(z_ref[:, c:2 * c])
        qn_ref[...] = _rms(z_ref[:, o_q:o_kv], gq_ref[...]).astype(BF16)
        kvn_ref[...] = _rms(z_ref[:, o_kv:o_kr], gkv_ref[...]).astype(BF16)
        kpe_ref[...] = _rope(z_ref[:, o_kr:o_kr + LANES], c_ref[...], sa_ref[...], sb_ref[...]).astype(BF16)

    return pl.pallas_call(
        body, name="split_fwd", grid=(s // ts,),
        in_specs=[_rows(ts, zw), _vec(ql), _vec(kvl)] + [_rows(ts, LANES)] * 3,
        out_specs=[_rows(ts, c), _rows(ts, ql), _rows(ts, kvl), _rows(ts, LANES)],
        out_shape=[jax.ShapeDtypeStruct((s, c), F32), jax.ShapeDtypeStruct((s, ql), BF16),
                   jax.ShapeDtypeStruct((s, kvl), BF16), jax.ShapeDtypeStruct((s, LANES), BF16)],
        compiler_params=_params(("parallel",)),
    )(z, gq, gkv, *tabs)


def _split_bwd(du0, z, dqn, dkvn, dkpe_h, gq, gkv, tabs, c, ql, kvl):
    s, zw = z.shape
    ts = _pick(s, (256, 128))
    o_q, o_kv, o_kr = 2 * c, 2 * c + ql, 2 * c + ql + kvl

    def body(du0_ref, z_ref, dqn_ref, dkvn_ref, dkh_ref, gq_ref, gkv_ref, c_ref, sa_ref, sb_ref,
             dz_ref, dgq_ref, dgkv_ref):
        _acc_init(pl.program_id(0), dgq_ref, dgkv_ref)
        du0 = du0_ref[...]
        a = z_ref[:, 0:c]
        sg = _sigmoid(z_ref[:, c:2 * c])
        dz_ref[:, 0:c] = (du0 * sg).astype(BF16)
        dz_ref[:, c:2 * c] = (du0 * a * sg * (1.0 - sg)).astype(BF16)
        dq, dgq = _rms_bwd(z_ref[:, o_q:o_kv], gq_ref[...], dqn_ref[...])
        dz_ref[:, o_q:o_kv] = dq.astype(BF16)
        dgq_ref[...] += dgq
        dkv, dgkv = _rms_bwd(z_ref[:, o_kv:o_kr], gkv_ref[...], dkvn_ref[...])
        dz_ref[:, o_kv:o_kr] = dkv.astype(BF16)
        dgkv_ref[...] += dgkv
        dk = dkh_ref[:, 0:LANES]
        for h in range(1, N_HEADS):
            dk = dk + dkh_ref[:, h * LANES:(h + 1) * LANES]
        dz_ref[:, o_kr:o_kr + LANES] = _rope_t(dk, c_ref[...], sa_ref[...], sb_ref[...]).astype(BF16)

    return pl.pallas_call(
        body, name="split_bwd", grid=(s // ts,),
        in_specs=[_rows(ts, c), _rows(ts, zw), _rows(ts, ql), _rows(ts, kvl), _rows(ts, N_HEADS * LANES),
                  _vec(ql), _vec(kvl)] + [_rows(ts, LANES)] * 3,
        out_specs=[_rows(ts, zw), _vec(ql), _vec(kvl)],
        out_shape=[jax.ShapeDtypeStruct((s, zw), BF16), jax.ShapeDtypeStruct((1, ql), F32),
                   jax.ShapeDtypeStruct((1, kvl), F32)],
        compiler_params=_params(("arbitrary",)),
    )(du0, z, dqn, dkvn, dkpe_h, gq, gkv, *tabs)


def _q_rope(qpre, tabs):
    s, w = qpre.shape
    ts = _pick(s, (256, 128))

    def body(q_ref, c_ref, sa_ref, sb_ref, o_ref):
        cs, sa, sb = c_ref[...], sa_ref[...], sb_ref[...]
        for h in range(N_HEADS):
            lo = h * HEAD_PAD
            o_ref[:, lo:lo + QK_NOPE] = q_ref[:, lo:lo + QK_NOPE].astype(BF16)
            o_ref[:, lo + QK_NOPE:lo + HEAD_PAD] = _rope(q_ref[:, lo + QK_NOPE:lo + HEAD_PAD], cs, sa, sb).astype(BF16)

    return pl.pallas_call(
        body, name="q_rope", grid=(s // ts,),
        in_specs=[_rows(ts, w)] + [_rows(ts, LANES)] * 3, out_specs=_rows(ts, w),
        out_shape=jax.ShapeDtypeStruct((s, w), BF16),
        compiler_params=_params(("parallel",)),
    )(qpre, *tabs)


def _conv_fwd(u0, w, b):
    s, c = u0.shape
    tc = LANES
    rc = _pick(s, (256, 128))

    def body(u_ref, w_ref, b_ref, o_ref, pad_ref):
        pad_ref[0:CONV_PAD, :] = jnp.zeros((CONV_PAD, tc), F32)
        pad_ref[CONV_PAD:CONV_PAD + s, :] = u_ref[...]
        for r in range(s // rc):
            acc = jnp.broadcast_to(b_ref[...], (rc, tc))
            for k in range(CONV_K):
                lo = r * rc + CONV_PAD - (CONV_K - 1) + k
                acc = acc + w_ref[k:k + 1, :] * pad_ref[lo:lo + rc, :]
            o_ref[r * rc:(r + 1) * rc, :] = acc

    col = lambda j: (0, j)
    return pl.pallas_call(
        body, name="conv_fwd", grid=(c // tc,),
        in_specs=[pl.BlockSpec((s, tc), col), pl.BlockSpec((CONV_K, tc), col), pl.BlockSpec((1, tc), col)],
        out_specs=pl.BlockSpec((s, tc), col),
        out_shape=jax.ShapeDtypeStruct((s, c), F32),
        scratch_shapes=[pltpu.VMEM((s + CONV_PAD, tc), F32)],
        compiler_params=_params(("parallel",)),
    )(u0, w, b)


def _conv_bwd(du1, u0, w):
    s, c = u0.shape
    tc = LANES
    rc = _pick(s, (256, 128))

    def body(d_ref, u_ref, w_ref, du_ref, dw_ref, db_ref, upad_ref, dpad_ref):
        upad_ref[0:CONV_PAD, :] = jnp.zeros((CONV_PAD, tc), F32)
        upad_ref[CONV_PAD:CONV_PAD + s, :] = u_ref[...]
        dpad_ref[0:s, :] = d_ref[...]
        dpad_ref[s:s + CONV_PAD, :] = jnp.zeros((CONV_PAD, tc), F32)
        for r in range(s // rc):
            acc = jnp.zeros((rc, tc), F32)
            for k in range(CONV_K):
                lo = r * rc + (CONV_K - 1) - k
                acc = acc + w_ref[k:k + 1, :] * dpad_ref[lo:lo + rc, :]
            du_ref[r * rc:(r + 1) * rc, :] = acc
        for k in range(CONV_K):
            acc8 = jnp.zeros((8, tc), F32)
            for r in range(s // rc):
                lo = r * rc + CONV_PAD - (CONV_K - 1) + k
                prod = d_ref[r * rc:(r + 1) * rc, :] * upad_ref[lo:lo + rc, :]
                acc8 = acc8 + jnp.sum(prod.reshape(rc // 8, 8, tc), axis=0)
            dw_ref[k:k + 1, :] = jnp.sum(acc8, axis=0, keepdims=True)
        db_ref[...] = jnp.sum(d_ref[...], axis=0, keepdims=True)

    col = lambda j: (0, j)
    return pl.pallas_call(
        body, name="conv_bwd", grid=(c // tc,),
        in_specs=[pl.BlockSpec((s, tc), col), pl.BlockSpec((s, tc), col), pl.BlockSpec((CONV_K, tc), col)],
        out_specs=[pl.BlockSpec((s, tc), col), pl.BlockSpec((CONV_K, tc), col), pl.BlockSpec((1, tc), col)],
        out_shape=[jax.ShapeDtypeStruct((s, c), F32), jax.ShapeDtypeStruct((CONV_K, c), F32),
                   jax.ShapeDtypeStruct((1, c), F32)],
        scratch_shapes=[pltpu.VMEM((s + CONV_PAD, tc), F32), pltpu.VMEM((s + CONV_PAD, tc), F32)],
        compiler_params=_params(("parallel",)),
    )(du1, u0, w)


def _causal_mask(sc, qi, kj, tq, tk):
    rows = qi * tq + lax.broadcasted_iota(jnp.int32, sc.shape, 0)
    cols = kj * tk + lax.broadcasted_iota(jnp.int32, sc.shape, 1)
    return jnp.where(cols <= rows, sc, NEG)


def _attn_fwd(q, kv, kpe):
    s = q.shape[0]
    tq = tk = _pick(s, (ATTN_BLOCK, 256, 128))
    reps = tk // LANES
    scale = QK_HEAD ** -0.5
    nt = (((1,), (1,)), ((), ()))

    def body(q_ref, kn_ref, v_ref, kpe_ref, o_ref, lse_ref, kf_ref, vb_ref, m_ref, l_ref, acc_ref):
        i = pl.program_id(1)

        @pl.when(i == 0)
        def _():
            kf_ref[:, 0:QK_NOPE] = kn_ref[...].astype(BF16)
            kf_ref[:, QK_NOPE:HEAD_PAD] = kpe_ref[...]
            vb_ref[...] = v_ref[...].astype(BF16)

        qb = q_ref[...]
        m_ref[...] = jnp.full((tq, LANES), NEG, F32)
        l_ref[...] = jnp.zeros((tq, LANES), F32)
        acc_ref[...] = jnp.zeros((tq, V_HEAD), F32)

        def block(j, diagonal):
            off = j * tk
            sc = lax.dot_general(qb, kf_ref[pl.ds(off, tk), :], nt, preferred_element_type=F32) * scale
            if diagonal:
                sc = _causal_mask(sc, 0, 0, tq, tk)
            m_prev = m_ref[...]
            m_new = jnp.maximum(m_prev, jnp.max(sc, axis=1, keepdims=True))
            p = jnp.exp(sc - jnp.tile(m_new, (1, reps)))
            alpha = jnp.exp(m_prev - m_new)
            l_ref[...] = alpha * l_ref[...] + jnp.sum(p, axis=1, keepdims=True)
            acc_ref[...] = alpha * acc_ref[...] + jnp.dot(p.astype(BF16), vb_ref[pl.ds(off, tk), :],
                                                          preferred_element_type=F32)
            m_ref[...] = m_new

        for qi in range(s // tq):
            @pl.when(i == qi)
            def _(qi=qi):
                for j in range(qi):
                    block(j, False)
                block(qi, True)

        o_ref[...] = acc_ref[...] / l_ref[...]
        lse_ref[...] = m_ref[...] + jnp.log(l_ref[...])

    return pl.pallas_call(
        body, name="attn_fwd", grid=(N_HEADS, s // tq),
        in_specs=[pl.BlockSpec((tq, HEAD_PAD), lambda h, i: (i, h)),
                  pl.BlockSpec((s, QK_NOPE), lambda h, i: (0, 2 * h)),
                  pl.BlockSpec((s, V_HEAD), lambda h, i: (0, 2 * h + 1)),
                  pl.BlockSpec((s, LANES), lambda h, i: (0, 0))],
        out_specs=[pl.BlockSpec((tq, V_HEAD), lambda h, i: (i, h)),
                   pl.BlockSpec((tq, LANES), lambda h, i: (i, h))],
        out_shape=[jax.ShapeDtypeStruct((s, N_HEADS * V_HEAD), F32),
                   jax.ShapeDtypeStruct((s, N_HEADS * LANES), F32)],
        scratch_shapes=[pltpu.VMEM((s, HEAD_PAD), BF16), pltpu.VMEM((s, V_HEAD), BF16),
                        pltpu.VMEM((tq, LANES), F32), pltpu.VMEM((tq, LANES), F32), pltpu.VMEM((tq, V_HEAD), F32)],
        compiler_params=_params(("parallel", "arbitrary")),
    )(q, kv, kv, kpe)


def _attn_bwd(q, kv, kpe, o, do, lse, tabs):
    s = q.shape[0]
    tq = tk = _pick(s, (ATTN_BLOCK, 256, 128))
    nq = s // tq
    reps = tk // LANES
    scale = QK_HEAD ** -0.5
    nt = (((1,), (1,)), ((), ()))
    tn = (((0,), (0,)), ((), ()))

    def body(q_ref, kn_ref, v_ref, kpe_ref, o_ref, do_ref, lse_ref, c_ref, sa_ref, sb_ref, dqpre_ref, dkv_ref,
             dkpe_ref, kf_ref, vb_ref, dk_ref, dv_ref, dq_ref):
        j = pl.program_id(1)

        @pl.when(j == 0)
        def _():
            dq_ref[...] = jnp.zeros_like(dq_ref)

        kf_ref[:, 0:QK_NOPE] = kn_ref[...].astype(BF16)
        kf_ref[:, QK_NOPE:HEAD_PAD] = kpe_ref[...]
        vb_ref[...] = v_ref[...].astype(BF16)
        dk_ref[...] = jnp.zeros_like(dk_ref)
        dv_ref[...] = jnp.zeros_like(dv_ref)

        def block(i, diagonal):
            off = i * tq
            qb = q_ref[pl.ds(off, tq), :]
            dob = do_ref[pl.ds(off, tq), :]
            delta = jnp.sum(dob * o_ref[pl.ds(off, tq), :], axis=1, keepdims=True)
            sc = lax.dot_general(qb, kf_ref[...], nt, preferred_element_type=F32) * scale
            if diagonal:
                sc = _causal_mask(sc, 0, 0, tq, tk)
            p = jnp.exp(sc - jnp.tile(lse_ref[pl.ds(off, tq), :], (1, reps)))
            dob16 = dob.astype(BF16)
            dv_ref[...] += lax.dot_general(p.astype(BF16), dob16, tn, preferred_element_type=F32)
            dp = lax.dot_general(dob16, vb_ref[...], nt, preferred_element_type=F32)
            ds = (p * (dp - delta) * scale).astype(BF16)
            dq_ref[pl.ds(off, tq), :] += jnp.dot(ds, kf_ref[...], preferred_element_type=F32)
            dk_ref[...] += lax.dot_general(ds, qb, tn, preferred_element_type=F32)

        for kj in range(nq):
            @pl.when(j == kj)
            def _(kj=kj):
                block(kj, True)
                for i in range(kj + 1, nq):
                    block(i, False)

        dkv_ref[:, 0:QK_NOPE] = dk_ref[:, 0:QK_NOPE]
        dkv_ref[:, QK_NOPE:HEAD_PAD] = dv_ref[...]
        dkpe_ref[...] = dk_ref[:, QK_NOPE:HEAD_PAD]

        @pl.when(j == nq - 1)
        def _():
            dqpre_ref[:, 0:QK_NOPE] = dq_ref[:, 0:QK_NOPE].astype(BF16)
            dqpre_ref[:, QK_NOPE:HEAD_PAD] = _rope_t(dq_ref[:, QK_NOPE:HEAD_PAD], c_ref[...], sa_ref[...],
                                                     sb_ref[...]).astype(BF16)

    head_rows = lambda w: pl.BlockSpec((s, w), lambda h, j: (0, h))
    table = pl.BlockSpec((s, LANES), lambda h, j: (0, 0))
    return pl.pallas_call(
        body, name="attn_bwd", grid=(N_HEADS, s // tk),
        in_specs=[head_rows(HEAD_PAD),
                  pl.BlockSpec((tk, QK_NOPE), lambda h, j: (j, 2 * h)),
                  pl.BlockSpec((tk, V_HEAD), lambda h, j: (j, 2 * h + 1)),
                  pl.BlockSpec((tk, LANES), lambda h, j: (j, 0)),
                  head_rows(V_HEAD), head_rows(V_HEAD), head_rows(LANES), table, table, table],
        out_specs=[head_rows(HEAD_PAD),
                   pl.BlockSpec((tk, HEAD_PAD), lambda h, j: (j, h)),
                   pl.BlockSpec((tk, LANES), lambda h, j: (j, h))],
        out_shape=[jax.ShapeDtypeStruct((s, N_HEADS * HEAD_PAD), BF16),
                   jax.ShapeDtypeStruct((s, N_HEADS * HEAD_PAD), F32),
                   jax.ShapeDtypeStruct((s, N_HEADS * LANES), F32)],
        scratch_shapes=[pltpu.VMEM((tk, HEAD_PAD), BF16), pltpu.VMEM((tk, V_HEAD), BF16),
                        pltpu.VMEM((tk, HEAD_PAD), F32), pltpu.VMEM((tk, V_HEAD), F32),
                        pltpu.VMEM((s, HEAD_PAD), F32)],
        compiler_params=_params(("parallel", "arbitrary")),
    )(q, kv, kv, kpe, o, do, lse, *tabs)


def _mix_fwd(u1, lng, lnb, gcon, attn, gattn):
    s, c = u1.shape
    ac = attn.shape[1]
    ts = _pick(s, (256, 128))

    def body(u_ref, lg_ref, lb_ref, gc_ref, a_ref, ga_ref, o_ref):
        t3 = _silu(_ln(u_ref[...], lg_ref[...], lb_ref[...]))
        o_ref[:, 0:c] = _rms(t3, gc_ref[...]).astype(BF16)
        o_ref[:, c:c + ac] = _rms(a_ref[...], ga_ref[...]).astype(BF16)

    return pl.pallas_call(
        body, name="mix_fwd", grid=(s // ts,),
        in_specs=[_rows(ts, c), _vec(c), _vec(c), _vec(c), _rows(ts, ac), _vec(ac)],
        out_specs=_rows(ts, c + ac),
        out_shape=jax.ShapeDtypeStruct((s, c + ac), BF16),
        compiler_params=_params(("parallel",)),
    )(u1, lng, lnb, gcon, attn, gattn)


def _mix_bwd(dmixin, u1, lng, lnb, gcon, attn, gattn):
    s, c = u1.shape
    ac = attn.shape[1]
    ts = _pick(s, (256, 128))

    def body(d_ref, u_ref, lg_ref, lb_ref, gc_ref, a_ref, ga_ref,
             du_ref, da_ref, dlg_ref, dlb_ref, dgc_ref, dga_ref):
        _acc_init(pl.program_id(0), dlg_ref, dlb_ref, dgc_ref, dga_ref)
        u = u_ref[...]
        t2 = _ln(u, lg_ref[...], lb_ref[...])
        dt3, dgc = _rms_bwd(_silu(t2), gc_ref[...], d_ref[:, 0:c])
        du, dlg, dlb = _ln_bwd(u, lg_ref[...], dt3 * _silu_grad(t2))
        du_ref[...] = du
        dlg_ref[...] += dlg
        dlb_ref[...] += dlb
        dgc_ref[...] += dgc
        da, dga = _rms_bwd(a_ref[...], ga_ref[...], d_ref[:, c:c + ac])
        da_ref[...] = da
        dga_ref[...] += dga

    return pl.pallas_call(
        body, name="mix_bwd", grid=(s // ts,),
        in_specs=[_rows(ts, c + ac), _rows(ts, c), _vec(c), _vec(c), _vec(c), _rows(ts, ac), _vec(ac)],
        out_specs=[_rows(ts, c), _rows(ts, ac), _vec(c), _vec(c), _vec(c), _vec(ac)],
        out_shape=[jax.ShapeDtypeStruct((s, c), F32), jax.ShapeDtypeStruct((s, ac), F32),
                   jax.ShapeDtypeStruct((1, c), F32), jax.ShapeDtypeStruct((1, c), F32),
                   jax.ShapeDtypeStruct((1, c), F32), jax.ShapeDtypeStruct((1, ac), F32)],
        compiler_params=_params(("arbitrary",)),
    )(dmixin, u1, lng, lnb, gcon, attn, gattn)


def _post_mix_fwd(x, mix, gpost, gpre):
    s, d = x.shape
    ts = _pick(s, (256, 128))

    def body(x_ref, m_ref, gp_ref, gf_ref, x1_ref, hf_ref):
        x1 = x_ref[...] + _rms(m_ref[...], gp_ref[...])
        x1_ref[...] = x1
        hf_ref[...] = _rms(x1, gf_ref[...]).astype(BF16)

    return pl.pallas_call(
        body, name="post_mix_fwd", grid=(s // ts,),
        in_specs=[_rows(ts, d), _rows(ts, d), _vec(d), _vec(d)],
        out_specs=[_rows(ts, d), _rows(ts, d)],
        out_shape=[jax.ShapeDtypeStruct((s, d), F32), jax.ShapeDtypeStruct((s, d), BF16)],
        compiler_params=_params(("parallel",)),
    )(x, mix, gpost, gpre)


def _post_mix_bwd(dy, dhf, x1, gpre, mix, gpost):
    s, d = x1.shape
    ts = _pick(s, (256, 128))

    def body(dy_ref, dh_ref, x1_ref, gf_ref, m_ref, gp_ref, dx1_ref, dm_ref, dgf_ref, dgp_ref):
        _acc_init(pl.program_id(0), dgf_ref, dgp_ref)
        dxa, dgf = _rms_bwd(x1_ref[...], gf_ref[...], dh_ref[...])
        dx1 = dy_ref[...] + dxa
        dx1_ref[...] = dx1
        dgf_ref[...] += dgf
        dm, dgp = _rms_bwd(m_ref[...], gp_ref[...], dx1)
        dm_ref[...] = dm.astype(BF16)
        dgp_ref[...] += dgp

    return pl.pallas_call(
        body, name="post_mix_bwd", grid=(s // ts,),
        in_specs=[_rows(ts, d), _rows(ts, d), _rows(ts, d), _vec(d), _rows(ts, d), _vec(d)],
        out_specs=[_rows(ts, d), _rows(ts, d), _vec(d), _vec(d)],
        out_shape=[jax.ShapeDtypeStruct((s, d), F32), jax.ShapeDtypeStruct((s, d), BF16),
                   jax.ShapeDtypeStruct((1, d), F32), jax.ShapeDtypeStruct((1, d), F32)],
        compiler_params=_params(("arbitrary",)),
    )(dy, dhf, x1, gpre, mix, gpost)


def _ffn_up(hf, wg, wu, slabs, name, earlier=None):
    s, d = hf.shape
    nsh, fs, _ = wg.shape
    tm = _pick(s, (1024, 512, 256, 128))
    nt = (((1,), (1,)), ((), ()))

    def body(slab_ref, h_ref, wg_ref, wu_ref, *rest):
        dg_ref, du_ref, a_ref = rest[-3:]
        h = h_ref[...]
        g = lax.dot_general(h, wg_ref[...], nt, preferred_element_type=F32)
        u = lax.dot_general(h, wu_ref[...], nt, preferred_element_type=F32)
        sg = _sigmoid(g)
        silu = g * sg
        dg_ref[...] = u * (sg * (1.0 + g * (1.0 - sg)))
        du_ref[...] = silu
        a_ref[...] = (silu * u).astype(BF16)

    w_spec = pl.BlockSpec((None, fs, d), lambda i, j, slab_ref: (slab_ref[j], 0, 0))
    o_spec = pl.BlockSpec((None, tm, fs), lambda i, j, slab_ref: (slab_ref[j], i, 0))
    earlier = [] if earlier is None else list(earlier)
    return pl.pallas_call(
        body, name=name,
        grid_spec=pltpu.PrefetchScalarGridSpec(
            num_scalar_prefetch=1, grid=(s // tm, slabs.shape[0]),
            in_specs=[pl.BlockSpec((tm, d), lambda i, j, slab_ref: (i, 0)), w_spec, w_spec]
            + [pl.BlockSpec(memory_space=pl.ANY)] * len(earlier),
            out_specs=[o_spec] * 3),
        out_shape=[jax.ShapeDtypeStruct((nsh, s, fs), F32)] * 2 + [jax.ShapeDtypeStruct((nsh, s, fs), BF16)],
        input_output_aliases={4 + k: k for k in range(len(earlier))},
        compiler_params=_params(("parallel", "parallel")),
    )(slabs, hf, wg, wu, *earlier)


def _ffn_down(acts, ws, name):
    n = len(acts)
    nsh, s, fs = acts[0].shape
    d = ws[0].shape[2]
    tm = _pick(s, (1024, 512, 256, 128))
    tn = _pick(d, (256, 128))
    a_mode = pl.Buffered(1)

    def body(*refs):
        acc = None
        for a_ref, w_ref in zip(refs[:n], refs[n:2 * n]):
            for j in range(nsh):
                part = jnp.dot(a_ref[j], w_ref[j], preferred_element_type=F32)
                acc = part if acc is None else acc + part
        refs[-1][...] = acc

    return pl.pallas_call(
        body, name=name, grid=(s // tm, d // tn),
        in_specs=[pl.BlockSpec((nsh, tm, fs), lambda i, j: (0, i, 0), pipeline_mode=a_mode)] * n
        + [pl.BlockSpec((nsh, fs, tn), lambda i, j: (0, 0, j))] * n,
        out_specs=pl.BlockSpec((tm, tn), lambda i, j: (i, j)),
        out_shape=jax.ShapeDtypeStruct((s, d), F32),
        compiler_params=_params(("parallel", "parallel")),
    )(*acts, *ws)


def _ffn_down_bwd(dff, wd, act_dgate, act_dup, behind):
    s, d = dff.shape
    nsh, fs, _ = wd.shape
    tm = _pick(s, (1024, 512, 256, 128))
    nt = (((1,), (1,)), ((), ()))

    def body(d_ref, w_ref, pg_ref, pu_ref, _, dg_ref, du_ref):
        dact = lax.dot_general(d_ref[...], w_ref[...], nt, preferred_element_type=F32)
        dg_ref[...] = (dact * pg_ref[...]).astype(BF16)
        du_ref[...] = (dact * pu_ref[...]).astype(BF16)

    h_spec = pl.BlockSpec((None, tm, fs), lambda i, j: (j, i, 0))
    return pl.pallas_call(
        body, name="ffn_down_bwd", grid=(s // tm, nsh),
        in_specs=[pl.BlockSpec((tm, d), lambda i, j: (i, 0)),
                  pl.BlockSpec((None, fs, d), lambda i, j: (j, 0, 0)), h_spec, h_spec,
                  pl.BlockSpec((8, LANES), lambda i, j: (0, 0))],
        out_specs=[h_spec] * 2,
        out_shape=[jax.ShapeDtypeStruct((nsh, s, fs), BF16)] * 2,
        compiler_params=_params(("parallel", "parallel")),
    )(dff, wd, act_dgate, act_dup, behind)


def _ffn_dw(hiddens, other, name):
    n = len(hiddens)
    nsh, s, fs = hiddens[0].shape
    d = other.shape[1]
    tn = (((0,), (0,)), ((), ()))

    def body(*refs):
        for a_ref, o_ref in zip(refs[:n], refs[n + 1:]):
            o_ref[...] = lax.dot_general(a_ref[...], refs[n][...], tn, preferred_element_type=F32).astype(BF16)

    return pl.pallas_call(
        body, name=name, grid=(nsh,),
        in_specs=[pl.BlockSpec((None, s, fs), lambda j: (j, 0, 0))] * n + [pl.BlockSpec((s, d), lambda j: (0, 0))],
        out_specs=[pl.BlockSpec((None, fs, d), lambda j: (j, 0, 0))] * n,
        out_shape=[jax.ShapeDtypeStruct((nsh, fs, d), BF16)] * n,
        compiler_params=_params(("parallel",)),
    )(*hiddens, other)


def _ffn_down_final(act, wd, x1, tgt, g):
    nsh, s, fs = act.shape
    d = x1.shape[1]
    ts = _pick(s, (256, 128))

    def body(a_ref, w_ref, x1_ref, t_ref, g_ref, loss_ref, dy_ref, dff_ref, dg_ref):
        _acc_init(pl.program_id(0), loss_ref, dg_ref)
        ff_v = jnp.dot(a_ref[0], w_ref[0], preferred_element_type=F32)
        for j in range(1, nsh):
            ff_v = ff_v + jnp.dot(a_ref[j], w_ref[j], preferred_element_type=F32)
        err = x1_ref[...] + _rms(ff_v, g_ref[...]) - t_ref[...]
        tok = jnp.mean(err * err, axis=-1, keepdims=True)
        loss_ref[...] += 0.5 * jnp.sum(tok, axis=0, keepdims=True)
        dy = err * (1.0 / d)
        dy_ref[...] = dy
        dff, dg = _rms_bwd(ff_v, g_ref[...], dy)
        dff_ref[...] = dff.astype(BF16)
        dg_ref[...] += dg

    return pl.pallas_call(
        body, name="ffn_down_final", grid=(s // ts,),
        in_specs=[pl.BlockSpec((nsh, ts, fs), lambda i: (0, i, 0)),
                  pl.BlockSpec((nsh, fs, d), lambda i: (0, 0, 0), pipeline_mode=pl.Buffered(1)),
                  _rows(ts, d), _rows(ts, d), _vec(d)],
        out_specs=[_vec(LANES), _rows(ts, d), _rows(ts, d), _vec(d)],
        out_shape=[jax.ShapeDtypeStruct((1, LANES), F32), jax.ShapeDtypeStruct((s, d), F32),
                   jax.ShapeDtypeStruct((s, d), BF16), jax.ShapeDtypeStruct((1, d), F32)],
        compiler_params=_params(("arbitrary",)),
    )(act, wd, x1, tgt, g)


def _pre_bwd(dx1, dh, x, g):
    s, d = x.shape
    ts = _pick(s, (256, 128))

    def body(dx1_ref, dh_ref, x_ref, g_ref, dx_ref, dg_ref):
        _acc_init(pl.program_id(0), dg_ref)
        dxa, dg = _rms_bwd(x_ref[...], g_ref[...], dh_ref[...])
        dx_ref[...] = dx1_ref[...] + dxa
        dg_ref[...] += dg

    return pl.pallas_call(
        body, name="pre_bwd", grid=(s // ts,),
        in_specs=[_rows(ts, d), _rows(ts, d), _rows(ts, d), _vec(d)],
        out_specs=[_rows(ts, d), _vec(d)],
        out_shape=[jax.ShapeDtypeStruct((s, d), F32), jax.ShapeDtypeStruct((1, d), F32)],
        compiler_params=_params(("arbitrary",)),
    )(dx1, dh, x, g)


def _local_step(x, pos, tgt, vecs, in_weights_fn, mix_weights_fn, up_weights_fn, down_weights_fn, loss_fn,
                grads_fn):
    c = vecs["conv_b"].shape[1]
    ql = vecs["q_norm"].shape[1]
    kvl = vecs["kv_norm"].shape[1]
    half = jnp.arange(0, QK_ROPE, 2, dtype=F32)
    freq = ROPE_THETA ** (-half / QK_ROPE)
    inv_freq = jnp.concatenate([freq, freq, jnp.zeros((LANES - QK_ROPE,), F32)])[None, :]
    tabs = _rope_tables(pos, inv_freq)

    h = _pre_fwd(x, vecs["pre_mix_norm"])
    w_in_t, zero = in_weights_fn(h)
    z = _mm(h, w_in_t, "nt", "mm_z")
    w_uq_t, w_ukv, conv_w, w_out = mix_weights_fn(z)
    u0, qn, kvn, kpe = _split_fwd(z, vecs["q_norm"] + zero, vecs["kv_norm"], tabs, c, ql, kvl)
    u1 = _conv_fwd(u0, conv_w, vecs["conv_b"])
    q = _q_rope(_mm(qn, w_uq_t, "nt", "mm_q"), tabs)
    kv = _mm(kvn, w_ukv, "nn", "mm_kv")
    attn, lse = _attn_fwd(q, kv, kpe)
    mixin = _mix_fwd(u1, vecs["conv_ln_g"], vecs["conv_ln_b"], vecs["conv_out_norm"], attn, vecs["attn_out_norm"])
    mix = _mm(mixin, w_out, "nn", "mm_mix")
    x1, hf = _post_mix_fwd(x, mix, vecs["post_mix_norm"], vecs["pre_ffn_norm"])
    w_gate, w_up, slabs, rest_fn = up_weights_fn(hf)
    part = _ffn_up(hf, w_gate, w_up, slabs, "ffn_up_first")
    w_gate, w_up, slabs = rest_fn(part[2])
    act_dgate, act_dup, act = _ffn_up(hf, w_gate, w_up, slabs, "ffn_up_rest", earlier=part)
    w_down = down_weights_fn(act)
    loss, dy, dff, d_post_ffn = _ffn_down_final(act, w_down, x1, tgt, vecs["post_ffn_norm"])

    g = {"post_ffn_norm": d_post_ffn}
    zero = loss_fn(loss) + grads_fn("down", ("w_down",), _ffn_dw([act], dff, "ffn_dw_down"), dff)
    dgate, dup = _ffn_down_bwd(dff, w_down, act_dgate, act_dup, jnp.zeros((8, LANES), F32) + zero)
    zero = zero + grads_fn("up", ("w_gate", "w_up"), _ffn_dw([dgate, dup], hf, "ffn_dw_up"), dgate)
    dhf = _ffn_down([dgate, dup], [w_gate, w_up], "ffn_dhf")
    dx1, dmix, g["pre_ffn_norm"], g["post_mix_norm"] = _post_mix_bwd(
        dy, dhf, x1, vecs["pre_ffn_norm"] + zero, mix, vecs["post_mix_norm"])
    dmixin = _mm(dmix, w_out, "nt", "mm_dmixin")
    dw_out = _mm(mixin, dmix, "tn", "mm_dw_out", BF16)
    zero = grads_fn("out", ("w_out",), [dw_out], dmix)
    du1, dattn, g["conv_ln_g"], g["conv_ln_b"], g["conv_out_norm"], g["attn_out_norm"] = _mix_bwd(
        dmixin, u1, vecs["conv_ln_g"] + zero, vecs["conv_ln_b"], vecs["conv_out_norm"], attn, vecs["attn_out_norm"])
    du0, dw_conv, g["conv_b"] = _conv_bwd(du1, u0, conv_w)
    dqpre, dkv, dkpe_h = _attn_bwd(q, kv, kpe, attn, dattn, lse, tabs)
    dqn = _mm(dqpre, w_uq_t, "nn", "mm_dqn")
    dw_uq = _mm(dqpre, qn, "tn", "mm_dw_uq", BF16)
    dkvn = _mm(dkv, w_ukv, "nt", "mm_dkvn")
    dw_ukv = _mm(kvn, dkv, "tn", "mm_dw_ukv", BF16)
    zero = grads_fn("", (), [], dkvn)
    dz, g["q_norm"], g["kv_norm"] = _split_bwd(du0, z, dqn, dkvn, dkpe_h, vecs["q_norm"] + zero, vecs["kv_norm"], tabs,
                                               c, ql, kvl)
    dw_in = _mm(dz, h, "tn", "mm_dw_in", BF16)
    zero = grads_fn("in", ("w_in", "w_uq", "w_ukv", "conv_w"), [dw_in, dw_uq, dw_ukv, dw_conv], dz)
    dh = _mm(dz, w_in_t, "nn", "mm_dh")
    zero = zero + grads_fn("", (), [], dh)
    grad_x, g["pre_mix_norm"] = _pre_bwd(dx1, dh, x, vecs["pre_mix_norm"] + zero)
    return loss, grad_x, g


def _my_index():
    return 4 * lax.axis_index("x") + 2 * lax.axis_index("y") + lax.axis_index("c")


def _coords(idx):
    return ((idx >> 2) & 1, (idx >> 1) & 1, idx & 1)


def _place():
    x, y, c = lax.axis_index("x"), lax.axis_index("y"), lax.axis_index("c")
    return (x, y, c), (x, y, 1 - c), [(1 - x, y), (x, 1 - y), (1 - x, 1 - y)]


def _small_copies(src, land, send_sem, recv_sem):
    me = _my_index()
    return [pltpu.make_async_remote_copy(src_ref=src, dst_ref=land.at[me], send_sem=send_sem, recv_sem=recv_sem,
                                         device_id=_coords(me ^ p), device_id_type=MESH)
            for p in range(1, N_DEV)]


HBM_SPEC = pl.BlockSpec(memory_space=pltpu.HBM)
SEM_SPEC = pl.BlockSpec(memory_space=pltpu.SEMAPHORE)
DATAFLOW = pltpu.SideEffectType.DATAFLOW_SIDE_EFFECTING


def _split_start(name, copies_of, srcs, lands, after):
    n = len(srcs)

    def body(*refs):
        outs = refs[2 * n + 1:]
        for k in range(n):
            for cp in copies_of(refs[k], refs[n + k], outs[k], outs[n + k]):
                cp.start()
        outs[-1][...] = jnp.zeros_like(outs[-1])

    hbm = lambda a: pltpu.HBM(a.shape, a.dtype)
    out = pl.pallas_call(
        body, name=name,
        in_specs=[HBM_SPEC] * (2 * n + 1),
        out_specs=[SEM_SPEC] * (2 * n) + [HBM_SPEC] * (2 * n) + [pl.BlockSpec(memory_space=pltpu.VMEM)],
        out_shape=[pltpu.SemaphoreType.DMA(())] * (2 * n) + [hbm(a) for a in srcs] + [hbm(a) for a in lands]
        + [jax.ShapeDtypeStruct((8, LANES), F32)],
        input_output_aliases={k: 2 * n + k for k in range(2 * n)},
        compiler_params=pltpu.CompilerParams(has_side_effects=DATAFLOW),
    )(*[pltpu.with_memory_space_constraint(a, pltpu.HBM) for a in list(srcs) + list(lands) + [after]])
    return (out[:n], out[n:2 * n], out[2 * n:3 * n], out[3 * n:4 * n]), out[-1][0, 0]


def _split_wait(name, n_copies, started, after):
    send_sems, recv_sems, srcs, lands = started
    n = len(srcs)

    def body(*refs):
        for k in range(n):
            slots = refs[n + k].at[pl.ds(0, n_copies)]
            all_copies = pltpu.make_async_remote_copy(
                src_ref=slots, dst_ref=slots, send_sem=refs[2 * n + k], recv_sem=refs[3 * n + k],
                device_id=_place()[0], device_id_type=MESH)
            all_copies.wait_send()
            all_copies.wait_recv()

    hbm = lambda a: pltpu.HBM(a.shape, a.dtype)
    out = pl.pallas_call(
        body, name=name,
        in_specs=[HBM_SPEC] * (2 * n) + [SEM_SPEC] * (2 * n) + [HBM_SPEC],
        out_specs=[HBM_SPEC] * (2 * n),
        out_shape=[hbm(a) for a in srcs] + [hbm(a) for a in lands],
        input_output_aliases={k: k for k in range(2 * n)},
        compiler_params=pltpu.CompilerParams(has_side_effects=DATAFLOW),
    )(*srcs, *lands, *send_sems, *recv_sems, pltpu.with_memory_space_constraint(after, pltpu.HBM))
    return out[:n], out[n:]


def _slot(chip, core):
    return 4 * chip[0] + 2 * chip[1] + core


def _gather_copies(src, land, send_sem, recv_sem):
    (x, y, c), sib, chips = _place()
    return [pltpu.make_async_remote_copy(src_ref=src, dst_ref=land.at[_slot((x, y), c)], send_sem=send_sem,
                                         recv_sem=recv_sem, device_id=to, device_id_type=MESH)
            for to in [sib] + [(*chip, c) for chip in chips]]


def _pass_on_copies(src, land, send_sem, recv_sem):
    (x, y, c), sib, chips = _place()
    return [pltpu.make_async_remote_copy(src_ref=land.at[_slot(chip, c)], dst_ref=land.at[_slot(chip, c)],
                                         send_sem=send_sem, recv_sem=recv_sem, device_id=sib, device_id_type=MESH)
            for chip in chips]


def _gather_pass_on(lands, name):
    n = len(lands)

    def body(*refs):
        ins, outs = refs[:n], refs[n:2 * n]
        send_sems, recv_sems = refs[2 * n:]
        (x, y, c), sib, chips = _place()
        sends = []
        for k in range(n):
            for j, chip in enumerate(chips):
                sends.append(pltpu.make_async_remote_copy(
                    src_ref=ins[k].at[_slot(chip, c)], dst_ref=outs[k].at[_slot(chip, c)],
                    send_sem=send_sems.at[k, j], recv_sem=recv_sems.at[k, j], device_id=sib, device_id_type=MESH))
        for cp in sends:
            cp.start()
        for cp in sends:
            cp.wait_recv()
        for cp in sends:
            cp.wait_send()

    any_spec = pl.BlockSpec(memory_space=pl.ANY)
    return pl.pallas_call(
        body, name=name,
        in_specs=[any_spec] * n, out_specs=[any_spec] * n,
        out_shape=[jax.ShapeDtypeStruct(a.shape, a.dtype) for a in lands],
        input_output_aliases={k: k for k in range(n)},
        scratch_shapes=[pltpu.SemaphoreType.DMA((n, 3))] * 2,
        compiler_params=pltpu.CompilerParams(has_side_effects=True),
    )(*lands)


def _chip_copies(src, land, send_sem, recv_sem):
    (x, y, c), _, chips = _place()
    return [pltpu.make_async_remote_copy(src_ref=src.at[j], dst_ref=land.at[j], send_sem=send_sem,
                                         recv_sem=recv_sem, device_id=(*chip, c), device_id_type=MESH)
            for j, chip in enumerate(chips)]


ROW_TILE_BYTES = 14 * 1024 * 1024


def _stream_tile(r, c, bytes_per_elem):
    if r * c * bytes_per_elem <= ROW_TILE_BYTES:
        return r, c
    rows = [t for t in range(16, r, 16) if r % t == 0 and t * c * bytes_per_elem <= ROW_TILE_BYTES]
    if rows:
        return max(rows), c
    cols = [t for t in range(LANES, c, LANES) if c % t == 0 and r * t * bytes_per_elem <= ROW_TILE_BYTES]
    return r, max(cols)


def _sibling_copies(src, land, send_sem, recv_sem):
    (x, y, c), sib, _ = _place()
    return [pltpu.make_async_remote_copy(src_ref=src.at[2 * q + 1 - c], dst_ref=land.at[q], send_sem=send_sem,
                                         recv_sem=recv_sem, device_id=sib, device_id_type=MESH)
            for q in range(4)]


def _pair_sum(slots, blocks, theirs, name):
    _, r, c = theirs.shape
    tr, tc = _stream_tile(r, c, 3 * theirs.dtype.itemsize)

    def body(slot_ref, a_ref, b_ref, o_ref):
        o_ref[...] = (a_ref[...].astype(F32) + b_ref[...].astype(F32)).astype(o_ref.dtype)

    return pl.pallas_call(
        body, name=name,
        grid_spec=pltpu.PrefetchScalarGridSpec(
            num_scalar_prefetch=1, grid=(3, r // tr, c // tc),
            in_specs=[pl.BlockSpec((1, tr, tc), lambda j, a, b, slot_ref: (slot_ref[j], a, b)),
                      pl.BlockSpec((1, tr, tc), lambda j, a, b, slot_ref: (slot_ref[3 + j], a, b))],
            out_specs=pl.BlockSpec((1, tr, tc), lambda j, a, b, slot_ref: (j, a, b))),
        out_shape=jax.ShapeDtypeStruct((3, r, c), theirs.dtype),
        compiler_params=_params(("parallel", "parallel", "parallel")),
    )(slots, blocks, theirs)


def _adamw_small(parts, ws, ms, vs):
    n = len(ws)
    c1 = 1.0 - ADAM_B1
    c2 = 1.0 - ADAM_B2
    bc1 = 1.0 - ADAM_B1 ** ADAM_STEP
    bc2 = 1.0 - ADAM_B2 ** ADAM_STEP

    def body(*refs):
        p_ref, outs = refs[0], refs[1 + 3 * n:]
        off = 0
        for k in range(n):
            w_ref, m_ref, v_ref = refs[1 + k], refs[1 + n + k], refs[1 + 2 * n + k]
            width = w_ref.shape[1]
            g = p_ref[0, :, off:off + width]
            for j in range(1, N_DEV):
                g = g + p_ref[j, :, off:off + width]
            nm = ADAM_B1 * m_ref[...] + c1 * g
            nv = ADAM_B2 * v_ref[...] + c2 * (g * g)
            outs[k][...] = g
            outs[n + k][...] = -ADAM_LR * ((nm / bc1) / (jnp.sqrt(nv / bc2) + ADAM_EPS) + ADAM_WD * w_ref[...])
            outs[2 * n + k][...] = nm
            outs[3 * n + k][...] = nv
            off += width

    out = pl.pallas_call(
        body, name="adamw_small",
        out_shape=[jax.ShapeDtypeStruct(a.shape, F32) for a in ws] * 4,
        compiler_params=pltpu.CompilerParams(vmem_limit_bytes=VMEM_LIMIT),
    )(parts, *ws, *ms, *vs)
    return out[:n], out[n:2 * n], out[2 * n:3 * n], out[3 * n:]


def _reduce_adamw(parts, w, m, v, name, blocks, theirs, slots):
    r, c = w.shape
    n_parts = parts.shape[0]
    tr, tc = _stream_tile(r, c, (n_parts + 2) * parts.dtype.itemsize + 7 * 4)
    c1 = 1.0 - ADAM_B1
    c2 = 1.0 - ADAM_B2
    bc1 = 1.0 - ADAM_B1 ** ADAM_STEP
    bc2 = 1.0 - ADAM_B2 ** ADAM_STEP

    def body(_, b_ref, t_ref, p_ref, w_ref, m_ref, v_ref, g_ref, d_ref, nm_ref, nv_ref):
        g = b_ref[0].astype(F32) + t_ref[0].astype(F32)
        for j in range(n_parts):
            g = g + p_ref[j].astype(F32)
        nm = ADAM_B1 * m_ref[...] + c1 * g
        nv = ADAM_B2 * v_ref[...] + c2 * (g * g)
        g_ref[...] = g
        nm_ref[...] = nm
        nv_ref[...] = nv
        d_ref[...] = -ADAM_LR * ((nm / bc1) / (jnp.sqrt(nv / bc2) + ADAM_EPS) + ADAM_WD * w_ref[...])

    out = jax.ShapeDtypeStruct((r, c), F32)
    grid = (r // tr, c // tc)
    blk = pl.BlockSpec((tr, tc), lambda i, j, slot_ref: (i, j))
    return pl.pallas_call(
        body, name=name,
        grid_spec=pltpu.PrefetchScalarGridSpec(
            num_scalar_prefetch=1, grid=grid,
            in_specs=[pl.BlockSpec((1, tr, tc), lambda i, j, slot_ref: (slot_ref[0], i, j)),
                      pl.BlockSpec((1, tr, tc), lambda i, j, slot_ref: (slot_ref[1], i, j)),
                      pl.BlockSpec((n_parts, tr, tc), lambda i, j, slot_ref: (0, i, j)), blk, blk, blk],
            out_specs=[blk] * 4),
        out_shape=[out] * 4,
        compiler_params=_params(("parallel", "parallel")),
    )(slots, blocks, theirs, parts, w, m, v)


_MIX = ("w_in", "w_uq", "w_ukv", "conv_w", "w_out")
_FFN = ("w_gate", "w_up", "w_down")
_BIG = _MIX + _FFN
_TRANSPOSED = ("w_in", "w_uq", "w_gate", "w_up")
_SMALL = ("pre_mix_norm", "q_norm", "kv_norm", "conv_b", "conv_ln_g", "conv_ln_b", "conv_out_norm",
          "attn_out_norm", "post_mix_norm", "pre_ffn_norm", "post_ffn_norm")
_ORDER = ("pre_mix_norm", "w_in", "q_norm", "w_uq", "kv_norm", "w_ukv", "conv_w", "conv_b", "conv_ln_g",
          "conv_ln_b", "conv_out_norm", "attn_out_norm", "w_out", "post_mix_norm", "pre_ffn_norm", "w_gate",
          "w_up", "w_down", "post_ffn_norm")


def _cols_from_shards(g):
    return jnp.transpose(g, (1, 0, 2)).reshape(g.shape[1], N_DEV * g.shape[2])


def _cols_to_shards(w):
    k, n8 = w.shape
    return jnp.transpose(w.reshape(k, N_DEV, n8 // N_DEV), (1, 0, 2))


def _step(x, positions, loss_target, w, m, v):
    s, d = x.shape[1], x.shape[2]
    x2, tgt = x[0], loss_target[0]
    pos = positions.reshape(s, 1)
    vecs = {n: w[n] for n in _SMALL}
    core = lax.axis_index("c").astype(jnp.int32)
    my_chip = (2 * lax.axis_index("x") + lax.axis_index("y")).astype(jnp.int32)
    other_chips = [my_chip ^ 2, my_chip ^ 1, my_chip ^ 3]
    pair_slots = jnp.stack([2 * q + core for q in other_chips] + other_chips)
    own_slots = jnp.stack([2 * my_chip + core, my_chip])
    own = {}
    n_in_cols = N_DEV * w["w_in"].shape[2]
    gathers, scatters, to_sibling = {}, {}, []

    def shard(t, n):
        return t[n][0].T if n in _TRANSPOSED else t[n][0]

    def gather_start(names, tag, after, zero=0.0):
        srcs = [w[n][0] if n == "conv_w" else (shard(w, n) + zero).astype(BF16) for n in names]
        lands = [lax.empty((N_DEV,) + a.shape, a.dtype) for a in srcs]
        gathers[tag], zero = _split_start("gather_" + tag + "_start", _gather_copies, srcs, lands, after)
        return zero

    def gather_finish(names, tag, after):
        srcs, lands = _split_wait("gather_" + tag + "_wait", 4, gathers[tag], after)
        lands = _gather_pass_on(lands, "gather_" + tag + "_pass_on")
        me = _my_index()
        return {n: lax.dynamic_update_slice(g, a[None], (me,) + (0,) * a.ndim) for n, g, a in zip(names, lands, srcs)}

    def in_weights_fn(h):
        w_in_g = gather_finish(("w_in",), "in", h)["w_in"]
        return jnp.pad(w_in_g.reshape(-1, d), ((0, LANES - QK_ROPE), (0, 0))), 0.0

    def mix_weights_fn(z):
        gath = gather_finish(_MIX[1:], "mix", z)
        w_uq_t = jnp.pad(gath["w_uq"], ((0, 0), (0, HEAD_PAD - QK_HEAD), (0, 0))).reshape(N_HEADS * HEAD_PAD, -1)
        return (w_uq_t, _cols_from_shards(gath["w_ukv"]), _cols_from_shards(gath["conv_w"]),
                gath["w_out"].reshape(-1, d))

    def up_weights_fn(hf):
        srcs, lands = _split_wait("gather_up_wait", 4, gathers["up"], hf)
        me = _my_index()
        lands = [lax.dynamic_update_slice(g, a[None], (me, 0, 0)) for g, a in zip(lands, srcs)]
        passing, _ = _split_start("gather_up_pass_on_start", _pass_on_copies, srcs, lands, hf)
        here = jnp.stack([me, me ^ 1] + [2 * q + core for q in other_chips])
        late = jnp.stack([2 * q + 1 - core for q in other_chips])

        def rest_fn(after):
            _, done = _split_wait("gather_up_pass_on_wait", 3, passing, after)
            return done[0], done[1], late

        return passing[3][0], passing[3][1], here, rest_fn

    def down_weights_fn(act):
        return gather_finish(("w_down",), "down", act)["w_down"]

    def scatter_advance(after):
        if not to_sibling:
            return 0.0
        names, tag, sent = to_sibling.pop()
        blocks, theirs = _split_wait("to_sibling_" + tag + "_wait", 4, sent, after)
        own.update(zip(names, zip(blocks, theirs)))
        pairs = [_pair_sum(pair_slots, b, t, "pair_sum_" + n) for n, b, t in zip(names, blocks, theirs)]
        lands = [lax.empty((3,) + p.shape[1:], p.dtype) for p in pairs]
        scatters[tag], zero = _split_start("scatter_" + tag + "_start", _chip_copies, pairs, lands, theirs[0])
        return zero

    to_blocks = {
        "w_in": lambda a: a[:n_in_cols].reshape(N_DEV, -1, d),
        "w_uq": lambda a: a.reshape(N_HEADS, HEAD_PAD, -1)[:, :QK_HEAD],
        "w_ukv": _cols_to_shards, "conv_w": _cols_to_shards,
        "w_out": lambda a: a.reshape(N_DEV, -1, d),
    }

    def grads_fn(tag, names, grads, after):
        zero = scatter_advance(after)
        if not grads:
            return zero
        blocks = [to_blocks.get(n, lambda a: a)(a) for n, a in zip(names, grads)]
        lands = [lax.empty((4,) + b.shape[1:], b.dtype) for b in blocks]
        sent, zero2 = _split_start("to_sibling_" + tag + "_start", _sibling_copies, blocks, lands, after)
        to_sibling.append((names, tag, sent))
        return zero + zero2

    zero = gather_start(("w_in",), "in", x2)
    zero = gather_start(_MIX[1:], "mix", x2, zero)
    zero = gather_start(("w_gate", "w_up"), "up", x2, zero)
    vecs["pre_mix_norm"] = vecs["pre_mix_norm"] + gather_start(("w_down",), "down", x2, zero)
    totals = []

    def loss_fn(loss):
        totals.append(lax.psum(loss[0, 0], ("x", "y", "c")))
        return jnp.minimum(jnp.abs(totals[0]), 0.0)

    _, grad_x, g = _local_step(x2, pos, tgt, vecs, in_weights_fn, mix_weights_fn, up_weights_fn, down_weights_fn,
                               loss_fn, grads_fn)
    total = totals[0]

    small = jnp.concatenate([g[n] for n in _SMALL], axis=1)
    small_started, zero = _split_start("gather_small_start", _small_copies, [small],
                                       [lax.empty((N_DEV,) + small.shape, F32)], grad_x)

    res = {}
    after = grad_x
    own_slots = own_slots + zero.astype(jnp.int32)
    for tag, names in (("down", ("w_down",)), ("up", ("w_gate", "w_up")), ("out", ("w_out",)),
                       ("in", ("w_in", "w_uq", "w_ukv", "conv_w"))):
        _, recv = _split_wait("scatter_" + tag + "_wait", 3, scatters[tag], after)
        for n, parts in zip(names, recv):
            res[n] = _reduce_adamw(parts, shard(w, n), shard(m, n), shard(v, n), "adamw_" + n, *own[n], own_slots)
            after = res[n][1]
            res[n] = [(t.T if n in _TRANSPOSED else t)[None] for t in res[n]]
    (small,), (small_all,) = _split_wait("gather_small_wait", N_DEV - 1, small_started, after)
    small_all = lax.dynamic_update_slice(small_all, small[None], (_my_index(), 0, 0))
    small_res = _adamw_small(small_all, *[[t[n] for n in _SMALL] for t in (w, m, v)])
    for k, n in enumerate(_SMALL):
        res[n] = [part[k] for part in small_res]

    outs = [total, grad_x[None]]
    for part in range(4):
        outs.extend(res[n][part] for n in _ORDER)
    return tuple(outs)


def kernel(x, positions, pre_mix_norm, w_in, q_norm, w_uq, kv_norm, w_ukv, conv_w, conv_b, conv_ln_g, conv_ln_b, conv_out_norm, attn_out_norm, w_out, post_mix_norm, pre_ffn_norm, w_gate, w_up, w_down, post_ffn_norm, loss_target, m_pre_mix_norm, m_w_in, m_q_norm, m_w_uq, m_kv_norm, m_w_ukv, m_conv_w, m_conv_b, m_conv_ln_g, m_conv_ln_b, m_conv_out_norm, m_attn_out_norm, m_w_out, m_post_mix_norm, m_pre_ffn_norm, m_w_gate, m_w_up, m_w_down, m_post_ffn_norm, v_pre_mix_norm, v_w_in, v_q_norm, v_w_uq, v_kv_norm, v_w_ukv, v_conv_w, v_conv_b, v_conv_ln_g, v_conv_ln_b, v_conv_out_norm, v_attn_out_norm, v_w_out, v_post_mix_norm, v_pre_ffn_norm, v_w_gate, v_w_up, v_w_down, v_post_ffn_norm):
    w = dict(zip(_ORDER, (pre_mix_norm, w_in, q_norm, w_uq, kv_norm, w_ukv, conv_w, conv_b, conv_ln_g, conv_ln_b,
                          conv_out_norm, attn_out_norm, w_out, post_mix_norm, pre_ffn_norm, w_gate, w_up, w_down,
                          post_ffn_norm)))
    m = dict(zip(_ORDER, (m_pre_mix_norm, m_w_in, m_q_norm, m_w_uq, m_kv_norm, m_w_ukv, m_conv_w, m_conv_b,
                          m_conv_ln_g, m_conv_ln_b, m_conv_out_norm, m_attn_out_norm, m_w_out, m_post_mix_norm,
                          m_pre_ffn_norm, m_w_gate, m_w_up, m_w_down, m_post_ffn_norm)))
    v = dict(zip(_ORDER, (v_pre_mix_norm, v_w_in, v_q_norm, v_w_uq, v_kv_norm, v_w_ukv, v_conv_w, v_conv_b,
                          v_conv_ln_g, v_conv_ln_b, v_conv_out_norm, v_attn_out_norm, v_w_out, v_post_mix_norm,
                          v_pre_ffn_norm, v_w_gate, v_w_up, v_w_down, v_post_ffn_norm)))
    return _step(x, positions, loss_target, w, m, v)
```

```python
import functools

import jax
import jax.numpy as jnp
from jax import lax
from jax.experimental import pallas as pl
from jax.experimental.pallas import tpu as pltpu

N_DEV = 8
N_HEADS = 8
QK_NOPE = 128
QK_ROPE = 64
V_HEAD = 128
QK_HEAD = QK_NOPE + QK_ROPE
HEAD_PAD = 256
LANES = 128
ATTN_BLOCK = 512
CONV_K = 31
CONV_PAD = 32
EPS = 1e-6
ROPE_THETA = 10000.0
ADAM_LR = 0.001
ADAM_B1 = 0.9
ADAM_B2 = 0.999
ADAM_EPS = 1e-08
ADAM_WD = 0.01
ADAM_STEP = 10
VMEM_LIMIT = 56 * 1024 * 1024
F32 = jnp.float32
BF16 = jnp.bfloat16
MESH = pl.DeviceIdType.MESH
NEG = -1e30


def _pick(n, prefs):
    for p in prefs:
        if p <= n and n % p == 0:
            return p
    return n


def _params(sem):
    return pltpu.CompilerParams(dimension_semantics=sem, vmem_limit_bytes=VMEM_LIMIT)


_DIMS = {"nn": (((1,), (0,)), ((), ())), "nt": (((1,), (1,)), ((), ())), "tn": (((0,), (0,)), ((), ()))}


MM_VMEM_BUDGET = 40 * 1024 * 1024
MM_MAX_MACS = 3 * 1024 ** 3


V7X_HBM_BYTES_PER_S = 3.0e12
V7X_MXU_MACS_PER_S = 0.45e15
GRID_STEP_S = 0.35e-6


def _mm_tiles(m, n, k, size_a, size_b, size_o):
    best = None
    for tm in sorted({m, 1024, 512, 256, 128}, reverse=True):
        if tm > m or m % tm:
            continue
        for tn in sorted({n, 2048, 1024, 512, 384, 256, 128}, reverse=True):
            if tn > n or n % tn:
                continue
            vmem = 2 * (tm * k * size_a + k * tn * size_b + tm * tn * size_o)
            if vmem > MM_VMEM_BUDGET or tm * tn * k > MM_MAX_MACS:
                continue
            b_reads = 1 if tn == n else m // tm
            traffic = m * k * size_a + b_reads * k * n * size_b + m * n * size_o
            exposed = tm * k * size_a + k * tn * size_b + tm * tn * size_o
            steps = (m // tm) * (n // tn)
            key = (max(traffic / V7X_HBM_BYTES_PER_S, m * n * k / V7X_MXU_MACS_PER_S)
                   + exposed / V7X_HBM_BYTES_PER_S + steps * GRID_STEP_S)
            if best is None or key < best[0]:
                best = (key, tm, tn)
    assert best is not None, (m, n, k)
    return best[1], best[2]


def _mm(a, b, mode, name, out_dtype=F32, behind=None):
    if mode == "nn":
        (m, k), (k2, n) = a.shape, b.shape
    elif mode == "nt":
        (m, k), (n, k2) = a.shape, b.shape
    else:
        (k, m), (k2, n) = a.shape, b.shape
    assert k == k2, (a.shape, b.shape, mode)
    tm, tn = _mm_tiles(m, n, k, a.dtype.itemsize, b.dtype.itemsize, jnp.dtype(out_dtype).itemsize)
    dims = _DIMS[mode]

    def body(a_ref, b_ref, *rest):
        acc = lax.dot_general(a_ref[...].astype(BF16), b_ref[...].astype(BF16), dims, preferred_element_type=F32)
        rest[-1][...] = acc.astype(rest[-1].dtype)

    if mode == "tn":
        a_spec = pl.BlockSpec((k, tm), lambda i, j: (0, i))
    else:
        a_spec = pl.BlockSpec((tm, k), lambda i, j: (i, 0))
    if mode == "nt":
        b_spec = pl.BlockSpec((tn, k), lambda i, j: (j, 0))
    else:
        b_spec = pl.BlockSpec((k, tn), lambda i, j: (0, j))
    o_spec = pl.BlockSpec((tm, tn), lambda i, j: (i, j))
    extra = [] if behind is None else [behind]
    return pl.pallas_call(
        body, name=name,
        grid=(m // tm, n // tn),
        in_specs=[a_spec, b_spec] + [pl.BlockSpec((8, LANES), lambda i, j: (0, 0))] * len(extra),
        out_specs=o_spec,
        out_shape=jax.ShapeDtypeStruct((m, n), out_dtype),
        compiler_params=_params(("parallel", "parallel")),
    )(a, b, *extra)


def _sigmoid(x):
    return 1.0 / (1.0 + jnp.exp(-x))


def _rms(x, g):
    r = lax.rsqrt(jnp.mean(x * x, axis=-1, keepdims=True) + EPS)
    return (x * r) * g


def _rms_bwd(x, g, dy):
    r = lax.rsqrt(jnp.mean(x * x, axis=-1, keepdims=True) + EPS)
    xh = x * r
    dyg = dy * g
    dx = r * (dyg - xh * jnp.mean(dyg * xh, axis=-1, keepdims=True))
    return dx, jnp.sum(dy * xh, axis=0, keepdims=True)


def _ln(x, g, b):
    mu = jnp.mean(x, axis=-1, keepdims=True)
    xc = x - mu
    rs = lax.rsqrt(jnp.mean(xc * xc, axis=-1, keepdims=True) + EPS)
    return (xc * rs) * g + b


def _ln_bwd(x, g, dy):
    mu = jnp.mean(x, axis=-1, keepdims=True)
    xc = x - mu
    rs = lax.rsqrt(jnp.mean(xc * xc, axis=-1, keepdims=True) + EPS)
    xh = xc * rs
    dyg = dy * g
    dx = rs * (dyg - jnp.mean(dyg, axis=-1, keepdims=True) - xh * jnp.mean(dyg * xh, axis=-1, keepdims=True))
    return dx, jnp.sum(dy * xh, axis=0, keepdims=True), jnp.sum(dy, axis=0, keepdims=True)


def _silu(x):
    return x * _sigmoid(x)


def _silu_grad(x):
    s = _sigmoid(x)
    return s * (1.0 + x * (1.0 - s))


def _rope(x, cos, sa, sb):
    return x * cos + pltpu.roll(x, 96, 1) * sa + pltpu.roll(x, 32, 1) * sb


def _rope_t(d, cos, sa, sb):
    return d * cos - pltpu.roll(d, 96, 1) * sa - pltpu.roll(d, 32, 1) * sb


def _rows(ts, w):
    return pl.BlockSpec((ts, w), lambda i: (i, 0))


def _vec(w):
    return pl.BlockSpec((1, w), lambda i: (0, 0))


def _acc_init(i, *refs):
    @pl.when(i == 0)
    def _():
        for r in refs:
            r[...] = jnp.zeros_like(r)


def _rope_tables(pos, inv_freq):
    s = pos.shape[0]
    ts = _pick(s, (512, 256, 128))

    def body(p_ref, f_ref, c_ref, sa_ref, sb_ref):
        ang = p_ref[...].astype(F32) * f_ref[...]
        lane = lax.broadcasted_iota(jnp.int32, ang.shape, 1)
        c, sn = jnp.cos(ang), jnp.sin(ang)
        c_ref[...] = jnp.where(lane < QK_ROPE, c, 0.0)
        sa_ref[...] = jnp.where(lane < QK_ROPE // 2, -sn, 0.0)
        sb_ref[...] = jnp.where((lane >= QK_ROPE // 2) & (lane < QK_ROPE), sn, 0.0)

    out = jax.ShapeDtypeStruct((s, LANES), F32)
    return pl.pallas_call(
        body, name="rope_tables", grid=(s // ts,),
        in_specs=[_rows(ts, 1), _vec(LANES)],
        out_specs=[_rows(ts, LANES)] * 3, out_shape=[out] * 3,
        compiler_params=_params(("parallel",)),
    )(pos, inv_freq)


def _pre_fwd(x, g):
    s, d = x.shape
    ts = _pick(s, (256, 128))

    def body(x_ref, g_ref, h_ref):
        h_ref[...] = _rms(x_ref[...], g_ref[...]).astype(BF16)

    return pl.pallas_call(
        body, name="pre_fwd", grid=(s // ts,),
        in_specs=[_rows(ts, d), _vec(d)], out_specs=_rows(ts, d),
        out_shape=jax.ShapeDtypeStruct((s, d), BF16),
        compiler_params=_params(("parallel",)),
    )(x, g)


def _split_fwd(z, gq, gkv, tabs, c, ql, kvl):
    s, zw = z.shape
    ts = _pick(s, (256, 128))
    o_q, o_kv, o_kr = 2 * c, 2 * c + ql, 2 * c + ql + kvl

    def body(z_ref, gq_ref, gkv_ref, c_ref, sa_ref, sb_ref, u0_ref, qn_ref, kvn_ref, kpe_ref):
        u0_ref[...] = z_ref[:, 0:c] * _sigmoid(z_ref[:, c:2 * c])
        qn_ref[...] = _rms(z_ref[:, o_q:o_kv], gq_ref[...]).astype(BF16)
        kvn_ref[...] = _rms(z_ref[:, o_kv:o_kr], gkv_ref[...]).astype(BF16)
        kpe_ref[...] = _rope(z_ref[:, o_kr:o_kr + LANES], c_ref[...], sa_ref[...], sb_ref[...]).astype(BF16)

    return pl.pallas_call(
        body, name="split_fwd", grid=(s // ts,),
        in_specs=[_rows(ts, zw), _vec(ql), _vec(kvl)] + [_rows(ts, LANES)] * 3,
        out_specs=[_rows(ts, c), _rows(ts, ql), _rows(ts, kvl), _rows(ts, LANES)],
        out_shape=[jax.ShapeDtypeStruct((s, c), F32), jax.ShapeDtypeStruct((s, ql), BF16),
                   jax.ShapeDtypeStruct((s, kvl), BF16), jax.ShapeDtypeStruct((s, LANES), BF16)],
        compiler_params=_params(("parallel",)),
    )(z, gq, gkv, *tabs)


def _split_bwd(du0, z, dqn, dkvn, dkpe_h, gq, gkv, tabs, c, ql, kvl):
    s, zw = z.shape
    ts = _pick(s, (256, 128))
    o_q, o_kv, o_kr = 2 * c, 2 * c + ql, 2 * c + ql + kvl

    def body(du0_ref, z_ref, dqn_ref, dkvn_ref, dkh_ref, gq_ref, gkv_ref, c_ref, sa_ref, sb_ref,
             dz_ref, dgq_ref, dgkv_ref):
        _acc_init(pl.program_id(0), dgq_ref, dgkv_ref)
        du0 = du0_ref[...]
        a = z_ref[:, 0:c]
        sg = _sigmoid(z_ref[:, c:2 * c])
        dz_ref[:, 0:c] = (du0 * sg).astype(BF16)
        dz_ref[:, c:2 * c] = (du0 * a * sg * (1.0 - sg)).astype(BF16)
        dq, dgq = _rms_bwd(z_ref[:, o_q:o_kv], gq_ref[...], dqn_ref[...])
        dz_ref[:, o_q:o_kv] = dq.astype(BF16)
        dgq_ref[...] += dgq
        dkv, dgkv = _rms_bwd(z_ref[:, o_kv:o_kr], gkv_ref[...], dkvn_ref[...])
        dz_ref[:, o_kv:o_kr] = dkv.astype(BF16)
        dgkv_ref[...] += dgkv
        dk = dkh_ref[:, 0:LANES]
        for h in range(1, N_HEADS):
            dk = dk + dkh_ref[:, h * LANES:(h + 1) * LANES]
        dz_ref[:, o_kr:o_kr + LANES] = _rope_t(dk, c_ref[...], sa_ref[...], sb_ref[...]).astype(BF16)

    return pl.pallas_call(
        body, name="split_bwd", grid=(s // ts,),
        in_specs=[_rows(ts, c), _rows(ts, zw), _rows(ts, ql), _rows(ts, kvl), _rows(ts, N_HEADS * LANES),
                  _vec(ql), _vec(kvl)] + [_rows(ts, LANES)] * 3,
        out_specs=[_rows(ts, zw), _vec(ql), _vec(kvl)],
        out_shape=[jax.ShapeDtypeStruct((s, zw), BF16), jax.ShapeDtypeStruct((1, ql), F32),
                   jax.ShapeDtypeStruct((1, kvl), F32)],
        compiler_params=_params(("arbitrary",)),
    )(du0, z, dqn, dkvn, dkpe_h, gq, gkv, *tabs)


def _q_rope(qpre, tabs):
    s, w = qpre.shape
    ts = _pick(s, (256, 128))

    def body(q_ref, c_ref, sa_ref, sb_ref, o_ref):
        cs, sa, sb = c_ref[...], sa_ref[...], sb_ref[...]
        for h in range(N_HEADS):
            lo = h * HEAD_PAD
            o_ref[:, lo:lo + QK_NOPE] = q_ref[:, lo:lo + QK_NOPE].astype(BF16)
            o_ref[:, lo + QK_NOPE:lo + HEAD_PAD] = _rope(q_ref[:, lo + QK_NOPE:lo + HEAD_PAD], cs, sa, sb).astype(BF16)

    return pl.pallas_call(
        body, name="q_rope", grid=(s // ts,),
        in_specs=[_rows(ts, w)] + [_rows(ts, LANES)] * 3, out_specs=_rows(ts, w),
        out_shape=jax.ShapeDtypeStruct((s, w), BF16),
        compiler_params=_params(("parallel",)),
    )(qpre, *tabs)


def _conv_fwd(u0, w, b):
    s, c = u0.shape
    tc = LANES
    rc = _pick(s, (256, 128))

    def body(u_ref, w_ref, b_ref, o_ref, pad_ref):
        pad_ref[0:CONV_PAD, :] = jnp.zeros((CONV_PAD, tc), F32)
        pad_ref[CONV_PAD:CONV_PAD + s, :] = u_ref[...]
        for r in range(s // rc):
            acc = jnp.broadcast_to(b_ref[...], (rc, tc))
            for k in range(CONV_K):
                lo = r * rc + CONV_PAD - (CONV_K - 1) + k
                acc = acc + w_ref[k:k + 1, :] * pad_ref[lo:lo + rc, :]
            o_ref[r * rc:(r + 1) * rc, :] = acc

    col = lambda j: (0, j)
    return pl.pallas_call(
        body, name="conv_fwd", grid=(c // tc,),
        in_specs=[pl.BlockSpec((s, tc), col), pl.BlockSpec((CONV_K, tc), col), pl.BlockSpec((1, tc), col)],
        out_specs=pl.BlockSpec((s, tc), col),
        out_shape=jax.ShapeDtypeStruct((s, c), F32),
        scratch_shapes=[pltpu.VMEM((s + CONV_PAD, tc), F32)],
        compiler_params=_params(("parallel",)),
    )(u0, w, b)


def _conv_bwd(du1, u0, w):
    s, c = u0.shape
    tc = LANES
    rc = _pick(s, (256, 128))

    def body(d_ref, u_ref, w_ref, du_ref, dw_ref, db_ref, upad_ref, dpad_ref):
        upad_ref[0:CONV_PAD, :] = jnp.zeros((CONV_PAD, tc), F32)
        upad_ref[CONV_PAD:CONV_PAD + s, :] = u_ref[...]
        dpad_ref[0:s, :] = d_ref[...]
        dpad_ref[s:s + CONV_PAD, :] = jnp.zeros((CONV_PAD, tc), F32)
        for r in range(s // rc):
            acc = jnp.zeros((rc, tc), F32)
            for k in range(CONV_K):
                lo = r * rc + (CONV_K - 1) - k
                acc = acc + w_ref[k:k + 1, :] * dpad_ref[lo:lo + rc, :]
            du_ref[r * rc:(r + 1) * rc, :] = acc
        for k in range(CONV_K):
            acc8 = jnp.zeros((8, tc), F32)
            for r in range(s // rc):
                lo = r * rc + CONV_PAD - (CONV_K - 1) + k
                prod = d_ref[r * rc:(r + 1) * rc, :] * upad_ref[lo:lo + rc, :]
                acc8 = acc8 + jnp.sum(prod.reshape(rc // 8, 8, tc), axis=0)
            dw_ref[k:k + 1, :] = jnp.sum(acc8, axis=0, keepdims=True)
        db_ref[...] = jnp.sum(d_ref[...], axis=0, keepdims=True)

    col = lambda j: (0, j)
    return pl.pallas_call(
        body, name="conv_bwd", grid=(c // tc,),
        in_specs=[pl.BlockSpec((s, tc), col), pl.BlockSpec((s, tc), col), pl.BlockSpec((CONV_K, tc), col)],
        out_specs=[pl.BlockSpec((s, tc), col), pl.BlockSpec((CONV_K, tc), col), pl.BlockSpec((1, tc), col)],
        out_shape=[jax.ShapeDtypeStruct((s, c), F32), jax.ShapeDtypeStruct((CONV_K, c), F32),
                   jax.ShapeDtypeStruct((1, c), F32)],
        scratch_shapes=[pltpu.VMEM((s + CONV_PAD, tc), F32), pltpu.VMEM((s + CONV_PAD, tc), F32)],
        compiler_params=_params(("parallel",)),
    )(du1, u0, w)


def _causal_mask(sc, qi, kj, tq, tk):
    rows = qi * tq + lax.broadcasted_iota(jnp.int32, sc.shape, 0)
    cols = kj * tk + lax.broadcasted_iota(jnp.int32, sc.shape, 1)
    return jnp.where(cols <= rows, sc, NEG)


def _attn_fwd(q, kv, kpe):
    s = q.shape[0]
    tq = tk = _pick(s, (ATTN_BLOCK, 256, 128))
    reps = tk // LANES
    scale = QK_HEAD ** -0.5
    nt = (((1,), (1,)), ((), ()))

    def body(q_ref, kn_ref, v_ref, kpe_ref, o_ref, lse_ref, kf_ref, vb_ref, m_ref, l_ref, acc_ref):
        i = pl.program_id(1)

        @pl.when(i == 0)
        def _():
            kf_ref[:, 0:QK_NOPE] = kn_ref[...].astype(BF16)
            kf_ref[:, QK_NOPE:HEAD_PAD] = kpe_ref[...]
            vb_ref[...] = v_ref[...].astype(BF16)

        qb = q_ref[...]
        m_ref[...] = jnp.full((tq, LANES), NEG, F32)
        l_ref[...] = jnp.zeros((tq, LANES), F32)
        acc_ref[...] = jnp.zeros((tq, V_HEAD), F32)

        def block(j, diagonal):
            off = j * tk
            sc = lax.dot_general(qb, kf_ref[pl.ds(off, tk), :], nt, preferred_element_type=F32) * scale
            if diagonal:
                sc = _causal_mask(sc, 0, 0, tq, tk)
            m_prev = m_ref[...]
            m_new = jnp.maximum(m_prev, jnp.max(sc, axis=1, keepdims=True))
            p = jnp.exp(sc - jnp.tile(m_new, (1, reps)))
            alpha = jnp.exp(m_prev - m_new)
            l_ref[...] = alpha * l_ref[...] + jnp.sum(p, axis=1, keepdims=True)
            acc_ref[...] = alpha * acc_ref[...] + jnp.dot(p.astype(BF16), vb_ref[pl.ds(off, tk), :],
                                                          preferred_element_type=F32)
            m_ref[...] = m_new

        for qi in range(s // tq):
            @pl.when(i == qi)
            def _(qi=qi):
                for j in range(qi):
                    block(j, False)
                block(qi, True)

        o_ref[...] = acc_ref[...] / l_ref[...]
        lse_ref[...] = m_ref[...] + jnp.log(l_ref[...])

    return pl.pallas_call(
        body, name="attn_fwd", grid=(N_HEADS, s // tq),
        in_specs=[pl.BlockSpec((tq, HEAD_PAD), lambda h, i: (i, h)),
                  pl.BlockSpec((s, QK_NOPE), lambda h, i: (0, 2 * h)),
                  pl.BlockSpec((s, V_HEAD), lambda h, i: (0, 2 * h + 1)),
                  pl.BlockSpec((s, LANES), lambda h, i: (0, 0))],
        out_specs=[pl.BlockSpec((tq, V_HEAD), lambda h, i: (i, h)),
                   pl.BlockSpec((tq, LANES), lambda h, i: (i, h))],
        out_shape=[jax.ShapeDtypeStruct((s, N_HEADS * V_HEAD), F32),
                   jax.ShapeDtypeStruct((s, N_HEADS * LANES), F32)],
        scratch_shapes=[pltpu.VMEM((s, HEAD_PAD), BF16), pltpu.VMEM((s, V_HEAD), BF16),
                        pltpu.VMEM((tq, LANES), F32), pltpu.VMEM((tq, LANES), F32), pltpu.VMEM((tq, V_HEAD), F32)],
        compiler_params=_params(("parallel", "arbitrary")),
    )(q, kv, kv, kpe)


def _attn_bwd(q, kv, kpe, o, do, lse, tabs):
    s = q.shape[0]
    tq = tk = _pick(s, (ATTN_BLOCK, 256, 128))
    nq = s // tq
    reps = tk // LANES
    scale = QK_HEAD ** -0.5
    nt = (((1,), (1,)), ((), ()))
    tn = (((0,), (0,)), ((), ()))

    def body(q_ref, kn_ref, v_ref, kpe_ref, o_ref, do_ref, lse_ref, c_ref, sa_ref, sb_ref, dqpre_ref, dkv_ref,
             dkpe_ref, kf_ref, vb_ref, dk_ref, dv_ref, dq_ref):
        j = pl.program_id(1)

        @pl.when(j == 0)
        def _():
            dq_ref[...] = jnp.zeros_like(dq_ref)

        kf_ref[:, 0:QK_NOPE] = kn_ref[...].astype(BF16)
        kf_ref[:, QK_NOPE:HEAD_PAD] = kpe_ref[...]
        vb_ref[...] = v_ref[...].astype(BF16)
        dk_ref[...] = jnp.zeros_like(dk_ref)
        dv_ref[...] = jnp.zeros_like(dv_ref)

        def block(i, diagonal):
            off = i * tq
            qb = q_ref[pl.ds(off, tq), :]
            dob = do_ref[pl.ds(off, tq), :]
            delta = jnp.sum(dob * o_ref[pl.ds(off, tq), :], axis=1, keepdims=True)
            sc = lax.dot_general(qb, kf_ref[...], nt, preferred_element_type=F32) * scale
            if diagonal:
                sc = _causal_mask(sc, 0, 0, tq, tk)
            p = jnp.exp(sc - jnp.tile(lse_ref[pl.ds(off, tq), :], (1, reps)))
            dob16 = dob.astype(BF16)
            dv_ref[...] += lax.dot_general(p.astype(BF16), dob16, tn, preferred_element_type=F32)
            dp = lax.dot_general(dob16, vb_ref[...], nt, preferred_element_type=F32)
            ds = (p * (dp - delta) * scale).astype(BF16)
            dq_ref[pl.ds(off, tq), :] += jnp.dot(ds, kf_ref[...], preferred_element_type=F32)
            dk_ref[...] += lax.dot_general(ds, qb, tn, preferred_element_type=F32)

        for kj in range(nq):
            @pl.when(j == kj)
            def _(kj=kj):
                block(kj, True)
                for i in range(kj + 1, nq):
                    block(i, False)

        dkv_ref[:, 0:QK_NOPE] = dk_ref[:, 0:QK_NOPE]
        dkv_ref[:, QK_NOPE:HEAD_PAD] = dv_ref[...]
        dkpe_ref[...] = dk_ref[:, QK_NOPE:HEAD_PAD]

        @pl.when(j == nq - 1)
        def _():
            dqpre_ref[:, 0:QK_NOPE] = dq_ref[:, 0:QK_NOPE].astype(BF16)
            dqpre_ref[:, QK_NOPE:HEAD_PAD] = _rope_t(dq_ref[:, QK_NOPE:HEAD_PAD], c_ref[...], sa_ref[...],
                                                     sb_ref[...]).astype(BF16)

    head_rows = lambda w: pl.BlockSpec((s, w), lambda h, j: (0, h))
    table = pl.BlockSpec((s, LANES), lambda h, j: (0, 0))
    return pl.pallas_call(
        body, name="attn_bwd", grid=(N_HEADS, s // tk),
        in_specs=[head_rows(HEAD_PAD),
                  pl.BlockSpec((tk, QK_NOPE), lambda h, j: (j, 2 * h)),
                  pl.BlockSpec((tk, V_HEAD), lambda h, j: (j, 2 * h + 1)),
                  pl.BlockSpec((tk, LANES), lambda h, j: (j, 0)),
                  head_rows(V_HEAD), head_rows(V_HEAD), head_rows(LANES), table, table, table],
        out_specs=[head_rows(HEAD_PAD),
                   pl.BlockSpec((tk, HEAD_PAD), lambda h, j: (j, h)),
                   pl.BlockSpec((tk, LANES), lambda h, j: (j, h))],
        out_shape=[jax.ShapeDtypeStruct((s, N_HEADS * HEAD_PAD), BF16),
                   jax.ShapeDtypeStruct((s, N_HEADS * HEAD_PAD), F32),
                   jax.ShapeDtypeStruct((s, N_HEADS * LANES), F32)],
        scratch_shapes=[pltpu.VMEM((tk, HEAD_PAD), BF16), pltpu.VMEM((tk, V_HEAD), BF16),
                        pltpu.VMEM((tk, HEAD_PAD), F32), pltpu.VMEM((tk, V_HEAD), F32),
                        pltpu.VMEM((s, HEAD_PAD), F32)],
        compiler_params=_params(("parallel", "arbitrary")),
    )(q, kv, kv, kpe, o, do, lse, *tabs)


def _mix_fwd(u1, lng, lnb, gcon, attn, gattn):
    s, c = u1.shape
    ac = attn.shape[1]
    ts = _pick(s, (256, 128))

    def body(u_ref, lg_ref, lb_ref, gc_ref, a_ref, ga_ref, o_ref):
        t3 = _silu(_ln(u_ref[...], lg_ref[...], lb_ref[...]))
        o_ref[:, 0:c] = _rms(t3, gc_ref[...]).astype(BF16)
        o_ref[:, c:c + ac] = _rms(a_ref[...], ga_ref[...]).astype(BF16)

    return pl.pallas_call(
        body, name="mix_fwd", grid=(s // ts,),
        in_specs=[_rows(ts, c), _vec(c), _vec(c), _vec(c), _rows(ts, ac), _vec(ac)],
        out_specs=_rows(ts, c + ac),
        out_shape=jax.ShapeDtypeStruct((s, c + ac), BF16),
        compiler_params=_params(("parallel",)),
    )(u1, lng, lnb, gcon, attn, gattn)


def _mix_bwd(dmixin, u1, lng, lnb, gcon, attn, gattn):
    s, c = u1.shape
    ac = attn.shape[1]
    ts = _pick(s, (256, 128))

    def body(d_ref, u_ref, lg_ref, lb_ref, gc_ref, a_ref, ga_ref,
             du_ref, da_ref, dlg_ref, dlb_ref, dgc_ref, dga_ref):
        _acc_init(pl.program_id(0), dlg_ref, dlb_ref, dgc_ref, dga_ref)
        u = u_ref[...]
        t2 = _ln(u, lg_ref[...], lb_ref[...])
        dt3, dgc = _rms_bwd(_silu(t2), gc_ref[...], d_ref[:, 0:c])
        du, dlg, dlb = _ln_bwd(u, lg_ref[...], dt3 * _silu_grad(t2))
        du_ref[...] = du
        dlg_ref[...] += dlg
        dlb_ref[...] += dlb
        dgc_ref[...] += dgc
        da, dga = _rms_bwd(a_ref[...], ga_ref[...], d_ref[:, c:c + ac])
        da_ref[...] = da
        dga_ref[...] += dga

    return pl.pallas_call(
        body, name="mix_bwd", grid=(s // ts,),
        in_specs=[_rows(ts, c + ac), _rows(ts, c), _vec(c), _vec(c), _vec(c), _rows(ts, ac), _vec(ac)],
        out_specs=[_rows(ts, c), _rows(ts, ac), _vec(c), _vec(c), _vec(c), _vec(ac)],
        out_shape=[jax.ShapeDtypeStruct((s, c), F32), jax.ShapeDtypeStruct((s, ac), F32),
                   jax.ShapeDtypeStruct((1, c), F32), jax.ShapeDtypeStruct((1, c), F32),
                   jax.ShapeDtypeStruct((1, c), F32), jax.ShapeDtypeStruct((1, ac), F32)],
        compiler_params=_params(("arbitrary",)),
    )(dmixin, u1, lng, lnb, gcon, attn, gattn)


def _post_mix_fwd(x, mix, gpost, gpre):
    s, d = x.shape
    ts = _pick(s, (256, 128))

    def body(x_ref, m_ref, gp_ref, gf_ref, x1_ref, hf_ref):
        x1 = x_ref[...] + _rms(m_ref[...], gp_ref[...])
        x1_ref[...] = x1
        hf_ref[...] = _rms(x1, gf_ref[...]).astype(BF16)

    return pl.pallas_call(
        body, name="post_mix_fwd", grid=(s // ts,),
        in_specs=[_rows(ts, d), _rows(ts, d), _vec(d), _vec(d)],
        out_specs=[_rows(ts, d), _rows(ts, d)],
        out_shape=[jax.ShapeDtypeStruct((s, d), F32), jax.ShapeDtypeStruct((s, d), BF16)],
        compiler_params=_params(("parallel",)),
    )(x, mix, gpost, gpre)


def _post_mix_bwd(dy, dhf, x1, gpre, mix, gpost):
    s, d = x1.shape
    ts = _pick(s, (256, 128))

    def body(dy_ref, dh_ref, x1_ref, gf_ref, m_ref, gp_ref, dx1_ref, dm_ref, dgf_ref, dgp_ref):
        _acc_init(pl.program_id(0), dgf_ref, dgp_ref)
        dxa, dgf = _rms_bwd(x1_ref[...], gf_ref[...], dh_ref[...])
        dx1 = dy_ref[...] + dxa
        dx1_ref[...] = dx1
        dgf_ref[...] += dgf
        dm, dgp = _rms_bwd(m_ref[...], gp_ref[...], dx1)
        dm_ref[...] = dm.astype(BF16)
        dgp_ref[...] += dgp

    return pl.pallas_call(
        body, name="post_mix_bwd", grid=(s // ts,),
        in_specs=[_rows(ts, d), _rows(ts, d), _rows(ts, d), _vec(d), _rows(ts, d), _vec(d)],
        out_specs=[_rows(ts, d), _rows(ts, d), _vec(d), _vec(d)],
        out_shape=[jax.ShapeDtypeStruct((s, d), F32), jax.ShapeDtypeStruct((s, d), BF16),
                   jax.ShapeDtypeStruct((1, d), F32), jax.ShapeDtypeStruct((1, d), F32)],
        compiler_params=_params(("arbitrary",)),
    )(dy, dhf, x1, gpre, mix, gpost)


def _ffn_up(hf, wg, wu, slabs, name, earlier=None):
    s, d = hf.shape
    nsh, fs, _ = wg.shape
    tm = _pick(s, (1024, 512, 256, 128))
    nt = (((1,), (1,)), ((), ()))

    def body(slab_ref, h_ref, wg_ref, wu_ref, *rest):
        dg_ref, du_ref, a_ref = rest[-3:]
        h = h_ref[...]
        g = lax.dot_general(h, wg_ref[...], nt, preferred_element_type=F32)
        u = lax.dot_general(h, wu_ref[...], nt, preferred_element_type=F32)
        sg = _sigmoid(g)
        silu = g * sg
        dg_ref[...] = u * (sg * (1.0 + g * (1.0 - sg)))
        du_ref[...] = silu
        a_ref[...] = (silu * u).astype(BF16)

    w_spec = pl.BlockSpec((None, fs, d), lambda i, j, slab_ref: (slab_ref[j], 0, 0))
    o_spec = pl.BlockSpec((None, tm, fs), lambda i, j, slab_ref: (slab_ref[j], i, 0))
    earlier = [] if earlier is None else list(earlier)
    return pl.pallas_call(
        body, name=name,
        grid_spec=pltpu.PrefetchScalarGridSpec(
            num_scalar_prefetch=1, grid=(s // tm, slabs.shape[0]),
            in_specs=[pl.BlockSpec((tm, d), lambda i, j, slab_ref: (i, 0)), w_spec, w_spec]
            + [pl.BlockSpec(memory_space=pl.ANY)] * len(earlier),
            out_specs=[o_spec] * 3),
        out_shape=[jax.ShapeDtypeStruct((nsh, s, fs), F32)] * 2 + [jax.ShapeDtypeStruct((nsh, s, fs), BF16)],
        input_output_aliases={4 + k: k for k in range(len(earlier))},
        compiler_params=_params(("parallel", "parallel")),
    )(slabs, hf, wg, wu, *earlier)


def _ffn_down(acts, ws, name):
    n = len(acts)
    nsh, s, fs = acts[0].shape
    d = ws[0].shape[2]
    tm = _pick(s, (1024, 512, 256, 128))
    tn = _pick(d, (256, 128))
    a_mode = pl.Buffered(1)

    def body(*refs):
        acc = None
        for a_ref, w_ref in zip(refs[:n], refs[n:2 * n]):
            for j in range(nsh):
                part = jnp.dot(a_ref[j], w_ref[j], preferred_element_type=F32)
                acc = part if acc is None else acc + part
        refs[-1][...] = acc

    return pl.pallas_call(
        body, name=name, grid=(s // tm, d // tn),
        in_specs=[pl.BlockSpec((nsh, tm, fs), lambda i, j: (0, i, 0), pipeline_mode=a_mode)] * n
        + [pl.BlockSpec((nsh, fs, tn), lambda i, j: (0, 0, j))] * n,
        out_specs=pl.BlockSpec((tm, tn), lambda i, j: (i, j)),
        out_shape=jax.ShapeDtypeStruct((s, d), F32),
        compiler_params=_params(("parallel", "parallel")),
    )(*acts, *ws)


def _ffn_down_bwd(dff, wd, act_dgate, act_dup, behind):
    s, d = dff.shape
    nsh, fs, _ = wd.shape
    tm = _pick(s, (1024, 512, 256, 128))
    nt = (((1,), (1,)), ((), ()))

    def body(d_ref, w_ref, pg_ref, pu_ref, _, dg_ref, du_ref):
        dact = lax.dot_general(d_ref[...], w_ref[...], nt, preferred_element_type=F32)
        dg_ref[...] = (dact * pg_ref[...]).astype(BF16)
        du_ref[...] = (dact * pu_ref[...]).astype(BF16)

    h_spec = pl.BlockSpec((None, tm, fs), lambda i, j: (j, i, 0))
    return pl.pallas_call(
        body, name="ffn_down_bwd", grid=(s // tm, nsh),
        in_specs=[pl.BlockSpec((tm, d), lambda i, j: (i, 0)),
                  pl.BlockSpec((None, fs, d), lambda i, j: (j, 0, 0)), h_spec, h_spec,
                  pl.BlockSpec((8, LANES), lambda i, j: (0, 0))],
        out_specs=[h_spec] * 2,
        out_shape=[jax.ShapeDtypeStruct((nsh, s, fs), BF16)] * 2,
        compiler_params=_params(("parallel", "parallel")),
    )(dff, wd, act_dgate, act_dup, behind)


def _ffn_dw(hiddens, other, name):
    n = len(hiddens)
    nsh, s, fs = hiddens[0].shape
    d = other.shape[1]
    tn = (((0,), (0,)), ((), ()))

    def body(*refs):
        for a_ref, o_ref in zip(refs[:n], refs[n + 1:]):
            o_ref[...] = lax.dot_general(a_ref[...], refs[n][...], tn, preferred_element_type=F32).astype(BF16)

    return pl.pallas_call(
        body, name=name, grid=(nsh,),
        in_specs=[pl.BlockSpec((None, s, fs), lambda j: (j, 0, 0))] * n + [pl.BlockSpec((s, d), lambda j: (0, 0))],
        out_specs=[pl.BlockSpec((None, fs, d), lambda j: (j, 0, 0))] * n,
        out_shape=[jax.ShapeDtypeStruct((nsh, fs, d), BF16)] * n,
        compiler_params=_params(("parallel",)),
    )(*hiddens, other)


def _ffn_down_final(act, wd, x1, tgt, g):
    nsh, s, fs = act.shape
    d = x1.shape[1]
    ts = _pick(s, (256, 128))

    def body(a_ref, w_ref, x1_ref, t_ref, g_ref, loss_ref, dy_ref, dff_ref, dg_ref):
        _acc_init(pl.program_id(0), loss_ref, dg_ref)
        ff_v = jnp.dot(a_ref[0], w_ref[0], preferred_element_type=F32)
        for j in range(1, nsh):
            ff_v = ff_v + jnp.dot(a_ref[j], w_ref[j], preferred_element_type=F32)
        err = x1_ref[...] + _rms(ff_v, g_ref[...]) - t_ref[...]
        tok = jnp.mean(err * err, axis=-1, keepdims=True)
        loss_ref[...] += 0.5 * jnp.sum(tok, axis=0, keepdims=True)
        dy = err * (1.0 / d)
        dy_ref[...] = dy
        dff, dg = _rms_bwd(ff_v, g_ref[...], dy)
        dff_ref[...] = dff.astype(BF16)
        dg_ref[...] += dg

    return pl.pallas_call(
        body, name="ffn_down_final", grid=(s // ts,),
        in_specs=[pl.BlockSpec((nsh, ts, fs), lambda i: (0, i, 0)),
                  pl.BlockSpec((nsh, fs, d), lambda i: (0, 0, 0), pipeline_mode=pl.Buffered(1)),
                  _rows(ts, d), _rows(ts, d), _vec(d)],
        out_specs=[_vec(LANES), _rows(ts, d), _rows(ts, d), _vec(d)],
        out_shape=[jax.ShapeDtypeStruct((1, LANES), F32), jax.ShapeDtypeStruct((s, d), F32),
                   jax.ShapeDtypeStruct((s, d), BF16), jax.ShapeDtypeStruct((1, d), F32)],
        compiler_params=_params(("arbitrary",)),
    )(act, wd, x1, tgt, g)


def _pre_bwd(dx1, dh, x, g):
    s, d = x.shape
    ts = _pick(s, (256, 128))

    def body(dx1_ref, dh_ref, x_ref, g_ref, dx_ref, dg_ref):
        _acc_init(pl.program_id(0), dg_ref)
        dxa, dg = _rms_bwd(x_ref[...], g_ref[...], dh_ref[...])
        dx_ref[...] = dx1_ref[...] + dxa
        dg_ref[...] += dg

    return pl.pallas_call(
        body, name="pre_bwd", grid=(s // ts,),
        in_specs=[_rows(ts, d), _rows(ts, d), _rows(ts, d), _vec(d)],
        out_specs=[_rows(ts, d), _vec(d)],
        out_shape=[jax.ShapeDtypeStruct((s, d), F32), jax.ShapeDtypeStruct((1, d), F32)],
        compiler_params=_params(("arbitrary",)),
    )(dx1, dh, x, g)


def _local_step(x, pos, tgt, vecs, in_weights_fn, mix_weights_fn, up_weights_fn, down_weights_fn, grads_fn):
    c = vecs["conv_b"].shape[1]
    ql = vecs["q_norm"].shape[1]
    kvl = vecs["kv_norm"].shape[1]
    half = jnp.arange(0, QK_ROPE, 2, dtype=F32)
    freq = ROPE_THETA ** (-half / QK_ROPE)
    inv_freq = jnp.concatenate([freq, freq, jnp.zeros((LANES - QK_ROPE,), F32)])[None, :]
    tabs = _rope_tables(pos, inv_freq)

    h = _pre_fwd(x, vecs["pre_mix_norm"])
    w_in_t, zero = in_weights_fn(h)
    z = _mm(h, w_in_t, "nt", "mm_z")
    w_uq_t, w_ukv, conv_w, w_out = mix_weights_fn(z)
    u0, qn, kvn, kpe = _split_fwd(z, vecs["q_norm"] + zero, vecs["kv_norm"], tabs, c, ql, kvl)
    u1 = _conv_fwd(u0, conv_w, vecs["conv_b"])
    q = _q_rope(_mm(qn, w_uq_t, "nt", "mm_q"), tabs)
    kv = _mm(kvn, w_ukv, "nn", "mm_kv")
    attn, lse = _attn_fwd(q, kv, kpe)
    mixin = _mix_fwd(u1, vecs["conv_ln_g"], vecs["conv_ln_b"], vecs["conv_out_norm"], attn, vecs["attn_out_norm"])
    mix = _mm(mixin, w_out, "nn", "mm_mix")
    x1, hf = _post_mix_fwd(x, mix, vecs["post_mix_norm"], vecs["pre_ffn_norm"])
    w_gate, w_up, slabs, rest_fn = up_weights_fn(hf)
    part = _ffn_up(hf, w_gate, w_up, slabs, "ffn_up_first")
    w_gate, w_up, slabs = rest_fn(part[2])
    act_dgate, act_dup, act = _ffn_up(hf, w_gate, w_up, slabs, "ffn_up_rest", earlier=part)
    w_down = down_weights_fn(act)
    loss, dy, dff, d_post_ffn = _ffn_down_final(act, w_down, x1, tgt, vecs["post_ffn_norm"])

    g = {"post_ffn_norm": d_post_ffn}
    zero = grads_fn("down", ("w_down",), _ffn_dw([act], dff, "ffn_dw_down"), dff)
    dgate, dup = _ffn_down_bwd(dff, w_down, act_dgate, act_dup, jnp.zeros((8, LANES), F32) + zero)
    zero = zero + grads_fn("up", ("w_gate", "w_up"), _ffn_dw([dgate, dup], hf, "ffn_dw_up"), dgate)
    dhf = _ffn_down([dgate, dup], [w_gate, w_up], "ffn_dhf")
    dx1, dmix, g["pre_ffn_norm"], g["post_mix_norm"] = _post_mix_bwd(
        dy, dhf, x1, vecs["pre_ffn_norm"] + zero, mix, vecs["post_mix_norm"])
    dmixin = _mm(dmix, w_out, "nt", "mm_dmixin")
    dw_out = _mm(mixin, dmix, "tn", "mm_dw_out", BF16)
    zero = grads_fn("out", ("w_out",), [dw_out], dmix)
    du1, dattn, g["conv_ln_g"], g["conv_ln_b"], g["conv_out_norm"], g["attn_out_norm"] = _mix_bwd(
        dmixin, u1, vecs["conv_ln_g"] + zero, vecs["conv_ln_b"], vecs["conv_out_norm"], attn, vecs["attn_out_norm"])
    du0, dw_conv, g["conv_b"] = _conv_bwd(du1, u0, conv_w)
    dqpre, dkv, dkpe_h = _attn_bwd(q, kv, kpe, attn, dattn, lse, tabs)
    dqn = _mm(dqpre, w_uq_t, "nn", "mm_dqn")
    dw_uq = _mm(dqpre, qn, "tn", "mm_dw_uq", BF16)
    dkvn = _mm(dkv, w_ukv, "nt", "mm_dkvn")
    dw_ukv = _mm(kvn, dkv, "tn", "mm_dw_ukv", BF16)
    zero = grads_fn("", (), [], dkvn)
    dz, g["q_norm"], g["kv_norm"] = _split_bwd(du0, z, dqn, dkvn, dkpe_h, vecs["q_norm"] + zero, vecs["kv_norm"], tabs,
                                               c, ql, kvl)
    dw_in = _mm(dz, h, "tn", "mm_dw_in", BF16)
    zero = grads_fn("in", ("w_in", "w_uq", "w_ukv", "conv_w"), [dw_in, dw_uq, dw_ukv, dw_conv], dz)
    dh = _mm(dz, w_in_t, "nn", "mm_dh")
    zero = zero + grads_fn("", (), [], dh)
    grad_x, g["pre_mix_norm"] = _pre_bwd(dx1, dh, x, vecs["pre_mix_norm"] + zero)
    return loss, grad_x, g


def _my_index():
    return 4 * lax.axis_index("x") + 2 * lax.axis_index("y") + lax.axis_index("c")


def _coords(idx):
    return ((idx >> 2) & 1, (idx >> 1) & 1, idx & 1)


def _place():
    x, y, c = lax.axis_index("x"), lax.axis_index("y"), lax.axis_index("c")
    return (x, y, c), (x, y, 1 - c), [(1 - x, y), (x, 1 - y), (1 - x, 1 - y)]


def _small_copies(src, land, send_sem, recv_sem):
    me = _my_index()
    return [pltpu.make_async_remote_copy(src_ref=src, dst_ref=land.at[me], send_sem=send_sem, recv_sem=recv_sem,
                                         device_id=_coords(me ^ p), device_id_type=MESH)
            for p in range(1, N_DEV)]


HBM_SPEC = pl.BlockSpec(memory_space=pltpu.HBM)
SEM_SPEC = pl.BlockSpec(memory_space=pltpu.SEMAPHORE)
DATAFLOW = pltpu.SideEffectType.DATAFLOW_SIDE_EFFECTING


def _split_start(name, copies_of, srcs, lands, after):
    n = len(srcs)

    def body(*refs):
        outs = refs[2 * n + 1:]
        for k in range(n):
            for cp in copies_of(refs[k], refs[n + k], outs[k], outs[n + k]):
                cp.start()
        outs[-1][...] = jnp.zeros_like(outs[-1])

    hbm = lambda a: pltpu.HBM(a.shape, a.dtype)
    out = pl.pallas_call(
        body, name=name,
        in_specs=[HBM_SPEC] * (2 * n + 1),
        out_specs=[SEM_SPEC] * (2 * n) + [HBM_SPEC] * (2 * n) + [pl.BlockSpec(memory_space=pltpu.VMEM)],
        out_shape=[pltpu.SemaphoreType.DMA(())] * (2 * n) + [hbm(a) for a in srcs] + [hbm(a) for a in lands]
        + [jax.ShapeDtypeStruct((8, LANES), F32)],
        input_output_aliases={k: 2 * n + k for k in range(2 * n)},
        compiler_params=pltpu.CompilerParams(has_side_effects=DATAFLOW),
    )(*[pltpu.with_memory_space_constraint(a, pltpu.HBM) for a in list(srcs) + list(lands) + [after]])
    return (out[:n], out[n:2 * n], out[2 * n:3 * n], out[3 * n:4 * n]), out[-1][0, 0]


def _split_wait(name, n_copies, started, after):
    send_sems, recv_sems, srcs, lands = started
    n = len(srcs)

    def body(*refs):
        for k in range(n):
            slots = refs[n + k].at[pl.ds(0, n_copies)]
            all_copies = pltpu.make_async_remote_copy(
                src_ref=slots, dst_ref=slots, send_sem=refs[2 * n + k], recv_sem=refs[3 * n + k],
                device_id=_place()[0], device_id_type=MESH)
            all_copies.wait_send()
            all_copies.wait_recv()

    hbm = lambda a: pltpu.HBM(a.shape, a.dtype)
    out = pl.pallas_call(
        body, name=name,
        in_specs=[HBM_SPEC] * (2 * n) + [SEM_SPEC] * (2 * n) + [HBM_SPEC],
        out_specs=[HBM_SPEC] * (2 * n),
        out_shape=[hbm(a) for a in srcs] + [hbm(a) for a in lands],
        input_output_aliases={k: k for k in range(2 * n)},
        compiler_params=pltpu.CompilerParams(has_side_effects=DATAFLOW),
    )(*srcs, *lands, *send_sems, *recv_sems, pltpu.with_memory_space_constraint(after, pltpu.HBM))
    return out[:n], out[n:]


def _slot(chip, core):
    return 4 * chip[0] + 2 * chip[1] + core


def _gather_copies(src, land, send_sem, recv_sem):
    (x, y, c), sib, chips = _place()
    return [pltpu.make_async_remote_copy(src_ref=src, dst_ref=land.at[_slot((x, y), c)], send_sem=send_sem,
                                         recv_sem=recv_sem, device_id=to, device_id_type=MESH)
            for to in [sib] + [(*chip, c) for chip in chips]]


def _pass_on_copies(src, land, send_sem, recv_sem):
    (x, y, c), sib, chips = _place()
    return [pltpu.make_async_remote_copy(src_ref=land.at[_slot(chip, c)], dst_ref=land.at[_slot(chip, c)],
                                         send_sem=send_sem, recv_sem=recv_sem, device_id=sib, device_id_type=MESH)
            for chip in chips]


def _gather_pass_on(lands, name):
    n = len(lands)

    def body(*refs):
        ins, outs = refs[:n], refs[n:2 * n]
        send_sems, recv_sems = refs[2 * n:]
        (x, y, c), sib, chips = _place()
        sends = []
        for k in range(n):
            for j, chip in enumerate(chips):
                sends.append(pltpu.make_async_remote_copy(
                    src_ref=ins[k].at[_slot(chip, c)], dst_ref=outs[k].at[_slot(chip, c)],
                    send_sem=send_sems.at[k, j], recv_sem=recv_sems.at[k, j], device_id=sib, device_id_type=MESH))
        for cp in sends:
            cp.start()
        for cp in sends:
            cp.wait_recv()
        for cp in sends:
            cp.wait_send()

    any_spec = pl.BlockSpec(memory_space=pl.ANY)
    return pl.pallas_call(
        body, name=name,
        in_specs=[any_spec] * n, out_specs=[any_spec] * n,
        out_shape=[jax.ShapeDtypeStruct(a.shape, a.dtype) for a in lands],
        input_output_aliases={k: k for k in range(n)},
        scratch_shapes=[pltpu.SemaphoreType.DMA((n, 3))] * 2,
        compiler_params=pltpu.CompilerParams(has_side_effects=True),
    )(*lands)


def _chip_copies(src, land, send_sem, recv_sem):
    (x, y, c), _, chips = _place()
    return [pltpu.make_async_remote_copy(src_ref=src.at[j], dst_ref=land.at[j], send_sem=send_sem,
                                         recv_sem=recv_sem, device_id=(*chip, c), device_id_type=MESH)
            for j, chip in enumerate(chips)]


ROW_TILE_BYTES = 14 * 1024 * 1024


def _stream_tile(r, c, bytes_per_elem):
    if r * c * bytes_per_elem <= ROW_TILE_BYTES:
        return r, c
    rows = [t for t in range(16, r, 16) if r % t == 0 and t * c * bytes_per_elem <= ROW_TILE_BYTES]
    if rows:
        return max(rows), c
    cols = [t for t in range(LANES, c, LANES) if c % t == 0 and r * t * bytes_per_elem <= ROW_TILE_BYTES]
    return r, max(cols)


def _sibling_copies(src, land, send_sem, recv_sem):
    (x, y, c), sib, _ = _place()
    return [pltpu.make_async_remote_copy(src_ref=src.at[2 * q + 1 - c], dst_ref=land.at[q], send_sem=send_sem,
                                         recv_sem=recv_sem, device_id=sib, device_id_type=MESH)
            for q in range(4)]


def _pair_sum(slots, blocks, theirs, name):
    _, r, c = theirs.shape
    tr, tc = _stream_tile(r, c, 3 * theirs.dtype.itemsize)

    def body(slot_ref, a_ref, b_ref, o_ref):
        o_ref[...] = (a_ref[...].astype(F32) + b_ref[...].astype(F32)).astype(o_ref.dtype)

    return pl.pallas_call(
        body, name=name,
        grid_spec=pltpu.PrefetchScalarGridSpec(
            num_scalar_prefetch=1, grid=(3, r // tr, c // tc),
            in_specs=[pl.BlockSpec((1, tr, tc), lambda j, a, b, slot_ref: (slot_ref[j], a, b)),
                      pl.BlockSpec((1, tr, tc), lambda j, a, b, slot_ref: (slot_ref[3 + j], a, b))],
            out_specs=pl.BlockSpec((1, tr, tc), lambda j, a, b, slot_ref: (j, a, b))),
        out_shape=jax.ShapeDtypeStruct((3, r, c), theirs.dtype),
        compiler_params=_params(("parallel", "parallel", "parallel")),
    )(slots, blocks, theirs)


def _adamw_small(parts, ws, ms, vs):
    n = len(ws)
    c1 = 1.0 - ADAM_B1
    c2 = 1.0 - ADAM_B2
    bc1 = 1.0 - ADAM_B1 ** ADAM_STEP
    bc2 = 1.0 - ADAM_B2 ** ADAM_STEP

    def body(*refs):
        p_ref, outs = refs[0], refs[1 + 3 * n:]
        off = 0
        for k in range(n):
            w_ref, m_ref, v_ref = refs[1 + k], refs[1 + n + k], refs[1 + 2 * n + k]
            width = w_ref.shape[1]
            g = p_ref[0, :, off:off + width]
            for j in range(1, N_DEV):
                g = g + p_ref[j, :, off:off + width]
            nm = ADAM_B1 * m_ref[...] + c1 * g
            nv = ADAM_B2 * v_ref[...] + c2 * (g * g)
            outs[k][...] = g
            outs[n + k][...] = -ADAM_LR * ((nm / bc1) / (jnp.sqrt(nv / bc2) + ADAM_EPS) + ADAM_WD * w_ref[...])
            outs[2 * n + k][...] = nm
            outs[3 * n + k][...] = nv
            off += width
        total = p_ref[0, :, off:off + LANES]
        for j in range(1, N_DEV):
            total = total + p_ref[j, :, off:off + LANES]
        outs[4 * n][...] = total

    out = pl.pallas_call(
        body, name="adamw_small",
        out_shape=[jax.ShapeDtypeStruct(a.shape, F32) for a in ws] * 4 + [jax.ShapeDtypeStruct((1, LANES), F32)],
        compiler_params=pltpu.CompilerParams(vmem_limit_bytes=VMEM_LIMIT),
    )(parts, *ws, *ms, *vs)
    return out[:n], out[n:2 * n], out[2 * n:3 * n], out[3 * n:4 * n], out[4 * n]


def _reduce_adamw(parts, w, m, v, name, blocks, theirs, slots):
    r, c = w.shape
    n_parts = parts.shape[0]
    tr, tc = _stream_tile(r, c, (n_parts + 2) * parts.dtype.itemsize + 7 * 4)
    c1 = 1.0 - ADAM_B1
    c2 = 1.0 - ADAM_B2
    bc1 = 1.0 - ADAM_B1 ** ADAM_STEP
    bc2 = 1.0 - ADAM_B2 ** ADAM_STEP

    def body(_, b_ref, t_ref, p_ref, w_ref, m_ref, v_ref, g_ref, d_ref, nm_ref, nv_ref):
        g = b_ref[0].astype(F32) + t_ref[0].astype(F32)
        for j in range(n_parts):
            g = g + p_ref[j].astype(F32)
        nm = ADAM_B1 * m_ref[...] + c1 * g
        nv = ADAM_B2 * v_ref[...] + c2 * (g * g)
        g_ref[...] = g
        nm_ref[...] = nm
        nv_ref[...] = nv
        d_ref[...] = -ADAM_LR * ((nm / bc1) / (jnp.sqrt(nv / bc2) + ADAM_EPS) + ADAM_WD * w_ref[...])

    out = jax.ShapeDtypeStruct((r, c), F32)
    grid = (r // tr, c // tc)
    blk = pl.BlockSpec((tr, tc), lambda i, j, slot_ref: (i, j))
    return pl.pallas_call(
        body, name=name,
        grid_spec=pltpu.PrefetchScalarGridSpec(
            num_scalar_prefetch=1, grid=grid,
            in_specs=[pl.BlockSpec((1, tr, tc), lambda i, j, slot_ref: (slot_ref[0], i, j)),
                      pl.BlockSpec((1, tr, tc), lambda i, j, slot_ref: (slot_ref[1], i, j)),
                      pl.BlockSpec((n_parts, tr, tc), lambda i, j, slot_ref: (0, i, j)), blk, blk, blk],
            out_specs=[blk] * 4),
        out_shape=[out] * 4,
        compiler_params=_params(("parallel", "parallel")),
    )(slots, blocks, theirs, parts, w, m, v)


_MIX = ("w_in", "w_uq", "w_ukv", "conv_w", "w_out")
_FFN = ("w_gate", "w_up", "w_down")
_BIG = _MIX + _FFN
_TRANSPOSED = ("w_in", "w_uq", "w_gate", "w_up")
_SMALL = ("pre_mix_norm", "q_norm", "kv_norm", "conv_b", "conv_ln_g", "conv_ln_b", "conv_out_norm",
          "attn_out_norm", "post_mix_norm", "pre_ffn_norm", "post_ffn_norm")
_ORDER = ("pre_mix_norm", "w_in", "q_norm", "w_uq", "kv_norm", "w_ukv", "conv_w", "conv_b", "conv_ln_g",
          "conv_ln_b", "conv_out_norm", "attn_out_norm", "w_out", "post_mix_norm", "pre_ffn_norm", "w_gate",
          "w_up", "w_down", "post_ffn_norm")


def _cols_from_shards(g):
    return jnp.transpose(g, (1, 0, 2)).reshape(g.shape[1], N_DEV * g.shape[2])


def _cols_to_shards(w):
    k, n8 = w.shape
    return jnp.transpose(w.reshape(k, N_DEV, n8 // N_DEV), (1, 0, 2))


def _step(x, positions, loss_target, w, m, v):
    s, d = x.shape[1], x.shape[2]
    x2, tgt = x[0], loss_target[0]
    pos = positions.reshape(s, 1)
    vecs = {n: w[n] for n in _SMALL}
    core = lax.axis_index("c").astype(jnp.int32)
    my_chip = (2 * lax.axis_index("x") + lax.axis_index("y")).astype(jnp.int32)
    other_chips = [my_chip ^ 2, my_chip ^ 1, my_chip ^ 3]
    pair_slots = jnp.stack([2 * q + core for q in other_chips] + other_chips)
    own_slots = jnp.stack([2 * my_chip + core, my_chip])
    own = {}
    n_in_cols = N_DEV * w["w_in"].shape[2]
    gathers, scatters, to_sibling = {}, {}, []

    def shard(t, n):
        return t[n][0].T if n in _TRANSPOSED else t[n][0]

    def gather_start(names, tag, after, zero=0.0):
        srcs = [w[n][0] if n == "conv_w" else (shard(w, n) + zero).astype(BF16) for n in names]
        lands = [lax.empty((N_DEV,) + a.shape, a.dtype) for a in srcs]
        gathers[tag], zero = _split_start("gather_" + tag + "_start", _gather_copies, srcs, lands, after)
        return zero

    def gather_finish(names, tag, after):
        srcs, lands = _split_wait("gather_" + tag + "_wait", 4, gathers[tag], after)
        lands = _gather_pass_on(lands, "gather_" + tag + "_pass_on")
        me = _my_index()
        return {n: lax.dynamic_update_slice(g, a[None], (me,) + (0,) * a.ndim) for n, g, a in zip(names, lands, srcs)}

    def in_weights_fn(h):
        w_in_g = gather_finish(("w_in",), "in", h)["w_in"]
        return jnp.pad(w_in_g.reshape(-1, d), ((0, LANES - QK_ROPE), (0, 0))), 0.0

    def mix_weights_fn(z):
        gath = gather_finish(_MIX[1:], "mix", z)
        w_uq_t = jnp.pad(gath["w_uq"], ((0, 0), (0, HEAD_PAD - QK_HEAD), (0, 0))).reshape(N_HEADS * HEAD_PAD, -1)
        return (w_uq_t, _cols_from_shards(gath["w_ukv"]), _cols_from_shards(gath["conv_w"]),
                gath["w_out"].reshape(-1, d))

    def up_weights_fn(hf):
        srcs, lands = _split_wait("gather_up_wait", 4, gathers["up"], hf)
        me = _my_index()
        lands = [lax.dynamic_update_slice(g, a[None], (me, 0, 0)) for g, a in zip(lands, srcs)]
        passing, _ = _split_start("gather_up_pass_on_start", _pass_on_copies, srcs, lands, hf)
        here = jnp.stack([me, me ^ 1] + [2 * q + core for q in other_chips])
        late = jnp.stack([2 * q + 1 - core for q in other_chips])

        def rest_fn(after):
            _, done = _split_wait("gather_up_pass_on_wait", 3, passing, after)
            return done[0], done[1], late

        return passing[3][0], passing[3][1], here, rest_fn

    def down_weights_fn(act):
        return gather_finish(("w_down",), "down", act)["w_down"]

    def scatter_advance(after):
        if not to_sibling:
            return 0.0
        names, tag, sent = to_sibling.pop()
        blocks, theirs = _split_wait("to_sibling_" + tag + "_wait", 4, sent, after)
        own.update(zip(names, zip(blocks, theirs)))
        pairs = [_pair_sum(pair_slots, b, t, "pair_sum_" + n) for n, b, t in zip(names, blocks, theirs)]
        lands = [lax.empty((3,) + p.shape[1:], p.dtype) for p in pairs]
        scatters[tag], zero = _split_start("scatter_" + tag + "_start", _chip_copies, pairs, lands, theirs[0])
        return zero

    to_blocks = {
        "w_in": lambda a: a[:n_in_cols].reshape(N_DEV, -1, d),
        "w_uq": lambda a: a.reshape(N_HEADS, HEAD_PAD, -1)[:, :QK_HEAD],
        "w_ukv": _cols_to_shards, "conv_w": _cols_to_shards,
        "w_out": lambda a: a.reshape(N_DEV, -1, d),
    }

    def grads_fn(tag, names, grads, after):
        zero = scatter_advance(after)
        if not grads:
            return zero
        blocks = [to_blocks.get(n, lambda a: a)(a) for n, a in zip(names, grads)]
        lands = [lax.empty((4,) + b.shape[1:], b.dtype) for b in blocks]
        sent, zero2 = _split_start("to_sibling_" + tag + "_start", _sibling_copies, blocks, lands, after)
        to_sibling.append((names, tag, sent))
        return zero + zero2

    zero = gather_start(("w_in",), "in", x2)
    zero = gather_start(_MIX[1:], "mix", x2, zero)
    zero = gather_start(("w_gate", "w_up"), "up", x2, zero)
    vecs["pre_mix_norm"] = vecs["pre_mix_norm"] + gather_start(("w_down",), "down", x2, zero)
    loss, grad_x, g = _local_step(x2, pos, tgt, vecs, in_weights_fn, mix_weights_fn, up_weights_fn, down_weights_fn,
                                  grads_fn)

    small = jnp.concatenate([g[n] for n in _SMALL] + [loss], axis=1)
    small_started, zero = _split_start("gather_small_start", _small_copies, [small],
                                       [lax.empty((N_DEV,) + small.shape, F32)], grad_x)

    res = {}
    after = grad_x
    own_slots = own_slots + zero.astype(jnp.int32)
    for tag, names in (("down", ("w_down",)), ("up", ("w_gate", "w_up")), ("out", ("w_out",)),
                       ("in", ("w_in", "w_uq", "w_ukv", "conv_w"))):
        _, recv = _split_wait("scatter_" + tag + "_wait", 3, scatters[tag], after)
        for n, parts in zip(names, recv):
            res[n] = _reduce_adamw(parts, shard(w, n), shard(m, n), shard(v, n), "adamw_" + n, *own[n], own_slots)
            after = res[n][1]
            res[n] = [(t.T if n in _TRANSPOSED else t)[None] for t in res[n]]
    (small,), (small_all,) = _split_wait("gather_small_wait", N_DEV - 1, small_started, after)
    small_all = lax.dynamic_update_slice(small_all, small[None], (_my_index(), 0, 0))
    *small_res, total = _adamw_small(small_all, *[[t[n] for n in _SMALL] for t in (w, m, v)])
    for k, n in enumerate(_SMALL):
        res[n] = [part[k] for part in small_res]

    outs = [total[0, 0], grad_x[None]]
    for part in range(4):
        outs.extend(res[n][part] for n in _ORDER)
    return tuple(outs)


def kernel(x, positions, pre_mix_norm, w_in, q_norm, w_uq, kv_norm, w_ukv, conv_w, conv_b, conv_ln_g, conv_ln_b, conv_out_norm, attn_out_norm, w_out, post_mix_norm, pre_ffn_norm, w_gate, w_up, w_down, post_ffn_norm, loss_target, m_pre_mix_norm, m_w_in, m_q_norm, m_w_uq, m_kv_norm, m_w_ukv, m_conv_w, m_conv_b, m_conv_ln_g, m_conv_ln_b, m_conv_out_norm, m_attn_out_norm, m_w_out, m_post_mix_norm, m_pre_ffn_norm, m_w_gate, m_w_up, m_w_down, m_post_ffn_norm, v_pre_mix_norm, v_w_in, v_q_norm, v_w_uq, v_kv_norm, v_w_ukv, v_conv_w, v_conv_b, v_conv_ln_g, v_conv_ln_b, v_conv_out_norm, v_attn_out_norm, v_w_out, v_post_mix_norm, v_pre_ffn_norm, v_w_gate, v_w_up, v_w_down, v_post_ffn_norm):
    w = dict(zip(_ORDER, (pre_mix_norm, w_in, q_norm, w_uq, kv_norm, w_ukv, conv_w, conv_b, conv_ln_g, conv_ln_b,
                          conv_out_norm, attn_out_norm, w_out, post_mix_norm, pre_ffn_norm, w_gate, w_up, w_down,
                          post_ffn_norm)))
    m = dict(zip(_ORDER, (m_pre_mix_norm, m_w_in, m_q_norm, m_w_uq, m_kv_norm, m_w_ukv, m_conv_w, m_conv_b,
                          m_conv_ln_g, m_conv_ln_b, m_conv_out_norm, m_attn_out_norm, m_w_out, m_post_mix_norm,
                          m_pre_ffn_norm, m_w_gate, m_w_up, m_w_down, m_post_ffn_norm)))
    v = dict(zip(_ORDER, (v_pre_mix_norm, v_w_in, v_q_norm, v_w_uq, v_kv_norm, v_w_ukv, v_conv_w, v_conv_b,
                          v_conv_ln_g, v_conv_ln_b, v_conv_out_norm, v_attn_out_norm, v_w_out, v_post_mix_norm,
                          v_pre_ffn_norm, v_w_gate, v_w_up, v_w_down, v_post_ffn_norm)))
    return _step(x, positions, loss_target, w, m, v)
```

```python
import functools

import jax
import jax.numpy as jnp
from jax import lax
from jax.experimental import pallas as pl
from jax.experimental.pallas import tpu as pltpu

N_DEV = 8
N_HEADS = 8
QK_NOPE = 128
QK_ROPE = 64
V_HEAD = 128
QK_HEAD = QK_NOPE + QK_ROPE
HEAD_PAD = 256
LANES = 128
ATTN_BLOCK = 512
CONV_K = 31
CONV_PAD = 32
EPS = 1e-6
ROPE_THETA = 10000.0
ADAM_LR = 0.001
ADAM_B1 = 0.9
ADAM_B2 = 0.999
ADAM_EPS = 1e-08
ADAM_WD = 0.01
ADAM_STEP = 10
VMEM_LIMIT = 56 * 1024 * 1024
F32 = jnp.float32
BF16 = jnp.bfloat16
MESH = pl.DeviceIdType.MESH
NEG = -1e30


def _pick(n, prefs):
    for p in prefs:
        if p <= n and n % p == 0:
            return p
    return n


def _params(sem):
    return pltpu.CompilerParams(dimension_semantics=sem, vmem_limit_bytes=VMEM_LIMIT)


_DIMS = {"nn": (((1,), (0,)), ((), ())), "nt": (((1,), (1,)), ((), ())), "tn": (((0,), (0,)), ((), ()))}


MM_VMEM_BUDGET = 40 * 1024 * 1024
MM_MAX_MACS = 3 * 1024 ** 3


V7X_HBM_BYTES_PER_S = 3.0e12
V7X_MXU_MACS_PER_S = 0.45e15
GRID_STEP_S = 0.35e-6


def _mm_tiles(m, n, k, size_a, size_b, size_o):
    best = None
    for tm in sorted({m, 1024, 512, 256, 128}, reverse=True):
        if tm > m or m % tm:
            continue
        for tn in sorted({n, 2048, 1024, 512, 384, 256, 128}, reverse=True):
            if tn > n or n % tn:
                continue
            vmem = 2 * (tm * k * size_a + k * tn * size_b + tm * tn * size_o)
            if vmem > MM_VMEM_BUDGET or tm * tn * k > MM_MAX_MACS:
                continue
            b_reads = 1 if tn == n else m // tm
            traffic = m * k * size_a + b_reads * k * n * size_b + m * n * size_o
            exposed = tm * k * size_a + k * tn * size_b + tm * tn * size_o
            steps = (m // tm) * (n // tn)
            key = (max(traffic / V7X_HBM_BYTES_PER_S, m * n * k / V7X_MXU_MACS_PER_S)
                   + exposed / V7X_HBM_BYTES_PER_S + steps * GRID_STEP_S)
            if best is None or key < best[0]:
                best = (key, tm, tn)
    assert best is not None, (m, n, k)
    return best[1], best[2]


def _mm(a, b, mode, name, out_dtype=F32, behind=None):
    if mode == "nn":
        (m, k), (k2, n) = a.shape, b.shape
    elif mode == "nt":
        (m, k), (n, k2) = a.shape, b.shape
    else:
        (k, m), (k2, n) = a.shape, b.shape
    assert k == k2, (a.shape, b.shape, mode)
    tm, tn = _mm_tiles(m, n, k, a.dtype.itemsize, b.dtype.itemsize, jnp.dtype(out_dtype).itemsize)
    dims = _DIMS[mode]

    def body(a_ref, b_ref, *rest):
        acc = lax.dot_general(a_ref[...].astype(BF16), b_ref[...].astype(BF16), dims, preferred_element_type=F32)
        rest[-1][...] = acc.astype(rest[-1].dtype)

    if mode == "tn":
        a_spec = pl.BlockSpec((k, tm), lambda i, j: (0, i))
    else:
        a_spec = pl.BlockSpec((tm, k), lambda i, j: (i, 0))
    if mode == "nt":
        b_spec = pl.BlockSpec((tn, k), lambda i, j: (j, 0))
    else:
        b_spec = pl.BlockSpec((k, tn), lambda i, j: (0, j))
    o_spec = pl.BlockSpec((tm, tn), lambda i, j: (i, j))
    extra = [] if behind is None else [behind]
    return pl.pallas_call(
        body, name=name,
        grid=(m // tm, n // tn),
        in_specs=[a_spec, b_spec] + [pl.BlockSpec((8, LANES), lambda i, j: (0, 0))] * len(extra),
        out_specs=o_spec,
        out_shape=jax.ShapeDtypeStruct((m, n), out_dtype),
        compiler_params=_params(("parallel", "parallel")),
    )(a, b, *extra)


def _sigmoid(x):
    return 1.0 / (1.0 + jnp.exp(-x))


def _rms(x, g):
    r = lax.rsqrt(jnp.mean(x * x, axis=-1, keepdims=True) + EPS)
    return (x * r) * g


def _rms_bwd(x, g, dy):
    r = lax.rsqrt(jnp.mean(x * x, axis=-1, keepdims=True) + EPS)
    xh = x * r
    dyg = dy * g
    dx = r * (dyg - xh * jnp.mean(dyg * xh, axis=-1, keepdims=True))
    return dx, jnp.sum(dy * xh, axis=0, keepdims=True)


def _ln(x, g, b):
    mu = jnp.mean(x, axis=-1, keepdims=True)
    xc = x - mu
    rs = lax.rsqrt(jnp.mean(xc * xc, axis=-1, keepdims=True) + EPS)
    return (xc * rs) * g + b


def _ln_bwd(x, g, dy):
    mu = jnp.mean(x, axis=-1, keepdims=True)
    xc = x - mu
    rs = lax.rsqrt(jnp.mean(xc * xc, axis=-1, keepdims=True) + EPS)
    xh = xc * rs
    dyg = dy * g
    dx = rs * (dyg - jnp.mean(dyg, axis=-1, keepdims=True) - xh * jnp.mean(dyg * xh, axis=-1, keepdims=True))
    return dx, jnp.sum(dy * xh, axis=0, keepdims=True), jnp.sum(dy, axis=0, keepdims=True)


def _silu(x):
    return x * _sigmoid(x)


def _silu_grad(x):
    s = _sigmoid(x)
    return s * (1.0 + x * (1.0 - s))


def _rope(x, cos, sa, sb):
    return x * cos + pltpu.roll(x, 96, 1) * sa + pltpu.roll(x, 32, 1) * sb


def _rope_t(d, cos, sa, sb):
    return d * cos - pltpu.roll(d, 96, 1) * sa - pltpu.roll(d, 32, 1) * sb


def _rows(ts, w):
    return pl.BlockSpec((ts, w), lambda i: (i, 0))


def _vec(w):
    return pl.BlockSpec((1, w), lambda i: (0, 0))


def _acc_init(i, *refs):
    @pl.when(i == 0)
    def _():
        for r in refs:
            r[...] = jnp.zeros_like(r)


def _rope_tables(pos, inv_freq):
    s = pos.shape[0]
    ts = _pick(s, (512, 256, 128))

    def body(p_ref, f_ref, c_ref, sa_ref, sb_ref):
        ang = p_ref[...].astype(F32) * f_ref[...]
        lane = lax.broadcasted_iota(jnp.int32, ang.shape, 1)
        c, sn = jnp.cos(ang), jnp.sin(ang)
        c_ref[...] = jnp.where(lane < QK_ROPE, c, 0.0)
        sa_ref[...] = jnp.where(lane < QK_ROPE // 2, -sn, 0.0)
        sb_ref[...] = jnp.where((lane >= QK_ROPE // 2) & (lane < QK_ROPE), sn, 0.0)

    out = jax.ShapeDtypeStruct((s, LANES), F32)
    return pl.pallas_call(
        body, name="rope_tables", grid=(s // ts,),
        in_specs=[_rows(ts, 1), _vec(LANES)],
        out_specs=[_rows(ts, LANES)] * 3, out_shape=[out] * 3,
        compiler_params=_params(("parallel",)),
    )(pos, inv_freq)


def _pre_fwd(x, g):
    s, d = x.shape
    ts = _pick(s, (256, 128))

    def body(x_ref, g_ref, h_ref):
        h_ref[...] = _rms(x_ref[...], g_ref[...]).astype(BF16)

    return pl.pallas_call(
        body, name="pre_fwd", grid=(s // ts,),
        in_specs=[_rows(ts, d), _vec(d)], out_specs=_rows(ts, d),
        out_shape=jax.ShapeDtypeStruct((s, d), BF16),
        compiler_params=_params(("parallel",)),
    )(x, g)


def _split_fwd(z, gq, gkv, tabs, c, ql, kvl):
    s, zw = z.shape
    ts = _pick(s, (256, 128))
    o_q, o_kv, o_kr = 2 * c, 2 * c + ql, 2 * c + ql + kvl

    def body(z_ref, gq_ref, gkv_ref, c_ref, sa_ref, sb_ref, u0_ref, qn_ref, kvn_ref, kpe_ref):
        u0_ref[...] = z_ref[:, 0:c] * _sigmoid(z_ref[:, c:2 * c])
        qn_ref[...] = _rms(z_ref[:, o_q:o_kv], gq_ref[...]).astype(BF16)
        kvn_ref[...] = _rms(z_ref[:, o_kv:o_kr], gkv_ref[...]).astype(BF16)
        kpe_ref[...] = _rope(z_ref[:, o_kr:o_kr + LANES], c_ref[...], sa_ref[...], sb_ref[...]).astype(BF16)

    return pl.pallas_call(
        body, name="split_fwd", grid=(s // ts,),
        in_specs=[_rows(ts, zw), _vec(ql), _vec(kvl)] + [_rows(ts, LANES)] * 3,
        out_specs=[_rows(ts, c), _rows(ts, ql), _rows(ts, kvl), _rows(ts, LANES)],
        out_shape=[jax.ShapeDtypeStruct((s, c), F32), jax.ShapeDtypeStruct((s, ql), BF16),
                   jax.ShapeDtypeStruct((s, kvl), BF16), jax.ShapeDtypeStruct((s, LANES), BF16)],
        compiler_params=_params(("parallel",)),
    )(z, gq, gkv, *tabs)


def _split_bwd(du0, z, dqn, dkvn, dkpe_h, gq, gkv, tabs, c, ql, kvl):
    s, zw = z.shape
    ts = _pick(s, (256, 128))
    o_q, o_kv, o_kr = 2 * c, 2 * c + ql, 2 * c + ql + kvl

    def body(du0_ref, z_ref, dqn_ref, dkvn_ref, dkh_ref, gq_ref, gkv_ref, c_ref, sa_ref, sb_ref,
             dz_ref, dgq_ref, dgkv_ref):
        _acc_init(pl.program_id(0), dgq_ref, dgkv_ref)
        du0 = du0_ref[...]
        a = z_ref[:, 0:c]
        sg = _sigmoid(z_ref[:, c:2 * c])
        dz_ref[:, 0:c] = (du0 * sg).astype(BF16)
        dz_ref[:, c:2 * c] = (du0 * a * sg * (1.0 - sg)).astype(BF16)
        dq, dgq = _rms_bwd(z_ref[:, o_q:o_kv], gq_ref[...], dqn_ref[...])
        dz_ref[:, o_q:o_kv] = dq.astype(BF16)
        dgq_ref[...] += dgq
        dkv, dgkv = _rms_bwd(z_ref[:, o_kv:o_kr], gkv_ref[...], dkvn_ref[...])
        dz_ref[:, o_kv:o_kr] = dkv.astype(BF16)
        dgkv_ref[...] += dgkv
        dk = dkh_ref[:, 0:LANES]
        for h in range(1, N_HEADS):
            dk = dk + dkh_ref[:, h * LANES:(h + 1) * LANES]
        dz_ref[:, o_kr:o_kr + LANES] = _rope_t(dk, c_ref[...], sa_ref[...], sb_ref[...]).astype(BF16)

    return pl.pallas_call(
        body, name="split_bwd", grid=(s // ts,),
        in_specs=[_rows(ts, c), _rows(ts, zw), _rows(ts, ql), _rows(ts, kvl), _rows(ts, N_HEADS * LANES),
                  _vec(ql), _vec(kvl)] + [_rows(ts, LANES)] * 3,
        out_specs=[_rows(ts, zw), _vec(ql), _vec(kvl)],
        out_shape=[jax.ShapeDtypeStruct((s, zw), BF16), jax.ShapeDtypeStruct((1, ql), F32),
                   jax.ShapeDtypeStruct((1, kvl), F32)],
        compiler_params=_params(("arbitrary",)),
    )(du0, z, dqn, dkvn, dkpe_h, gq, gkv, *tabs)


def _q_rope(qpre, tabs):
    s, w = qpre.shape
    ts = _pick(s, (256, 128))

    def body(q_ref, c_ref, sa_ref, sb_ref, o_ref):
        cs, sa, sb = c_ref[...], sa_ref[...], sb_ref[...]
        for h in range(N_HEADS):
            lo = h * HEAD_PAD
            o_ref[:, lo:lo + QK_NOPE] = q_ref[:, lo:lo + QK_NOPE].astype(BF16)
            o_ref[:, lo + QK_NOPE:lo + HEAD_PAD] = _rope(q_ref[:, lo + QK_NOPE:lo + HEAD_PAD], cs, sa, sb).astype(BF16)

    return pl.pallas_call(
        body, name="q_rope", grid=(s // ts,),
        in_specs=[_rows(ts, w)] + [_rows(ts, LANES)] * 3, out_specs=_rows(ts, w),
        out_shape=jax.ShapeDtypeStruct((s, w), BF16),
        compiler_params=_params(("parallel",)),
    )(qpre, *tabs)


def _conv_fwd(u0, w, b):
    s, c = u0.shape
    tc = LANES
    rc = _pick(s, (256, 128))

    def body(u_ref, w_ref, b_ref, o_ref, pad_ref):
        pad_ref[0:CONV_PAD, :] = jnp.zeros((CONV_PAD, tc), F32)
        pad_ref[CONV_PAD:CONV_PAD + s, :] = u_ref[...]
        for r in range(s // rc):
            acc = jnp.broadcast_to(b_ref[...], (rc, tc))
            for k in range(CONV_K):
                lo = r * rc + CONV_PAD - (CONV_K - 1) + k
                acc = acc + w_ref[k:k + 1, :] * pad_ref[lo:lo + rc, :]
            o_ref[r * rc:(r + 1) * rc, :] = acc

    col = lambda j: (0, j)
    return pl.pallas_call(
        body, name="conv_fwd", grid=(c // tc,),
        in_specs=[pl.BlockSpec((s, tc), col), pl.BlockSpec((CONV_K, tc), col), pl.BlockSpec((1, tc), col)],
        out_specs=pl.BlockSpec((s, tc), col),
        out_shape=jax.ShapeDtypeStruct((s, c), F32),
        scratch_shapes=[pltpu.VMEM((s + CONV_PAD, tc), F32)],
        compiler_params=_params(("parallel",)),
    )(u0, w, b)


def _conv_bwd(du1, u0, w):
    s, c = u0.shape
    tc = LANES
    rc = _pick(s, (256, 128))

    def body(d_ref, u_ref, w_ref, du_ref, dw_ref, db_ref, upad_ref, dpad_ref):
        upad_ref[0:CONV_PAD, :] = jnp.zeros((CONV_PAD, tc), F32)
        upad_ref[CONV_PAD:CONV_PAD + s, :] = u_ref[...]
        dpad_ref[0:s, :] = d_ref[...]
        dpad_ref[s:s + CONV_PAD, :] = jnp.zeros((CONV_PAD, tc), F32)
        for r in range(s // rc):
            acc = jnp.zeros((rc, tc), F32)
            for k in range(CONV_K):
                lo = r * rc + (CONV_K - 1) - k
                acc = acc + w_ref[k:k + 1, :] * dpad_ref[lo:lo + rc, :]
            du_ref[r * rc:(r + 1) * rc, :] = acc
        for k in range(CONV_K):
            acc8 = jnp.zeros((8, tc), F32)
            for r in range(s // rc):
                lo = r * rc + CONV_PAD - (CONV_K - 1) + k
                prod = d_ref[r * rc:(r + 1) * rc, :] * upad_ref[lo:lo + rc, :]
                acc8 = acc8 + jnp.sum(prod.reshape(rc // 8, 8, tc), axis=0)
            dw_ref[k:k + 1, :] = jnp.sum(acc8, axis=0, keepdims=True)
        db_ref[...] = jnp.sum(d_ref[...], axis=0, keepdims=True)

    col = lambda j: (0, j)
    return pl.pallas_call(
        body, name="conv_bwd", grid=(c // tc,),
        in_specs=[pl.BlockSpec((s, tc), col), pl.BlockSpec((s, tc), col), pl.BlockSpec((CONV_K, tc), col)],
        out_specs=[pl.BlockSpec((s, tc), col), pl.BlockSpec((CONV_K, tc), col), pl.BlockSpec((1, tc), col)],
        out_shape=[jax.ShapeDtypeStruct((s, c), F32), jax.ShapeDtypeStruct((CONV_K, c), F32),
                   jax.ShapeDtypeStruct((1, c), F32)],
        scratch_shapes=[pltpu.VMEM((s + CONV_PAD, tc), F32), pltpu.VMEM((s + CONV_PAD, tc), F32)],
        compiler_params=_params(("parallel",)),
    )(du1, u0, w)


def _causal_mask(sc, qi, kj, tq, tk):
    rows = qi * tq + lax.broadcasted_iota(jnp.int32, sc.shape, 0)
    cols = kj * tk + lax.broadcasted_iota(jnp.int32, sc.shape, 1)
    return jnp.where(cols <= rows, sc, NEG)


def _attn_fwd(q, kv, kpe):
    s = q.shape[0]
    tq = tk = _pick(s, (ATTN_BLOCK, 256, 128))
    reps = tk // LANES
    scale = QK_HEAD ** -0.5
    nt = (((1,), (1,)), ((), ()))

    def body(q_ref, kn_ref, v_ref, kpe_ref, o_ref, lse_ref, kf_ref, vb_ref, m_ref, l_ref, acc_ref):
        i = pl.program_id(1)

        @pl.when(i == 0)
        def _():
            kf_ref[:, 0:QK_NOPE] = kn_ref[...].astype(BF16)
            kf_ref[:, QK_NOPE:HEAD_PAD] = kpe_ref[...]
            vb_ref[...] = v_ref[...].astype(BF16)

        qb = q_ref[...]
        m_ref[...] = jnp.full((tq, LANES), NEG, F32)
        l_ref[...] = jnp.zeros((tq, LANES), F32)
        acc_ref[...] = jnp.zeros((tq, V_HEAD), F32)

        def block(j, diagonal):
            off = j * tk
            sc = lax.dot_general(qb, kf_ref[pl.ds(off, tk), :], nt, preferred_element_type=F32) * scale
            if diagonal:
                sc = _causal_mask(sc, 0, 0, tq, tk)
            m_prev = m_ref[...]
            m_new = jnp.maximum(m_prev, jnp.max(sc, axis=1, keepdims=True))
            p = jnp.exp(sc - jnp.tile(m_new, (1, reps)))
            alpha = jnp.exp(m_prev - m_new)
            l_ref[...] = alpha * l_ref[...] + jnp.sum(p, axis=1, keepdims=True)
            acc_ref[...] = alpha * acc_ref[...] + jnp.dot(p.astype(BF16), vb_ref[pl.ds(off, tk), :],
                                                          preferred_element_type=F32)
            m_ref[...] = m_new

        for qi in range(s // tq):
            @pl.when(i == qi)
            def _(qi=qi):
                for j in range(qi):
                    block(j, False)
                block(qi, True)

        o_ref[...] = acc_ref[...] / l_ref[...]
        lse_ref[...] = m_ref[...] + jnp.log(l_ref[...])

    return pl.pallas_call(
        body, name="attn_fwd", grid=(N_HEADS, s // tq),
        in_specs=[pl.BlockSpec((tq, HEAD_PAD), lambda h, i: (i, h)),
                  pl.BlockSpec((s, QK_NOPE), lambda h, i: (0, 2 * h)),
                  pl.BlockSpec((s, V_HEAD), lambda h, i: (0, 2 * h + 1)),
                  pl.BlockSpec((s, LANES), lambda h, i: (0, 0))],
        out_specs=[pl.BlockSpec((tq, V_HEAD), lambda h, i: (i, h)),
                   pl.BlockSpec((tq, LANES), lambda h, i: (i, h))],
        out_shape=[jax.ShapeDtypeStruct((s, N_HEADS * V_HEAD), F32),
                   jax.ShapeDtypeStruct((s, N_HEADS * LANES), F32)],
        scratch_shapes=[pltpu.VMEM((s, HEAD_PAD), BF16), pltpu.VMEM((s, V_HEAD), BF16),
                        pltpu.VMEM((tq, LANES), F32), pltpu.VMEM((tq, LANES), F32), pltpu.VMEM((tq, V_HEAD), F32)],
        compiler_params=_params(("parallel", "arbitrary")),
    )(q, kv, kv, kpe)


def _attn_bwd(q, kv, kpe, o, do, lse, tabs):
    s = q.shape[0]
    tq = tk = _pick(s, (ATTN_BLOCK, 256, 128))
    nq = s // tq
    reps = tk // LANES
    scale = QK_HEAD ** -0.5
    nt = (((1,), (1,)), ((), ()))
    tn = (((0,), (0,)), ((), ()))

    def body(q_ref, kn_ref, v_ref, kpe_ref, o_ref, do_ref, lse_ref, c_ref, sa_ref, sb_ref, dqpre_ref, dkv_ref,
             dkpe_ref, kf_ref, vb_ref, dk_ref, dv_ref, dq_ref):
        j = pl.program_id(1)

        @pl.when(j == 0)
        def _():
            dq_ref[...] = jnp.zeros_like(dq_ref)

        kf_ref[:, 0:QK_NOPE] = kn_ref[...].astype(BF16)
        kf_ref[:, QK_NOPE:HEAD_PAD] = kpe_ref[...]
        vb_ref[...] = v_ref[...].astype(BF16)
        dk_ref[...] = jnp.zeros_like(dk_ref)
        dv_ref[...] = jnp.zeros_like(dv_ref)

        def block(i, diagonal):
            off = i * tq
            qb = q_ref[pl.ds(off, tq), :]
            dob = do_ref[pl.ds(off, tq), :]
            delta = jnp.sum(dob * o_ref[pl.ds(off, tq), :], axis=1, keepdims=True)
            sc = lax.dot_general(qb, kf_ref[...], nt, preferred_element_type=F32) * scale
            if diagonal:
                sc = _causal_mask(sc, 0, 0, tq, tk)
            p = jnp.exp(sc - jnp.tile(lse_ref[pl.ds(off, tq), :], (1, reps)))
            dob16 = dob.astype(BF16)
            dv_ref[...] += lax.dot_general(p.astype(BF16), dob16, tn, preferred_element_type=F32)
            dp = lax.dot_general(dob16, vb_ref[...], nt, preferred_element_type=F32)
            ds = (p * (dp - delta) * scale).astype(BF16)
            dq_ref[pl.ds(off, tq), :] += jnp.dot(ds, kf_ref[...], preferred_element_type=F32)
            dk_ref[...] += lax.dot_general(ds, qb, tn, preferred_element_type=F32)

        for kj in range(nq):
            @pl.when(j == kj)
            def _(kj=kj):
                block(kj, True)
                for i in range(kj + 1, nq):
                    block(i, False)

        dkv_ref[:, 0:QK_NOPE] = dk_ref[:, 0:QK_NOPE]
        dkv_ref[:, QK_NOPE:HEAD_PAD] = dv_ref[...]
        dkpe_ref[...] = dk_ref[:, QK_NOPE:HEAD_PAD]

        @pl.when(j == nq - 1)
        def _():
            dqpre_ref[:, 0:QK_NOPE] = dq_ref[:, 0:QK_NOPE].astype(BF16)
            dqpre_ref[:, QK_NOPE:HEAD_PAD] = _rope_t(dq_ref[:, QK_NOPE:HEAD_PAD], c_ref[...], sa_ref[...],
                                                     sb_ref[...]).astype(BF16)

    head_rows = lambda w: pl.BlockSpec((s, w), lambda h, j: (0, h))
    table = pl.BlockSpec((s, LANES), lambda h, j: (0, 0))
    return pl.pallas_call(
        body, name="attn_bwd", grid=(N_HEADS, s // tk),
        in_specs=[head_rows(HEAD_PAD),
                  pl.BlockSpec((tk, QK_NOPE), lambda h, j: (j, 2 * h)),
                  pl.BlockSpec((tk, V_HEAD), lambda h, j: (j, 2 * h + 1)),
                  pl.BlockSpec((tk, LANES), lambda h, j: (j, 0)),
                  head_rows(V_HEAD), head_rows(V_HEAD), head_rows(LANES), table, table, table],
        out_specs=[head_rows(HEAD_PAD),
                   pl.BlockSpec((tk, HEAD_PAD), lambda h, j: (j, h)),
                   pl.BlockSpec((tk, LANES), lambda h, j: (j, h))],
        out_shape=[jax.ShapeDtypeStruct((s, N_HEADS * HEAD_PAD), BF16),
                   jax.ShapeDtypeStruct((s, N_HEADS * HEAD_PAD), F32),
                   jax.ShapeDtypeStruct((s, N_HEADS * LANES), F32)],
        scratch_shapes=[pltpu.VMEM((tk, HEAD_PAD), BF16), pltpu.VMEM((tk, V_HEAD), BF16),
                        pltpu.VMEM((tk, HEAD_PAD), F32), pltpu.VMEM((tk, V_HEAD), F32),
                        pltpu.VMEM((s, HEAD_PAD), F32)],
        compiler_params=_params(("parallel", "arbitrary")),
    )(q, kv, kv, kpe, o, do, lse, *tabs)


def _mix_fwd(u1, lng, lnb, gcon, attn, gattn):
    s, c = u1.shape
    ac = attn.shape[1]
    ts = _pick(s, (256, 128))

    def body(u_ref, lg_ref, lb_ref, gc_ref, a_ref, ga_ref, o_ref):
        t3 = _silu(_ln(u_ref[...], lg_ref[...], lb_ref[...]))
        o_ref[:, 0:c] = _rms(t3, gc_ref[...]).astype(BF16)
        o_ref[:, c:c + ac] = _rms(a_ref[...], ga_ref[...]).astype(BF16)

    return pl.pallas_call(
        body, name="mix_fwd", grid=(s // ts,),
        in_specs=[_rows(ts, c), _vec(c), _vec(c), _vec(c), _rows(ts, ac), _vec(ac)],
        out_specs=_rows(ts, c + ac),
        out_shape=jax.ShapeDtypeStruct((s, c + ac), BF16),
        compiler_params=_params(("parallel",)),
    )(u1, lng, lnb, gcon, attn, gattn)


def _mix_bwd(dmixin, u1, lng, lnb, gcon, attn, gattn):
    s, c = u1.shape
    ac = attn.shape[1]
    ts = _pick(s, (256, 128))

    def body(d_ref, u_ref, lg_ref, lb_ref, gc_ref, a_ref, ga_ref,
             du_ref, da_ref, dlg_ref, dlb_ref, dgc_ref, dga_ref):
        _acc_init(pl.program_id(0), dlg_ref, dlb_ref, dgc_ref, dga_ref)
        u = u_ref[...]
        t2 = _ln(u, lg_ref[...], lb_ref[...])
        dt3, dgc = _rms_bwd(_silu(t2), gc_ref[...], d_ref[:, 0:c])
        du, dlg, dlb = _ln_bwd(u, lg_ref[...], dt3 * _silu_grad(t2))
        du_ref[...] = du
        dlg_ref[...] += dlg
        dlb_ref[...] += dlb
        dgc_ref[...] += dgc
        da, dga = _rms_bwd(a_ref[...], ga_ref[...], d_ref[:, c:c + ac])
        da_ref[...] = da
        dga_ref[...] += dga

    return pl.pallas_call(
        body, name="mix_bwd", grid=(s // ts,),
        in_specs=[_rows(ts, c + ac), _rows(ts, c), _vec(c), _vec(c), _vec(c), _rows(ts, ac), _vec(ac)],
        out_specs=[_rows(ts, c), _rows(ts, ac), _vec(c), _vec(c), _vec(c), _vec(ac)],
        out_shape=[jax.ShapeDtypeStruct((s, c), F32), jax.ShapeDtypeStruct((s, ac), F32),
                   jax.ShapeDtypeStruct((1, c), F32), jax.ShapeDtypeStruct((1, c), F32),
                   jax.ShapeDtypeStruct((1, c), F32), jax.ShapeDtypeStruct((1, ac), F32)],
        compiler_params=_params(("arbitrary",)),
    )(dmixin, u1, lng, lnb, gcon, attn, gattn)


def _post_mix_fwd(x, mix, gpost, gpre):
    s, d = x.shape
    ts = _pick(s, (256, 128))

    def body(x_ref, m_ref, gp_ref, gf_ref, x1_ref, hf_ref):
        x1 = x_ref[...] + _rms(m_ref[...], gp_ref[...])
        x1_ref[...] = x1
        hf_ref[...] = _rms(x1, gf_ref[...]).astype(BF16)

    return pl.pallas_call(
        body, name="post_mix_fwd", grid=(s // ts,),
        in_specs=[_rows(ts, d), _rows(ts, d), _vec(d), _vec(d)],
        out_specs=[_rows(ts, d), _rows(ts, d)],
        out_shape=[jax.ShapeDtypeStruct((s, d), F32), jax.ShapeDtypeStruct((s, d), BF16)],
        compiler_params=_params(("parallel",)),
    )(x, mix, gpost, gpre)


def _post_mix_bwd(dy, dhf, x1, gpre, mix, gpost):
    s, d = x1.shape
    ts = _pick(s, (256, 128))

    def body(dy_ref, dh_ref, x1_ref, gf_ref, m_ref, gp_ref, dx1_ref, dm_ref, dgf_ref, dgp_ref):
        _acc_init(pl.program_id(0), dgf_ref, dgp_ref)
        dxa, dgf = _rms_bwd(x1_ref[...], gf_ref[...], dh_ref[...])
        dx1 = dy_ref[...] + dxa
        dx1_ref[...] = dx1
        dgf_ref[...] += dgf
        dm, dgp = _rms_bwd(m_ref[...], gp_ref[...], dx1)
        dm_ref[...] = dm.astype(BF16)
        dgp_ref[...] += dgp

    return pl.pallas_call(
        body, name="post_mix_bwd", grid=(s // ts,),
        in_specs=[_rows(ts, d), _rows(ts, d), _rows(ts, d), _vec(d), _rows(ts, d), _vec(d)],
        out_specs=[_rows(ts, d), _rows(ts, d), _vec(d), _vec(d)],
        out_shape=[jax.ShapeDtypeStruct((s, d), F32), jax.ShapeDtypeStruct((s, d), BF16),
                   jax.ShapeDtypeStruct((1, d), F32), jax.ShapeDtypeStruct((1, d), F32)],
        compiler_params=_params(("arbitrary",)),
    )(dy, dhf, x1, gpre, mix, gpost)


def _ffn_up(hf, wg, wu, slabs, name, earlier=None):
    s, d = hf.shape
    nsh, fs, _ = wg.shape
    tm = _pick(s, (1024, 512, 256, 128))
    nt = (((1,), (1,)), ((), ()))

    def body(slab_ref, h_ref, wg_ref, wu_ref, *rest):
        dg_ref, du_ref, a_ref = rest[-3:]
        h = h_ref[...]
        g = lax.dot_general(h, wg_ref[...], nt, preferred_element_type=F32)
        u = lax.dot_general(h, wu_ref[...], nt, preferred_element_type=F32)
        sg = _sigmoid(g)
        silu = g * sg
        dg_ref[...] = u * (sg * (1.0 + g * (1.0 - sg)))
        du_ref[...] = silu
        a_ref[...] = (silu * u).astype(BF16)

    w_spec = pl.BlockSpec((None, fs, d), lambda i, j, slab_ref: (slab_ref[j], 0, 0))
    o_spec = pl.BlockSpec((None, tm, fs), lambda i, j, slab_ref: (slab_ref[j], i, 0))
    earlier = [] if earlier is None else list(earlier)
    return pl.pallas_call(
        body, name=name,
        grid_spec=pltpu.PrefetchScalarGridSpec(
            num_scalar_prefetch=1, grid=(s // tm, slabs.shape[0]),
            in_specs=[pl.BlockSpec((tm, d), lambda i, j, slab_ref: (i, 0)), w_spec, w_spec]
            + [pl.BlockSpec(memory_space=pl.ANY)] * len(earlier),
            out_specs=[o_spec] * 3),
        out_shape=[jax.ShapeDtypeStruct((nsh, s, fs), F32)] * 2 + [jax.ShapeDtypeStruct((nsh, s, fs), BF16)],
        input_output_aliases={4 + k: k for k in range(len(earlier))},
        compiler_params=_params(("parallel", "parallel")),
    )(slabs, hf, wg, wu, *earlier)


def _ffn_down(acts, ws, name):
    n = len(acts)
    nsh, s, fs = acts[0].shape
    d = ws[0].shape[2]
    tm = _pick(s, (1024, 512, 256, 128))
    tn = _pick(d, (256, 128))
    a_mode = pl.Buffered(1)

    def body(*refs):
        acc = None
        for a_ref, w_ref in zip(refs[:n], refs[n:2 * n]):
            for j in range(nsh):
                part = jnp.dot(a_ref[j], w_ref[j], preferred_element_type=F32)
                acc = part if acc is None else acc + part
        refs[-1][...] = acc

    return pl.pallas_call(
        body, name=name, grid=(s // tm, d // tn),
        in_specs=[pl.BlockSpec((nsh, tm, fs), lambda i, j: (0, i, 0), pipeline_mode=a_mode)] * n
        + [pl.BlockSpec((nsh, fs, tn), lambda i, j: (0, 0, j))] * n,
        out_specs=pl.BlockSpec((tm, tn), lambda i, j: (i, j)),
        out_shape=jax.ShapeDtypeStruct((s, d), F32),
        compiler_params=_params(("parallel", "parallel")),
    )(*acts, *ws)


def _ffn_down_bwd(dff, wd, act_dgate, act_dup, behind):
    s, d = dff.shape
    nsh, fs, _ = wd.shape
    tm = _pick(s, (1024, 512, 256, 128))
    nt = (((1,), (1,)), ((), ()))

    def body(d_ref, w_ref, pg_ref, pu_ref, _, dg_ref, du_ref):
        dact = lax.dot_general(d_ref[...], w_ref[...], nt, preferred_element_type=F32)
        dg_ref[...] = (dact * pg_ref[...]).astype(BF16)
        du_ref[...] = (dact * pu_ref[...]).astype(BF16)

    h_spec = pl.BlockSpec((None, tm, fs), lambda i, j: (j, i, 0))
    return pl.pallas_call(
        body, name="ffn_down_bwd", grid=(s // tm, nsh),
        in_specs=[pl.BlockSpec((tm, d), lambda i, j: (i, 0)),
                  pl.BlockSpec((None, fs, d), lambda i, j: (j, 0, 0)), h_spec, h_spec,
                  pl.BlockSpec((8, LANES), lambda i, j: (0, 0))],
        out_specs=[h_spec] * 2,
        out_shape=[jax.ShapeDtypeStruct((nsh, s, fs), BF16)] * 2,
        compiler_params=_params(("parallel", "parallel")),
    )(dff, wd, act_dgate, act_dup, behind)


def _ffn_dw(hiddens, other, name):
    n = len(hiddens)
    nsh, s, fs = hiddens[0].shape
    d = other.shape[1]
    tn = (((0,), (0,)), ((), ()))

    def body(*refs):
        for a_ref, o_ref in zip(refs[:n], refs[n + 1:]):
            o_ref[...] = lax.dot_general(a_ref[...], refs[n][...], tn, preferred_element_type=F32).astype(BF16)

    return pl.pallas_call(
        body, name=name, grid=(nsh,),
        in_specs=[pl.BlockSpec((None, s, fs), lambda j: (j, 0, 0))] * n + [pl.BlockSpec((s, d), lambda j: (0, 0))],
        out_specs=[pl.BlockSpec((None, fs, d), lambda j: (j, 0, 0))] * n,
        out_shape=[jax.ShapeDtypeStruct((nsh, fs, d), BF16)] * n,
        compiler_params=_params(("parallel",)),
    )(*hiddens, other)


def _ffn_down_final(act, wd, x1, tgt, g):
    nsh, s, fs = act.shape
    d = x1.shape[1]
    ts = _pick(s, (256, 128))

    def body(a_ref, w_ref, x1_ref, t_ref, g_ref, loss_ref, dy_ref, dff_ref, dg_ref):
        _acc_init(pl.program_id(0), loss_ref, dg_ref)
        ff_v = jnp.dot(a_ref[0], w_ref[0], preferred_element_type=F32)
        for j in range(1, nsh):
            ff_v = ff_v + jnp.dot(a_ref[j], w_ref[j], preferred_element_type=F32)
        err = x1_ref[...] + _rms(ff_v, g_ref[...]) - t_ref[...]
        tok = jnp.mean(err * err, axis=-1, keepdims=True)
        loss_ref[...] += 0.5 * jnp.sum(tok, axis=0, keepdims=True)
        dy = err * (1.0 / d)
        dy_ref[...] = dy
        dff, dg = _rms_bwd(ff_v, g_ref[...], dy)
        dff_ref[...] = dff.astype(BF16)
        dg_ref[...] += dg

    return pl.pallas_call(
        body, name="ffn_down_final", grid=(s // ts,),
        in_specs=[pl.BlockSpec((nsh, ts, fs), lambda i: (0, i, 0)),
                  pl.BlockSpec((nsh, fs, d), lambda i: (0, 0, 0), pipeline_mode=pl.Buffered(1)),
                  _rows(ts, d), _rows(ts, d), _vec(d)],
        out_specs=[_vec(LANES), _rows(ts, d), _rows(ts, d), _vec(d)],
        out_shape=[jax.ShapeDtypeStruct((1, LANES), F32), jax.ShapeDtypeStruct((s, d), F32),
                   jax.ShapeDtypeStruct((s, d), BF16), jax.ShapeDtypeStruct((1, d), F32)],
        compiler_params=_params(("arbitrary",)),
    )(act, wd, x1, tgt, g)


def _pre_bwd(dx1, dh, x, g):
    s, d = x.shape
    ts = _pick(s, (256, 128))

    def body(dx1_ref, dh_ref, x_ref, g_ref, dx_ref, dg_ref):
        _acc_init(pl.program_id(0), dg_ref)
        dxa, dg = _rms_bwd(x_ref[...], g_ref[...], dh_ref[...])
        dx_ref[...] = dx1_ref[...] + dxa
        dg_ref[...] += dg

    return pl.pallas_call(
        body, name="pre_bwd", grid=(s // ts,),
        in_specs=[_rows(ts, d), _rows(ts, d), _rows(ts, d), _vec(d)],
        out_specs=[_rows(ts, d), _vec(d)],
        out_shape=[jax.ShapeDtypeStruct((s, d), F32), jax.ShapeDtypeStruct((1, d), F32)],
        compiler_params=_params(("arbitrary",)),
    )(dx1, dh, x, g)


def _local_step(x, pos, tgt, vecs, in_weights_fn, mix_weights_fn, up_weights_fn, down_weights_fn, grads_fn):
    c = vecs["conv_b"].shape[1]
    ql = vecs["q_norm"].shape[1]
    kvl = vecs["kv_norm"].shape[1]
    half = jnp.arange(0, QK_ROPE, 2, dtype=F32)
    freq = ROPE_THETA ** (-half / QK_ROPE)
    inv_freq = jnp.concatenate([freq, freq, jnp.zeros((LANES - QK_ROPE,), F32)])[None, :]
    tabs = _rope_tables(pos, inv_freq)

    h = _pre_fwd(x, vecs["pre_mix_norm"])
    w_in_t, zero = in_weights_fn(h)
    z = _mm(h, w_in_t, "nt", "mm_z")
    w_uq_t, w_ukv, conv_w, w_out = mix_weights_fn(z)
    u0, qn, kvn, kpe = _split_fwd(z, vecs["q_norm"] + zero, vecs["kv_norm"], tabs, c, ql, kvl)
    u1 = _conv_fwd(u0, conv_w, vecs["conv_b"])
    q = _q_rope(_mm(qn, w_uq_t, "nt", "mm_q"), tabs)
    kv = _mm(kvn, w_ukv, "nn", "mm_kv")
    attn, lse = _attn_fwd(q, kv, kpe)
    mixin = _mix_fwd(u1, vecs["conv_ln_g"], vecs["conv_ln_b"], vecs["conv_out_norm"], attn, vecs["attn_out_norm"])
    mix = _mm(mixin, w_out, "nn", "mm_mix")
    x1, hf = _post_mix_fwd(x, mix, vecs["post_mix_norm"], vecs["pre_ffn_norm"])
    w_gate, w_up, slabs, rest_fn = up_weights_fn(hf)
    part = _ffn_up(hf, w_gate, w_up, slabs, "ffn_up_first")
    w_gate, w_up, slabs = rest_fn(part[2])
    act_dgate, act_dup, act = _ffn_up(hf, w_gate, w_up, slabs, "ffn_up_rest", earlier=part)
    w_down = down_weights_fn(act)
    loss, dy, dff, d_post_ffn = _ffn_down_final(act, w_down, x1, tgt, vecs["post_ffn_norm"])

    g = {"post_ffn_norm": d_post_ffn}
    zero = grads_fn("down", ("w_down",), _ffn_dw([act], dff, "ffn_dw_down"), dff)
    dgate, dup = _ffn_down_bwd(dff, w_down, act_dgate, act_dup, jnp.zeros((8, LANES), F32) + zero)
    zero = zero + grads_fn("up", ("w_gate", "w_up"), _ffn_dw([dgate, dup], hf, "ffn_dw_up"), dgate)
    dhf = _ffn_down([dgate, dup], [w_gate, w_up], "ffn_dhf")
    dx1, dmix, g["pre_ffn_norm"], g["post_mix_norm"] = _post_mix_bwd(
        dy, dhf, x1, vecs["pre_ffn_norm"] + zero, mix, vecs["post_mix_norm"])
    dmixin = _mm(dmix, w_out, "nt", "mm_dmixin")
    dw_out = _mm(mixin, dmix, "tn", "mm_dw_out", BF16)
    zero = grads_fn("out", ("w_out",), [dw_out], dmix)
    du1, dattn, g["conv_ln_g"], g["conv_ln_b"], g["conv_out_norm"], g["attn_out_norm"] = _mix_bwd(
        dmixin, u1, vecs["conv_ln_g"] + zero, vecs["conv_ln_b"], vecs["conv_out_norm"], attn, vecs["attn_out_norm"])
    du0, dw_conv, g["conv_b"] = _conv_bwd(du1, u0, conv_w)
    dqpre, dkv, dkpe_h = _attn_bwd(q, kv, kpe, attn, dattn, lse, tabs)
    dqn = _mm(dqpre, w_uq_t, "nn", "mm_dqn")
    dw_uq = _mm(dqpre, qn, "tn", "mm_dw_uq", BF16)
    dkvn = _mm(dkv, w_ukv, "nt", "mm_dkvn")
    dw_ukv = _mm(kvn, dkv, "tn", "mm_dw_ukv", BF16)
    zero = grads_fn("", (), [], dkvn)
    dz, g["q_norm"], g["kv_norm"] = _split_bwd(du0, z, dqn, dkvn, dkpe_h, vecs["q_norm"] + zero, vecs["kv_norm"], tabs,
                                               c, ql, kvl)
    dw_in = _mm(dz, h, "tn", "mm_dw_in", BF16)
    zero = grads_fn("in", ("w_in", "w_uq", "w_ukv", "conv_w"), [dw_in, dw_uq, dw_ukv, dw_conv], dz)
    dh = _mm(dz, w_in_t, "nn", "mm_dh")
    zero = zero + grads_fn("", (), [], dh)
    grad_x, g["pre_mix_norm"] = _pre_bwd(dx1, dh, x, vecs["pre_mix_norm"] + zero)
    return loss, grad_x, g


def _my_index():
    return 4 * lax.axis_index("x") + 2 * lax.axis_index("y") + lax.axis_index("c")


def _coords(idx):
    return ((idx >> 2) & 1, (idx >> 1) & 1, idx & 1)


def _place():
    x, y, c = lax.axis_index("x"), lax.axis_index("y"), lax.axis_index("c")
    return (x, y, c), (x, y, 1 - c), [(1 - x, y), (x, 1 - y), (1 - x, 1 - y)]


def _small_copies(src, land, send_sem, recv_sem):
    me = _my_index()
    return [pltpu.make_async_remote_copy(src_ref=src, dst_ref=land.at[me], send_sem=send_sem, recv_sem=recv_sem,
                                         device_id=_coords(me ^ p), device_id_type=MESH)
            for p in range(1, N_DEV)]


HBM_SPEC = pl.BlockSpec(memory_space=pltpu.HBM)
SEM_SPEC = pl.BlockSpec(memory_space=pltpu.SEMAPHORE)
DATAFLOW = pltpu.SideEffectType.DATAFLOW_SIDE_EFFECTING


def _split_start(name, copies_of, srcs, lands, after):
    n = len(srcs)
    builders = list(copies_of) if isinstance(copies_of, (list, tuple)) else [copies_of] * n

    def body(*refs):
        outs = refs[2 * n + 1:]
        for k in range(n):
            for cp in builders[k](refs[k], refs[n + k], outs[k], outs[n + k]):
                cp.start()
        outs[-1][...] = jnp.zeros_like(outs[-1])

    hbm = lambda a: pltpu.HBM(a.shape, a.dtype)
    out = pl.pallas_call(
        body, name=name,
        in_specs=[HBM_SPEC] * (2 * n + 1),
        out_specs=[SEM_SPEC] * (2 * n) + [HBM_SPEC] * (2 * n) + [pl.BlockSpec(memory_space=pltpu.VMEM)],
        out_shape=[pltpu.SemaphoreType.DMA(())] * (2 * n) + [hbm(a) for a in srcs] + [hbm(a) for a in lands]
        + [jax.ShapeDtypeStruct((8, LANES), F32)],
        input_output_aliases={k: 2 * n + k for k in range(2 * n)},
        compiler_params=pltpu.CompilerParams(has_side_effects=DATAFLOW),
    )(*[pltpu.with_memory_space_constraint(a, pltpu.HBM) for a in list(srcs) + list(lands) + [after]])
    return (out[:n], out[n:2 * n], out[2 * n:3 * n], out[3 * n:4 * n]), out[-1][0, 0]


def _split_wait(name, n_copies, started, after):
    send_sems, recv_sems, srcs, lands = started
    n = len(srcs)

    def body(*refs):
        for k in range(n):
            slots = refs[n + k].at[pl.ds(0, n_copies)]
            all_copies = pltpu.make_async_remote_copy(
                src_ref=slots, dst_ref=slots, send_sem=refs[2 * n + k], recv_sem=refs[3 * n + k],
                device_id=_place()[0], device_id_type=MESH)
            all_copies.wait_send()
            all_copies.wait_recv()

    hbm = lambda a: pltpu.HBM(a.shape, a.dtype)
    out = pl.pallas_call(
        body, name=name,
        in_specs=[HBM_SPEC] * (2 * n) + [SEM_SPEC] * (2 * n) + [HBM_SPEC],
        out_specs=[HBM_SPEC] * (2 * n),
        out_shape=[hbm(a) for a in srcs] + [hbm(a) for a in lands],
        input_output_aliases={k: k for k in range(2 * n)},
        compiler_params=pltpu.CompilerParams(has_side_effects=DATAFLOW),
    )(*srcs, *lands, *send_sems, *recv_sems, pltpu.with_memory_space_constraint(after, pltpu.HBM))
    return out[:n], out[n:]


def _slot(chip, core):
    return 4 * chip[0] + 2 * chip[1] + core


def _gather_copies(src, land, send_sem, recv_sem):
    (x, y, c), sib, chips = _place()
    return [pltpu.make_async_remote_copy(src_ref=src, dst_ref=land.at[_slot((x, y), c)], send_sem=send_sem,
                                         recv_sem=recv_sem, device_id=to, device_id_type=MESH)
            for to in [sib] + [(*chip, c) for chip in chips]]


def _pass_on_copies(src, land, send_sem, recv_sem):
    (x, y, c), sib, chips = _place()
    return [pltpu.make_async_remote_copy(src_ref=land.at[_slot(chip, c)], dst_ref=land.at[_slot(chip, c)],
                                         send_sem=send_sem, recv_sem=recv_sem, device_id=sib, device_id_type=MESH)
            for chip in chips]


def _gather_pass_on(lands, name):
    n = len(lands)

    def body(*refs):
        ins, outs = refs[:n], refs[n:2 * n]
        send_sems, recv_sems = refs[2 * n:]
        (x, y, c), sib, chips = _place()
        sends = []
        for k in range(n):
            for j, chip in enumerate(chips):
                sends.append(pltpu.make_async_remote_copy(
                    src_ref=ins[k].at[_slot(chip, c)], dst_ref=outs[k].at[_slot(chip, c)],
                    send_sem=send_sems.at[k, j], recv_sem=recv_sems.at[k, j], device_id=sib, device_id_type=MESH))
        for cp in sends:
            cp.start()
        for cp in sends:
            cp.wait_recv()
        for cp in sends:
            cp.wait_send()

    any_spec = pl.BlockSpec(memory_space=pl.ANY)
    return pl.pallas_call(
        body, name=name,
        in_specs=[any_spec] * n, out_specs=[any_spec] * n,
        out_shape=[jax.ShapeDtypeStruct(a.shape, a.dtype) for a in lands],
        input_output_aliases={k: k for k in range(n)},
        scratch_shapes=[pltpu.SemaphoreType.DMA((n, 3))] * 2,
        compiler_params=pltpu.CompilerParams(has_side_effects=True),
    )(*lands)


def _chip_copies(src, land, send_sem, recv_sem):
    (x, y, c), _, chips = _place()
    return [pltpu.make_async_remote_copy(src_ref=src.at[j], dst_ref=land.at[j], send_sem=send_sem,
                                         recv_sem=recv_sem, device_id=(*chip, c), device_id_type=MESH)
            for j, chip in enumerate(chips)]


ROW_TILE_BYTES = 14 * 1024 * 1024


def _stream_tile(r, c, bytes_per_elem):
    if r * c * bytes_per_elem <= ROW_TILE_BYTES:
        return r, c
    rows = [t for t in range(16, r, 16) if r % t == 0 and t * c * bytes_per_elem <= ROW_TILE_BYTES]
    if rows:
        return max(rows), c
    cols = [t for t in range(LANES, c, LANES) if c % t == 0 and r * t * bytes_per_elem <= ROW_TILE_BYTES]
    return r, max(cols)


def _sibling_copies(src, land, send_sem, recv_sem):
    (x, y, c), sib, _ = _place()
    return [pltpu.make_async_remote_copy(src_ref=src.at[2 * q + 1 - c], dst_ref=land.at[q], send_sem=send_sem,
                                         recv_sem=recv_sem, device_id=sib, device_id_type=MESH)
            for q in range(4)]


def _pair_sum(slots, blocks, theirs, name):
    _, r, c = theirs.shape
    tr, tc = _stream_tile(r, c, 3 * theirs.dtype.itemsize)

    def body(slot_ref, a_ref, b_ref, o_ref):
        o_ref[...] = (a_ref[...].astype(F32) + b_ref[...].astype(F32)).astype(o_ref.dtype)

    return pl.pallas_call(
        body, name=name,
        grid_spec=pltpu.PrefetchScalarGridSpec(
            num_scalar_prefetch=1, grid=(3, r // tr, c // tc),
            in_specs=[pl.BlockSpec((1, tr, tc), lambda j, a, b, slot_ref: (slot_ref[j], a, b)),
                      pl.BlockSpec((1, tr, tc), lambda j, a, b, slot_ref: (slot_ref[3 + j], a, b))],
            out_specs=pl.BlockSpec((1, tr, tc), lambda j, a, b, slot_ref: (j, a, b))),
        out_shape=jax.ShapeDtypeStruct((3, r, c), theirs.dtype),
        compiler_params=_params(("parallel", "parallel", "parallel")),
    )(slots, blocks, theirs)


def _adamw_small(parts, ws, ms, vs):
    n = len(ws)
    c1 = 1.0 - ADAM_B1
    c2 = 1.0 - ADAM_B2
    bc1 = 1.0 - ADAM_B1 ** ADAM_STEP
    bc2 = 1.0 - ADAM_B2 ** ADAM_STEP

    def body(*refs):
        p_ref, outs = refs[0], refs[1 + 3 * n:]
        off = 0
        for k in range(n):
            w_ref, m_ref, v_ref = refs[1 + k], refs[1 + n + k], refs[1 + 2 * n + k]
            width = w_ref.shape[1]
            g = p_ref[0, :, off:off + width]
            for j in range(1, N_DEV):
                g = g + p_ref[j, :, off:off + width]
            nm = ADAM_B1 * m_ref[...] + c1 * g
            nv = ADAM_B2 * v_ref[...] + c2 * (g * g)
            outs[k][...] = g
            outs[n + k][...] = -ADAM_LR * ((nm / bc1) / (jnp.sqrt(nv / bc2) + ADAM_EPS) + ADAM_WD * w_ref[...])
            outs[2 * n + k][...] = nm
            outs[3 * n + k][...] = nv
            off += width
        total = p_ref[0, :, off:off + LANES]
        for j in range(1, N_DEV):
            total = total + p_ref[j, :, off:off + LANES]
        outs[4 * n][...] = total

    out = pl.pallas_call(
        body, name="adamw_small",
        out_shape=[jax.ShapeDtypeStruct(a.shape, F32) for a in ws] * 4 + [jax.ShapeDtypeStruct((1, LANES), F32)],
        compiler_params=pltpu.CompilerParams(vmem_limit_bytes=VMEM_LIMIT),
    )(parts, *ws, *ms, *vs)
    return out[:n], out[n:2 * n], out[2 * n:3 * n], out[3 * n:4 * n], out[4 * n]


def _reduce_adamw(parts, w, m, v, name, blocks, theirs, slots):
    r, c = w.shape
    n_parts = parts.shape[0]
    tr, tc = _stream_tile(r, c, (n_parts + 2) * parts.dtype.itemsize + 7 * 4)
    c1 = 1.0 - ADAM_B1
    c2 = 1.0 - ADAM_B2
    bc1 = 1.0 - ADAM_B1 ** ADAM_STEP
    bc2 = 1.0 - ADAM_B2 ** ADAM_STEP

    def body(_, b_ref, t_ref, p_ref, w_ref, m_ref, v_ref, g_ref, d_ref, nm_ref, nv_ref):
        g = b_ref[0].astype(F32) + t_ref[0].astype(F32)
        for j in range(n_parts):
            g = g + p_ref[j].astype(F32)
        nm = ADAM_B1 * m_ref[...] + c1 * g
        nv = ADAM_B2 * v_ref[...] + c2 * (g * g)
        g_ref[...] = g
        nm_ref[...] = nm
        nv_ref[...] = nv
        d_ref[...] = -ADAM_LR * ((nm / bc1) / (jnp.sqrt(nv / bc2) + ADAM_EPS) + ADAM_WD * w_ref[...])

    out = jax.ShapeDtypeStruct((r, c), F32)
    grid = (r // tr, c // tc)
    blk = pl.BlockSpec((tr, tc), lambda i, j, slot_ref: (i, j))
    return pl.pallas_call(
        body, name=name,
        grid_spec=pltpu.PrefetchScalarGridSpec(
            num_scalar_prefetch=1, grid=grid,
            in_specs=[pl.BlockSpec((1, tr, tc), lambda i, j, slot_ref: (slot_ref[0], i, j)),
                      pl.BlockSpec((1, tr, tc), lambda i, j, slot_ref: (slot_ref[1], i, j)),
                      pl.BlockSpec((n_parts, tr, tc), lambda i, j, slot_ref: (0, i, j)), blk, blk, blk],
            out_specs=[blk] * 4),
        out_shape=[out] * 4,
        compiler_params=_params(("parallel", "parallel")),
    )(slots, blocks, theirs, parts, w, m, v)


_MIX = ("w_in", "w_uq", "w_ukv", "conv_w", "w_out")
_FFN = ("w_gate", "w_up", "w_down")
_BIG = _MIX + _FFN
_TRANSPOSED = ("w_in", "w_uq", "w_gate", "w_up")
_SMALL = ("pre_mix_norm", "q_norm", "kv_norm", "conv_b", "conv_ln_g", "conv_ln_b", "conv_out_norm",
          "attn_out_norm", "post_mix_norm", "pre_ffn_norm", "post_ffn_norm")
_ORDER = ("pre_mix_norm", "w_in", "q_norm", "w_uq", "kv_norm", "w_ukv", "conv_w", "conv_b", "conv_ln_g",
          "conv_ln_b", "conv_out_norm", "attn_out_norm", "w_out", "post_mix_norm", "pre_ffn_norm", "w_gate",
          "w_up", "w_down", "post_ffn_norm")


def _cols_from_shards(g):
    return jnp.transpose(g, (1, 0, 2)).reshape(g.shape[1], N_DEV * g.shape[2])


def _cols_to_shards(w):
    k, n8 = w.shape
    return jnp.transpose(w.reshape(k, N_DEV, n8 // N_DEV), (1, 0, 2))


def _step(x, positions, loss_target, w, m, v):
    s, d = x.shape[1], x.shape[2]
    x2, tgt = x[0], loss_target[0]
    pos = positions.reshape(s, 1)
    vecs = {n: w[n] for n in _SMALL}
    core = lax.axis_index("c").astype(jnp.int32)
    my_chip = (2 * lax.axis_index("x") + lax.axis_index("y")).astype(jnp.int32)
    other_chips = [my_chip ^ 2, my_chip ^ 1, my_chip ^ 3]
    pair_slots = jnp.stack([2 * q + core for q in other_chips] + other_chips)
    own_slots = jnp.stack([2 * my_chip + core, my_chip])
    own = {}
    n_in_cols = N_DEV * w["w_in"].shape[2]
    gathers, scatters, to_sibling = {}, {}, []

    def shard(t, n):
        return t[n][0].T if n in _TRANSPOSED else t[n][0]

    def gather_start(names, tag, after, zero=0.0):
        srcs = [w[n][0] if n == "conv_w" else (shard(w, n) + zero).astype(BF16) for n in names]
        lands = [lax.empty((N_DEV,) + a.shape, a.dtype) for a in srcs]
        gathers[tag], zero = _split_start("gather_" + tag + "_start", _gather_copies, srcs, lands, after)
        return zero

    def gather_finish(names, tag, after):
        srcs, lands = _split_wait("gather_" + tag + "_wait", 4, gathers[tag], after)
        lands = _gather_pass_on(lands, "gather_" + tag + "_pass_on")
        me = _my_index()
        return {n: lax.dynamic_update_slice(g, a[None], (me,) + (0,) * a.ndim) for n, g, a in zip(names, lands, srcs)}

    def in_weights_fn(h):
        w_in_g = gather_finish(("w_in",), "in", h)["w_in"]
        return jnp.pad(w_in_g.reshape(-1, d), ((0, LANES - QK_ROPE), (0, 0))), 0.0

    def mix_weights_fn(z):
        gath = gather_finish(_MIX[1:], "mix", z)
        w_uq_t = jnp.pad(gath["w_uq"], ((0, 0), (0, HEAD_PAD - QK_HEAD), (0, 0))).reshape(N_HEADS * HEAD_PAD, -1)
        return (w_uq_t, _cols_from_shards(gath["w_ukv"]), _cols_from_shards(gath["conv_w"]),
                gath["w_out"].reshape(-1, d))

    def up_weights_fn(hf):
        srcs, lands = _split_wait("gather_up_wait", 4, gathers["up"], hf)
        me = _my_index()
        lands = [lax.dynamic_update_slice(g, a[None], (me, 0, 0)) for g, a in zip(lands, srcs)]
        passing, _ = _split_start("gather_up_pass_on_start", _pass_on_copies, srcs, lands, hf)
        here = jnp.stack([me, me ^ 1] + [2 * q + core for q in other_chips])
        late = jnp.stack([2 * q + 1 - core for q in other_chips])

        def rest_fn(after):
            _, done = _split_wait("gather_up_pass_on_wait", 3, passing, after)
            return done[0], done[1], late

        return passing[3][0], passing[3][1], here, rest_fn

    def down_weights_fn(act):
        return gather_finish(("w_down",), "down", act)["w_down"]

    def pairs_of_pending(after):
        names, tag, sent = to_sibling.pop()
        blocks, theirs = _split_wait("to_sibling_" + tag + "_wait", 4, sent, after)
        own.update(zip(names, zip(blocks, theirs)))
        return tag, [_pair_sum(pair_slots, b, t, "pair_sum_" + n) for n, b, t in zip(names, blocks, theirs)]

    to_blocks = {
        "w_in": lambda a: a[:n_in_cols].reshape(N_DEV, -1, d),
        "w_uq": lambda a: a.reshape(N_HEADS, HEAD_PAD, -1)[:, :QK_HEAD],
        "w_ukv": _cols_to_shards, "conv_w": _cols_to_shards,
        "w_out": lambda a: a.reshape(N_DEV, -1, d),
    }

    def grads_fn(tag, names, grads, after):
        pending, pairs = pairs_of_pending(after) if to_sibling else ("", [])
        blocks = [to_blocks.get(n, lambda a: a)(a) for n, a in zip(names, grads)]
        if not pairs and not blocks:
            return 0.0
        lands = [lax.empty((3,) + p.shape[1:], p.dtype) for p in pairs] + \
                [lax.empty((4,) + b.shape[1:], b.dtype) for b in blocks]
        plans = [_chip_copies] * len(pairs) + [_sibling_copies] * len(blocks)
        started, zero = _split_start("start_" + pending + "_" + tag, plans, pairs + blocks, lands, after)
        k = len(pairs)
        if pairs:
            scatters[pending] = tuple(part[:k] for part in started)
        if blocks:
            to_sibling.append((names, tag, tuple(part[k:] for part in started)))
        return zero

    zero = gather_start(("w_in",), "in", x2)
    zero = gather_start(_MIX[1:], "mix", x2, zero)
    zero = gather_start(("w_gate", "w_up"), "up", x2, zero)
    vecs["pre_mix_norm"] = vecs["pre_mix_norm"] + gather_start(("w_down",), "down", x2, zero)
    loss, grad_x, g = _local_step(x2, pos, tgt, vecs, in_weights_fn, mix_weights_fn, up_weights_fn, down_weights_fn,
                                  grads_fn)

    small = jnp.concatenate([g[n] for n in _SMALL] + [loss], axis=1)
    small_started, zero = _split_start("gather_small_start", _small_copies, [small],
                                       [lax.empty((N_DEV,) + small.shape, F32)], grad_x)

    res = {}
    after = grad_x
    own_slots = own_slots + zero.astype(jnp.int32)
    for tag, names in (("down", ("w_down",)), ("up", ("w_gate", "w_up")), ("out", ("w_out",)),
                       ("in", ("w_in", "w_uq", "w_ukv", "conv_w"))):
        _, recv = _split_wait("scatter_" + tag + "_wait", 3, scatters[tag], after)
        for n, parts in zip(names, recv):
            res[n] = _reduce_adamw(parts, shard(w, n), shard(m, n), shard(v, n), "adamw_" + n, *own[n], own_slots)
            after = res[n][1]
            res[n] = [(t.T if n in _TRANSPOSED else t)[None] for t in res[n]]
    (small,), (small_all,) = _split_wait("gather_small_wait", N_DEV - 1, small_started, after)
    small_all = lax.dynamic_update_slice(small_all, small[None], (_my_index(), 0, 0))
    *small_res, total = _adamw_small(small_all, *[[t[n] for n in _SMALL] for t in (w, m, v)])
    for k, n in enumerate(_SMALL):
        res[n] = [part[k] for part in small_res]

    outs = [total[0, 0], grad_x[None]]
    for part in range(4):
        outs.extend(res[n][part] for n in _ORDER)
    return tuple(outs)


def kernel(x, positions, pre_mix_norm, w_in, q_norm, w_uq, kv_norm, w_ukv, conv_w, conv_b, conv_ln_g, conv_ln_b, conv_out_norm, attn_out_norm, w_out, post_mix_norm, pre_ffn_norm, w_gate, w_up, w_down, post_ffn_norm, loss_target, m_pre_mix_norm, m_w_in, m_q_norm, m_w_uq, m_kv_norm, m_w_ukv, m_conv_w, m_conv_b, m_conv_ln_g, m_conv_ln_b, m_conv_out_norm, m_attn_out_norm, m_w_out, m_post_mix_norm, m_pre_ffn_norm, m_w_gate, m_w_up, m_w_down, m_post_ffn_norm, v_pre_mix_norm, v_w_in, v_q_norm, v_w_uq, v_kv_norm, v_w_ukv, v_conv_w, v_conv_b, v_conv_ln_g, v_conv_ln_b, v_conv_out_norm, v_attn_out_norm, v_w_out, v_post_mix_norm, v_pre_ffn_norm, v_w_gate, v_w_up, v_w_down, v_post_ffn_norm):
    w = dict(zip(_ORDER, (pre_mix_norm, w_in, q_norm, w_uq, kv_norm, w_ukv, conv_w, conv_b, conv_ln_g, conv_ln_b,
                          conv_out_norm, attn_out_norm, w_out, post_mix_norm, pre_ffn_norm, w_gate, w_up, w_down,
                          post_ffn_norm)))
    m = dict(zip(_ORDER, (m_pre_mix_norm, m_w_in, m_q_norm, m_w_uq, m_kv_norm, m_w_ukv, m_conv_w, m_conv_b,
                          m_conv_ln_g, m_conv_ln_b, m_conv_out_norm, m_attn_out_norm, m_w_out, m_post_mix_norm,
                          m_pre_ffn_norm, m_w_gate, m_w_up, m_w_down, m_post_ffn_norm)))
    v = dict(zip(_ORDER, (v_pre_mix_norm, v_w_in, v_q_norm, v_w_uq, v_kv_norm, v_w_ukv, v_conv_w, v_conv_b,
                          v_conv_ln_g, v_conv_ln_b, v_conv_out_norm, v_attn_out_norm, v_w_out, v_post_mix_norm,
                          v_pre_ffn_norm, v_w_gate, v_w_up, v_w_down, v_post_ffn_norm)))
    return _step(x, positions, loss_target, w, m, v)
```

```python
import functools

import jax
import jax.numpy as jnp
from jax import lax
from jax.experimental import pallas as pl
from jax.experimental.pallas import tpu as pltpu

N_DEV = 8
N_HEADS = 8
QK_NOPE = 128
QK_ROPE = 64
V_HEAD = 128
QK_HEAD = QK_NOPE + QK_ROPE
HEAD_PAD = 256
LANES = 128
ATTN_BLOCK = 512
CONV_K = 31
CONV_PAD = 32
EPS = 1e-6
ROPE_THETA = 10000.0
ADAM_LR = 0.001
ADAM_B1 = 0.9
ADAM_B2 = 0.999
ADAM_EPS = 1e-08
ADAM_WD = 0.01
ADAM_STEP = 10
VMEM_LIMIT = 56 * 1024 * 1024
F32 = jnp.float32
BF16 = jnp.bfloat16
MESH = pl.DeviceIdType.MESH
NEG = -1e30


def _pick(n, prefs):
    for p in prefs:
        if p <= n and n % p == 0:
            return p
    return n


def _params(sem):
    return pltpu.CompilerParams(dimension_semantics=sem, vmem_limit_bytes=VMEM_LIMIT)


_DIMS = {"nn": (((1,), (0,)), ((), ())), "nt": (((1,), (1,)), ((), ())), "tn": (((0,), (0,)), ((), ()))}


MM_VMEM_BUDGET = 40 * 1024 * 1024
MM_MAX_MACS = 3 * 1024 ** 3


V7X_HBM_BYTES_PER_S = 3.0e12
V7X_MXU_MACS_PER_S = 0.45e15
GRID_STEP_S = 0.35e-6


def _mm_tiles(m, n, k, size_a, size_b, size_o):
    best = None
    for tm in sorted({m, 1024, 512, 256, 128}, reverse=True):
        if tm > m or m % tm:
            continue
        for tn in sorted({n, 2048, 1024, 512, 384, 256, 128}, reverse=True):
            if tn > n or n % tn:
                continue
            vmem = 2 * (tm * k * size_a + k * tn * size_b + tm * tn * size_o)
            if vmem > MM_VMEM_BUDGET or tm * tn * k > MM_MAX_MACS:
                continue
            b_reads = 1 if tn == n else m // tm
            traffic = m * k * size_a + b_reads * k * n * size_b + m * n * size_o
            exposed = tm * k * size_a + k * tn * size_b + tm * tn * size_o
            steps = (m // tm) * (n // tn)
            key = (max(traffic / V7X_HBM_BYTES_PER_S, m * n * k / V7X_MXU_MACS_PER_S)
                   + exposed / V7X_HBM_BYTES_PER_S + steps * GRID_STEP_S)
            if best is None or key < best[0]:
                best = (key, tm, tn)
    assert best is not None, (m, n, k)
    return best[1], best[2]


def _mm(a, b, mode, name, out_dtype=F32, behind=None):
    if mode == "nn":
        (m, k), (k2, n) = a.shape, b.shape
    elif mode == "nt":
        (m, k), (n, k2) = a.shape, b.shape
    else:
        (k, m), (k2, n) = a.shape, b.shape
    assert k == k2, (a.shape, b.shape, mode)
    tm, tn = _mm_tiles(m, n, k, a.dtype.itemsize, b.dtype.itemsize, jnp.dtype(out_dtype).itemsize)
    dims = _DIMS[mode]

    def body(a_ref, b_ref, *rest):
        acc = lax.dot_general(a_ref[...].astype(BF16), b_ref[...].astype(BF16), dims, preferred_element_type=F32)
        rest[-1][...] = acc.astype(rest[-1].dtype)

    if mode == "tn":
        a_spec = pl.BlockSpec((k, tm), lambda i, j: (0, i))
    else:
        a_spec = pl.BlockSpec((tm, k), lambda i, j: (i, 0))
    if mode == "nt":
        b_spec = pl.BlockSpec((tn, k), lambda i, j: (j, 0))
    else:
        b_spec = pl.BlockSpec((k, tn), lambda i, j: (0, j))
    o_spec = pl.BlockSpec((tm, tn), lambda i, j: (i, j))
    extra = [] if behind is None else [behind]
    return pl.pallas_call(
        body, name=name,
        grid=(m // tm, n // tn),
        in_specs=[a_spec, b_spec] + [pl.BlockSpec((8, LANES), lambda i, j: (0, 0))] * len(extra),
        out_specs=o_spec,
        out_shape=jax.ShapeDtypeStruct((m, n), out_dtype),
        compiler_params=_params(("parallel", "parallel")),
    )(a, b, *extra)


def _sigmoid(x):
    return 1.0 / (1.0 + jnp.exp(-x))


def _rms(x, g):
    r = lax.rsqrt(jnp.mean(x * x, axis=-1, keepdims=True) + EPS)
    return (x * r) * g


def _rms_bwd(x, g, dy):
    r = lax.rsqrt(jnp.mean(x * x, axis=-1, keepdims=True) + EPS)
    xh = x * r
    dyg = dy * g
    dx = r * (dyg - xh * jnp.mean(dyg * xh, axis=-1, keepdims=True))
    return dx, jnp.sum(dy * xh, axis=0, keepdims=True)


def _ln(x, g, b):
    mu = jnp.mean(x, axis=-1, keepdims=True)
    xc = x - mu
    rs = lax.rsqrt(jnp.mean(xc * xc, axis=-1, keepdims=True) + EPS)
    return (xc * rs) * g + b


def _ln_bwd(x, g, dy):
    mu = jnp.mean(x, axis=-1, keepdims=True)
    xc = x - mu
    rs = lax.rsqrt(jnp.mean(xc * xc, axis=-1, keepdims=True) + EPS)
    xh = xc * rs
    dyg = dy * g
    dx = rs * (dyg - jnp.mean(dyg, axis=-1, keepdims=True) - xh * jnp.mean(dyg * xh, axis=-1, keepdims=True))
    return dx, jnp.sum(dy * xh, axis=0, keepdims=True), jnp.sum(dy, axis=0, keepdims=True)


def _silu(x):
    return x * _sigmoid(x)


def _silu_grad(x):
    s = _sigmoid(x)
    return s * (1.0 + x * (1.0 - s))


def _rope(x, cos, sa, sb):
    return x * cos + pltpu.roll(x, 96, 1) * sa + pltpu.roll(x, 32, 1) * sb


def _rope_t(d, cos, sa, sb):
    return d * cos - pltpu.roll(d, 96, 1) * sa - pltpu.roll(d, 32, 1) * sb


def _rows(ts, w):
    return pl.BlockSpec((ts, w), lambda i: (i, 0))


def _vec(w):
    return pl.BlockSpec((1, w), lambda i: (0, 0))


def _acc_init(i, *refs):
    @pl.when(i == 0)
    def _():
        for r in refs:
            r[...] = jnp.zeros_like(r)


def _rope_tables(pos, inv_freq):
    s = pos.shape[0]
    ts = _pick(s, (512, 256, 128))

    def body(p_ref, f_ref, c_ref, sa_ref, sb_ref):
        ang = p_ref[...].astype(F32) * f_ref[...]
        lane = lax.broadcasted_iota(jnp.int32, ang.shape, 1)
        c, sn = jnp.cos(ang), jnp.sin(ang)
        c_ref[...] = jnp.where(lane < QK_ROPE, c, 0.0)
        sa_ref[...] = jnp.where(lane < QK_ROPE // 2, -sn, 0.0)
        sb_ref[...] = jnp.where((lane >= QK_ROPE // 2) & (lane < QK_ROPE), sn, 0.0)

    out = jax.ShapeDtypeStruct((s, LANES), F32)
    return pl.pallas_call(
        body, name="rope_tables", grid=(s // ts,),
        in_specs=[_rows(ts, 1), _vec(LANES)],
        out_specs=[_rows(ts, LANES)] * 3, out_shape=[out] * 3,
        compiler_params=_params(("parallel",)),
    )(pos, inv_freq)


def _pre_fwd(x, g):
    s, d = x.shape
    ts = _pick(s, (256, 128))

    def body(x_ref, g_ref, h_ref):
        h_ref[...] = _rms(x_ref[...], g_ref[...]).astype(BF16)

    return pl.pallas_call(
        body, name="pre_fwd", grid=(s // ts,),
        in_specs=[_rows(ts, d), _vec(d)], out_specs=_rows(ts, d),
        out_shape=jax.ShapeDtypeStruct((s, d), BF16),
        compiler_params=_params(("parallel",)),
    )(x, g)


def _split_fwd(z, gq, gkv, tabs, c, ql, kvl):
    s, zw = z.shape
    ts = _pick(s, (256, 128))
    o_q, o_kv, o_kr = 2 * c, 2 * c + ql, 2 * c + ql + kvl

    def body(z_ref, gq_ref, gkv_ref, c_ref, sa_ref, sb_ref, u0_ref, qn_ref, kvn_ref, kpe_ref):
        u0_ref[...] = z_ref[:, 0:c] * _sigmoid(z_ref[:, c:2 * c])
        qn_ref[...] = _rms(z_ref[:, o_q:o_kv], gq_ref[...]).astype(BF16)
        kvn_ref[...] = _rms(z_ref[:, o_kv:o_kr], gkv_ref[...]).astype(BF16)
        kpe_ref[...] = _rope(z_ref[:, o_kr:o_kr + LANES], c_ref[...], sa_ref[...], sb_ref[...]).astype(BF16)

    return pl.pallas_call(
        body, name="split_fwd", grid=(s // ts,),
        in_specs=[_rows(ts, zw), _vec(ql), _vec(kvl)] + [_rows(ts, LANES)] * 3,
        out_specs=[_rows(ts, c), _rows(ts, ql), _rows(ts, kvl), _rows(ts, LANES)],
        out_shape=[jax.ShapeDtypeStruct((s, c), F32), jax.ShapeDtypeStruct((s, ql), BF16),
                   jax.ShapeDtypeStruct((s, kvl), BF16), jax.ShapeDtypeStruct((s, LANES), BF16)],
        compiler_params=_params(("parallel",)),
    )(z, gq, gkv, *tabs)


def _split_bwd(du0, z, dqn, dkvn, dkpe_h, gq, gkv, tabs, c, ql, kvl):
    s, zw = z.shape
    ts = _pick(s, (256, 128))
    o_q, o_kv, o_kr = 2 * c, 2 * c + ql, 2 * c + ql + kvl

    def body(du0_ref, z_ref, dqn_ref, dkvn_ref, dkh_ref, gq_ref, gkv_ref, c_ref, sa_ref, sb_ref,
             dz_ref, dgq_ref, dgkv_ref):
        _acc_init(pl.program_id(0), dgq_ref, dgkv_ref)
        du0 = du0_ref[...]
        a = z_ref[:, 0:c]
        sg = _sigmoid(z_ref[:, c:2 * c])
        dz_ref[:, 0:c] = (du0 * sg).astype(BF16)
        dz_ref[:, c:2 * c] = (du0 * a * sg * (1.0 - sg)).astype(BF16)
        dq, dgq = _rms_bwd(z_ref[:, o_q:o_kv], gq_ref[...], dqn_ref[...])
        dz_ref[:, o_q:o_kv] = dq.astype(BF16)
        dgq_ref[...] += dgq
        dkv, dgkv = _rms_bwd(z_ref[:, o_kv:o_kr], gkv_ref[...], dkvn_ref[...])
        dz_ref[:, o_kv:o_kr] = dkv.astype(BF16)
        dgkv_ref[...] += dgkv
        dk = dkh_ref[:, 0:LANES]
        for h in range(1, N_HEADS):
            dk = dk + dkh_ref[:, h * LANES:(h + 1) * LANES]
        dz_ref[:, o_kr:o_kr + LANES] = _rope_t(dk, c_ref[...], sa_ref[...], sb_ref[...]).astype(BF16)

    return pl.pallas_call(
        body, name="split_bwd", grid=(s // ts,),
        in_specs=[_rows(ts, c), _rows(ts, zw), _rows(ts, ql), _rows(ts, kvl), _rows(ts, N_HEADS * LANES),
                  _vec(ql), _vec(kvl)] + [_rows(ts, LANES)] * 3,
        out_specs=[_rows(ts, zw), _vec(ql), _vec(kvl)],
        out_shape=[jax.ShapeDtypeStruct((s, zw), BF16), jax.ShapeDtypeStruct((1, ql), F32),
                   jax.ShapeDtypeStruct((1, kvl), F32)],
        compiler_params=_params(("arbitrary",)),
    )(du0, z, dqn, dkvn, dkpe_h, gq, gkv, *tabs)


def _q_rope(qpre, tabs):
    s, w = qpre.shape
    ts = _pick(s, (256, 128))

    def body(q_ref, c_ref, sa_ref, sb_ref, o_ref):
        cs, sa, sb = c_ref[...], sa_ref[...], sb_ref[...]
        for h in range(N_HEADS):
            lo = h * HEAD_PAD
            o_ref[:, lo:lo + QK_NOPE] = q_ref[:, lo:lo + QK_NOPE].astype(BF16)
            o_ref[:, lo + QK_NOPE:lo + HEAD_PAD] = _rope(q_ref[:, lo + QK_NOPE:lo + HEAD_PAD], cs, sa, sb).astype(BF16)

    return pl.pallas_call(
        body, name="q_rope", grid=(s // ts,),
        in_specs=[_rows(ts, w)] + [_rows(ts, LANES)] * 3, out_specs=_rows(ts, w),
        out_shape=jax.ShapeDtypeStruct((s, w), BF16),
        compiler_params=_params(("parallel",)),
    )(qpre, *tabs)


def _conv_fwd(u0, w, b):
    s, c = u0.shape
    tc = LANES
    rc = _pick(s, (256, 128))

    def body(u_ref, w_ref, b_ref, o_ref, pad_ref):
        pad_ref[0:CONV_PAD, :] = jnp.zeros((CONV_PAD, tc), F32)
        pad_ref[CONV_PAD:CONV_PAD + s, :] = u_ref[...]
        for r in range(s // rc):
            acc = jnp.broadcast_to(b_ref[...], (rc, tc))
            for k in range(CONV_K):
                lo = r * rc + CONV_PAD - (CONV_K - 1) + k
                acc = acc + w_ref[k:k + 1, :] * pad_ref[lo:lo + rc, :]
            o_ref[r * rc:(r + 1) * rc, :] = acc

    col = lambda j: (0, j)
    return pl.pallas_call(
        body, name="conv_fwd", grid=(c // tc,),
        in_specs=[pl.BlockSpec((s, tc), col), pl.BlockSpec((CONV_K, tc), col), pl.BlockSpec((1, tc), col)],
        out_specs=pl.BlockSpec((s, tc), col),
        out_shape=jax.ShapeDtypeStruct((s, c), F32),
        scratch_shapes=[pltpu.VMEM((s + CONV_PAD, tc), F32)],
        compiler_params=_params(("parallel",)),
    )(u0, w, b)


def _conv_bwd(du1, u0, w):
    s, c = u0.shape
    tc = LANES
    rc = _pick(s, (256, 128))

    def body(d_ref, u_ref, w_ref, du_ref, dw_ref, db_ref, upad_ref, dpad_ref):
        upad_ref[0:CONV_PAD, :] = jnp.zeros((CONV_PAD, tc), F32)
        upad_ref[CONV_PAD:CONV_PAD + s, :] = u_ref[...]
        dpad_ref[0:s, :] = d_ref[...]
        dpad_ref[s:s + CONV_PAD, :] = jnp.zeros((CONV_PAD, tc), F32)
        for r in range(s // rc):
            acc = jnp.zeros((rc, tc), F32)
            for k in range(CONV_K):
                lo = r * rc + (CONV_K - 1) - k
                acc = acc + w_ref[k:k + 1, :] * dpad_ref[lo:lo + rc, :]
            du_ref[r * rc:(r + 1) * rc, :] = acc
        for k in range(CONV_K):
            acc8 = jnp.zeros((8, tc), F32)
            for r in range(s // rc):
                lo = r * rc + CONV_PAD - (CONV_K - 1) + k
                prod = d_ref[r * rc:(r + 1) * rc, :] * upad_ref[lo:lo + rc, :]
                acc8 = acc8 + jnp.sum(prod.reshape(rc // 8, 8, tc), axis=0)
            dw_ref[k:k + 1, :] = jnp.sum(acc8, axis=0, keepdims=True)
        db_ref[...] = jnp.sum(d_ref[...], axis=0, keepdims=True)

    col = lambda j: (0, j)
    return pl.pallas_call(
        body, name="conv_bwd", grid=(c // tc,),
        in_specs=[pl.BlockSpec((s, tc), col), pl.BlockSpec((s, tc), col), pl.BlockSpec((CONV_K, tc), col)],
        out_specs=[pl.BlockSpec((s, tc), col), pl.BlockSpec((CONV_K, tc), col), pl.BlockSpec((1, tc), col)],
        out_shape=[jax.ShapeDtypeStruct((s, c), F32), jax.ShapeDtypeStruct((CONV_K, c), F32),
                   jax.ShapeDtypeStruct((1, c), F32)],
        scratch_shapes=[pltpu.VMEM((s + CONV_PAD, tc), F32), pltpu.VMEM((s + CONV_PAD, tc), F32)],
        compiler_params=_params(("parallel",)),
    )(du1, u0, w)


def _causal_mask(sc, qi, kj, tq, tk):
    rows = qi * tq + lax.broadcasted_iota(jnp.int32, sc.shape, 0)
    cols = kj * tk + lax.broadcasted_iota(jnp.int32, sc.shape, 1)
    return jnp.where(cols <= rows, sc, NEG)


def _attn_fwd(q, kv, kpe):
    s = q.shape[0]
    tq = tk = _pick(s, (ATTN_BLOCK, 256, 128))
    reps = tk // LANES
    scale = QK_HEAD ** -0.5
    nt = (((1,), (1,)), ((), ()))

    def body(q_ref, kn_ref, v_ref, kpe_ref, o_ref, lse_ref, kf_ref, vb_ref, m_ref, l_ref, acc_ref):
        i = pl.program_id(1)

        @pl.when(i == 0)
        def _():
            kf_ref[:, 0:QK_NOPE] = kn_ref[...].astype(BF16)
            kf_ref[:, QK_NOPE:HEAD_PAD] = kpe_ref[...]
            vb_ref[...] = v_ref[...].astype(BF16)

        qb = q_ref[...]
        m_ref[...] = jnp.full((tq, LANES), NEG, F32)
        l_ref[...] = jnp.zeros((tq, LANES), F32)
        acc_ref[...] = jnp.zeros((tq, V_HEAD), F32)

        def block(j, diagonal):
            off = j * tk
            sc = lax.dot_general(qb, kf_ref[pl.ds(off, tk), :], nt, preferred_element_type=F32) * scale
            if diagonal:
                sc = _causal_mask(sc, 0, 0, tq, tk)
            m_prev = m_ref[...]
            m_new = jnp.maximum(m_prev, jnp.max(sc, axis=1, keepdims=True))
            p = jnp.exp(sc - jnp.tile(m_new, (1, reps)))
            alpha = jnp.exp(m_prev - m_new)
            l_ref[...] = alpha * l_ref[...] + jnp.sum(p, axis=1, keepdims=True)
            acc_ref[...] = alpha * acc_ref[...] + jnp.dot(p.astype(BF16), vb_ref[pl.ds(off, tk), :],
                                                          preferred_element_type=F32)
            m_ref[...] = m_new

        for qi in range(s // tq):
            @pl.when(i == qi)
            def _(qi=qi):
                for j in range(qi):
                    block(j, False)
                block(qi, True)

        o_ref[...] = acc_ref[...] / l_ref[...]
        lse_ref[...] = m_ref[...] + jnp.log(l_ref[...])

    return pl.pallas_call(
        body, name="attn_fwd", grid=(N_HEADS, s // tq),
        in_specs=[pl.BlockSpec((tq, HEAD_PAD), lambda h, i: (i, h)),
                  pl.BlockSpec((s, QK_NOPE), lambda h, i: (0, 2 * h)),
                  pl.BlockSpec((s, V_HEAD), lambda h, i: (0, 2 * h + 1)),
                  pl.BlockSpec((s, LANES), lambda h, i: (0, 0))],
        out_specs=[pl.BlockSpec((tq, V_HEAD), lambda h, i: (i, h)),
                   pl.BlockSpec((tq, LANES), lambda h, i: (i, h))],
        out_shape=[jax.ShapeDtypeStruct((s, N_HEADS * V_HEAD), F32),
                   jax.ShapeDtypeStruct((s, N_HEADS * LANES), F32)],
        scratch_shapes=[pltpu.VMEM((s, HEAD_PAD), BF16), pltpu.VMEM((s, V_HEAD), BF16),
                        pltpu.VMEM((tq, LANES), F32), pltpu.VMEM((tq, LANES), F32), pltpu.VMEM((tq, V_HEAD), F32)],
        compiler_params=_params(("parallel", "arbitrary")),
    )(q, kv, kv, kpe)


def _attn_bwd(q, kv, kpe, o, do, lse, tabs):
    s = q.shape[0]
    tq = tk = _pick(s, (ATTN_BLOCK, 256, 128))
    nq = s // tq
    reps = tk // LANES
    scale = QK_HEAD ** -0.5
    nt = (((1,), (1,)), ((), ()))
    tn = (((0,), (0,)), ((), ()))

    def body(q_ref, kn_ref, v_ref, kpe_ref, o_ref, do_ref, lse_ref, c_ref, sa_ref, sb_ref, dqpre_ref, dkv_ref,
             dkpe_ref, kf_ref, vb_ref, dk_ref, dv_ref, dq_ref):
        j = pl.program_id(1)

        @pl.when(j == 0)
        def _():
            dq_ref[...] = jnp.zeros_like(dq_ref)

        kf_ref[:, 0:QK_NOPE] = kn_ref[...].astype(BF16)
        kf_ref[:, QK_NOPE:HEAD_PAD] = kpe_ref[...]
        vb_ref[...] = v_ref[...].astype(BF16)
        dk_ref[...] = jnp.zeros_like(dk_ref)
        dv_ref[...] = jnp.zeros_like(dv_ref)

        def block(i, diagonal):
            off = i * tq
            qb = q_ref[pl.ds(off, tq), :]
            dob = do_ref[pl.ds(off, tq), :]
            delta = jnp.sum(dob * o_ref[pl.ds(off, tq), :], axis=1, keepdims=True)
            sc = lax.dot_general(qb, kf_ref[...], nt, preferred_element_type=F32) * scale
            if diagonal:
                sc = _causal_mask(sc, 0, 0, tq, tk)
            p = jnp.exp(sc - jnp.tile(lse_ref[pl.ds(off, tq), :], (1, reps)))
            dob16 = dob.astype(BF16)
            dv_ref[...] += lax.dot_general(p.astype(BF16), dob16, tn, preferred_element_type=F32)
            dp = lax.dot_general(dob16, vb_ref[...], nt, preferred_element_type=F32)
            ds = (p * (dp - delta) * scale).astype(BF16)
            dq_ref[pl.ds(off, tq), :] += jnp.dot(ds, kf_ref[...], preferred_element_type=F32)
            dk_ref[...] += lax.dot_general(ds, qb, tn, preferred_element_type=F32)

        for kj in range(nq):
            @pl.when(j == kj)
            def _(kj=kj):
                block(kj, True)
                for i in range(kj + 1, nq):
                    block(i, False)

        dkv_ref[:, 0:QK_NOPE] = dk_ref[:, 0:QK_NOPE]
        dkv_ref[:, QK_NOPE:HEAD_PAD] = dv_ref[...]
        dkpe_ref[...] = dk_ref[:, QK_NOPE:HEAD_PAD]

        @pl.when(j == nq - 1)
        def _():
            dqpre_ref[:, 0:QK_NOPE] = dq_ref[:, 0:QK_NOPE].astype(BF16)
            dqpre_ref[:, QK_NOPE:HEAD_PAD] = _rope_t(dq_ref[:, QK_NOPE:HEAD_PAD], c_ref[...], sa_ref[...],
                                                     sb_ref[...]).astype(BF16)

    head_rows = lambda w: pl.BlockSpec((s, w), lambda h, j: (0, h))
    table = pl.BlockSpec((s, LANES), lambda h, j: (0, 0))
    return pl.pallas_call(
        body, name="attn_bwd", grid=(N_HEADS, s // tk),
        in_specs=[head_rows(HEAD_PAD),
                  pl.BlockSpec((tk, QK_NOPE), lambda h, j: (j, 2 * h)),
                  pl.BlockSpec((tk, V_HEAD), lambda h, j: (j, 2 * h + 1)),
                  pl.BlockSpec((tk, LANES), lambda h, j: (j, 0)),
                  head_rows(V_HEAD), head_rows(V_HEAD), head_rows(LANES), table, table, table],
        out_specs=[head_rows(HEAD_PAD),
                   pl.BlockSpec((tk, HEAD_PAD), lambda h, j: (j, h)),
                   pl.BlockSpec((tk, LANES), lambda h, j: (j, h))],
        out_shape=[jax.ShapeDtypeStruct((s, N_HEADS * HEAD_PAD), BF16),
                   jax.ShapeDtypeStruct((s, N_HEADS * HEAD_PAD), F32),
                   jax.ShapeDtypeStruct((s, N_HEADS * LANES), F32)],
        scratch_shapes=[pltpu.VMEM((tk, HEAD_PAD), BF16), pltpu.VMEM((tk, V_HEAD), BF16),
                        pltpu.VMEM((tk, HEAD_PAD), F32), pltpu.VMEM((tk, V_HEAD), F32),
                        pltpu.VMEM((s, HEAD_PAD), F32)],
        compiler_params=_params(("parallel", "arbitrary")),
    )(q, kv, kv, kpe, o, do, lse, *tabs)


def _mix_fwd(u1, lng, lnb, gcon, attn, gattn):
    s, c = u1.shape
    ac = attn.shape[1]
    ts = _pick(s, (256, 128))

    def body(u_ref, lg_ref, lb_ref, gc_ref, a_ref, ga_ref, o_ref):
        t3 = _silu(_ln(u_ref[...], lg_ref[...], lb_ref[...]))
        o_ref[:, 0:c] = _rms(t3, gc_ref[...]).astype(BF16)
        o_ref[:, c:c + ac] = _rms(a_ref[...], ga_ref[...]).astype(BF16)

    return pl.pallas_call(
        body, name="mix_fwd", grid=(s // ts,),
        in_specs=[_rows(ts, c), _vec(c), _vec(c), _vec(c), _rows(ts, ac), _vec(ac)],
        out_specs=_rows(ts, c + ac),
        out_shape=jax.ShapeDtypeStruct((s, c + ac), BF16),
        compiler_params=_params(("parallel",)),
    )(u1, lng, lnb, gcon, attn, gattn)


def _mix_bwd(dmixin, u1, lng, lnb, gcon, attn, gattn):
    s, c = u1.shape
    ac = attn.shape[1]
    ts = _pick(s, (256, 128))

    def body(d_ref, u_ref, lg_ref, lb_ref, gc_ref, a_ref, ga_ref,
             du_ref, da_ref, dlg_ref, dlb_ref, dgc_ref, dga_ref):
        _acc_init(pl.program_id(0), dlg_ref, dlb_ref, dgc_ref, dga_ref)
        u = u_ref[...]
        t2 = _ln(u, lg_ref[...], lb_ref[...])
        dt3, dgc = _rms_bwd(_silu(t2), gc_ref[...], d_ref[:, 0:c])
        du, dlg, dlb = _ln_bwd(u, lg_ref[...], dt3 * _silu_grad(t2))
        du_ref[...] = du
        dlg_ref[...] += dlg
        dlb_ref[...] += dlb
        dgc_ref[...] += dgc
        da, dga = _rms_bwd(a_ref[...], ga_ref[...], d_ref[:, c:c + ac])
        da_ref[...] = da
        dga_ref[...] += dga

    return pl.pallas_call(
        body, name="mix_bwd", grid=(s // ts,),
        in_specs=[_rows(ts, c + ac), _rows(ts, c), _vec(c), _vec(c), _vec(c), _rows(ts, ac), _vec(ac)],
        out_specs=[_rows(ts, c), _rows(ts, ac), _vec(c), _vec(c), _vec(c), _vec(ac)],
        out_shape=[jax.ShapeDtypeStruct((s, c), F32), jax.ShapeDtypeStruct((s, ac), F32),
                   jax.ShapeDtypeStruct((1, c), F32), jax.ShapeDtypeStruct((1, c), F32),
                   jax.ShapeDtypeStruct((1, c), F32), jax.ShapeDtypeStruct((1, ac), F32)],
        compiler_params=_params(("arbitrary",)),
    )(dmixin, u1, lng, lnb, gcon, attn, gattn)


def _post_mix_fwd(x, mix, gpost, gpre):
    s, d = x.shape
    ts = _pick(s, (256, 128))

    def body(x_ref, m_ref, gp_ref, gf_ref, x1_ref, hf_ref):
        x1 = x_ref[...] + _rms(m_ref[...], gp_ref[...])
        x1_ref[...] = x1
        hf_ref[...] = _rms(x1, gf_ref[...]).astype(BF16)

    return pl.pallas_call(
        body, name="post_mix_fwd", grid=(s // ts,),
        in_specs=[_rows(ts, d), _rows(ts, d), _vec(d), _vec(d)],
        out_specs=[_rows(ts, d), _rows(ts, d)],
        out_shape=[jax.ShapeDtypeStruct((s, d), F32), jax.ShapeDtypeStruct((s, d), BF16)],
        compiler_params=_params(("parallel",)),
    )(x, mix, gpost, gpre)


def _post_mix_bwd(dy, dhf, x1, gpre, mix, gpost):
    s, d = x1.shape
    ts = _pick(s, (256, 128))

    def body(dy_ref, dh_ref, x1_ref, gf_ref, m_ref, gp_ref, dx1_ref, dm_ref, dgf_ref, dgp_ref):
        _acc_init(pl.program_id(0), dgf_ref, dgp_ref)
        dxa, dgf = _rms_bwd(x1_ref[...], gf_ref[...], dh_ref[...])
        dx1 = dy_ref[...] + dxa
        dx1_ref[...] = dx1
        dgf_ref[...] += dgf
        dm, dgp = _rms_bwd(m_ref[...], gp_ref[...], dx1)
        dm_ref[...] = dm.astype(BF16)
        dgp_ref[...] += dgp

    return pl.pallas_call(
        body, name="post_mix_bwd", grid=(s // ts,),
        in_specs=[_rows(ts, d), _rows(ts, d), _rows(ts, d), _vec(d), _rows(ts, d), _vec(d)],
        out_specs=[_rows(ts, d), _rows(ts, d), _vec(d), _vec(d)],
        out_shape=[jax.ShapeDtypeStruct((s, d), F32), jax.ShapeDtypeStruct((s, d), BF16),
                   jax.ShapeDtypeStruct((1, d), F32), jax.ShapeDtypeStruct((1, d), F32)],
        compiler_params=_params(("arbitrary",)),
    )(dy, dhf, x1, gpre, mix, gpost)


def _ffn_up(hf, wg, wu, slabs, name, earlier=None):
    s, d = hf.shape
    nsh, fs, _ = wg.shape
    tm = _pick(s, (1024, 512, 256, 128))
    nt = (((1,), (1,)), ((), ()))

    def body(slab_ref, h_ref, wg_ref, wu_ref, *rest):
        dg_ref, du_ref, a_ref = rest[-3:]
        h = h_ref[...]
        g = lax.dot_general(h, wg_ref[...], nt, preferred_element_type=F32)
        u = lax.dot_general(h, wu_ref[...], nt, preferred_element_type=F32)
        sg = _sigmoid(g)
        silu = g * sg
        dg_ref[...] = u * (sg * (1.0 + g * (1.0 - sg)))
        du_ref[...] = silu
        a_ref[...] = (silu * u).astype(BF16)

    w_spec = pl.BlockSpec((None, fs, d), lambda i, j, slab_ref: (slab_ref[j], 0, 0))
    o_spec = pl.BlockSpec((None, tm, fs), lambda i, j, slab_ref: (slab_ref[j], i, 0))
    earlier = [] if earlier is None else list(earlier)
    return pl.pallas_call(
        body, name=name,
        grid_spec=pltpu.PrefetchScalarGridSpec(
            num_scalar_prefetch=1, grid=(s // tm, slabs.shape[0]),
            in_specs=[pl.BlockSpec((tm, d), lambda i, j, slab_ref: (i, 0)), w_spec, w_spec]
            + [pl.BlockSpec(memory_space=pl.ANY)] * len(earlier),
            out_specs=[o_spec] * 3),
        out_shape=[jax.ShapeDtypeStruct((nsh, s, fs), F32)] * 2 + [jax.ShapeDtypeStruct((nsh, s, fs), BF16)],
        input_output_aliases={4 + k: k for k in range(len(earlier))},
        compiler_params=_params(("parallel", "parallel")),
    )(slabs, hf, wg, wu, *earlier)


def _ffn_down(acts, ws, name):
    n = len(acts)
    nsh, s, fs = acts[0].shape
    d = ws[0].shape[2]
    tm = _pick(s, (512, 256, 128))
    tn = _pick(d, (256, 128))
    ni = s // tm

    def body(*refs):
        a_hbm, w_refs, o_ref = refs[:n], refs[n:2 * n], refs[2 * n]
        bufs, sems = refs[2 * n + 1:3 * n + 1], refs[3 * n + 1]
        i, j = pl.program_id(0), pl.program_id(1)

        def fetch(k, tile, slot):
            row = tile * tm if isinstance(tile, int) else pl.multiple_of(tile * tm, tm)
            return pltpu.make_async_copy(a_hbm[k].at[:, pl.ds(row, tm), :], bufs[k].at[slot], sems.at[k, slot])

        @pl.when(j == 0)
        def _():
            @pl.when(i == 0)
            def _():
                for k in range(n):
                    fetch(k, 0, 0).start()

            @pl.when(i + 1 < ni)
            def _():
                for k in range(n):
                    fetch(k, i + 1, (i + 1) % 2).start()

            for k in range(n):
                fetch(k, i, i % 2).wait()

        acc = None
        for k in range(n):
            for sl in range(nsh):
                part = jnp.dot(bufs[k][i % 2, sl], w_refs[k][sl], preferred_element_type=F32)
                acc = part if acc is None else acc + part
        o_ref[...] = acc

    return pl.pallas_call(
        body, name=name, grid=(ni, d // tn),
        in_specs=[pl.BlockSpec(memory_space=pl.ANY)] * n
        + [pl.BlockSpec((nsh, fs, tn), lambda i, j: (0, 0, j))] * n,
        out_specs=pl.BlockSpec((tm, tn), lambda i, j: (i, j)),
        out_shape=jax.ShapeDtypeStruct((s, d), F32),
        scratch_shapes=[pltpu.VMEM((2, nsh, tm, fs), acts[0].dtype)] * n + [pltpu.SemaphoreType.DMA((n, 2))],
        compiler_params=_params(("arbitrary", "arbitrary")),
    )(*acts, *ws)


def _ffn_down_bwd(dff, wd, act_dgate, act_dup, behind):
    s, d = dff.shape
    nsh, fs, _ = wd.shape
    tm = _pick(s, (1024, 512, 256, 128))
    nt = (((1,), (1,)), ((), ()))

    def body(d_ref, w_ref, pg_ref, pu_ref, _, dg_ref, du_ref):
        dact = lax.dot_general(d_ref[...], w_ref[...], nt, preferred_element_type=F32)
        dg_ref[...] = (dact * pg_ref[...]).astype(BF16)
        du_ref[...] = (dact * pu_ref[...]).astype(BF16)

    h_spec = pl.BlockSpec((None, tm, fs), lambda i, j: (j, i, 0))
    return pl.pallas_call(
        body, name="ffn_down_bwd", grid=(s // tm, nsh),
        in_specs=[pl.BlockSpec((tm, d), lambda i, j: (i, 0)),
                  pl.BlockSpec((None, fs, d), lambda i, j: (j, 0, 0)), h_spec, h_spec,
                  pl.BlockSpec((8, LANES), lambda i, j: (0, 0))],
        out_specs=[h_spec] * 2,
        out_shape=[jax.ShapeDtypeStruct((nsh, s, fs), BF16)] * 2,
        compiler_params=_params(("parallel", "parallel")),
    )(dff, wd, act_dgate, act_dup, behind)


def _ffn_dw(hiddens, other, name):
    n = len(hiddens)
    nsh, s, fs = hiddens[0].shape
    d = other.shape[1]
    tn = (((0,), (0,)), ((), ()))

    def body(*refs):
        for a_ref, o_ref in zip(refs[:n], refs[n + 1:]):
            o_ref[...] = lax.dot_general(a_ref[...], refs[n][...], tn, preferred_element_type=F32).astype(BF16)

    return pl.pallas_call(
        body, name=name, grid=(nsh,),
        in_specs=[pl.BlockSpec((None, s, fs), lambda j: (j, 0, 0))] * n + [pl.BlockSpec((s, d), lambda j: (0, 0))],
        out_specs=[pl.BlockSpec((None, fs, d), lambda j: (j, 0, 0))] * n,
        out_shape=[jax.ShapeDtypeStruct((nsh, fs, d), BF16)] * n,
        compiler_params=_params(("parallel",)),
    )(*hiddens, other)


def _ffn_down_final(act, wd, x1, tgt, g):
    nsh, s, fs = act.shape
    d = x1.shape[1]
    ts = _pick(s, (256, 128))

    def body(a_ref, w_ref, x1_ref, t_ref, g_ref, loss_ref, dy_ref, dff_ref, dg_ref):
        _acc_init(pl.program_id(0), loss_ref, dg_ref)
        ff_v = jnp.dot(a_ref[0], w_ref[0], preferred_element_type=F32)
        for j in range(1, nsh):
            ff_v = ff_v + jnp.dot(a_ref[j], w_ref[j], preferred_element_type=F32)
        err = x1_ref[...] + _rms(ff_v, g_ref[...]) - t_ref[...]
        tok = jnp.mean(err * err, axis=-1, keepdims=True)
        loss_ref[...] += 0.5 * jnp.sum(tok, axis=0, keepdims=True)
        dy = err * (1.0 / d)
        dy_ref[...] = dy
        dff, dg = _rms_bwd(ff_v, g_ref[...], dy)
        dff_ref[...] = dff.astype(BF16)
        dg_ref[...] += dg

    return pl.pallas_call(
        body, name="ffn_down_final", grid=(s // ts,),
        in_specs=[pl.BlockSpec((nsh, ts, fs), lambda i: (0, i, 0)),
                  pl.BlockSpec((nsh, fs, d), lambda i: (0, 0, 0), pipeline_mode=pl.Buffered(1)),
                  _rows(ts, d), _rows(ts, d), _vec(d)],
        out_specs=[_vec(LANES), _rows(ts, d), _rows(ts, d), _vec(d)],
        out_shape=[jax.ShapeDtypeStruct((1, LANES), F32), jax.ShapeDtypeStruct((s, d), F32),
                   jax.ShapeDtypeStruct((s, d), BF16), jax.ShapeDtypeStruct((1, d), F32)],
        compiler_params=_params(("arbitrary",)),
    )(act, wd, x1, tgt, g)


def _pre_bwd(dx1, dh, x, g):
    s, d = x.shape
    ts = _pick(s, (256, 128))

    def body(dx1_ref, dh_ref, x_ref, g_ref, dx_ref, dg_ref):
        _acc_init(pl.program_id(0), dg_ref)
        dxa, dg = _rms_bwd(x_ref[...], g_ref[...], dh_ref[...])
        dx_ref[...] = dx1_ref[...] + dxa
        dg_ref[...] += dg

    return pl.pallas_call(
        body, name="pre_bwd", grid=(s // ts,),
        in_specs=[_rows(ts, d), _rows(ts, d), _rows(ts, d), _vec(d)],
        out_specs=[_rows(ts, d), _vec(d)],
        out_shape=[jax.ShapeDtypeStruct((s, d), F32), jax.ShapeDtypeStruct((1, d), F32)],
        compiler_params=_params(("arbitrary",)),
    )(dx1, dh, x, g)


def _local_step(x, pos, tgt, vecs, in_weights_fn, mix_weights_fn, up_weights_fn, down_weights_fn, grads_fn):
    c = vecs["conv_b"].shape[1]
    ql = vecs["q_norm"].shape[1]
    kvl = vecs["kv_norm"].shape[1]
    half = jnp.arange(0, QK_ROPE, 2, dtype=F32)
    freq = ROPE_THETA ** (-half / QK_ROPE)
    inv_freq = jnp.concatenate([freq, freq, jnp.zeros((LANES - QK_ROPE,), F32)])[None, :]
    tabs = _rope_tables(pos, inv_freq)

    h = _pre_fwd(x, vecs["pre_mix_norm"])
    w_in_t, zero = in_weights_fn(h)
    z = _mm(h, w_in_t, "nt", "mm_z")
    w_uq_t, w_ukv, conv_w, w_out = mix_weights_fn(z)
    u0, qn, kvn, kpe = _split_fwd(z, vecs["q_norm"] + zero, vecs["kv_norm"], tabs, c, ql, kvl)
    u1 = _conv_fwd(u0, conv_w, vecs["conv_b"])
    q = _q_rope(_mm(qn, w_uq_t, "nt", "mm_q"), tabs)
    kv = _mm(kvn, w_ukv, "nn", "mm_kv")
    attn, lse = _attn_fwd(q, kv, kpe)
    mixin = _mix_fwd(u1, vecs["conv_ln_g"], vecs["conv_ln_b"], vecs["conv_out_norm"], attn, vecs["attn_out_norm"])
    mix = _mm(mixin, w_out, "nn", "mm_mix")
    x1, hf = _post_mix_fwd(x, mix, vecs["post_mix_norm"], vecs["pre_ffn_norm"])
    w_gate, w_up, slabs, rest_fn = up_weights_fn(hf)
    part = _ffn_up(hf, w_gate, w_up, slabs, "ffn_up_first")
    w_gate, w_up, slabs = rest_fn(part[2])
    act_dgate, act_dup, act = _ffn_up(hf, w_gate, w_up, slabs, "ffn_up_rest", earlier=part)
    w_down = down_weights_fn(act)
    loss, dy, dff, d_post_ffn = _ffn_down_final(act, w_down, x1, tgt, vecs["post_ffn_norm"])

    g = {"post_ffn_norm": d_post_ffn}
    zero = grads_fn("down", ("w_down",), _ffn_dw([act], dff, "ffn_dw_down"), dff)
    dgate, dup = _ffn_down_bwd(dff, w_down, act_dgate, act_dup, jnp.zeros((8, LANES), F32) + zero)
    zero = zero + grads_fn("up", ("w_gate", "w_up"), _ffn_dw([dgate, dup], hf, "ffn_dw_up"), dgate)
    dhf = _ffn_down([dgate, dup], [w_gate, w_up], "ffn_dhf")
    dx1, dmix, g["pre_ffn_norm"], g["post_mix_norm"] = _post_mix_bwd(
        dy, dhf, x1, vecs["pre_ffn_norm"] + zero, mix, vecs["post_mix_norm"])
    dmixin = _mm(dmix, w_out, "nt", "mm_dmixin")
    dw_out = _mm(mixin, dmix, "tn", "mm_dw_out", BF16)
    zero = grads_fn("out", ("w_out",), [dw_out], dmix)
    du1, dattn, g["conv_ln_g"], g["conv_ln_b"], g["conv_out_norm"], g["attn_out_norm"] = _mix_bwd(
        dmixin, u1, vecs["conv_ln_g"] + zero, vecs["conv_ln_b"], vecs["conv_out_norm"], attn, vecs["attn_out_norm"])
    du0, dw_conv, g["conv_b"] = _conv_bwd(du1, u0, conv_w)
    dqpre, dkv, dkpe_h = _attn_bwd(q, kv, kpe, attn, dattn, lse, tabs)
    dqn = _mm(dqpre, w_uq_t, "nn", "mm_dqn")
    dw_uq = _mm(dqpre, qn, "tn", "mm_dw_uq", BF16)
    dkvn = _mm(dkv, w_ukv, "nt", "mm_dkvn")
    dw_ukv = _mm(kvn, dkv, "tn", "mm_dw_ukv", BF16)
    zero = grads_fn("", (), [], dkvn)
    dz, g["q_norm"], g["kv_norm"] = _split_bwd(du0, z, dqn, dkvn, dkpe_h, vecs["q_norm"] + zero, vecs["kv_norm"], tabs,
                                               c, ql, kvl)
    dw_in = _mm(dz, h, "tn", "mm_dw_in", BF16)
    zero = grads_fn("in", ("w_in", "w_uq", "w_ukv", "conv_w"), [dw_in, dw_uq, dw_ukv, dw_conv], dz)
    dh = _mm(dz, w_in_t, "nn", "mm_dh")
    zero = zero + grads_fn("", (), [], dh)
    grad_x, g["pre_mix_norm"] = _pre_bwd(dx1, dh, x, vecs["pre_mix_norm"] + zero)
    return loss, grad_x, g


def _my_index():
    return 4 * lax.axis_index("x") + 2 * lax.axis_index("y") + lax.axis_index("c")


def _coords(idx):
    return ((idx >> 2) & 1, (idx >> 1) & 1, idx & 1)


def _place():
    x, y, c = lax.axis_index("x"), lax.axis_index("y"), lax.axis_index("c")
    return (x, y, c), (x, y, 1 - c), [(1 - x, y), (x, 1 - y), (1 - x, 1 - y)]


def _small_copies(src, land, send_sem, recv_sem):
    me = _my_index()
    return [pltpu.make_async_remote_copy(src_ref=src, dst_ref=land.at[me], send_sem=send_sem, recv_sem=recv_sem,
                                         device_id=_coords(me ^ p), device_id_type=MESH)
            for p in range(1, N_DEV)]


HBM_SPEC = pl.BlockSpec(memory_space=pltpu.HBM)
SEM_SPEC = pl.BlockSpec(memory_space=pltpu.SEMAPHORE)
DATAFLOW = pltpu.SideEffectType.DATAFLOW_SIDE_EFFECTING


def _split_start(name, copies_of, srcs, lands, after):
    n = len(srcs)
    builders = list(copies_of) if isinstance(copies_of, (list, tuple)) else [copies_of] * n

    def body(*refs):
        outs = refs[2 * n + 1:]
        for k in range(n):
            for cp in builders[k](refs[k], refs[n + k], outs[k], outs[n + k]):
                cp.start()
        outs[-1][...] = jnp.zeros_like(outs[-1])

    hbm = lambda a: pltpu.HBM(a.shape, a.dtype)
    out = pl.pallas_call(
        body, name=name,
        in_specs=[HBM_SPEC] * (2 * n + 1),
        out_specs=[SEM_SPEC] * (2 * n) + [HBM_SPEC] * (2 * n) + [pl.BlockSpec(memory_space=pltpu.VMEM)],
        out_shape=[pltpu.SemaphoreType.DMA(())] * (2 * n) + [hbm(a) for a in srcs] + [hbm(a) for a in lands]
        + [jax.ShapeDtypeStruct((8, LANES), F32)],
        input_output_aliases={k: 2 * n + k for k in range(2 * n)},
        compiler_params=pltpu.CompilerParams(has_side_effects=DATAFLOW),
    )(*[pltpu.with_memory_space_constraint(a, pltpu.HBM) for a in list(srcs) + list(lands) + [after]])
    return (out[:n], out[n:2 * n], out[2 * n:3 * n], out[3 * n:4 * n]), out[-1][0, 0]


def _split_wait(name, n_copies, started, after):
    send_sems, recv_sems, srcs, lands = started
    n = len(srcs)

    def body(*refs):
        for k in range(n):
            slots = refs[n + k].at[pl.ds(0, n_copies)]
            all_copies = pltpu.make_async_remote_copy(
                src_ref=slots, dst_ref=slots, send_sem=refs[2 * n + k], recv_sem=refs[3 * n + k],
                device_id=_place()[0], device_id_type=MESH)
            all_copies.wait_send()
            all_copies.wait_recv()

    hbm = lambda a: pltpu.HBM(a.shape, a.dtype)
    out = pl.pallas_call(
        body, name=name,
        in_specs=[HBM_SPEC] * (2 * n) + [SEM_SPEC] * (2 * n) + [HBM_SPEC],
        out_specs=[HBM_SPEC] * (2 * n),
        out_shape=[hbm(a) for a in srcs] + [hbm(a) for a in lands],
        input_output_aliases={k: k for k in range(2 * n)},
        compiler_params=pltpu.CompilerParams(has_side_effects=DATAFLOW),
    )(*srcs, *lands, *send_sems, *recv_sems, pltpu.with_memory_space_constraint(after, pltpu.HBM))
    return out[:n], out[n:]


def _slot(chip, core):
    return 4 * chip[0] + 2 * chip[1] + core


def _gather_copies(src, land, send_sem, recv_sem):
    (x, y, c), sib, chips = _place()
    return [pltpu.make_async_remote_copy(src_ref=src, dst_ref=land.at[_slot((x, y), c)], send_sem=send_sem,
                                         recv_sem=recv_sem, device_id=to, device_id_type=MESH)
            for to in [sib] + [(*chip, c) for chip in chips]]


def _pass_on_copies(src, land, send_sem, recv_sem):
    (x, y, c), sib, chips = _place()
    return [pltpu.make_async_remote_copy(src_ref=land.at[_slot(chip, c)], dst_ref=land.at[_slot(chip, c)],
                                         send_sem=send_sem, recv_sem=recv_sem, device_id=sib, device_id_type=MESH)
            for chip in chips]


def _gather_pass_on(lands, name):
    n = len(lands)

    def body(*refs):
        ins, outs = refs[:n], refs[n:2 * n]
        send_sems, recv_sems = refs[2 * n:]
        (x, y, c), sib, chips = _place()
        sends = []
        for k in range(n):
            for j, chip in enumerate(chips):
                sends.append(pltpu.make_async_remote_copy(
                    src_ref=ins[k].at[_slot(chip, c)], dst_ref=outs[k].at[_slot(chip, c)],
                    send_sem=send_sems.at[k, j], recv_sem=recv_sems.at[k, j], device_id=sib, device_id_type=MESH))
        for cp in sends:
            cp.start()
        for cp in sends:
            cp.wait_recv()
        for cp in sends:
            cp.wait_send()

    any_spec = pl.BlockSpec(memory_space=pl.ANY)
    return pl.pallas_call(
        body, name=name,
        in_specs=[any_spec] * n, out_specs=[any_spec] * n,
        out_shape=[jax.ShapeDtypeStruct(a.shape, a.dtype) for a in lands],
        input_output_aliases={k: k for k in range(n)},
        scratch_shapes=[pltpu.SemaphoreType.DMA((n, 3))] * 2,
        compiler_params=pltpu.CompilerParams(has_side_effects=True),
    )(*lands)


def _chip_copies(src, land, send_sem, recv_sem):
    (x, y, c), _, chips = _place()
    return [pltpu.make_async_remote_copy(src_ref=src.at[j], dst_ref=land.at[j], send_sem=send_sem,
                                         recv_sem=recv_sem, device_id=(*chip, c), device_id_type=MESH)
            for j, chip in enumerate(chips)]


ROW_TILE_BYTES = 14 * 1024 * 1024


def _stream_tile(r, c, bytes_per_elem):
    if r * c * bytes_per_elem <= ROW_TILE_BYTES:
        return r, c
    rows = [t for t in range(16, r, 16) if r % t == 0 and t * c * bytes_per_elem <= ROW_TILE_BYTES]
    if rows:
        return max(rows), c
    cols = [t for t in range(LANES, c, LANES) if c % t == 0 and r * t * bytes_per_elem <= ROW_TILE_BYTES]
    return r, max(cols)


def _sibling_copies(src, land, send_sem, recv_sem):
    (x, y, c), sib, _ = _place()
    return [pltpu.make_async_remote_copy(src_ref=src.at[2 * q + 1 - c], dst_ref=land.at[q], send_sem=send_sem,
                                         recv_sem=recv_sem, device_id=sib, device_id_type=MESH)
            for q in range(4)]


def _pair_sum(slots, blocks, theirs, name):
    _, r, c = theirs.shape
    tr, tc = _stream_tile(r, c, 3 * theirs.dtype.itemsize)

    def body(slot_ref, a_ref, b_ref, o_ref):
        o_ref[...] = (a_ref[...].astype(F32) + b_ref[...].astype(F32)).astype(o_ref.dtype)

    return pl.pallas_call(
        body, name=name,
        grid_spec=pltpu.PrefetchScalarGridSpec(
            num_scalar_prefetch=1, grid=(3, r // tr, c // tc),
            in_specs=[pl.BlockSpec((1, tr, tc), lambda j, a, b, slot_ref: (slot_ref[j], a, b)),
                      pl.BlockSpec((1, tr, tc), lambda j, a, b, slot_ref: (slot_ref[3 + j], a, b))],
            out_specs=pl.BlockSpec((1, tr, tc), lambda j, a, b, slot_ref: (j, a, b))),
        out_shape=jax.ShapeDtypeStruct((3, r, c), theirs.dtype),
        compiler_params=_params(("parallel", "parallel", "parallel")),
    )(slots, blocks, theirs)


def _adamw_small(parts, ws, ms, vs):
    n = len(ws)
    c1 = 1.0 - ADAM_B1
    c2 = 1.0 - ADAM_B2
    bc1 = 1.0 - ADAM_B1 ** ADAM_STEP
    bc2 = 1.0 - ADAM_B2 ** ADAM_STEP

    def body(*refs):
        p_ref, outs = refs[0], refs[1 + 3 * n:]
        off = 0
        for k in range(n):
            w_ref, m_ref, v_ref = refs[1 + k], refs[1 + n + k], refs[1 + 2 * n + k]
            width = w_ref.shape[1]
            g = p_ref[0, :, off:off + width]
            for j in range(1, N_DEV):
                g = g + p_ref[j, :, off:off + width]
            nm = ADAM_B1 * m_ref[...] + c1 * g
            nv = ADAM_B2 * v_ref[...] + c2 * (g * g)
            outs[k][...] = g
            outs[n + k][...] = -ADAM_LR * ((nm / bc1) / (jnp.sqrt(nv / bc2) + ADAM_EPS) + ADAM_WD * w_ref[...])
            outs[2 * n + k][...] = nm
            outs[3 * n + k][...] = nv
            off += width
        total = p_ref[0, :, off:off + LANES]
        for j in range(1, N_DEV):
            total = total + p_ref[j, :, off:off + LANES]
        outs[4 * n][...] = total

    out = pl.pallas_call(
        body, name="adamw_small",
        out_shape=[jax.ShapeDtypeStruct(a.shape, F32) for a in ws] * 4 + [jax.ShapeDtypeStruct((1, LANES), F32)],
        compiler_params=pltpu.CompilerParams(vmem_limit_bytes=VMEM_LIMIT),
    )(parts, *ws, *ms, *vs)
    return out[:n], out[n:2 * n], out[2 * n:3 * n], out[3 * n:4 * n], out[4 * n]


def _reduce_adamw(parts, w, m, v, name, blocks, theirs, slots):
    r, c = w.shape
    n_parts = parts.shape[0]
    tr, tc = _stream_tile(r, c, (n_parts + 2) * parts.dtype.itemsize + 7 * 4)
    c1 = 1.0 - ADAM_B1
    c2 = 1.0 - ADAM_B2
    bc1 = 1.0 - ADAM_B1 ** ADAM_STEP
    bc2 = 1.0 - ADAM_B2 ** ADAM_STEP

    def body(_, b_ref, t_ref, p_ref, w_ref, m_ref, v_ref, g_ref, d_ref, nm_ref, nv_ref):
        g = b_ref[0].astype(F32) + t_ref[0].astype(F32)
        for j in range(n_parts):
            g = g + p_ref[j].astype(F32)
        nm = ADAM_B1 * m_ref[...] + c1 * g
        nv = ADAM_B2 * v_ref[...] + c2 * (g * g)
        g_ref[...] = g
        nm_ref[...] = nm
        nv_ref[...] = nv
        d_ref[...] = -ADAM_LR * ((nm / bc1) / (jnp.sqrt(nv / bc2) + ADAM_EPS) + ADAM_WD * w_ref[...])

    out = jax.ShapeDtypeStruct((r, c), F32)
    grid = (r // tr, c // tc)
    blk = pl.BlockSpec((tr, tc), lambda i, j, slot_ref: (i, j))
    return pl.pallas_call(
        body, name=name,
        grid_spec=pltpu.PrefetchScalarGridSpec(
            num_scalar_prefetch=1, grid=grid,
            in_specs=[pl.BlockSpec((1, tr, tc), lambda i, j, slot_ref: (slot_ref[0], i, j)),
                      pl.BlockSpec((1, tr, tc), lambda i, j, slot_ref: (slot_ref[1], i, j)),
                      pl.BlockSpec((n_parts, tr, tc), lambda i, j, slot_ref: (0, i, j)), blk, blk, blk],
            out_specs=[blk] * 4),
        out_shape=[out] * 4,
        compiler_params=_params(("parallel", "parallel")),
    )(slots, blocks, theirs, parts, w, m, v)


_MIX = ("w_in", "w_uq", "w_ukv", "conv_w", "w_out")
_FFN = ("w_gate", "w_up", "w_down")
_BIG = _MIX + _FFN
_TRANSPOSED = ("w_in", "w_uq", "w_gate", "w_up")
_SMALL = ("pre_mix_norm", "q_norm", "kv_norm", "conv_b", "conv_ln_g", "conv_ln_b", "conv_out_norm",
          "attn_out_norm", "post_mix_norm", "pre_ffn_norm", "post_ffn_norm")
_ORDER = ("pre_mix_norm", "w_in", "q_norm", "w_uq", "kv_norm", "w_ukv", "conv_w", "conv_b", "conv_ln_g",
          "conv_ln_b", "conv_out_norm", "attn_out_norm", "w_out", "post_mix_norm", "pre_ffn_norm", "w_gate",
          "w_up", "w_down", "post_ffn_norm")


def _cols_from_shards(g):
    return jnp.transpose(g, (1, 0, 2)).reshape(g.shape[1], N_DEV * g.shape[2])


def _cols_to_shards(w):
    k, n8 = w.shape
    return jnp.transpose(w.reshape(k, N_DEV, n8 // N_DEV), (1, 0, 2))


def _step(x, positions, loss_target, w, m, v):
    s, d = x.shape[1], x.shape[2]
    x2, tgt = x[0], loss_target[0]
    pos = positions.reshape(s, 1)
    vecs = {n: w[n] for n in _SMALL}
    core = lax.axis_index("c").astype(jnp.int32)
    my_chip = (2 * lax.axis_index("x") + lax.axis_index("y")).astype(jnp.int32)
    other_chips = [my_chip ^ 2, my_chip ^ 1, my_chip ^ 3]
    pair_slots = jnp.stack([2 * q + core for q in other_chips] + other_chips)
    own_slots = jnp.stack([2 * my_chip + core, my_chip])
    own = {}
    n_in_cols = N_DEV * w["w_in"].shape[2]
    gathers, scatters, to_sibling = {}, {}, []

    def shard(t, n):
        return t[n][0].T if n in _TRANSPOSED else t[n][0]

    def gather_start(names, tag, after, zero=0.0):
        srcs = [w[n][0] if n == "conv_w" else (shard(w, n) + zero).astype(BF16) for n in names]
        lands = [lax.empty((N_DEV,) + a.shape, a.dtype) for a in srcs]
        gathers[tag], zero = _split_start("gather_" + tag + "_start", _gather_copies, srcs, lands, after)
        return zero

    def gather_finish(names, tag, after):
        srcs, lands = _split_wait("gather_" + tag + "_wait", 4, gathers[tag], after)
        lands = _gather_pass_on(lands, "gather_" + tag + "_pass_on")
        me = _my_index()
        return {n: lax.dynamic_update_slice(g, a[None], (me,) + (0,) * a.ndim) for n, g, a in zip(names, lands, srcs)}

    def in_weights_fn(h):
        w_in_g = gather_finish(("w_in",), "in", h)["w_in"]
        return jnp.pad(w_in_g.reshape(-1, d), ((0, LANES - QK_ROPE), (0, 0))), 0.0

    def mix_weights_fn(z):
        gath = gather_finish(_MIX[1:], "mix", z)
        w_uq_t = jnp.pad(gath["w_uq"], ((0, 0), (0, HEAD_PAD - QK_HEAD), (0, 0))).reshape(N_HEADS * HEAD_PAD, -1)
        return (w_uq_t, _cols_from_shards(gath["w_ukv"]), _cols_from_shards(gath["conv_w"]),
                gath["w_out"].reshape(-1, d))

    def up_weights_fn(hf):
        srcs, lands = _split_wait("gather_up_wait", 4, gathers["up"], hf)
        me = _my_index()
        lands = [lax.dynamic_update_slice(g, a[None], (me, 0, 0)) for g, a in zip(lands, srcs)]
        passing, _ = _split_start("gather_up_pass_on_start", _pass_on_copies, srcs, lands, hf)
        here = jnp.stack([me, me ^ 1] + [2 * q + core for q in other_chips])
        late = jnp.stack([2 * q + 1 - core for q in other_chips])

        def rest_fn(after):
            _, done = _split_wait("gather_up_pass_on_wait", 3, passing, after)
            return done[0], done[1], late

        return passing[3][0], passing[3][1], here, rest_fn

    def down_weights_fn(act):
        return gather_finish(("w_down",), "down", act)["w_down"]

    def pairs_of_pending(after):
        names, tag, sent = to_sibling.pop()
        blocks, theirs = _split_wait("to_sibling_" + tag + "_wait", 4, sent, after)
        own.update(zip(names, zip(blocks, theirs)))
        return tag, [_pair_sum(pair_slots, b, t, "pair_sum_" + n) for n, b, t in zip(names, blocks, theirs)]

    to_blocks = {
        "w_in": lambda a: a[:n_in_cols].reshape(N_DEV, -1, d),
        "w_uq": lambda a: a.reshape(N_HEADS, HEAD_PAD, -1)[:, :QK_HEAD],
        "w_ukv": _cols_to_shards, "conv_w": _cols_to_shards,
        "w_out": lambda a: a.reshape(N_DEV, -1, d),
    }

    def grads_fn(tag, names, grads, after):
        pending, pairs = pairs_of_pending(after) if to_sibling else ("", [])
        blocks = [to_blocks.get(n, lambda a: a)(a) for n, a in zip(names, grads)]
        if not pairs and not blocks:
            return 0.0
        lands = [lax.empty((3,) + p.shape[1:], p.dtype) for p in pairs] + \
                [lax.empty((4,) + b.shape[1:], b.dtype) for b in blocks]
        plans = [_chip_copies] * len(pairs) + [_sibling_copies] * len(blocks)
        started, zero = _split_start("start_" + pending + "_" + tag, plans, pairs + blocks, lands, after)
        k = len(pairs)
        if pairs:
            scatters[pending] = tuple(part[:k] for part in started)
        if blocks:
            to_sibling.append((names, tag, tuple(part[k:] for part in started)))
        return zero

    zero = gather_start(("w_in",), "in", x2)
    zero = gather_start(_MIX[1:], "mix", x2, zero)
    zero = gather_start(("w_gate", "w_up"), "up", x2, zero)
    vecs["pre_mix_norm"] = vecs["pre_mix_norm"] + gather_start(("w_down",), "down", x2, zero)
    loss, grad_x, g = _local_step(x2, pos, tgt, vecs, in_weights_fn, mix_weights_fn, up_weights_fn, down_weights_fn,
                                  grads_fn)

    small = jnp.concatenate([g[n] for n in _SMALL] + [loss], axis=1)
    small_started, zero = _split_start("gather_small_start", _small_copies, [small],
                                       [lax.empty((N_DEV,) + small.shape, F32)], grad_x)

    res = {}
    after = grad_x
    own_slots = own_slots + zero.astype(jnp.int32)
    for tag, names in (("down", ("w_down",)), ("up", ("w_gate", "w_up")), ("out", ("w_out",)),
                       ("in", ("w_in", "w_uq", "w_ukv", "conv_w"))):
        _, recv = _split_wait("scatter_" + tag + "_wait", 3, scatters[tag], after)
        for n, parts in zip(names, recv):
            res[n] = _reduce_adamw(parts, shard(w, n), shard(m, n), shard(v, n), "adamw_" + n, *own[n], own_slots)
            after = res[n][1]
            res[n] = [(t.T if n in _TRANSPOSED else t)[None] for t in res[n]]
    (small,), (small_all,) = _split_wait("gather_small_wait", N_DEV - 1, small_started, after)
    small_all = lax.dynamic_update_slice(small_all, small[None], (_my_index(), 0, 0))
    *small_res, total = _adamw_small(small_all, *[[t[n] for n in _SMALL] for t in (w, m, v)])
    for k, n in enumerate(_SMALL):
        res[n] = [part[k] for part in small_res]

    outs = [total[0, 0], grad_x[None]]
    for part in range(4):
        outs.extend(res[n][part] for n in _ORDER)
    return tuple(outs)


def kernel(x, positions, pre_mix_norm, w_in, q_norm, w_uq, kv_norm, w_ukv, conv_w, conv_b, conv_ln_g, conv_ln_b, conv_out_norm, attn_out_norm, w_out, post_mix_norm, pre_ffn_norm, w_gate, w_up, w_down, post_ffn_norm, loss_target, m_pre_mix_norm, m_w_in, m_q_norm, m_w_uq, m_kv_norm, m_w_ukv, m_conv_w, m_conv_b, m_conv_ln_g, m_conv_ln_b, m_conv_out_norm, m_attn_out_norm, m_w_out, m_post_mix_norm, m_pre_ffn_norm, m_w_gate, m_w_up, m_w_down, m_post_ffn_norm, v_pre_mix_norm, v_w_in, v_q_norm, v_w_uq, v_kv_norm, v_w_ukv, v_conv_w, v_conv_b, v_conv_ln_g, v_conv_ln_b, v_conv_out_norm, v_attn_out_norm, v_w_out, v_post_mix_norm, v_pre_ffn_norm, v_w_gate, v_w_up, v_w_down, v_post_ffn_norm):
    w = dict(zip(_ORDER, (pre_mix_norm, w_in, q_norm, w_uq, kv_norm, w_ukv, conv_w, conv_b, conv_ln_g, conv_ln_b,
                          conv_out_norm, attn_out_norm, w_out, post_mix_norm, pre_ffn_norm, w_gate, w_up, w_down,
                          post_ffn_norm)))
    m = dict(zip(_ORDER, (m_pre_mix_norm, m_w_in, m_q_norm, m_w_uq, m_kv_norm, m_w_ukv, m_conv_w, m_conv_b,
                          m_conv_ln_g, m_conv_ln_b, m_conv_out_norm, m_attn_out_norm, m_w_out, m_post_mix_norm,
                          m_pre_ffn_norm, m_w_gate, m_w_up, m_w_down, m_post_ffn_norm)))
    v = dict(zip(_ORDER, (v_pre_mix_norm, v_w_in, v_q_norm, v_w_uq, v_kv_norm, v_w_ukv, v_conv_w, v_conv_b,
                          v_conv_ln_g, v_conv_ln_b, v_conv_out_norm, v_attn_out_norm, v_w_out, v_post_mix_norm,
                          v_pre_ffn_norm, v_w_gate, v_w_up, v_w_down, v_post_ffn_norm)))
    return _step(x, positions, loss_target, w, m, v)
```
